```python
import jax, jax.numpy as jnp
from jax import lax

D_MODEL = 1024
BATCH = 8
SEQ = 4096
DEPTH = 1

N_ATTN_HEADS = 8
HEAD_DIM = 64
ATTN_WIDTH = N_ATTN_HEADS * HEAD_DIM
N_CONV_GROUPS = 8
CONV_GROUP_DIM = 64
CONV_WIDTH = N_CONV_GROUPS * CONV_GROUP_DIM
CONV_KSIZE = 3
D_FF = 2816
Q_BLOCK = 128
N_SUBLAYERS = 3
N_MOD = 3
EPS = 1e-6
FFN_RES_WEIGHT = 0.5
MIX_IN_WIDTH = 3 * CONV_WIDTH + 3 * ATTN_WIDTH + 2 * D_MODEL

kernel_name = "hybrid_shortconv_stickbreaking_macaron_block"


def rmsnorm(x, g):
    xf = x.astype(jnp.float32)
    inv = lax.rsqrt(jnp.mean(xf * xf, axis=-1, keepdims=True) + EPS)
    return (xf * inv).astype(x.dtype) * g


def modulate(x, shift, scale):
    return x * (1 + scale[:, None, :]) + shift[:, None, :]


def swiglu(x, w_gu, w_down):
    g, u = jnp.split(x @ w_gu, 2, axis=-1)
    return (jax.nn.silu(g) * u) @ w_down


def short_conv(b_gate, c_gate, xin, conv_w):
    v = c_gate * xin
    S = v.shape[1]
    vp = jnp.pad(v, ((0, 0), (CONV_KSIZE - 1, 0), (0, 0)))
    y = conv_w[0] * vp[:, 0:S, :]
    for k in range(1, CONV_KSIZE):
        y = y + conv_w[k] * vp[:, k:k + S, :]
    return b_gate * y


def stick_breaking_attention(q, k, v):
    S = q.shape[2]
    scale = HEAD_DIM ** -0.5
    qf = q.astype(jnp.float32)
    kf = k.astype(jnp.float32)
    vf = v.astype(jnp.float32)
    outs = []
    for i in range(S // Q_BLOCK):
        start = i * Q_BLOCK
        end = start + Q_BLOCK
        q_blk = qf[:, :, start:end, :]
        k_c = kf[:, :, :end, :]
        v_c = vf[:, :, :end, :]
        z = jnp.einsum('bhqd,bhkd->bhqk', q_blk, k_c) * scale
        t_pos = start + jnp.arange(Q_BLOCK)
        s_pos = jnp.arange(end)
        valid = s_pos[None, :] < t_pos[:, None]
        sp = jnp.where(valid, jax.nn.softplus(z), 0.0)
        rem = lax.cumsum(sp, axis=3, reverse=True) - sp
        log_a = jax.nn.log_sigmoid(z) - rem
        a = jnp.where(valid, jnp.exp(log_a), 0.0)
        outs.append(jnp.einsum('bhqk,bhkd->bhqd', a, v_c))
    return jnp.concatenate(outs, axis=2).astype(q.dtype)


def mixer(u, w_mix_in, b_merge, conv_w, w_conv_out, w_attn_out, w_out):
    B, S, _ = u.shape
    proj = u @ w_mix_in
    idx = [CONV_WIDTH, 2 * CONV_WIDTH, 3 * CONV_WIDTH,
           3 * CONV_WIDTH + ATTN_WIDTH, 3 * CONV_WIDTH + 2 * ATTN_WIDTH,
           3 * CONV_WIDTH + 3 * ATTN_WIDTH, 3 * CONV_WIDTH + 3 * ATTN_WIDTH + D_MODEL]
    cb, cc, cx, q, k, v, ga, gb = jnp.split(proj, idx, axis=-1)
    ya = short_conv(cb, cc, cx, conv_w) @ w_conv_out
    def heads(t):
        return t.reshape(B, S, N_ATTN_HEADS, HEAD_DIM).transpose(0, 2, 1, 3)
    o = stick_breaking_attention(heads(q), heads(k), heads(v))
    o = o.transpose(0, 2, 1, 3).reshape(B, S, ATTN_WIDTH)
    yb = o @ w_attn_out
    merged = jax.nn.sigmoid(ga + b_merge[0]) * ya + jax.nn.sigmoid(gb + b_merge[1]) * yb
    return merged @ w_out


def _fwd_setup_inputs(seed: int = 0) -> dict:
    key = jax.random.key(seed)
    ks = jax.random.split(key, 24)
    f32 = jnp.float32
    L, D = DEPTH, D_MODEL

    def nrm(k, shape, s):
        return jax.random.normal(k, shape, f32) * s

    return {
        "x": nrm(ks[0], (BATCH, SEQ, D), 1.0),
        "c": nrm(ks[1], (BATCH, D), 1.0),
        "w_ada": nrm(ks[2], (L, D, N_SUBLAYERS * N_MOD * D), 0.5 * D ** -0.5),
        "b_ada": nrm(ks[3], (L, N_SUBLAYERS * N_MOD * D), 0.02),
        "norm1_g": 1.0 + nrm(ks[4], (L, D), 0.02),
        "ffn1_w_gu": nrm(ks[5], (L, D, 2 * D_FF), D ** -0.5),
        "ffn1_w_down": nrm(ks[6], (L, D_FF, D), D_FF ** -0.5),
        "norm2_g": 1.0 + nrm(ks[7], (L, D), 0.02),
        "w_mix_in": nrm(ks[8], (L, D, MIX_IN_WIDTH), D ** -0.5),
        "b_merge": nrm(ks[9], (L, 2, D), 0.02),
        "conv_w": nrm(ks[10], (L, CONV_KSIZE, CONV_WIDTH), CONV_KSIZE ** -0.5),
        "w_conv_out": nrm(ks[11], (L, CONV_WIDTH, D), CONV_WIDTH ** -0.5),
        "w_attn_out": nrm(ks[12], (L, ATTN_WIDTH, D), ATTN_WIDTH ** -0.5),
        "w_out": nrm(ks[13], (L, D, D), D ** -0.5),
        "norm3_g": 1.0 + nrm(ks[14], (L, D), 0.02),
        "ffn2_w_gu": nrm(ks[15], (L, D, 2 * D_FF), D ** -0.5),
        "ffn2_w_down": nrm(ks[16], (L, D_FF, D), D_FF ** -0.5),
        "final_g": 1.0 + nrm(ks[17], (D,), 0.02),
    }


def _fwd_reference(x, c, w_ada, b_ada, norm1_g, ffn1_w_gu, ffn1_w_down, norm2_g,
              w_mix_in, b_merge, conv_w, w_conv_out, w_attn_out, w_out,
              norm3_g, ffn2_w_gu, ffn2_w_down, final_g):
    B = x.shape[0]
    c_act = jax.nn.silu(c)
    h = x
    for l in range(DEPTH):
        mod = (c_act @ w_ada[l] + b_ada[l]).reshape(B, N_SUBLAYERS, N_MOD, D_MODEL)
        u = modulate(rmsnorm(h, norm1_g[l]), mod[:, 0, 0], mod[:, 0, 1])
        h = h + FFN_RES_WEIGHT * mod[:, 0, 2][:, None, :] * swiglu(u, ffn1_w_gu[l], ffn1_w_down[l])
        u = modulate(rmsnorm(h, norm2_g[l]), mod[:, 1, 0], mod[:, 1, 1])
        y = mixer(u, w_mix_in[l], b_merge[l], conv_w[l], w_conv_out[l], w_attn_out[l], w_out[l])
        h = h + mod[:, 1, 2][:, None, :] * y
        u = modulate(rmsnorm(h, norm3_g[l]), mod[:, 2, 0], mod[:, 2, 1])
        h = h + FFN_RES_WEIGHT * mod[:, 2, 2][:, None, :] * swiglu(u, ffn2_w_gu[l], ffn2_w_down[l])
    return rmsnorm(h, final_g)


import jax as _jax
import jax.numpy as _jnp

TWIN_FORMAT = 'train_step'
FWD_PARAMS = ['x', 'c', 'w_ada', 'b_ada', 'norm1_g', 'ffn1_w_gu', 'ffn1_w_down', 'norm2_g', 'w_mix_in', 'b_merge', 'conv_w', 'w_conv_out', 'w_attn_out', 'w_out', 'norm3_g', 'ffn2_w_gu', 'ffn2_w_down', 'final_g']
TWIN_WEIGHTS = ['w_ada', 'b_ada', 'norm1_g', 'ffn1_w_gu', 'ffn1_w_down', 'norm2_g', 'w_mix_in', 'b_merge', 'conv_w', 'w_conv_out', 'w_attn_out', 'w_out', 'norm3_g', 'ffn2_w_gu', 'ffn2_w_down', 'final_g']
TWIN_DIFF_INPUT = 'x'
TWIN_INPUTS = ['x', 'c', 'w_ada', 'b_ada', 'norm1_g', 'ffn1_w_gu', 'ffn1_w_down', 'norm2_g', 'w_mix_in', 'b_merge', 'conv_w', 'w_conv_out', 'w_attn_out', 'w_out', 'norm3_g', 'ffn2_w_gu', 'ffn2_w_down', 'final_g', 'loss_target', 'm_w_ada', 'm_b_ada', 'm_norm1_g', 'm_ffn1_w_gu', 'm_ffn1_w_down', 'm_norm2_g', 'm_w_mix_in', 'm_b_merge', 'm_conv_w', 'm_w_conv_out', 'm_w_attn_out', 'm_w_out', 'm_norm3_g', 'm_ffn2_w_gu', 'm_ffn2_w_down', 'm_final_g', 'v_w_ada', 'v_b_ada', 'v_norm1_g', 'v_ffn1_w_gu', 'v_ffn1_w_down', 'v_norm2_g', 'v_w_mix_in', 'v_b_merge', 'v_conv_w', 'v_w_conv_out', 'v_w_attn_out', 'v_w_out', 'v_norm3_g', 'v_ffn2_w_gu', 'v_ffn2_w_down', 'v_final_g']
TWIN_OUTPUTS = ['loss', 'grad_x', 'grad_w_ada', 'grad_b_ada', 'grad_norm1_g', 'grad_ffn1_w_gu', 'grad_ffn1_w_down', 'grad_norm2_g', 'grad_w_mix_in', 'grad_b_merge', 'grad_conv_w', 'grad_w_conv_out', 'grad_w_attn_out', 'grad_w_out', 'grad_norm3_g', 'grad_ffn2_w_gu', 'grad_ffn2_w_down', 'grad_final_g', 'delta_w_ada', 'delta_b_ada', 'delta_norm1_g', 'delta_ffn1_w_gu', 'delta_ffn1_w_down', 'delta_norm2_g', 'delta_w_mix_in', 'delta_b_merge', 'delta_conv_w', 'delta_w_conv_out', 'delta_w_attn_out', 'delta_w_out', 'delta_norm3_g', 'delta_ffn2_w_gu', 'delta_ffn2_w_down', 'delta_final_g', 'new_m_w_ada', 'new_m_b_ada', 'new_m_norm1_g', 'new_m_ffn1_w_gu', 'new_m_ffn1_w_down', 'new_m_norm2_g', 'new_m_w_mix_in', 'new_m_b_merge', 'new_m_conv_w', 'new_m_w_conv_out', 'new_m_w_attn_out', 'new_m_w_out', 'new_m_norm3_g', 'new_m_ffn2_w_gu', 'new_m_ffn2_w_down', 'new_m_final_g', 'new_v_w_ada', 'new_v_b_ada', 'new_v_norm1_g', 'new_v_ffn1_w_gu', 'new_v_ffn1_w_down', 'new_v_norm2_g', 'new_v_w_mix_in', 'new_v_b_merge', 'new_v_conv_w', 'new_v_w_conv_out', 'new_v_w_attn_out', 'new_v_w_out', 'new_v_norm3_g', 'new_v_ffn2_w_gu', 'new_v_ffn2_w_down', 'new_v_final_g']
TWIN_LEAF_KINDS = {'loss': 'loss', 'grad_x': 'grad_x', 'grad_w_ada': 'grad_w', 'grad_b_ada': 'grad_w', 'grad_norm1_g': 'grad_w', 'grad_ffn1_w_gu': 'grad_w', 'grad_ffn1_w_down': 'grad_w', 'grad_norm2_g': 'grad_w', 'grad_w_mix_in': 'grad_w', 'grad_b_merge': 'grad_w', 'grad_conv_w': 'grad_w', 'grad_w_conv_out': 'grad_w', 'grad_w_attn_out': 'grad_w', 'grad_w_out': 'grad_w', 'grad_norm3_g': 'grad_w', 'grad_ffn2_w_gu': 'grad_w', 'grad_ffn2_w_down': 'grad_w', 'grad_final_g': 'grad_w', 'delta_w_ada': 'delta_w', 'delta_b_ada': 'delta_w', 'delta_norm1_g': 'delta_w', 'delta_ffn1_w_gu': 'delta_w', 'delta_ffn1_w_down': 'delta_w', 'delta_norm2_g': 'delta_w', 'delta_w_mix_in': 'delta_w', 'delta_b_merge': 'delta_w', 'delta_conv_w': 'delta_w', 'delta_w_conv_out': 'delta_w', 'delta_w_attn_out': 'delta_w', 'delta_w_out': 'delta_w', 'delta_norm3_g': 'delta_w', 'delta_ffn2_w_gu': 'delta_w', 'delta_ffn2_w_down': 'delta_w', 'delta_final_g': 'delta_w', 'new_m_w_ada': 'new_m', 'new_m_b_ada': 'new_m', 'new_m_norm1_g': 'new_m', 'new_m_ffn1_w_gu': 'new_m', 'new_m_ffn1_w_down': 'new_m', 'new_m_norm2_g': 'new_m', 'new_m_w_mix_in': 'new_m', 'new_m_b_merge': 'new_m', 'new_m_conv_w': 'new_m', 'new_m_w_conv_out': 'new_m', 'new_m_w_attn_out': 'new_m', 'new_m_w_out': 'new_m', 'new_m_norm3_g': 'new_m', 'new_m_ffn2_w_gu': 'new_m', 'new_m_ffn2_w_down': 'new_m', 'new_m_final_g': 'new_m', 'new_v_w_ada': 'new_v', 'new_v_b_ada': 'new_v', 'new_v_norm1_g': 'new_v', 'new_v_ffn1_w_gu': 'new_v', 'new_v_ffn1_w_down': 'new_v', 'new_v_norm2_g': 'new_v', 'new_v_w_mix_in': 'new_v', 'new_v_b_merge': 'new_v', 'new_v_conv_w': 'new_v', 'new_v_w_conv_out': 'new_v', 'new_v_w_attn_out': 'new_v', 'new_v_w_out': 'new_v', 'new_v_norm3_g': 'new_v', 'new_v_ffn2_w_gu': 'new_v', 'new_v_ffn2_w_down': 'new_v', 'new_v_final_g': 'new_v'}


def _forward(args):
    return _fwd_reference(*[args[k] for k in FWD_PARAMS])


def _output_shape():
    def fwd():
        inp = _fwd_setup_inputs(0)
        return _fwd_reference(*[inp[k] for k in FWD_PARAMS])
    out = _jax.eval_shape(fwd)
    return out.shape, out.dtype

N_MICROBATCH = 1
ADAM_LR = 0.001
ADAM_B1 = 0.9
ADAM_B2 = 0.999
ADAM_EPS = 1e-08
ADAM_WD = 0.01
ADAM_STEP = 10
PER_EXAMPLE_BATCH_AXIS = {'x': 0, 'c': 0, 'loss_target': 0}
SHARED_INPUTS = []
_WEIGHT_DTYPES = {'w_ada': _jnp.float32, 'b_ada': _jnp.float32, 'norm1_g': _jnp.float32, 'ffn1_w_gu': _jnp.float32, 'ffn1_w_down': _jnp.float32, 'norm2_g': _jnp.float32, 'w_mix_in': _jnp.float32, 'b_merge': _jnp.float32, 'conv_w': _jnp.float32, 'w_conv_out': _jnp.float32, 'w_attn_out': _jnp.float32, 'w_out': _jnp.float32, 'norm3_g': _jnp.float32, 'ffn2_w_gu': _jnp.float32, 'ffn2_w_down': _jnp.float32, 'final_g': _jnp.float32}
MOMENT_SCALE = {'w_ada': 4.197014e-02, 'b_ada': 7.333103e-02, 'norm1_g': 2.700039e-02, 'ffn1_w_gu': 1.197634e-02, 'ffn1_w_down': 1.956405e-02, 'norm2_g': 6.814598e-02, 'w_mix_in': 3.190746e-02, 'b_merge': 1.136475e-02, 'conv_w': 5.159439e-02, 'w_conv_out': 3.669439e-02, 'w_attn_out': 2.175773e-02, 'w_out': 4.282918e-02, 'norm3_g': 2.532103e-02, 'ffn2_w_gu': 1.129386e-02, 'ffn2_w_down': 1.841702e-02, 'final_g': 3.202841e+01}


def _to_microbatches(a, axis):
    t = _jnp.moveaxis(a, axis, 0)
    t = t.reshape((N_MICROBATCH, t.shape[0] // N_MICROBATCH) + t.shape[1:])
    return _jnp.moveaxis(t, 1, axis + 1)


def setup_inputs(seed: int = 0) -> dict:
    inp = _fwd_setup_inputs(seed)
    key = _jax.random.fold_in(_jax.random.key(seed), 7919)
    shape, _ = _output_shape()
    out = dict(inp)
    out["loss_target"] = _jax.random.normal(_jax.random.fold_in(key, 0), shape, _jnp.float32)
    for i, name in enumerate(TWIN_WEIGHTS):
        w = inp[name].astype(_jnp.float32)
        if MOMENT_SCALE is None:
            s = _jnp.sqrt(_jnp.mean(_jnp.square(w)) + 1e-30)
        else:
            s = MOMENT_SCALE[name]
        km, kv = _jax.random.split(_jax.random.fold_in(key, i + 1))
        out[name] = w
        out["m_" + name] = s * _jax.random.normal(km, w.shape, _jnp.float32)
        out["v_" + name] = (s * s) * _jax.random.uniform(kv, w.shape, _jnp.float32, 0.5, 1.5)
    if N_MICROBATCH > 1:
        for name, axis in PER_EXAMPLE_BATCH_AXIS.items():
            out[name] = _to_microbatches(out[name], axis)
    return {'x': out['x'], 'c': out['c'], 'w_ada': out['w_ada'], 'b_ada': out['b_ada'], 'norm1_g': out['norm1_g'], 'ffn1_w_gu': out['ffn1_w_gu'], 'ffn1_w_down': out['ffn1_w_down'], 'norm2_g': out['norm2_g'], 'w_mix_in': out['w_mix_in'], 'b_merge': out['b_merge'], 'conv_w': out['conv_w'], 'w_conv_out': out['w_conv_out'], 'w_attn_out': out['w_attn_out'], 'w_out': out['w_out'], 'norm3_g': out['norm3_g'], 'ffn2_w_gu': out['ffn2_w_gu'], 'ffn2_w_down': out['ffn2_w_down'], 'final_g': out['final_g'], 'loss_target': out['loss_target'], 'm_w_ada': out['m_w_ada'], 'm_b_ada': out['m_b_ada'], 'm_norm1_g': out['m_norm1_g'], 'm_ffn1_w_gu': out['m_ffn1_w_gu'], 'm_ffn1_w_down': out['m_ffn1_w_down'], 'm_norm2_g': out['m_norm2_g'], 'm_w_mix_in': out['m_w_mix_in'], 'm_b_merge': out['m_b_merge'], 'm_conv_w': out['m_conv_w'], 'm_w_conv_out': out['m_w_conv_out'], 'm_w_attn_out': out['m_w_attn_out'], 'm_w_out': out['m_w_out'], 'm_norm3_g': out['m_norm3_g'], 'm_ffn2_w_gu': out['m_ffn2_w_gu'], 'm_ffn2_w_down': out['m_ffn2_w_down'], 'm_final_g': out['m_final_g'], 'v_w_ada': out['v_w_ada'], 'v_b_ada': out['v_b_ada'], 'v_norm1_g': out['v_norm1_g'], 'v_ffn1_w_gu': out['v_ffn1_w_gu'], 'v_ffn1_w_down': out['v_ffn1_w_down'], 'v_norm2_g': out['v_norm2_g'], 'v_w_mix_in': out['v_w_mix_in'], 'v_b_merge': out['v_b_merge'], 'v_conv_w': out['v_conv_w'], 'v_w_conv_out': out['v_w_conv_out'], 'v_w_attn_out': out['v_w_attn_out'], 'v_w_out': out['v_w_out'], 'v_norm3_g': out['v_norm3_g'], 'v_ffn2_w_gu': out['v_ffn2_w_gu'], 'v_ffn2_w_down': out['v_ffn2_w_down'], 'v_final_g': out['v_final_g']}


def _loss(weights, diff, rest, loss_target):
    with _jax.named_scope("forward"):
        args = {**rest, TWIN_DIFF_INPUT: diff, **{k: w.astype(_WEIGHT_DTYPES[k]) for k, w in weights.items()}}
        y = _forward(args)
    with _jax.named_scope("loss_head"):
        err = _jnp.square(y.astype(_jnp.float32) - loss_target)
        return 0.5 * _jnp.sum(_jnp.mean(err, axis=-1)) if err.ndim else 0.5 * err


def _adamw(w, g, m, v):
    m = ADAM_B1 * m + (1.0 - ADAM_B1) * g
    v = ADAM_B2 * v + (1.0 - ADAM_B2) * _jnp.square(g)
    m_hat = m / (1.0 - ADAM_B1 ** ADAM_STEP)
    v_hat = v / (1.0 - ADAM_B2 ** ADAM_STEP)
    delta = -ADAM_LR * (m_hat / (_jnp.sqrt(v_hat) + ADAM_EPS) + ADAM_WD * w)
    return delta, m, v


def reference(x, c, w_ada, b_ada, norm1_g, ffn1_w_gu, ffn1_w_down, norm2_g, w_mix_in, b_merge, conv_w, w_conv_out, w_attn_out, w_out, norm3_g, ffn2_w_gu, ffn2_w_down, final_g, loss_target, m_w_ada, m_b_ada, m_norm1_g, m_ffn1_w_gu, m_ffn1_w_down, m_norm2_g, m_w_mix_in, m_b_merge, m_conv_w, m_w_conv_out, m_w_attn_out, m_w_out, m_norm3_g, m_ffn2_w_gu, m_ffn2_w_down, m_final_g, v_w_ada, v_b_ada, v_norm1_g, v_ffn1_w_gu, v_ffn1_w_down, v_norm2_g, v_w_mix_in, v_b_merge, v_conv_w, v_w_conv_out, v_w_attn_out, v_w_out, v_norm3_g, v_ffn2_w_gu, v_ffn2_w_down, v_final_g):
    given = dict(x=x, c=c, w_ada=w_ada, b_ada=b_ada, norm1_g=norm1_g, ffn1_w_gu=ffn1_w_gu, ffn1_w_down=ffn1_w_down, norm2_g=norm2_g, w_mix_in=w_mix_in, b_merge=b_merge, conv_w=conv_w, w_conv_out=w_conv_out, w_attn_out=w_attn_out, w_out=w_out, norm3_g=norm3_g, ffn2_w_gu=ffn2_w_gu, ffn2_w_down=ffn2_w_down, final_g=final_g, loss_target=loss_target, m_w_ada=m_w_ada, m_b_ada=m_b_ada, m_norm1_g=m_norm1_g, m_ffn1_w_gu=m_ffn1_w_gu, m_ffn1_w_down=m_ffn1_w_down, m_norm2_g=m_norm2_g, m_w_mix_in=m_w_mix_in, m_b_merge=m_b_merge, m_conv_w=m_conv_w, m_w_conv_out=m_w_conv_out, m_w_attn_out=m_w_attn_out, m_w_out=m_w_out, m_norm3_g=m_norm3_g, m_ffn2_w_gu=m_ffn2_w_gu, m_ffn2_w_down=m_ffn2_w_down, m_final_g=m_final_g, v_w_ada=v_w_ada, v_b_ada=v_b_ada, v_norm1_g=v_norm1_g, v_ffn1_w_gu=v_ffn1_w_gu, v_ffn1_w_down=v_ffn1_w_down, v_norm2_g=v_norm2_g, v_w_mix_in=v_w_mix_in, v_b_merge=v_b_merge, v_conv_w=v_conv_w, v_w_conv_out=v_w_conv_out, v_w_attn_out=v_w_attn_out, v_w_out=v_w_out, v_norm3_g=v_norm3_g, v_ffn2_w_gu=v_ffn2_w_gu, v_ffn2_w_down=v_ffn2_w_down, v_final_g=v_final_g)
    weights = {n: given[n] for n in TWIN_WEIGHTS}
    shared = {n: given[n] for n in SHARED_INPUTS}
    per_example = {n: given[n] for n in ['x', 'c']}
    grad_fn = _jax.value_and_grad(_loss, argnums=(0, 1))

    def one_microbatch(ex, loss_target):
        ex = dict(ex)
        diff = ex.pop(TWIN_DIFF_INPUT)
        return grad_fn(weights, diff, {**shared, **ex}, loss_target)

    if N_MICROBATCH == 1:
        loss, (grad_w, grad_x) = one_microbatch(per_example, given["loss_target"])
    else:
        def body(carry, xs):
            loss_sum, grad_sum = carry
            l_k, (gw_k, gx_k) = one_microbatch(xs[0], xs[1])
            with _jax.named_scope("update"):
                return (loss_sum + l_k, _jax.tree.map(_jnp.add, grad_sum, gw_k)), gx_k

        init = (_jnp.zeros((), _jnp.float32), _jax.tree.map(_jnp.zeros_like, weights))
        (loss, grad_w), grad_x = _jax.lax.scan(body, init, (per_example, given["loss_target"]))
    with _jax.named_scope("update"):
        delta_w, new_m, new_v = {}, {}, {}
        for n in TWIN_WEIGHTS:
            delta_w[n], new_m[n], new_v[n] = _adamw(weights[n], grad_w[n], given["m_" + n], given["v_" + n])
    return (loss, grad_x, *[grad_w[n] for n in TWIN_WEIGHTS], *[delta_w[n] for n in TWIN_WEIGHTS],
            *[new_m[n] for n in TWIN_WEIGHTS], *[new_v[n] for n in TWIN_WEIGHTS])
```

```python
import functools

import jax
import jax.numpy as jnp
from jax import lax
from jax.experimental import pallas as pl
from jax.experimental.pallas import tpu as pltpu

F32 = jnp.float32
BF16 = jnp.bfloat16
MESH = pl.DeviceIdType.MESH
AXES = ("x", "y", "c")

VMEM_LIMIT = 56 * 1024 * 1024
LANES = 128

D = 1024
D_FF = 2816
FF_SHARD = 1408
MIX_SHARD = 1280
MIX_W = 5120
HEAD_PAIRS = 4
HEAD_DIM = 64
CONV_W = 512
EPS = 1e-6
ATT_BLK = 128

ADAM_LR = 0.001
ADAM_B1 = 0.9
ADAM_B2 = 0.999
ADAM_EPS = 1e-08
ADAM_WD = 0.01
ADAM_STEP = 10


def _params(semantics=None):
    return pltpu.CompilerParams(dimension_semantics=semantics, vmem_limit_bytes=VMEM_LIMIT)


def _sigmoid(x):
    return 1.0 / (1.0 + jnp.exp(-x))


def _place():
    x, y, c = lax.axis_index("x"), lax.axis_index("y"), lax.axis_index("c")
    chips = [(1 - x, y), (x, 1 - y), (1 - x, 1 - y)]
    return x, y, c, chips


def _allgather_rows(name, blk):
    m_per, n = blk.shape

    def body(x_ref, out_ref, send_sems, recv_sems, local_sem):
        x, y, c, chips = _place()
        me, sibling = (x, y, c), (x, y, 1 - c)

        def rows(px, py, pc):
            return out_ref.at[pl.ds((4 * px + 2 * py + pc) * m_per, m_per), :]

        def copy(k, block, to, src=None):
            return pltpu.make_async_remote_copy(
                src_ref=rows(*block) if src is None else src, dst_ref=rows(*block),
                send_sem=send_sems.at[k], recv_sem=recv_sems.at[k], device_id=to, device_id_type=MESH)

        mine = pltpu.make_async_copy(x_ref, rows(*me), local_sem)
        mine.start()
        first = [copy(0, me, sibling, src=x_ref)]
        first += [copy(1 + j, me, (*chip, c), src=x_ref) for j, chip in enumerate(chips)]
        for cp in first:
            cp.start()
        passed = [copy(4 + j, (*chip, c), sibling) for j, chip in enumerate(chips)]
        for j, chip in enumerate(chips):
            copy(1 + j, (*chip, c), me).wait_recv()
            passed[j].start()
        copy(0, sibling, me).wait_recv()
        for j, chip in enumerate(chips):
            copy(4 + j, (*chip, 1 - c), me).wait_recv()
        for cp in first + passed:
            cp.wait_send()
        mine.wait()

    return pl.pallas_call(
        body, name=name,
        out_shape=jax.ShapeDtypeStruct((8 * m_per, n), blk.dtype),
        in_specs=[pl.BlockSpec(memory_space=pltpu.VMEM)],
        out_specs=pl.BlockSpec(memory_space=pltpu.VMEM),
        scratch_shapes=[pltpu.SemaphoreType.DMA((7,)), pltpu.SemaphoreType.DMA((7,)), pltpu.SemaphoreType.DMA],
        compiler_params=pltpu.CompilerParams(vmem_limit_bytes=VMEM_LIMIT),
    )(blk)


def _hbm_specs(n):
    return [pl.BlockSpec(memory_space=pltpu.HBM)] * n


def _allgather_weights(shards):
    n = len(shards)

    def body(*refs):
        ins, outs = refs[:n], refs[n:2 * n]
        send_sems, recv_sems, local_sems = refs[2 * n:]
        x, y, c, chips = _place()
        sibling = (x, y, 1 - c)
        me_k = 2 * x + y

        def half(w, k, hc):
            h = ins[w].shape[0] // 2
            return outs[w].at[k, pl.ds(pl.multiple_of(hc * h, 8), h), :]

        def copy(w, j, k, hc, to, src=None):
            dst = half(w, k, hc)
            return pltpu.make_async_remote_copy(
                src_ref=dst if src is None else src, dst_ref=dst,
                send_sem=send_sems.at[6 * w + j], recv_sem=recv_sems.at[6 * w + j],
                device_id=to, device_id_type=MESH)

        mine, first, passed = [], [], []
        for w in range(n):
            cp = pltpu.make_async_copy(ins[w], outs[w].at[me_k], local_sems.at[w])
            cp.start()
            mine.append(cp)
            h = ins[w].shape[0] // 2
            src = ins[w].at[pl.ds(pl.multiple_of(c * h, 8), h), :]
            for j, chip in enumerate(chips):
                cp = copy(w, j, me_k, c, (*chip, c), src=src)
                cp.start()
                first.append(cp)
        for w in range(n):
            for j, (px, py) in enumerate(chips):
                copy(w, j, 2 * px + py, c, (x, y, c)).wait_recv()
                cp = copy(w, 3 + j, 2 * px + py, c, sibling)
                cp.start()
                passed.append(cp)
        for w in range(n):
            for j, (px, py) in enumerate(chips):
                copy(w, 3 + j, 2 * px + py, 1 - c, (x, y, c)).wait_recv()
        for cp in first + passed:
            cp.wait_send()
        for cp in mine:
            cp.wait()

    return pl.pallas_call(
        body, name="allgather_weights",
        out_shape=[jax.ShapeDtypeStruct((4, *s.shape), s.dtype) for s in shards],
        in_specs=_hbm_specs(n), out_specs=_hbm_specs(n),
        scratch_shapes=[pltpu.SemaphoreType.DMA((6 * n,)), pltpu.SemaphoreType.DMA((6 * n,)),
                        pltpu.SemaphoreType.DMA((n,))],
    )(*shards)


def _sibling_swap_halves(grads):
    n = len(grads)

    def body(*refs):
        ins, outs = refs[:n], refs[n:2 * n]
        send_sems, recv_sems = refs[2 * n:]
        x, y, c, _ = _place()
        cps = []
        for w in range(n):
            cp = pltpu.make_async_remote_copy(
                src_ref=ins[w].at[:, 1 - c], dst_ref=outs[w],
                send_sem=send_sems.at[w], recv_sem=recv_sems.at[w],
                device_id=(x, y, 1 - c), device_id_type=MESH)
            cp.start()
            cps.append(cp)
        for cp in cps:
            cp.wait()

    return pl.pallas_call(
        body, name="grad_sibling_swap",
        out_shape=[jax.ShapeDtypeStruct((4, *g.shape[2:]), g.dtype) for g in grads],
        in_specs=_hbm_specs(n), out_specs=_hbm_specs(n),
        scratch_shapes=[pltpu.SemaphoreType.DMA((n,)), pltpu.SemaphoreType.DMA((n,))],
    )(*grads)


def _chip_all_to_all(parts):
    n = len(parts)

    def body(*refs):
        ins, outs = refs[:n], refs[n:2 * n]
        send_sems, recv_sems, local_sems = refs[2 * n:]
        x, y, c, chips = _place()
        me_k = 2 * x + y
        mine, sent = [], []
        for w in range(n):
            cp = pltpu.make_async_copy(ins[w].at[me_k], outs[w].at[me_k], local_sems.at[w])
            cp.start()
            mine.append(cp)
            for j, (px, py) in enumerate(chips):
                cp = pltpu.make_async_remote_copy(
                    src_ref=ins[w].at[2 * px + py], dst_ref=outs[w].at[me_k],
                    send_sem=send_sems.at[3 * w + j], recv_sem=recv_sems.at[3 * w + j],
                    device_id=(px, py, c), device_id_type=MESH)
                cp.start()
                sent.append(cp)
        for w in range(n):
            for j, (px, py) in enumerate(chips):
                slab = outs[w].at[2 * px + py]
                pltpu.make_async_remote_copy(
                    src_ref=slab, dst_ref=slab, send_sem=send_sems.at[3 * w + j],
                    recv_sem=recv_sems.at[3 * w + j], device_id=(px, py, c), device_id_type=MESH).wait_recv()
        for cp in sent:
            cp.wait_send()
        for cp in mine:
            cp.wait()

    return pl.pallas_call(
        body, name="grad_chip_all_to_all",
        out_shape=[jax.ShapeDtypeStruct(p.shape, p.dtype) for p in parts],
        in_specs=_hbm_specs(n), out_specs=_hbm_specs(n),
        scratch_shapes=[pltpu.SemaphoreType.DMA((3 * n,)), pltpu.SemaphoreType.DMA((3 * n,)),
                        pltpu.SemaphoreType.DMA((n,))],
    )(*parts)


def _sibling_share(halves):
    n = len(halves)

    def body(*refs):
        ins, outs = refs[:n], refs[n:2 * n]
        send_sems, recv_sems, local_sems = refs[2 * n:]
        x, y, c, _ = _place()
        mine, sent = [], []
        for w in range(n):
            cp = pltpu.make_async_copy(ins[w], outs[w].at[c], local_sems.at[w])
            cp.start()
            mine.append(cp)
            cp = pltpu.make_async_remote_copy(
                src_ref=ins[w], dst_ref=outs[w].at[c], send_sem=send_sems.at[w], recv_sem=recv_sems.at[w],
                device_id=(x, y, 1 - c), device_id_type=MESH)
            cp.start()
            sent.append(cp)
        for w in range(n):
            slab = outs[w].at[1 - c]
            pltpu.make_async_remote_copy(
                src_ref=slab, dst_ref=slab, send_sem=send_sems.at[w], recv_sem=recv_sems.at[w],
                device_id=(x, y, 1 - c), device_id_type=MESH).wait_recv()
        for cp in sent:
            cp.wait_send()
        for cp in mine:
            cp.wait()

    return pl.pallas_call(
        body, name="grad_sibling_share",
        out_shape=[jax.ShapeDtypeStruct((2, *p.shape), p.dtype) for p in halves],
        in_specs=_hbm_specs(n), out_specs=_hbm_specs(n),
        scratch_shapes=[pltpu.SemaphoreType.DMA((n,)), pltpu.SemaphoreType.DMA((n,)),
                        pltpu.SemaphoreType.DMA((n,))],
    )(*halves)


def _mm(name, grid, a, a_spec, b, b_spec, contract, out_shapes, out_specs, epilogue,
        extras=(), extra_specs=(), nk=1, acc_shape=None, semantics=None):
    ne, no = len(extras), len(out_shapes)
    nd = len(grid)

    def body(*refs):
        a_ref, b_ref = refs[0], refs[1]
        ex, outs = refs[2:2 + ne], refs[2 + ne:2 + ne + no]

        def prod():
            return lax.dot_general(a_ref[...], b_ref[...], (((contract[0],), (contract[1],)), ((), ())),
                                   preferred_element_type=F32)

        if nk == 1:
            epilogue(prod(), ex, outs)
        else:
            acc = refs[-1]
            k = pl.program_id(nd - 1)

            @pl.when(k == 0)
            def _():
                acc[...] = prod()

            @pl.when(k > 0)
            def _():
                acc[...] += prod()

            @pl.when(k == nk - 1)
            def _():
                epilogue(acc[...], ex, outs)

    if semantics is None:
        semantics = ("arbitrary",) * nd
    return pl.pallas_call(
        body, name=name, grid=grid,
        in_specs=[a_spec, b_spec, *extra_specs], out_specs=list(out_specs), out_shape=list(out_shapes),
        scratch_shapes=[] if nk == 1 else [pltpu.VMEM(acc_shape, F32)],
        compiler_params=_params(semantics),
    )(a, b, *extras)


def _store(dtype):
    def epilogue(acc, ex, outs):
        outs[0][...] = acc.astype(dtype)
    return epilogue


def _sds(shape, dtype):
    return jax.ShapeDtypeStruct(shape, dtype)


TR = 512


def _row_spec(width, tr=TR):
    return pl.BlockSpec((tr, width), lambda i: (i, 0))


def _const_spec(shape):
    nd = len(shape)
    return pl.BlockSpec(shape, lambda i: (0,) * nd)


def _norm_mod_fwd(name, h, p):
    S = h.shape[0]

    def body(h_ref, p_ref, u_ref):
        hv = h_ref[...]
        r = lax.rsqrt(jnp.mean(hv * hv, axis=-1, keepdims=True) + EPS)
        nrm = (hv * r) * p_ref[0:1, :]
        u_ref[...] = (nrm * (1.0 + p_ref[1:2, :]) + p_ref[2:3, :]).astype(BF16)

    return pl.pallas_call(
        body, name=name, grid=(S // TR,),
        in_specs=[_row_spec(D), _const_spec((8, D))], out_specs=_row_spec(D),
        out_shape=_sds((S, D), BF16), compiler_params=_params(("parallel",)),
    )(h, p)


def _final_loss_bwd(h, gf, target, p3, f3):
    S = h.shape[0]

    def body(h_ref, g_ref, t_ref, p_ref, f_ref, dh_ref, df_ref, sums_ref, loss_ref):
        i = pl.program_id(0)

        @pl.when(i == 0)
        def _():
            sums_ref[...] = jnp.zeros_like(sums_ref)
            loss_ref[...] = jnp.zeros_like(loss_ref)

        hv = h_ref[...]
        g = g_ref[0:1, :]
        r = lax.rsqrt(jnp.mean(hv * hv, axis=-1, keepdims=True) + EPS)
        xn = hv * r
        err = xn * g - t_ref[...]
        loss_ref[...] += 0.5 * jnp.sum(err * err) * (1.0 / D)
        dout = err * (1.0 / D)
        dxn = dout * g
        dh = r * (dxn - xn * jnp.mean(dxn * xn, axis=-1, keepdims=True))
        dh_ref[...] = dh
        gate = p_ref[3:4, :]
        df_ref[...] = (0.5 * gate * dh).astype(BF16)
        sums_ref[0:1, :] += jnp.sum(dout * xn, axis=0, keepdims=True)
        sums_ref[1:2, :] += 0.5 * jnp.sum(dh * f_ref[...].astype(F32), axis=0, keepdims=True)

    return pl.pallas_call(
        body, name="final_loss_bwd", grid=(S // TR,),
        in_specs=[_row_spec(D), _const_spec((8, D)), _row_spec(D), _const_spec((8, D)), _row_spec(D)],
        out_specs=[_row_spec(D), _row_spec(D), _const_spec((8, D)), _const_spec((8, LANES))],
        out_shape=[_sds((S, D), F32), _sds((S, D), BF16), _sds((8, D), F32), _sds((8, LANES), F32)],
        compiler_params=_params(("arbitrary",)),
    )(h, gf, target, p3, f3)


def _norm_mod_bwd(name, du, h, p, dh_res, prev=None):
    S = h.shape[0]
    has_prev = prev is not None

    def body(*refs):
        if has_prev:
            du_ref, h_ref, p_ref, r_ref, pp_ref, f_ref, dh_ref, df_ref, sums_ref = refs
        else:
            du_ref, h_ref, p_ref, r_ref, dh_ref, sums_ref = refs
        i = pl.program_id(0)

        @pl.when(i == 0)
        def _():
            sums_ref[...] = jnp.zeros_like(sums_ref)

        hv = h_ref[...]
        duv = du_ref[...]
        g = p_ref[0:1, :]
        one_scale = 1.0 + p_ref[1:2, :]
        r = lax.rsqrt(jnp.mean(hv * hv, axis=-1, keepdims=True) + EPS)
        xn = hv * r
        dn = duv * one_scale
        dxn = dn * g
        dh = r_ref[...] + r * (dxn - xn * jnp.mean(dxn * xn, axis=-1, keepdims=True))
        dh_ref[...] = dh
        sums_ref[0:1, :] += jnp.sum(duv, axis=0, keepdims=True)
        sums_ref[1:2, :] += jnp.sum(duv * (xn * g), axis=0, keepdims=True)
        sums_ref[2:3, :] += jnp.sum(dn * xn, axis=0, keepdims=True)
        if has_prev:
            wgt = prev[2]
            df_ref[...] = (wgt * pp_ref[3:4, :] * dh).astype(BF16)
            sums_ref[3:4, :] += wgt * jnp.sum(dh * f_ref[...].astype(F32), axis=0, keepdims=True)

    ins = [du, h, p, dh_res]
    in_specs = [_row_spec(D), _row_spec(D), _const_spec((8, D)), _row_spec(D)]
    out_specs = [_row_spec(D)]
    out_shape = [_sds((S, D), F32)]
    if has_prev:
        ins += [prev[0], prev[1]]
        in_specs += [_const_spec((8, D)), _row_spec(D)]
        out_specs.append(_row_spec(D))
        out_shape.append(_sds((S, D), BF16))
    out_specs.append(_const_spec((8, D)))
    out_shape.append(_sds((8, D), F32))
    return pl.pallas_call(
        body, name=name, grid=(S // TR,), in_specs=in_specs, out_specs=out_specs, out_shape=out_shape,
        compiler_params=_params(("arbitrary",)),
    )(*ins)


TM = 512


def _ffn_up(name, u, wgu4):
    S = u.shape[0]

    def body(u_ref, wg_ref, wu_ref, gu_ref, hm_ref):
        uv = u_ref[...]
        g = jnp.dot(uv, wg_ref[...], preferred_element_type=F32)
        up = jnp.dot(uv, wu_ref[...], preferred_element_type=F32)
        gu_ref[0] = g.astype(BF16)
        gu_ref[1] = up.astype(BF16)
        hm_ref[...] = (g * _sigmoid(g) * up).astype(BF16)

    return pl.pallas_call(
        body, name=name, grid=(2, S // TM),
        in_specs=[pl.BlockSpec((TM, D), lambda s, i: (i, 0)),
                  pl.BlockSpec((None, D, FF_SHARD), lambda s, i: (s, 0, 0)),
                  pl.BlockSpec((None, D, FF_SHARD), lambda s, i: (s + 2, 0, 0))],
        out_specs=[pl.BlockSpec((2, TM, FF_SHARD), lambda s, i: (0, i, s)),
                   pl.BlockSpec((TM, FF_SHARD), lambda s, i: (i, s))],
        out_shape=[_sds((2, S, D_FF), BF16), _sds((S, D_FF), BF16)],
        compiler_params=_params(("parallel", "parallel")),
    )(u, wgu4, wgu4)


def _proj_residual(name, a, w, h, p, weight):
    S, K = a.shape

    def epilogue(acc, ex, outs):
        h_ref, p_ref = ex
        outs[0][...] = acc.astype(BF16)
        outs[1][...] = h_ref[...] + weight * p_ref[3:4, :] * acc

    return _mm(
        name, (S // TM,), a, pl.BlockSpec((TM, K), lambda i: (i, 0)), w, pl.BlockSpec((K, D), lambda i: (0, 0)),
        (1, 0), [_sds((S, D), BF16), _sds((S, D), F32)], [_row_spec(D, TM), _row_spec(D, TM)], epilogue,
        extras=(h, p), extra_specs=(_row_spec(D, TM), _const_spec((8, D))), semantics=("parallel",))


def _ffn_down_bwd(name, df, wd, gu):
    S = df.shape[0]

    def epilogue(acc, ex, outs):
        g = ex[0][0].astype(F32)
        up = ex[0][1].astype(F32)
        sg = _sigmoid(g)
        outs[0][0] = (acc * up * (sg * (1.0 + g * (1.0 - sg)))).astype(BF16)
        outs[0][1] = (acc * g * sg).astype(BF16)

    gu_spec = pl.BlockSpec((2, TM, FF_SHARD), lambda n, i: (0, i, n))
    return _mm(
        name, (2, S // TM), df, pl.BlockSpec((TM, D), lambda n, i: (i, 0)),
        wd, pl.BlockSpec((FF_SHARD, D), lambda n, i: (n, 0)), (1, 1),
        [_sds((2, S, D_FF), BF16)], [gu_spec], epilogue, extras=(gu,), extra_specs=(gu_spec,),
        semantics=("parallel", "parallel"))[0]


TK = 512


def _grad_w(name, a, a_w, b, b_w, b_map, n_out, out_shape, out_block, out_map):
    S = a.shape[0]
    nk = S // TK
    return _mm(
        name, (n_out, nk), a, pl.BlockSpec((TK, a_w), lambda s, k: (k, 0)), b, pl.BlockSpec(
            (None, TK, b_w) if b.ndim == 3 else (TK, b_w), b_map), (0, 0),
        [_sds(out_shape, F32)], [pl.BlockSpec(out_block, out_map)], _store(F32), nk=nk, acc_shape=(a_w, b_w),
        semantics=("parallel", "arbitrary"))[0]


def _ffn_bwd(tag, df, u_in, gu, hm, wgu4, wd):
    S = df.shape[0]
    dgu = _ffn_down_bwd(tag + "_down_bwd", df, wd, gu)
    dwd = _mm(
        tag + "_dw_down", (2, S // TK), hm, pl.BlockSpec((TK, FF_SHARD), lambda m, k: (k, m)),
        df, pl.BlockSpec((TK, D), lambda m, k: (k, 0)), (0, 0),
        [_sds((D_FF, D), F32)], [pl.BlockSpec((FF_SHARD, D), lambda m, k: (m, 0))], _store(F32),
        nk=S // TK, acc_shape=(FF_SHARD, D), semantics=("parallel", "arbitrary"))[0]
    du = _mm(
        tag + "_up_bwd", (S // TM, 4), dgu, pl.BlockSpec((None, TM, FF_SHARD), lambda i, s: (s // 2, i, s % 2)),
        wgu4, pl.BlockSpec((None, D, FF_SHARD), lambda i, s: (s, 0, 0)), (1, 1),
        [_sds((S, D), F32)], [pl.BlockSpec((TM, D), lambda i, s: (i, 0))], _store(F32),
        nk=4, acc_shape=(TM, D), semantics=("parallel", "arbitrary"))[0]
    dwgu = _grad_w(tag + "_dw_gu", u_in, D, dgu, FF_SHARD, lambda s, k: (s // 2, k, s % 2), 4,
                   (4, D, FF_SHARD), (None, D, FF_SHARD), lambda s, k: (s, 0, 0))
    return du, dwgu, dwd


def _shift_down(v, k, row):
    return jnp.where(row >= k, pltpu.roll(v, k, axis=0), 0.0)


def _shift_up(v, k, row, S):
    return jnp.where(row < S - k, pltpu.roll(v, S - k, axis=0), 0.0)


def _conv_specs(S):
    cols = CONV_W // LANES
    return [pl.BlockSpec((S, LANES), functools.partial(lambda j, off: (0, off + j), off=o * cols))
            for o in range(3)]


def _conv_fwd(proj, conv_w):
    S = proj.shape[0]

    def body(cb_ref, cc_ref, cx_ref, w_ref, sc_ref):
        row = lax.broadcasted_iota(jnp.int32, (S, LANES), 0)
        v = cc_ref[...].astype(F32) * cx_ref[...].astype(F32)
        yv = w_ref[0:1, :] * _shift_down(v, 2, row) + w_ref[1:2, :] * _shift_down(v, 1, row) + w_ref[2:3, :] * v
        sc_ref[...] = (cb_ref[...].astype(F32) * yv).astype(BF16)

    return pl.pallas_call(
        body, name="conv_fwd", grid=(CONV_W // LANES,),
        in_specs=_conv_specs(S) + [pl.BlockSpec((3, LANES), lambda j: (0, j))],
        out_specs=pl.BlockSpec((S, LANES), lambda j: (0, j)), out_shape=_sds((S, CONV_W), BF16),
        compiler_params=_params(("parallel",)),
    )(proj, proj, proj, conv_w)


def _conv_bwd(dsc, proj, conv_w):
    S = proj.shape[0]

    def body(d_ref, cb_ref, cc_ref, cx_ref, w_ref, dcb_ref, dcc_ref, dcx_ref, dw_ref):
        row = lax.broadcasted_iota(jnp.int32, (S, LANES), 0)
        cc = cc_ref[...].astype(F32)
        cx = cx_ref[...].astype(F32)
        d = d_ref[...].astype(F32)
        v = cc * cx
        v1 = _shift_down(v, 1, row)
        v2 = _shift_down(v, 2, row)
        w0, w1, w2 = w_ref[0:1, :], w_ref[1:2, :], w_ref[2:3, :]
        dcb_ref[...] = (d * (w0 * v2 + w1 * v1 + w2 * v)).astype(BF16)
        dy = d * cb_ref[...].astype(F32)
        dw_ref[0:1, :] = jnp.sum(dy * v2, axis=0, keepdims=True)
        dw_ref[1:2, :] = jnp.sum(dy * v1, axis=0, keepdims=True)
        dw_ref[2:3, :] = jnp.sum(dy * v, axis=0, keepdims=True)
        dv = w2 * dy + w1 * _shift_up(dy, 1, row, S) + w0 * _shift_up(dy, 2, row, S)
        dcc_ref[...] = (dv * cx).astype(BF16)
        dcx_ref[...] = (dv * cc).astype(BF16)

    col = pl.BlockSpec((S, LANES), lambda j: (0, j))
    return pl.pallas_call(
        body, name="conv_bwd", grid=(CONV_W // LANES,),
        in_specs=[col] + _conv_specs(S) + [pl.BlockSpec((3, LANES), lambda j: (0, j))],
        out_specs=[col, col, col, pl.BlockSpec((3, LANES), lambda j: (0, j))],
        out_shape=[_sds((S, CONV_W), BF16)] * 3 + [_sds((3, CONV_W), F32)],
        compiler_params=_params(("parallel",)),
    )(dsc, proj, proj, proj, conv_w)


Q_COL, K_COL, V_COL = 1536 // LANES, 2048 // LANES, 2560 // LANES


def _split_dot(x, tri):
    hi = x.astype(BF16)
    lo = (x - hi.astype(F32)).astype(BF16)
    return jnp.dot(hi, tri, preferred_element_type=F32) + jnp.dot(lo, tri, preferred_element_type=F32)


def _softplus(z):
    return jnp.maximum(z, 0.0) + jnp.log1p(jnp.exp(-jnp.abs(z)))


def _nt(a, b):
    return lax.dot_general(a, b, (((1,), (1,)), ((), ())), preferred_element_type=F32)


def _tn(a, b):
    return lax.dot_general(a, b, (((0,), (0,)), ((), ())), preferred_element_type=F32)


def _attn_fwd(proj):
    S = proj.shape[0]
    B = ATT_BLK

    def body(q_ref, k_ref, v_ref, o_ref, t_ref):
        i = pl.program_id(1)
        lo_lane = lax.broadcasted_iota(jnp.int32, (B, LANES), 1) < HEAD_DIM
        row = lax.broadcasted_iota(jnp.int32, (B, B), 0)
        col = lax.broadcasted_iota(jnp.int32, (B, B), 1)
        after = (row > col).astype(BF16)
        causal = col < row
        q2 = q_ref[...] * 0.125
        zero = jnp.zeros((), BF16)

        def tile(kb, carry, diag):
            ra, rb, acc = carry
            k2 = k_ref[pl.ds(pl.multiple_of(kb * B, B), B), :]
            v2 = v_ref[pl.ds(pl.multiple_of(kb * B, B), B), :]
            new_r = []
            for lo, r in ((True, ra), (False, rb)):
                keep = lo_lane if lo else jnp.logical_not(lo_lane)
                kh = jnp.where(keep, k2, zero)
                vh = jnp.where(keep, v2, zero)
                z = _nt(q2, kh)
                spz = _softplus(z)
                sp = jnp.where(causal, spz, 0.0) if diag else spz
                rem = _split_dot(sp, after) + r
                a = jnp.exp(z - spz - rem)
                if diag:
                    a = jnp.where(causal, a, 0.0)
                acc = acc + jnp.dot(a.astype(BF16), vh, preferred_element_type=F32)
                new_r.append(r + jnp.sum(sp, axis=1, keepdims=True))
            return new_r[0], new_r[1], acc

        init = (jnp.zeros((B, 1), F32), jnp.zeros((B, 1), F32), jnp.zeros((B, LANES), F32))
        carry = tile(i, init, True)
        ra, rb, acc = lax.fori_loop(0, i, lambda j, cr: tile(i - 1 - j, cr, False), carry)
        o_ref[...] = acc.astype(BF16)
        t_ref[...] = jnp.where(lo_lane, ra, rb)

    seq = lambda off: pl.BlockSpec((S, LANES), lambda p, i: (0, off + p))
    blk = pl.BlockSpec((B, LANES), lambda p, i: (i, p))
    return pl.pallas_call(
        body, name="attn_fwd", grid=(HEAD_PAIRS, S // B),
        in_specs=[pl.BlockSpec((B, LANES), lambda p, i: (i, Q_COL + p)), seq(K_COL), seq(V_COL)],
        out_specs=[blk, blk], out_shape=[_sds((S, 512), BF16), _sds((S, 512), F32)],
        compiler_params=_params(("parallel", "parallel")),
    )(proj, proj, proj)


def _attn_bwd(proj, o, t, do):
    S = proj.shape[0]
    B = ATT_BLK
    nq = S // B

    def body(q_ref, k_ref, v_ref, o_ref, t_ref, do_ref, dq_ref, dk_ref, dv_ref, dk_acc, dv_acc):
        i = pl.program_id(1)

        @pl.when(i == 0)
        def _():
            dk_acc[...] = jnp.zeros_like(dk_acc)
            dv_acc[...] = jnp.zeros_like(dv_acc)

        lo_lane = lax.broadcasted_iota(jnp.int32, (B, LANES), 1) < HEAD_DIM
        row = lax.broadcasted_iota(jnp.int32, (B, B), 0)
        col = lax.broadcasted_iota(jnp.int32, (B, B), 1)
        upto = (row <= col).astype(BF16)
        before = (row < col).astype(BF16)
        causal = col < row
        zero = jnp.zeros((), BF16)
        q2 = q_ref[...] * 0.125
        do2 = do_ref[...]
        tv = t_ref[...]
        heads = []
        for lo in (True, False):
            keep = lo_lane if lo else jnp.logical_not(lo_lane)
            lane0 = 0 if lo else HEAD_DIM
            heads.append((keep, jnp.where(keep, q2, zero), jnp.where(keep, do2, zero), tv[:, lane0:lane0 + 1]))

        def tile(kb, carry, diag):
            pa, pb, ea, eb, dq = carry
            rows = pl.ds(pl.multiple_of(kb * B, B), B)
            k2 = k_ref[rows, :]
            v2 = v_ref[rows, :]
            dk_t = jnp.zeros((B, LANES), F32)
            dv_t = jnp.zeros((B, LANES), F32)
            new_p, new_e = [], []
            for (keep, qh, doh, th), pc, ec in zip(heads, (pa, pb), (ea, eb)):
                kh = jnp.where(keep, k2, zero)
                vh = jnp.where(keep, v2, zero)
                z = _nt(q2, kh)
                spz = _softplus(z)
                sp = jnp.where(causal, spz, 0.0) if diag else spz
                rem = th - pc - _split_dot(sp, upto)
                a = jnp.exp(z - spz - rem)
                if diag:
                    a = jnp.where(causal, a, 0.0)
                e = a * _nt(do2, vh)
                e_before = ec + _split_dot(e, before)
                sig = jnp.exp(z - spz)
                dz = e * (1.0 - sig) - sig * e_before
                if diag:
                    dz = jnp.where(causal, dz, 0.0)
                dzb = dz.astype(BF16)
                dq = dq + jnp.dot(dzb, kh, preferred_element_type=F32)
                dk_t = dk_t + _tn(dzb, qh)
                dv_t = dv_t + _tn(a.astype(BF16), doh)
                new_p.append(pc + jnp.sum(sp, axis=1, keepdims=True))
                new_e.append(ec + jnp.sum(e, axis=1, keepdims=True))
            dk_acc[rows, :] += dk_t
            dv_acc[rows, :] += dv_t
            return new_p[0], new_p[1], new_e[0], new_e[1], dq

        zc = jnp.zeros((B, 1), F32)
        carry = lax.fori_loop(0, i, lambda j, cr: tile(j, cr, False), (zc, zc, zc, zc, jnp.zeros((B, LANES), F32)))
        dq = tile(i, carry, True)[4]
        dq_ref[...] = (dq * 0.125).astype(BF16)

        @pl.when(i == nq - 1)
        def _():
            dk_ref[...] = dk_acc[...].astype(BF16)
            dv_ref[...] = dv_acc[...].astype(BF16)

    seq = lambda off: pl.BlockSpec((S, LANES), lambda p, i: (0, off + p))
    blk = pl.BlockSpec((B, LANES), lambda p, i: (i, p))
    whole = pl.BlockSpec((S, LANES), lambda p, i: (0, p))
    return pl.pallas_call(
        body, name="attn_bwd", grid=(HEAD_PAIRS, nq),
        in_specs=[pl.BlockSpec((B, LANES), lambda p, i: (i, Q_COL + p)), seq(K_COL), seq(V_COL), blk, blk, blk],
        out_specs=[blk, whole, whole], out_shape=[_sds((S, 512), BF16)] * 3,
        scratch_shapes=[pltpu.VMEM((S, LANES), F32), pltpu.VMEM((S, LANES), F32)],
        compiler_params=_params(("parallel", "arbitrary")),
    )(proj, proj, proj, o, t, do)


GA_COL, GB_COL = 3072 // 256, 4096 // 256


def _merge_fwd(sc, o, wco4, wao4, proj, bm):
    S = sc.shape[0]

    def body(sc_ref, o_ref, wc_ref, wa_ref, ga_ref, gb_ref, bm_ref, ya_ref, yb_ref, mg_ref):
        ya = jnp.dot(sc_ref[...], wc_ref[...], preferred_element_type=F32)
        yb = jnp.dot(o_ref[...], wa_ref[...], preferred_element_type=F32)
        sa = _sigmoid(ga_ref[...].astype(F32) + bm_ref[0:1, :])
        sb = _sigmoid(gb_ref[...].astype(F32) + bm_ref[1:2, :])
        ya_ref[...] = ya.astype(BF16)
        yb_ref[...] = yb.astype(BF16)
        mg_ref[...] = (sa * ya + sb * yb).astype(BF16)

    wide = pl.BlockSpec((TM, 512), lambda n, i: (i, 0))
    wsp = pl.BlockSpec((None, 512, 256), lambda n, i: (n, 0, 0))
    out = pl.BlockSpec((TM, 256), lambda n, i: (i, n))
    return pl.pallas_call(
        body, name="merge_fwd", grid=(4, S // TM),
        in_specs=[wide, wide, wsp, wsp, pl.BlockSpec((TM, 256), lambda n, i: (i, GA_COL + n)),
                  pl.BlockSpec((TM, 256), lambda n, i: (i, GB_COL + n)), pl.BlockSpec((2, 256), lambda n, i: (0, n))],
        out_specs=[out, out, out], out_shape=[_sds((S, D), BF16)] * 3,
        compiler_params=_params(("parallel", "parallel")),
    )(sc, o, wco4, wao4, proj, proj, bm)


def _merge_bwd(dy2, wout, ya, yb, proj, bm):
    S = dy2.shape[0]

    def epilogue(acc, ex, outs):
        ya_ref, yb_ref, ga_ref, gb_ref, bm_ref = ex
        i = pl.program_id(1)
        sa = _sigmoid(ga_ref[...].astype(F32) + bm_ref[0:1, :])
        sb = _sigmoid(gb_ref[...].astype(F32) + bm_ref[1:2, :])
        dga = acc * ya_ref[...].astype(F32) * (sa * (1.0 - sa))
        dgb = acc * yb_ref[...].astype(F32) * (sb * (1.0 - sb))
        outs[0][...] = (acc * sa).astype(BF16)
        outs[1][...] = (acc * sb).astype(BF16)
        outs[2][...] = dga.astype(BF16)
        outs[3][...] = dgb.astype(BF16)

        @pl.when(i == 0)
        def _():
            outs[4][...] = jnp.zeros_like(outs[4])

        outs[4][0:1, :] += jnp.sum(dga, axis=0, keepdims=True)
        outs[4][1:2, :] += jnp.sum(dgb, axis=0, keepdims=True)

    out = pl.BlockSpec((TM, 256), lambda n, i: (i, n))
    return _mm(
        "merge_bwd", (4, S // TM), dy2, pl.BlockSpec((TM, D), lambda n, i: (i, 0)),
        wout, pl.BlockSpec((256, D), lambda n, i: (n, 0)), (1, 1),
        [_sds((S, D), BF16)] * 4 + [_sds((8, D), F32)], [out, out, out, out, pl.BlockSpec((8, 256), lambda n, i: (0, n))],
        epilogue, extras=(ya, yb, proj, proj, bm),
        extra_specs=(out, out, pl.BlockSpec((TM, 256), lambda n, i: (i, GA_COL + n)),
                     pl.BlockSpec((TM, 256), lambda n, i: (i, GB_COL + n)), pl.BlockSpec((2, 256), lambda n, i: (0, n))),
        semantics=("parallel", "arbitrary"))


def _back_through_cols(name, dy, w4, width):
    S = dy.shape[0]
    return _mm(
        name, (S // TM, 4), dy, pl.BlockSpec((TM, 256), lambda i, s: (i, s)),
        w4, pl.BlockSpec((None, width, 256), lambda i, s: (s, 0, 0)), (1, 1),
        [_sds((S, width), BF16)], [pl.BlockSpec((TM, width), lambda i, s: (i, 0))], _store(BF16),
        nk=4, acc_shape=(TM, width), semantics=("parallel", "arbitrary"))[0]


ADA_SHARD = 2304
ADA_TN = 768


def _ada_fwd(c_all, w_ada_l, b_l):
    def body(c_ref, w_ref, b_ref, o_ref):
        cv = c_ref[...]
        ca = cv * _sigmoid(cv)
        o_ref[...] = jnp.dot(ca.astype(BF16), w_ref[...].astype(BF16), preferred_element_type=F32) + b_ref[...]

    return pl.pallas_call(
        body, name="ada_fwd", grid=(ADA_SHARD // ADA_TN,),
        in_specs=[pl.BlockSpec((8, D), lambda j: (0, 0)), pl.BlockSpec((D, ADA_TN), lambda j: (0, j)),
                  pl.BlockSpec((1, ADA_TN), lambda j: (0, j))],
        out_specs=pl.BlockSpec((8, ADA_TN), lambda j: (0, j)), out_shape=_sds((8, ADA_SHARD), F32),
        compiler_params=_params(("parallel",)),
    )(c_all, w_ada_l, b_l)


def _ada_bwd(c_all_t, dmod_l):
    def body(c_ref, d_ref, o_ref):
        cv = c_ref[...]
        ca = cv * _sigmoid(cv)
        o_ref[...] = jnp.dot(ca.astype(BF16).astype(F32), d_ref[...].astype(BF16).astype(F32),
                             preferred_element_type=F32, precision=lax.Precision.HIGHEST)

    return pl.pallas_call(
        body, name="ada_bwd", grid=(ADA_SHARD // ADA_TN,),
        in_specs=[pl.BlockSpec((D, 8), lambda j: (0, 0)), pl.BlockSpec((8, ADA_TN), lambda j: (0, j))],
        out_specs=pl.BlockSpec((D, ADA_TN), lambda j: (0, j)), out_shape=_sds((D, ADA_SHARD), F32),
        compiler_params=_params(("parallel",)),
    )(c_all_t, dmod_l)


def _sum_rows(name, x):
    n = x.shape[1]

    def body(x_ref, o_ref):
        s = x_ref[0:1, :]
        for d in range(1, 8):
            s = s + x_ref[d:d + 1, :]
        o_ref[...] = s

    return pl.pallas_call(
        body, name=name, in_specs=[pl.BlockSpec(memory_space=pltpu.VMEM)],
        out_specs=pl.BlockSpec(memory_space=pltpu.VMEM), out_shape=_sds((1, n), F32),
        compiler_params=pltpu.CompilerParams(vmem_limit_bytes=VMEM_LIMIT),
    )(x)


def _pair_sum(name, g4, recv, c_idx):
    _, _, h, C = g4.shape
    tr = h if h <= 512 else h // (h // 256) if h % 256 == 0 else h // 2

    def body(c_ref, g_ref, r_ref, o_ref):
        o_ref[...] = g_ref[...] + r_ref[...]

    grid_spec = pltpu.PrefetchScalarGridSpec(
        num_scalar_prefetch=1, grid=(4, h // tr),
        in_specs=[pl.BlockSpec((None, None, tr, C), lambda k, i, c: (k, c[0], i, 0)),
                  pl.BlockSpec((None, tr, C), lambda k, i, c: (k, i, 0))],
        out_specs=pl.BlockSpec((None, tr, C), lambda k, i, c: (k, i, 0)))
    return pl.pallas_call(
        body, name=name, grid_spec=grid_spec, out_shape=_sds((4, h, C), F32),
        compiler_params=_params(("parallel", "parallel")),
    )(c_idx, g4, recv)


def _sum4(name, q):
    _, h, C = q.shape
    tr = h if h <= 512 else h // (h // 256) if h % 256 == 0 else h // 2

    def body(q_ref, o_ref):
        o_ref[...] = ((q_ref[0] + q_ref[1]) + q_ref[2]) + q_ref[3]

    return pl.pallas_call(
        body, name=name, grid=(h // tr,),
        in_specs=[pl.BlockSpec((4, tr, C), lambda i: (0, i, 0))], out_specs=pl.BlockSpec((tr, C), lambda i: (i, 0)),
        out_shape=_sds((h, C), F32), compiler_params=_params(("parallel",)),
    )(q)


def _adamw(name, w, g, m, v):
    R, C = w.shape
    tr = R
    while tr * C * 4 > (1 << 20) and tr % 16 == 0:
        tr //= 2
    c1 = 1.0 - ADAM_B1 ** ADAM_STEP
    c2 = 1.0 - ADAM_B2 ** ADAM_STEP

    def body(w_ref, g_ref, m_ref, v_ref, d_ref, nm_ref, nv_ref):
        gv = g_ref[...]
        nm = ADAM_B1 * m_ref[...] + (1.0 - ADAM_B1) * gv
        nv = ADAM_B2 * v_ref[...] + (1.0 - ADAM_B2) * (gv * gv)
        nm_ref[...] = nm
        nv_ref[...] = nv
        d_ref[...] = -ADAM_LR * ((nm / c1) / (jnp.sqrt(nv / c2) + ADAM_EPS) + ADAM_WD * w_ref[...])

    spec = pl.BlockSpec((tr, C), lambda i: (i, 0))
    return pl.pallas_call(
        body, name=name, grid=(R // tr,), in_specs=[spec] * 4, out_specs=[spec] * 3,
        out_shape=[_sds((R, C), F32)] * 3, compiler_params=_params(("parallel",)),
    )(w, g, m, v)


def _pack(g, scale, shift, gate):
    rows = jnp.stack([g, scale, shift, gate]).astype(F32)
    return jnp.concatenate([rows, jnp.zeros((4, D), F32)], axis=0)


def _pad8(vec):
    return jnp.concatenate([vec[None, :], jnp.zeros((7, vec.shape[0]), vec.dtype)], axis=0)


def kernel(x, c, w_ada, b_ada, norm1_g, ffn1_w_gu, ffn1_w_down, norm2_g, w_mix_in, b_merge, conv_w, w_conv_out, w_attn_out, w_out, norm3_g, ffn2_w_gu, ffn2_w_down, final_g, loss_target, m_w_ada, m_b_ada, m_norm1_g, m_ffn1_w_gu, m_ffn1_w_down, m_norm2_g, m_w_mix_in, m_b_merge, m_conv_w, m_w_conv_out, m_w_attn_out, m_w_out, m_norm3_g, m_ffn2_w_gu, m_ffn2_w_down, m_final_g, v_w_ada, v_b_ada, v_norm1_g, v_ffn1_w_gu, v_ffn1_w_down, v_norm2_g, v_w_mix_in, v_b_merge, v_conv_w, v_w_conv_out, v_w_attn_out, v_w_out, v_norm3_g, v_ffn2_w_gu, v_ffn2_w_down, v_final_g):
    xi, yi, ci = lax.axis_index("x"), lax.axis_index("y"), lax.axis_index("c")
    chip = 2 * xi + yi
    dev = 4 * xi + 2 * yi + ci
    S = x.shape[1]
    h0 = x[0]
    target = loss_target[0]

    big = [ffn1_w_gu[0], ffn1_w_down[0], w_mix_in[0], w_conv_out[0], w_attn_out[0], w_out[0], ffn2_w_gu[0],
           ffn2_w_down[0]]
    wgu1, wd1, wmix, wco, wao, wout, wgu2, wd2 = _allgather_weights([w.astype(BF16) for w in big])
    wd1 = wd1.reshape(D_FF, D)
    wd2 = wd2.reshape(D_FF, D)
    wout = wout.reshape(D, D)

    small = jnp.concatenate([c[0], b_merge[0].reshape(-1), conv_w[0].reshape(-1)])
    gathered = _allgather_rows("allgather_small", _pad8(small)).reshape(8, 8, -1)[:, 0, :]
    c_all = gathered[:, :D]
    per_chip = gathered[0::2]
    bm_full = jnp.concatenate([per_chip[k, D:D + 512].reshape(2, 256) for k in range(4)], axis=1)
    cw_full = jnp.concatenate([per_chip[k, D + 512:].reshape(3, 128) for k in range(4)], axis=1)
    b_l = lax.dynamic_slice_in_dim(b_ada, chip * ADA_SHARD, ADA_SHARD, axis=1)
    mod_l = _ada_fwd(c_all, w_ada[0], b_l)
    mod_g = _allgather_rows("allgather_mod", mod_l).reshape(8, 8, ADA_SHARD)
    mod_all = jnp.concatenate([mod_g[2 * k] for k in range(4)], axis=1)
    mod = lax.dynamic_slice_in_dim(mod_all, dev, 1, axis=0).reshape(3, 3, D)
    p1 = _pack(norm1_g[0], mod[0, 1], mod[0, 0], mod[0, 2])
    p2 = _pack(norm2_g[0], mod[1, 1], mod[1, 0], mod[1, 2])
    p3 = _pack(norm3_g[0], mod[2, 1], mod[2, 0], mod[2, 2])
    pf = _pack(final_g, final_g, final_g, final_g)

    u1 = _norm_mod_fwd("norm1_fwd", h0, p1)
    gu1, hm1 = _ffn_up("ffn1_up", u1, wgu1)
    f1, h1 = _proj_residual("ffn1_down", hm1, wd1, h0, p1, 0.5)
    u2 = _norm_mod_fwd("norm2_fwd", h1, p2)
    proj = _mm("mix_in", (4, S // TM), u2, pl.BlockSpec((TM, D), lambda s, i: (i, 0)),
               wmix, pl.BlockSpec((None, D, MIX_SHARD), lambda s, i: (s, 0, 0)), (1, 0),
               [_sds((S, MIX_W), BF16)], [pl.BlockSpec((TM, MIX_SHARD), lambda s, i: (i, s))], _store(BF16),
               semantics=("parallel", "parallel"))[0]
    sc = _conv_fwd(proj, cw_full)
    o, t_tot = _attn_fwd(proj)
    ya, yb, merged = _merge_fwd(sc, o, wco, wao, proj, bm_full)
    y2, h2 = _proj_residual("mix_out", merged, wout, h1, p2, 1.0)
    u3 = _norm_mod_fwd("norm3_fwd", h2, p3)
    gu3, hm3 = _ffn_up("ffn2_up", u3, wgu2)
    f3, h3 = _proj_residual("ffn2_down", hm3, wd2, h2, p3, 0.5)

    dh3, df3, sums_f, loss_blk = _final_loss_bwd(h3, pf, target, p3, f3)
    loss = lax.psum(loss_blk[0, 0], AXES)
    du3, dwgu2, dwd2 = _ffn_bwd("ffn2", df3, u3, gu3, hm3, wgu2, wd2)
    dh2, dy2, sums3 = _norm_mod_bwd("norm3_bwd", du3, h2, p3, dh3, prev=(p2, y2, 1.0))

    dya, dyb, dga, dgb, sums_bm = _merge_bwd(dy2, wout, ya, yb, proj, bm_full)
    dwout = _mm("dw_out", (1, S // TK), merged, pl.BlockSpec((TK, D), lambda n, k: (k, 0)),
                dy2, pl.BlockSpec((TK, D), lambda n, k: (k, 0)), (0, 0),
                [_sds((D, D), F32)], [pl.BlockSpec((D, D), lambda n, k: (0, 0))], _store(F32),
                nk=S // TK, acc_shape=(D, D))[0]
    dsc = _back_through_cols("conv_out_bwd", dya, wco, 512)
    do = _back_through_cols("attn_out_bwd", dyb, wao, 512)
    dwco = _grad_w("dw_conv_out", sc, 512, dya, 256, lambda s, k: (k, s), 4, (4, 512, 256), (None, 512, 256),
                   lambda s, k: (s, 0, 0))
    dwao = _grad_w("dw_attn_out", o, 512, dyb, 256, lambda s, k: (k, s), 4, (4, 512, 256), (None, 512, 256),
                   lambda s, k: (s, 0, 0))
    dcb, dcc, dcx, dcw = _conv_bwd(dsc, proj, cw_full)
    dq, dk, dv = _attn_bwd(proj, o, t_tot, do)
    dproj = jnp.concatenate([dcb, dcc, dcx, dq, dk, dv, dga, dgb], axis=1)
    du2 = _mm("mix_in_bwd", (S // TM, 4), dproj, pl.BlockSpec((TM, MIX_SHARD), lambda i, s: (i, s)),
              wmix, pl.BlockSpec((None, D, MIX_SHARD), lambda i, s: (s, 0, 0)), (1, 1),
              [_sds((S, D), F32)], [pl.BlockSpec((TM, D), lambda i, s: (i, 0))], _store(F32),
              nk=4, acc_shape=(TM, D), semantics=("parallel", "arbitrary"))[0]
    dwmix = _grad_w("dw_mix_in", u2, D, dproj, MIX_SHARD, lambda s, k: (k, s), 4, (4, D, MIX_SHARD),
                    (None, D, MIX_SHARD), lambda s, k: (s, 0, 0))
    dh1, df1, sums2 = _norm_mod_bwd("norm2_bwd", du2, h1, p2, dh2, prev=(p1, f1, 0.5))

    du1, dwgu1, dwd1 = _ffn_bwd("ffn1", df1, u1, gu1, hm1, wgu1, wd1)
    grad_x, sums1 = _norm_mod_bwd("norm1_bwd", du1, h0, p1, dh1)

    dmod = jnp.stack([sums1[0], sums1[1], sums2[3], sums2[0], sums2[1], sums3[3], sums3[0], sums3[1], sums_f[1]])
    small_g = jnp.concatenate([dmod.reshape(-1), sums1[2], sums2[2], sums3[2], sums_f[0],
                               sums_bm[0], sums_bm[1], dcw.reshape(-1)])
    all_g = _allgather_rows("allgather_small_grads", _pad8(small_g)).reshape(8, 8, -1)[:, 0, :]
    tot = _sum_rows("sum_small_grads", all_g)[0]
    g_b_ada = tot[:9 * D][None, :]
    g_n1, g_n2, g_n3 = (tot[(9 + k) * D:(10 + k) * D][None, :] for k in range(3))
    g_fin = tot[12 * D:13 * D]
    g_bm = lax.dynamic_slice_in_dim(tot[13 * D:15 * D].reshape(2, D), chip * 256, 256, axis=1)[None]
    g_cw = lax.dynamic_slice_in_dim(tot[15 * D:].reshape(3, 512), chip * 128, 128, axis=1)[None]
    dmod_l = lax.dynamic_slice_in_dim(all_g[:, :9 * D], chip * ADA_SHARD, ADA_SHARD, axis=1)
    g_w_ada = _ada_bwd(c_all.T, dmod_l)[None]

    grads = [dwgu1, dwd1.reshape(4, 704, D), dwmix, dwco, dwao, dwout.reshape(4, 256, D), dwgu2,
             dwd2.reshape(4, 704, D)]
    names = ["ffn1_w_gu", "ffn1_w_down", "w_mix_in", "w_conv_out", "w_attn_out", "w_out", "ffn2_w_gu", "ffn2_w_down"]
    g4 = [g.reshape(4, 2, g.shape[1] // 2, g.shape[2]) for g in grads]
    recv = _sibling_swap_halves(g4)
    c_idx = jnp.reshape(ci, (1,)).astype(jnp.int32)
    part = [_pair_sum("pair_sum_" + nm, a, b, c_idx) for nm, a, b in zip(names, g4, recv)]
    came = _chip_all_to_all(part)
    half = [_sum4("chip_sum_" + nm, q) for nm, q in zip(names, came)]
    full = [f.reshape(1, 2 * f.shape[1], f.shape[2]) for f in _sibling_share(half)]
    g_big = dict(zip(names, full))

    weights = dict(w_ada=w_ada, b_ada=b_ada, norm1_g=norm1_g, ffn1_w_gu=ffn1_w_gu, ffn1_w_down=ffn1_w_down,
                   norm2_g=norm2_g, w_mix_in=w_mix_in, b_merge=b_merge, conv_w=conv_w, w_conv_out=w_conv_out,
                   w_attn_out=w_attn_out, w_out=w_out, norm3_g=norm3_g, ffn2_w_gu=ffn2_w_gu,
                   ffn2_w_down=ffn2_w_down, final_g=final_g)
    ms = dict(w_ada=m_w_ada, b_ada=m_b_ada, norm1_g=m_norm1_g, ffn1_w_gu=m_ffn1_w_gu, ffn1_w_down=m_ffn1_w_down,
              norm2_g=m_norm2_g, w_mix_in=m_w_mix_in, b_merge=m_b_merge, conv_w=m_conv_w, w_conv_out=m_w_conv_out,
              w_attn_out=m_w_attn_out, w_out=m_w_out, norm3_g=m_norm3_g, ffn2_w_gu=m_ffn2_w_gu,
              ffn2_w_down=m_ffn2_w_down, final_g=m_final_g)
    vs = dict(w_ada=v_w_ada, b_ada=v_b_ada, norm1_g=v_norm1_g, ffn1_w_gu=v_ffn1_w_gu, ffn1_w_down=v_ffn1_w_down,
              norm2_g=v_norm2_g, w_mix_in=v_w_mix_in, b_merge=v_b_merge, conv_w=v_conv_w, w_conv_out=v_w_conv_out,
              w_attn_out=v_w_attn_out, w_out=v_w_out, norm3_g=v_norm3_g, ffn2_w_gu=v_ffn2_w_gu,
              ffn2_w_down=v_ffn2_w_down, final_g=v_final_g)
    order = list(weights)
    grad = dict(g_big)
    grad.update(w_ada=g_w_ada, b_ada=g_b_ada, norm1_g=g_n1, norm2_g=g_n2, norm3_g=g_n3, final_g=g_fin,
                b_merge=g_bm, conv_w=g_cw)
    delta, new_m, new_v = {}, {}, {}
    small_names = ["b_ada", "norm1_g", "norm2_g", "norm3_g", "final_g", "b_merge", "conv_w"]
    flat = lambda d: jnp.concatenate([d[nm].reshape(-1) for nm in small_names])[None, :]
    sd, sm, sv = _adamw("adamw_small", flat(weights), flat(grad), flat(ms), flat(vs))
    off = 0
    for nm in small_names:
        size = weights[nm].size
        for dst, src in ((delta, sd), (new_m, sm), (new_v, sv)):
            dst[nm] = src[0, off:off + size].reshape(weights[nm].shape)
        off += size
    for nm in order:
        if nm in small_names:
            continue
        shp = weights[nm].shape
        d2, m2, v2 = _adamw("adamw_" + nm, weights[nm][0], grad[nm][0], ms[nm][0], vs[nm][0])
        delta[nm], new_m[nm], new_v[nm] = d2.reshape(shp), m2.reshape(shp), v2.reshape(shp)

    return (loss, grad_x[None], *[grad[nm] for nm in order], *[delta[nm] for nm in order],
            *[new_m[nm] for nm in order], *[new_v[nm] for nm in order])
```

```python
import functools

import jax
import jax.numpy as jnp
from jax import lax
from jax.experimental import pallas as pl
from jax.experimental.pallas import tpu as pltpu

F32 = jnp.float32
BF16 = jnp.bfloat16
MESH = pl.DeviceIdType.MESH
AXES = ("x", "y", "c")

VMEM_LIMIT = 56 * 1024 * 1024
LANES = 128

D = 1024
D_FF = 2816
FF_SHARD = 1408
MIX_SHARD = 1280
MIX_W = 5120
HEAD_PAIRS = 4
HEAD_DIM = 64
CONV_W = 512
EPS = 1e-6
ATT_BLK = 256

ADAM_LR = 0.001
ADAM_B1 = 0.9
ADAM_B2 = 0.999
ADAM_EPS = 1e-08
ADAM_WD = 0.01
ADAM_STEP = 10


def _params(semantics=None):
    return pltpu.CompilerParams(dimension_semantics=semantics, vmem_limit_bytes=VMEM_LIMIT)


def _sigmoid(x):
    return 1.0 / (1.0 + jnp.exp(-x))


def _place():
    x, y, c = lax.axis_index("x"), lax.axis_index("y"), lax.axis_index("c")
    chips = [(1 - x, y), (x, 1 - y), (1 - x, 1 - y)]
    return x, y, c, chips


def _allgather_rows(name, blk):
    m_per, n = blk.shape

    def body(x_ref, out_ref, send_sems, recv_sems, local_sem):
        x, y, c, chips = _place()
        me, sibling = (x, y, c), (x, y, 1 - c)

        def rows(px, py, pc):
            return out_ref.at[pl.ds((4 * px + 2 * py + pc) * m_per, m_per), :]

        def copy(k, block, to, src=None):
            return pltpu.make_async_remote_copy(
                src_ref=rows(*block) if src is None else src, dst_ref=rows(*block),
                send_sem=send_sems.at[k], recv_sem=recv_sems.at[k], device_id=to, device_id_type=MESH)

        mine = pltpu.make_async_copy(x_ref, rows(*me), local_sem)
        mine.start()
        first = [copy(0, me, sibling, src=x_ref)]
        first += [copy(1 + j, me, (*chip, c), src=x_ref) for j, chip in enumerate(chips)]
        for cp in first:
            cp.start()
        passed = [copy(4 + j, (*chip, c), sibling) for j, chip in enumerate(chips)]
        for j, chip in enumerate(chips):
            copy(1 + j, (*chip, c), me).wait_recv()
            passed[j].start()
        copy(0, sibling, me).wait_recv()
        for j, chip in enumerate(chips):
            copy(4 + j, (*chip, 1 - c), me).wait_recv()
        for cp in first + passed:
            cp.wait_send()
        mine.wait()

    return pl.pallas_call(
        body, name=name,
        out_shape=jax.ShapeDtypeStruct((8 * m_per, n), blk.dtype),
        in_specs=[pl.BlockSpec(memory_space=pltpu.VMEM)],
        out_specs=pl.BlockSpec(memory_space=pltpu.VMEM),
        scratch_shapes=[pltpu.SemaphoreType.DMA((7,)), pltpu.SemaphoreType.DMA((7,)), pltpu.SemaphoreType.DMA],
        compiler_params=pltpu.CompilerParams(vmem_limit_bytes=VMEM_LIMIT),
    )(blk)


def _hbm_specs(n):
    return [pl.BlockSpec(memory_space=pltpu.HBM)] * n


def _allgather_weights(shards):
    n = len(shards)

    def body(*refs):
        ins, outs = refs[:n], refs[n:2 * n]
        send_sems, recv_sems = refs[2 * n:]
        x, y, c, chips = _place()
        sibling = (x, y, 1 - c)
        me_k = 2 * x + y

        def half(w, k, hc):
            h = ins[w].shape[0] // 2
            return outs[w].at[k, pl.ds(pl.multiple_of(hc * h, 8), h), :]

        def copy(w, j, k, hc, to, src=None):
            dst = half(w, k, hc)
            return pltpu.make_async_remote_copy(
                src_ref=dst if src is None else src, dst_ref=dst,
                send_sem=send_sems.at[6 * w + j], recv_sem=recv_sems.at[6 * w + j],
                device_id=to, device_id_type=MESH)

        first, passed = [], []
        for w in range(n):
            h = ins[w].shape[0] // 2
            src = ins[w].at[pl.ds(pl.multiple_of(c * h, 8), h), :]
            for j, chip in enumerate(chips):
                cp = copy(w, j, me_k, c, (*chip, c), src=src)
                cp.start()
                first.append(cp)
        for w in range(n):
            for j, (px, py) in enumerate(chips):
                copy(w, j, 2 * px + py, c, (x, y, c)).wait_recv()
                cp = copy(w, 3 + j, 2 * px + py, c, sibling)
                cp.start()
                passed.append(cp)
        for w in range(n):
            for j, (px, py) in enumerate(chips):
                copy(w, 3 + j, 2 * px + py, 1 - c, (x, y, c)).wait_recv()
        for cp in first + passed:
            cp.wait_send()

    gathered = pl.pallas_call(
        body, name="allgather_weights",
        out_shape=[jax.ShapeDtypeStruct((4, *s.shape), s.dtype) for s in shards],
        in_specs=_hbm_specs(n), out_specs=_hbm_specs(n),
        scratch_shapes=[pltpu.SemaphoreType.DMA((6 * n,)), pltpu.SemaphoreType.DMA((6 * n,))],
    )(*shards)
    chip = 2 * lax.axis_index("x") + lax.axis_index("y")
    return [lax.dynamic_update_slice(g, s[None], (chip, 0, 0)) for g, s in zip(gathered, shards)]


def _sibling_swap_halves(grads):
    n = len(grads)

    def body(*refs):
        ins, outs = refs[:n], refs[n:2 * n]
        send_sems, recv_sems = refs[2 * n:]
        x, y, c, _ = _place()
        cps = []
        for w in range(n):
            cp = pltpu.make_async_remote_copy(
                src_ref=ins[w].at[:, 1 - c], dst_ref=outs[w],
                send_sem=send_sems.at[w], recv_sem=recv_sems.at[w],
                device_id=(x, y, 1 - c), device_id_type=MESH)
            cp.start()
            cps.append(cp)
        for cp in cps:
            cp.wait()

    return pl.pallas_call(
        body, name="grad_sibling_swap",
        out_shape=[jax.ShapeDtypeStruct((4, *g.shape[2:]), g.dtype) for g in grads],
        in_specs=_hbm_specs(n), out_specs=_hbm_specs(n),
        scratch_shapes=[pltpu.SemaphoreType.DMA((n,)), pltpu.SemaphoreType.DMA((n,))],
    )(*grads)


def _chip_all_to_all(parts):
    n = len(parts)

    def body(*refs):
        ins, outs = refs[:n], refs[n:2 * n]
        send_sems, recv_sems = refs[2 * n:]
        x, y, c, chips = _place()
        me_k = 2 * x + y
        sent = []
        for w in range(n):
            for j, (px, py) in enumerate(chips):
                cp = pltpu.make_async_remote_copy(
                    src_ref=ins[w].at[2 * px + py], dst_ref=outs[w].at[me_k],
                    send_sem=send_sems.at[3 * w + j], recv_sem=recv_sems.at[3 * w + j],
                    device_id=(px, py, c), device_id_type=MESH)
                cp.start()
                sent.append(cp)
        for w in range(n):
            for j, (px, py) in enumerate(chips):
                slab = outs[w].at[2 * px + py]
                pltpu.make_async_remote_copy(
                    src_ref=slab, dst_ref=slab, send_sem=send_sems.at[3 * w + j],
                    recv_sem=recv_sems.at[3 * w + j], device_id=(px, py, c), device_id_type=MESH).wait_recv()
        for cp in sent:
            cp.wait_send()

    return pl.pallas_call(
        body, name="grad_chip_all_to_all",
        out_shape=[jax.ShapeDtypeStruct(p.shape, p.dtype) for p in parts],
        in_specs=_hbm_specs(n), out_specs=_hbm_specs(n),
        scratch_shapes=[pltpu.SemaphoreType.DMA((3 * n,)), pltpu.SemaphoreType.DMA((3 * n,))],
    )(*parts)


def _sibling_share(halves):
    n = len(halves)

    def body(*refs):
        ins, outs = refs[:n], refs[n:2 * n]
        send_sems, recv_sems = refs[2 * n:]
        x, y, c, _ = _place()
        cps = []
        for w in range(n):
            cp = pltpu.make_async_remote_copy(
                src_ref=ins[w], dst_ref=outs[w], send_sem=send_sems.at[w], recv_sem=recv_sems.at[w],
                device_id=(x, y, 1 - c), device_id_type=MESH)
            cp.start()
            cps.append(cp)
        for cp in cps:
            cp.wait()

    return pl.pallas_call(
        body, name="grad_sibling_share",
        out_shape=[jax.ShapeDtypeStruct(p.shape, p.dtype) for p in halves],
        in_specs=_hbm_specs(n), out_specs=_hbm_specs(n),
        scratch_shapes=[pltpu.SemaphoreType.DMA((n,)), pltpu.SemaphoreType.DMA((n,))],
    )(*halves)


def _mm(name, grid, a, a_spec, b, b_spec, contract, out_shapes, out_specs, epilogue,
        extras=(), extra_specs=(), nk=1, acc_shape=None, semantics=None):
    ne, no = len(extras), len(out_shapes)
    nd = len(grid)

    def body(*refs):
        a_ref, b_ref = refs[0], refs[1]
        ex, outs = refs[2:2 + ne], refs[2 + ne:2 + ne + no]

        def prod():
            return lax.dot_general(a_ref[...], b_ref[...], (((contract[0],), (contract[1],)), ((), ())),
                                   preferred_element_type=F32)

        if nk == 1:
            epilogue(prod(), ex, outs)
        else:
            acc = refs[-1]
            k = pl.program_id(nd - 1)

            @pl.when(k == 0)
            def _():
                acc[...] = prod()

            @pl.when(k > 0)
            def _():
                acc[...] += prod()

            @pl.when(k == nk - 1)
            def _():
                epilogue(acc[...], ex, outs)

    if semantics is None:
        semantics = ("arbitrary",) * nd
    return pl.pallas_call(
        body, name=name, grid=grid,
        in_specs=[a_spec, b_spec, *extra_specs], out_specs=list(out_specs), out_shape=list(out_shapes),
        scratch_shapes=[] if nk == 1 else [pltpu.VMEM(acc_shape, F32)],
        compiler_params=_params(semantics),
    )(a, b, *extras)


def _store(dtype):
    def epilogue(acc, ex, outs):
        outs[0][...] = acc.astype(dtype)
    return epilogue


def _sds(shape, dtype):
    return jax.ShapeDtypeStruct(shape, dtype)


TR = 512


def _row_spec(width, tr=TR):
    return pl.BlockSpec((tr, width), lambda i: (i, 0))


def _const_spec(shape):
    nd = len(shape)
    return pl.BlockSpec(shape, lambda i: (0,) * nd)


def _norm_mod_fwd(name, h, p):
    S = h.shape[0]

    def body(h_ref, p_ref, u_ref):
        hv = h_ref[...]
        r = lax.rsqrt(jnp.mean(hv * hv, axis=-1, keepdims=True) + EPS)
        nrm = (hv * r) * p_ref[0:1, :]
        u_ref[...] = (nrm * (1.0 + p_ref[1:2, :]) + p_ref[2:3, :]).astype(BF16)

    return pl.pallas_call(
        body, name=name, grid=(S // TR,),
        in_specs=[_row_spec(D), _const_spec((8, D))], out_specs=_row_spec(D),
        out_shape=_sds((S, D), BF16), compiler_params=_params(("parallel",)),
    )(h, p)


def _final_loss_bwd(h, gf, target, p3, f3):
    S = h.shape[0]

    def body(h_ref, g_ref, t_ref, p_ref, f_ref, dh_ref, df_ref, sums_ref, loss_ref):
        i = pl.program_id(0)

        @pl.when(i == 0)
        def _():
            sums_ref[...] = jnp.zeros_like(sums_ref)
            loss_ref[...] = jnp.zeros_like(loss_ref)

        hv = h_ref[...]
        g = g_ref[0:1, :]
        r = lax.rsqrt(jnp.mean(hv * hv, axis=-1, keepdims=True) + EPS)
        xn = hv * r
        err = xn * g - t_ref[...]
        loss_ref[...] += 0.5 * jnp.sum(err * err) * (1.0 / D)
        dout = err * (1.0 / D)
        dxn = dout * g
        dh = r * (dxn - xn * jnp.mean(dxn * xn, axis=-1, keepdims=True))
        dh_ref[...] = dh
        gate = p_ref[3:4, :]
        df_ref[...] = (0.5 * gate * dh).astype(BF16)
        sums_ref[0:1, :] += jnp.sum(dout * xn, axis=0, keepdims=True)
        sums_ref[1:2, :] += 0.5 * jnp.sum(dh * f_ref[...].astype(F32), axis=0, keepdims=True)

    return pl.pallas_call(
        body, name="final_loss_bwd", grid=(S // TR,),
        in_specs=[_row_spec(D), _const_spec((8, D)), _row_spec(D), _const_spec((8, D)), _row_spec(D)],
        out_specs=[_row_spec(D), _row_spec(D), _const_spec((8, D)), _const_spec((8, LANES))],
        out_shape=[_sds((S, D), F32), _sds((S, D), BF16), _sds((8, D), F32), _sds((8, LANES), F32)],
        compiler_params=_params(("arbitrary",)),
    )(h, gf, target, p3, f3)


def _norm_mod_bwd(name, du, h, p, dh_res, prev=None):
    S = h.shape[0]
    has_prev = prev is not None

    def body(*refs):
        if has_prev:
            du_ref, h_ref, p_ref, r_ref, pp_ref, f_ref, dh_ref, df_ref, sums_ref = refs
        else:
            du_ref, h_ref, p_ref, r_ref, dh_ref, sums_ref = refs
        i = pl.program_id(0)

        @pl.when(i == 0)
        def _():
            sums_ref[...] = jnp.zeros_like(sums_ref)

        hv = h_ref[...]
        duv = du_ref[...]
        g = p_ref[0:1, :]
        one_scale = 1.0 + p_ref[1:2, :]
        r = lax.rsqrt(jnp.mean(hv * hv, axis=-1, keepdims=True) + EPS)
        xn = hv * r
        dn = duv * one_scale
        dxn = dn * g
        dh = r_ref[...] + r * (dxn - xn * jnp.mean(dxn * xn, axis=-1, keepdims=True))
        dh_ref[...] = dh
        sums_ref[0:1, :] += jnp.sum(duv, axis=0, keepdims=True)
        sums_ref[1:2, :] += jnp.sum(duv * (xn * g), axis=0, keepdims=True)
        sums_ref[2:3, :] += jnp.sum(dn * xn, axis=0, keepdims=True)
        if has_prev:
            wgt = prev[2]
            df_ref[...] = (wgt * pp_ref[3:4, :] * dh).astype(BF16)
            sums_ref[3:4, :] += wgt * jnp.sum(dh * f_ref[...].astype(F32), axis=0, keepdims=True)

    ins = [du, h, p, dh_res]
    in_specs = [_row_spec(D), _row_spec(D), _const_spec((8, D)), _row_spec(D)]
    out_specs = [_row_spec(D)]
    out_shape = [_sds((S, D), F32)]
    if has_prev:
        ins += [prev[0], prev[1]]
        in_specs += [_const_spec((8, D)), _row_spec(D)]
        out_specs.append(_row_spec(D))
        out_shape.append(_sds((S, D), BF16))
    out_specs.append(_const_spec((8, D)))
    out_shape.append(_sds((8, D), F32))
    return pl.pallas_call(
        body, name=name, grid=(S // TR,), in_specs=in_specs, out_specs=out_specs, out_shape=out_shape,
        compiler_params=_params(("arbitrary",)),
    )(*ins)


TM = 512


def _ffn_up(name, u, wgu4):
    S = u.shape[0]

    def body(u_ref, wg_ref, wu_ref, gu_ref, hm_ref):
        uv = u_ref[...]
        g = jnp.dot(uv, wg_ref[...], preferred_element_type=F32)
        up = jnp.dot(uv, wu_ref[...], preferred_element_type=F32)
        gu_ref[0] = g.astype(BF16)
        gu_ref[1] = up.astype(BF16)
        hm_ref[...] = (g * _sigmoid(g) * up).astype(BF16)

    return pl.pallas_call(
        body, name=name, grid=(2, S // TM),
        in_specs=[pl.BlockSpec((TM, D), lambda s, i: (i, 0)),
                  pl.BlockSpec((None, D, FF_SHARD), lambda s, i: (s, 0, 0)),
                  pl.BlockSpec((None, D, FF_SHARD), lambda s, i: (s + 2, 0, 0))],
        out_specs=[pl.BlockSpec((2, TM, FF_SHARD), lambda s, i: (0, i, s)),
                   pl.BlockSpec((TM, FF_SHARD), lambda s, i: (i, s))],
        out_shape=[_sds((2, S, D_FF), BF16), _sds((S, D_FF), BF16)],
        compiler_params=_params(("parallel", "parallel")),
    )(u, wgu4, wgu4)


def _proj_residual(name, a, w, h, p, weight):
    S, K = a.shape

    def epilogue(acc, ex, outs):
        h_ref, p_ref = ex
        outs[0][...] = acc.astype(BF16)
        outs[1][...] = h_ref[...] + weight * p_ref[3:4, :] * acc

    return _mm(
        name, (S // TM,), a, pl.BlockSpec((TM, K), lambda i: (i, 0)), w, pl.BlockSpec((K, D), lambda i: (0, 0)),
        (1, 0), [_sds((S, D), BF16), _sds((S, D), F32)], [_row_spec(D, TM), _row_spec(D, TM)], epilogue,
        extras=(h, p), extra_specs=(_row_spec(D, TM), _const_spec((8, D))), semantics=("parallel",))


def _ffn_down_bwd(name, df, wd, gu):
    S = df.shape[0]

    def epilogue(acc, ex, outs):
        g = ex[0][0].astype(F32)
        up = ex[0][1].astype(F32)
        sg = _sigmoid(g)
        outs[0][0] = (acc * up * (sg * (1.0 + g * (1.0 - sg)))).astype(BF16)
        outs[0][1] = (acc * g * sg).astype(BF16)

    gu_spec = pl.BlockSpec((2, TM, FF_SHARD), lambda n, i: (0, i, n))
    return _mm(
        name, (2, S // TM), df, pl.BlockSpec((TM, D), lambda n, i: (i, 0)),
        wd, pl.BlockSpec((FF_SHARD, D), lambda n, i: (n, 0)), (1, 1),
        [_sds((2, S, D_FF), BF16)], [gu_spec], epilogue, extras=(gu,), extra_specs=(gu_spec,),
        semantics=("parallel", "parallel"))[0]


TK = 512


def _grad_w(name, a, a_w, b, b_w, b_map, n_out, out_shape, out_block, out_map):
    S = a.shape[0]
    nk = S // TK
    return _mm(
        name, (n_out, nk), a, pl.BlockSpec((TK, a_w), lambda s, k: (k, 0)), b, pl.BlockSpec(
            (None, TK, b_w) if b.ndim == 3 else (TK, b_w), b_map), (0, 0),
        [_sds(out_shape, F32)], [pl.BlockSpec(out_block, out_map)], _store(F32), nk=nk, acc_shape=(a_w, b_w),
        semantics=("parallel", "arbitrary"))[0]


def _ffn_bwd(tag, df, u_in, gu, hm, wgu4, wd):
    S = df.shape[0]
    dgu = _ffn_down_bwd(tag + "_down_bwd", df, wd, gu)
    dwd = _mm(
        tag + "_dw_down", (2, S // TK), hm, pl.BlockSpec((TK, FF_SHARD), lambda m, k: (k, m)),
        df, pl.BlockSpec((TK, D), lambda m, k: (k, 0)), (0, 0),
        [_sds((D_FF, D), F32)], [pl.BlockSpec((FF_SHARD, D), lambda m, k: (m, 0))], _store(F32),
        nk=S // TK, acc_shape=(FF_SHARD, D), semantics=("parallel", "arbitrary"))[0]
    du = _mm(
        tag + "_up_bwd", (S // TM, 4), dgu, pl.BlockSpec((None, TM, FF_SHARD), lambda i, s: (s // 2, i, s % 2)),
        wgu4, pl.BlockSpec((None, D, FF_SHARD), lambda i, s: (s, 0, 0)), (1, 1),
        [_sds((S, D), F32)], [pl.BlockSpec((TM, D), lambda i, s: (i, 0))], _store(F32),
        nk=4, acc_shape=(TM, D), semantics=("parallel", "arbitrary"))[0]
    dwgu = _grad_w(tag + "_dw_gu", u_in, D, dgu, FF_SHARD, lambda s, k: (s // 2, k, s % 2), 4,
                   (4, D, FF_SHARD), (None, D, FF_SHARD), lambda s, k: (s, 0, 0))
    return du, dwgu, dwd


def _shift_down(v, k, row):
    return jnp.where(row >= k, pltpu.roll(v, k, axis=0), 0.0)


def _shift_up(v, k, row, S):
    return jnp.where(row < S - k, pltpu.roll(v, S - k, axis=0), 0.0)


def _conv_specs(S):
    cols = CONV_W // LANES
    return [pl.BlockSpec((S, LANES), functools.partial(lambda j, off: (0, off + j), off=o * cols))
            for o in range(3)]


def _conv_fwd(proj, conv_w):
    S = proj.shape[0]

    def body(cb_ref, cc_ref, cx_ref, w_ref, sc_ref):
        row = lax.broadcasted_iota(jnp.int32, (S, LANES), 0)
        v = cc_ref[...].astype(F32) * cx_ref[...].astype(F32)
        yv = w_ref[0:1, :] * _shift_down(v, 2, row) + w_ref[1:2, :] * _shift_down(v, 1, row) + w_ref[2:3, :] * v
        sc_ref[...] = (cb_ref[...].astype(F32) * yv).astype(BF16)

    return pl.pallas_call(
        body, name="conv_fwd", grid=(CONV_W // LANES,),
        in_specs=_conv_specs(S) + [pl.BlockSpec((3, LANES), lambda j: (0, j))],
        out_specs=pl.BlockSpec((S, LANES), lambda j: (0, j)), out_shape=_sds((S, CONV_W), BF16),
        compiler_params=_params(("parallel",)),
    )(proj, proj, proj, conv_w)


def _conv_bwd(dsc, proj, conv_w):
    S = proj.shape[0]

    def body(d_ref, cb_ref, cc_ref, cx_ref, w_ref, dcb_ref, dcc_ref, dcx_ref, dw_ref):
        row = lax.broadcasted_iota(jnp.int32, (S, LANES), 0)
        cc = cc_ref[...].astype(F32)
        cx = cx_ref[...].astype(F32)
        d = d_ref[...].astype(F32)
        v = cc * cx
        v1 = _shift_down(v, 1, row)
        v2 = _shift_down(v, 2, row)
        w0, w1, w2 = w_ref[0:1, :], w_ref[1:2, :], w_ref[2:3, :]
        dcb_ref[...] = (d * (w0 * v2 + w1 * v1 + w2 * v)).astype(BF16)
        dy = d * cb_ref[...].astype(F32)
        dw_ref[0:1, :] = jnp.sum(dy * v2, axis=0, keepdims=True)
        dw_ref[1:2, :] = jnp.sum(dy * v1, axis=0, keepdims=True)
        dw_ref[2:3, :] = jnp.sum(dy * v, axis=0, keepdims=True)
        dv = w2 * dy + w1 * _shift_up(dy, 1, row, S) + w0 * _shift_up(dy, 2, row, S)
        dcc_ref[...] = (dv * cx).astype(BF16)
        dcx_ref[...] = (dv * cc).astype(BF16)

    col = pl.BlockSpec((S, LANES), lambda j: (0, j))
    return pl.pallas_call(
        body, name="conv_bwd", grid=(CONV_W // LANES,),
        in_specs=[col] + _conv_specs(S) + [pl.BlockSpec((3, LANES), lambda j: (0, j))],
        out_specs=[col, col, col, pl.BlockSpec((3, LANES), lambda j: (0, j))],
        out_shape=[_sds((S, CONV_W), BF16)] * 3 + [_sds((3, CONV_W), F32)],
        compiler_params=_params(("parallel",)),
    )(dsc, proj, proj, proj, conv_w)


Q_COL, K_COL, V_COL = 1536 // LANES, 2048 // LANES, 2560 // LANES


def _split_dot(x, tri):
    hi = x.astype(BF16)
    lo = (x - hi.astype(F32)).astype(BF16)
    return jnp.dot(hi, tri, preferred_element_type=F32) + jnp.dot(lo, tri, preferred_element_type=F32)


def _tri_dot(tri, x):
    hi = x.astype(BF16)
    lo = (x - hi.astype(F32)).astype(BF16)
    return jnp.dot(tri, hi, preferred_element_type=F32) + jnp.dot(tri, lo, preferred_element_type=F32)


def _softplus(z):
    return jnp.maximum(z, 0.0) + jnp.log(1.0 + jnp.exp(-jnp.abs(z)))


def _nt(a, b):
    return lax.dot_general(a, b, (((1,), (1,)), ((), ())), preferred_element_type=F32)


def _tn(a, b):
    return lax.dot_general(a, b, (((0,), (0,)), ((), ())), preferred_element_type=F32)


def _attn_fwd(proj):
    S = proj.shape[0]
    B = ATT_BLK

    def body(q_ref, k_ref, v_ref, o_ref, t_ref):
        i = pl.program_id(1)
        lo_lane = lax.broadcasted_iota(jnp.int32, (B, LANES), 1) < HEAD_DIM
        row = lax.broadcasted_iota(jnp.int32, (B, B), 0)
        col = lax.broadcasted_iota(jnp.int32, (B, B), 1)
        after = (row > col).astype(BF16)
        causal = jnp.concatenate([col < row, col < row], axis=0)
        q2 = q_ref[...] * 0.125
        zero = jnp.zeros((), BF16)
        q_st = jnp.concatenate([jnp.where(lo_lane, q2, zero), jnp.where(lo_lane, zero, q2)], axis=0)

        def tile(kb, carry, diag):
            r, acc = carry
            k2 = k_ref[pl.ds(pl.multiple_of(kb * B, B), B), :]
            v2 = v_ref[pl.ds(pl.multiple_of(kb * B, B), B), :]
            z = _nt(q_st, k2)
            spz = _softplus(z)
            sp = jnp.where(causal, spz, 0.0) if diag else spz
            rem = _split_dot(sp, after) + r
            a = jnp.exp(z - spz - rem)
            if diag:
                a = jnp.where(causal, a, 0.0)
            acc = acc + jnp.dot(a.astype(BF16), v2, preferred_element_type=F32)
            return r + jnp.sum(sp, axis=1, keepdims=True), acc

        init = (jnp.zeros((2 * B, 1), F32), jnp.zeros((2 * B, LANES), F32))
        carry = tile(i, init, True)
        carry = lax.fori_loop(0, i % 2, lambda j, cr: tile(i - 1, cr, False), carry)
        first = i - 1 - i % 2

        def pair(j, cr):
            return tile(first - 2 * j - 1, tile(first - 2 * j, cr, False), False)

        r, acc = lax.fori_loop(0, i // 2, pair, carry)
        o_ref[...] = jnp.where(lo_lane, acc[:B], acc[B:]).astype(BF16)
        t_ref[...] = jnp.where(lo_lane, r[:B], r[B:]).T

    seq = lambda off: pl.BlockSpec((S, LANES), lambda p, i: (0, off + p))
    blk = pl.BlockSpec((B, LANES), lambda p, i: (i, p))
    return pl.pallas_call(
        body, name="attn_fwd", grid=(HEAD_PAIRS, S // B),
        in_specs=[pl.BlockSpec((B, LANES), lambda p, i: (i, Q_COL + p)), seq(K_COL), seq(V_COL)],
        out_specs=[blk, pl.BlockSpec((LANES, B), lambda p, i: (p, i))],
        out_shape=[_sds((S, 512), BF16), _sds((512, S), F32)],
        compiler_params=_params(("parallel", "parallel")),
    )(proj, proj, proj)


def _attn_bwd(proj, t, do):
    S = proj.shape[0]
    kt = proj[:, K_COL * LANES:V_COL * LANES].T
    B = ATT_BLK
    nq = S // B

    def body(q_ref, k_ref, v_ref, kt_ref, t_ref, do_ref, dq_ref, dk_ref, dv_ref, dk_acc, dv_acc):
        i = pl.program_id(1)

        @pl.when(i == 0)
        def _():
            dk_acc[...] = jnp.zeros_like(dk_acc)
            dv_acc[...] = jnp.zeros_like(dv_acc)

        lo_lane = lax.broadcasted_iota(jnp.int32, (B, LANES), 1) < HEAD_DIM
        key = lax.broadcasted_iota(jnp.int32, (B, B), 0)
        qry = lax.broadcasted_iota(jnp.int32, (B, B), 1)
        upto = (qry <= key).astype(BF16)
        before = (qry < key).astype(BF16)
        causal = jnp.concatenate([key < qry, key < qry], axis=1)
        zero = jnp.zeros((), BF16)
        q2 = q_ref[...] * 0.125
        do2 = do_ref[...]
        q_st = jnp.concatenate([jnp.where(lo_lane, q2, zero), jnp.where(lo_lane, zero, q2)], axis=0)
        do_st = jnp.concatenate([jnp.where(lo_lane, do2, zero), jnp.where(lo_lane, zero, do2)], axis=0)
        t_st = jnp.concatenate([t_ref[0:1, :], t_ref[HEAD_DIM:HEAD_DIM + 1, :]], axis=1)

        def tile(kb, carry, diag):
            pc, ec, dqt = carry
            rows = pl.ds(pl.multiple_of(kb * B, B), B)
            k2 = k_ref[rows, :]
            v2 = v_ref[rows, :]
            z = _nt(k2, q_st)
            spz = _softplus(z)
            sp = jnp.where(causal, spz, 0.0) if diag else spz
            rem = (t_st - pc) - _tri_dot(upto, sp)
            a = jnp.exp(z - spz - rem)
            if diag:
                a = jnp.where(causal, a, 0.0)
            e = a * _nt(v2, do_st)
            e_before = ec + jnp.dot(before, e.astype(BF16), preferred_element_type=F32)
            u = jnp.exp(-spz)
            dz = u * (e + e_before) - e_before
            if diag:
                dz = jnp.where(causal, dz, 0.0)
            dzb = dz.astype(BF16)
            dk_acc[rows, :] += jnp.dot(dzb, q_st, preferred_element_type=F32)
            dv_acc[rows, :] += jnp.dot(a.astype(BF16), do_st, preferred_element_type=F32)
            kt = kt_ref[:, pl.ds(pl.multiple_of(kb * B, B), B)]
            return (pc + jnp.sum(sp, axis=0, keepdims=True), ec + jnp.sum(e, axis=0, keepdims=True),
                    dqt + jnp.dot(kt, dzb, preferred_element_type=F32))

        def pair(j, carry):
            return tile(2 * j + 1, tile(2 * j, carry, False), False)

        zc = jnp.zeros((1, 2 * B), F32)
        carry = lax.fori_loop(0, i // 2, pair, (zc, zc, jnp.zeros((LANES, 2 * B), F32)))
        carry = lax.fori_loop(0, i % 2, lambda j, cr: tile(i - 1, cr, False), carry)
        dqt = tile(i, carry, True)[2]
        head0 = lax.broadcasted_iota(jnp.int32, (LANES, B), 0) < HEAD_DIM
        dq_ref[...] = (jnp.where(head0, dqt[:, :B], dqt[:, B:]).T * 0.125).astype(BF16)

        @pl.when(i == nq - 1)
        def _():
            dk_ref[...] = dk_acc[...].astype(BF16)
            dv_ref[...] = dv_acc[...].astype(BF16)

    seq = lambda off: pl.BlockSpec((S, LANES), lambda p, i: (0, off + p))
    blk = pl.BlockSpec((B, LANES), lambda p, i: (i, p))
    whole = pl.BlockSpec((S, LANES), lambda p, i: (0, p))
    return pl.pallas_call(
        body, name="attn_bwd", grid=(HEAD_PAIRS, nq),
        in_specs=[pl.BlockSpec((B, LANES), lambda p, i: (i, Q_COL + p)), seq(K_COL), seq(V_COL),
                  pl.BlockSpec((LANES, S), lambda p, i: (p, 0)), pl.BlockSpec((LANES, B), lambda p, i: (p, i)), blk],
        out_specs=[blk, whole, whole], out_shape=[_sds((S, 512), BF16)] * 3,
        scratch_shapes=[pltpu.VMEM((S, LANES), F32), pltpu.VMEM((S, LANES), F32)],
        compiler_params=_params(("parallel", "arbitrary")),
    )(proj, proj, proj, kt, t, do)


GA_COL, GB_COL = 3072 // 256, 4096 // 256


def _merge_fwd(sc, o, wco4, wao4, proj, bm):
    S = sc.shape[0]

    def body(sc_ref, o_ref, wc_ref, wa_ref, ga_ref, gb_ref, bm_ref, ya_ref, yb_ref, mg_ref):
        ya = jnp.dot(sc_ref[...], wc_ref[...], preferred_element_type=F32)
        yb = jnp.dot(o_ref[...], wa_ref[...], preferred_element_type=F32)
        sa = _sigmoid(ga_ref[...].astype(F32) + bm_ref[0:1, :])
        sb = _sigmoid(gb_ref[...].astype(F32) + bm_ref[1:2, :])
        ya_ref[...] = ya.astype(BF16)
        yb_ref[...] = yb.astype(BF16)
        mg_ref[...] = (sa * ya + sb * yb).astype(BF16)

    wide = pl.BlockSpec((TM, 512), lambda n, i: (i, 0))
    wsp = pl.BlockSpec((None, 512, 256), lambda n, i: (n, 0, 0))
    out = pl.BlockSpec((TM, 256), lambda n, i: (i, n))
    return pl.pallas_call(
        body, name="merge_fwd", grid=(4, S // TM),
        in_specs=[wide, wide, wsp, wsp, pl.BlockSpec((TM, 256), lambda n, i: (i, GA_COL + n)),
                  pl.BlockSpec((TM, 256), lambda n, i: (i, GB_COL + n)), pl.BlockSpec((2, 256), lambda n, i: (0, n))],
        out_specs=[out, out, out], out_shape=[_sds((S, D), BF16)] * 3,
        compiler_params=_params(("parallel", "parallel")),
    )(sc, o, wco4, wao4, proj, proj, bm)


def _merge_bwd(dy2, wout, ya, yb, proj, bm):
    S = dy2.shape[0]

    def epilogue(acc, ex, outs):
        ya_ref, yb_ref, ga_ref, gb_ref, bm_ref = ex
        i = pl.program_id(1)
        sa = _sigmoid(ga_ref[...].astype(F32) + bm_ref[0:1, :])
        sb = _sigmoid(gb_ref[...].astype(F32) + bm_ref[1:2, :])
        dga = acc * ya_ref[...].astype(F32) * (sa * (1.0 - sa))
        dgb = acc * yb_ref[...].astype(F32) * (sb * (1.0 - sb))
        outs[0][...] = (acc * sa).astype(BF16)
        outs[1][...] = (acc * sb).astype(BF16)
        outs[2][...] = dga.astype(BF16)
        outs[3][...] = dgb.astype(BF16)

        @pl.when(i == 0)
        def _():
            outs[4][...] = jnp.zeros_like(outs[4])

        outs[4][0:1, :] += jnp.sum(dga, axis=0, keepdims=True)
        outs[4][1:2, :] += jnp.sum(dgb, axis=0, keepdims=True)

    out = pl.BlockSpec((TM, 256), lambda n, i: (i, n))
    return _mm(
        "merge_bwd", (4, S // TM), dy2, pl.BlockSpec((TM, D), lambda n, i: (i, 0)),
        wout, pl.BlockSpec((256, D), lambda n, i: (n, 0)), (1, 1),
        [_sds((S, D), BF16)] * 4 + [_sds((8, D), F32)], [out, out, out, out, pl.BlockSpec((8, 256), lambda n, i: (0, n))],
        epilogue, extras=(ya, yb, proj, proj, bm),
        extra_specs=(out, out, pl.BlockSpec((TM, 256), lambda n, i: (i, GA_COL + n)),
                     pl.BlockSpec((TM, 256), lambda n, i: (i, GB_COL + n)), pl.BlockSpec((2, 256), lambda n, i: (0, n))),
        semantics=("parallel", "arbitrary"))


def _back_through_cols(name, dy, w4, width):
    S = dy.shape[0]
    return _mm(
        name, (S // TM, 4), dy, pl.BlockSpec((TM, 256), lambda i, s: (i, s)),
        w4, pl.BlockSpec((None, width, 256), lambda i, s: (s, 0, 0)), (1, 1),
        [_sds((S, width), BF16)], [pl.BlockSpec((TM, width), lambda i, s: (i, 0))], _store(BF16),
        nk=4, acc_shape=(TM, width), semantics=("parallel", "arbitrary"))[0]


ADA_SHARD = 2304
ADA_TN = 768


def _ada_fwd(c_all, w_ada_l, b_l):
    def body(c_ref, w_ref, b_ref, o_ref):
        cv = c_ref[...]
        ca = cv * _sigmoid(cv)
        o_ref[...] = jnp.dot(ca.astype(BF16), w_ref[...].astype(BF16), preferred_element_type=F32) + b_ref[...]

    return pl.pallas_call(
        body, name="ada_fwd", grid=(ADA_SHARD // ADA_TN,),
        in_specs=[pl.BlockSpec((8, D), lambda j: (0, 0)), pl.BlockSpec((D, ADA_TN), lambda j: (0, j)),
                  pl.BlockSpec((1, ADA_TN), lambda j: (0, j))],
        out_specs=pl.BlockSpec((8, ADA_TN), lambda j: (0, j)), out_shape=_sds((8, ADA_SHARD), F32),
        compiler_params=_params(("parallel",)),
    )(c_all, w_ada_l, b_l)


def _ada_bwd(c_all_t, dmod_l):
    def body(c_ref, d_ref, o_ref):
        cv = c_ref[...]
        ca = cv * _sigmoid(cv)
        o_ref[...] = jnp.dot(ca.astype(BF16).astype(F32), d_ref[...].astype(BF16).astype(F32),
                             preferred_element_type=F32, precision=lax.Precision.HIGHEST)

    return pl.pallas_call(
        body, name="ada_bwd", grid=(ADA_SHARD // ADA_TN,),
        in_specs=[pl.BlockSpec((D, 8), lambda j: (0, 0)), pl.BlockSpec((8, ADA_TN), lambda j: (0, j))],
        out_specs=pl.BlockSpec((D, ADA_TN), lambda j: (0, j)), out_shape=_sds((D, ADA_SHARD), F32),
        compiler_params=_params(("parallel",)),
    )(c_all_t, dmod_l)


def _sum_rows(name, x):
    n = x.shape[1]

    def body(x_ref, o_ref):
        s = x_ref[0:1, :]
        for d in range(1, 8):
            s = s + x_ref[d:d + 1, :]
        o_ref[...] = s

    return pl.pallas_call(
        body, name=name, in_specs=[pl.BlockSpec(memory_space=pltpu.VMEM)],
        out_specs=pl.BlockSpec(memory_space=pltpu.VMEM), out_shape=_sds((1, n), F32),
        compiler_params=pltpu.CompilerParams(vmem_limit_bytes=VMEM_LIMIT),
    )(x)


def _pair_sum(name, g4, recv, c_idx):
    _, _, h, C = g4.shape
    tr = h if h <= 512 else h // (h // 256) if h % 256 == 0 else h // 2

    def body(c_ref, g_ref, r_ref, o_ref):
        o_ref[...] = g_ref[...] + r_ref[...]

    grid_spec = pltpu.PrefetchScalarGridSpec(
        num_scalar_prefetch=1, grid=(4, h // tr),
        in_specs=[pl.BlockSpec((None, None, tr, C), lambda k, i, c: (k, c[0], i, 0)),
                  pl.BlockSpec((None, tr, C), lambda k, i, c: (k, i, 0))],
        out_specs=pl.BlockSpec((None, tr, C), lambda k, i, c: (k, i, 0)))
    return pl.pallas_call(
        body, name=name, grid_spec=grid_spec, out_shape=_sds((4, h, C), F32),
        compiler_params=_params(("parallel", "parallel")),
    )(c_idx, g4, recv)


def _sum4(name, q, p, chip_idx):
    _, h, C = q.shape
    tr = h if h <= 512 else h // (h // 256) if h % 256 == 0 else h // 2

    def body(k_ref, q_ref, p_ref, o_ref):
        me = k_ref[0]
        terms = [jnp.where(me == k, p_ref[...], q_ref[k]) for k in range(4)]
        o_ref[...] = ((terms[0] + terms[1]) + terms[2]) + terms[3]

    grid_spec = pltpu.PrefetchScalarGridSpec(
        num_scalar_prefetch=1, grid=(h // tr,),
        in_specs=[pl.BlockSpec((4, tr, C), lambda i, k: (0, i, 0)),
                  pl.BlockSpec((None, tr, C), lambda i, k: (k[0], i, 0))],
        out_specs=pl.BlockSpec((tr, C), lambda i, k: (i, 0)))
    return pl.pallas_call(
        body, name=name, grid_spec=grid_spec, out_shape=_sds((h, C), F32), compiler_params=_params(("parallel",)),
    )(chip_idx, q, p)


def _adamw(name, w, g, m, v):
    R, C = w.shape
    tr = R
    while tr * C * 4 > (1 << 20) and tr % 16 == 0:
        tr //= 2
    c1 = 1.0 - ADAM_B1 ** ADAM_STEP
    c2 = 1.0 - ADAM_B2 ** ADAM_STEP

    def body(w_ref, g_ref, m_ref, v_ref, d_ref, nm_ref, nv_ref):
        gv = g_ref[...]
        nm = ADAM_B1 * m_ref[...] + (1.0 - ADAM_B1) * gv
        nv = ADAM_B2 * v_ref[...] + (1.0 - ADAM_B2) * (gv * gv)
        nm_ref[...] = nm
        nv_ref[...] = nv
        d_ref[...] = -ADAM_LR * ((nm / c1) / (jnp.sqrt(nv / c2) + ADAM_EPS) + ADAM_WD * w_ref[...])

    spec = pl.BlockSpec((tr, C), lambda i: (i, 0))
    return pl.pallas_call(
        body, name=name, grid=(R // tr,), in_specs=[spec] * 4, out_specs=[spec] * 3,
        out_shape=[_sds((R, C), F32)] * 3, compiler_params=_params(("parallel",)),
    )(w, g, m, v)


def _adamw_halves(name, w, own, sib, m, v, c_idx):
    R, C = w.shape
    h = R // 2
    tr = h
    while tr * C * 4 > (1 << 20) and tr % 16 == 0:
        tr //= 2
    nb = h // tr
    c1 = 1.0 - ADAM_B1 ** ADAM_STEP
    c2 = 1.0 - ADAM_B2 ** ADAM_STEP

    def body(c_ref, w_ref, own_ref, sib_ref, m_ref, v_ref, g_ref, d_ref, nm_ref, nv_ref):
        mine = (pl.program_id(0) // nb) == c_ref[0]
        gv = jnp.where(mine, own_ref[...], sib_ref[...])
        nm = ADAM_B1 * m_ref[...] + (1.0 - ADAM_B1) * gv
        nv = ADAM_B2 * v_ref[...] + (1.0 - ADAM_B2) * (gv * gv)
        g_ref[...] = gv
        nm_ref[...] = nm
        nv_ref[...] = nv
        d_ref[...] = -ADAM_LR * ((nm / c1) / (jnp.sqrt(nv / c2) + ADAM_EPS) + ADAM_WD * w_ref[...])

    spec = pl.BlockSpec((tr, C), lambda i, c: (i, 0))
    half = pl.BlockSpec((tr, C), lambda i, c: (i % nb, 0))
    grid_spec = pltpu.PrefetchScalarGridSpec(
        num_scalar_prefetch=1, grid=(R // tr,), in_specs=[spec, half, half, spec, spec], out_specs=[spec] * 4)
    return pl.pallas_call(
        body, name=name, grid_spec=grid_spec, out_shape=[_sds((R, C), F32)] * 4,
        compiler_params=_params(("parallel",)),
    )(c_idx, w, own, sib, m, v)


def _pack(g, scale, shift, gate):
    rows = jnp.stack([g, scale, shift, gate]).astype(F32)
    return jnp.concatenate([rows, jnp.zeros((4, D), F32)], axis=0)


def _pad8(vec):
    return jnp.concatenate([vec[None, :], jnp.zeros((7, vec.shape[0]), vec.dtype)], axis=0)


def kernel(x, c, w_ada, b_ada, norm1_g, ffn1_w_gu, ffn1_w_down, norm2_g, w_mix_in, b_merge, conv_w, w_conv_out, w_attn_out, w_out, norm3_g, ffn2_w_gu, ffn2_w_down, final_g, loss_target, m_w_ada, m_b_ada, m_norm1_g, m_ffn1_w_gu, m_ffn1_w_down, m_norm2_g, m_w_mix_in, m_b_merge, m_conv_w, m_w_conv_out, m_w_attn_out, m_w_out, m_norm3_g, m_ffn2_w_gu, m_ffn2_w_down, m_final_g, v_w_ada, v_b_ada, v_norm1_g, v_ffn1_w_gu, v_ffn1_w_down, v_norm2_g, v_w_mix_in, v_b_merge, v_conv_w, v_w_conv_out, v_w_attn_out, v_w_out, v_norm3_g, v_ffn2_w_gu, v_ffn2_w_down, v_final_g):
    xi, yi, ci = lax.axis_index("x"), lax.axis_index("y"), lax.axis_index("c")
    chip = 2 * xi + yi
    dev = 4 * xi + 2 * yi + ci
    S = x.shape[1]
    h0 = x[0]
    target = loss_target[0]

    big = [ffn1_w_gu[0], ffn1_w_down[0], w_mix_in[0], w_conv_out[0], w_attn_out[0], w_out[0], ffn2_w_gu[0],
           ffn2_w_down[0]]
    wgu1, wd1, wmix, wco, wao, wout, wgu2, wd2 = _allgather_weights([w.astype(BF16) for w in big])
    wd1 = wd1.reshape(D_FF, D)
    wd2 = wd2.reshape(D_FF, D)
    wout = wout.reshape(D, D)

    small = jnp.concatenate([c[0], b_merge[0].reshape(-1), conv_w[0].reshape(-1)])
    gathered = _allgather_rows("allgather_small", _pad8(small)).reshape(8, 8, -1)[:, 0, :]
    c_all = gathered[:, :D]
    per_chip = gathered[0::2]
    bm_full = jnp.concatenate([per_chip[k, D:D + 512].reshape(2, 256) for k in range(4)], axis=1)
    cw_full = jnp.concatenate([per_chip[k, D + 512:].reshape(3, 128) for k in range(4)], axis=1)
    b_l = lax.dynamic_slice_in_dim(b_ada, chip * ADA_SHARD, ADA_SHARD, axis=1)
    mod_l = _ada_fwd(c_all, w_ada[0], b_l)
    mod_g = _allgather_rows("allgather_mod", mod_l).reshape(8, 8, ADA_SHARD)
    mod_all = jnp.concatenate([mod_g[2 * k] for k in range(4)], axis=1)
    mod = lax.dynamic_slice_in_dim(mod_all, dev, 1, axis=0).reshape(3, 3, D)
    p1 = _pack(norm1_g[0], mod[0, 1], mod[0, 0], mod[0, 2])
    p2 = _pack(norm2_g[0], mod[1, 1], mod[1, 0], mod[1, 2])
    p3 = _pack(norm3_g[0], mod[2, 1], mod[2, 0], mod[2, 2])
    pf = _pack(final_g, final_g, final_g, final_g)

    u1 = _norm_mod_fwd("norm1_fwd", h0, p1)
    gu1, hm1 = _ffn_up("ffn1_up", u1, wgu1)
    f1, h1 = _proj_residual("ffn1_down", hm1, wd1, h0, p1, 0.5)
    u2 = _norm_mod_fwd("norm2_fwd", h1, p2)
    proj = _mm("mix_in", (4, S // TM), u2, pl.BlockSpec((TM, D), lambda s, i: (i, 0)),
               wmix, pl.BlockSpec((None, D, MIX_SHARD), lambda s, i: (s, 0, 0)), (1, 0),
               [_sds((S, MIX_W), BF16)], [pl.BlockSpec((TM, MIX_SHARD), lambda s, i: (i, s))], _store(BF16),
               semantics=("parallel", "parallel"))[0]
    sc = _conv_fwd(proj, cw_full)
    o, t_tot = _attn_fwd(proj)
    ya, yb, merged = _merge_fwd(sc, o, wco, wao, proj, bm_full)
    y2, h2 = _proj_residual("mix_out", merged, wout, h1, p2, 1.0)
    u3 = _norm_mod_fwd("norm3_fwd", h2, p3)
    gu3, hm3 = _ffn_up("ffn2_up", u3, wgu2)
    f3, h3 = _proj_residual("ffn2_down", hm3, wd2, h2, p3, 0.5)

    dh3, df3, sums_f, loss_blk = _final_loss_bwd(h3, pf, target, p3, f3)
    loss = lax.psum(loss_blk[0, 0], AXES)
    du3, dwgu2, dwd2 = _ffn_bwd("ffn2", df3, u3, gu3, hm3, wgu2, wd2)
    dh2, dy2, sums3 = _norm_mod_bwd("norm3_bwd", du3, h2, p3, dh3, prev=(p2, y2, 1.0))

    dya, dyb, dga, dgb, sums_bm = _merge_bwd(dy2, wout, ya, yb, proj, bm_full)
    dwout = _mm("dw_out", (1, S // TK), merged, pl.BlockSpec((TK, D), lambda n, k: (k, 0)),
                dy2, pl.BlockSpec((TK, D), lambda n, k: (k, 0)), (0, 0),
                [_sds((D, D), F32)], [pl.BlockSpec((D, D), lambda n, k: (0, 0))], _store(F32),
                nk=S // TK, acc_shape=(D, D))[0]
    dsc = _back_through_cols("conv_out_bwd", dya, wco, 512)
    do = _back_through_cols("attn_out_bwd", dyb, wao, 512)
    dwco = _grad_w("dw_conv_out", sc, 512, dya, 256, lambda s, k: (k, s), 4, (4, 512, 256), (None, 512, 256),
                   lambda s, k: (s, 0, 0))
    dwao = _grad_w("dw_attn_out", o, 512, dyb, 256, lambda s, k: (k, s), 4, (4, 512, 256), (None, 512, 256),
                   lambda s, k: (s, 0, 0))
    dcb, dcc, dcx, dcw = _conv_bwd(dsc, proj, cw_full)
    dq, dk, dv = _attn_bwd(proj, t_tot, do)
    dproj = jnp.concatenate([dcb, dcc, dcx, dq, dk, dv, dga, dgb], axis=1)
    du2 = _mm("mix_in_bwd", (S // TM, 4), dproj, pl.BlockSpec((TM, MIX_SHARD), lambda i, s: (i, s)),
              wmix, pl.BlockSpec((None, D, MIX_SHARD), lambda i, s: (s, 0, 0)), (1, 1),
              [_sds((S, D), F32)], [pl.BlockSpec((TM, D), lambda i, s: (i, 0))], _store(F32),
              nk=4, acc_shape=(TM, D), semantics=("parallel", "arbitrary"))[0]
    dwmix = _grad_w("dw_mix_in", u2, D, dproj, MIX_SHARD, lambda s, k: (k, s), 4, (4, D, MIX_SHARD),
                    (None, D, MIX_SHARD), lambda s, k: (s, 0, 0))
    dh1, df1, sums2 = _norm_mod_bwd("norm2_bwd", du2, h1, p2, dh2, prev=(p1, f1, 0.5))

    du1, dwgu1, dwd1 = _ffn_bwd("ffn1", df1, u1, gu1, hm1, wgu1, wd1)
    grad_x, sums1 = _norm_mod_bwd("norm1_bwd", du1, h0, p1, dh1)

    dmod = jnp.stack([sums1[0], sums1[1], sums2[3], sums2[0], sums2[1], sums3[3], sums3[0], sums3[1], sums_f[1]])
    small_g = jnp.concatenate([dmod.reshape(-1), sums1[2], sums2[2], sums3[2], sums_f[0],
                               sums_bm[0], sums_bm[1], dcw.reshape(-1)])
    all_g = _allgather_rows("allgather_small_grads", _pad8(small_g)).reshape(8, 8, -1)[:, 0, :]
    tot = _sum_rows("sum_small_grads", all_g)[0]
    g_b_ada = tot[:9 * D][None, :]
    g_n1, g_n2, g_n3 = (tot[(9 + k) * D:(10 + k) * D][None, :] for k in range(3))
    g_fin = tot[12 * D:13 * D]
    g_bm = lax.dynamic_slice_in_dim(tot[13 * D:15 * D].reshape(2, D), chip * 256, 256, axis=1)[None]
    g_cw = lax.dynamic_slice_in_dim(tot[15 * D:].reshape(3, 512), chip * 128, 128, axis=1)[None]
    dmod_l = lax.dynamic_slice_in_dim(all_g[:, :9 * D], chip * ADA_SHARD, ADA_SHARD, axis=1)
    g_w_ada = _ada_bwd(c_all.T, dmod_l)[None]

    grads = [dwgu1, dwd1.reshape(4, 704, D), dwmix, dwco, dwao, dwout.reshape(4, 256, D), dwgu2,
             dwd2.reshape(4, 704, D)]
    names = ["ffn1_w_gu", "ffn1_w_down", "w_mix_in", "w_conv_out", "w_attn_out", "w_out", "ffn2_w_gu", "ffn2_w_down"]
    g4 = [g.reshape(4, 2, g.shape[1] // 2, g.shape[2]) for g in grads]
    recv = _sibling_swap_halves(g4)
    c_idx = jnp.reshape(ci, (1,)).astype(jnp.int32)
    part = [_pair_sum("pair_sum_" + nm, a, b, c_idx) for nm, a, b in zip(names, g4, recv)]
    came = _chip_all_to_all(part)
    chip_idx = jnp.reshape(chip, (1,)).astype(jnp.int32)
    half = [_sum4("chip_sum_" + nm, q, p, chip_idx) for nm, q, p in zip(names, came, part)]
    g_own = dict(zip(names, half))
    g_sib = dict(zip(names, _sibling_share(half)))

    weights = dict(w_ada=w_ada, b_ada=b_ada, norm1_g=norm1_g, ffn1_w_gu=ffn1_w_gu, ffn1_w_down=ffn1_w_down,
                   norm2_g=norm2_g, w_mix_in=w_mix_in, b_merge=b_merge, conv_w=conv_w, w_conv_out=w_conv_out,
                   w_attn_out=w_attn_out, w_out=w_out, norm3_g=norm3_g, ffn2_w_gu=ffn2_w_gu,
                   ffn2_w_down=ffn2_w_down, final_g=final_g)
    ms = dict(w_ada=m_w_ada, b_ada=m_b_ada, norm1_g=m_norm1_g, ffn1_w_gu=m_ffn1_w_gu, ffn1_w_down=m_ffn1_w_down,
              norm2_g=m_norm2_g, w_mix_in=m_w_mix_in, b_merge=m_b_merge, conv_w=m_conv_w, w_conv_out=m_w_conv_out,
              w_attn_out=m_w_attn_out, w_out=m_w_out, norm3_g=m_norm3_g, ffn2_w_gu=m_ffn2_w_gu,
              ffn2_w_down=m_ffn2_w_down, final_g=m_final_g)
    vs = dict(w_ada=v_w_ada, b_ada=v_b_ada, norm1_g=v_norm1_g, ffn1_w_gu=v_ffn1_w_gu, ffn1_w_down=v_ffn1_w_down,
              norm2_g=v_norm2_g, w_mix_in=v_w_mix_in, b_merge=v_b_merge, conv_w=v_conv_w, w_conv_out=v_w_conv_out,
              w_attn_out=v_w_attn_out, w_out=v_w_out, norm3_g=v_norm3_g, ffn2_w_gu=v_ffn2_w_gu,
              ffn2_w_down=v_ffn2_w_down, final_g=v_final_g)
    order = list(weights)
    grad = dict(w_ada=g_w_ada, b_ada=g_b_ada, norm1_g=g_n1, norm2_g=g_n2, norm3_g=g_n3, final_g=g_fin,
                b_merge=g_bm, conv_w=g_cw)
    delta, new_m, new_v = {}, {}, {}
    small_names = ["b_ada", "norm1_g", "norm2_g", "norm3_g", "final_g", "b_merge", "conv_w"]
    flat = lambda d: jnp.concatenate([d[nm].reshape(-1) for nm in small_names])[None, :]
    sd, sm, sv = _adamw("adamw_small", flat(weights), flat(grad), flat(ms), flat(vs))
    off = 0
    for nm in small_names:
        size = weights[nm].size
        for dst, src in ((delta, sd), (new_m, sm), (new_v, sv)):
            dst[nm] = src[0, off:off + size].reshape(weights[nm].shape)
        off += size
    for nm in order:
        if nm in small_names:
            continue
        shp = weights[nm].shape
        if nm in g_own:
            g2, d2, m2, v2 = _adamw_halves("adamw_" + nm, weights[nm][0], g_own[nm], g_sib[nm], ms[nm][0], vs[nm][0],
                                           c_idx)
            grad[nm] = g2.reshape(shp)
        else:
            d2, m2, v2 = _adamw("adamw_" + nm, weights[nm][0], grad[nm][0], ms[nm][0], vs[nm][0])
        delta[nm], new_m[nm], new_v[nm] = d2.reshape(shp), m2.reshape(shp), v2.reshape(shp)

    return (loss, grad_x[None], *[grad[nm] for nm in order], *[delta[nm] for nm in order],
            *[new_m[nm] for nm in order], *[new_v[nm] for nm in order])
```

```python
import functools

import jax
import jax.numpy as jnp
from jax import lax
from jax.experimental import pallas as pl
from jax.experimental.pallas import tpu as pltpu

F32 = jnp.float32
BF16 = jnp.bfloat16
MESH = pl.DeviceIdType.MESH
AXES = ("x", "y", "c")

VMEM_LIMIT = 56 * 1024 * 1024
LANES = 128

D = 1024
D_FF = 2816
FF_SHARD = 1408
MIX_SHARD = 1280
MIX_W = 5120
HEAD_PAIRS = 4
HEAD_DIM = 64
CONV_W = 512
EPS = 1e-6
ATT_BLK = 256

ADAM_LR = 0.001
ADAM_B1 = 0.9
ADAM_B2 = 0.999
ADAM_EPS = 1e-08
ADAM_WD = 0.01
ADAM_STEP = 10


def _params(semantics=None):
    return pltpu.CompilerParams(dimension_semantics=semantics, vmem_limit_bytes=VMEM_LIMIT)


def _sigmoid(x):
    return 1.0 / (1.0 + jnp.exp(-x))


def _place():
    x, y, c = lax.axis_index("x"), lax.axis_index("y"), lax.axis_index("c")
    chips = [(1 - x, y), (x, 1 - y), (1 - x, 1 - y)]
    return x, y, c, chips


def _allgather_rows(name, blk):
    m_per, n = blk.shape

    def body(x_ref, out_ref, send_sems, recv_sems, local_sem):
        x, y, c, chips = _place()
        me, sibling = (x, y, c), (x, y, 1 - c)

        def rows(px, py, pc):
            return out_ref.at[pl.ds((4 * px + 2 * py + pc) * m_per, m_per), :]

        def copy(k, block, to, src=None):
            return pltpu.make_async_remote_copy(
                src_ref=rows(*block) if src is None else src, dst_ref=rows(*block),
                send_sem=send_sems.at[k], recv_sem=recv_sems.at[k], device_id=to, device_id_type=MESH)

        mine = pltpu.make_async_copy(x_ref, rows(*me), local_sem)
        mine.start()
        first = [copy(0, me, sibling, src=x_ref)]
        first += [copy(1 + j, me, (*chip, c), src=x_ref) for j, chip in enumerate(chips)]
        for cp in first:
            cp.start()
        passed = [copy(4 + j, (*chip, c), sibling) for j, chip in enumerate(chips)]
        for j, chip in enumerate(chips):
            copy(1 + j, (*chip, c), me).wait_recv()
            passed[j].start()
        copy(0, sibling, me).wait_recv()
        for j, chip in enumerate(chips):
            copy(4 + j, (*chip, 1 - c), me).wait_recv()
        for cp in first + passed:
            cp.wait_send()
        mine.wait()

    return pl.pallas_call(
        body, name=name,
        out_shape=jax.ShapeDtypeStruct((8 * m_per, n), blk.dtype),
        in_specs=[pl.BlockSpec(memory_space=pltpu.VMEM)],
        out_specs=pl.BlockSpec(memory_space=pltpu.VMEM),
        scratch_shapes=[pltpu.SemaphoreType.DMA((7,)), pltpu.SemaphoreType.DMA((7,)), pltpu.SemaphoreType.DMA],
        compiler_params=pltpu.CompilerParams(vmem_limit_bytes=VMEM_LIMIT),
    )(blk)


def _hbm_specs(n):
    return [pl.BlockSpec(memory_space=pltpu.HBM)] * n


def _allgather_weights(shards):
    n = len(shards)

    def body(*refs):
        start, relay, finish = _gather_protocol(refs[:n], refs[n:2 * n], *refs[2 * n:])
        start()
        relay()
        finish()

    gathered = pl.pallas_call(
        body, name="allgather_weights",
        out_shape=_gather_shapes(shards), in_specs=_hbm_specs(n), out_specs=_hbm_specs(n),
        scratch_shapes=_gather_sems(n),
    )(*shards)
    return _with_own_shard(gathered, shards)


def _gather_shapes(shards):
    return [jax.ShapeDtypeStruct((4, *s.shape), s.dtype) for s in shards]


def _gather_sems(n):
    return [pltpu.SemaphoreType.DMA((6 * n,)), pltpu.SemaphoreType.DMA((6 * n,))]


def _with_own_shard(gathered, shards):
    chip = 2 * lax.axis_index("x") + lax.axis_index("y")
    return [lax.dynamic_update_slice(g, s[None], (chip, 0, 0)) for g, s in zip(gathered, shards)]


def _gather_protocol(ins, outs, send_sems, recv_sems):
    n = len(ins)
    x, y, c, chips = _place()
    me, sibling = (x, y, c), (x, y, 1 - c)
    me_k = 2 * x + y

    def half(w, k, hc):
        h = ins[w].shape[0] // 2
        return outs[w].at[k, pl.ds(pl.multiple_of(hc * h, 8), h), :]

    def copy(w, j, k, hc, to, src=None):
        dst = half(w, k, hc)
        return pltpu.make_async_remote_copy(
            src_ref=dst if src is None else src, dst_ref=dst,
            send_sem=send_sems.at[6 * w + j], recv_sem=recv_sems.at[6 * w + j],
            device_id=to, device_id_type=MESH)

    def first(w, j):
        h = ins[w].shape[0] // 2
        src = ins[w].at[pl.ds(pl.multiple_of(c * h, 8), h), :]
        return copy(w, j, me_k, c, (*chips[j], c), src=src)

    def passed(w, j):
        px, py = chips[j]
        return copy(w, 3 + j, 2 * px + py, c, sibling)

    pairs = [(w, j) for w in range(n) for j in range(3)]

    def start():
        for w, j in pairs:
            first(w, j).start()

    def relay():
        for w, j in pairs:
            px, py = chips[j]
            copy(w, j, 2 * px + py, c, me).wait_recv()
            passed(w, j).start()

    def finish():
        for w, j in pairs:
            px, py = chips[j]
            copy(w, 3 + j, 2 * px + py, 1 - c, me).wait_recv()
        for w, j in pairs:
            first(w, j).wait_send()
            passed(w, j).wait_send()

    return start, relay, finish


def _sibling_swap_halves(name, grads):
    n = len(grads)

    def body(*refs):
        ins, outs = refs[:n], refs[n:2 * n]
        send_sems, recv_sems = refs[2 * n:]
        x, y, c, _ = _place()
        cps = []
        for w in range(n):
            cp = pltpu.make_async_remote_copy(
                src_ref=ins[w].at[:, 1 - c], dst_ref=outs[w],
                send_sem=send_sems.at[w], recv_sem=recv_sems.at[w],
                device_id=(x, y, 1 - c), device_id_type=MESH)
            cp.start()
            cps.append(cp)
        for cp in cps:
            cp.wait()

    return pl.pallas_call(
        body, name=name,
        out_shape=[jax.ShapeDtypeStruct((4, *g.shape[2:]), g.dtype) for g in grads],
        in_specs=_hbm_specs(n), out_specs=_hbm_specs(n),
        scratch_shapes=[pltpu.SemaphoreType.DMA((n,)), pltpu.SemaphoreType.DMA((n,))],
    )(*grads)


def _chip_all_to_all(parts):
    n = len(parts)

    def body(*refs):
        start, finish = _all_to_all_protocol(refs[:n], refs[n:2 * n], *refs[2 * n:])
        start()
        finish()

    return pl.pallas_call(
        body, name="grad_chip_all_to_all",
        out_shape=[jax.ShapeDtypeStruct(p.shape, p.dtype) for p in parts],
        in_specs=_hbm_specs(n), out_specs=_hbm_specs(n), scratch_shapes=_all_to_all_sems(n),
    )(*parts)


def _all_to_all_sems(n):
    return [pltpu.SemaphoreType.DMA((3 * n,)), pltpu.SemaphoreType.DMA((3 * n,))]


def _all_to_all_protocol(ins, outs, send_sems, recv_sems):
    n = len(ins)
    x, y, c, chips = _place()
    me_k = 2 * x + y
    pairs = [(w, j) for w in range(n) for j in range(3)]

    def sent(w, j):
        px, py = chips[j]
        return pltpu.make_async_remote_copy(
            src_ref=ins[w].at[2 * px + py], dst_ref=outs[w].at[me_k],
            send_sem=send_sems.at[3 * w + j], recv_sem=recv_sems.at[3 * w + j],
            device_id=(px, py, c), device_id_type=MESH)

    def start():
        for w, j in pairs:
            sent(w, j).start()

    def finish():
        for w, j in pairs:
            px, py = chips[j]
            slab = outs[w].at[2 * px + py]
            pltpu.make_async_remote_copy(
                src_ref=slab, dst_ref=slab, send_sem=send_sems.at[3 * w + j],
                recv_sem=recv_sems.at[3 * w + j], device_id=(px, py, c), device_id_type=MESH).wait_recv()
        for w, j in pairs:
            sent(w, j).wait_send()

    return start, finish


def _sibling_share(halves):
    n = len(halves)

    def body(*refs):
        ins, outs = refs[:n], refs[n:2 * n]
        send_sems, recv_sems = refs[2 * n:]
        x, y, c, _ = _place()
        cps = []
        for w in range(n):
            cp = pltpu.make_async_remote_copy(
                src_ref=ins[w], dst_ref=outs[w], send_sem=send_sems.at[w], recv_sem=recv_sems.at[w],
                device_id=(x, y, 1 - c), device_id_type=MESH)
            cp.start()
            cps.append(cp)
        for cp in cps:
            cp.wait()

    return pl.pallas_call(
        body, name="grad_sibling_share",
        out_shape=[jax.ShapeDtypeStruct(p.shape, p.dtype) for p in halves],
        in_specs=_hbm_specs(n), out_specs=_hbm_specs(n),
        scratch_shapes=[pltpu.SemaphoreType.DMA((n,)), pltpu.SemaphoreType.DMA((n,))],
    )(*halves)


def _mm(name, grid, a, a_spec, b, b_spec, contract, out_shapes, out_specs, epilogue,
        extras=(), extra_specs=(), nk=1, acc_shape=None, semantics=None):
    ne, no = len(extras), len(out_shapes)
    nd = len(grid)

    def body(*refs):
        a_ref, b_ref = refs[0], refs[1]
        ex, outs = refs[2:2 + ne], refs[2 + ne:2 + ne + no]

        def prod():
            return lax.dot_general(a_ref[...], b_ref[...], (((contract[0],), (contract[1],)), ((), ())),
                                   preferred_element_type=F32)

        if nk == 1:
            epilogue(prod(), ex, outs)
        else:
            acc = refs[-1]
            k = pl.program_id(nd - 1)

            @pl.when(k == 0)
            def _():
                acc[...] = prod()

            @pl.when(k > 0)
            def _():
                acc[...] += prod()

            @pl.when(k == nk - 1)
            def _():
                epilogue(acc[...], ex, outs)

    if semantics is None:
        semantics = ("arbitrary",) * nd
    return pl.pallas_call(
        body, name=name, grid=grid,
        in_specs=[a_spec, b_spec, *extra_specs], out_specs=list(out_specs), out_shape=list(out_shapes),
        scratch_shapes=[] if nk == 1 else [pltpu.VMEM(acc_shape, F32)],
        compiler_params=_params(semantics),
    )(a, b, *extras)


def _store(dtype):
    def epilogue(acc, ex, outs):
        outs[0][...] = acc.astype(dtype)
    return epilogue


def _sds(shape, dtype):
    return jax.ShapeDtypeStruct(shape, dtype)


TR = 512


def _row_spec(width, tr=TR):
    return pl.BlockSpec((tr, width), lambda i: (i, 0))


def _const_spec(shape):
    nd = len(shape)
    return pl.BlockSpec(shape, lambda i: (0,) * nd)


def _norm_mod_fwd(name, h, p):
    S = h.shape[0]

    def body(h_ref, p_ref, u_ref):
        hv = h_ref[...]
        r = lax.rsqrt(jnp.mean(hv * hv, axis=-1, keepdims=True) + EPS)
        nrm = (hv * r) * p_ref[0:1, :]
        u_ref[...] = (nrm * (1.0 + p_ref[1:2, :]) + p_ref[2:3, :]).astype(BF16)

    return pl.pallas_call(
        body, name=name, grid=(S // TR,),
        in_specs=[_row_spec(D), _const_spec((8, D))], out_specs=_row_spec(D),
        out_shape=_sds((S, D), BF16), compiler_params=_params(("parallel",)),
    )(h, p)


def _final_loss_bwd(h, gf, target, p3, f3):
    S = h.shape[0]

    def body(h_ref, g_ref, t_ref, p_ref, f_ref, dh_ref, df_ref, sums_ref, loss_ref):
        i = pl.program_id(0)

        @pl.when(i == 0)
        def _():
            sums_ref[...] = jnp.zeros_like(sums_ref)
            loss_ref[...] = jnp.zeros_like(loss_ref)

        hv = h_ref[...]
        g = g_ref[0:1, :]
        r = lax.rsqrt(jnp.mean(hv * hv, axis=-1, keepdims=True) + EPS)
        xn = hv * r
        err = xn * g - t_ref[...]
        loss_ref[...] += 0.5 * jnp.sum(err * err) * (1.0 / D)
        dout = err * (1.0 / D)
        dxn = dout * g
        dh = r * (dxn - xn * jnp.mean(dxn * xn, axis=-1, keepdims=True))
        dh_ref[...] = dh
        gate = p_ref[3:4, :]
        df_ref[...] = (0.5 * gate * dh).astype(BF16)
        sums_ref[0:1, :] += jnp.sum(dout * xn, axis=0, keepdims=True)
        sums_ref[1:2, :] += 0.5 * jnp.sum(dh * f_ref[...].astype(F32), axis=0, keepdims=True)

    return pl.pallas_call(
        body, name="final_loss_bwd", grid=(S // TR,),
        in_specs=[_row_spec(D), _const_spec((8, D)), _row_spec(D), _const_spec((8, D)), _row_spec(D)],
        out_specs=[_row_spec(D), _row_spec(D), _const_spec((8, D)), _const_spec((8, LANES))],
        out_shape=[_sds((S, D), F32), _sds((S, D), BF16), _sds((8, D), F32), _sds((8, LANES), F32)],
        compiler_params=_params(("arbitrary",)),
    )(h, gf, target, p3, f3)


def _norm_mod_bwd(name, du, h, p, dh_res, prev=None):
    S = h.shape[0]
    has_prev = prev is not None

    def body(*refs):
        if has_prev:
            du_ref, h_ref, p_ref, r_ref, pp_ref, f_ref, dh_ref, df_ref, sums_ref = refs
        else:
            du_ref, h_ref, p_ref, r_ref, dh_ref, sums_ref = refs
        i = pl.program_id(0)

        @pl.when(i == 0)
        def _():
            sums_ref[...] = jnp.zeros_like(sums_ref)

        hv = h_ref[...]
        duv = du_ref[...]
        g = p_ref[0:1, :]
        one_scale = 1.0 + p_ref[1:2, :]
        r = lax.rsqrt(jnp.mean(hv * hv, axis=-1, keepdims=True) + EPS)
        xn = hv * r
        dn = duv * one_scale
        dxn = dn * g
        dh = r_ref[...] + r * (dxn - xn * jnp.mean(dxn * xn, axis=-1, keepdims=True))
        dh_ref[...] = dh
        sums_ref[0:1, :] += jnp.sum(duv, axis=0, keepdims=True)
        sums_ref[1:2, :] += jnp.sum(duv * (xn * g), axis=0, keepdims=True)
        sums_ref[2:3, :] += jnp.sum(dn * xn, axis=0, keepdims=True)
        if has_prev:
            wgt = prev[2]
            df_ref[...] = (wgt * pp_ref[3:4, :] * dh).astype(BF16)
            sums_ref[3:4, :] += wgt * jnp.sum(dh * f_ref[...].astype(F32), axis=0, keepdims=True)

    ins = [du, h, p, dh_res]
    in_specs = [_row_spec(D), _row_spec(D), _const_spec((8, D)), _row_spec(D)]
    out_specs = [_row_spec(D)]
    out_shape = [_sds((S, D), F32)]
    if has_prev:
        ins += [prev[0], prev[1]]
        in_specs += [_const_spec((8, D)), _row_spec(D)]
        out_specs.append(_row_spec(D))
        out_shape.append(_sds((S, D), BF16))
    out_specs.append(_const_spec((8, D)))
    out_shape.append(_sds((8, D), F32))
    return pl.pallas_call(
        body, name=name, grid=(S // TR,), in_specs=in_specs, out_specs=out_specs, out_shape=out_shape,
        compiler_params=_params(("arbitrary",)),
    )(*ins)


TM = 512


def _ffn_up(name, u, wgu4):
    S = u.shape[0]

    def body(u_ref, wg_ref, wu_ref, gu_ref, hm_ref):
        uv = u_ref[...]
        g = jnp.dot(uv, wg_ref[...], preferred_element_type=F32)
        up = jnp.dot(uv, wu_ref[...], preferred_element_type=F32)
        gu_ref[0] = g.astype(BF16)
        gu_ref[1] = up.astype(BF16)
        hm_ref[...] = (g * _sigmoid(g) * up).astype(BF16)

    return pl.pallas_call(
        body, name=name, grid=(2, S // TM),
        in_specs=[pl.BlockSpec((TM, D), lambda s, i: (i, 0)),
                  pl.BlockSpec((None, D, FF_SHARD), lambda s, i: (s, 0, 0)),
                  pl.BlockSpec((None, D, FF_SHARD), lambda s, i: (s + 2, 0, 0))],
        out_specs=[pl.BlockSpec((2, TM, FF_SHARD), lambda s, i: (0, i, s)),
                   pl.BlockSpec((TM, FF_SHARD), lambda s, i: (i, s))],
        out_shape=[_sds((2, S, D_FF), BF16), _sds((S, D_FF), BF16)],
        compiler_params=_params(("parallel", "parallel")),
    )(u, wgu4, wgu4)


def _proj_residual(name, a, w, h, p, weight):
    S, K = a.shape

    def epilogue(acc, ex, outs):
        h_ref, p_ref = ex
        outs[0][...] = acc.astype(BF16)
        outs[1][...] = h_ref[...] + weight * p_ref[3:4, :] * acc

    return _mm(
        name, (S // TM,), a, pl.BlockSpec((TM, K), lambda i: (i, 0)), w, pl.BlockSpec((K, D), lambda i: (0, 0)),
        (1, 0), [_sds((S, D), BF16), _sds((S, D), F32)], [_row_spec(D, TM), _row_spec(D, TM)], epilogue,
        extras=(h, p), extra_specs=(_row_spec(D, TM), _const_spec((8, D))), semantics=("parallel",))


def _ffn_down_bwd(name, df, wd, gu):
    S = df.shape[0]

    def epilogue(acc, ex, outs):
        g = ex[0][0].astype(F32)
        up = ex[0][1].astype(F32)
        sg = _sigmoid(g)
        outs[0][0] = (acc * up * (sg * (1.0 + g * (1.0 - sg)))).astype(BF16)
        outs[0][1] = (acc * g * sg).astype(BF16)

    gu_spec = pl.BlockSpec((2, TM, FF_SHARD), lambda n, i: (0, i, n))
    return _mm(
        name, (2, S // TM), df, pl.BlockSpec((TM, D), lambda n, i: (i, 0)),
        wd, pl.BlockSpec((FF_SHARD, D), lambda n, i: (n, 0)), (1, 1),
        [_sds((2, S, D_FF), BF16)], [gu_spec], epilogue, extras=(gu,), extra_specs=(gu_spec,),
        semantics=("parallel", "parallel"))[0]


TK = 512


def _grad_w(name, a, a_w, b, b_w, b_map, n_out, out_shape, out_block, out_map):
    S = a.shape[0]
    nk = S // TK
    return _mm(
        name, (n_out, nk), a, pl.BlockSpec((TK, a_w), lambda s, k: (k, 0)), b, pl.BlockSpec(
            (None, TK, b_w) if b.ndim == 3 else (TK, b_w), b_map), (0, 0),
        [_sds(out_shape, F32)], [pl.BlockSpec(out_block, out_map)], _store(F32), nk=nk, acc_shape=(a_w, b_w),
        semantics=("parallel", "arbitrary"))[0]


def _ffn_bwd(tag, df, u_in, gu, hm, wgu4, wd):
    S = df.shape[0]
    dgu = _ffn_down_bwd(tag + "_down_bwd", df, wd, gu)
    dwd = _mm(
        tag + "_dw_down", (2, S // TK), hm, pl.BlockSpec((TK, FF_SHARD), lambda m, k: (k, m)),
        df, pl.BlockSpec((TK, D), lambda m, k: (k, 0)), (0, 0),
        [_sds((D_FF, D), F32)], [pl.BlockSpec((FF_SHARD, D), lambda m, k: (m, 0))], _store(F32),
        nk=S // TK, acc_shape=(FF_SHARD, D), semantics=("parallel", "arbitrary"))[0]
    du = _mm(
        tag + "_up_bwd", (S // TM, 4), dgu, pl.BlockSpec((None, TM, FF_SHARD), lambda i, s: (s // 2, i, s % 2)),
        wgu4, pl.BlockSpec((None, D, FF_SHARD), lambda i, s: (s, 0, 0)), (1, 1),
        [_sds((S, D), F32)], [pl.BlockSpec((TM, D), lambda i, s: (i, 0))], _store(F32),
        nk=4, acc_shape=(TM, D), semantics=("parallel", "arbitrary"))[0]
    dwgu = _grad_w(tag + "_dw_gu", u_in, D, dgu, FF_SHARD, lambda s, k: (s // 2, k, s % 2), 4,
                   (4, D, FF_SHARD), (None, D, FF_SHARD), lambda s, k: (s, 0, 0))
    return du, dwgu, dwd


def _shift_down(v, k, row):
    return jnp.where(row >= k, pltpu.roll(v, k, axis=0), 0.0)


def _shift_up(v, k, row, S):
    return jnp.where(row < S - k, pltpu.roll(v, S - k, axis=0), 0.0)


def _conv_specs(S):
    cols = CONV_W // LANES
    return [pl.BlockSpec((S, LANES), functools.partial(lambda j, off: (0, off + j), off=o * cols))
            for o in range(3)]


def _conv_fwd(proj, conv_w):
    S = proj.shape[0]

    def body(cb_ref, cc_ref, cx_ref, w_ref, sc_ref):
        row = lax.broadcasted_iota(jnp.int32, (S, LANES), 0)
        v = cc_ref[...].astype(F32) * cx_ref[...].astype(F32)
        yv = w_ref[0:1, :] * _shift_down(v, 2, row) + w_ref[1:2, :] * _shift_down(v, 1, row) + w_ref[2:3, :] * v
        sc_ref[...] = (cb_ref[...].astype(F32) * yv).astype(BF16)

    return pl.pallas_call(
        body, name="conv_fwd", grid=(CONV_W // LANES,),
        in_specs=_conv_specs(S) + [pl.BlockSpec((3, LANES), lambda j: (0, j))],
        out_specs=pl.BlockSpec((S, LANES), lambda j: (0, j)), out_shape=_sds((S, CONV_W), BF16),
        compiler_params=_params(("parallel",)),
    )(proj, proj, proj, conv_w)


def _conv_bwd(dsc, proj, conv_w):
    S = proj.shape[0]

    def body(d_ref, cb_ref, cc_ref, cx_ref, w_ref, dcb_ref, dcc_ref, dcx_ref, dw_ref):
        row = lax.broadcasted_iota(jnp.int32, (S, LANES), 0)
        cc = cc_ref[...].astype(F32)
        cx = cx_ref[...].astype(F32)
        d = d_ref[...].astype(F32)
        v = cc * cx
        v1 = _shift_down(v, 1, row)
        v2 = _shift_down(v, 2, row)
        w0, w1, w2 = w_ref[0:1, :], w_ref[1:2, :], w_ref[2:3, :]
        dcb_ref[...] = (d * (w0 * v2 + w1 * v1 + w2 * v)).astype(BF16)
        dy = d * cb_ref[...].astype(F32)
        dw_ref[0:1, :] = jnp.sum(dy * v2, axis=0, keepdims=True)
        dw_ref[1:2, :] = jnp.sum(dy * v1, axis=0, keepdims=True)
        dw_ref[2:3, :] = jnp.sum(dy * v, axis=0, keepdims=True)
        dv = w2 * dy + w1 * _shift_up(dy, 1, row, S) + w0 * _shift_up(dy, 2, row, S)
        dcc_ref[...] = (dv * cx).astype(BF16)
        dcx_ref[...] = (dv * cc).astype(BF16)

    col = pl.BlockSpec((S, LANES), lambda j: (0, j))
    return pl.pallas_call(
        body, name="conv_bwd", grid=(CONV_W // LANES,),
        in_specs=[col] + _conv_specs(S) + [pl.BlockSpec((3, LANES), lambda j: (0, j))],
        out_specs=[col, col, col, pl.BlockSpec((3, LANES), lambda j: (0, j))],
        out_shape=[_sds((S, CONV_W), BF16)] * 3 + [_sds((3, CONV_W), F32)],
        compiler_params=_params(("parallel",)),
    )(dsc, proj, proj, proj, conv_w)


Q_COL, K_COL, V_COL = 1536 // LANES, 2048 // LANES, 2560 // LANES


def _split_dot(x, tri):
    hi = x.astype(BF16)
    lo = (x - hi.astype(F32)).astype(BF16)
    return jnp.dot(hi, tri, preferred_element_type=F32) + jnp.dot(lo, tri, preferred_element_type=F32)


def _tri_dot(tri, x):
    hi = x.astype(BF16)
    lo = (x - hi.astype(F32)).astype(BF16)
    return jnp.dot(tri, hi, preferred_element_type=F32) + jnp.dot(tri, lo, preferred_element_type=F32)


def _softplus(z):
    return jnp.maximum(z, 0.0) + jnp.log(1.0 + jnp.exp(-jnp.abs(z)))


def _nt(a, b):
    return lax.dot_general(a, b, (((1,), (1,)), ((), ())), preferred_element_type=F32)


def _tn(a, b):
    return lax.dot_general(a, b, (((0,), (0,)), ((), ())), preferred_element_type=F32)


def _attn_fwd(proj, shards):
    S = proj.shape[0]
    B = ATT_BLK
    nq = S // B
    n = len(shards)

    def body(q_ref, k_ref, v_ref, *rest):
        o_ref, t_ref = rest[n:n + 2]
        start, relay, finish = _gather_protocol(rest[:n], rest[n + 2:2 * n + 2], *rest[2 * n + 2:])
        p = pl.program_id(0)
        i = pl.program_id(1)
        pl.when((p == 0) & (i == 0))(start)
        pl.when((p == HEAD_PAIRS // 2) & (i == 0))(relay)
        lo_lane = lax.broadcasted_iota(jnp.int32, (B, LANES), 1) < HEAD_DIM
        row = lax.broadcasted_iota(jnp.int32, (B, B), 0)
        col = lax.broadcasted_iota(jnp.int32, (B, B), 1)
        after = (row > col).astype(BF16)
        causal = jnp.concatenate([col < row, col < row], axis=0)
        q2 = q_ref[...] * 0.125
        zero = jnp.zeros((), BF16)
        q_st = jnp.concatenate([jnp.where(lo_lane, q2, zero), jnp.where(lo_lane, zero, q2)], axis=0)

        def tile(kb, carry, diag):
            r, acc = carry
            k2 = k_ref[pl.ds(pl.multiple_of(kb * B, B), B), :]
            v2 = v_ref[pl.ds(pl.multiple_of(kb * B, B), B), :]
            z = _nt(q_st, k2)
            spz = _softplus(z)
            sp = jnp.where(causal, spz, 0.0) if diag else spz
            rem = _split_dot(sp, after) + r
            a = jnp.exp(z - spz - rem)
            if diag:
                a = jnp.where(causal, a, 0.0)
            acc = acc + jnp.dot(a.astype(BF16), v2, preferred_element_type=F32)
            return r + jnp.sum(sp, axis=1, keepdims=True), acc

        init = (jnp.zeros((2 * B, 1), F32), jnp.zeros((2 * B, LANES), F32))
        carry = tile(i, init, True)
        carry = lax.fori_loop(0, i % 2, lambda j, cr: tile(i - 1, cr, False), carry)
        first = i - 1 - i % 2

        def pair(j, cr):
            return tile(first - 2 * j - 1, tile(first - 2 * j, cr, False), False)

        r, acc = lax.fori_loop(0, i // 2, pair, carry)
        o_ref[...] = jnp.where(lo_lane, acc[:B], acc[B:]).astype(BF16)
        t_ref[...] = jnp.where(lo_lane, r[:B], r[B:]).T
        pl.when((p == HEAD_PAIRS - 1) & (i == nq - 1))(finish)

    seq = lambda off: pl.BlockSpec((S, LANES), lambda p, i: (0, off + p))
    blk = pl.BlockSpec((B, LANES), lambda p, i: (i, p))
    o, t, *gathered = pl.pallas_call(
        body, name="attn_fwd", grid=(HEAD_PAIRS, nq),
        in_specs=[pl.BlockSpec((B, LANES), lambda p, i: (i, Q_COL + p)), seq(K_COL), seq(V_COL)] + _hbm_specs(n),
        out_specs=[blk, pl.BlockSpec((LANES, B), lambda p, i: (p, i))] + _hbm_specs(n),
        out_shape=[_sds((S, 512), BF16), _sds((512, S), F32)] + _gather_shapes(shards),
        scratch_shapes=_gather_sems(n),
        compiler_params=_params(("arbitrary", "arbitrary")),
    )(proj, proj, proj, *shards)
    return o, t, _with_own_shard(gathered, shards)


def _attn_bwd(proj, t, do, parts):
    S = proj.shape[0]
    kt = proj[:, K_COL * LANES:V_COL * LANES].T
    B = ATT_BLK
    nq = S // B
    n = len(parts)

    def body(q_ref, k_ref, v_ref, kt_ref, t_ref, do_ref, *rest):
        dq_ref, dk_ref, dv_ref = rest[n:n + 3]
        dk_acc, dv_acc = rest[2 * n + 3:2 * n + 5]
        start, finish = _all_to_all_protocol(rest[:n], rest[n + 3:2 * n + 3], *rest[2 * n + 5:])
        i = pl.program_id(1)
        pl.when((pl.program_id(0) == 0) & (i == 0))(start)

        @pl.when(i == 0)
        def _():
            dk_acc[...] = jnp.zeros_like(dk_acc)
            dv_acc[...] = jnp.zeros_like(dv_acc)

        lo_lane = lax.broadcasted_iota(jnp.int32, (B, LANES), 1) < HEAD_DIM
        key = lax.broadcasted_iota(jnp.int32, (B, B), 0)
        qry = lax.broadcasted_iota(jnp.int32, (B, B), 1)
        upto = (qry <= key).astype(BF16)
        before = (qry < key).astype(BF16)
        causal = jnp.concatenate([key < qry, key < qry], axis=1)
        zero = jnp.zeros((), BF16)
        q2 = q_ref[...] * 0.125
        do2 = do_ref[...]
        q_st = jnp.concatenate([jnp.where(lo_lane, q2, zero), jnp.where(lo_lane, zero, q2)], axis=0)
        do_st = jnp.concatenate([jnp.where(lo_lane, do2, zero), jnp.where(lo_lane, zero, do2)], axis=0)
        t_st = jnp.concatenate([t_ref[0:1, :], t_ref[HEAD_DIM:HEAD_DIM + 1, :]], axis=1)

        def tile(kb, carry, diag):
            pc, ec, dqt = carry
            rows = pl.ds(pl.multiple_of(kb * B, B), B)
            k2 = k_ref[rows, :]
            v2 = v_ref[rows, :]
            z = _nt(k2, q_st)
            spz = _softplus(z)
            sp = jnp.where(causal, spz, 0.0) if diag else spz
            rem = (t_st - pc) - _tri_dot(upto, sp)
            a = jnp.exp(z - spz - rem)
            if diag:
                a = jnp.where(causal, a, 0.0)
            e = a * _nt(v2, do_st)
            e_before = ec + jnp.dot(before, e.astype(BF16), preferred_element_type=F32)
            u = jnp.exp(-spz)
            dz = u * (e + e_before) - e_before
            if diag:
                dz = jnp.where(causal, dz, 0.0)
            dzb = dz.astype(BF16)
            dk_acc[rows, :] += jnp.dot(dzb, q_st, preferred_element_type=F32)
            dv_acc[rows, :] += jnp.dot(a.astype(BF16), do_st, preferred_element_type=F32)
            kt = kt_ref[:, pl.ds(pl.multiple_of(kb * B, B), B)]
            return (pc + jnp.sum(sp, axis=0, keepdims=True), ec + jnp.sum(e, axis=0, keepdims=True),
                    dqt + jnp.dot(kt, dzb, preferred_element_type=F32))

        def pair(j, carry):
            return tile(2 * j + 1, tile(2 * j, carry, False), False)

        zc = jnp.zeros((1, 2 * B), F32)
        carry = lax.fori_loop(0, i // 2, pair, (zc, zc, jnp.zeros((LANES, 2 * B), F32)))
        carry = lax.fori_loop(0, i % 2, lambda j, cr: tile(i - 1, cr, False), carry)
        dqt = tile(i, carry, True)[2]
        head0 = lax.broadcasted_iota(jnp.int32, (LANES, B), 0) < HEAD_DIM
        dq_ref[...] = (jnp.where(head0, dqt[:, :B], dqt[:, B:]).T * 0.125).astype(BF16)

        @pl.when(i == nq - 1)
        def _():
            dk_ref[...] = dk_acc[...].astype(BF16)
            dv_ref[...] = dv_acc[...].astype(BF16)

        pl.when((pl.program_id(0) == HEAD_PAIRS - 1) & (i == nq - 1))(finish)

    seq = lambda off: pl.BlockSpec((S, LANES), lambda p, i: (0, off + p))
    blk = pl.BlockSpec((B, LANES), lambda p, i: (i, p))
    whole = pl.BlockSpec((S, LANES), lambda p, i: (0, p))
    dq, dk, dv, *came = pl.pallas_call(
        body, name="attn_bwd", grid=(HEAD_PAIRS, nq),
        in_specs=[pl.BlockSpec((B, LANES), lambda p, i: (i, Q_COL + p)), seq(K_COL), seq(V_COL),
                  pl.BlockSpec((LANES, S), lambda p, i: (p, 0)), pl.BlockSpec((LANES, B), lambda p, i: (p, i)), blk]
        + _hbm_specs(n),
        out_specs=[blk, whole, whole] + _hbm_specs(n),
        out_shape=[_sds((S, 512), BF16)] * 3 + [jax.ShapeDtypeStruct(p.shape, p.dtype) for p in parts],
        scratch_shapes=[pltpu.VMEM((S, LANES), F32), pltpu.VMEM((S, LANES), F32)] + _all_to_all_sems(n),
        compiler_params=_params(("arbitrary", "arbitrary")),
    )(proj, proj, proj, kt, t, do, *parts)
    return dq, dk, dv, came


GA_COL, GB_COL = 3072 // 256, 4096 // 256


def _merge_fwd(sc, o, wco4, wao4, proj, bm):
    S = sc.shape[0]

    def body(sc_ref, o_ref, wc_ref, wa_ref, ga_ref, gb_ref, bm_ref, ya_ref, yb_ref, mg_ref):
        ya = jnp.dot(sc_ref[...], wc_ref[...], preferred_element_type=F32)
        yb = jnp.dot(o_ref[...], wa_ref[...], preferred_element_type=F32)
        sa = _sigmoid(ga_ref[...].astype(F32) + bm_ref[0:1, :])
        sb = _sigmoid(gb_ref[...].astype(F32) + bm_ref[1:2, :])
        ya_ref[...] = ya.astype(BF16)
        yb_ref[...] = yb.astype(BF16)
        mg_ref[...] = (sa * ya + sb * yb).astype(BF16)

    wide = pl.BlockSpec((TM, 512), lambda n, i: (i, 0))
    wsp = pl.BlockSpec((None, 512, 256), lambda n, i: (n, 0, 0))
    out = pl.BlockSpec((TM, 256), lambda n, i: (i, n))
    return pl.pallas_call(
        body, name="merge_fwd", grid=(4, S // TM),
        in_specs=[wide, wide, wsp, wsp, pl.BlockSpec((TM, 256), lambda n, i: (i, GA_COL + n)),
                  pl.BlockSpec((TM, 256), lambda n, i: (i, GB_COL + n)), pl.BlockSpec((2, 256), lambda n, i: (0, n))],
        out_specs=[out, out, out], out_shape=[_sds((S, D), BF16)] * 3,
        compiler_params=_params(("parallel", "parallel")),
    )(sc, o, wco4, wao4, proj, proj, bm)


def _merge_bwd(dy2, wout, ya, yb, proj, bm):
    S = dy2.shape[0]

    def epilogue(acc, ex, outs):
        ya_ref, yb_ref, ga_ref, gb_ref, bm_ref = ex
        i = pl.program_id(1)
        sa = _sigmoid(ga_ref[...].astype(F32) + bm_ref[0:1, :])
        sb = _sigmoid(gb_ref[...].astype(F32) + bm_ref[1:2, :])
        dga = acc * ya_ref[...].astype(F32) * (sa * (1.0 - sa))
        dgb = acc * yb_ref[...].astype(F32) * (sb * (1.0 - sb))
        outs[0][...] = (acc * sa).astype(BF16)
        outs[1][...] = (acc * sb).astype(BF16)
        outs[2][...] = dga.astype(BF16)
        outs[3][...] = dgb.astype(BF16)

        @pl.when(i == 0)
        def _():
            outs[4][...] = jnp.zeros_like(outs[4])

        outs[4][0:1, :] += jnp.sum(dga, axis=0, keepdims=True)
        outs[4][1:2, :] += jnp.sum(dgb, axis=0, keepdims=True)

    out = pl.BlockSpec((TM, 256), lambda n, i: (i, n))
    return _mm(
        "merge_bwd", (4, S // TM), dy2, pl.BlockSpec((TM, D), lambda n, i: (i, 0)),
        wout, pl.BlockSpec((256, D), lambda n, i: (n, 0)), (1, 1),
        [_sds((S, D), BF16)] * 4 + [_sds((8, D), F32)], [out, out, out, out, pl.BlockSpec((8, 256), lambda n, i: (0, n))],
        epilogue, extras=(ya, yb, proj, proj, bm),
        extra_specs=(out, out, pl.BlockSpec((TM, 256), lambda n, i: (i, GA_COL + n)),
                     pl.BlockSpec((TM, 256), lambda n, i: (i, GB_COL + n)), pl.BlockSpec((2, 256), lambda n, i: (0, n))),
        semantics=("parallel", "arbitrary"))


def _back_through_cols(name, dy, w4, width):
    S = dy.shape[0]
    return _mm(
        name, (S // TM, 4), dy, pl.BlockSpec((TM, 256), lambda i, s: (i, s)),
        w4, pl.BlockSpec((None, width, 256), lambda i, s: (s, 0, 0)), (1, 1),
        [_sds((S, width), BF16)], [pl.BlockSpec((TM, width), lambda i, s: (i, 0))], _store(BF16),
        nk=4, acc_shape=(TM, width), semantics=("parallel", "arbitrary"))[0]


ADA_SHARD = 2304
ADA_TN = 768


def _ada_fwd(c_all, w_ada_l, b_l):
    def body(c_ref, w_ref, b_ref, o_ref):
        cv = c_ref[...]
        ca = cv * _sigmoid(cv)
        o_ref[...] = jnp.dot(ca.astype(BF16), w_ref[...].astype(BF16), preferred_element_type=F32) + b_ref[...]

    return pl.pallas_call(
        body, name="ada_fwd", grid=(ADA_SHARD // ADA_TN,),
        in_specs=[pl.BlockSpec((8, D), lambda j: (0, 0)), pl.BlockSpec((D, ADA_TN), lambda j: (0, j)),
                  pl.BlockSpec((1, ADA_TN), lambda j: (0, j))],
        out_specs=pl.BlockSpec((8, ADA_TN), lambda j: (0, j)), out_shape=_sds((8, ADA_SHARD), F32),
        compiler_params=_params(("parallel",)),
    )(c_all, w_ada_l, b_l)


def _ada_bwd(c_all_t, dmod_l):
    def body(c_ref, d_ref, o_ref):
        cv = c_ref[...]
        ca = cv * _sigmoid(cv)
        o_ref[...] = jnp.dot(ca.astype(BF16).astype(F32), d_ref[...].astype(BF16).astype(F32),
                             preferred_element_type=F32, precision=lax.Precision.HIGHEST)

    return pl.pallas_call(
        body, name="ada_bwd", grid=(ADA_SHARD // ADA_TN,),
        in_specs=[pl.BlockSpec((D, 8), lambda j: (0, 0)), pl.BlockSpec((8, ADA_TN), lambda j: (0, j))],
        out_specs=pl.BlockSpec((D, ADA_TN), lambda j: (0, j)), out_shape=_sds((D, ADA_SHARD), F32),
        compiler_params=_params(("parallel",)),
    )(c_all_t, dmod_l)


def _sum_rows(name, x):
    n = x.shape[1]

    def body(x_ref, o_ref):
        s = x_ref[0:1, :]
        for d in range(1, 8):
            s = s + x_ref[d:d + 1, :]
        o_ref[...] = s

    return pl.pallas_call(
        body, name=name, in_specs=[pl.BlockSpec(memory_space=pltpu.VMEM)],
        out_specs=pl.BlockSpec(memory_space=pltpu.VMEM), out_shape=_sds((1, n), F32),
        compiler_params=pltpu.CompilerParams(vmem_limit_bytes=VMEM_LIMIT),
    )(x)


def _pair_sum(name, g4, recv, c_idx):
    _, _, h, C = g4.shape
    tr = h if h <= 512 else h // (h // 256) if h % 256 == 0 else h // 2

    def body(c_ref, g_ref, r_ref, o_ref):
        o_ref[...] = (g_ref[...] + r_ref[...]).astype(BF16)

    grid_spec = pltpu.PrefetchScalarGridSpec(
        num_scalar_prefetch=1, grid=(4, h // tr),
        in_specs=[pl.BlockSpec((None, None, tr, C), lambda k, i, c: (k, c[0], i, 0)),
                  pl.BlockSpec((None, tr, C), lambda k, i, c: (k, i, 0))],
        out_specs=pl.BlockSpec((None, tr, C), lambda k, i, c: (k, i, 0)))
    return pl.pallas_call(
        body, name=name, grid_spec=grid_spec, out_shape=_sds((4, h, C), BF16),
        compiler_params=_params(("parallel", "parallel")),
    )(c_idx, g4, recv)


def _sum4(name, q, p, chip_idx):
    _, h, C = q.shape
    tr = h if h <= 512 else h // (h // 256) if h % 256 == 0 else h // 2

    def body(k_ref, q_ref, p_ref, o_ref):
        me = k_ref[0]
        terms = [jnp.where(me == k, p_ref[...], q_ref[k]).astype(F32) for k in range(4)]
        o_ref[...] = ((terms[0] + terms[1]) + terms[2]) + terms[3]

    grid_spec = pltpu.PrefetchScalarGridSpec(
        num_scalar_prefetch=1, grid=(h // tr,),
        in_specs=[pl.BlockSpec((4, tr, C), lambda i, k: (0, i, 0)),
                  pl.BlockSpec((None, tr, C), lambda i, k: (k[0], i, 0))],
        out_specs=pl.BlockSpec((tr, C), lambda i, k: (i, 0)))
    return pl.pallas_call(
        body, name=name, grid_spec=grid_spec, out_shape=_sds((h, C), F32), compiler_params=_params(("parallel",)),
    )(chip_idx, q, p)


def _adamw(name, w, g, m, v):
    R, C = w.shape
    tr = R
    while tr * C * 4 > (1 << 20) and tr % 16 == 0:
        tr //= 2
    c1 = 1.0 - ADAM_B1 ** ADAM_STEP
    c2 = 1.0 - ADAM_B2 ** ADAM_STEP

    def body(w_ref, g_ref, m_ref, v_ref, d_ref, nm_ref, nv_ref):
        gv = g_ref[...]
        nm = ADAM_B1 * m_ref[...] + (1.0 - ADAM_B1) * gv
        nv = ADAM_B2 * v_ref[...] + (1.0 - ADAM_B2) * (gv * gv)
        nm_ref[...] = nm
        nv_ref[...] = nv
        d_ref[...] = -ADAM_LR * ((nm / c1) / (jnp.sqrt(nv / c2) + ADAM_EPS) + ADAM_WD * w_ref[...])

    spec = pl.BlockSpec((tr, C), lambda i: (i, 0))
    return pl.pallas_call(
        body, name=name, grid=(R // tr,), in_specs=[spec] * 4, out_specs=[spec] * 3,
        out_shape=[_sds((R, C), F32)] * 3, compiler_params=_params(("parallel",)),
    )(w, g, m, v)


def _adamw_halves(name, w, own, sib, m, v, c_idx):
    R, C = w.shape
    h = R // 2
    tr = h
    while tr * C * 4 > (1 << 20) and tr % 16 == 0:
        tr //= 2
    nb = h // tr
    c1 = 1.0 - ADAM_B1 ** ADAM_STEP
    c2 = 1.0 - ADAM_B2 ** ADAM_STEP

    def body(c_ref, w_ref, own_ref, sib_ref, m_ref, v_ref, g_ref, d_ref, nm_ref, nv_ref):
        mine = (pl.program_id(0) // nb) == c_ref[0]
        gv = jnp.where(mine, own_ref[...], sib_ref[...])
        nm = ADAM_B1 * m_ref[...] + (1.0 - ADAM_B1) * gv
        nv = ADAM_B2 * v_ref[...] + (1.0 - ADAM_B2) * (gv * gv)
        g_ref[...] = gv
        nm_ref[...] = nm
        nv_ref[...] = nv
        d_ref[...] = -ADAM_LR * ((nm / c1) / (jnp.sqrt(nv / c2) + ADAM_EPS) + ADAM_WD * w_ref[...])

    spec = pl.BlockSpec((tr, C), lambda i, c: (i, 0))
    half = pl.BlockSpec((tr, C), lambda i, c: (i % nb, 0))
    grid_spec = pltpu.PrefetchScalarGridSpec(
        num_scalar_prefetch=1, grid=(R // tr,), in_specs=[spec, half, half, spec, spec], out_specs=[spec] * 4)
    return pl.pallas_call(
        body, name=name, grid_spec=grid_spec, out_shape=[_sds((R, C), F32)] * 4,
        compiler_params=_params(("parallel",)),
    )(c_idx, w, own, sib, m, v)


def _pack(g, scale, shift, gate):
    rows = jnp.stack([g, scale, shift, gate]).astype(F32)
    return jnp.concatenate([rows, jnp.zeros((4, D), F32)], axis=0)


def _pad8(vec):
    return jnp.concatenate([vec[None, :], jnp.zeros((7, vec.shape[0]), vec.dtype)], axis=0)


def kernel(x, c, w_ada, b_ada, norm1_g, ffn1_w_gu, ffn1_w_down, norm2_g, w_mix_in, b_merge, conv_w, w_conv_out, w_attn_out, w_out, norm3_g, ffn2_w_gu, ffn2_w_down, final_g, loss_target, m_w_ada, m_b_ada, m_norm1_g, m_ffn1_w_gu, m_ffn1_w_down, m_norm2_g, m_w_mix_in, m_b_merge, m_conv_w, m_w_conv_out, m_w_attn_out, m_w_out, m_norm3_g, m_ffn2_w_gu, m_ffn2_w_down, m_final_g, v_w_ada, v_b_ada, v_norm1_g, v_ffn1_w_gu, v_ffn1_w_down, v_norm2_g, v_w_mix_in, v_b_merge, v_conv_w, v_w_conv_out, v_w_attn_out, v_w_out, v_norm3_g, v_ffn2_w_gu, v_ffn2_w_down, v_final_g):
    xi, yi, ci = lax.axis_index("x"), lax.axis_index("y"), lax.axis_index("c")
    chip = 2 * xi + yi
    dev = 4 * xi + 2 * yi + ci
    S = x.shape[1]
    h0 = x[0]
    target = loss_target[0]

    wgu1, wd1, wmix = _allgather_weights([w[0].astype(BF16) for w in (ffn1_w_gu, ffn1_w_down, w_mix_in)])
    wd1 = wd1.reshape(D_FF, D)
    late_shards = [w[0].astype(BF16) for w in (w_conv_out, w_attn_out, w_out, ffn2_w_gu, ffn2_w_down)]
    c_idx = jnp.reshape(ci, (1,)).astype(jnp.int32)
    chip_idx = jnp.reshape(chip, (1,)).astype(jnp.int32)

    def reduce_pairs(tag, names, grads):
        g4 = [g.reshape(4, 2, g.shape[1] // 2, g.shape[2]) for g in grads]
        recv = _sibling_swap_halves("grad_sibling_swap_" + tag, g4)
        return [_pair_sum("pair_sum_" + nm, a, b, c_idx) for nm, a, b in zip(names, g4, recv)]

    small = jnp.concatenate([c[0], b_merge[0].reshape(-1), conv_w[0].reshape(-1)])
    gathered = _allgather_rows("allgather_small", _pad8(small)).reshape(8, 8, -1)[:, 0, :]
    c_all = gathered[:, :D]
    per_chip = gathered[0::2]
    bm_full = jnp.concatenate([per_chip[k, D:D + 512].reshape(2, 256) for k in range(4)], axis=1)
    cw_full = jnp.concatenate([per_chip[k, D + 512:].reshape(3, 128) for k in range(4)], axis=1)
    b_l = lax.dynamic_slice_in_dim(b_ada, chip * ADA_SHARD, ADA_SHARD, axis=1)
    mod_l = _ada_fwd(c_all, w_ada[0], b_l)
    mod_g = _allgather_rows("allgather_mod", mod_l).reshape(8, 8, ADA_SHARD)
    mod_all = jnp.concatenate([mod_g[2 * k] for k in range(4)], axis=1)
    mod = lax.dynamic_slice_in_dim(mod_all, dev, 1, axis=0).reshape(3, 3, D)
    p1 = _pack(norm1_g[0], mod[0, 1], mod[0, 0], mod[0, 2])
    p2 = _pack(norm2_g[0], mod[1, 1], mod[1, 0], mod[1, 2])
    p3 = _pack(norm3_g[0], mod[2, 1], mod[2, 0], mod[2, 2])
    pf = _pack(final_g, final_g, final_g, final_g)

    u1 = _norm_mod_fwd("norm1_fwd", h0, p1)
    gu1, hm1 = _ffn_up("ffn1_up", u1, wgu1)
    f1, h1 = _proj_residual("ffn1_down", hm1, wd1, h0, p1, 0.5)
    u2 = _norm_mod_fwd("norm2_fwd", h1, p2)
    proj = _mm("mix_in", (4, S // TM), u2, pl.BlockSpec((TM, D), lambda s, i: (i, 0)),
               wmix, pl.BlockSpec((None, D, MIX_SHARD), lambda s, i: (s, 0, 0)), (1, 0),
               [_sds((S, MIX_W), BF16)], [pl.BlockSpec((TM, MIX_SHARD), lambda s, i: (i, s))], _store(BF16),
               semantics=("parallel", "parallel"))[0]
    sc = _conv_fwd(proj, cw_full)
    o, t_tot, (wco, wao, wout, wgu2, wd2) = _attn_fwd(proj, late_shards)
    wout = wout.reshape(D, D)
    wd2 = wd2.reshape(D_FF, D)
    ya, yb, merged = _merge_fwd(sc, o, wco, wao, proj, bm_full)
    y2, h2 = _proj_residual("mix_out", merged, wout, h1, p2, 1.0)
    u3 = _norm_mod_fwd("norm3_fwd", h2, p3)
    gu3, hm3 = _ffn_up("ffn2_up", u3, wgu2)
    f3, h3 = _proj_residual("ffn2_down", hm3, wd2, h2, p3, 0.5)

    dh3, df3, sums_f, loss_blk = _final_loss_bwd(h3, pf, target, p3, f3)
    loss = lax.psum(loss_blk[0, 0], AXES)
    du3, dwgu2, dwd2 = _ffn_bwd("ffn2", df3, u3, gu3, hm3, wgu2, wd2)
    dh2, dy2, sums3 = _norm_mod_bwd("norm3_bwd", du3, h2, p3, dh3, prev=(p2, y2, 1.0))

    dya, dyb, dga, dgb, sums_bm = _merge_bwd(dy2, wout, ya, yb, proj, bm_full)
    dwout = _mm("dw_out", (1, S // TK), merged, pl.BlockSpec((TK, D), lambda n, k: (k, 0)),
                dy2, pl.BlockSpec((TK, D), lambda n, k: (k, 0)), (0, 0),
                [_sds((D, D), F32)], [pl.BlockSpec((D, D), lambda n, k: (0, 0))], _store(F32),
                nk=S // TK, acc_shape=(D, D))[0]
    dsc = _back_through_cols("conv_out_bwd", dya, wco, 512)
    do = _back_through_cols("attn_out_bwd", dyb, wao, 512)
    dwco = _grad_w("dw_conv_out", sc, 512, dya, 256, lambda s, k: (k, s), 4, (4, 512, 256), (None, 512, 256),
                   lambda s, k: (s, 0, 0))
    dwao = _grad_w("dw_attn_out", o, 512, dyb, 256, lambda s, k: (k, s), 4, (4, 512, 256), (None, 512, 256),
                   lambda s, k: (s, 0, 0))
    dcb, dcc, dcx, dcw = _conv_bwd(dsc, proj, cw_full)
    names_e = ["ffn2_w_gu", "ffn2_w_down", "w_out", "w_conv_out", "w_attn_out"]
    part_e = reduce_pairs("early", names_e, [dwgu2, dwd2.reshape(4, 704, D), dwout.reshape(4, 256, D), dwco, dwao])
    dq, dk, dv, came_e = _attn_bwd(proj, t_tot, do, part_e)
    dproj = jnp.concatenate([dcb, dcc, dcx, dq, dk, dv, dga, dgb], axis=1)
    du2 = _mm("mix_in_bwd", (S // TM, 4), dproj, pl.BlockSpec((TM, MIX_SHARD), lambda i, s: (i, s)),
              wmix, pl.BlockSpec((None, D, MIX_SHARD), lambda i, s: (s, 0, 0)), (1, 1),
              [_sds((S, D), F32)], [pl.BlockSpec((TM, D), lambda i, s: (i, 0))], _store(F32),
              nk=4, acc_shape=(TM, D), semantics=("parallel", "arbitrary"))[0]
    dwmix = _grad_w("dw_mix_in", u2, D, dproj, MIX_SHARD, lambda s, k: (k, s), 4, (4, D, MIX_SHARD),
                    (None, D, MIX_SHARD), lambda s, k: (s, 0, 0))
    dh1, df1, sums2 = _norm_mod_bwd("norm2_bwd", du2, h1, p2, dh2, prev=(p1, f1, 0.5))

    du1, dwgu1, dwd1 = _ffn_bwd("ffn1", df1, u1, gu1, hm1, wgu1, wd1)
    grad_x, sums1 = _norm_mod_bwd("norm1_bwd", du1, h0, p1, dh1)

    dmod = jnp.stack([sums1[0], sums1[1], sums2[3], sums2[0], sums2[1], sums3[3], sums3[0], sums3[1], sums_f[1]])
    small_g = jnp.concatenate([dmod.reshape(-1), sums1[2], sums2[2], sums3[2], sums_f[0],
                               sums_bm[0], sums_bm[1], dcw.reshape(-1)])
    all_g = _allgather_rows("allgather_small_grads", _pad8(small_g)).reshape(8, 8, -1)[:, 0, :]
    tot = _sum_rows("sum_small_grads", all_g)[0]
    g_b_ada = tot[:9 * D][None, :]
    g_n1, g_n2, g_n3 = (tot[(9 + k) * D:(10 + k) * D][None, :] for k in range(3))
    g_fin = tot[12 * D:13 * D]
    g_bm = lax.dynamic_slice_in_dim(tot[13 * D:15 * D].reshape(2, D), chip * 256, 256, axis=1)[None]
    g_cw = lax.dynamic_slice_in_dim(tot[15 * D:].reshape(3, 512), chip * 128, 128, axis=1)[None]
    dmod_l = lax.dynamic_slice_in_dim(all_g[:, :9 * D], chip * ADA_SHARD, ADA_SHARD, axis=1)
    g_w_ada = _ada_bwd(c_all.T, dmod_l)[None]

    names_l = ["w_mix_in", "ffn1_w_gu", "ffn1_w_down"]
    part_l = reduce_pairs("late", names_l, [dwmix, dwgu1, dwd1.reshape(4, 704, D)])
    came_l = _chip_all_to_all(part_l)
    names = names_e + names_l
    half = [_sum4("chip_sum_" + nm, q, p, chip_idx)
            for nm, q, p in zip(names, list(came_e) + list(came_l), part_e + part_l)]
    g_own = dict(zip(names, half))
    g_sib = dict(zip(names, _sibling_share(half)))

    weights = dict(w_ada=w_ada, b_ada=b_ada, norm1_g=norm1_g, ffn1_w_gu=ffn1_w_gu, ffn1_w_down=ffn1_w_down,
                   norm2_g=norm2_g, w_mix_in=w_mix_in, b_merge=b_merge, conv_w=conv_w, w_conv_out=w_conv_out,
                   w_attn_out=w_attn_out, w_out=w_out, norm3_g=norm3_g, ffn2_w_gu=ffn2_w_gu,
                   ffn2_w_down=ffn2_w_down, final_g=final_g)
    ms = dict(w_ada=m_w_ada, b_ada=m_b_ada, norm1_g=m_norm1_g, ffn1_w_gu=m_ffn1_w_gu, ffn1_w_down=m_ffn1_w_down,
              norm2_g=m_norm2_g, w_mix_in=m_w_mix_in, b_merge=m_b_merge, conv_w=m_conv_w, w_conv_out=m_w_conv_out,
              w_attn_out=m_w_attn_out, w_out=m_w_out, norm3_g=m_norm3_g, ffn2_w_gu=m_ffn2_w_gu,
              ffn2_w_down=m_ffn2_w_down, final_g=m_final_g)
    vs = dict(w_ada=v_w_ada, b_ada=v_b_ada, norm1_g=v_norm1_g, ffn1_w_gu=v_ffn1_w_gu, ffn1_w_down=v_ffn1_w_down,
              norm2_g=v_norm2_g, w_mix_in=v_w_mix_in, b_merge=v_b_merge, conv_w=v_conv_w, w_conv_out=v_w_conv_out,
              w_attn_out=v_w_attn_out, w_out=v_w_out, norm3_g=v_norm3_g, ffn2_w_gu=v_ffn2_w_gu,
              ffn2_w_down=v_ffn2_w_down, final_g=v_final_g)
    order = list(weights)
    grad = dict(w_ada=g_w_ada, b_ada=g_b_ada, norm1_g=g_n1, norm2_g=g_n2, norm3_g=g_n3, final_g=g_fin,
                b_merge=g_bm, conv_w=g_cw)
    delta, new_m, new_v = {}, {}, {}
    small_names = ["b_ada", "norm1_g", "norm2_g", "norm3_g", "final_g", "b_merge", "conv_w"]
    flat = lambda d: jnp.concatenate([d[nm].reshape(-1) for nm in small_names])[None, :]
    sd, sm, sv = _adamw("adamw_small", flat(weights), flat(grad), flat(ms), flat(vs))
    off = 0
    for nm in small_names:
        size = weights[nm].size
        for dst, src in ((delta, sd), (new_m, sm), (new_v, sv)):
            dst[nm] = src[0, off:off + size].reshape(weights[nm].shape)
        off += size
    for nm in order:
        if nm in small_names:
            continue
        shp = weights[nm].shape
        if nm in g_own:
            g2, d2, m2, v2 = _adamw_halves("adamw_" + nm, weights[nm][0], g_own[nm], g_sib[nm], ms[nm][0], vs[nm][0],
                                           c_idx)
            grad[nm] = g2.reshape(shp)
        else:
            d2, m2, v2 = _adamw("adamw_" + nm, weights[nm][0], grad[nm][0], ms[nm][0], vs[nm][0])
        delta[nm], new_m[nm], new_v[nm] = d2.reshape(shp), m2.reshape(shp), v2.reshape(shp)

    return (loss, grad_x[None], *[grad[nm] for nm in order], *[delta[nm] for nm in order],
            *[new_m[nm] for nm in order], *[new_v[nm] for nm in order])
```

```python
import functools

import jax
import jax.numpy as jnp
from jax import lax
from jax.experimental import pallas as pl
from jax.experimental.pallas import tpu as pltpu

F32 = jnp.float32
BF16 = jnp.bfloat16
MESH = pl.DeviceIdType.MESH
AXES = ("x", "y", "c")

VMEM_LIMIT = 56 * 1024 * 1024
LANES = 128

D = 1024
D_FF = 2816
FF_SHARD = 1408
MIX_SHARD = 1280
MIX_W = 5120
HEAD_PAIRS = 4
HEAD_DIM = 64
CONV_W = 512
EPS = 1e-6
ATT_BLK = 256

ADAM_LR = 0.001
ADAM_B1 = 0.9
ADAM_B2 = 0.999
ADAM_EPS = 1e-08
ADAM_WD = 0.01
ADAM_STEP = 10


def _params(semantics=None):
    return pltpu.CompilerParams(dimension_semantics=semantics, vmem_limit_bytes=VMEM_LIMIT)


def _sigmoid(x):
    return 1.0 / (1.0 + jnp.exp(-x))


def _place():
    x, y, c = lax.axis_index("x"), lax.axis_index("y"), lax.axis_index("c")
    chips = [(1 - x, y), (x, 1 - y), (1 - x, 1 - y)]
    return x, y, c, chips


def _allgather_rows(name, blk):
    m_per, n = blk.shape

    def body(x_ref, out_ref, send_sems, recv_sems, local_sem):
        x, y, c, chips = _place()
        me, sibling = (x, y, c), (x, y, 1 - c)

        def rows(px, py, pc):
            return out_ref.at[pl.ds((4 * px + 2 * py + pc) * m_per, m_per), :]

        def copy(k, block, to, src=None):
            return pltpu.make_async_remote_copy(
                src_ref=rows(*block) if src is None else src, dst_ref=rows(*block),
                send_sem=send_sems.at[k], recv_sem=recv_sems.at[k], device_id=to, device_id_type=MESH)

        mine = pltpu.make_async_copy(x_ref, rows(*me), local_sem)
        mine.start()
        first = [copy(0, me, sibling, src=x_ref)]
        first += [copy(1 + j, me, (*chip, c), src=x_ref) for j, chip in enumerate(chips)]
        for cp in first:
            cp.start()
        passed = [copy(4 + j, (*chip, c), sibling) for j, chip in enumerate(chips)]
        for j, chip in enumerate(chips):
            copy(1 + j, (*chip, c), me).wait_recv()
            passed[j].start()
        copy(0, sibling, me).wait_recv()
        for j, chip in enumerate(chips):
            copy(4 + j, (*chip, 1 - c), me).wait_recv()
        for cp in first + passed:
            cp.wait_send()
        mine.wait()

    return pl.pallas_call(
        body, name=name,
        out_shape=jax.ShapeDtypeStruct((8 * m_per, n), blk.dtype),
        in_specs=[pl.BlockSpec(memory_space=pltpu.VMEM)],
        out_specs=pl.BlockSpec(memory_space=pltpu.VMEM),
        scratch_shapes=[pltpu.SemaphoreType.DMA((7,)), pltpu.SemaphoreType.DMA((7,)), pltpu.SemaphoreType.DMA],
        compiler_params=pltpu.CompilerParams(vmem_limit_bytes=VMEM_LIMIT),
    )(blk)


def _hbm_specs(n):
    return [pl.BlockSpec(memory_space=pltpu.HBM)] * n


def _allgather_weights(shards):
    n = len(shards)

    def body(*refs):
        start, relay, finish = _gather_protocol(refs[:n], refs[n:2 * n], *refs[2 * n:])
        start()
        relay()
        finish()

    gathered = pl.pallas_call(
        body, name="allgather_weights",
        out_shape=_gather_shapes(shards), in_specs=_hbm_specs(n), out_specs=_hbm_specs(n),
        scratch_shapes=_gather_sems(n),
    )(*shards)
    return _with_own_shard(gathered, shards)


def _gather_shapes(shards):
    return [jax.ShapeDtypeStruct((4, *s.shape), s.dtype) for s in shards]


def _gather_sems(n):
    return [pltpu.SemaphoreType.DMA((6 * n,)), pltpu.SemaphoreType.DMA((6 * n,))]


def _with_own_shard(gathered, shards):
    chip = 2 * lax.axis_index("x") + lax.axis_index("y")
    return [lax.dynamic_update_slice(g, s[None], (chip, 0, 0)) for g, s in zip(gathered, shards)]


def _gather_protocol(ins, outs, send_sems, recv_sems):
    n = len(ins)
    x, y, c, chips = _place()
    me, sibling = (x, y, c), (x, y, 1 - c)
    me_k = 2 * x + y

    def half(w, k, hc):
        h = ins[w].shape[0] // 2
        return outs[w].at[k, pl.ds(pl.multiple_of(hc * h, 8), h), :]

    def copy(w, j, k, hc, to, src=None):
        dst = half(w, k, hc)
        return pltpu.make_async_remote_copy(
            src_ref=dst if src is None else src, dst_ref=dst,
            send_sem=send_sems.at[6 * w + j], recv_sem=recv_sems.at[6 * w + j],
            device_id=to, device_id_type=MESH)

    def first(w, j):
        h = ins[w].shape[0] // 2
        src = ins[w].at[pl.ds(pl.multiple_of(c * h, 8), h), :]
        return copy(w, j, me_k, c, (*chips[j], c), src=src)

    def passed(w, j):
        px, py = chips[j]
        return copy(w, 3 + j, 2 * px + py, c, sibling)

    pairs = [(w, j) for w in range(n) for j in range(3)]

    def start():
        for w, j in pairs:
            first(w, j).start()

    def relay():
        for w, j in pairs:
            px, py = chips[j]
            copy(w, j, 2 * px + py, c, me).wait_recv()
            passed(w, j).start()

    def finish():
        for w, j in pairs:
            px, py = chips[j]
            copy(w, 3 + j, 2 * px + py, 1 - c, me).wait_recv()
        for w, j in pairs:
            first(w, j).wait_send()
            passed(w, j).wait_send()

    return start, relay, finish


def _sibling_swap_halves(name, grads):
    n = len(grads)

    def body(*refs):
        ins, outs = refs[:n], refs[n:2 * n]
        send_sems, recv_sems = refs[2 * n:]
        x, y, c, _ = _place()
        cps = []
        for w in range(n):
            cp = pltpu.make_async_remote_copy(
                src_ref=ins[w].at[:, 1 - c], dst_ref=outs[w],
                send_sem=send_sems.at[w], recv_sem=recv_sems.at[w],
                device_id=(x, y, 1 - c), device_id_type=MESH)
            cp.start()
            cps.append(cp)
        for cp in cps:
            cp.wait()

    return pl.pallas_call(
        body, name=name,
        out_shape=[jax.ShapeDtypeStruct((4, *g.shape[2:]), g.dtype) for g in grads],
        in_specs=_hbm_specs(n), out_specs=_hbm_specs(n),
        scratch_shapes=[pltpu.SemaphoreType.DMA((n,)), pltpu.SemaphoreType.DMA((n,))],
    )(*grads)


def _all_to_all_sems(n):
    return [pltpu.SemaphoreType.DMA((3 * n,)), pltpu.SemaphoreType.DMA((3 * n,))]


def _all_to_all_protocol(ins, outs, send_sems, recv_sems):
    n = len(ins)
    x, y, c, chips = _place()
    me_k = 2 * x + y
    pairs = [(w, j) for w in range(n) for j in range(3)]

    def sent(w, j):
        px, py = chips[j]
        return pltpu.make_async_remote_copy(
            src_ref=ins[w].at[2 * px + py], dst_ref=outs[w].at[me_k],
            send_sem=send_sems.at[3 * w + j], recv_sem=recv_sems.at[3 * w + j],
            device_id=(px, py, c), device_id_type=MESH)

    def start():
        for w, j in pairs:
            sent(w, j).start()

    def finish():
        for w, j in pairs:
            px, py = chips[j]
            slab = outs[w].at[2 * px + py]
            pltpu.make_async_remote_copy(
                src_ref=slab, dst_ref=slab, send_sem=send_sems.at[3 * w + j],
                recv_sem=recv_sems.at[3 * w + j], device_id=(px, py, c), device_id_type=MESH).wait_recv()
        for w, j in pairs:
            sent(w, j).wait_send()

    return start, finish


def _sibling_share(halves):
    n = len(halves)

    def body(*refs):
        ins, outs = refs[:n], refs[n:2 * n]
        send_sems, recv_sems = refs[2 * n:]
        x, y, c, _ = _place()
        cps = []
        for w in range(n):
            cp = pltpu.make_async_remote_copy(
                src_ref=ins[w], dst_ref=outs[w], send_sem=send_sems.at[w], recv_sem=recv_sems.at[w],
                device_id=(x, y, 1 - c), device_id_type=MESH)
            cp.start()
            cps.append(cp)
        for cp in cps:
            cp.wait()

    return pl.pallas_call(
        body, name="grad_sibling_share",
        out_shape=[jax.ShapeDtypeStruct(p.shape, p.dtype) for p in halves],
        in_specs=_hbm_specs(n), out_specs=_hbm_specs(n),
        scratch_shapes=[pltpu.SemaphoreType.DMA((n,)), pltpu.SemaphoreType.DMA((n,))],
    )(*halves)


def _mm(name, grid, a, a_spec, b, b_spec, contract, out_shapes, out_specs, epilogue,
        extras=(), extra_specs=(), nk=1, acc_shape=None, semantics=None, a2a_parts=()):
    ne, no, nc = len(extras), len(out_shapes), len(a2a_parts)
    nd = len(grid)

    def body(*refs):
        a_ref, b_ref = refs[0], refs[1]
        ex, outs = refs[2:2 + ne], refs[2 + ne + nc:2 + ne + nc + no]
        if nc:
            ids = [pl.program_id(d) for d in range(nd)]
            start, finish = _all_to_all_protocol(refs[2 + ne:2 + ne + nc],
                                                 refs[2 + ne + nc + no:2 + ne + 2 * nc + no], *refs[-2:])
            pl.when(functools.reduce(jnp.logical_and, [i == 0 for i in ids]))(start)

        def prod():
            return lax.dot_general(a_ref[...], b_ref[...], (((contract[0],), (contract[1],)), ((), ())),
                                   preferred_element_type=F32)

        if nk == 1:
            epilogue(prod(), ex, outs)
        else:
            acc = refs[2 + ne + 2 * nc + no]
            k = pl.program_id(nd - 1)

            @pl.when(k == 0)
            def _():
                acc[...] = prod()

            @pl.when(k > 0)
            def _():
                acc[...] += prod()

            @pl.when(k == nk - 1)
            def _():
                epilogue(acc[...], ex, outs)

        if nc:
            pl.when(functools.reduce(jnp.logical_and, [i == g - 1 for i, g in zip(ids, grid)]))(finish)

    if semantics is None or nc:
        semantics = ("arbitrary",) * nd
    return pl.pallas_call(
        body, name=name, grid=grid,
        in_specs=[a_spec, b_spec, *extra_specs] + _hbm_specs(nc),
        out_specs=list(out_specs) + _hbm_specs(nc),
        out_shape=list(out_shapes) + [jax.ShapeDtypeStruct(p.shape, p.dtype) for p in a2a_parts],
        scratch_shapes=([] if nk == 1 else [pltpu.VMEM(acc_shape, F32)]) + (_all_to_all_sems(nc) if nc else []),
        compiler_params=_params(semantics),
    )(a, b, *extras, *a2a_parts)


def _store(dtype):
    def epilogue(acc, ex, outs):
        outs[0][...] = acc.astype(dtype)
    return epilogue


def _sds(shape, dtype):
    return jax.ShapeDtypeStruct(shape, dtype)


TR = 512


def _row_spec(width, tr=TR):
    return pl.BlockSpec((tr, width), lambda i: (i, 0))


def _const_spec(shape):
    nd = len(shape)
    return pl.BlockSpec(shape, lambda i: (0,) * nd)


def _norm_mod_fwd(name, h, p):
    S = h.shape[0]

    def body(h_ref, p_ref, u_ref):
        hv = h_ref[...]
        r = lax.rsqrt(jnp.mean(hv * hv, axis=-1, keepdims=True) + EPS)
        nrm = (hv * r) * p_ref[0:1, :]
        u_ref[...] = (nrm * (1.0 + p_ref[1:2, :]) + p_ref[2:3, :]).astype(BF16)

    return pl.pallas_call(
        body, name=name, grid=(S // TR,),
        in_specs=[_row_spec(D), _const_spec((8, D))], out_specs=_row_spec(D),
        out_shape=_sds((S, D), BF16), compiler_params=_params(("parallel",)),
    )(h, p)


def _final_loss_bwd(h, gf, target, p3, f3):
    S = h.shape[0]

    def body(h_ref, g_ref, t_ref, p_ref, f_ref, dh_ref, df_ref, sums_ref, loss_ref):
        i = pl.program_id(0)

        @pl.when(i == 0)
        def _():
            sums_ref[...] = jnp.zeros_like(sums_ref)
            loss_ref[...] = jnp.zeros_like(loss_ref)

        hv = h_ref[...]
        g = g_ref[0:1, :]
        r = lax.rsqrt(jnp.mean(hv * hv, axis=-1, keepdims=True) + EPS)
        xn = hv * r
        err = xn * g - t_ref[...]
        loss_ref[...] += 0.5 * jnp.sum(err * err) * (1.0 / D)
        dout = err * (1.0 / D)
        dxn = dout * g
        dh = r * (dxn - xn * jnp.mean(dxn * xn, axis=-1, keepdims=True))
        dh_ref[...] = dh
        gate = p_ref[3:4, :]
        df_ref[...] = (0.5 * gate * dh).astype(BF16)
        sums_ref[0:1, :] += jnp.sum(dout * xn, axis=0, keepdims=True)
        sums_ref[1:2, :] += 0.5 * jnp.sum(dh * f_ref[...].astype(F32), axis=0, keepdims=True)

    return pl.pallas_call(
        body, name="final_loss_bwd", grid=(S // TR,),
        in_specs=[_row_spec(D), _const_spec((8, D)), _row_spec(D), _const_spec((8, D)), _row_spec(D)],
        out_specs=[_row_spec(D), _row_spec(D), _const_spec((8, D)), _const_spec((8, LANES))],
        out_shape=[_sds((S, D), F32), _sds((S, D), BF16), _sds((8, D), F32), _sds((8, LANES), F32)],
        compiler_params=_params(("arbitrary",)),
    )(h, gf, target, p3, f3)


def _norm_mod_bwd(name, du, h, p, dh_res, prev=None):
    S = h.shape[0]
    has_prev = prev is not None

    def body(*refs):
        if has_prev:
            du_ref, h_ref, p_ref, r_ref, pp_ref, f_ref, dh_ref, df_ref, sums_ref = refs
        else:
            du_ref, h_ref, p_ref, r_ref, dh_ref, sums_ref = refs
        i = pl.program_id(0)

        @pl.when(i == 0)
        def _():
            sums_ref[...] = jnp.zeros_like(sums_ref)

        hv = h_ref[...]
        duv = du_ref[...]
        g = p_ref[0:1, :]
        one_scale = 1.0 + p_ref[1:2, :]
        r = lax.rsqrt(jnp.mean(hv * hv, axis=-1, keepdims=True) + EPS)
        xn = hv * r
        dn = duv * one_scale
        dxn = dn * g
        dh = r_ref[...] + r * (dxn - xn * jnp.mean(dxn * xn, axis=-1, keepdims=True))
        dh_ref[...] = dh
        sums_ref[0:1, :] += jnp.sum(duv, axis=0, keepdims=True)
        sums_ref[1:2, :] += jnp.sum(duv * (xn * g), axis=0, keepdims=True)
        sums_ref[2:3, :] += jnp.sum(dn * xn, axis=0, keepdims=True)
        if has_prev:
            wgt = prev[2]
            df_ref[...] = (wgt * pp_ref[3:4, :] * dh).astype(BF16)
            sums_ref[3:4, :] += wgt * jnp.sum(dh * f_ref[...].astype(F32), axis=0, keepdims=True)

    ins = [du, h, p, dh_res]
    in_specs = [_row_spec(D), _row_spec(D), _const_spec((8, D)), _row_spec(D)]
    out_specs = [_row_spec(D)]
    out_shape = [_sds((S, D), F32)]
    if has_prev:
        ins += [prev[0], prev[1]]
        in_specs += [_const_spec((8, D)), _row_spec(D)]
        out_specs.append(_row_spec(D))
        out_shape.append(_sds((S, D), BF16))
    out_specs.append(_const_spec((8, D)))
    out_shape.append(_sds((8, D), F32))
    return pl.pallas_call(
        body, name=name, grid=(S // TR,), in_specs=in_specs, out_specs=out_specs, out_shape=out_shape,
        compiler_params=_params(("arbitrary",)),
    )(*ins)


TM = 512


def _ffn_up(name, u, wgu4, shards=()):
    S = u.shape[0]
    n = len(shards)
    ni = S // TM

    def body(u_ref, wg_ref, wu_ref, *rest):
        gu_ref, hm_ref = rest[n:n + 2]
        s, i = pl.program_id(0), pl.program_id(1)
        if n:
            start, relay, finish = _gather_protocol(rest[:n], rest[n + 2:2 * n + 2], *rest[2 * n + 2:])
            pl.when((s == 0) & (i == 0))(start)
            pl.when((s == 1) & (i == 0))(relay)
        uv = u_ref[...]
        g = jnp.dot(uv, wg_ref[...], preferred_element_type=F32)
        up = jnp.dot(uv, wu_ref[...], preferred_element_type=F32)
        gu_ref[0] = g.astype(BF16)
        gu_ref[1] = up.astype(BF16)
        hm_ref[...] = (g * _sigmoid(g) * up).astype(BF16)
        if n:
            pl.when((s == 1) & (i == ni - 1))(finish)

    gu, hm, *gathered = pl.pallas_call(
        body, name=name, grid=(2, ni),
        in_specs=[pl.BlockSpec((TM, D), lambda s, i: (i, 0)),
                  pl.BlockSpec((None, D, FF_SHARD), lambda s, i: (s, 0, 0)),
                  pl.BlockSpec((None, D, FF_SHARD), lambda s, i: (s + 2, 0, 0))] + _hbm_specs(n),
        out_specs=[pl.BlockSpec((2, TM, FF_SHARD), lambda s, i: (0, i, s)),
                   pl.BlockSpec((TM, FF_SHARD), lambda s, i: (i, s))] + _hbm_specs(n),
        out_shape=[_sds((2, S, D_FF), BF16), _sds((S, D_FF), BF16)] + _gather_shapes(shards),
        scratch_shapes=_gather_sems(n) if n else [],
        compiler_params=_params(("arbitrary", "arbitrary") if n else ("parallel", "parallel")),
    )(u, wgu4, wgu4, *shards)
    return gu, hm, _with_own_shard(gathered, shards)


def _proj_residual(name, a, w, h, p, weight):
    S, K = a.shape

    def epilogue(acc, ex, outs):
        h_ref, p_ref = ex
        outs[0][...] = acc.astype(BF16)
        outs[1][...] = h_ref[...] + weight * p_ref[3:4, :] * acc

    return _mm(
        name, (S // TM,), a, pl.BlockSpec((TM, K), lambda i: (i, 0)), w, pl.BlockSpec((K, D), lambda i: (0, 0)),
        (1, 0), [_sds((S, D), BF16), _sds((S, D), F32)], [_row_spec(D, TM), _row_spec(D, TM)], epilogue,
        extras=(h, p), extra_specs=(_row_spec(D, TM), _const_spec((8, D))), semantics=("parallel",))


def _ffn_down_bwd(name, df, wd, gu, a2a_parts=()):
    S = df.shape[0]

    def epilogue(acc, ex, outs):
        g = ex[0][0].astype(F32)
        up = ex[0][1].astype(F32)
        sg = _sigmoid(g)
        outs[0][0] = (acc * up * (sg * (1.0 + g * (1.0 - sg)))).astype(BF16)
        outs[0][1] = (acc * g * sg).astype(BF16)

    gu_spec = pl.BlockSpec((2, TM, FF_SHARD), lambda n, i: (0, i, n))
    return _mm(
        name, (2, S // TM), df, pl.BlockSpec((TM, D), lambda n, i: (i, 0)),
        wd, pl.BlockSpec((FF_SHARD, D), lambda n, i: (n, 0)), (1, 1),
        [_sds((2, S, D_FF), BF16)], [gu_spec], epilogue, extras=(gu,), extra_specs=(gu_spec,),
        semantics=("parallel", "parallel"), a2a_parts=a2a_parts)


TK = 512


def _grad_w(name, a, a_w, b, b_w, b_map, n_out, out_shape, out_block, out_map, a2a_parts=()):
    S = a.shape[0]
    nk = S // TK
    res = _mm(
        name, (n_out, nk), a, pl.BlockSpec((TK, a_w), lambda s, k: (k, 0)), b, pl.BlockSpec(
            (None, TK, b_w) if b.ndim == 3 else (TK, b_w), b_map), (0, 0),
        [_sds(out_shape, F32)], [pl.BlockSpec(out_block, out_map)], _store(F32), nk=nk, acc_shape=(a_w, b_w),
        semantics=("parallel", "arbitrary"), a2a_parts=a2a_parts)
    return res if a2a_parts else res[0]


def _ffn_dw_down(name, hm, df):
    S = df.shape[0]
    return _mm(
        name, (2, S // TK), hm, pl.BlockSpec((TK, FF_SHARD), lambda m, k: (k, m)),
        df, pl.BlockSpec((TK, D), lambda m, k: (k, 0)), (0, 0),
        [_sds((D_FF, D), F32)], [pl.BlockSpec((FF_SHARD, D), lambda m, k: (m, 0))], _store(F32),
        nk=S // TK, acc_shape=(FF_SHARD, D), semantics=("parallel", "arbitrary"))[0]


def _ffn_up_bwd(name, dgu, wgu4, a2a_parts=()):
    S = dgu.shape[1]
    return _mm(
        name, (S // TM, 4), dgu, pl.BlockSpec((None, TM, FF_SHARD), lambda i, s: (s // 2, i, s % 2)),
        wgu4, pl.BlockSpec((None, D, FF_SHARD), lambda i, s: (s, 0, 0)), (1, 1),
        [_sds((S, D), F32)], [pl.BlockSpec((TM, D), lambda i, s: (i, 0))], _store(F32),
        nk=4, acc_shape=(TM, D), semantics=("parallel", "arbitrary"), a2a_parts=a2a_parts)


def _ffn_dw_gu(name, u_in, dgu, a2a_parts=()):
    return _grad_w(name, u_in, D, dgu, FF_SHARD, lambda s, k: (s // 2, k, s % 2), 4,
                   (4, D, FF_SHARD), (None, D, FF_SHARD), lambda s, k: (s, 0, 0), a2a_parts=a2a_parts)


def _ffn_bwd(tag, df, u_in, gu, hm, wgu4, wd):
    dgu = _ffn_down_bwd(tag + "_down_bwd", df, wd, gu)[0]
    dwd = _ffn_dw_down(tag + "_dw_down", hm, df)
    du = _ffn_up_bwd(tag + "_up_bwd", dgu, wgu4)[0]
    dwgu = _ffn_dw_gu(tag + "_dw_gu", u_in, dgu)
    return du, dwgu, dwd


def _shift_down(v, k, row):
    return jnp.where(row >= k, pltpu.roll(v, k, axis=0), 0.0)


def _shift_up(v, k, row, S):
    return jnp.where(row < S - k, pltpu.roll(v, S - k, axis=0), 0.0)


def _conv_specs(S):
    cols = CONV_W // LANES
    return [pl.BlockSpec((S, LANES), functools.partial(lambda j, off: (0, off + j), off=o * cols))
            for o in range(3)]


def _conv_fwd(proj, conv_w):
    S = proj.shape[0]

    def body(cb_ref, cc_ref, cx_ref, w_ref, sc_ref):
        row = lax.broadcasted_iota(jnp.int32, (S, LANES), 0)
        v = cc_ref[...].astype(F32) * cx_ref[...].astype(F32)
        yv = w_ref[0:1, :] * _shift_down(v, 2, row) + w_ref[1:2, :] * _shift_down(v, 1, row) + w_ref[2:3, :] * v
        sc_ref[...] = (cb_ref[...].astype(F32) * yv).astype(BF16)

    return pl.pallas_call(
        body, name="conv_fwd", grid=(CONV_W // LANES,),
        in_specs=_conv_specs(S) + [pl.BlockSpec((3, LANES), lambda j: (0, j))],
        out_specs=pl.BlockSpec((S, LANES), lambda j: (0, j)), out_shape=_sds((S, CONV_W), BF16),
        compiler_params=_params(("parallel",)),
    )(proj, proj, proj, conv_w)


def _conv_bwd(dsc, proj, conv_w):
    S = proj.shape[0]

    def body(d_ref, cb_ref, cc_ref, cx_ref, w_ref, dcb_ref, dcc_ref, dcx_ref, dw_ref):
        row = lax.broadcasted_iota(jnp.int32, (S, LANES), 0)
        cc = cc_ref[...].astype(F32)
        cx = cx_ref[...].astype(F32)
        d = d_ref[...].astype(F32)
        v = cc * cx
        v1 = _shift_down(v, 1, row)
        v2 = _shift_down(v, 2, row)
        w0, w1, w2 = w_ref[0:1, :], w_ref[1:2, :], w_ref[2:3, :]
        dcb_ref[...] = (d * (w0 * v2 + w1 * v1 + w2 * v)).astype(BF16)
        dy = d * cb_ref[...].astype(F32)
        dw_ref[0:1, :] = jnp.sum(dy * v2, axis=0, keepdims=True)
        dw_ref[1:2, :] = jnp.sum(dy * v1, axis=0, keepdims=True)
        dw_ref[2:3, :] = jnp.sum(dy * v, axis=0, keepdims=True)
        dv = w2 * dy + w1 * _shift_up(dy, 1, row, S) + w0 * _shift_up(dy, 2, row, S)
        dcc_ref[...] = (dv * cx).astype(BF16)
        dcx_ref[...] = (dv * cc).astype(BF16)

    col = pl.BlockSpec((S, LANES), lambda j: (0, j))
    return pl.pallas_call(
        body, name="conv_bwd", grid=(CONV_W // LANES,),
        in_specs=[col] + _conv_specs(S) + [pl.BlockSpec((3, LANES), lambda j: (0, j))],
        out_specs=[col, col, col, pl.BlockSpec((3, LANES), lambda j: (0, j))],
        out_shape=[_sds((S, CONV_W), BF16)] * 3 + [_sds((3, CONV_W), F32)],
        compiler_params=_params(("parallel",)),
    )(dsc, proj, proj, proj, conv_w)


Q_COL, K_COL, V_COL = 1536 // LANES, 2048 // LANES, 2560 // LANES


def _split_dot(x, tri):
    hi = x.astype(BF16)
    lo = (x - hi.astype(F32)).astype(BF16)
    return jnp.dot(hi, tri, preferred_element_type=F32) + jnp.dot(lo, tri, preferred_element_type=F32)


def _tri_dot(tri, x):
    hi = x.astype(BF16)
    lo = (x - hi.astype(F32)).astype(BF16)
    return jnp.dot(tri, hi, preferred_element_type=F32) + jnp.dot(tri, lo, preferred_element_type=F32)


def _softplus(z):
    return jnp.maximum(z, 0.0) + jnp.log(1.0 + jnp.exp(-jnp.abs(z)))


def _nt(a, b):
    return lax.dot_general(a, b, (((1,), (1,)), ((), ())), preferred_element_type=F32)


def _tn(a, b):
    return lax.dot_general(a, b, (((0,), (0,)), ((), ())), preferred_element_type=F32)


def _attn_fwd(proj, shards):
    S = proj.shape[0]
    B = ATT_BLK
    nq = S // B
    n = len(shards)

    def body(q_ref, k_ref, v_ref, *rest):
        o_ref, t_ref = rest[n:n + 2]
        start, relay, finish = _gather_protocol(rest[:n], rest[n + 2:2 * n + 2], *rest[2 * n + 2:])
        p = pl.program_id(0)
        i = pl.program_id(1)
        pl.when((p == 0) & (i == 0))(start)
        pl.when((p == HEAD_PAIRS // 2) & (i == 0))(relay)
        lo_lane = lax.broadcasted_iota(jnp.int32, (B, LANES), 1) < HEAD_DIM
        row = lax.broadcasted_iota(jnp.int32, (B, B), 0)
        col = lax.broadcasted_iota(jnp.int32, (B, B), 1)
        after = (row > col).astype(BF16)
        causal = jnp.concatenate([col < row, col < row], axis=0)
        q2 = q_ref[...] * 0.125
        zero = jnp.zeros((), BF16)
        q_st = jnp.concatenate([jnp.where(lo_lane, q2, zero), jnp.where(lo_lane, zero, q2)], axis=0)

        def tile(kb, carry, diag):
            r, acc = carry
            k2 = k_ref[pl.ds(pl.multiple_of(kb * B, B), B), :]
            v2 = v_ref[pl.ds(pl.multiple_of(kb * B, B), B), :]
            z = _nt(q_st, k2)
            spz = _softplus(z)
            sp = jnp.where(causal, spz, 0.0) if diag else spz
            rem = _split_dot(sp, after) + r
            a = jnp.exp(z - spz - rem)
            if diag:
                a = jnp.where(causal, a, 0.0)
            acc = acc + jnp.dot(a.astype(BF16), v2, preferred_element_type=F32)
            return r + jnp.sum(sp, axis=1, keepdims=True), acc

        init = (jnp.zeros((2 * B, 1), F32), jnp.zeros((2 * B, LANES), F32))
        carry = tile(i, init, True)
        carry = lax.fori_loop(0, i % 2, lambda j, cr: tile(i - 1, cr, False), carry)
        first = i - 1 - i % 2

        def pair(j, cr):
            return tile(first - 2 * j - 1, tile(first - 2 * j, cr, False), False)

        r, acc = lax.fori_loop(0, i // 2, pair, carry)
        o_ref[...] = jnp.where(lo_lane, acc[:B], acc[B:]).astype(BF16)
        t_ref[...] = jnp.where(lo_lane, r[:B], r[B:]).T
        pl.when((p == HEAD_PAIRS - 1) & (i == nq - 1))(finish)

    seq = lambda off: pl.BlockSpec((S, LANES), lambda p, i: (0, off + p))
    blk = pl.BlockSpec((B, LANES), lambda p, i: (i, p))
    o, t, *gathered = pl.pallas_call(
        body, name="attn_fwd", grid=(HEAD_PAIRS, nq),
        in_specs=[pl.BlockSpec((B, LANES), lambda p, i: (i, Q_COL + p)), seq(K_COL), seq(V_COL)] + _hbm_specs(n),
        out_specs=[blk, pl.BlockSpec((LANES, B), lambda p, i: (p, i))] + _hbm_specs(n),
        out_shape=[_sds((S, 512), BF16), _sds((512, S), F32)] + _gather_shapes(shards),
        scratch_shapes=_gather_sems(n),
        compiler_params=_params(("arbitrary", "arbitrary")),
    )(proj, proj, proj, *shards)
    return o, t, _with_own_shard(gathered, shards)


def _attn_bwd(proj, t, do, parts):
    S = proj.shape[0]
    kt = proj[:, K_COL * LANES:V_COL * LANES].T
    B = ATT_BLK
    nq = S // B
    n = len(parts)

    def body(q_ref, k_ref, v_ref, kt_ref, t_ref, do_ref, *rest):
        dq_ref, dk_ref, dv_ref = rest[n:n + 3]
        dk_acc, dv_acc = rest[2 * n + 3:2 * n + 5]
        start, finish = _all_to_all_protocol(rest[:n], rest[n + 3:2 * n + 3], *rest[2 * n + 5:])
        i = pl.program_id(1)
        pl.when((pl.program_id(0) == 0) & (i == 0))(start)

        @pl.when(i == 0)
        def _():
            dk_acc[...] = jnp.zeros_like(dk_acc)
            dv_acc[...] = jnp.zeros_like(dv_acc)

        lo_lane = lax.broadcasted_iota(jnp.int32, (B, LANES), 1) < HEAD_DIM
        key = lax.broadcasted_iota(jnp.int32, (B, B), 0)
        qry = lax.broadcasted_iota(jnp.int32, (B, B), 1)
        upto = (qry <= key).astype(BF16)
        before = (qry < key).astype(BF16)
        causal = jnp.concatenate([key < qry, key < qry], axis=1)
        zero = jnp.zeros((), BF16)
        q2 = q_ref[...] * 0.125
        do2 = do_ref[...]
        q_st = jnp.concatenate([jnp.where(lo_lane, q2, zero), jnp.where(lo_lane, zero, q2)], axis=0)
        do_st = jnp.concatenate([jnp.where(lo_lane, do2, zero), jnp.where(lo_lane, zero, do2)], axis=0)
        t_st = jnp.concatenate([t_ref[0:1, :], t_ref[HEAD_DIM:HEAD_DIM + 1, :]], axis=1)

        def tile(kb, carry, diag):
            pc, ec, dqt = carry
            rows = pl.ds(pl.multiple_of(kb * B, B), B)
            k2 = k_ref[rows, :]
            v2 = v_ref[rows, :]
            z = _nt(k2, q_st)
            spz = _softplus(z)
            sp = jnp.where(causal, spz, 0.0) if diag else spz
            rem = (t_st - pc) - _tri_dot(upto, sp)
            a = jnp.exp(z - spz - rem)
            if diag:
                a = jnp.where(causal, a, 0.0)
            e = a * _nt(v2, do_st)
            e_before = ec + jnp.dot(before, e.astype(BF16), preferred_element_type=F32)
            u = jnp.exp(-spz)
            dz = u * (e + e_before) - e_before
            if diag:
                dz = jnp.where(causal, dz, 0.0)
            dzb = dz.astype(BF16)
            dk_acc[rows, :] += jnp.dot(dzb, q_st, preferred_element_type=F32)
            dv_acc[rows, :] += jnp.dot(a.astype(BF16), do_st, preferred_element_type=F32)
            kt = kt_ref[:, pl.ds(pl.multiple_of(kb * B, B), B)]
            return (pc + jnp.sum(sp, axis=0, keepdims=True), ec + jnp.sum(e, axis=0, keepdims=True),
                    dqt + jnp.dot(kt, dzb, preferred_element_type=F32))

        def pair(j, carry):
            return tile(2 * j + 1, tile(2 * j, carry, False), False)

        zc = jnp.zeros((1, 2 * B), F32)
        carry = lax.fori_loop(0, i // 2, pair, (zc, zc, jnp.zeros((LANES, 2 * B), F32)))
        carry = lax.fori_loop(0, i % 2, lambda j, cr: tile(i - 1, cr, False), carry)
        dqt = tile(i, carry, True)[2]
        head0 = lax.broadcasted_iota(jnp.int32, (LANES, B), 0) < HEAD_DIM
        dq_ref[...] = (jnp.where(head0, dqt[:, :B], dqt[:, B:]).T * 0.125).astype(BF16)

        @pl.when(i == nq - 1)
        def _():
            dk_ref[...] = dk_acc[...].astype(BF16)
            dv_ref[...] = dv_acc[...].astype(BF16)

        pl.when((pl.program_id(0) == HEAD_PAIRS - 1) & (i == nq - 1))(finish)

    seq = lambda off: pl.BlockSpec((S, LANES), lambda p, i: (0, off + p))
    blk = pl.BlockSpec((B, LANES), lambda p, i: (i, p))
    whole = pl.BlockSpec((S, LANES), lambda p, i: (0, p))
    dq, dk, dv, *came = pl.pallas_call(
        body, name="attn_bwd", grid=(HEAD_PAIRS, nq),
        in_specs=[pl.BlockSpec((B, LANES), lambda p, i: (i, Q_COL + p)), seq(K_COL), seq(V_COL),
                  pl.BlockSpec((LANES, S), lambda p, i: (p, 0)), pl.BlockSpec((LANES, B), lambda p, i: (p, i)), blk]
        + _hbm_specs(n),
        out_specs=[blk, whole, whole] + _hbm_specs(n),
        out_shape=[_sds((S, 512), BF16)] * 3 + [jax.ShapeDtypeStruct(p.shape, p.dtype) for p in parts],
        scratch_shapes=[pltpu.VMEM((S, LANES), F32), pltpu.VMEM((S, LANES), F32)] + _all_to_all_sems(n),
        compiler_params=_params(("arbitrary", "arbitrary")),
    )(proj, proj, proj, kt, t, do, *parts)
    return dq, dk, dv, came


GA_COL, GB_COL = 3072 // 256, 4096 // 256


def _merge_fwd(sc, o, wco4, wao4, proj, bm):
    S = sc.shape[0]

    def body(sc_ref, o_ref, wc_ref, wa_ref, ga_ref, gb_ref, bm_ref, ya_ref, yb_ref, mg_ref):
        ya = jnp.dot(sc_ref[...], wc_ref[...], preferred_element_type=F32)
        yb = jnp.dot(o_ref[...], wa_ref[...], preferred_element_type=F32)
        sa = _sigmoid(ga_ref[...].astype(F32) + bm_ref[0:1, :])
        sb = _sigmoid(gb_ref[...].astype(F32) + bm_ref[1:2, :])
        ya_ref[...] = ya.astype(BF16)
        yb_ref[...] = yb.astype(BF16)
        mg_ref[...] = (sa * ya + sb * yb).astype(BF16)

    wide = pl.BlockSpec((TM, 512), lambda n, i: (i, 0))
    wsp = pl.BlockSpec((None, 512, 256), lambda n, i: (n, 0, 0))
    out = pl.BlockSpec((TM, 256), lambda n, i: (i, n))
    return pl.pallas_call(
        body, name="merge_fwd", grid=(4, S // TM),
        in_specs=[wide, wide, wsp, wsp, pl.BlockSpec((TM, 256), lambda n, i: (i, GA_COL + n)),
                  pl.BlockSpec((TM, 256), lambda n, i: (i, GB_COL + n)), pl.BlockSpec((2, 256), lambda n, i: (0, n))],
        out_specs=[out, out, out], out_shape=[_sds((S, D), BF16)] * 3,
        compiler_params=_params(("parallel", "parallel")),
    )(sc, o, wco4, wao4, proj, proj, bm)


def _merge_bwd(dy2, wout, ya, yb, proj, bm):
    S = dy2.shape[0]

    def epilogue(acc, ex, outs):
        ya_ref, yb_ref, ga_ref, gb_ref, bm_ref = ex
        i = pl.program_id(1)
        sa = _sigmoid(ga_ref[...].astype(F32) + bm_ref[0:1, :])
        sb = _sigmoid(gb_ref[...].astype(F32) + bm_ref[1:2, :])
        dga = acc * ya_ref[...].astype(F32) * (sa * (1.0 - sa))
        dgb = acc * yb_ref[...].astype(F32) * (sb * (1.0 - sb))
        outs[0][...] = (acc * sa).astype(BF16)
        outs[1][...] = (acc * sb).astype(BF16)
        outs[2][...] = dga.astype(BF16)
        outs[3][...] = dgb.astype(BF16)

        @pl.when(i == 0)
        def _():
            outs[4][...] = jnp.zeros_like(outs[4])

        outs[4][0:1, :] += jnp.sum(dga, axis=0, keepdims=True)
        outs[4][1:2, :] += jnp.sum(dgb, axis=0, keepdims=True)

    out = pl.BlockSpec((TM, 256), lambda n, i: (i, n))
    return _mm(
        "merge_bwd", (4, S // TM), dy2, pl.BlockSpec((TM, D), lambda n, i: (i, 0)),
        wout, pl.BlockSpec((256, D), lambda n, i: (n, 0)), (1, 1),
        [_sds((S, D), BF16)] * 4 + [_sds((8, D), F32)], [out, out, out, out, pl.BlockSpec((8, 256), lambda n, i: (0, n))],
        epilogue, extras=(ya, yb, proj, proj, bm),
        extra_specs=(out, out, pl.BlockSpec((TM, 256), lambda n, i: (i, GA_COL + n)),
                     pl.BlockSpec((TM, 256), lambda n, i: (i, GB_COL + n)), pl.BlockSpec((2, 256), lambda n, i: (0, n))),
        semantics=("parallel", "arbitrary"))


def _back_through_cols(name, dy, w4, width):
    S = dy.shape[0]
    return _mm(
        name, (S // TM, 4), dy, pl.BlockSpec((TM, 256), lambda i, s: (i, s)),
        w4, pl.BlockSpec((None, width, 256), lambda i, s: (s, 0, 0)), (1, 1),
        [_sds((S, width), BF16)], [pl.BlockSpec((TM, width), lambda i, s: (i, 0))], _store(BF16),
        nk=4, acc_shape=(TM, width), semantics=("parallel", "arbitrary"))[0]


ADA_SHARD = 2304
ADA_TN = 768


def _ada_fwd(c_all, w_ada_l, b_l):
    def body(c_ref, w_ref, b_ref, o_ref):
        cv = c_ref[...]
        ca = cv * _sigmoid(cv)
        o_ref[...] = jnp.dot(ca.astype(BF16), w_ref[...].astype(BF16), preferred_element_type=F32) + b_ref[...]

    return pl.pallas_call(
        body, name="ada_fwd", grid=(ADA_SHARD // ADA_TN,),
        in_specs=[pl.BlockSpec((8, D), lambda j: (0, 0)), pl.BlockSpec((D, ADA_TN), lambda j: (0, j)),
                  pl.BlockSpec((1, ADA_TN), lambda j: (0, j))],
        out_specs=pl.BlockSpec((8, ADA_TN), lambda j: (0, j)), out_shape=_sds((8, ADA_SHARD), F32),
        compiler_params=_params(("parallel",)),
    )(c_all, w_ada_l, b_l)


def _ada_bwd(c_all_t, dmod_l):
    def body(c_ref, d_ref, o_ref):
        cv = c_ref[...]
        ca = cv * _sigmoid(cv)
        o_ref[...] = jnp.dot(ca.astype(BF16).astype(F32), d_ref[...].astype(BF16).astype(F32),
                             preferred_element_type=F32, precision=lax.Precision.HIGHEST)

    return pl.pallas_call(
        body, name="ada_bwd", grid=(ADA_SHARD // ADA_TN,),
        in_specs=[pl.BlockSpec((D, 8), lambda j: (0, 0)), pl.BlockSpec((8, ADA_TN), lambda j: (0, j))],
        out_specs=pl.BlockSpec((D, ADA_TN), lambda j: (0, j)), out_shape=_sds((D, ADA_SHARD), F32),
        compiler_params=_params(("parallel",)),
    )(c_all_t, dmod_l)


def _sum_rows(name, x):
    n = x.shape[1]

    def body(x_ref, o_ref):
        s = x_ref[0:1, :]
        for d in range(1, 8):
            s = s + x_ref[d:d + 1, :]
        o_ref[...] = s

    return pl.pallas_call(
        body, name=name, in_specs=[pl.BlockSpec(memory_space=pltpu.VMEM)],
        out_specs=pl.BlockSpec(memory_space=pltpu.VMEM), out_shape=_sds((1, n), F32),
        compiler_params=pltpu.CompilerParams(vmem_limit_bytes=VMEM_LIMIT),
    )(x)


def _pair_sum(name, g4, recv, c_idx):
    _, _, h, C = g4.shape
    tr = h if h <= 512 else h // (h // 256) if h % 256 == 0 else h // 2

    def body(c_ref, g_ref, r_ref, o_ref):
        o_ref[...] = (g_ref[...] + r_ref[...]).astype(BF16)

    grid_spec = pltpu.PrefetchScalarGridSpec(
        num_scalar_prefetch=1, grid=(4, h // tr),
        in_specs=[pl.BlockSpec((None, None, tr, C), lambda k, i, c: (k, c[0], i, 0)),
                  pl.BlockSpec((None, tr, C), lambda k, i, c: (k, i, 0))],
        out_specs=pl.BlockSpec((None, tr, C), lambda k, i, c: (k, i, 0)))
    return pl.pallas_call(
        body, name=name, grid_spec=grid_spec, out_shape=_sds((4, h, C), BF16),
        compiler_params=_params(("parallel", "parallel")),
    )(c_idx, g4, recv)


def _sum4(name, q, p, chip_idx):
    _, h, C = q.shape
    tr = h if h <= 512 else h // (h // 256) if h % 256 == 0 else h // 2

    def body(k_ref, q_ref, p_ref, o_ref):
        me = k_ref[0]
        terms = [jnp.where(me == k, p_ref[...], q_ref[k]).astype(F32) for k in range(4)]
        o_ref[...] = ((terms[0] + terms[1]) + terms[2]) + terms[3]

    grid_spec = pltpu.PrefetchScalarGridSpec(
        num_scalar_prefetch=1, grid=(h // tr,),
        in_specs=[pl.BlockSpec((4, tr, C), lambda i, k: (0, i, 0)),
                  pl.BlockSpec((None, tr, C), lambda i, k: (k[0], i, 0))],
        out_specs=pl.BlockSpec((tr, C), lambda i, k: (i, 0)))
    return pl.pallas_call(
        body, name=name, grid_spec=grid_spec, out_shape=_sds((h, C), F32), compiler_params=_params(("parallel",)),
    )(chip_idx, q, p)


def _adamw(name, w, g, m, v):
    R, C = w.shape
    tr = R
    while tr * C * 4 > (1 << 20) and tr % 16 == 0:
        tr //= 2
    c1 = 1.0 - ADAM_B1 ** ADAM_STEP
    c2 = 1.0 - ADAM_B2 ** ADAM_STEP

    def body(w_ref, g_ref, m_ref, v_ref, d_ref, nm_ref, nv_ref):
        gv = g_ref[...]
        nm = ADAM_B1 * m_ref[...] + (1.0 - ADAM_B1) * gv
        nv = ADAM_B2 * v_ref[...] + (1.0 - ADAM_B2) * (gv * gv)
        nm_ref[...] = nm
        nv_ref[...] = nv
        d_ref[...] = -ADAM_LR * ((nm * (1.0 / c1)) / (jnp.sqrt(nv * (1.0 / c2)) + ADAM_EPS) + ADAM_WD * w_ref[...])

    spec = pl.BlockSpec((tr, C), lambda i: (i, 0))
    return pl.pallas_call(
        body, name=name, grid=(R // tr,), in_specs=[spec] * 4, out_specs=[spec] * 3,
        out_shape=[_sds((R, C), F32)] * 3, compiler_params=_params(("parallel",)),
    )(w, g, m, v)


def _adamw_halves(name, w, own, sib, m, v, c_idx):
    R, C = w.shape
    h = R // 2
    tr = h
    while tr * C * 4 > (1 << 20) and tr % 16 == 0:
        tr //= 2
    nb = h // tr
    c1 = 1.0 - ADAM_B1 ** ADAM_STEP
    c2 = 1.0 - ADAM_B2 ** ADAM_STEP

    def body(c_ref, w_ref, own_ref, sib_ref, m_ref, v_ref, g_ref, d_ref, nm_ref, nv_ref):
        mine = (pl.program_id(0) // nb) == c_ref[0]
        gv = jnp.where(mine, own_ref[...], sib_ref[...])
        nm = ADAM_B1 * m_ref[...] + (1.0 - ADAM_B1) * gv
        nv = ADAM_B2 * v_ref[...] + (1.0 - ADAM_B2) * (gv * gv)
        g_ref[...] = gv
        nm_ref[...] = nm
        nv_ref[...] = nv
        d_ref[...] = -ADAM_LR * ((nm * (1.0 / c1)) / (jnp.sqrt(nv * (1.0 / c2)) + ADAM_EPS) + ADAM_WD * w_ref[...])

    spec = pl.BlockSpec((tr, C), lambda i, c: (i, 0))
    half = pl.BlockSpec((tr, C), lambda i, c: (i % nb, 0))
    grid_spec = pltpu.PrefetchScalarGridSpec(
        num_scalar_prefetch=1, grid=(R // tr,), in_specs=[spec, half, half, spec, spec], out_specs=[spec] * 4)
    return pl.pallas_call(
        body, name=name, grid_spec=grid_spec, out_shape=[_sds((R, C), F32)] * 4,
        compiler_params=_params(("parallel",)),
    )(c_idx, w, own, sib, m, v)


def _pack(g, scale, shift, gate):
    rows = jnp.stack([g, scale, shift, gate]).astype(F32)
    return jnp.concatenate([rows, jnp.zeros((4, D), F32)], axis=0)


def _pad8(vec):
    return jnp.concatenate([vec[None, :], jnp.zeros((7, vec.shape[0]), vec.dtype)], axis=0)


def kernel(x, c, w_ada, b_ada, norm1_g, ffn1_w_gu, ffn1_w_down, norm2_g, w_mix_in, b_merge, conv_w, w_conv_out, w_attn_out, w_out, norm3_g, ffn2_w_gu, ffn2_w_down, final_g, loss_target, m_w_ada, m_b_ada, m_norm1_g, m_ffn1_w_gu, m_ffn1_w_down, m_norm2_g, m_w_mix_in, m_b_merge, m_conv_w, m_w_conv_out, m_w_attn_out, m_w_out, m_norm3_g, m_ffn2_w_gu, m_ffn2_w_down, m_final_g, v_w_ada, v_b_ada, v_norm1_g, v_ffn1_w_gu, v_ffn1_w_down, v_norm2_g, v_w_mix_in, v_b_merge, v_conv_w, v_w_conv_out, v_w_attn_out, v_w_out, v_norm3_g, v_ffn2_w_gu, v_ffn2_w_down, v_final_g):
    xi, yi, ci = lax.axis_index("x"), lax.axis_index("y"), lax.axis_index("c")
    chip = 2 * xi + yi
    dev = 4 * xi + 2 * yi + ci
    S = x.shape[1]
    h0 = x[0]
    target = loss_target[0]

    wgu1, wd1 = _allgather_weights([w[0].astype(BF16) for w in (ffn1_w_gu, ffn1_w_down)])
    wd1 = wd1.reshape(D_FF, D)
    late_shards = [w[0].astype(BF16) for w in (w_conv_out, w_attn_out, w_out, ffn2_w_gu, ffn2_w_down)]
    c_idx = jnp.reshape(ci, (1,)).astype(jnp.int32)
    chip_idx = jnp.reshape(chip, (1,)).astype(jnp.int32)

    def reduce_pairs(tag, names, grads):
        g4 = [g.reshape(4, 2, g.shape[1] // 2, g.shape[2]) for g in grads]
        recv = _sibling_swap_halves("grad_sibling_swap_" + tag, g4)
        return [_pair_sum("pair_sum_" + nm, a, b, c_idx) for nm, a, b in zip(names, g4, recv)]

    small = jnp.concatenate([c[0], b_merge[0].reshape(-1), conv_w[0].reshape(-1)])
    gathered = _allgather_rows("allgather_small", _pad8(small)).reshape(8, 8, -1)[:, 0, :]
    c_all = gathered[:, :D]
    per_chip = gathered[0::2]
    bm_full = jnp.concatenate([per_chip[k, D:D + 512].reshape(2, 256) for k in range(4)], axis=1)
    cw_full = jnp.concatenate([per_chip[k, D + 512:].reshape(3, 128) for k in range(4)], axis=1)
    b_l = lax.dynamic_slice_in_dim(b_ada, chip * ADA_SHARD, ADA_SHARD, axis=1)
    mod_l = _ada_fwd(c_all, w_ada[0], b_l)
    mod_g = _allgather_rows("allgather_mod", mod_l).reshape(8, 8, ADA_SHARD)
    mod_all = jnp.concatenate([mod_g[2 * k] for k in range(4)], axis=1)
    mod = lax.dynamic_slice_in_dim(mod_all, dev, 1, axis=0).reshape(3, 3, D)
    p1 = _pack(norm1_g[0], mod[0, 1], mod[0, 0], mod[0, 2])
    p2 = _pack(norm2_g[0], mod[1, 1], mod[1, 0], mod[1, 2])
    p3 = _pack(norm3_g[0], mod[2, 1], mod[2, 0], mod[2, 2])
    pf = _pack(final_g, final_g, final_g, final_g)

    u1 = _norm_mod_fwd("norm1_fwd", h0, p1)
    gu1, hm1, (wmix,) = _ffn_up("ffn1_up", u1, wgu1, [w_mix_in[0].astype(BF16)])
    f1, h1 = _proj_residual("ffn1_down", hm1, wd1, h0, p1, 0.5)
    u2 = _norm_mod_fwd("norm2_fwd", h1, p2)
    proj = _mm("mix_in", (4, S // TM), u2, pl.BlockSpec((TM, D), lambda s, i: (i, 0)),
               wmix, pl.BlockSpec((None, D, MIX_SHARD), lambda s, i: (s, 0, 0)), (1, 0),
               [_sds((S, MIX_W), BF16)], [pl.BlockSpec((TM, MIX_SHARD), lambda s, i: (i, s))], _store(BF16),
               semantics=("parallel", "parallel"))[0]
    sc = _conv_fwd(proj, cw_full)
    o, t_tot, (wco, wao, wout, wgu2, wd2) = _attn_fwd(proj, late_shards)
    wout = wout.reshape(D, D)
    wd2 = wd2.reshape(D_FF, D)
    ya, yb, merged = _merge_fwd(sc, o, wco, wao, proj, bm_full)
    y2, h2 = _proj_residual("mix_out", merged, wout, h1, p2, 1.0)
    u3 = _norm_mod_fwd("norm3_fwd", h2, p3)
    gu3, hm3, _ = _ffn_up("ffn2_up", u3, wgu2)
    f3, h3 = _proj_residual("ffn2_down", hm3, wd2, h2, p3, 0.5)

    dh3, df3, sums_f, loss_blk = _final_loss_bwd(h3, pf, target, p3, f3)
    loss = lax.psum(loss_blk[0, 0], AXES)
    du3, dwgu2, dwd2 = _ffn_bwd("ffn2", df3, u3, gu3, hm3, wgu2, wd2)
    dh2, dy2, sums3 = _norm_mod_bwd("norm3_bwd", du3, h2, p3, dh3, prev=(p2, y2, 1.0))

    dya, dyb, dga, dgb, sums_bm = _merge_bwd(dy2, wout, ya, yb, proj, bm_full)
    dwout = _mm("dw_out", (1, S // TK), merged, pl.BlockSpec((TK, D), lambda n, k: (k, 0)),
                dy2, pl.BlockSpec((TK, D), lambda n, k: (k, 0)), (0, 0),
                [_sds((D, D), F32)], [pl.BlockSpec((D, D), lambda n, k: (0, 0))], _store(F32),
                nk=S // TK, acc_shape=(D, D))[0]
    dsc = _back_through_cols("conv_out_bwd", dya, wco, 512)
    do = _back_through_cols("attn_out_bwd", dyb, wao, 512)
    dwco = _grad_w("dw_conv_out", sc, 512, dya, 256, lambda s, k: (k, s), 4, (4, 512, 256), (None, 512, 256),
                   lambda s, k: (s, 0, 0))
    dwao = _grad_w("dw_attn_out", o, 512, dyb, 256, lambda s, k: (k, s), 4, (4, 512, 256), (None, 512, 256),
                   lambda s, k: (s, 0, 0))
    dcb, dcc, dcx, dcw = _conv_bwd(dsc, proj, cw_full)
    names_e = ["ffn2_w_gu", "ffn2_w_down", "w_out", "w_conv_out", "w_attn_out"]
    part_e = reduce_pairs("early", names_e, [dwgu2, dwd2.reshape(4, 704, D), dwout.reshape(4, 256, D), dwco, dwao])
    dq, dk, dv, came_e = _attn_bwd(proj, t_tot, do, part_e)
    dproj = jnp.concatenate([dcb, dcc, dcx, dq, dk, dv, dga, dgb], axis=1)
    du2 = _mm("mix_in_bwd", (S // TM, 4), dproj, pl.BlockSpec((TM, MIX_SHARD), lambda i, s: (i, s)),
              wmix, pl.BlockSpec((None, D, MIX_SHARD), lambda i, s: (s, 0, 0)), (1, 1),
              [_sds((S, D), F32)], [pl.BlockSpec((TM, D), lambda i, s: (i, 0))], _store(F32),
              nk=4, acc_shape=(TM, D), semantics=("parallel", "arbitrary"))[0]
    dwmix = _grad_w("dw_mix_in", u2, D, dproj, MIX_SHARD, lambda s, k: (k, s), 4, (4, D, MIX_SHARD),
                    (None, D, MIX_SHARD), lambda s, k: (s, 0, 0))
    dh1, df1, sums2 = _norm_mod_bwd("norm2_bwd", du2, h1, p2, dh2, prev=(p1, f1, 0.5))

    part_mix = reduce_pairs("mix", ["w_mix_in"], [dwmix])
    dgu1, *came_mix = _ffn_down_bwd("ffn1_down_bwd", df1, wd1, gu1, a2a_parts=part_mix)
    dwd1 = _ffn_dw_down("ffn1_dw_down", hm1, df1)
    part_wd1 = reduce_pairs("wd1", ["ffn1_w_down"], [dwd1.reshape(4, 704, D)])
    dwgu1, *came_wd1 = _ffn_dw_gu("ffn1_dw_gu", u1, dgu1, a2a_parts=part_wd1)
    part_gu1 = reduce_pairs("gu1", ["ffn1_w_gu"], [dwgu1])
    du1, *came_gu1 = _ffn_up_bwd("ffn1_up_bwd", dgu1, wgu1, a2a_parts=part_gu1)
    grad_x, sums1 = _norm_mod_bwd("norm1_bwd", du1, h0, p1, dh1)

    dmod = jnp.stack([sums1[0], sums1[1], sums2[3], sums2[0], sums2[1], sums3[3], sums3[0], sums3[1], sums_f[1]])
    small_g = jnp.concatenate([dmod.reshape(-1), sums1[2], sums2[2], sums3[2], sums_f[0],
                               sums_bm[0], sums_bm[1], dcw.reshape(-1)])
    all_g = _allgather_rows("allgather_small_grads", _pad8(small_g)).reshape(8, 8, -1)[:, 0, :]
    tot = _sum_rows("sum_small_grads", all_g)[0]
    g_b_ada = tot[:9 * D][None, :]
    g_n1, g_n2, g_n3 = (tot[(9 + k) * D:(10 + k) * D][None, :] for k in range(3))
    g_fin = tot[12 * D:13 * D]
    g_bm = lax.dynamic_slice_in_dim(tot[13 * D:15 * D].reshape(2, D), chip * 256, 256, axis=1)[None]
    g_cw = lax.dynamic_slice_in_dim(tot[15 * D:].reshape(3, 512), chip * 128, 128, axis=1)[None]
    dmod_l = lax.dynamic_slice_in_dim(all_g[:, :9 * D], chip * ADA_SHARD, ADA_SHARD, axis=1)
    g_w_ada = _ada_bwd(c_all.T, dmod_l)[None]

    names = names_e + ["w_mix_in", "ffn1_w_down", "ffn1_w_gu"]
    came = list(came_e) + came_mix + came_wd1 + came_gu1
    part = part_e + part_mix + part_wd1 + part_gu1
    half = [_sum4("chip_sum_" + nm, q, p, chip_idx) for nm, q, p in zip(names, came, part)]
    g_own = dict(zip(names, half))
    g_sib = dict(zip(names, _sibling_share(half)))

    weights = dict(w_ada=w_ada, b_ada=b_ada, norm1_g=norm1_g, ffn1_w_gu=ffn1_w_gu, ffn1_w_down=ffn1_w_down,
                   norm2_g=norm2_g, w_mix_in=w_mix_in, b_merge=b_merge, conv_w=conv_w, w_conv_out=w_conv_out,
                   w_attn_out=w_attn_out, w_out=w_out, norm3_g=norm3_g, ffn2_w_gu=ffn2_w_gu,
                   ffn2_w_down=ffn2_w_down, final_g=final_g)
    ms = dict(w_ada=m_w_ada, b_ada=m_b_ada, norm1_g=m_norm1_g, ffn1_w_gu=m_ffn1_w_gu, ffn1_w_down=m_ffn1_w_down,
              norm2_g=m_norm2_g, w_mix_in=m_w_mix_in, b_merge=m_b_merge, conv_w=m_conv_w, w_conv_out=m_w_conv_out,
              w_attn_out=m_w_attn_out, w_out=m_w_out, norm3_g=m_norm3_g, ffn2_w_gu=m_ffn2_w_gu,
              ffn2_w_down=m_ffn2_w_down, final_g=m_final_g)
    vs = dict(w_ada=v_w_ada, b_ada=v_b_ada, norm1_g=v_norm1_g, ffn1_w_gu=v_ffn1_w_gu, ffn1_w_down=v_ffn1_w_down,
              norm2_g=v_norm2_g, w_mix_in=v_w_mix_in, b_merge=v_b_merge, conv_w=v_conv_w, w_conv_out=v_w_conv_out,
              w_attn_out=v_w_attn_out, w_out=v_w_out, norm3_g=v_norm3_g, ffn2_w_gu=v_ffn2_w_gu,
              ffn2_w_down=v_ffn2_w_down, final_g=v_final_g)
    order = list(weights)
    grad = dict(w_ada=g_w_ada, b_ada=g_b_ada, norm1_g=g_n1, norm2_g=g_n2, norm3_g=g_n3, final_g=g_fin,
                b_merge=g_bm, conv_w=g_cw)
    delta, new_m, new_v = {}, {}, {}
    small_names = ["b_ada", "norm1_g", "norm2_g", "norm3_g", "final_g", "b_merge", "conv_w"]
    flat = lambda d: jnp.concatenate([d[nm].reshape(-1) for nm in small_names])[None, :]
    sd, sm, sv = _adamw("adamw_small", flat(weights), flat(grad), flat(ms), flat(vs))
    off = 0
    for nm in small_names:
        size = weights[nm].size
        for dst, src in ((delta, sd), (new_m, sm), (new_v, sv)):
            dst[nm] = src[0, off:off + size].reshape(weights[nm].shape)
        off += size
    for nm in order:
        if nm in small_names:
            continue
        shp = weights[nm].shape
        if nm in g_own:
            g2, d2, m2, v2 = _adamw_halves("adamw_" + nm, weights[nm][0], g_own[nm], g_sib[nm], ms[nm][0], vs[nm][0],
                                           c_idx)
            grad[nm] = g2.reshape(shp)
        else:
            d2, m2, v2 = _adamw("adamw_" + nm, weights[nm][0], grad[nm][0], ms[nm][0], vs[nm][0])
        delta[nm], new_m[nm], new_v[nm] = d2.reshape(shp), m2.reshape(shp), v2.reshape(shp)

    return (loss, grad_x[None], *[grad[nm] for nm in order], *[delta[nm] for nm in order],
            *[new_m[nm] for nm in order], *[new_v[nm] for nm in order])
```

```python
import functools

import jax
import jax.numpy as jnp
from jax import lax
from jax.experimental import pallas as pl
from jax.experimental.pallas import tpu as pltpu

F32 = jnp.float32
BF16 = jnp.bfloat16
MESH = pl.DeviceIdType.MESH
AXES = ("x", "y", "c")

VMEM_LIMIT = 56 * 1024 * 1024
LANES = 128

D = 1024
D_FF = 2816
FF_SHARD = 1408
MIX_SHARD = 1280
MIX_W = 5120
HEAD_PAIRS = 4
HEAD_DIM = 64
CONV_W = 512
EPS = 1e-6
ATT_BLK = 256

ADAM_LR = 0.001
ADAM_B1 = 0.9
ADAM_B2 = 0.999
ADAM_EPS = 1e-08
ADAM_WD = 0.01
ADAM_STEP = 10


def _params(semantics=None):
    return pltpu.CompilerParams(dimension_semantics=semantics, vmem_limit_bytes=VMEM_LIMIT)


def _sigmoid(x):
    return 1.0 / (1.0 + jnp.exp(-x))


def _place():
    x, y, c = lax.axis_index("x"), lax.axis_index("y"), lax.axis_index("c")
    chips = [(1 - x, y), (x, 1 - y), (1 - x, 1 - y)]
    return x, y, c, chips


def _allgather_rows(name, blk):
    m_per, n = blk.shape

    def body(x_ref, out_ref, send_sems, recv_sems, local_sem):
        x, y, c, chips = _place()
        me, sibling = (x, y, c), (x, y, 1 - c)

        def rows(px, py, pc):
            return out_ref.at[pl.ds((4 * px + 2 * py + pc) * m_per, m_per), :]

        def copy(k, block, to, src=None):
            return pltpu.make_async_remote_copy(
                src_ref=rows(*block) if src is None else src, dst_ref=rows(*block),
                send_sem=send_sems.at[k], recv_sem=recv_sems.at[k], device_id=to, device_id_type=MESH)

        mine = pltpu.make_async_copy(x_ref, rows(*me), local_sem)
        mine.start()
        first = [copy(0, me, sibling, src=x_ref)]
        first += [copy(1 + j, me, (*chip, c), src=x_ref) for j, chip in enumerate(chips)]
        for cp in first:
            cp.start()
        passed = [copy(4 + j, (*chip, c), sibling) for j, chip in enumerate(chips)]
        for j, chip in enumerate(chips):
            copy(1 + j, (*chip, c), me).wait_recv()
            passed[j].start()
        copy(0, sibling, me).wait_recv()
        for j, chip in enumerate(chips):
            copy(4 + j, (*chip, 1 - c), me).wait_recv()
        for cp in first + passed:
            cp.wait_send()
        mine.wait()

    return pl.pallas_call(
        body, name=name,
        out_shape=jax.ShapeDtypeStruct((8 * m_per, n), blk.dtype),
        in_specs=[pl.BlockSpec(memory_space=pltpu.VMEM)],
        out_specs=pl.BlockSpec(memory_space=pltpu.VMEM),
        scratch_shapes=[pltpu.SemaphoreType.DMA((7,)), pltpu.SemaphoreType.DMA((7,)), pltpu.SemaphoreType.DMA],
        compiler_params=pltpu.CompilerParams(vmem_limit_bytes=VMEM_LIMIT),
    )(blk)


def _hbm_specs(n):
    return [pl.BlockSpec(memory_space=pltpu.HBM)] * n


def _allgather_weights(shards):
    n = len(shards)

    def body(*refs):
        start, relay, finish = _gather_protocol(refs[:n], refs[n:2 * n], *refs[2 * n:])
        start()
        relay()
        finish()

    gathered = pl.pallas_call(
        body, name="allgather_weights",
        out_shape=_gather_shapes(shards), in_specs=_hbm_specs(n), out_specs=_hbm_specs(n),
        scratch_shapes=_gather_sems(n),
    )(*shards)
    return _with_own_shard(gathered, shards)


def _gather_shapes(shards):
    return [jax.ShapeDtypeStruct((4, *s.shape), s.dtype) for s in shards]


def _gather_sems(n):
    return [pltpu.SemaphoreType.DMA((6 * n,)), pltpu.SemaphoreType.DMA((6 * n,))]


def _with_own_shard(gathered, shards):
    chip = 2 * lax.axis_index("x") + lax.axis_index("y")
    return [lax.dynamic_update_slice(g, s[None], (chip, 0, 0)) for g, s in zip(gathered, shards)]


def _gather_protocol(ins, outs, send_sems, recv_sems):
    n = len(ins)
    x, y, c, chips = _place()
    me, sibling = (x, y, c), (x, y, 1 - c)
    me_k = 2 * x + y

    def half(w, k, hc):
        h = ins[w].shape[0] // 2
        return outs[w].at[k, pl.ds(pl.multiple_of(hc * h, 8), h), :]

    def copy(w, j, k, hc, to, src=None):
        dst = half(w, k, hc)
        return pltpu.make_async_remote_copy(
            src_ref=dst if src is None else src, dst_ref=dst,
            send_sem=send_sems.at[6 * w + j], recv_sem=recv_sems.at[6 * w + j],
            device_id=to, device_id_type=MESH)

    def first(w, j):
        h = ins[w].shape[0] // 2
        src = ins[w].at[pl.ds(pl.multiple_of(c * h, 8), h), :]
        return copy(w, j, me_k, c, (*chips[j], c), src=src)

    def passed(w, j):
        px, py = chips[j]
        return copy(w, 3 + j, 2 * px + py, c, sibling)

    pairs = [(w, j) for w in range(n) for j in range(3)]

    def start():
        for w, j in pairs:
            first(w, j).start()

    def relay():
        for w, j in pairs:
            px, py = chips[j]
            copy(w, j, 2 * px + py, c, me).wait_recv()
            passed(w, j).start()

    def finish():
        for w, j in pairs:
            px, py = chips[j]
            copy(w, 3 + j, 2 * px + py, 1 - c, me).wait_recv()
        for w, j in pairs:
            first(w, j).wait_send()
            passed(w, j).wait_send()

    return start, relay, finish


def _sibling_swap_halves(name, grads):
    n = len(grads)

    def body(*refs):
        ins, outs = refs[:n], refs[n:2 * n]
        send_sems, recv_sems = refs[2 * n:]
        x, y, c, _ = _place()
        cps = []
        for w in range(n):
            cp = pltpu.make_async_remote_copy(
                src_ref=ins[w].at[:, 1 - c], dst_ref=outs[w],
                send_sem=send_sems.at[w], recv_sem=recv_sems.at[w],
                device_id=(x, y, 1 - c), device_id_type=MESH)
            cp.start()
            cps.append(cp)
        for cp in cps:
            cp.wait()

    return pl.pallas_call(
        body, name=name,
        out_shape=[jax.ShapeDtypeStruct((4, *g.shape[2:]), g.dtype) for g in grads],
        in_specs=_hbm_specs(n), out_specs=_hbm_specs(n),
        scratch_shapes=[pltpu.SemaphoreType.DMA((n,)), pltpu.SemaphoreType.DMA((n,))],
    )(*grads)


def _all_to_all_sems(n):
    return [pltpu.SemaphoreType.DMA((3 * n,)), pltpu.SemaphoreType.DMA((3 * n,))]


def _all_to_all_protocol(ins, outs, send_sems, recv_sems):
    n = len(ins)
    x, y, c, chips = _place()
    me_k = 2 * x + y
    pairs = [(w, j) for w in range(n) for j in range(3)]

    def sent(w, j):
        px, py = chips[j]
        return pltpu.make_async_remote_copy(
            src_ref=ins[w].at[2 * px + py], dst_ref=outs[w].at[me_k],
            send_sem=send_sems.at[3 * w + j], recv_sem=recv_sems.at[3 * w + j],
            device_id=(px, py, c), device_id_type=MESH)

    def start():
        for w, j in pairs:
            sent(w, j).start()

    def finish():
        for w, j in pairs:
            px, py = chips[j]
            slab = outs[w].at[2 * px + py]
            pltpu.make_async_remote_copy(
                src_ref=slab, dst_ref=slab, send_sem=send_sems.at[3 * w + j],
                recv_sem=recv_sems.at[3 * w + j], device_id=(px, py, c), device_id_type=MESH).wait_recv()
        for w, j in pairs:
            sent(w, j).wait_send()

    return start, finish


def _sibling_share(halves):
    n = len(halves)

    def body(*refs):
        ins, outs = refs[:n], refs[n:2 * n]
        send_sems, recv_sems = refs[2 * n:]
        x, y, c, _ = _place()
        cps = []
        for w in range(n):
            cp = pltpu.make_async_remote_copy(
                src_ref=ins[w], dst_ref=outs[w], send_sem=send_sems.at[w], recv_sem=recv_sems.at[w],
                device_id=(x, y, 1 - c), device_id_type=MESH)
            cp.start()
            cps.append(cp)
        for cp in cps:
            cp.wait()

    return pl.pallas_call(
        body, name="grad_sibling_share",
        out_shape=[jax.ShapeDtypeStruct(p.shape, p.dtype) for p in halves],
        in_specs=_hbm_specs(n), out_specs=_hbm_specs(n),
        scratch_shapes=[pltpu.SemaphoreType.DMA((n,)), pltpu.SemaphoreType.DMA((n,))],
    )(*halves)


def _mm(name, grid, a, a_spec, b, b_spec, contract, out_shapes, out_specs, epilogue,
        extras=(), extra_specs=(), nk=1, acc_shape=None, semantics=None, a2a_parts=()):
    ne, no, nc = len(extras), len(out_shapes), len(a2a_parts)
    nd = len(grid)

    def body(*refs):
        a_ref, b_ref = refs[0], refs[1]
        ex, outs = refs[2:2 + ne], refs[2 + ne + nc:2 + ne + nc + no]
        if nc:
            ids = [pl.program_id(d) for d in range(nd)]
            start, finish = _all_to_all_protocol(refs[2 + ne:2 + ne + nc],
                                                 refs[2 + ne + nc + no:2 + ne + 2 * nc + no], *refs[-2:])
            pl.when(functools.reduce(jnp.logical_and, [i == 0 for i in ids]))(start)

        def prod():
            return lax.dot_general(a_ref[...], b_ref[...], (((contract[0],), (contract[1],)), ((), ())),
                                   preferred_element_type=F32)

        if nk == 1:
            epilogue(prod(), ex, outs)
        else:
            acc = refs[2 + ne + 2 * nc + no]
            k = pl.program_id(nd - 1)

            @pl.when(k == 0)
            def _():
                acc[...] = prod()

            @pl.when(k > 0)
            def _():
                acc[...] += prod()

            @pl.when(k == nk - 1)
            def _():
                epilogue(acc[...], ex, outs)

        if nc:
            pl.when(functools.reduce(jnp.logical_and, [i == g - 1 for i, g in zip(ids, grid)]))(finish)

    if semantics is None or nc:
        semantics = ("arbitrary",) * nd
    return pl.pallas_call(
        body, name=name, grid=grid,
        in_specs=[a_spec, b_spec, *extra_specs] + _hbm_specs(nc),
        out_specs=list(out_specs) + _hbm_specs(nc),
        out_shape=list(out_shapes) + [jax.ShapeDtypeStruct(p.shape, p.dtype) for p in a2a_parts],
        scratch_shapes=([] if nk == 1 else [pltpu.VMEM(acc_shape, F32)]) + (_all_to_all_sems(nc) if nc else []),
        compiler_params=_params(semantics),
    )(a, b, *extras, *a2a_parts)


def _store(dtype):
    def epilogue(acc, ex, outs):
        outs[0][...] = acc.astype(dtype)
    return epilogue


def _sds(shape, dtype):
    return jax.ShapeDtypeStruct(shape, dtype)


TR = 512


def _row_spec(width, tr=TR):
    return pl.BlockSpec((tr, width), lambda i: (i, 0))


def _const_spec(shape):
    nd = len(shape)
    return pl.BlockSpec(shape, lambda i: (0,) * nd)


def _norm_mod_fwd(name, h, p):
    S = h.shape[0]

    def body(h_ref, p_ref, u_ref):
        hv = h_ref[...]
        r = lax.rsqrt(jnp.mean(hv * hv, axis=-1, keepdims=True) + EPS)
        nrm = (hv * r) * p_ref[0:1, :]
        u_ref[...] = (nrm * (1.0 + p_ref[1:2, :]) + p_ref[2:3, :]).astype(BF16)

    return pl.pallas_call(
        body, name=name, grid=(S // TR,),
        in_specs=[_row_spec(D), _const_spec((8, D))], out_specs=_row_spec(D),
        out_shape=_sds((S, D), BF16), compiler_params=_params(("parallel",)),
    )(h, p)


def _final_loss_bwd(h, gf, target, p3, f3):
    S = h.shape[0]

    def body(h_ref, g_ref, t_ref, p_ref, f_ref, dh_ref, df_ref, sums_ref, loss_ref):
        i = pl.program_id(0)

        @pl.when(i == 0)
        def _():
            sums_ref[...] = jnp.zeros_like(sums_ref)
            loss_ref[...] = jnp.zeros_like(loss_ref)

        hv = h_ref[...]
        g = g_ref[0:1, :]
        r = lax.rsqrt(jnp.mean(hv * hv, axis=-1, keepdims=True) + EPS)
        xn = hv * r
        err = xn * g - t_ref[...]
        loss_ref[...] += 0.5 * jnp.sum(err * err) * (1.0 / D)
        dout = err * (1.0 / D)
        dxn = dout * g
        dh = r * (dxn - xn * jnp.mean(dxn * xn, axis=-1, keepdims=True))
        dh_ref[...] = dh
        gate = p_ref[3:4, :]
        df_ref[...] = (0.5 * gate * dh).astype(BF16)
        sums_ref[0:1, :] += jnp.sum(dout * xn, axis=0, keepdims=True)
        sums_ref[1:2, :] += 0.5 * jnp.sum(dh * f_ref[...].astype(F32), axis=0, keepdims=True)

    return pl.pallas_call(
        body, name="final_loss_bwd", grid=(S // TR,),
        in_specs=[_row_spec(D), _const_spec((8, D)), _row_spec(D), _const_spec((8, D)), _row_spec(D)],
        out_specs=[_row_spec(D), _row_spec(D), _const_spec((8, D)), _const_spec((8, LANES))],
        out_shape=[_sds((S, D), F32), _sds((S, D), BF16), _sds((8, D), F32), _sds((8, LANES), F32)],
        compiler_params=_params(("arbitrary",)),
    )(h, gf, target, p3, f3)


def _norm_mod_bwd(name, du, h, p, dh_res, prev=None):
    S = h.shape[0]
    has_prev = prev is not None

    def body(*refs):
        if has_prev:
            du_ref, h_ref, p_ref, r_ref, pp_ref, f_ref, dh_ref, df_ref, sums_ref = refs
        else:
            du_ref, h_ref, p_ref, r_ref, dh_ref, sums_ref = refs
        i = pl.program_id(0)

        @pl.when(i == 0)
        def _():
            sums_ref[...] = jnp.zeros_like(sums_ref)

        hv = h_ref[...]
        duv = du_ref[...]
        g = p_ref[0:1, :]
        one_scale = 1.0 + p_ref[1:2, :]
        r = lax.rsqrt(jnp.mean(hv * hv, axis=-1, keepdims=True) + EPS)
        xn = hv * r
        dn = duv * one_scale
        dxn = dn * g
        dh = r_ref[...] + r * (dxn - xn * jnp.mean(dxn * xn, axis=-1, keepdims=True))
        dh_ref[...] = dh
        sums_ref[0:1, :] += jnp.sum(duv, axis=0, keepdims=True)
        sums_ref[1:2, :] += jnp.sum(duv * (xn * g), axis=0, keepdims=True)
        sums_ref[2:3, :] += jnp.sum(dn * xn, axis=0, keepdims=True)
        if has_prev:
            wgt = prev[2]
            df_ref[...] = (wgt * pp_ref[3:4, :] * dh).astype(BF16)
            sums_ref[3:4, :] += wgt * jnp.sum(dh * f_ref[...].astype(F32), axis=0, keepdims=True)

    ins = [du, h, p, dh_res]
    in_specs = [_row_spec(D), _row_spec(D), _const_spec((8, D)), _row_spec(D)]
    out_specs = [_row_spec(D)]
    out_shape = [_sds((S, D), F32)]
    if has_prev:
        ins += [prev[0], prev[1]]
        in_specs += [_const_spec((8, D)), _row_spec(D)]
        out_specs.append(_row_spec(D))
        out_shape.append(_sds((S, D), BF16))
    out_specs.append(_const_spec((8, D)))
    out_shape.append(_sds((8, D), F32))
    return pl.pallas_call(
        body, name=name, grid=(S // TR,), in_specs=in_specs, out_specs=out_specs, out_shape=out_shape,
        compiler_params=_params(("arbitrary",)),
    )(*ins)


TM = 512


def _ffn_up(name, u, wgu4, shards=()):
    S = u.shape[0]
    n = len(shards)
    ni = S // TM

    def body(u_ref, wg_ref, wu_ref, *rest):
        gu_ref, hm_ref = rest[n:n + 2]
        s, i = pl.program_id(0), pl.program_id(1)
        if n:
            start, relay, finish = _gather_protocol(rest[:n], rest[n + 2:2 * n + 2], *rest[2 * n + 2:])
            pl.when((s == 0) & (i == 0))(start)
            pl.when((s == 1) & (i == 0))(relay)
        uv = u_ref[...]
        g = jnp.dot(uv, wg_ref[...], preferred_element_type=F32)
        up = jnp.dot(uv, wu_ref[...], preferred_element_type=F32)
        gu_ref[0] = g.astype(BF16)
        gu_ref[1] = up.astype(BF16)
        hm_ref[...] = (g * _sigmoid(g) * up).astype(BF16)
        if n:
            pl.when((s == 1) & (i == ni - 1))(finish)

    gu, hm, *gathered = pl.pallas_call(
        body, name=name, grid=(2, ni),
        in_specs=[pl.BlockSpec((TM, D), lambda s, i: (i, 0)),
                  pl.BlockSpec((None, D, FF_SHARD), lambda s, i: (s, 0, 0)),
                  pl.BlockSpec((None, D, FF_SHARD), lambda s, i: (s + 2, 0, 0))] + _hbm_specs(n),
        out_specs=[pl.BlockSpec((2, TM, FF_SHARD), lambda s, i: (0, i, s)),
                   pl.BlockSpec((TM, FF_SHARD), lambda s, i: (i, s))] + _hbm_specs(n),
        out_shape=[_sds((2, S, D_FF), BF16), _sds((S, D_FF), BF16)] + _gather_shapes(shards),
        scratch_shapes=_gather_sems(n) if n else [],
        compiler_params=_params(("arbitrary", "arbitrary") if n else ("parallel", "parallel")),
    )(u, wgu4, wgu4, *shards)
    return gu, hm, _with_own_shard(gathered, shards)


def _proj_residual(name, a, w, h, p, weight):
    S, K = a.shape

    def epilogue(acc, ex, outs):
        h_ref, p_ref = ex
        outs[0][...] = acc.astype(BF16)
        outs[1][...] = h_ref[...] + weight * p_ref[3:4, :] * acc

    return _mm(
        name, (S // TM,), a, pl.BlockSpec((TM, K), lambda i: (i, 0)), w, pl.BlockSpec((K, D), lambda i: (0, 0)),
        (1, 0), [_sds((S, D), BF16), _sds((S, D), F32)], [_row_spec(D, TM), _row_spec(D, TM)], epilogue,
        extras=(h, p), extra_specs=(_row_spec(D, TM), _const_spec((8, D))), semantics=("parallel",))


def _ffn_down_bwd(name, df, wd, gu, a2a_parts=()):
    S = df.shape[0]

    def epilogue(acc, ex, outs):
        g = ex[0][0].astype(F32)
        up = ex[0][1].astype(F32)
        sg = _sigmoid(g)
        outs[0][0] = (acc * up * (sg * (1.0 + g * (1.0 - sg)))).astype(BF16)
        outs[0][1] = (acc * g * sg).astype(BF16)

    gu_spec = pl.BlockSpec((2, TM, FF_SHARD), lambda n, i: (0, i, n))
    return _mm(
        name, (2, S // TM), df, pl.BlockSpec((TM, D), lambda n, i: (i, 0)),
        wd, pl.BlockSpec((FF_SHARD, D), lambda n, i: (n, 0)), (1, 1),
        [_sds((2, S, D_FF), BF16)], [gu_spec], epilogue, extras=(gu,), extra_specs=(gu_spec,),
        semantics=("parallel", "parallel"), a2a_parts=a2a_parts)


TK = 512


def _grad_w(name, a, a_w, b, b_w, b_map, n_out, out_shape, out_block, out_map, a2a_parts=()):
    S = a.shape[0]
    nk = S // TK
    res = _mm(
        name, (n_out, nk), a, pl.BlockSpec((TK, a_w), lambda s, k: (k, 0)), b, pl.BlockSpec(
            (None, TK, b_w) if b.ndim == 3 else (TK, b_w), b_map), (0, 0),
        [_sds(out_shape, F32)], [pl.BlockSpec(out_block, out_map)], _store(F32), nk=nk, acc_shape=(a_w, b_w),
        semantics=("parallel", "arbitrary"), a2a_parts=a2a_parts)
    return res if a2a_parts else res[0]


def _ffn_dw_down(name, hm, df):
    S = df.shape[0]
    return _mm(
        name, (2, S // TK), hm, pl.BlockSpec((TK, FF_SHARD), lambda m, k: (k, m)),
        df, pl.BlockSpec((TK, D), lambda m, k: (k, 0)), (0, 0),
        [_sds((D_FF, D), F32)], [pl.BlockSpec((FF_SHARD, D), lambda m, k: (m, 0))], _store(F32),
        nk=S // TK, acc_shape=(FF_SHARD, D), semantics=("parallel", "arbitrary"))[0]


def _ffn_up_bwd(name, dgu, wgu4, a2a_parts=()):
    S = dgu.shape[1]
    return _mm(
        name, (S // TM, 4), dgu, pl.BlockSpec((None, TM, FF_SHARD), lambda i, s: (s // 2, i, s % 2)),
        wgu4, pl.BlockSpec((None, D, FF_SHARD), lambda i, s: (s, 0, 0)), (1, 1),
        [_sds((S, D), F32)], [pl.BlockSpec((TM, D), lambda i, s: (i, 0))], _store(F32),
        nk=4, acc_shape=(TM, D), semantics=("parallel", "arbitrary"), a2a_parts=a2a_parts)


def _ffn_dw_gu(name, u_in, dgu, a2a_parts=()):
    return _grad_w(name, u_in, D, dgu, FF_SHARD, lambda s, k: (s // 2, k, s % 2), 4,
                   (4, D, FF_SHARD), (None, D, FF_SHARD), lambda s, k: (s, 0, 0), a2a_parts=a2a_parts)


def _ffn_bwd(tag, df, u_in, gu, hm, wgu4, wd):
    dgu = _ffn_down_bwd(tag + "_down_bwd", df, wd, gu)[0]
    dwd = _ffn_dw_down(tag + "_dw_down", hm, df)
    du = _ffn_up_bwd(tag + "_up_bwd", dgu, wgu4)[0]
    dwgu = _ffn_dw_gu(tag + "_dw_gu", u_in, dgu)
    return du, dwgu, dwd


def _shift_down(v, k, row):
    return jnp.where(row >= k, pltpu.roll(v, k, axis=0), 0.0)


def _shift_up(v, k, row, S):
    return jnp.where(row < S - k, pltpu.roll(v, S - k, axis=0), 0.0)


def _conv_specs(S):
    cols = CONV_W // LANES
    return [pl.BlockSpec((S, LANES), functools.partial(lambda j, off: (0, off + j), off=o * cols))
            for o in range(3)]


def _conv_fwd(proj, conv_w):
    S = proj.shape[0]

    def body(cb_ref, cc_ref, cx_ref, w_ref, sc_ref):
        row = lax.broadcasted_iota(jnp.int32, (S, LANES), 0)
        v = cc_ref[...].astype(F32) * cx_ref[...].astype(F32)
        yv = w_ref[0:1, :] * _shift_down(v, 2, row) + w_ref[1:2, :] * _shift_down(v, 1, row) + w_ref[2:3, :] * v
        sc_ref[...] = (cb_ref[...].astype(F32) * yv).astype(BF16)

    return pl.pallas_call(
        body, name="conv_fwd", grid=(CONV_W // LANES,),
        in_specs=_conv_specs(S) + [pl.BlockSpec((3, LANES), lambda j: (0, j))],
        out_specs=pl.BlockSpec((S, LANES), lambda j: (0, j)), out_shape=_sds((S, CONV_W), BF16),
        compiler_params=_params(("parallel",)),
    )(proj, proj, proj, conv_w)


def _conv_bwd(dsc, proj, conv_w):
    S = proj.shape[0]

    def body(d_ref, cb_ref, cc_ref, cx_ref, w_ref, dcb_ref, dcc_ref, dcx_ref, dw_ref):
        row = lax.broadcasted_iota(jnp.int32, (S, LANES), 0)
        cc = cc_ref[...].astype(F32)
        cx = cx_ref[...].astype(F32)
        d = d_ref[...].astype(F32)
        v = cc * cx
        v1 = _shift_down(v, 1, row)
        v2 = _shift_down(v, 2, row)
        w0, w1, w2 = w_ref[0:1, :], w_ref[1:2, :], w_ref[2:3, :]
        dcb_ref[...] = (d * (w0 * v2 + w1 * v1 + w2 * v)).astype(BF16)
        dy = d * cb_ref[...].astype(F32)
        dw_ref[0:1, :] = jnp.sum(dy * v2, axis=0, keepdims=True)
        dw_ref[1:2, :] = jnp.sum(dy * v1, axis=0, keepdims=True)
        dw_ref[2:3, :] = jnp.sum(dy * v, axis=0, keepdims=True)
        dv = w2 * dy + w1 * _shift_up(dy, 1, row, S) + w0 * _shift_up(dy, 2, row, S)
        dcc_ref[...] = (dv * cx).astype(BF16)
        dcx_ref[...] = (dv * cc).astype(BF16)

    col = pl.BlockSpec((S, LANES), lambda j: (0, j))
    return pl.pallas_call(
        body, name="conv_bwd", grid=(CONV_W // LANES,),
        in_specs=[col] + _conv_specs(S) + [pl.BlockSpec((3, LANES), lambda j: (0, j))],
        out_specs=[col, col, col, pl.BlockSpec((3, LANES), lambda j: (0, j))],
        out_shape=[_sds((S, CONV_W), BF16)] * 3 + [_sds((3, CONV_W), F32)],
        compiler_params=_params(("parallel",)),
    )(dsc, proj, proj, proj, conv_w)


Q_COL, K_COL, V_COL = 1536 // LANES, 2048 // LANES, 2560 // LANES


def _split_dot(x, tri):
    hi = x.astype(BF16)
    lo = (x - hi.astype(F32)).astype(BF16)
    return jnp.dot(hi, tri, preferred_element_type=F32) + jnp.dot(lo, tri, preferred_element_type=F32)


def _tri_dot(tri, x):
    hi = x.astype(BF16)
    lo = (x - hi.astype(F32)).astype(BF16)
    return jnp.dot(tri, hi, preferred_element_type=F32) + jnp.dot(tri, lo, preferred_element_type=F32)


def _softplus(z):
    return jnp.maximum(z, 0.0) + jnp.log(1.0 + jnp.exp(-jnp.abs(z)))


def _nt(a, b):
    return lax.dot_general(a, b, (((1,), (1,)), ((), ())), preferred_element_type=F32)


def _tn(a, b):
    return lax.dot_general(a, b, (((0,), (0,)), ((), ())), preferred_element_type=F32)


def _interleave(gens, delays):
    results = [None] * len(gens)
    live = list(range(len(gens)))
    rnd = 0
    while live:
        for g in list(live):
            if rnd < delays[g]:
                continue
            try:
                next(gens[g])
            except StopIteration as stop:
                results[g] = stop.value
                live.remove(g)
        rnd += 1
    return results


def _attn_fwd(proj, shards):
    S = proj.shape[0]
    B = ATT_BLK
    nq = S // B
    n = len(shards)

    def body(q_ref, k_ref, v_ref, *rest):
        o_ref, t_ref = rest[n:n + 2]
        start, relay, finish = _gather_protocol(rest[:n], rest[n + 2:2 * n + 2], *rest[2 * n + 2:])
        p = pl.program_id(0)
        i = pl.program_id(1)
        pl.when((p == 0) & (i == 0))(start)
        pl.when((p == HEAD_PAIRS // 2) & (i == 0))(relay)
        lo_lane = lax.broadcasted_iota(jnp.int32, (B, LANES), 1) < HEAD_DIM
        row = lax.broadcasted_iota(jnp.int32, (B, B), 0)
        col = lax.broadcasted_iota(jnp.int32, (B, B), 1)
        after = (row > col).astype(BF16)
        causal = col < row
        q2 = q_ref[...] * 0.125
        zero = jnp.zeros((), BF16)
        q_heads = (jnp.where(lo_lane, q2, zero), jnp.where(lo_lane, zero, q2))

        def head_tile(q_h, st, kb, diag):
            k2 = k_ref[pl.ds(pl.multiple_of(kb * B, B), B), :]
            z = _nt(q_h, k2)
            yield
            spz = _softplus(z)
            sp = jnp.where(causal, spz, 0.0) if diag else spz
            hi = sp.astype(BF16)
            lo = (sp - hi.astype(F32)).astype(BF16)
            r = st["r"]
            st["r"] = r + jnp.sum(sp, axis=1, keepdims=True)
            yield
            rem = jnp.dot(hi, after, preferred_element_type=F32) + jnp.dot(lo, after, preferred_element_type=F32)
            yield
            a = jnp.exp(z - spz - (rem + r))
            if diag:
                a = jnp.where(causal, a, 0.0)
            ab = a.astype(BF16)
            yield
            v2 = v_ref[pl.ds(pl.multiple_of(kb * B, B), B), :]
            st["acc"] = st["acc"] + jnp.dot(ab, v2, preferred_element_type=F32)

        def tiles(kbs, carry, diag):
            sts = [dict(r=carry[0], acc=carry[1]), dict(r=carry[2], acc=carry[3])]
            gens = [head_tile(q_h, st, kb, diag) for kb in kbs for q_h, st in zip(q_heads, sts)]
            _interleave(gens, [t for t in range(len(kbs)) for _ in q_heads])
            return sts[0]["r"], sts[0]["acc"], sts[1]["r"], sts[1]["acc"]

        zr, za = jnp.zeros((B, 1), F32), jnp.zeros((B, LANES), F32)
        carry = tiles([i], (zr, za, zr, za), True)
        carry = lax.fori_loop(0, i % 2, lambda j, cr: tiles([i - 1], cr, False), carry)
        first = i - 1 - i % 2
        ra, acc_a, rb, acc_b = lax.fori_loop(
            0, i // 2, lambda j, cr: tiles([first - 2 * j, first - 2 * j - 1], cr, False), carry)
        o_ref[...] = jnp.where(lo_lane, acc_a, acc_b).astype(BF16)
        t_ref[...] = jnp.where(lo_lane, ra, rb).T
        pl.when((p == HEAD_PAIRS - 1) & (i == nq - 1))(finish)

    seq = lambda off: pl.BlockSpec((S, LANES), lambda p, i: (0, off + p))
    blk = pl.BlockSpec((B, LANES), lambda p, i: (i, p))
    o, t, *gathered = pl.pallas_call(
        body, name="attn_fwd", grid=(HEAD_PAIRS, nq),
        in_specs=[pl.BlockSpec((B, LANES), lambda p, i: (i, Q_COL + p)), seq(K_COL), seq(V_COL)] + _hbm_specs(n),
        out_specs=[blk, pl.BlockSpec((LANES, B), lambda p, i: (p, i))] + _hbm_specs(n),
        out_shape=[_sds((S, 512), BF16), _sds((512, S), F32)] + _gather_shapes(shards),
        scratch_shapes=_gather_sems(n),
        compiler_params=_params(("arbitrary", "arbitrary")),
    )(proj, proj, proj, *shards)
    return o, t, _with_own_shard(gathered, shards)


def _attn_bwd(proj, t, do, parts):
    S = proj.shape[0]
    kt = proj[:, K_COL * LANES:V_COL * LANES].T
    B = ATT_BLK
    nq = S // B
    n = len(parts)

    def body(q_ref, k_ref, v_ref, kt_ref, t_ref, do_ref, *rest):
        dq_ref, dk_ref, dv_ref = rest[n:n + 3]
        dk_acc, dv_acc = rest[2 * n + 3:2 * n + 5]
        start, finish = _all_to_all_protocol(rest[:n], rest[n + 3:2 * n + 3], *rest[2 * n + 5:])
        i = pl.program_id(1)
        pl.when((pl.program_id(0) == 0) & (i == 0))(start)

        @pl.when(i == 0)
        def _():
            dk_acc[...] = jnp.zeros_like(dk_acc)
            dv_acc[...] = jnp.zeros_like(dv_acc)

        lo_lane = lax.broadcasted_iota(jnp.int32, (B, LANES), 1) < HEAD_DIM
        key = lax.broadcasted_iota(jnp.int32, (B, B), 0)
        qry = lax.broadcasted_iota(jnp.int32, (B, B), 1)
        upto = (qry <= key).astype(BF16)
        before = (qry < key).astype(BF16)
        causal = key < qry
        zero = jnp.zeros((), BF16)
        q2 = q_ref[...] * 0.125
        do2 = do_ref[...]
        heads = ((jnp.where(lo_lane, q2, zero), jnp.where(lo_lane, do2, zero), t_ref[0:1, :]),
                 (jnp.where(lo_lane, zero, q2), jnp.where(lo_lane, zero, do2), t_ref[HEAD_DIM:HEAD_DIM + 1, :]))

        def head_tile(head, st, kb, diag):
            q_h, do_h, t_h = head
            rows = pl.ds(pl.multiple_of(kb * B, B), B)
            z = _nt(k_ref[rows, :], q_h)
            da = _nt(v_ref[rows, :], do_h)
            yield
            spz = _softplus(z)
            sp = jnp.where(causal, spz, 0.0) if diag else spz
            hi = sp.astype(BF16)
            lo = (sp - hi.astype(F32)).astype(BF16)
            pc = st["pc"]
            st["pc"] = pc + jnp.sum(sp, axis=0, keepdims=True)
            yield
            pref = jnp.dot(upto, hi, preferred_element_type=F32) + jnp.dot(upto, lo, preferred_element_type=F32)
            yield
            a = jnp.exp(z - spz - ((t_h - pc) - pref))
            if diag:
                a = jnp.where(causal, a, 0.0)
            e = a * da
            eb = e.astype(BF16)
            ab = a.astype(BF16)
            ec = st["ec"]
            st["ec"] = ec + jnp.sum(e, axis=0, keepdims=True)
            yield
            e_before = ec + jnp.dot(before, eb, preferred_element_type=F32)
            yield
            u = jnp.exp(-spz)
            dz = u * (e + e_before) - e_before
            if diag:
                dz = jnp.where(causal, dz, 0.0)
            dzb = dz.astype(BF16)
            yield
            st["dqt"] = st["dqt"] + jnp.dot(kt_ref[:, rows], dzb, preferred_element_type=F32)
            return (jnp.dot(dzb, q_h, preferred_element_type=F32), jnp.dot(ab, do_h, preferred_element_type=F32))

        def tiles(kbs, carry, diag):
            sts = [dict(pc=carry[3 * h], ec=carry[3 * h + 1], dqt=carry[3 * h + 2]) for h in range(2)]
            gens = [head_tile(hd, st, kb, diag) for kb in kbs for hd, st in zip(heads, sts)]
            res = _interleave(gens, [t for t in range(len(kbs)) for _ in heads])
            for t, kb in enumerate(kbs):
                rows = pl.ds(pl.multiple_of(kb * B, B), B)
                (dk_a, dv_a), (dk_b, dv_b) = res[2 * t], res[2 * t + 1]
                dk_acc[rows, :] += dk_a + dk_b
                dv_acc[rows, :] += dv_a + dv_b
            return tuple(st[nm] for st in sts for nm in ("pc", "ec", "dqt"))

        zc, zq = jnp.zeros((1, B), F32), jnp.zeros((LANES, B), F32)
        carry = lax.fori_loop(0, i // 2, lambda j, cr: tiles([2 * j, 2 * j + 1], cr, False), (zc, zc, zq, zc, zc, zq))
        carry = lax.fori_loop(0, i % 2, lambda j, cr: tiles([i - 1], cr, False), carry)
        _, _, dqt_a, _, _, dqt_b = tiles([i], carry, True)
        head0 = lax.broadcasted_iota(jnp.int32, (LANES, B), 0) < HEAD_DIM
        dq_ref[...] = (jnp.where(head0, dqt_a, dqt_b).T * 0.125).astype(BF16)

        @pl.when(i == nq - 1)
        def _():
            dk_ref[...] = dk_acc[...].astype(BF16)
            dv_ref[...] = dv_acc[...].astype(BF16)

        pl.when((pl.program_id(0) == HEAD_PAIRS - 1) & (i == nq - 1))(finish)

    seq = lambda off: pl.BlockSpec((S, LANES), lambda p, i: (0, off + p))
    blk = pl.BlockSpec((B, LANES), lambda p, i: (i, p))
    whole = pl.BlockSpec((S, LANES), lambda p, i: (0, p))
    dq, dk, dv, *came = pl.pallas_call(
        body, name="attn_bwd", grid=(HEAD_PAIRS, nq),
        in_specs=[pl.BlockSpec((B, LANES), lambda p, i: (i, Q_COL + p)), seq(K_COL), seq(V_COL),
                  pl.BlockSpec((LANES, S), lambda p, i: (p, 0)), pl.BlockSpec((LANES, B), lambda p, i: (p, i)), blk]
        + _hbm_specs(n),
        out_specs=[blk, whole, whole] + _hbm_specs(n),
        out_shape=[_sds((S, 512), BF16)] * 3 + [jax.ShapeDtypeStruct(p.shape, p.dtype) for p in parts],
        scratch_shapes=[pltpu.VMEM((S, LANES), F32), pltpu.VMEM((S, LANES), F32)] + _all_to_all_sems(n),
        compiler_params=_params(("arbitrary", "arbitrary")),
    )(proj, proj, proj, kt, t, do, *parts)
    return dq, dk, dv, came


GA_COL, GB_COL = 3072 // 256, 4096 // 256


def _merge_fwd(sc, o, wco4, wao4, proj, bm):
    S = sc.shape[0]

    def body(sc_ref, o_ref, wc_ref, wa_ref, ga_ref, gb_ref, bm_ref, ya_ref, yb_ref, mg_ref):
        ya = jnp.dot(sc_ref[...], wc_ref[...], preferred_element_type=F32)
        yb = jnp.dot(o_ref[...], wa_ref[...], preferred_element_type=F32)
        sa = _sigmoid(ga_ref[...].astype(F32) + bm_ref[0:1, :])
        sb = _sigmoid(gb_ref[...].astype(F32) + bm_ref[1:2, :])
        ya_ref[...] = ya.astype(BF16)
        yb_ref[...] = yb.astype(BF16)
        mg_ref[...] = (sa * ya + sb * yb).astype(BF16)

    wide = pl.BlockSpec((TM, 512), lambda n, i: (i, 0))
    wsp = pl.BlockSpec((None, 512, 256), lambda n, i: (n, 0, 0))
    out = pl.BlockSpec((TM, 256), lambda n, i: (i, n))
    return pl.pallas_call(
        body, name="merge_fwd", grid=(4, S // TM),
        in_specs=[wide, wide, wsp, wsp, pl.BlockSpec((TM, 256), lambda n, i: (i, GA_COL + n)),
                  pl.BlockSpec((TM, 256), lambda n, i: (i, GB_COL + n)), pl.BlockSpec((2, 256), lambda n, i: (0, n))],
        out_specs=[out, out, out], out_shape=[_sds((S, D), BF16)] * 3,
        compiler_params=_params(("parallel", "parallel")),
    )(sc, o, wco4, wao4, proj, proj, bm)


def _merge_bwd(dy2, wout, ya, yb, proj, bm):
    S = dy2.shape[0]

    def epilogue(acc, ex, outs):
        ya_ref, yb_ref, ga_ref, gb_ref, bm_ref = ex
        i = pl.program_id(1)
        sa = _sigmoid(ga_ref[...].astype(F32) + bm_ref[0:1, :])
        sb = _sigmoid(gb_ref[...].astype(F32) + bm_ref[1:2, :])
        dga = acc * ya_ref[...].astype(F32) * (sa * (1.0 - sa))
        dgb = acc * yb_ref[...].astype(F32) * (sb * (1.0 - sb))
        outs[0][...] = (acc * sa).astype(BF16)
        outs[1][...] = (acc * sb).astype(BF16)
        outs[2][...] = dga.astype(BF16)
        outs[3][...] = dgb.astype(BF16)

        @pl.when(i == 0)
        def _():
            outs[4][...] = jnp.zeros_like(outs[4])

        outs[4][0:1, :] += jnp.sum(dga, axis=0, keepdims=True)
        outs[4][1:2, :] += jnp.sum(dgb, axis=0, keepdims=True)

    out = pl.BlockSpec((TM, 256), lambda n, i: (i, n))
    return _mm(
        "merge_bwd", (4, S // TM), dy2, pl.BlockSpec((TM, D), lambda n, i: (i, 0)),
        wout, pl.BlockSpec((256, D), lambda n, i: (n, 0)), (1, 1),
        [_sds((S, D), BF16)] * 4 + [_sds((8, D), F32)], [out, out, out, out, pl.BlockSpec((8, 256), lambda n, i: (0, n))],
        epilogue, extras=(ya, yb, proj, proj, bm),
        extra_specs=(out, out, pl.BlockSpec((TM, 256), lambda n, i: (i, GA_COL + n)),
                     pl.BlockSpec((TM, 256), lambda n, i: (i, GB_COL + n)), pl.BlockSpec((2, 256), lambda n, i: (0, n))),
        semantics=("parallel", "arbitrary"))


def _back_through_cols(name, dy, w4, width):
    S = dy.shape[0]
    return _mm(
        name, (S // TM, 4), dy, pl.BlockSpec((TM, 256), lambda i, s: (i, s)),
        w4, pl.BlockSpec((None, width, 256), lambda i, s: (s, 0, 0)), (1, 1),
        [_sds((S, width), BF16)], [pl.BlockSpec((TM, width), lambda i, s: (i, 0))], _store(BF16),
        nk=4, acc_shape=(TM, width), semantics=("parallel", "arbitrary"))[0]


ADA_SHARD = 2304
ADA_TN = 768


def _ada_fwd(c_all, w_ada_l, b_l):
    def body(c_ref, w_ref, b_ref, o_ref):
        cv = c_ref[...]
        ca = cv * _sigmoid(cv)
        o_ref[...] = jnp.dot(ca.astype(BF16), w_ref[...].astype(BF16), preferred_element_type=F32) + b_ref[...]

    return pl.pallas_call(
        body, name="ada_fwd", grid=(ADA_SHARD // ADA_TN,),
        in_specs=[pl.BlockSpec((8, D), lambda j: (0, 0)), pl.BlockSpec((D, ADA_TN), lambda j: (0, j)),
                  pl.BlockSpec((1, ADA_TN), lambda j: (0, j))],
        out_specs=pl.BlockSpec((8, ADA_TN), lambda j: (0, j)), out_shape=_sds((8, ADA_SHARD), F32),
        compiler_params=_params(("parallel",)),
    )(c_all, w_ada_l, b_l)


def _ada_bwd(c_all_t, dmod_l):
    def body(c_ref, d_ref, o_ref):
        cv = c_ref[...]
        ca = cv * _sigmoid(cv)
        o_ref[...] = jnp.dot(ca.astype(BF16).astype(F32), d_ref[...].astype(BF16).astype(F32),
                             preferred_element_type=F32, precision=lax.Precision.HIGHEST)

    return pl.pallas_call(
        body, name="ada_bwd", grid=(ADA_SHARD // ADA_TN,),
        in_specs=[pl.BlockSpec((D, 8), lambda j: (0, 0)), pl.BlockSpec((8, ADA_TN), lambda j: (0, j))],
        out_specs=pl.BlockSpec((D, ADA_TN), lambda j: (0, j)), out_shape=_sds((D, ADA_SHARD), F32),
        compiler_params=_params(("parallel",)),
    )(c_all_t, dmod_l)


def _sum_rows(name, x):
    n = x.shape[1]

    def body(x_ref, o_ref):
        s = x_ref[0:1, :]
        for d in range(1, 8):
            s = s + x_ref[d:d + 1, :]
        o_ref[...] = s

    return pl.pallas_call(
        body, name=name, in_specs=[pl.BlockSpec(memory_space=pltpu.VMEM)],
        out_specs=pl.BlockSpec(memory_space=pltpu.VMEM), out_shape=_sds((1, n), F32),
        compiler_params=pltpu.CompilerParams(vmem_limit_bytes=VMEM_LIMIT),
    )(x)


def _pair_sum(name, g4, recv, c_idx):
    _, _, h, C = g4.shape
    tr = h if h <= 512 else h // (h // 256) if h % 256 == 0 else h // 2

    def body(c_ref, g_ref, r_ref, o_ref):
        o_ref[...] = (g_ref[...] + r_ref[...]).astype(BF16)

    grid_spec = pltpu.PrefetchScalarGridSpec(
        num_scalar_prefetch=1, grid=(4, h // tr),
        in_specs=[pl.BlockSpec((None, None, tr, C), lambda k, i, c: (k, c[0], i, 0)),
                  pl.BlockSpec((None, tr, C), lambda k, i, c: (k, i, 0))],
        out_specs=pl.BlockSpec((None, tr, C), lambda k, i, c: (k, i, 0)))
    return pl.pallas_call(
        body, name=name, grid_spec=grid_spec, out_shape=_sds((4, h, C), BF16),
        compiler_params=_params(("parallel", "parallel")),
    )(c_idx, g4, recv)


def _sum4(name, q, p, chip_idx):
    _, h, C = q.shape
    tr = h if h <= 512 else h // (h // 256) if h % 256 == 0 else h // 2

    def body(k_ref, q_ref, p_ref, o_ref):
        me = k_ref[0]
        terms = [jnp.where(me == k, p_ref[...], q_ref[k]).astype(F32) for k in range(4)]
        o_ref[...] = ((terms[0] + terms[1]) + terms[2]) + terms[3]

    grid_spec = pltpu.PrefetchScalarGridSpec(
        num_scalar_prefetch=1, grid=(h // tr,),
        in_specs=[pl.BlockSpec((4, tr, C), lambda i, k: (0, i, 0)),
                  pl.BlockSpec((None, tr, C), lambda i, k: (k[0], i, 0))],
        out_specs=pl.BlockSpec((tr, C), lambda i, k: (i, 0)))
    return pl.pallas_call(
        body, name=name, grid_spec=grid_spec, out_shape=_sds((h, C), F32), compiler_params=_params(("parallel",)),
    )(chip_idx, q, p)


def _adamw(name, w, g, m, v):
    R, C = w.shape
    tr = R
    while tr * C * 4 > (1 << 20) and tr % 16 == 0:
        tr //= 2
    c1 = 1.0 - ADAM_B1 ** ADAM_STEP
    c2 = 1.0 - ADAM_B2 ** ADAM_STEP

    def body(w_ref, g_ref, m_ref, v_ref, d_ref, nm_ref, nv_ref):
        gv = g_ref[...]
        nm = ADAM_B1 * m_ref[...] + (1.0 - ADAM_B1) * gv
        nv = ADAM_B2 * v_ref[...] + (1.0 - ADAM_B2) * (gv * gv)
        nm_ref[...] = nm
        nv_ref[...] = nv
        d_ref[...] = -ADAM_LR * ((nm * (1.0 / c1)) / (jnp.sqrt(nv * (1.0 / c2)) + ADAM_EPS) + ADAM_WD * w_ref[...])

    spec = pl.BlockSpec((tr, C), lambda i: (i, 0))
    return pl.pallas_call(
        body, name=name, grid=(R // tr,), in_specs=[spec] * 4, out_specs=[spec] * 3,
        out_shape=[_sds((R, C), F32)] * 3, compiler_params=_params(("parallel",)),
    )(w, g, m, v)


def _adamw_halves(name, w, own, sib, m, v, c_idx):
    R, C = w.shape
    h = R // 2
    tr = h
    while tr * C * 4 > (1 << 20) and tr % 16 == 0:
        tr //= 2
    nb = h // tr
    c1 = 1.0 - ADAM_B1 ** ADAM_STEP
    c2 = 1.0 - ADAM_B2 ** ADAM_STEP

    def body(c_ref, w_ref, own_ref, sib_ref, m_ref, v_ref, g_ref, d_ref, nm_ref, nv_ref):
        mine = (pl.program_id(0) // nb) == c_ref[0]
        gv = jnp.where(mine, own_ref[...], sib_ref[...])
        nm = ADAM_B1 * m_ref[...] + (1.0 - ADAM_B1) * gv
        nv = ADAM_B2 * v_ref[...] + (1.0 - ADAM_B2) * (gv * gv)
        g_ref[...] = gv
        nm_ref[...] = nm
        nv_ref[...] = nv
        d_ref[...] = -ADAM_LR * ((nm * (1.0 / c1)) / (jnp.sqrt(nv * (1.0 / c2)) + ADAM_EPS) + ADAM_WD * w_ref[...])

    spec = pl.BlockSpec((tr, C), lambda i, c: (i, 0))
    half = pl.BlockSpec((tr, C), lambda i, c: (i % nb, 0))
    grid_spec = pltpu.PrefetchScalarGridSpec(
        num_scalar_prefetch=1, grid=(R // tr,), in_specs=[spec, half, half, spec, spec], out_specs=[spec] * 4)
    return pl.pallas_call(
        body, name=name, grid_spec=grid_spec, out_shape=[_sds((R, C), F32)] * 4,
        compiler_params=_params(("parallel",)),
    )(c_idx, w, own, sib, m, v)


def _pack(g, scale, shift, gate):
    rows = jnp.stack([g, scale, shift, gate]).astype(F32)
    return jnp.concatenate([rows, jnp.zeros((4, D), F32)], axis=0)


def _pad8(vec):
    return jnp.concatenate([vec[None, :], jnp.zeros((7, vec.shape[0]), vec.dtype)], axis=0)


def kernel(x, c, w_ada, b_ada, norm1_g, ffn1_w_gu, ffn1_w_down, norm2_g, w_mix_in, b_merge, conv_w, w_conv_out, w_attn_out, w_out, norm3_g, ffn2_w_gu, ffn2_w_down, final_g, loss_target, m_w_ada, m_b_ada, m_norm1_g, m_ffn1_w_gu, m_ffn1_w_down, m_norm2_g, m_w_mix_in, m_b_merge, m_conv_w, m_w_conv_out, m_w_attn_out, m_w_out, m_norm3_g, m_ffn2_w_gu, m_ffn2_w_down, m_final_g, v_w_ada, v_b_ada, v_norm1_g, v_ffn1_w_gu, v_ffn1_w_down, v_norm2_g, v_w_mix_in, v_b_merge, v_conv_w, v_w_conv_out, v_w_attn_out, v_w_out, v_norm3_g, v_ffn2_w_gu, v_ffn2_w_down, v_final_g):
    xi, yi, ci = lax.axis_index("x"), lax.axis_index("y"), lax.axis_index("c")
    chip = 2 * xi + yi
    dev = 4 * xi + 2 * yi + ci
    S = x.shape[1]
    h0 = x[0]
    target = loss_target[0]

    wgu1, wd1 = _allgather_weights([w[0].astype(BF16) for w in (ffn1_w_gu, ffn1_w_down)])
    wd1 = wd1.reshape(D_FF, D)
    late_shards = [w[0].astype(BF16) for w in (w_conv_out, w_attn_out, w_out, ffn2_w_gu, ffn2_w_down)]
    c_idx = jnp.reshape(ci, (1,)).astype(jnp.int32)
    chip_idx = jnp.reshape(chip, (1,)).astype(jnp.int32)

    def reduce_pairs(tag, names, grads):
        g4 = [g.reshape(4, 2, g.shape[1] // 2, g.shape[2]) for g in grads]
        recv = _sibling_swap_halves("grad_sibling_swap_" + tag, g4)
        return [_pair_sum("pair_sum_" + nm, a, b, c_idx) for nm, a, b in zip(names, g4, recv)]

    small = jnp.concatenate([c[0], b_merge[0].reshape(-1), conv_w[0].reshape(-1)])
    gathered = _allgather_rows("allgather_small", _pad8(small)).reshape(8, 8, -1)[:, 0, :]
    c_all = gathered[:, :D]
    per_chip = gathered[0::2]
    bm_full = jnp.concatenate([per_chip[k, D:D + 512].reshape(2, 256) for k in range(4)], axis=1)
    cw_full = jnp.concatenate([per_chip[k, D + 512:].reshape(3, 128) for k in range(4)], axis=1)
    b_l = lax.dynamic_slice_in_dim(b_ada, chip * ADA_SHARD, ADA_SHARD, axis=1)
    mod_l = _ada_fwd(c_all, w_ada[0], b_l)
    mod_g = _allgather_rows("allgather_mod", mod_l).reshape(8, 8, ADA_SHARD)
    mod_all = jnp.concatenate([mod_g[2 * k] for k in range(4)], axis=1)
    mod = lax.dynamic_slice_in_dim(mod_all, dev, 1, axis=0).reshape(3, 3, D)
    p1 = _pack(norm1_g[0], mod[0, 1], mod[0, 0], mod[0, 2])
    p2 = _pack(norm2_g[0], mod[1, 1], mod[1, 0], mod[1, 2])
    p3 = _pack(norm3_g[0], mod[2, 1], mod[2, 0], mod[2, 2])
    pf = _pack(final_g, final_g, final_g, final_g)

    u1 = _norm_mod_fwd("norm1_fwd", h0, p1)
    gu1, hm1, (wmix,) = _ffn_up("ffn1_up", u1, wgu1, [w_mix_in[0].astype(BF16)])
    f1, h1 = _proj_residual("ffn1_down", hm1, wd1, h0, p1, 0.5)
    u2 = _norm_mod_fwd("norm2_fwd", h1, p2)
    proj = _mm("mix_in", (4, S // TM), u2, pl.BlockSpec((TM, D), lambda s, i: (i, 0)),
               wmix, pl.BlockSpec((None, D, MIX_SHARD), lambda s, i: (s, 0, 0)), (1, 0),
               [_sds((S, MIX_W), BF16)], [pl.BlockSpec((TM, MIX_SHARD), lambda s, i: (i, s))], _store(BF16),
               semantics=("parallel", "parallel"))[0]
    sc = _conv_fwd(proj, cw_full)
    o, t_tot, (wco, wao, wout, wgu2, wd2) = _attn_fwd(proj, late_shards)
    wout = wout.reshape(D, D)
    wd2 = wd2.reshape(D_FF, D)
    ya, yb, merged = _merge_fwd(sc, o, wco, wao, proj, bm_full)
    y2, h2 = _proj_residual("mix_out", merged, wout, h1, p2, 1.0)
    u3 = _norm_mod_fwd("norm3_fwd", h2, p3)
    gu3, hm3, _ = _ffn_up("ffn2_up", u3, wgu2)
    f3, h3 = _proj_residual("ffn2_down", hm3, wd2, h2, p3, 0.5)

    dh3, df3, sums_f, loss_blk = _final_loss_bwd(h3, pf, target, p3, f3)
    loss = lax.psum(loss_blk[0, 0], AXES)
    du3, dwgu2, dwd2 = _ffn_bwd("ffn2", df3, u3, gu3, hm3, wgu2, wd2)
    dh2, dy2, sums3 = _norm_mod_bwd("norm3_bwd", du3, h2, p3, dh3, prev=(p2, y2, 1.0))

    dya, dyb, dga, dgb, sums_bm = _merge_bwd(dy2, wout, ya, yb, proj, bm_full)
    dwout = _mm("dw_out", (1, S // TK), merged, pl.BlockSpec((TK, D), lambda n, k: (k, 0)),
                dy2, pl.BlockSpec((TK, D), lambda n, k: (k, 0)), (0, 0),
                [_sds((D, D), F32)], [pl.BlockSpec((D, D), lambda n, k: (0, 0))], _store(F32),
                nk=S // TK, acc_shape=(D, D))[0]
    dsc = _back_through_cols("conv_out_bwd", dya, wco, 512)
    do = _back_through_cols("attn_out_bwd", dyb, wao, 512)
    dwco = _grad_w("dw_conv_out", sc, 512, dya, 256, lambda s, k: (k, s), 4, (4, 512, 256), (None, 512, 256),
                   lambda s, k: (s, 0, 0))
    dwao = _grad_w("dw_attn_out", o, 512, dyb, 256, lambda s, k: (k, s), 4, (4, 512, 256), (None, 512, 256),
                   lambda s, k: (s, 0, 0))
    dcb, dcc, dcx, dcw = _conv_bwd(dsc, proj, cw_full)
    names_e = ["ffn2_w_gu", "ffn2_w_down", "w_out", "w_conv_out", "w_attn_out"]
    part_e = reduce_pairs("early", names_e, [dwgu2, dwd2.reshape(4, 704, D), dwout.reshape(4, 256, D), dwco, dwao])
    dq, dk, dv, came_e = _attn_bwd(proj, t_tot, do, part_e)
    dproj = jnp.concatenate([dcb, dcc, dcx, dq, dk, dv, dga, dgb], axis=1)
    du2 = _mm("mix_in_bwd", (S // TM, 4), dproj, pl.BlockSpec((TM, MIX_SHARD), lambda i, s: (i, s)),
              wmix, pl.BlockSpec((None, D, MIX_SHARD), lambda i, s: (s, 0, 0)), (1, 1),
              [_sds((S, D), F32)], [pl.BlockSpec((TM, D), lambda i, s: (i, 0))], _store(F32),
              nk=4, acc_shape=(TM, D), semantics=("parallel", "arbitrary"))[0]
    dwmix = _grad_w("dw_mix_in", u2, D, dproj, MIX_SHARD, lambda s, k: (k, s), 4, (4, D, MIX_SHARD),
                    (None, D, MIX_SHARD), lambda s, k: (s, 0, 0))
    dh1, df1, sums2 = _norm_mod_bwd("norm2_bwd", du2, h1, p2, dh2, prev=(p1, f1, 0.5))

    part_mix = reduce_pairs("mix", ["w_mix_in"], [dwmix])
    dgu1, *came_mix = _ffn_down_bwd("ffn1_down_bwd", df1, wd1, gu1, a2a_parts=part_mix)
    dwd1 = _ffn_dw_down("ffn1_dw_down", hm1, df1)
    part_wd1 = reduce_pairs("wd1", ["ffn1_w_down"], [dwd1.reshape(4, 704, D)])
    dwgu1, *came_wd1 = _ffn_dw_gu("ffn1_dw_gu", u1, dgu1, a2a_parts=part_wd1)
    part_gu1 = reduce_pairs("gu1", ["ffn1_w_gu"], [dwgu1])
    du1, *came_gu1 = _ffn_up_bwd("ffn1_up_bwd", dgu1, wgu1, a2a_parts=part_gu1)
    grad_x, sums1 = _norm_mod_bwd("norm1_bwd", du1, h0, p1, dh1)

    dmod = jnp.stack([sums1[0], sums1[1], sums2[3], sums2[0], sums2[1], sums3[3], sums3[0], sums3[1], sums_f[1]])
    small_g = jnp.concatenate([dmod.reshape(-1), sums1[2], sums2[2], sums3[2], sums_f[0],
                               sums_bm[0], sums_bm[1], dcw.reshape(-1)])
    all_g = _allgather_rows("allgather_small_grads", _pad8(small_g)).reshape(8, 8, -1)[:, 0, :]
    tot = _sum_rows("sum_small_grads", all_g)[0]
    g_b_ada = tot[:9 * D][None, :]
    g_n1, g_n2, g_n3 = (tot[(9 + k) * D:(10 + k) * D][None, :] for k in range(3))
    g_fin = tot[12 * D:13 * D]
    g_bm = lax.dynamic_slice_in_dim(tot[13 * D:15 * D].reshape(2, D), chip * 256, 256, axis=1)[None]
    g_cw = lax.dynamic_slice_in_dim(tot[15 * D:].reshape(3, 512), chip * 128, 128, axis=1)[None]
    dmod_l = lax.dynamic_slice_in_dim(all_g[:, :9 * D], chip * ADA_SHARD, ADA_SHARD, axis=1)
    g_w_ada = _ada_bwd(c_all.T, dmod_l)[None]

    names = names_e + ["w_mix_in", "ffn1_w_down", "ffn1_w_gu"]
    came = list(came_e) + came_mix + came_wd1 + came_gu1
    part = part_e + part_mix + part_wd1 + part_gu1
    half = [_sum4("chip_sum_" + nm, q, p, chip_idx) for nm, q, p in zip(names, came, part)]
    g_own = dict(zip(names, half))
    g_sib = dict(zip(names, _sibling_share(half)))

    weights = dict(w_ada=w_ada, b_ada=b_ada, norm1_g=norm1_g, ffn1_w_gu=ffn1_w_gu, ffn1_w_down=ffn1_w_down,
                   norm2_g=norm2_g, w_mix_in=w_mix_in, b_merge=b_merge, conv_w=conv_w, w_conv_out=w_conv_out,
                   w_attn_out=w_attn_out, w_out=w_out, norm3_g=norm3_g, ffn2_w_gu=ffn2_w_gu,
                   ffn2_w_down=ffn2_w_down, final_g=final_g)
    ms = dict(w_ada=m_w_ada, b_ada=m_b_ada, norm1_g=m_norm1_g, ffn1_w_gu=m_ffn1_w_gu, ffn1_w_down=m_ffn1_w_down,
              norm2_g=m_norm2_g, w_mix_in=m_w_mix_in, b_merge=m_b_merge, conv_w=m_conv_w, w_conv_out=m_w_conv_out,
              w_attn_out=m_w_attn_out, w_out=m_w_out, norm3_g=m_norm3_g, ffn2_w_gu=m_ffn2_w_gu,
              ffn2_w_down=m_ffn2_w_down, final_g=m_final_g)
    vs = dict(w_ada=v_w_ada, b_ada=v_b_ada, norm1_g=v_norm1_g, ffn1_w_gu=v_ffn1_w_gu, ffn1_w_down=v_ffn1_w_down,
              norm2_g=v_norm2_g, w_mix_in=v_w_mix_in, b_merge=v_b_merge, conv_w=v_conv_w, w_conv_out=v_w_conv_out,
              w_attn_out=v_w_attn_out, w_out=v_w_out, norm3_g=v_norm3_g, ffn2_w_gu=v_ffn2_w_gu,
              ffn2_w_down=v_ffn2_w_down, final_g=v_final_g)
    order = list(weights)
    grad = dict(w_ada=g_w_ada, b_ada=g_b_ada, norm1_g=g_n1, norm2_g=g_n2, norm3_g=g_n3, final_g=g_fin,
                b_merge=g_bm, conv_w=g_cw)
    delta, new_m, new_v = {}, {}, {}
    small_names = ["b_ada", "norm1_g", "norm2_g", "norm3_g", "final_g", "b_merge", "conv_w"]
    flat = lambda d: jnp.concatenate([d[nm].reshape(-1) for nm in small_names])[None, :]
    sd, sm, sv = _adamw("adamw_small", flat(weights), flat(grad), flat(ms), flat(vs))
    off = 0
    for nm in small_names:
        size = weights[nm].size
        for dst, src in ((delta, sd), (new_m, sm), (new_v, sv)):
            dst[nm] = src[0, off:off + size].reshape(weights[nm].shape)
        off += size
    for nm in order:
        if nm in small_names:
            continue
        shp = weights[nm].shape
        if nm in g_own:
            g2, d2, m2, v2 = _adamw_halves("adamw_" + nm, weights[nm][0], g_own[nm], g_sib[nm], ms[nm][0], vs[nm][0],
                                           c_idx)
            grad[nm] = g2.reshape(shp)
        else:
            d2, m2, v2 = _adamw("adamw_" + nm, weights[nm][0], grad[nm][0], ms[nm][0], vs[nm][0])
        delta[nm], new_m[nm], new_v[nm] = d2.reshape(shp), m2.reshape(shp), v2.reshape(shp)

    return (loss, grad_x[None], *[grad[nm] for nm in order], *[delta[nm] for nm in order],
            *[new_m[nm] for nm in order], *[new_v[nm] for nm in order])
```

```python
import functools

import jax
import jax.numpy as jnp
from jax import lax
from jax.experimental import pallas as pl
from jax.experimental.pallas import tpu as pltpu

F32 = jnp.float32
BF16 = jnp.bfloat16
MESH = pl.DeviceIdType.MESH

VMEM_LIMIT = 56 * 1024 * 1024
LANES = 128

D = 1024
D_FF = 2816
FF_SHARD = 1408
MIX_SHARD = 1280
MIX_W = 5120
HEAD_PAIRS = 4
HEAD_DIM = 64
CONV_W = 512
EPS = 1e-6
ATT_BLK = 256

ADAM_LR = 0.001
ADAM_B1 = 0.9
ADAM_B2 = 0.999
ADAM_EPS = 1e-08
ADAM_WD = 0.01
ADAM_STEP = 10


def _params(semantics=None):
    return pltpu.CompilerParams(dimension_semantics=semantics, vmem_limit_bytes=VMEM_LIMIT)


def _sigmoid(x):
    return 1.0 / (1.0 + jnp.exp(-x))


def _place():
    x, y, c = lax.axis_index("x"), lax.axis_index("y"), lax.axis_index("c")
    chips = [(1 - x, y), (x, 1 - y), (1 - x, 1 - y)]
    return x, y, c, chips


def _allgather_rows(name, blk):
    m_per, n = blk.shape

    def body(x_ref, out_ref, send_sems, recv_sems, local_sem):
        x, y, c, chips = _place()
        me, sibling = (x, y, c), (x, y, 1 - c)

        def rows(px, py, pc):
            return out_ref.at[pl.ds((4 * px + 2 * py + pc) * m_per, m_per), :]

        def copy(k, block, to, src=None):
            return pltpu.make_async_remote_copy(
                src_ref=rows(*block) if src is None else src, dst_ref=rows(*block),
                send_sem=send_sems.at[k], recv_sem=recv_sems.at[k], device_id=to, device_id_type=MESH)

        mine = pltpu.make_async_copy(x_ref, rows(*me), local_sem)
        mine.start()
        first = [copy(0, me, sibling, src=x_ref)]
        first += [copy(1 + j, me, (*chip, c), src=x_ref) for j, chip in enumerate(chips)]
        for cp in first:
            cp.start()
        passed = [copy(4 + j, (*chip, c), sibling) for j, chip in enumerate(chips)]
        for j, chip in enumerate(chips):
            copy(1 + j, (*chip, c), me).wait_recv()
            passed[j].start()
        copy(0, sibling, me).wait_recv()
        for j, chip in enumerate(chips):
            copy(4 + j, (*chip, 1 - c), me).wait_recv()
        for cp in first + passed:
            cp.wait_send()
        mine.wait()

    return pl.pallas_call(
        body, name=name,
        out_shape=jax.ShapeDtypeStruct((8 * m_per, n), blk.dtype),
        in_specs=[pl.BlockSpec(memory_space=pltpu.VMEM)],
        out_specs=pl.BlockSpec(memory_space=pltpu.VMEM),
        scratch_shapes=[pltpu.SemaphoreType.DMA((7,)), pltpu.SemaphoreType.DMA((7,)), pltpu.SemaphoreType.DMA],
        compiler_params=pltpu.CompilerParams(vmem_limit_bytes=VMEM_LIMIT),
    )(blk)


def _hbm_specs(n):
    return [pl.BlockSpec(memory_space=pltpu.HBM)] * n


def _allgather_weights(shards):
    n = len(shards)

    def body(*refs):
        start, relay, finish = _gather_protocol(refs[:n], refs[n:2 * n], *refs[2 * n:])
        start()
        relay()
        finish()

    gathered = pl.pallas_call(
        body, name="allgather_weights",
        out_shape=_gather_shapes(shards), in_specs=_hbm_specs(n), out_specs=_hbm_specs(n),
        scratch_shapes=_gather_sems(n),
    )(*shards)
    return _with_own_shard(gathered, shards)


def _gather_shapes(shards):
    return [jax.ShapeDtypeStruct((4, *s.shape), s.dtype) for s in shards]


def _gather_sems(n):
    return [pltpu.SemaphoreType.DMA((6 * n,)), pltpu.SemaphoreType.DMA((6 * n,))]


def _with_own_shard(gathered, shards):
    chip = 2 * lax.axis_index("x") + lax.axis_index("y")
    return [lax.dynamic_update_slice(g, s[None], (chip, 0, 0)) for g, s in zip(gathered, shards)]


def _gather_protocol(ins, outs, send_sems, recv_sems):
    n = len(ins)
    x, y, c, chips = _place()
    me, sibling = (x, y, c), (x, y, 1 - c)
    me_k = 2 * x + y

    def half(w, k, hc):
        h = ins[w].shape[0] // 2
        return outs[w].at[k, pl.ds(pl.multiple_of(hc * h, 8), h), :]

    def copy(w, j, k, hc, to, src=None):
        dst = half(w, k, hc)
        return pltpu.make_async_remote_copy(
            src_ref=dst if src is None else src, dst_ref=dst,
            send_sem=send_sems.at[6 * w + j], recv_sem=recv_sems.at[6 * w + j],
            device_id=to, device_id_type=MESH)

    def first(w, j):
        h = ins[w].shape[0] // 2
        src = ins[w].at[pl.ds(pl.multiple_of(c * h, 8), h), :]
        return copy(w, j, me_k, c, (*chips[j], c), src=src)

    def passed(w, j):
        px, py = chips[j]
        return copy(w, 3 + j, 2 * px + py, c, sibling)

    pairs = [(w, j) for w in range(n) for j in range(3)]

    def start():
        for w, j in pairs:
            first(w, j).start()

    def relay():
        for w, j in pairs:
            px, py = chips[j]
            copy(w, j, 2 * px + py, c, me).wait_recv()
            passed(w, j).start()

    def finish():
        for w, j in pairs:
            px, py = chips[j]
            copy(w, 3 + j, 2 * px + py, 1 - c, me).wait_recv()
        for w, j in pairs:
            first(w, j).wait_send()
            passed(w, j).wait_send()

    return start, relay, finish


def _sibling_swap_halves(name, grads):
    n = len(grads)

    def body(*refs):
        ins, outs = refs[:n], refs[n:2 * n]
        send_sems, recv_sems = refs[2 * n:]
        x, y, c, _ = _place()
        cps = []
        for w in range(n):
            cp = pltpu.make_async_remote_copy(
                src_ref=ins[w].at[:, 1 - c], dst_ref=outs[w],
                send_sem=send_sems.at[w], recv_sem=recv_sems.at[w],
                device_id=(x, y, 1 - c), device_id_type=MESH)
            cp.start()
            cps.append(cp)
        for cp in cps:
            cp.wait()

    return pl.pallas_call(
        body, name=name,
        out_shape=[jax.ShapeDtypeStruct((4, *g.shape[2:]), g.dtype) for g in grads],
        in_specs=_hbm_specs(n), out_specs=_hbm_specs(n),
        scratch_shapes=[pltpu.SemaphoreType.DMA((n,)), pltpu.SemaphoreType.DMA((n,))],
    )(*grads)


def _all_to_all_sems(n):
    return [pltpu.SemaphoreType.DMA((3 * n,)), pltpu.SemaphoreType.DMA((3 * n,))]


def _all_to_all_protocol(ins, outs, send_sems, recv_sems):
    n = len(ins)
    x, y, c, chips = _place()
    me_k = 2 * x + y
    pairs = [(w, j) for w in range(n) for j in range(3)]

    def sent(w, j):
        px, py = chips[j]
        return pltpu.make_async_remote_copy(
            src_ref=ins[w].at[2 * px + py], dst_ref=outs[w].at[me_k],
            send_sem=send_sems.at[3 * w + j], recv_sem=recv_sems.at[3 * w + j],
            device_id=(px, py, c), device_id_type=MESH)

    def start():
        for w, j in pairs:
            sent(w, j).start()

    def finish():
        for w, j in pairs:
            px, py = chips[j]
            slab = outs[w].at[2 * px + py]
            pltpu.make_async_remote_copy(
                src_ref=slab, dst_ref=slab, send_sem=send_sems.at[3 * w + j],
                recv_sem=recv_sems.at[3 * w + j], device_id=(px, py, c), device_id_type=MESH).wait_recv()
        for w, j in pairs:
            sent(w, j).wait_send()

    return start, finish


def _sibling_share(halves):
    n = len(halves)

    def body(*refs):
        ins, outs = refs[:n], refs[n:2 * n]
        send_sems, recv_sems = refs[2 * n:]
        x, y, c, _ = _place()
        cps = []
        for w in range(n):
            cp = pltpu.make_async_remote_copy(
                src_ref=ins[w], dst_ref=outs[w], send_sem=send_sems.at[w], recv_sem=recv_sems.at[w],
                device_id=(x, y, 1 - c), device_id_type=MESH)
            cp.start()
            cps.append(cp)
        for cp in cps:
            cp.wait()

    return pl.pallas_call(
        body, name="grad_sibling_share",
        out_shape=[jax.ShapeDtypeStruct(p.shape, p.dtype) for p in halves],
        in_specs=_hbm_specs(n), out_specs=_hbm_specs(n),
        scratch_shapes=[pltpu.SemaphoreType.DMA((n,)), pltpu.SemaphoreType.DMA((n,))],
    )(*halves)


def _mm(name, grid, a, a_spec, b, b_spec, contract, out_shapes, out_specs, epilogue,
        extras=(), extra_specs=(), nk=1, acc_shape=None, semantics=None, a2a_parts=()):
    ne, no, nc = len(extras), len(out_shapes), len(a2a_parts)
    nd = len(grid)

    def body(*refs):
        a_ref, b_ref = refs[0], refs[1]
        ex, outs = refs[2:2 + ne], refs[2 + ne + nc:2 + ne + nc + no]
        if nc:
            ids = [pl.program_id(d) for d in range(nd)]
            start, finish = _all_to_all_protocol(refs[2 + ne:2 + ne + nc],
                                                 refs[2 + ne + nc + no:2 + ne + 2 * nc + no], *refs[-2:])
            pl.when(functools.reduce(jnp.logical_and, [i == 0 for i in ids]))(start)

        def prod():
            return lax.dot_general(a_ref[...], b_ref[...], (((contract[0],), (contract[1],)), ((), ())),
                                   preferred_element_type=F32)

        if nk == 1:
            epilogue(prod(), ex, outs)
        else:
            acc = refs[2 + ne + 2 * nc + no]
            k = pl.program_id(nd - 1)

            @pl.when(k == 0)
            def _():
                acc[...] = prod()

            @pl.when(k > 0)
            def _():
                acc[...] += prod()

            @pl.when(k == nk - 1)
            def _():
                epilogue(acc[...], ex, outs)

        if nc:
            pl.when(functools.reduce(jnp.logical_and, [i == g - 1 for i, g in zip(ids, grid)]))(finish)

    if semantics is None or nc:
        semantics = ("arbitrary",) * nd
    return pl.pallas_call(
        body, name=name, grid=grid,
        in_specs=[a_spec, b_spec, *extra_specs] + _hbm_specs(nc),
        out_specs=list(out_specs) + _hbm_specs(nc),
        out_shape=list(out_shapes) + [jax.ShapeDtypeStruct(p.shape, p.dtype) for p in a2a_parts],
        scratch_shapes=([] if nk == 1 else [pltpu.VMEM(acc_shape, F32)]) + (_all_to_all_sems(nc) if nc else []),
        compiler_params=_params(semantics),
    )(a, b, *extras, *a2a_parts)


def _store(dtype):
    def epilogue(acc, ex, outs):
        outs[0][...] = acc.astype(dtype)
    return epilogue


def _sds(shape, dtype):
    return jax.ShapeDtypeStruct(shape, dtype)


TR = 512


def _row_spec(width, tr=TR):
    return pl.BlockSpec((tr, width), lambda i: (i, 0))


def _const_spec(shape):
    nd = len(shape)
    return pl.BlockSpec(shape, lambda i: (0,) * nd)


def _norm_mod_fwd(name, h, p):
    S = h.shape[0]

    def body(h_ref, p_ref, u_ref):
        hv = h_ref[...]
        r = lax.rsqrt(jnp.mean(hv * hv, axis=-1, keepdims=True) + EPS)
        nrm = (hv * r) * p_ref[0:1, :]
        u_ref[...] = (nrm * (1.0 + p_ref[1:2, :]) + p_ref[2:3, :]).astype(BF16)

    return pl.pallas_call(
        body, name=name, grid=(S // TR,),
        in_specs=[_row_spec(D), _const_spec((8, D))], out_specs=_row_spec(D),
        out_shape=_sds((S, D), BF16), compiler_params=_params(("parallel",)),
    )(h, p)


def _final_loss_bwd(h, gf, target, p3, f3):
    S = h.shape[0]

    def body(h_ref, g_ref, t_ref, p_ref, f_ref, dh_ref, df_ref, sums_ref, loss_ref):
        i = pl.program_id(0)

        @pl.when(i == 0)
        def _():
            sums_ref[...] = jnp.zeros_like(sums_ref)
            loss_ref[...] = jnp.zeros_like(loss_ref)

        hv = h_ref[...]
        g = g_ref[0:1, :]
        r = lax.rsqrt(jnp.mean(hv * hv, axis=-1, keepdims=True) + EPS)
        xn = hv * r
        err = xn * g - t_ref[...]
        loss_ref[...] += 0.5 * jnp.sum(err * err) * (1.0 / D)
        dout = err * (1.0 / D)
        dxn = dout * g
        dh = r * (dxn - xn * jnp.mean(dxn * xn, axis=-1, keepdims=True))
        dh_ref[...] = dh
        gate = p_ref[3:4, :]
        df_ref[...] = (0.5 * gate * dh).astype(BF16)
        sums_ref[0:1, :] += jnp.sum(dout * xn, axis=0, keepdims=True)
        sums_ref[1:2, :] += 0.5 * jnp.sum(dh * f_ref[...].astype(F32), axis=0, keepdims=True)

    return pl.pallas_call(
        body, name="final_loss_bwd", grid=(S // TR,),
        in_specs=[_row_spec(D), _const_spec((8, D)), _row_spec(D), _const_spec((8, D)), _row_spec(D)],
        out_specs=[_row_spec(D), _row_spec(D), _const_spec((8, D)), _const_spec((8, LANES))],
        out_shape=[_sds((S, D), F32), _sds((S, D), BF16), _sds((8, D), F32), _sds((8, LANES), F32)],
        compiler_params=_params(("arbitrary",)),
    )(h, gf, target, p3, f3)


def _norm_mod_bwd(name, du, h, p, dh_res, prev=None):
    S = h.shape[0]
    has_prev = prev is not None

    def body(*refs):
        if has_prev:
            du_ref, h_ref, p_ref, r_ref, pp_ref, f_ref, dh_ref, df_ref, sums_ref = refs
        else:
            du_ref, h_ref, p_ref, r_ref, dh_ref, sums_ref = refs
        i = pl.program_id(0)

        @pl.when(i == 0)
        def _():
            sums_ref[...] = jnp.zeros_like(sums_ref)

        hv = h_ref[...]
        duv = du_ref[...]
        g = p_ref[0:1, :]
        one_scale = 1.0 + p_ref[1:2, :]
        r = lax.rsqrt(jnp.mean(hv * hv, axis=-1, keepdims=True) + EPS)
        xn = hv * r
        dn = duv * one_scale
        dxn = dn * g
        dh = r_ref[...] + r * (dxn - xn * jnp.mean(dxn * xn, axis=-1, keepdims=True))
        dh_ref[...] = dh
        sums_ref[0:1, :] += jnp.sum(duv, axis=0, keepdims=True)
        sums_ref[1:2, :] += jnp.sum(duv * (xn * g), axis=0, keepdims=True)
        sums_ref[2:3, :] += jnp.sum(dn * xn, axis=0, keepdims=True)
        if has_prev:
            wgt = prev[2]
            df_ref[...] = (wgt * pp_ref[3:4, :] * dh).astype(BF16)
            sums_ref[3:4, :] += wgt * jnp.sum(dh * f_ref[...].astype(F32), axis=0, keepdims=True)

    ins = [du, h, p, dh_res]
    in_specs = [_row_spec(D), _row_spec(D), _const_spec((8, D)), _row_spec(D)]
    out_specs = [_row_spec(D)]
    out_shape = [_sds((S, D), F32)]
    if has_prev:
        ins += [prev[0], prev[1]]
        in_specs += [_const_spec((8, D)), _row_spec(D)]
        out_specs.append(_row_spec(D))
        out_shape.append(_sds((S, D), BF16))
    out_specs.append(_const_spec((8, D)))
    out_shape.append(_sds((8, D), F32))
    return pl.pallas_call(
        body, name=name, grid=(S // TR,), in_specs=in_specs, out_specs=out_specs, out_shape=out_shape,
        compiler_params=_params(("arbitrary",)),
    )(*ins)


TM = 512


def _ffn_up(name, u, wgu4, shards=()):
    S = u.shape[0]
    n = len(shards)
    ni = S // TM

    def body(u_ref, wg_ref, wu_ref, *rest):
        gu_ref, hm_ref = rest[n:n + 2]
        s, i = pl.program_id(0), pl.program_id(1)
        if n:
            start, relay, finish = _gather_protocol(rest[:n], rest[n + 2:2 * n + 2], *rest[2 * n + 2:])
            pl.when((s == 0) & (i == 0))(start)
            pl.when((s == 1) & (i == 0))(relay)
        uv = u_ref[...]
        g = jnp.dot(uv, wg_ref[...], preferred_element_type=F32)
        up = jnp.dot(uv, wu_ref[...], preferred_element_type=F32)
        gu_ref[0] = g.astype(BF16)
        gu_ref[1] = up.astype(BF16)
        hm_ref[...] = (g * _sigmoid(g) * up).astype(BF16)
        if n:
            pl.when((s == 1) & (i == ni - 1))(finish)

    gu, hm, *gathered = pl.pallas_call(
        body, name=name, grid=(2, ni),
        in_specs=[pl.BlockSpec((TM, D), lambda s, i: (i, 0)),
                  pl.BlockSpec((None, D, FF_SHARD), lambda s, i: (s, 0, 0)),
                  pl.BlockSpec((None, D, FF_SHARD), lambda s, i: (s + 2, 0, 0))] + _hbm_specs(n),
        out_specs=[pl.BlockSpec((2, TM, FF_SHARD), lambda s, i: (0, i, s)),
                   pl.BlockSpec((TM, FF_SHARD), lambda s, i: (i, s))] + _hbm_specs(n),
        out_shape=[_sds((2, S, D_FF), BF16), _sds((S, D_FF), BF16)] + _gather_shapes(shards),
        scratch_shapes=_gather_sems(n) if n else [],
        compiler_params=_params(("arbitrary", "arbitrary") if n else ("parallel", "parallel")),
    )(u, wgu4, wgu4, *shards)
    return gu, hm, _with_own_shard(gathered, shards)


def _proj_residual(name, a, w, h, p, weight):
    S, K = a.shape

    def epilogue(acc, ex, outs):
        h_ref, p_ref = ex
        outs[0][...] = acc.astype(BF16)
        outs[1][...] = h_ref[...] + weight * p_ref[3:4, :] * acc

    return _mm(
        name, (S // TM,), a, pl.BlockSpec((TM, K), lambda i: (i, 0)), w, pl.BlockSpec((K, D), lambda i: (0, 0)),
        (1, 0), [_sds((S, D), BF16), _sds((S, D), F32)], [_row_spec(D, TM), _row_spec(D, TM)], epilogue,
        extras=(h, p), extra_specs=(_row_spec(D, TM), _const_spec((8, D))), semantics=("parallel",))


def _ffn_down_bwd(name, df, wd, gu, a2a_parts=()):
    S = df.shape[0]

    def epilogue(acc, ex, outs):
        g = ex[0][0].astype(F32)
        up = ex[0][1].astype(F32)
        sg = _sigmoid(g)
        outs[0][0] = (acc * up * (sg * (1.0 + g * (1.0 - sg)))).astype(BF16)
        outs[0][1] = (acc * g * sg).astype(BF16)

    gu_spec = pl.BlockSpec((2, TM, FF_SHARD), lambda n, i: (0, i, n))
    return _mm(
        name, (2, S // TM), df, pl.BlockSpec((TM, D), lambda n, i: (i, 0)),
        wd, pl.BlockSpec((FF_SHARD, D), lambda n, i: (n, 0)), (1, 1),
        [_sds((2, S, D_FF), BF16)], [gu_spec], epilogue, extras=(gu,), extra_specs=(gu_spec,),
        semantics=("parallel", "parallel"), a2a_parts=a2a_parts)


TK = 512


def _grad_w(name, a, a_w, b, b_w, b_map, n_out, out_shape, out_block, out_map, a2a_parts=()):
    S = a.shape[0]
    nk = S // TK
    res = _mm(
        name, (n_out, nk), a, pl.BlockSpec((TK, a_w), lambda s, k: (k, 0)), b, pl.BlockSpec(
            (None, TK, b_w) if b.ndim == 3 else (TK, b_w), b_map), (0, 0),
        [_sds(out_shape, BF16)], [pl.BlockSpec(out_block, out_map)], _store(BF16), nk=nk, acc_shape=(a_w, b_w),
        semantics=("parallel", "arbitrary"), a2a_parts=a2a_parts)
    return res if a2a_parts else res[0]


def _ffn_dw_down(name, hm, df):
    S = df.shape[0]
    return _mm(
        name, (2, S // TK), hm, pl.BlockSpec((TK, FF_SHARD), lambda m, k: (k, m)),
        df, pl.BlockSpec((TK, D), lambda m, k: (k, 0)), (0, 0),
        [_sds((D_FF, D), BF16)], [pl.BlockSpec((FF_SHARD, D), lambda m, k: (m, 0))], _store(BF16),
        nk=S // TK, acc_shape=(FF_SHARD, D), semantics=("parallel", "arbitrary"))[0]


def _ffn_up_bwd(name, dgu, wgu4, a2a_parts=()):
    S = dgu.shape[1]
    return _mm(
        name, (S // TM, 4), dgu, pl.BlockSpec((None, TM, FF_SHARD), lambda i, s: (s // 2, i, s % 2)),
        wgu4, pl.BlockSpec((None, D, FF_SHARD), lambda i, s: (s, 0, 0)), (1, 1),
        [_sds((S, D), F32)], [pl.BlockSpec((TM, D), lambda i, s: (i, 0))], _store(F32),
        nk=4, acc_shape=(TM, D), semantics=("parallel", "arbitrary"), a2a_parts=a2a_parts)


def _ffn_dw_gu(name, u_in, dgu, a2a_parts=()):
    return _grad_w(name, u_in, D, dgu, FF_SHARD, lambda s, k: (s // 2, k, s % 2), 4,
                   (4, D, FF_SHARD), (None, D, FF_SHARD), lambda s, k: (s, 0, 0), a2a_parts=a2a_parts)


def _ffn_bwd(tag, df, u_in, gu, hm, wgu4, wd):
    dgu = _ffn_down_bwd(tag + "_down_bwd", df, wd, gu)[0]
    dwd = _ffn_dw_down(tag + "_dw_down", hm, df)
    du = _ffn_up_bwd(tag + "_up_bwd", dgu, wgu4)[0]
    dwgu = _ffn_dw_gu(tag + "_dw_gu", u_in, dgu)
    return du, dwgu, dwd


def _shift_down(v, k, row):
    return jnp.where(row >= k, pltpu.roll(v, k, axis=0), 0.0)


def _shift_up(v, k, row, S):
    return jnp.where(row < S - k, pltpu.roll(v, S - k, axis=0), 0.0)


def _conv_specs(S):
    cols = CONV_W // LANES
    return [pl.BlockSpec((S, LANES), functools.partial(lambda j, off: (0, off + j), off=o * cols))
            for o in range(3)]


def _conv_fwd(proj, conv_w):
    S = proj.shape[0]

    def body(cb_ref, cc_ref, cx_ref, w_ref, sc_ref):
        row = lax.broadcasted_iota(jnp.int32, (S, LANES), 0)
        v = cc_ref[...].astype(F32) * cx_ref[...].astype(F32)
        yv = w_ref[0:1, :] * _shift_down(v, 2, row) + w_ref[1:2, :] * _shift_down(v, 1, row) + w_ref[2:3, :] * v
        sc_ref[...] = (cb_ref[...].astype(F32) * yv).astype(BF16)

    return pl.pallas_call(
        body, name="conv_fwd", grid=(CONV_W // LANES,),
        in_specs=_conv_specs(S) + [pl.BlockSpec((3, LANES), lambda j: (0, j))],
        out_specs=pl.BlockSpec((S, LANES), lambda j: (0, j)), out_shape=_sds((S, CONV_W), BF16),
        compiler_params=_params(("parallel",)),
    )(proj, proj, proj, conv_w)


def _conv_bwd(dsc, proj, conv_w):
    S = proj.shape[0]

    def body(d_ref, cb_ref, cc_ref, cx_ref, w_ref, dcb_ref, dcc_ref, dcx_ref, dw_ref):
        row = lax.broadcasted_iota(jnp.int32, (S, LANES), 0)
        cc = cc_ref[...].astype(F32)
        cx = cx_ref[...].astype(F32)
        d = d_ref[...].astype(F32)
        v = cc * cx
        v1 = _shift_down(v, 1, row)
        v2 = _shift_down(v, 2, row)
        w0, w1, w2 = w_ref[0:1, :], w_ref[1:2, :], w_ref[2:3, :]
        dcb_ref[...] = (d * (w0 * v2 + w1 * v1 + w2 * v)).astype(BF16)
        dy = d * cb_ref[...].astype(F32)
        dw_ref[0:1, :] = jnp.sum(dy * v2, axis=0, keepdims=True)
        dw_ref[1:2, :] = jnp.sum(dy * v1, axis=0, keepdims=True)
        dw_ref[2:3, :] = jnp.sum(dy * v, axis=0, keepdims=True)
        dv = w2 * dy + w1 * _shift_up(dy, 1, row, S) + w0 * _shift_up(dy, 2, row, S)
        dcc_ref[...] = (dv * cx).astype(BF16)
        dcx_ref[...] = (dv * cc).astype(BF16)

    col = pl.BlockSpec((S, LANES), lambda j: (0, j))
    return pl.pallas_call(
        body, name="conv_bwd", grid=(CONV_W // LANES,),
        in_specs=[col] + _conv_specs(S) + [pl.BlockSpec((3, LANES), lambda j: (0, j))],
        out_specs=[col, col, col, pl.BlockSpec((3, LANES), lambda j: (0, j))],
        out_shape=[_sds((S, CONV_W), BF16)] * 3 + [_sds((3, CONV_W), F32)],
        compiler_params=_params(("parallel",)),
    )(dsc, proj, proj, proj, conv_w)


Q_COL, K_COL, V_COL = 1536 // LANES, 2048 // LANES, 2560 // LANES


def _split_dot(x, tri):
    hi = x.astype(BF16)
    lo = (x - hi.astype(F32)).astype(BF16)
    return jnp.dot(hi, tri, preferred_element_type=F32) + jnp.dot(lo, tri, preferred_element_type=F32)


def _tri_dot(tri, x):
    hi = x.astype(BF16)
    lo = (x - hi.astype(F32)).astype(BF16)
    return jnp.dot(tri, hi, preferred_element_type=F32) + jnp.dot(tri, lo, preferred_element_type=F32)


def _softplus(z):
    return jnp.maximum(z, 0.0) + jnp.log(1.0 + jnp.exp(-jnp.abs(z)))


def _nt(a, b):
    return lax.dot_general(a, b, (((1,), (1,)), ((), ())), preferred_element_type=F32)


def _tn(a, b):
    return lax.dot_general(a, b, (((0,), (0,)), ((), ())), preferred_element_type=F32)


def _interleave(gens, delays):
    results = [None] * len(gens)
    live = list(range(len(gens)))
    rnd = 0
    while live:
        for g in list(live):
            if rnd < delays[g]:
                continue
            try:
                next(gens[g])
            except StopIteration as stop:
                results[g] = stop.value
                live.remove(g)
        rnd += 1
    return results


def _attn_fwd(proj, shards):
    S = proj.shape[0]
    B = ATT_BLK
    nq = S // B
    n = len(shards)

    def body(q_ref, k_ref, v_ref, *rest):
        o_ref, t_ref = rest[n:n + 2]
        start, relay, finish = _gather_protocol(rest[:n], rest[n + 2:2 * n + 2], *rest[2 * n + 2:])
        p = pl.program_id(0)
        i = pl.program_id(1)
        pl.when((p == 0) & (i == 0))(start)
        pl.when((p == HEAD_PAIRS // 2) & (i == 0))(relay)
        lo_lane = lax.broadcasted_iota(jnp.int32, (B, LANES), 1) < HEAD_DIM
        row = lax.broadcasted_iota(jnp.int32, (B, B), 0)
        col = lax.broadcasted_iota(jnp.int32, (B, B), 1)
        after = (row > col).astype(BF16)
        causal = col < row
        q2 = q_ref[...] * 0.125
        zero = jnp.zeros((), BF16)
        q_heads = (jnp.where(lo_lane, q2, zero), jnp.where(lo_lane, zero, q2))

        def head_tile(q_h, st, kb, diag):
            k2 = k_ref[pl.ds(pl.multiple_of(kb * B, B), B), :]
            z = _nt(q_h, k2)
            yield
            spz = _softplus(z)
            sp = jnp.where(causal, spz, 0.0) if diag else spz
            hi = sp.astype(BF16)
            lo = (sp - hi.astype(F32)).astype(BF16)
            r = st["r"]
            st["r"] = r + jnp.sum(sp, axis=1, keepdims=True)
            yield
            rem = jnp.dot(hi, after, preferred_element_type=F32) + jnp.dot(lo, after, preferred_element_type=F32)
            yield
            a = jnp.exp(z - spz - (rem + r))
            if diag:
                a = jnp.where(causal, a, 0.0)
            ab = a.astype(BF16)
            yield
            v2 = v_ref[pl.ds(pl.multiple_of(kb * B, B), B), :]
            st["acc"] = st["acc"] + jnp.dot(ab, v2, preferred_element_type=F32)

        def tiles(kbs, carry, diags=(False, False)):
            sts = [dict(r=carry[0], acc=carry[1]), dict(r=carry[2], acc=carry[3])]
            gens = [head_tile(q_h, st, kb, dg) for kb, dg in zip(kbs, diags) for q_h, st in zip(q_heads, sts)]
            _interleave(gens, [t for t in range(len(kbs)) for _ in q_heads])
            return sts[0]["r"], sts[0]["acc"], sts[1]["r"], sts[1]["acc"]

        zr, za = jnp.zeros((B, 1), F32), jnp.zeros((B, LANES), F32)
        carry = lax.fori_loop(0, i % 2, lambda j, cr: tiles([i, i - 1], cr, (True, False)), (zr, za, zr, za))
        carry = lax.fori_loop(0, 1 - i % 2, lambda j, cr: tiles([i], cr, (True,)), carry)
        first = i - 1 - i % 2
        ra, acc_a, rb, acc_b = lax.fori_loop(
            0, i // 2, lambda j, cr: tiles([first - 2 * j, first - 2 * j - 1], cr), carry)
        o_ref[...] = jnp.where(lo_lane, acc_a, acc_b).astype(BF16)
        t_ref[...] = jnp.where(lo_lane, ra, rb).T
        pl.when((p == HEAD_PAIRS - 1) & (i == nq - 1))(finish)

    seq = lambda off: pl.BlockSpec((S, LANES), lambda p, i: (0, off + p))
    blk = pl.BlockSpec((B, LANES), lambda p, i: (i, p))
    o, t, *gathered = pl.pallas_call(
        body, name="attn_fwd", grid=(HEAD_PAIRS, nq),
        in_specs=[pl.BlockSpec((B, LANES), lambda p, i: (i, Q_COL + p)), seq(K_COL), seq(V_COL)] + _hbm_specs(n),
        out_specs=[blk, pl.BlockSpec((LANES, B), lambda p, i: (p, i))] + _hbm_specs(n),
        out_shape=[_sds((S, 512), BF16), _sds((512, S), F32)] + _gather_shapes(shards),
        scratch_shapes=_gather_sems(n),
        compiler_params=_params(("arbitrary", "arbitrary")),
    )(proj, proj, proj, *shards)
    return o, t, _with_own_shard(gathered, shards)


def _attn_bwd(proj, t, do, parts):
    S = proj.shape[0]
    kt = proj[:, K_COL * LANES:V_COL * LANES].T
    B = ATT_BLK
    nq = S // B
    n = len(parts)

    def body(q_ref, k_ref, v_ref, kt_ref, t_ref, do_ref, *rest):
        dq_ref, dk_ref, dv_ref = rest[n:n + 3]
        dk_acc, dv_acc = rest[2 * n + 3:2 * n + 5]
        start, finish = _all_to_all_protocol(rest[:n], rest[n + 3:2 * n + 3], *rest[2 * n + 5:])
        i = pl.program_id(1)
        pl.when((pl.program_id(0) == 0) & (i == 0))(start)

        @pl.when(i == 0)
        def _():
            dk_acc[...] = jnp.zeros_like(dk_acc)
            dv_acc[...] = jnp.zeros_like(dv_acc)

        lo_lane = lax.broadcasted_iota(jnp.int32, (B, LANES), 1) < HEAD_DIM
        key = lax.broadcasted_iota(jnp.int32, (B, B), 0)
        qry = lax.broadcasted_iota(jnp.int32, (B, B), 1)
        upto = (qry <= key).astype(BF16)
        before = (qry < key).astype(BF16)
        causal = key < qry
        zero = jnp.zeros((), BF16)
        q2 = q_ref[...] * 0.125
        do2 = do_ref[...]
        heads = ((jnp.where(lo_lane, q2, zero), jnp.where(lo_lane, do2, zero), t_ref[0:1, :]),
                 (jnp.where(lo_lane, zero, q2), jnp.where(lo_lane, zero, do2), t_ref[HEAD_DIM:HEAD_DIM + 1, :]))

        def head_tile(head, st, kb, diag):
            q_h, do_h, t_h = head
            rows = pl.ds(pl.multiple_of(kb * B, B), B)
            z = _nt(k_ref[rows, :], q_h)
            da = _nt(v_ref[rows, :], do_h)
            yield
            spz = _softplus(z)
            sp = jnp.where(causal, spz, 0.0) if diag else spz
            hi = sp.astype(BF16)
            lo = (sp - hi.astype(F32)).astype(BF16)
            pc = st["pc"]
            st["pc"] = pc + jnp.sum(sp, axis=0, keepdims=True)
            yield
            pref = jnp.dot(upto, hi, preferred_element_type=F32) + jnp.dot(upto, lo, preferred_element_type=F32)
            yield
            a = jnp.exp(z - spz - ((t_h - pc) - pref))
            if diag:
                a = jnp.where(causal, a, 0.0)
            e = a * da
            eb = e.astype(BF16)
            ab = a.astype(BF16)
            ec = st["ec"]
            st["ec"] = ec + jnp.sum(e, axis=0, keepdims=True)
            yield
            e_before = ec + jnp.dot(before, eb, preferred_element_type=F32)
            yield
            u = jnp.exp(-spz)
            dz = u * (e + e_before) - e_before
            if diag:
                dz = jnp.where(causal, dz, 0.0)
            dzb = dz.astype(BF16)
            yield
            st["dqt"] = st["dqt"] + jnp.dot(kt_ref[:, rows], dzb, preferred_element_type=F32)
            return (jnp.dot(dzb, q_h, preferred_element_type=F32), jnp.dot(ab, do_h, preferred_element_type=F32))

        def tiles(kbs, carry, diags=(False, False)):
            sts = [dict(pc=carry[3 * h], ec=carry[3 * h + 1], dqt=carry[3 * h + 2]) for h in range(2)]
            gens = [head_tile(hd, st, kb, dg) for kb, dg in zip(kbs, diags) for hd, st in zip(heads, sts)]
            res = _interleave(gens, [t for t in range(len(kbs)) for _ in heads])
            for t, kb in enumerate(kbs):
                rows = pl.ds(pl.multiple_of(kb * B, B), B)
                (dk_a, dv_a), (dk_b, dv_b) = res[2 * t], res[2 * t + 1]
                dk_acc[rows, :] += dk_a + dk_b
                dv_acc[rows, :] += dv_a + dv_b
            return tuple(st[nm] for st in sts for nm in ("pc", "ec", "dqt"))

        zc, zq = jnp.zeros((1, B), F32), jnp.zeros((LANES, B), F32)
        carry = lax.fori_loop(0, i // 2, lambda j, cr: tiles([2 * j, 2 * j + 1], cr), (zc, zc, zq, zc, zc, zq))
        carry = lax.fori_loop(0, i % 2, lambda j, cr: tiles([i - 1, i], cr, (False, True)), carry)
        _, _, dqt_a, _, _, dqt_b = lax.fori_loop(0, 1 - i % 2, lambda j, cr: tiles([i], cr, (True,)), carry)
        head0 = lax.broadcasted_iota(jnp.int32, (LANES, B), 0) < HEAD_DIM
        dq_ref[...] = (jnp.where(head0, dqt_a, dqt_b).T * 0.125).astype(BF16)

        @pl.when(i == nq - 1)
        def _():
            dk_ref[...] = dk_acc[...].astype(BF16)
            dv_ref[...] = dv_acc[...].astype(BF16)

        pl.when((pl.program_id(0) == HEAD_PAIRS - 1) & (i == nq - 1))(finish)

    seq = lambda off: pl.BlockSpec((S, LANES), lambda p, i: (0, off + p))
    blk = pl.BlockSpec((B, LANES), lambda p, i: (i, p))
    whole = pl.BlockSpec((S, LANES), lambda p, i: (0, p))
    dq, dk, dv, *came = pl.pallas_call(
        body, name="attn_bwd", grid=(HEAD_PAIRS, nq),
        in_specs=[pl.BlockSpec((B, LANES), lambda p, i: (i, Q_COL + p)), seq(K_COL), seq(V_COL),
                  pl.BlockSpec((LANES, S), lambda p, i: (p, 0)), pl.BlockSpec((LANES, B), lambda p, i: (p, i)), blk]
        + _hbm_specs(n),
        out_specs=[blk, whole, whole] + _hbm_specs(n),
        out_shape=[_sds((S, 512), BF16)] * 3 + [jax.ShapeDtypeStruct(p.shape, p.dtype) for p in parts],
        scratch_shapes=[pltpu.VMEM((S, LANES), F32), pltpu.VMEM((S, LANES), F32)] + _all_to_all_sems(n),
        compiler_params=_params(("arbitrary", "arbitrary")),
    )(proj, proj, proj, kt, t, do, *parts)
    return dq, dk, dv, came


GA_COL, GB_COL = 3072 // 256, 4096 // 256


def _merge_fwd(sc, o, wco4, wao4, proj, bm):
    S = sc.shape[0]

    def body(sc_ref, o_ref, wc_ref, wa_ref, ga_ref, gb_ref, bm_ref, ya_ref, yb_ref, mg_ref):
        ya = jnp.dot(sc_ref[...], wc_ref[...], preferred_element_type=F32)
        yb = jnp.dot(o_ref[...], wa_ref[...], preferred_element_type=F32)
        sa = _sigmoid(ga_ref[...].astype(F32) + bm_ref[0:1, :])
        sb = _sigmoid(gb_ref[...].astype(F32) + bm_ref[1:2, :])
        ya_ref[...] = ya.astype(BF16)
        yb_ref[...] = yb.astype(BF16)
        mg_ref[...] = (sa * ya + sb * yb).astype(BF16)

    wide = pl.BlockSpec((TM, 512), lambda n, i: (i, 0))
    wsp = pl.BlockSpec((None, 512, 256), lambda n, i: (n, 0, 0))
    out = pl.BlockSpec((TM, 256), lambda n, i: (i, n))
    return pl.pallas_call(
        body, name="merge_fwd", grid=(4, S // TM),
        in_specs=[wide, wide, wsp, wsp, pl.BlockSpec((TM, 256), lambda n, i: (i, GA_COL + n)),
                  pl.BlockSpec((TM, 256), lambda n, i: (i, GB_COL + n)), pl.BlockSpec((2, 256), lambda n, i: (0, n))],
        out_specs=[out, out, out], out_shape=[_sds((S, D), BF16)] * 3,
        compiler_params=_params(("parallel", "parallel")),
    )(sc, o, wco4, wao4, proj, proj, bm)


def _merge_bwd(dy2, wout, ya, yb, proj, bm):
    S = dy2.shape[0]

    def epilogue(acc, ex, outs):
        ya_ref, yb_ref, ga_ref, gb_ref, bm_ref = ex
        i = pl.program_id(1)
        sa = _sigmoid(ga_ref[...].astype(F32) + bm_ref[0:1, :])
        sb = _sigmoid(gb_ref[...].astype(F32) + bm_ref[1:2, :])
        dga = acc * ya_ref[...].astype(F32) * (sa * (1.0 - sa))
        dgb = acc * yb_ref[...].astype(F32) * (sb * (1.0 - sb))
        outs[0][...] = (acc * sa).astype(BF16)
        outs[1][...] = (acc * sb).astype(BF16)
        outs[2][...] = dga.astype(BF16)
        outs[3][...] = dgb.astype(BF16)

        @pl.when(i == 0)
        def _():
            outs[4][...] = jnp.zeros_like(outs[4])

        outs[4][0:1, :] += jnp.sum(dga, axis=0, keepdims=True)
        outs[4][1:2, :] += jnp.sum(dgb, axis=0, keepdims=True)

    out = pl.BlockSpec((TM, 256), lambda n, i: (i, n))
    return _mm(
        "merge_bwd", (4, S // TM), dy2, pl.BlockSpec((TM, D), lambda n, i: (i, 0)),
        wout, pl.BlockSpec((256, D), lambda n, i: (n, 0)), (1, 1),
        [_sds((S, D), BF16)] * 4 + [_sds((8, D), F32)], [out, out, out, out, pl.BlockSpec((8, 256), lambda n, i: (0, n))],
        epilogue, extras=(ya, yb, proj, proj, bm),
        extra_specs=(out, out, pl.BlockSpec((TM, 256), lambda n, i: (i, GA_COL + n)),
                     pl.BlockSpec((TM, 256), lambda n, i: (i, GB_COL + n)), pl.BlockSpec((2, 256), lambda n, i: (0, n))),
        semantics=("parallel", "arbitrary"))


def _back_through_cols(name, dy, w4, width):
    S = dy.shape[0]
    return _mm(
        name, (S // TM, 4), dy, pl.BlockSpec((TM, 256), lambda i, s: (i, s)),
        w4, pl.BlockSpec((None, width, 256), lambda i, s: (s, 0, 0)), (1, 1),
        [_sds((S, width), BF16)], [pl.BlockSpec((TM, width), lambda i, s: (i, 0))], _store(BF16),
        nk=4, acc_shape=(TM, width), semantics=("parallel", "arbitrary"))[0]


ADA_SHARD = 2304
ADA_TN = 768


def _ada_fwd(c_all, w_ada_l, b_l):
    def body(c_ref, w_ref, b_ref, o_ref):
        cv = c_ref[...]
        ca = cv * _sigmoid(cv)
        o_ref[...] = jnp.dot(ca.astype(BF16), w_ref[...].astype(BF16), preferred_element_type=F32) + b_ref[...]

    return pl.pallas_call(
        body, name="ada_fwd", grid=(ADA_SHARD // ADA_TN,),
        in_specs=[pl.BlockSpec((8, D), lambda j: (0, 0)), pl.BlockSpec((D, ADA_TN), lambda j: (0, j)),
                  pl.BlockSpec((1, ADA_TN), lambda j: (0, j))],
        out_specs=pl.BlockSpec((8, ADA_TN), lambda j: (0, j)), out_shape=_sds((8, ADA_SHARD), F32),
        compiler_params=_params(("parallel",)),
    )(c_all, w_ada_l, b_l)


def _ada_bwd(c_all_t, dmod_l):
    def body(c_ref, d_ref, o_ref):
        cv = c_ref[...]
        ca = cv * _sigmoid(cv)
        o_ref[...] = jnp.dot(ca.astype(BF16).astype(F32), d_ref[...].astype(BF16).astype(F32),
                             preferred_element_type=F32, precision=lax.Precision.HIGHEST)

    return pl.pallas_call(
        body, name="ada_bwd", grid=(ADA_SHARD // ADA_TN,),
        in_specs=[pl.BlockSpec((D, 8), lambda j: (0, 0)), pl.BlockSpec((8, ADA_TN), lambda j: (0, j))],
        out_specs=pl.BlockSpec((D, ADA_TN), lambda j: (0, j)), out_shape=_sds((D, ADA_SHARD), F32),
        compiler_params=_params(("parallel",)),
    )(c_all_t, dmod_l)


def _sum_rows(name, x):
    n = x.shape[1]

    def body(x_ref, o_ref):
        s = x_ref[0:1, :]
        for d in range(1, 8):
            s = s + x_ref[d:d + 1, :]
        o_ref[...] = s

    return pl.pallas_call(
        body, name=name, in_specs=[pl.BlockSpec(memory_space=pltpu.VMEM)],
        out_specs=pl.BlockSpec(memory_space=pltpu.VMEM), out_shape=_sds((1, n), F32),
        compiler_params=pltpu.CompilerParams(vmem_limit_bytes=VMEM_LIMIT),
    )(x)


def _pair_sum(name, g4, recv, c_idx):
    _, _, h, C = g4.shape
    tr = h if h <= 512 else h // (h // 256) if h % 256 == 0 else h // 2

    def body(c_ref, g_ref, r_ref, o_ref):
        o_ref[...] = (g_ref[...].astype(F32) + r_ref[...].astype(F32)).astype(BF16)

    grid_spec = pltpu.PrefetchScalarGridSpec(
        num_scalar_prefetch=1, grid=(4, h // tr),
        in_specs=[pl.BlockSpec((None, None, tr, C), lambda k, i, c: (k, c[0], i, 0)),
                  pl.BlockSpec((None, tr, C), lambda k, i, c: (k, i, 0))],
        out_specs=pl.BlockSpec((None, tr, C), lambda k, i, c: (k, i, 0)))
    return pl.pallas_call(
        body, name=name, grid_spec=grid_spec, out_shape=_sds((4, h, C), BF16),
        compiler_params=_params(("parallel", "parallel")),
    )(c_idx, g4, recv)


def _sum4(name, q, p, chip_idx):
    _, h, C = q.shape
    tr = h if h <= 512 else h // (h // 256) if h % 256 == 0 else h // 2

    def body(k_ref, q_ref, p_ref, o_ref):
        me = k_ref[0]
        terms = [jnp.where(me == k, p_ref[...], q_ref[k]).astype(F32) for k in range(4)]
        o_ref[...] = ((terms[0] + terms[1]) + terms[2]) + terms[3]

    grid_spec = pltpu.PrefetchScalarGridSpec(
        num_scalar_prefetch=1, grid=(h // tr,),
        in_specs=[pl.BlockSpec((4, tr, C), lambda i, k: (0, i, 0)),
                  pl.BlockSpec((None, tr, C), lambda i, k: (k[0], i, 0))],
        out_specs=pl.BlockSpec((tr, C), lambda i, k: (i, 0)))
    return pl.pallas_call(
        body, name=name, grid_spec=grid_spec, out_shape=_sds((h, C), F32), compiler_params=_params(("parallel",)),
    )(chip_idx, q, p)


def _adamw(name, w, g, m, v):
    R, C = w.shape
    tr = R
    while tr * C * 4 > (1 << 20) and tr % 16 == 0:
        tr //= 2
    c1 = 1.0 - ADAM_B1 ** ADAM_STEP
    c2 = 1.0 - ADAM_B2 ** ADAM_STEP

    def body(w_ref, g_ref, m_ref, v_ref, d_ref, nm_ref, nv_ref):
        gv = g_ref[...]
        nm = ADAM_B1 * m_ref[...] + (1.0 - ADAM_B1) * gv
        nv = ADAM_B2 * v_ref[...] + (1.0 - ADAM_B2) * (gv * gv)
        nm_ref[...] = nm
        nv_ref[...] = nv
        d_ref[...] = -ADAM_LR * ((nm * (1.0 / c1)) / (jnp.sqrt(nv * (1.0 / c2)) + ADAM_EPS) + ADAM_WD * w_ref[...])

    spec = pl.BlockSpec((tr, C), lambda i: (i, 0))
    return pl.pallas_call(
        body, name=name, grid=(R // tr,), in_specs=[spec] * 4, out_specs=[spec] * 3,
        out_shape=[_sds((R, C), F32)] * 3, compiler_params=_params(("parallel",)),
    )(w, g, m, v)


def _adamw_halves(name, w, own, sib, m, v, c_idx):
    R, C = w.shape
    h = R // 2
    tr = h
    while tr * C * 4 > (1 << 20) and tr % 16 == 0:
        tr //= 2
    nb = h // tr
    c1 = 1.0 - ADAM_B1 ** ADAM_STEP
    c2 = 1.0 - ADAM_B2 ** ADAM_STEP

    def body(c_ref, w_ref, own_ref, sib_ref, m_ref, v_ref, g_ref, d_ref, nm_ref, nv_ref):
        mine = (pl.program_id(0) // nb) == c_ref[0]
        gv = jnp.where(mine, own_ref[...], sib_ref[...])
        nm = ADAM_B1 * m_ref[...] + (1.0 - ADAM_B1) * gv
        nv = ADAM_B2 * v_ref[...] + (1.0 - ADAM_B2) * (gv * gv)
        g_ref[...] = gv
        nm_ref[...] = nm
        nv_ref[...] = nv
        d_ref[...] = -ADAM_LR * ((nm * (1.0 / c1)) / (jnp.sqrt(nv * (1.0 / c2)) + ADAM_EPS) + ADAM_WD * w_ref[...])

    spec = pl.BlockSpec((tr, C), lambda i, c: (i, 0))
    half = pl.BlockSpec((tr, C), lambda i, c: (i % nb, 0))
    grid_spec = pltpu.PrefetchScalarGridSpec(
        num_scalar_prefetch=1, grid=(R // tr,), in_specs=[spec, half, half, spec, spec], out_specs=[spec] * 4)
    return pl.pallas_call(
        body, name=name, grid_spec=grid_spec, out_shape=[_sds((R, C), F32)] * 4,
        compiler_params=_params(("parallel",)),
    )(c_idx, w, own, sib, m, v)


def _pack(g, scale, shift, gate):
    rows = jnp.stack([g, scale, shift, gate]).astype(F32)
    return jnp.concatenate([rows, jnp.zeros((4, D), F32)], axis=0)


def _fold8(vec):
    m = -(-vec.shape[0] // (8 * LANES)) * LANES
    return jnp.concatenate([vec, jnp.zeros((8 * m - vec.shape[0],), vec.dtype)]).reshape(8, m)


def _allgather_vectors(name, vec):
    return _allgather_rows(name, _fold8(vec)).reshape(8, -1)


def kernel(x, c, w_ada, b_ada, norm1_g, ffn1_w_gu, ffn1_w_down, norm2_g, w_mix_in, b_merge, conv_w, w_conv_out, w_attn_out, w_out, norm3_g, ffn2_w_gu, ffn2_w_down, final_g, loss_target, m_w_ada, m_b_ada, m_norm1_g, m_ffn1_w_gu, m_ffn1_w_down, m_norm2_g, m_w_mix_in, m_b_merge, m_conv_w, m_w_conv_out, m_w_attn_out, m_w_out, m_norm3_g, m_ffn2_w_gu, m_ffn2_w_down, m_final_g, v_w_ada, v_b_ada, v_norm1_g, v_ffn1_w_gu, v_ffn1_w_down, v_norm2_g, v_w_mix_in, v_b_merge, v_conv_w, v_w_conv_out, v_w_attn_out, v_w_out, v_norm3_g, v_ffn2_w_gu, v_ffn2_w_down, v_final_g):
    xi, yi, ci = lax.axis_index("x"), lax.axis_index("y"), lax.axis_index("c")
    chip = 2 * xi + yi
    dev = 4 * xi + 2 * yi + ci
    S = x.shape[1]
    h0 = x[0]
    target = loss_target[0]

    wgu1, wd1 = _allgather_weights([w[0].astype(BF16) for w in (ffn1_w_gu, ffn1_w_down)])
    wd1 = wd1.reshape(D_FF, D)
    late_shards = [w[0].astype(BF16) for w in (w_conv_out, w_attn_out, w_out, ffn2_w_gu, ffn2_w_down)]
    c_idx = jnp.reshape(ci, (1,)).astype(jnp.int32)
    chip_idx = jnp.reshape(chip, (1,)).astype(jnp.int32)

    def reduce_pairs(tag, names, grads):
        g4 = [g.reshape(4, 2, g.shape[1] // 2, g.shape[2]) for g in grads]
        recv = _sibling_swap_halves("grad_sibling_swap_" + tag, g4)
        return [_pair_sum("pair_sum_" + nm, a, b, c_idx) for nm, a, b in zip(names, g4, recv)]

    small = jnp.concatenate([c[0], b_merge[0].reshape(-1), conv_w[0].reshape(-1)])
    gathered = _allgather_vectors("allgather_small", small)
    c_all = gathered[:, :D]
    per_chip = gathered[0::2]
    bm_full = jnp.concatenate([per_chip[k, D:D + 512].reshape(2, 256) for k in range(4)], axis=1)
    cw_full = jnp.concatenate([per_chip[k, D + 512:D + 896].reshape(3, 128) for k in range(4)], axis=1)
    b_l = lax.dynamic_slice_in_dim(b_ada, chip * ADA_SHARD, ADA_SHARD, axis=1)
    mod_l = _ada_fwd(c_all, w_ada[0], b_l)
    mod_g = _allgather_rows("allgather_mod", mod_l).reshape(8, 8, ADA_SHARD)
    mod_all = jnp.concatenate([mod_g[2 * k] for k in range(4)], axis=1)
    mod = lax.dynamic_slice_in_dim(mod_all, dev, 1, axis=0).reshape(3, 3, D)
    p1 = _pack(norm1_g[0], mod[0, 1], mod[0, 0], mod[0, 2])
    p2 = _pack(norm2_g[0], mod[1, 1], mod[1, 0], mod[1, 2])
    p3 = _pack(norm3_g[0], mod[2, 1], mod[2, 0], mod[2, 2])
    pf = _pack(final_g, final_g, final_g, final_g)

    u1 = _norm_mod_fwd("norm1_fwd", h0, p1)
    gu1, hm1, (wmix,) = _ffn_up("ffn1_up", u1, wgu1, [w_mix_in[0].astype(BF16)])
    f1, h1 = _proj_residual("ffn1_down", hm1, wd1, h0, p1, 0.5)
    u2 = _norm_mod_fwd("norm2_fwd", h1, p2)
    proj = _mm("mix_in", (4, S // TM), u2, pl.BlockSpec((TM, D), lambda s, i: (i, 0)),
               wmix, pl.BlockSpec((None, D, MIX_SHARD), lambda s, i: (s, 0, 0)), (1, 0),
               [_sds((S, MIX_W), BF16)], [pl.BlockSpec((TM, MIX_SHARD), lambda s, i: (i, s))], _store(BF16),
               semantics=("parallel", "parallel"))[0]
    sc = _conv_fwd(proj, cw_full)
    o, t_tot, (wco, wao, wout, wgu2, wd2) = _attn_fwd(proj, late_shards)
    wout = wout.reshape(D, D)
    wd2 = wd2.reshape(D_FF, D)
    ya, yb, merged = _merge_fwd(sc, o, wco, wao, proj, bm_full)
    y2, h2 = _proj_residual("mix_out", merged, wout, h1, p2, 1.0)
    u3 = _norm_mod_fwd("norm3_fwd", h2, p3)
    gu3, hm3, _ = _ffn_up("ffn2_up", u3, wgu2)
    f3, h3 = _proj_residual("ffn2_down", hm3, wd2, h2, p3, 0.5)

    dh3, df3, sums_f, loss_blk = _final_loss_bwd(h3, pf, target, p3, f3)
    du3, dwgu2, dwd2 = _ffn_bwd("ffn2", df3, u3, gu3, hm3, wgu2, wd2)
    dh2, dy2, sums3 = _norm_mod_bwd("norm3_bwd", du3, h2, p3, dh3, prev=(p2, y2, 1.0))

    dya, dyb, dga, dgb, sums_bm = _merge_bwd(dy2, wout, ya, yb, proj, bm_full)
    dwout = _mm("dw_out", (1, S // TK), merged, pl.BlockSpec((TK, D), lambda n, k: (k, 0)),
                dy2, pl.BlockSpec((TK, D), lambda n, k: (k, 0)), (0, 0),
                [_sds((D, D), BF16)], [pl.BlockSpec((D, D), lambda n, k: (0, 0))], _store(BF16),
                nk=S // TK, acc_shape=(D, D))[0]
    dsc = _back_through_cols("conv_out_bwd", dya, wco, 512)
    do = _back_through_cols("attn_out_bwd", dyb, wao, 512)
    dwco = _grad_w("dw_conv_out", sc, 512, dya, 256, lambda s, k: (k, s), 4, (4, 512, 256), (None, 512, 256),
                   lambda s, k: (s, 0, 0))
    dwao = _grad_w("dw_attn_out", o, 512, dyb, 256, lambda s, k: (k, s), 4, (4, 512, 256), (None, 512, 256),
                   lambda s, k: (s, 0, 0))
    dcb, dcc, dcx, dcw = _conv_bwd(dsc, proj, cw_full)
    names_e = ["ffn2_w_gu", "ffn2_w_down", "w_out", "w_conv_out", "w_attn_out"]
    part_e = reduce_pairs("early", names_e, [dwgu2, dwd2.reshape(4, 704, D), dwout.reshape(4, 256, D), dwco, dwao])
    dq, dk, dv, came_e = _attn_bwd(proj, t_tot, do, part_e)
    dproj = jnp.concatenate([dcb, dcc, dcx, dq, dk, dv, dga, dgb], axis=1)
    du2 = _mm("mix_in_bwd", (S // TM, 4), dproj, pl.BlockSpec((TM, MIX_SHARD), lambda i, s: (i, s)),
              wmix, pl.BlockSpec((None, D, MIX_SHARD), lambda i, s: (s, 0, 0)), (1, 1),
              [_sds((S, D), F32)], [pl.BlockSpec((TM, D), lambda i, s: (i, 0))], _store(F32),
              nk=4, acc_shape=(TM, D), semantics=("parallel", "arbitrary"))[0]
    dwmix = _grad_w("dw_mix_in", u2, D, dproj, MIX_SHARD, lambda s, k: (k, s), 4, (4, D, MIX_SHARD),
                    (None, D, MIX_SHARD), lambda s, k: (s, 0, 0))
    dh1, df1, sums2 = _norm_mod_bwd("norm2_bwd", du2, h1, p2, dh2, prev=(p1, f1, 0.5))

    part_mix = reduce_pairs("mix", ["w_mix_in"], [dwmix])
    dgu1, *came_mix = _ffn_down_bwd("ffn1_down_bwd", df1, wd1, gu1, a2a_parts=part_mix)
    dwd1 = _ffn_dw_down("ffn1_dw_down", hm1, df1)
    part_wd1 = reduce_pairs("wd1", ["ffn1_w_down"], [dwd1.reshape(4, 704, D)])
    dwgu1, *came_wd1 = _ffn_dw_gu("ffn1_dw_gu", u1, dgu1, a2a_parts=part_wd1)
    part_gu1 = reduce_pairs("gu1", ["ffn1_w_gu"], [dwgu1])
    du1, *came_gu1 = _ffn_up_bwd("ffn1_up_bwd", dgu1, wgu1, a2a_parts=part_gu1)
    grad_x, sums1 = _norm_mod_bwd("norm1_bwd", du1, h0, p1, dh1)

    dmod = jnp.stack([sums1[0], sums1[1], sums2[3], sums2[0], sums2[1], sums3[3], sums3[0], sums3[1], sums_f[1]])
    small_g = jnp.concatenate([dmod.reshape(-1), sums1[2], sums2[2], sums3[2], sums_f[0],
                               sums_bm[0], sums_bm[1], dcw.reshape(-1), loss_blk[0, 0:1]])
    all_g = _allgather_vectors("allgather_small_grads", small_g)
    tot = _sum_rows("sum_small_grads", all_g)[0]
    loss = tot[16 * D + 512]
    g_b_ada = tot[:9 * D][None, :]
    g_n1, g_n2, g_n3 = (tot[(9 + k) * D:(10 + k) * D][None, :] for k in range(3))
    g_fin = tot[12 * D:13 * D]
    g_bm = lax.dynamic_slice_in_dim(tot[13 * D:15 * D].reshape(2, D), chip * 256, 256, axis=1)[None]
    g_cw = lax.dynamic_slice_in_dim(tot[15 * D:16 * D + 512].reshape(3, 512), chip * 128, 128, axis=1)[None]
    dmod_l = lax.dynamic_slice_in_dim(all_g[:, :9 * D], chip * ADA_SHARD, ADA_SHARD, axis=1)
    g_w_ada = _ada_bwd(c_all.T, dmod_l)[None]

    names = names_e + ["w_mix_in", "ffn1_w_down", "ffn1_w_gu"]
    came = list(came_e) + came_mix + came_wd1 + came_gu1
    part = part_e + part_mix + part_wd1 + part_gu1
    half = [_sum4("chip_sum_" + nm, q, p, chip_idx) for nm, q, p in zip(names, came, part)]
    g_own = dict(zip(names, half))
    g_sib = dict(zip(names, _sibling_share(half)))

    weights = dict(w_ada=w_ada, b_ada=b_ada, norm1_g=norm1_g, ffn1_w_gu=ffn1_w_gu, ffn1_w_down=ffn1_w_down,
                   norm2_g=norm2_g, w_mix_in=w_mix_in, b_merge=b_merge, conv_w=conv_w, w_conv_out=w_conv_out,
                   w_attn_out=w_attn_out, w_out=w_out, norm3_g=norm3_g, ffn2_w_gu=ffn2_w_gu,
                   ffn2_w_down=ffn2_w_down, final_g=final_g)
    ms = dict(w_ada=m_w_ada, b_ada=m_b_ada, norm1_g=m_norm1_g, ffn1_w_gu=m_ffn1_w_gu, ffn1_w_down=m_ffn1_w_down,
              norm2_g=m_norm2_g, w_mix_in=m_w_mix_in, b_merge=m_b_merge, conv_w=m_conv_w, w_conv_out=m_w_conv_out,
              w_attn_out=m_w_attn_out, w_out=m_w_out, norm3_g=m_norm3_g, ffn2_w_gu=m_ffn2_w_gu,
              ffn2_w_down=m_ffn2_w_down, final_g=m_final_g)
    vs = dict(w_ada=v_w_ada, b_ada=v_b_ada, norm1_g=v_norm1_g, ffn1_w_gu=v_ffn1_w_gu, ffn1_w_down=v_ffn1_w_down,
              norm2_g=v_norm2_g, w_mix_in=v_w_mix_in, b_merge=v_b_merge, conv_w=v_conv_w, w_conv_out=v_w_conv_out,
              w_attn_out=v_w_attn_out, w_out=v_w_out, norm3_g=v_norm3_g, ffn2_w_gu=v_ffn2_w_gu,
              ffn2_w_down=v_ffn2_w_down, final_g=v_final_g)
    order = list(weights)
    grad = dict(w_ada=g_w_ada, b_ada=g_b_ada, norm1_g=g_n1, norm2_g=g_n2, norm3_g=g_n3, final_g=g_fin,
                b_merge=g_bm, conv_w=g_cw)
    delta, new_m, new_v = {}, {}, {}
    small_names = ["b_ada", "norm1_g", "norm2_g", "norm3_g", "final_g", "b_merge", "conv_w"]
    flat = lambda d: jnp.concatenate([d[nm].reshape(-1) for nm in small_names])[None, :]
    sd, sm, sv = _adamw("adamw_small", flat(weights), flat(grad), flat(ms), flat(vs))
    off = 0
    for nm in small_names:
        size = weights[nm].size
        for dst, src in ((delta, sd), (new_m, sm), (new_v, sv)):
            dst[nm] = src[0, off:off + size].reshape(weights[nm].shape)
        off += size
    for nm in order:
        if nm in small_names:
            continue
        shp = weights[nm].shape
        if nm in g_own:
            g2, d2, m2, v2 = _adamw_halves("adamw_" + nm, weights[nm][0], g_own[nm], g_sib[nm], ms[nm][0], vs[nm][0],
                                           c_idx)
            grad[nm] = g2.reshape(shp)
        else:
            d2, m2, v2 = _adamw("adamw_" + nm, weights[nm][0], grad[nm][0], ms[nm][0], vs[nm][0])
        delta[nm], new_m[nm], new_v[nm] = d2.reshape(shp), m2.reshape(shp), v2.reshape(shp)

    return (loss, grad_x[None], *[grad[nm] for nm in order], *[delta[nm] for nm in order],
            *[new_m[nm] for nm in order], *[new_v[nm] for nm in order])
```

```python
import functools

import jax
import jax.numpy as jnp
from jax import lax
from jax.experimental import pallas as pl
from jax.experimental.pallas import tpu as pltpu

F32 = jnp.float32
BF16 = jnp.bfloat16
MESH = pl.DeviceIdType.MESH

VMEM_LIMIT = 56 * 1024 * 1024
LANES = 128

D = 1024
D_FF = 2816
FF_SHARD = 1408
MIX_SHARD = 1280
MIX_W = 5120
HEAD_PAIRS = 4
HEAD_DIM = 64
CONV_W = 512
EPS = 1e-6
ATT_BLK = 256

ADAM_LR = 0.001
ADAM_B1 = 0.9
ADAM_B2 = 0.999
ADAM_EPS = 1e-08
ADAM_WD = 0.01
ADAM_STEP = 10


def _params(semantics=None):
    return pltpu.CompilerParams(dimension_semantics=semantics, vmem_limit_bytes=VMEM_LIMIT)


def _sigmoid(x):
    return 1.0 / (1.0 + jnp.exp(-x))


def _place():
    x, y, c = lax.axis_index("x"), lax.axis_index("y"), lax.axis_index("c")
    chips = [(1 - x, y), (x, 1 - y), (1 - x, 1 - y)]
    return x, y, c, chips


def _allgather_rows(name, blk):
    m_per, n = blk.shape

    def body(x_ref, out_ref, send_sems, recv_sems, local_sem):
        x, y, c, chips = _place()
        me, sibling = (x, y, c), (x, y, 1 - c)

        def rows(px, py, pc):
            return out_ref.at[pl.ds((4 * px + 2 * py + pc) * m_per, m_per), :]

        def copy(k, block, to, src=None):
            return pltpu.make_async_remote_copy(
                src_ref=rows(*block) if src is None else src, dst_ref=rows(*block),
                send_sem=send_sems.at[k], recv_sem=recv_sems.at[k], device_id=to, device_id_type=MESH)

        mine = pltpu.make_async_copy(x_ref, rows(*me), local_sem)
        mine.start()
        first = [copy(0, me, sibling, src=x_ref)]
        first += [copy(1 + j, me, (*chip, c), src=x_ref) for j, chip in enumerate(chips)]
        for cp in first:
            cp.start()
        passed = [copy(4 + j, (*chip, c), sibling) for j, chip in enumerate(chips)]
        for j, chip in enumerate(chips):
            copy(1 + j, (*chip, c), me).wait_recv()
            passed[j].start()
        copy(0, sibling, me).wait_recv()
        for j, chip in enumerate(chips):
            copy(4 + j, (*chip, 1 - c), me).wait_recv()
        for cp in first + passed:
            cp.wait_send()
        mine.wait()

    return pl.pallas_call(
        body, name=name,
        out_shape=jax.ShapeDtypeStruct((8 * m_per, n), blk.dtype),
        in_specs=[pl.BlockSpec(memory_space=pltpu.VMEM)],
        out_specs=pl.BlockSpec(memory_space=pltpu.VMEM),
        scratch_shapes=[pltpu.SemaphoreType.DMA((7,)), pltpu.SemaphoreType.DMA((7,)), pltpu.SemaphoreType.DMA],
        compiler_params=pltpu.CompilerParams(vmem_limit_bytes=VMEM_LIMIT),
    )(blk)


def _hbm_specs(n):
    return [pl.BlockSpec(memory_space=pltpu.HBM)] * n


def _allgather_weights(shards):
    n = len(shards)

    def body(*refs):
        start, relay, finish = _gather_protocol(refs[:n], refs[n:2 * n], *refs[2 * n:])
        start()
        relay()
        finish()

    gathered = pl.pallas_call(
        body, name="allgather_weights",
        out_shape=_gather_shapes(shards), in_specs=_hbm_specs(n), out_specs=_hbm_specs(n),
        scratch_shapes=_gather_sems(n),
    )(*shards)
    return _with_own_shard(gathered, shards)


def _gather_shapes(shards):
    return [jax.ShapeDtypeStruct((4, *s.shape), s.dtype) for s in shards]


def _gather_sems(n):
    return [pltpu.SemaphoreType.DMA((6 * n,)), pltpu.SemaphoreType.DMA((6 * n,))]


def _with_own_shard(gathered, shards):
    chip = 2 * lax.axis_index("x") + lax.axis_index("y")
    return [lax.dynamic_update_slice(g, s[None], (chip, 0, 0)) for g, s in zip(gathered, shards)]


def _gather_protocol(ins, outs, send_sems, recv_sems):
    n = len(ins)
    x, y, c, chips = _place()
    me, sibling = (x, y, c), (x, y, 1 - c)
    me_k = 2 * x + y

    def half(w, k, hc):
        h = ins[w].shape[0] // 2
        return outs[w].at[k, pl.ds(pl.multiple_of(hc * h, 8), h), :]

    def copy(w, j, k, hc, to, src=None):
        dst = half(w, k, hc)
        return pltpu.make_async_remote_copy(
            src_ref=dst if src is None else src, dst_ref=dst,
            send_sem=send_sems.at[6 * w + j], recv_sem=recv_sems.at[6 * w + j],
            device_id=to, device_id_type=MESH)

    def first(w, j):
        h = ins[w].shape[0] // 2
        src = ins[w].at[pl.ds(pl.multiple_of(c * h, 8), h), :]
        return copy(w, j, me_k, c, (*chips[j], c), src=src)

    def passed(w, j):
        px, py = chips[j]
        return copy(w, 3 + j, 2 * px + py, c, sibling)

    pairs = [(w, j) for w in range(n) for j in range(3)]

    def start():
        for w, j in pairs:
            first(w, j).start()

    def relay():
        for w, j in pairs:
            px, py = chips[j]
            copy(w, j, 2 * px + py, c, me).wait_recv()
            passed(w, j).start()

    def finish():
        for w, j in pairs:
            px, py = chips[j]
            copy(w, 3 + j, 2 * px + py, 1 - c, me).wait_recv()
        for w, j in pairs:
            first(w, j).wait_send()
            passed(w, j).wait_send()

    return start, relay, finish


def _sibling_swap_halves(name, grads):
    n = len(grads)

    def body(*refs):
        ins, outs = refs[:n], refs[n:2 * n]
        send_sems, recv_sems = refs[2 * n:]
        x, y, c, _ = _place()
        cps = []
        for w in range(n):
            cp = pltpu.make_async_remote_copy(
                src_ref=ins[w].at[:, 1 - c], dst_ref=outs[w],
                send_sem=send_sems.at[w], recv_sem=recv_sems.at[w],
                device_id=(x, y, 1 - c), device_id_type=MESH)
            cp.start()
            cps.append(cp)
        for cp in cps:
            cp.wait()

    return pl.pallas_call(
        body, name=name,
        out_shape=[jax.ShapeDtypeStruct((4, *g.shape[2:]), g.dtype) for g in grads],
        in_specs=_hbm_specs(n), out_specs=_hbm_specs(n),
        scratch_shapes=[pltpu.SemaphoreType.DMA((n,)), pltpu.SemaphoreType.DMA((n,))],
    )(*grads)


def _all_to_all_sems(n):
    return [pltpu.SemaphoreType.DMA((3 * n,)), pltpu.SemaphoreType.DMA((3 * n,))]


def _all_to_all_protocol(ins, outs, send_sems, recv_sems):
    n = len(ins)
    x, y, c, chips = _place()
    me_k = 2 * x + y
    pairs = [(w, j) for w in range(n) for j in range(3)]

    def sent(w, j):
        px, py = chips[j]
        return pltpu.make_async_remote_copy(
            src_ref=ins[w].at[2 * px + py], dst_ref=outs[w].at[me_k],
            send_sem=send_sems.at[3 * w + j], recv_sem=recv_sems.at[3 * w + j],
            device_id=(px, py, c), device_id_type=MESH)

    def start():
        for w, j in pairs:
            sent(w, j).start()

    def finish():
        for w, j in pairs:
            px, py = chips[j]
            slab = outs[w].at[2 * px + py]
            pltpu.make_async_remote_copy(
                src_ref=slab, dst_ref=slab, send_sem=send_sems.at[3 * w + j],
                recv_sem=recv_sems.at[3 * w + j], device_id=(px, py, c), device_id_type=MESH).wait_recv()
        for w, j in pairs:
            sent(w, j).wait_send()

    return start, finish


def _sibling_share(halves):
    n = len(halves)

    def body(*refs):
        ins, outs = refs[:n], refs[n:2 * n]
        send_sems, recv_sems = refs[2 * n:]
        x, y, c, _ = _place()
        cps = []
        for w in range(n):
            cp = pltpu.make_async_remote_copy(
                src_ref=ins[w], dst_ref=outs[w], send_sem=send_sems.at[w], recv_sem=recv_sems.at[w],
                device_id=(x, y, 1 - c), device_id_type=MESH)
            cp.start()
            cps.append(cp)
        for cp in cps:
            cp.wait()

    return pl.pallas_call(
        body, name="grad_sibling_share",
        out_shape=[jax.ShapeDtypeStruct(p.shape, p.dtype) for p in halves],
        in_specs=_hbm_specs(n), out_specs=_hbm_specs(n),
        scratch_shapes=[pltpu.SemaphoreType.DMA((n,)), pltpu.SemaphoreType.DMA((n,))],
    )(*halves)


def _mm(name, grid, a, a_spec, b, b_spec, contract, out_shapes, out_specs, epilogue,
        extras=(), extra_specs=(), nk=1, acc_shape=None, semantics=None, a2a_parts=(), gather_shards=()):
    assert not (a2a_parts and gather_shards)
    moved = tuple(a2a_parts) + tuple(gather_shards)
    ne, no, nc = len(extras), len(out_shapes), len(moved)
    nd = len(grid)

    def body(*refs):
        a_ref, b_ref = refs[0], refs[1]
        ex, outs = refs[2:2 + ne], refs[2 + ne + nc:2 + ne + nc + no]
        if nc:
            ids = [pl.program_id(d) for d in range(nd)]
            comm_refs = (refs[2 + ne:2 + ne + nc], refs[2 + ne + nc + no:2 + ne + 2 * nc + no], *refs[-2:])
            at_start = functools.reduce(jnp.logical_and, [i == 0 for i in ids])
            if a2a_parts:
                start, finish = _all_to_all_protocol(*comm_refs)
                pl.when(at_start)(start)
            else:
                start, relay, finish = _gather_protocol(*comm_refs)
                pl.when(at_start)(start)
                pl.when(functools.reduce(jnp.logical_and, [ids[0] == grid[0] // 2] + [i == 0 for i in ids[1:]]))(relay)

        def prod():
            return lax.dot_general(a_ref[...], b_ref[...], (((contract[0],), (contract[1],)), ((), ())),
                                   preferred_element_type=F32)

        if nk == 1:
            epilogue(prod(), ex, outs)
        else:
            acc = refs[2 + ne + 2 * nc + no]
            k = pl.program_id(nd - 1)

            @pl.when(k == 0)
            def _():
                acc[...] = prod()

            @pl.when(k > 0)
            def _():
                acc[...] += prod()

            @pl.when(k == nk - 1)
            def _():
                epilogue(acc[...], ex, outs)

        if nc:
            pl.when(functools.reduce(jnp.logical_and, [i == g - 1 for i, g in zip(ids, grid)]))(finish)

    if semantics is None or nc:
        semantics = ("arbitrary",) * nd
    return pl.pallas_call(
        body, name=name, grid=grid,
        in_specs=[a_spec, b_spec, *extra_specs] + _hbm_specs(nc),
        out_specs=list(out_specs) + _hbm_specs(nc),
        out_shape=list(out_shapes) + [jax.ShapeDtypeStruct(p.shape, p.dtype) for p in a2a_parts]
        + _gather_shapes(gather_shards),
        scratch_shapes=([] if nk == 1 else [pltpu.VMEM(acc_shape, F32)])
        + (_all_to_all_sems(nc) if a2a_parts else _gather_sems(nc) if gather_shards else []),
        compiler_params=_params(semantics),
    )(a, b, *extras, *moved)


def _store(dtype):
    def epilogue(acc, ex, outs):
        outs[0][...] = acc.astype(dtype)
    return epilogue


def _sds(shape, dtype):
    return jax.ShapeDtypeStruct(shape, dtype)


TR = 512


def _row_spec(width, tr=TR):
    return pl.BlockSpec((tr, width), lambda i: (i, 0))


def _const_spec(shape):
    nd = len(shape)
    return pl.BlockSpec(shape, lambda i: (0,) * nd)


def _norm_mod_fwd(name, h, p):
    S = h.shape[0]

    def body(h_ref, p_ref, u_ref):
        hv = h_ref[...]
        r = lax.rsqrt(jnp.mean(hv * hv, axis=-1, keepdims=True) + EPS)
        nrm = (hv * r) * p_ref[0:1, :]
        u_ref[...] = (nrm * (1.0 + p_ref[1:2, :]) + p_ref[2:3, :]).astype(BF16)

    return pl.pallas_call(
        body, name=name, grid=(S // TR,),
        in_specs=[_row_spec(D), _const_spec((8, D))], out_specs=_row_spec(D),
        out_shape=_sds((S, D), BF16), compiler_params=_params(("parallel",)),
    )(h, p)


def _norm_mod_bwd(name, du, h, p, dh_res, prev=None):
    S = h.shape[0]
    has_prev = prev is not None

    def body(*refs):
        if has_prev:
            du_ref, h_ref, p_ref, r_ref, pp_ref, f_ref, dh_ref, df_ref, sums_ref = refs
        else:
            du_ref, h_ref, p_ref, r_ref, dh_ref, sums_ref = refs
        i = pl.program_id(0)

        @pl.when(i == 0)
        def _():
            sums_ref[...] = jnp.zeros_like(sums_ref)

        hv = h_ref[...]
        duv = du_ref[...]
        g = p_ref[0:1, :]
        one_scale = 1.0 + p_ref[1:2, :]
        r = lax.rsqrt(jnp.mean(hv * hv, axis=-1, keepdims=True) + EPS)
        xn = hv * r
        dn = duv * one_scale
        dxn = dn * g
        dh = r_ref[...] + r * (dxn - xn * jnp.mean(dxn * xn, axis=-1, keepdims=True))
        dh_ref[...] = dh
        sums_ref[0:1, :] += jnp.sum(duv, axis=0, keepdims=True)
        sums_ref[1:2, :] += jnp.sum(duv * (xn * g), axis=0, keepdims=True)
        sums_ref[2:3, :] += jnp.sum(dn * xn, axis=0, keepdims=True)
        if has_prev:
            wgt = prev[2]
            df_ref[...] = (wgt * pp_ref[3:4, :] * dh).astype(BF16)
            sums_ref[3:4, :] += wgt * jnp.sum(dh * f_ref[...].astype(F32), axis=0, keepdims=True)

    ins = [du, h, p, dh_res]
    in_specs = [_row_spec(D), _row_spec(D), _const_spec((8, D)), _row_spec(D)]
    out_specs = [_row_spec(D)]
    out_shape = [_sds((S, D), F32)]
    if has_prev:
        ins += [prev[0], prev[1]]
        in_specs += [_const_spec((8, D)), _row_spec(D)]
        out_specs.append(_row_spec(D))
        out_shape.append(_sds((S, D), BF16))
    out_specs.append(_const_spec((8, D)))
    out_shape.append(_sds((8, D), F32))
    return pl.pallas_call(
        body, name=name, grid=(S // TR,), in_specs=in_specs, out_specs=out_specs, out_shape=out_shape,
        compiler_params=_params(("arbitrary",)),
    )(*ins)


TM = 512


def _ffn_up(name, u, wgu4, shards=()):
    S = u.shape[0]
    n = len(shards)
    ni = S // TM

    def body(u_ref, wg_ref, wu_ref, *rest):
        gu_ref, hm_ref = rest[n:n + 2]
        s, i = pl.program_id(0), pl.program_id(1)
        if n:
            start, relay, finish = _gather_protocol(rest[:n], rest[n + 2:2 * n + 2], *rest[2 * n + 2:])
            pl.when((s == 0) & (i == 0))(start)
            pl.when((s == 1) & (i == 0))(relay)
        uv = u_ref[...]
        g = jnp.dot(uv, wg_ref[...], preferred_element_type=F32)
        up = jnp.dot(uv, wu_ref[...], preferred_element_type=F32)
        gu_ref[0] = g.astype(BF16)
        gu_ref[1] = up.astype(BF16)
        hm_ref[...] = (g * _sigmoid(g) * up).astype(BF16)
        if n:
            pl.when((s == 1) & (i == ni - 1))(finish)

    gu, hm, *gathered = pl.pallas_call(
        body, name=name, grid=(2, ni),
        in_specs=[pl.BlockSpec((TM, D), lambda s, i: (i, 0)),
                  pl.BlockSpec((None, D, FF_SHARD), lambda s, i: (s, 0, 0)),
                  pl.BlockSpec((None, D, FF_SHARD), lambda s, i: (s + 2, 0, 0))] + _hbm_specs(n),
        out_specs=[pl.BlockSpec((2, TM, FF_SHARD), lambda s, i: (0, i, s)),
                   pl.BlockSpec((TM, FF_SHARD), lambda s, i: (i, s))] + _hbm_specs(n),
        out_shape=[_sds((2, S, D_FF), BF16), _sds((S, D_FF), BF16)] + _gather_shapes(shards),
        scratch_shapes=_gather_sems(n) if n else [],
        compiler_params=_params(("arbitrary", "arbitrary") if n else ("parallel", "parallel")),
    )(u, wgu4, wgu4, *shards)
    return gu, hm, _with_own_shard(gathered, shards)


def _rmsnorm_parts(hv):
    r = lax.rsqrt(jnp.mean(hv * hv, axis=-1, keepdims=True) + EPS)
    return r, hv * r


def _proj_residual(name, a, w, h, p, weight, p_next, gather_shards=()):
    S, K = a.shape

    def epilogue(acc, ex, outs):
        h_ref, p_ref, pn_ref = ex
        outs[0][...] = acc.astype(BF16)
        hout = h_ref[...] + weight * p_ref[3:4, :] * acc
        outs[1][...] = hout
        _, xn = _rmsnorm_parts(hout)
        outs[2][...] = ((xn * pn_ref[0:1, :]) * (1.0 + pn_ref[1:2, :]) + pn_ref[2:3, :]).astype(BF16)

    row = _row_spec(D, TM)
    res = _mm(
        name, (S // TM,), a, pl.BlockSpec((TM, K), lambda i: (i, 0)), w, pl.BlockSpec((K, D), lambda i: (0, 0)),
        (1, 0), [_sds((S, D), BF16), _sds((S, D), F32), _sds((S, D), BF16)], [row, row, row], epilogue,
        extras=(h, p, p_next), extra_specs=(row, _const_spec((8, D)), _const_spec((8, D))),
        semantics=("parallel",), gather_shards=gather_shards)
    return res[0], res[1], res[2], _with_own_shard(res[3:], gather_shards)


def _proj_residual_loss(name, a, w, h, p, weight, gf, target):
    S, K = a.shape

    def epilogue(acc, ex, outs):
        h_ref, p_ref, g_ref, t_ref = ex
        dh_ref, df_ref, sums_ref, loss_ref = outs

        @pl.when(pl.program_id(0) == 0)
        def _():
            sums_ref[...] = jnp.zeros_like(sums_ref)
            loss_ref[...] = jnp.zeros_like(loss_ref)

        gate = p_ref[3:4, :]
        g = g_ref[0:1, :]
        r, xn = _rmsnorm_parts(h_ref[...] + weight * gate * acc)
        err = xn * g - t_ref[...]
        loss_ref[...] += 0.5 * jnp.sum(err * err) * (1.0 / D)
        dout = err * (1.0 / D)
        dxn = dout * g
        dh = r * (dxn - xn * jnp.mean(dxn * xn, axis=-1, keepdims=True))
        dh_ref[...] = dh
        df_ref[...] = (weight * gate * dh).astype(BF16)
        sums_ref[0:1, :] += jnp.sum(dout * xn, axis=0, keepdims=True)
        sums_ref[1:2, :] += weight * jnp.sum(dh * acc, axis=0, keepdims=True)

    row = _row_spec(D, TM)
    return _mm(
        name, (S // TM,), a, pl.BlockSpec((TM, K), lambda i: (i, 0)), w, pl.BlockSpec((K, D), lambda i: (0, 0)),
        (1, 0), [_sds((S, D), F32), _sds((S, D), BF16), _sds((8, D), F32), _sds((8, LANES), F32)],
        [row, row, _const_spec((8, D)), _const_spec((8, LANES))], epilogue,
        extras=(h, p, gf, target), extra_specs=(row, _const_spec((8, D)), _const_spec((8, D)), row),
        semantics=("arbitrary",))


def _ffn_down_bwd(name, df, wd, gu, a2a_parts=()):
    S = df.shape[0]

    def epilogue(acc, ex, outs):
        g = ex[0][0].astype(F32)
        up = ex[0][1].astype(F32)
        sg = _sigmoid(g)
        outs[0][0] = (acc * up * (sg * (1.0 + g * (1.0 - sg)))).astype(BF16)
        outs[0][1] = (acc * g * sg).astype(BF16)

    gu_spec = pl.BlockSpec((2, TM, FF_SHARD), lambda n, i: (0, i, n))
    return _mm(
        name, (2, S // TM), df, pl.BlockSpec((TM, D), lambda n, i: (i, 0)),
        wd, pl.BlockSpec((FF_SHARD, D), lambda n, i: (n, 0)), (1, 1),
        [_sds((2, S, D_FF), BF16)], [gu_spec], epilogue, extras=(gu,), extra_specs=(gu_spec,),
        semantics=("parallel", "parallel"), a2a_parts=a2a_parts)


TK = 512


def _grad_w(name, a, a_w, b, b_w, b_map, n_out, out_shape, out_block, out_map, a2a_parts=()):
    S = a.shape[0]
    nk = S // TK
    res = _mm(
        name, (n_out, nk), a, pl.BlockSpec((TK, a_w), lambda s, k: (k, 0)), b, pl.BlockSpec(
            (None, TK, b_w) if b.ndim == 3 else (TK, b_w), b_map), (0, 0),
        [_sds(out_shape, BF16)], [pl.BlockSpec(out_block, out_map)], _store(BF16), nk=nk, acc_shape=(a_w, b_w),
        semantics=("parallel", "arbitrary"), a2a_parts=a2a_parts)
    return res if a2a_parts else res[0]


def _ffn_dw_down(name, hm, df):
    S = df.shape[0]
    return _mm(
        name, (2, S // TK), hm, pl.BlockSpec((TK, FF_SHARD), lambda m, k: (k, m)),
        df, pl.BlockSpec((TK, D), lambda m, k: (k, 0)), (0, 0),
        [_sds((D_FF, D), BF16)], [pl.BlockSpec((FF_SHARD, D), lambda m, k: (m, 0))], _store(BF16),
        nk=S // TK, acc_shape=(FF_SHARD, D), semantics=("parallel", "arbitrary"))[0]


def _ffn_up_bwd(name, dgu, wgu4, a2a_parts=()):
    S = dgu.shape[1]
    return _mm(
        name, (S // TM, 4), dgu, pl.BlockSpec((None, TM, FF_SHARD), lambda i, s: (s // 2, i, s % 2)),
        wgu4, pl.BlockSpec((None, D, FF_SHARD), lambda i, s: (s, 0, 0)), (1, 1),
        [_sds((S, D), F32)], [pl.BlockSpec((TM, D), lambda i, s: (i, 0))], _store(F32),
        nk=4, acc_shape=(TM, D), semantics=("parallel", "arbitrary"), a2a_parts=a2a_parts)


def _ffn_dw_gu(name, u_in, dgu, a2a_parts=()):
    return _grad_w(name, u_in, D, dgu, FF_SHARD, lambda s, k: (s // 2, k, s % 2), 4,
                   (4, D, FF_SHARD), (None, D, FF_SHARD), lambda s, k: (s, 0, 0), a2a_parts=a2a_parts)


def _ffn_bwd(tag, df, u_in, gu, hm, wgu4, wd):
    dgu = _ffn_down_bwd(tag + "_down_bwd", df, wd, gu)[0]
    dwd = _ffn_dw_down(tag + "_dw_down", hm, df)
    du = _ffn_up_bwd(tag + "_up_bwd", dgu, wgu4)[0]
    dwgu = _ffn_dw_gu(tag + "_dw_gu", u_in, dgu)
    return du, dwgu, dwd


def _shift_down(v, k, row):
    return jnp.where(row >= k, pltpu.roll(v, k, axis=0), 0.0)


def _shift_up(v, k, row, S):
    return jnp.where(row < S - k, pltpu.roll(v, S - k, axis=0), 0.0)


def _conv_specs(S):
    cols = CONV_W // LANES
    return [pl.BlockSpec((S, LANES), functools.partial(lambda j, off: (0, off + j), off=o * cols))
            for o in range(3)]


def _conv_fwd(proj, conv_w):
    S = proj.shape[0]

    def body(cb_ref, cc_ref, cx_ref, w_ref, sc_ref):
        row = lax.broadcasted_iota(jnp.int32, (S, LANES), 0)
        v = cc_ref[...].astype(F32) * cx_ref[...].astype(F32)
        yv = w_ref[0:1, :] * _shift_down(v, 2, row) + w_ref[1:2, :] * _shift_down(v, 1, row) + w_ref[2:3, :] * v
        sc_ref[...] = (cb_ref[...].astype(F32) * yv).astype(BF16)

    return pl.pallas_call(
        body, name="conv_fwd", grid=(CONV_W // LANES,),
        in_specs=_conv_specs(S) + [pl.BlockSpec((3, LANES), lambda j: (0, j))],
        out_specs=pl.BlockSpec((S, LANES), lambda j: (0, j)), out_shape=_sds((S, CONV_W), BF16),
        compiler_params=_params(("parallel",)),
    )(proj, proj, proj, conv_w)


def _conv_bwd(dsc, proj, conv_w):
    S = proj.shape[0]

    def body(d_ref, cb_ref, cc_ref, cx_ref, w_ref, dcb_ref, dcc_ref, dcx_ref, dw_ref):
        row = lax.broadcasted_iota(jnp.int32, (S, LANES), 0)
        cc = cc_ref[...].astype(F32)
        cx = cx_ref[...].astype(F32)
        d = d_ref[...].astype(F32)
        v = cc * cx
        v1 = _shift_down(v, 1, row)
        v2 = _shift_down(v, 2, row)
        w0, w1, w2 = w_ref[0:1, :], w_ref[1:2, :], w_ref[2:3, :]
        dcb_ref[...] = (d * (w0 * v2 + w1 * v1 + w2 * v)).astype(BF16)
        dy = d * cb_ref[...].astype(F32)
        dw_ref[0:1, :] = jnp.sum(dy * v2, axis=0, keepdims=True)
        dw_ref[1:2, :] = jnp.sum(dy * v1, axis=0, keepdims=True)
        dw_ref[2:3, :] = jnp.sum(dy * v, axis=0, keepdims=True)
        dv = w2 * dy + w1 * _shift_up(dy, 1, row, S) + w0 * _shift_up(dy, 2, row, S)
        dcc_ref[...] = (dv * cx).astype(BF16)
        dcx_ref[...] = (dv * cc).astype(BF16)

    col = pl.BlockSpec((S, LANES), lambda j: (0, j))
    return pl.pallas_call(
        body, name="conv_bwd", grid=(CONV_W // LANES,),
        in_specs=[col] + _conv_specs(S) + [pl.BlockSpec((3, LANES), lambda j: (0, j))],
        out_specs=[col, col, col, pl.BlockSpec((3, LANES), lambda j: (0, j))],
        out_shape=[_sds((S, CONV_W), BF16)] * 3 + [_sds((3, CONV_W), F32)],
        compiler_params=_params(("parallel",)),
    )(dsc, proj, proj, proj, conv_w)


Q_COL, K_COL, V_COL = 1536 // LANES, 2048 // LANES, 2560 // LANES


def _split_dot(x, tri):
    hi = x.astype(BF16)
    lo = (x - hi.astype(F32)).astype(BF16)
    return jnp.dot(hi, tri, preferred_element_type=F32) + jnp.dot(lo, tri, preferred_element_type=F32)


def _tri_dot(tri, x):
    hi = x.astype(BF16)
    lo = (x - hi.astype(F32)).astype(BF16)
    return jnp.dot(tri, hi, preferred_element_type=F32) + jnp.dot(tri, lo, preferred_element_type=F32)


def _softplus(z):
    return jnp.maximum(z, 0.0) + jnp.log(1.0 + jnp.exp(-jnp.abs(z)))


def _nt(a, b):
    return lax.dot_general(a, b, (((1,), (1,)), ((), ())), preferred_element_type=F32)


def _tn(a, b):
    return lax.dot_general(a, b, (((0,), (0,)), ((), ())), preferred_element_type=F32)


def _interleave(gens, delays):
    results = [None] * len(gens)
    live = list(range(len(gens)))
    rnd = 0
    while live:
        for g in list(live):
            if rnd < delays[g]:
                continue
            try:
                next(gens[g])
            except StopIteration as stop:
                results[g] = stop.value
                live.remove(g)
        rnd += 1
    return results


def _attn_fwd(proj, shards):
    S = proj.shape[0]
    B = ATT_BLK
    nq = S // B
    n = len(shards)

    def body(q_ref, k_ref, v_ref, *rest):
        o_ref, t_ref = rest[n:n + 2]
        start, relay, finish = _gather_protocol(rest[:n], rest[n + 2:2 * n + 2], *rest[2 * n + 2:])
        p = pl.program_id(0)
        i = pl.program_id(1)
        pl.when((p == 0) & (i == 0))(start)
        pl.when((p == HEAD_PAIRS // 2) & (i == 0))(relay)
        lo_lane = lax.broadcasted_iota(jnp.int32, (B, LANES), 1) < HEAD_DIM
        row = lax.broadcasted_iota(jnp.int32, (B, B), 0)
        col = lax.broadcasted_iota(jnp.int32, (B, B), 1)
        after = (row > col).astype(BF16)
        causal = col < row
        q2 = q_ref[...] * 0.125
        zero = jnp.zeros((), BF16)
        q_heads = (jnp.where(lo_lane, q2, zero), jnp.where(lo_lane, zero, q2))

        def head_tile(q_h, st, kb, diag):
            k2 = k_ref[pl.ds(pl.multiple_of(kb * B, B), B), :]
            z = _nt(q_h, k2)
            yield
            spz = _softplus(z)
            sp = jnp.where(causal, spz, 0.0) if diag else spz
            hi = sp.astype(BF16)
            lo = (sp - hi.astype(F32)).astype(BF16)
            r = st["r"]
            st["r"] = r + jnp.sum(sp, axis=1, keepdims=True)
            yield
            rem = jnp.dot(hi, after, preferred_element_type=F32) + jnp.dot(lo, after, preferred_element_type=F32)
            yield
            a = jnp.exp(z - spz - (rem + r))
            if diag:
                a = jnp.where(causal, a, 0.0)
            ab = a.astype(BF16)
            yield
            v2 = v_ref[pl.ds(pl.multiple_of(kb * B, B), B), :]
            st["acc"] = st["acc"] + jnp.dot(ab, v2, preferred_element_type=F32)

        def tiles(kbs, carry, diags=(False, False)):
            sts = [dict(r=carry[0], acc=carry[1]), dict(r=carry[2], acc=carry[3])]
            gens = [head_tile(q_h, st, kb, dg) for kb, dg in zip(kbs, diags) for q_h, st in zip(q_heads, sts)]
            _interleave(gens, [t for t in range(len(kbs)) for _ in q_heads])
            return sts[0]["r"], sts[0]["acc"], sts[1]["r"], sts[1]["acc"]

        zr, za = jnp.zeros((B, 1), F32), jnp.zeros((B, LANES), F32)
        carry = lax.fori_loop(0, i % 2, lambda j, cr: tiles([i, i - 1], cr, (True, False)), (zr, za, zr, za))
        carry = lax.fori_loop(0, 1 - i % 2, lambda j, cr: tiles([i], cr, (True,)), carry)
        first = i - 1 - i % 2
        ra, acc_a, rb, acc_b = lax.fori_loop(
            0, i // 2, lambda j, cr: tiles([first - 2 * j, first - 2 * j - 1], cr), carry)
        o_ref[...] = jnp.where(lo_lane, acc_a, acc_b).astype(BF16)
        t_ref[...] = jnp.where(lo_lane, ra, rb).T
        pl.when((p == HEAD_PAIRS - 1) & (i == nq - 1))(finish)

    seq = lambda off: pl.BlockSpec((S, LANES), lambda p, i: (0, off + p))
    blk = pl.BlockSpec((B, LANES), lambda p, i: (i, p))
    o, t, *gathered = pl.pallas_call(
        body, name="attn_fwd", grid=(HEAD_PAIRS, nq),
        in_specs=[pl.BlockSpec((B, LANES), lambda p, i: (i, Q_COL + p)), seq(K_COL), seq(V_COL)] + _hbm_specs(n),
        out_specs=[blk, pl.BlockSpec((LANES, B), lambda p, i: (p, i))] + _hbm_specs(n),
        out_shape=[_sds((S, 512), BF16), _sds((512, S), F32)] + _gather_shapes(shards),
        scratch_shapes=_gather_sems(n),
        compiler_params=_params(("arbitrary", "arbitrary")),
    )(proj, proj, proj, *shards)
    return o, t, _with_own_shard(gathered, shards)


def _attn_bwd(proj, t, do, parts):
    S = proj.shape[0]
    kt = proj[:, K_COL * LANES:V_COL * LANES].T
    B = ATT_BLK
    nq = S // B
    n = len(parts)

    def body(q_ref, k_ref, v_ref, kt_ref, t_ref, do_ref, *rest):
        dq_ref, dk_ref, dv_ref = rest[n:n + 3]
        dk_acc, dv_acc = rest[2 * n + 3:2 * n + 5]
        start, finish = _all_to_all_protocol(rest[:n], rest[n + 3:2 * n + 3], *rest[2 * n + 5:])
        i = pl.program_id(1)
        pl.when((pl.program_id(0) == 0) & (i == 0))(start)

        @pl.when(i == 0)
        def _():
            dk_acc[...] = jnp.zeros_like(dk_acc)
            dv_acc[...] = jnp.zeros_like(dv_acc)

        lo_lane = lax.broadcasted_iota(jnp.int32, (B, LANES), 1) < HEAD_DIM
        key = lax.broadcasted_iota(jnp.int32, (B, B), 0)
        qry = lax.broadcasted_iota(jnp.int32, (B, B), 1)
        upto = (qry <= key).astype(BF16)
        before = (qry < key).astype(BF16)
        causal = key < qry
        zero = jnp.zeros((), BF16)
        q2 = q_ref[...] * 0.125
        do2 = do_ref[...]
        heads = ((jnp.where(lo_lane, q2, zero), jnp.where(lo_lane, do2, zero), t_ref[0:1, :]),
                 (jnp.where(lo_lane, zero, q2), jnp.where(lo_lane, zero, do2), t_ref[HEAD_DIM:HEAD_DIM + 1, :]))

        def head_tile(head, st, kb, diag):
            q_h, do_h, t_h = head
            rows = pl.ds(pl.multiple_of(kb * B, B), B)
            z = _nt(k_ref[rows, :], q_h)
            da = _nt(v_ref[rows, :], do_h)
            yield
            spz = _softplus(z)
            sp = jnp.where(causal, spz, 0.0) if diag else spz
            hi = sp.astype(BF16)
            lo = (sp - hi.astype(F32)).astype(BF16)
            pc = st["pc"]
            st["pc"] = pc + jnp.sum(sp, axis=0, keepdims=True)
            yield
            pref = jnp.dot(upto, hi, preferred_element_type=F32) + jnp.dot(upto, lo, preferred_element_type=F32)
            yield
            a = jnp.exp(z - spz - ((t_h - pc) - pref))
            if diag:
                a = jnp.where(causal, a, 0.0)
            e = a * da
            eb = e.astype(BF16)
            ab = a.astype(BF16)
            ec = st["ec"]
            st["ec"] = ec + jnp.sum(e, axis=0, keepdims=True)
            yield
            e_before = ec + jnp.dot(before, eb, preferred_element_type=F32)
            yield
            u = jnp.exp(-spz)
            dz = u * (e + e_before) - e_before
            if diag:
                dz = jnp.where(causal, dz, 0.0)
            dzb = dz.astype(BF16)
            yield
            st["dqt"] = st["dqt"] + jnp.dot(kt_ref[:, rows], dzb, preferred_element_type=F32)
            return (jnp.dot(dzb, q_h, preferred_element_type=F32), jnp.dot(ab, do_h, preferred_element_type=F32))

        def tiles(kbs, carry, diags=(False, False)):
            sts = [dict(pc=carry[3 * h], ec=carry[3 * h + 1], dqt=carry[3 * h + 2]) for h in range(2)]
            gens = [head_tile(hd, st, kb, dg) for kb, dg in zip(kbs, diags) for hd, st in zip(heads, sts)]
            res = _interleave(gens, [t for t in range(len(kbs)) for _ in heads])
            for t, kb in enumerate(kbs):
                rows = pl.ds(pl.multiple_of(kb * B, B), B)
                (dk_a, dv_a), (dk_b, dv_b) = res[2 * t], res[2 * t + 1]
                dk_acc[rows, :] += dk_a + dk_b
                dv_acc[rows, :] += dv_a + dv_b
            return tuple(st[nm] for st in sts for nm in ("pc", "ec", "dqt"))

        zc, zq = jnp.zeros((1, B), F32), jnp.zeros((LANES, B), F32)
        carry = lax.fori_loop(0, i // 2, lambda j, cr: tiles([2 * j, 2 * j + 1], cr), (zc, zc, zq, zc, zc, zq))
        carry = lax.fori_loop(0, i % 2, lambda j, cr: tiles([i - 1, i], cr, (False, True)), carry)
        _, _, dqt_a, _, _, dqt_b = lax.fori_loop(0, 1 - i % 2, lambda j, cr: tiles([i], cr, (True,)), carry)
        head0 = lax.broadcasted_iota(jnp.int32, (LANES, B), 0) < HEAD_DIM
        dq_ref[...] = (jnp.where(head0, dqt_a, dqt_b).T * 0.125).astype(BF16)

        @pl.when(i == nq - 1)
        def _():
            dk_ref[...] = dk_acc[...].astype(BF16)
            dv_ref[...] = dv_acc[...].astype(BF16)

        pl.when((pl.program_id(0) == HEAD_PAIRS - 1) & (i == nq - 1))(finish)

    seq = lambda off: pl.BlockSpec((S, LANES), lambda p, i: (0, off + p))
    blk = pl.BlockSpec((B, LANES), lambda p, i: (i, p))
    whole = pl.BlockSpec((S, LANES), lambda p, i: (0, p))
    dq, dk, dv, *came = pl.pallas_call(
        body, name="attn_bwd", grid=(HEAD_PAIRS, nq),
        in_specs=[pl.BlockSpec((B, LANES), lambda p, i: (i, Q_COL + p)), seq(K_COL), seq(V_COL),
                  pl.BlockSpec((LANES, S), lambda p, i: (p, 0)), pl.BlockSpec((LANES, B), lambda p, i: (p, i)), blk]
        + _hbm_specs(n),
        out_specs=[blk, whole, whole] + _hbm_specs(n),
        out_shape=[_sds((S, 512), BF16)] * 3 + [jax.ShapeDtypeStruct(p.shape, p.dtype) for p in parts],
        scratch_shapes=[pltpu.VMEM((S, LANES), F32), pltpu.VMEM((S, LANES), F32)] + _all_to_all_sems(n),
        compiler_params=_params(("arbitrary", "arbitrary")),
    )(proj, proj, proj, kt, t, do, *parts)
    return dq, dk, dv, came


GA_COL, GB_COL = 3072 // 256, 4096 // 256


def _merge_fwd(sc, o, wco4, wao4, proj, bm):
    S = sc.shape[0]

    def body(sc_ref, o_ref, wc_ref, wa_ref, ga_ref, gb_ref, bm_ref, ya_ref, yb_ref, mg_ref):
        ya = jnp.dot(sc_ref[...], wc_ref[...], preferred_element_type=F32)
        yb = jnp.dot(o_ref[...], wa_ref[...], preferred_element_type=F32)
        sa = _sigmoid(ga_ref[...].astype(F32) + bm_ref[0:1, :])
        sb = _sigmoid(gb_ref[...].astype(F32) + bm_ref[1:2, :])
        ya_ref[...] = ya.astype(BF16)
        yb_ref[...] = yb.astype(BF16)
        mg_ref[...] = (sa * ya + sb * yb).astype(BF16)

    wide = pl.BlockSpec((TM, 512), lambda n, i: (i, 0))
    wsp = pl.BlockSpec((None, 512, 256), lambda n, i: (n, 0, 0))
    out = pl.BlockSpec((TM, 256), lambda n, i: (i, n))
    return pl.pallas_call(
        body, name="merge_fwd", grid=(4, S // TM),
        in_specs=[wide, wide, wsp, wsp, pl.BlockSpec((TM, 256), lambda n, i: (i, GA_COL + n)),
                  pl.BlockSpec((TM, 256), lambda n, i: (i, GB_COL + n)), pl.BlockSpec((2, 256), lambda n, i: (0, n))],
        out_specs=[out, out, out], out_shape=[_sds((S, D), BF16)] * 3,
        compiler_params=_params(("parallel", "parallel")),
    )(sc, o, wco4, wao4, proj, proj, bm)


def _merge_bwd(dy2, wout, ya, yb, proj, bm):
    S = dy2.shape[0]

    def epilogue(acc, ex, outs):
        ya_ref, yb_ref, ga_ref, gb_ref, bm_ref = ex
        i = pl.program_id(1)
        sa = _sigmoid(ga_ref[...].astype(F32) + bm_ref[0:1, :])
        sb = _sigmoid(gb_ref[...].astype(F32) + bm_ref[1:2, :])
        dga = acc * ya_ref[...].astype(F32) * (sa * (1.0 - sa))
        dgb = acc * yb_ref[...].astype(F32) * (sb * (1.0 - sb))
        outs[0][...] = (acc * sa).astype(BF16)
        outs[1][...] = (acc * sb).astype(BF16)
        outs[2][...] = dga.astype(BF16)
        outs[3][...] = dgb.astype(BF16)

        @pl.when(i == 0)
        def _():
            outs[4][...] = jnp.zeros_like(outs[4])

        outs[4][0:1, :] += jnp.sum(dga, axis=0, keepdims=True)
        outs[4][1:2, :] += jnp.sum(dgb, axis=0, keepdims=True)

    out = pl.BlockSpec((TM, 256), lambda n, i: (i, n))
    return _mm(
        "merge_bwd", (4, S // TM), dy2, pl.BlockSpec((TM, D), lambda n, i: (i, 0)),
        wout, pl.BlockSpec((256, D), lambda n, i: (n, 0)), (1, 1),
        [_sds((S, D), BF16)] * 4 + [_sds((8, D), F32)], [out, out, out, out, pl.BlockSpec((8, 256), lambda n, i: (0, n))],
        epilogue, extras=(ya, yb, proj, proj, bm),
        extra_specs=(out, out, pl.BlockSpec((TM, 256), lambda n, i: (i, GA_COL + n)),
                     pl.BlockSpec((TM, 256), lambda n, i: (i, GB_COL + n)), pl.BlockSpec((2, 256), lambda n, i: (0, n))),
        semantics=("parallel", "arbitrary"))


def _back_through_cols(name, dy, w4, width):
    S = dy.shape[0]
    return _mm(
        name, (S // TM, 4), dy, pl.BlockSpec((TM, 256), lambda i, s: (i, s)),
        w4, pl.BlockSpec((None, width, 256), lambda i, s: (s, 0, 0)), (1, 1),
        [_sds((S, width), BF16)], [pl.BlockSpec((TM, width), lambda i, s: (i, 0))], _store(BF16),
        nk=4, acc_shape=(TM, width), semantics=("parallel", "arbitrary"))[0]


ADA_SHARD = 2304
ADA_TN = 768


def _ada_fwd(c_all, w_ada_l, b_l):
    def body(c_ref, w_ref, b_ref, o_ref):
        cv = c_ref[...]
        ca = cv * _sigmoid(cv)
        o_ref[...] = jnp.dot(ca.astype(BF16), w_ref[...].astype(BF16), preferred_element_type=F32) + b_ref[...]

    return pl.pallas_call(
        body, name="ada_fwd", grid=(ADA_SHARD // ADA_TN,),
        in_specs=[pl.BlockSpec((8, D), lambda j: (0, 0)), pl.BlockSpec((D, ADA_TN), lambda j: (0, j)),
                  pl.BlockSpec((1, ADA_TN), lambda j: (0, j))],
        out_specs=pl.BlockSpec((8, ADA_TN), lambda j: (0, j)), out_shape=_sds((8, ADA_SHARD), F32),
        compiler_params=_params(("parallel",)),
    )(c_all, w_ada_l, b_l)


def _ada_bwd(c_all_t, dmod_l):
    def body(c_ref, d_ref, o_ref):
        cv = c_ref[...]
        ca = cv * _sigmoid(cv)
        o_ref[...] = jnp.dot(ca.astype(BF16).astype(F32), d_ref[...].astype(BF16).astype(F32),
                             preferred_element_type=F32, precision=lax.Precision.HIGHEST)

    return pl.pallas_call(
        body, name="ada_bwd", grid=(ADA_SHARD // ADA_TN,),
        in_specs=[pl.BlockSpec((D, 8), lambda j: (0, 0)), pl.BlockSpec((8, ADA_TN), lambda j: (0, j))],
        out_specs=pl.BlockSpec((D, ADA_TN), lambda j: (0, j)), out_shape=_sds((D, ADA_SHARD), F32),
        compiler_params=_params(("parallel",)),
    )(c_all_t, dmod_l)


def _sum_rows(name, x):
    n = x.shape[1]

    def body(x_ref, o_ref):
        s = x_ref[0:1, :]
        for d in range(1, 8):
            s = s + x_ref[d:d + 1, :]
        o_ref[...] = s

    return pl.pallas_call(
        body, name=name, in_specs=[pl.BlockSpec(memory_space=pltpu.VMEM)],
        out_specs=pl.BlockSpec(memory_space=pltpu.VMEM), out_shape=_sds((1, n), F32),
        compiler_params=pltpu.CompilerParams(vmem_limit_bytes=VMEM_LIMIT),
    )(x)


def _pair_sum(name, g4, recv, c_idx):
    _, _, h, C = g4.shape
    tr = h if h <= 512 else h // (h // 256) if h % 256 == 0 else h // 2

    def body(c_ref, g_ref, r_ref, o_ref):
        o_ref[...] = (g_ref[...].astype(F32) + r_ref[...].astype(F32)).astype(BF16)

    grid_spec = pltpu.PrefetchScalarGridSpec(
        num_scalar_prefetch=1, grid=(4, h // tr),
        in_specs=[pl.BlockSpec((None, None, tr, C), lambda k, i, c: (k, c[0], i, 0)),
                  pl.BlockSpec((None, tr, C), lambda k, i, c: (k, i, 0))],
        out_specs=pl.BlockSpec((None, tr, C), lambda k, i, c: (k, i, 0)))
    return pl.pallas_call(
        body, name=name, grid_spec=grid_spec, out_shape=_sds((4, h, C), BF16),
        compiler_params=_params(("parallel", "parallel")),
    )(c_idx, g4, recv)


def _sum4(name, q, p, chip_idx):
    _, h, C = q.shape
    tr = h if h <= 512 else h // (h // 256) if h % 256 == 0 else h // 2

    def body(k_ref, q_ref, p_ref, o_ref):
        me = k_ref[0]
        terms = [jnp.where(me == k, p_ref[...], q_ref[k]).astype(F32) for k in range(4)]
        o_ref[...] = ((terms[0] + terms[1]) + terms[2]) + terms[3]

    grid_spec = pltpu.PrefetchScalarGridSpec(
        num_scalar_prefetch=1, grid=(h // tr,),
        in_specs=[pl.BlockSpec((4, tr, C), lambda i, k: (0, i, 0)),
                  pl.BlockSpec((None, tr, C), lambda i, k: (k[0], i, 0))],
        out_specs=pl.BlockSpec((tr, C), lambda i, k: (i, 0)))
    return pl.pallas_call(
        body, name=name, grid_spec=grid_spec, out_shape=_sds((h, C), F32), compiler_params=_params(("parallel",)),
    )(chip_idx, q, p)


def _adamw(name, w, g, m, v):
    R, C = w.shape
    tr = R
    while tr * C * 4 > (1 << 20) and tr % 16 == 0:
        tr //= 2
    c1 = 1.0 - ADAM_B1 ** ADAM_STEP
    c2 = 1.0 - ADAM_B2 ** ADAM_STEP

    def body(w_ref, g_ref, m_ref, v_ref, d_ref, nm_ref, nv_ref):
        gv = g_ref[...]
        nm = ADAM_B1 * m_ref[...] + (1.0 - ADAM_B1) * gv
        nv = ADAM_B2 * v_ref[...] + (1.0 - ADAM_B2) * (gv * gv)
        nm_ref[...] = nm
        nv_ref[...] = nv
        d_ref[...] = -ADAM_LR * ((nm * (1.0 / c1)) / (jnp.sqrt(nv * (1.0 / c2)) + ADAM_EPS) + ADAM_WD * w_ref[...])

    spec = pl.BlockSpec((tr, C), lambda i: (i, 0))
    return pl.pallas_call(
        body, name=name, grid=(R // tr,), in_specs=[spec] * 4, out_specs=[spec] * 3,
        out_shape=[_sds((R, C), F32)] * 3, compiler_params=_params(("parallel",)),
    )(w, g, m, v)


def _adamw_halves(name, w, own, sib, m, v, c_idx):
    R, C = w.shape
    h = R // 2
    tr = h
    while tr * C * 4 > (1 << 20) and tr % 16 == 0:
        tr //= 2
    nb = h // tr
    c1 = 1.0 - ADAM_B1 ** ADAM_STEP
    c2 = 1.0 - ADAM_B2 ** ADAM_STEP

    def body(c_ref, w_ref, own_ref, sib_ref, m_ref, v_ref, g_ref, d_ref, nm_ref, nv_ref):
        mine = (pl.program_id(0) // nb) == c_ref[0]
        gv = jnp.where(mine, own_ref[...], sib_ref[...])
        nm = ADAM_B1 * m_ref[...] + (1.0 - ADAM_B1) * gv
        nv = ADAM_B2 * v_ref[...] + (1.0 - ADAM_B2) * (gv * gv)
        g_ref[...] = gv
        nm_ref[...] = nm
        nv_ref[...] = nv
        d_ref[...] = -ADAM_LR * ((nm * (1.0 / c1)) / (jnp.sqrt(nv * (1.0 / c2)) + ADAM_EPS) + ADAM_WD * w_ref[...])

    spec = pl.BlockSpec((tr, C), lambda i, c: (i, 0))
    half = pl.BlockSpec((tr, C), lambda i, c: (i % nb, 0))
    grid_spec = pltpu.PrefetchScalarGridSpec(
        num_scalar_prefetch=1, grid=(R // tr,), in_specs=[spec, half, half, spec, spec], out_specs=[spec] * 4)
    return pl.pallas_call(
        body, name=name, grid_spec=grid_spec, out_shape=[_sds((R, C), F32)] * 4,
        compiler_params=_params(("parallel",)),
    )(c_idx, w, own, sib, m, v)


def _pack(g, scale, shift, gate):
    rows = jnp.stack([g, scale, shift, gate]).astype(F32)
    return jnp.concatenate([rows, jnp.zeros((4, D), F32)], axis=0)


def _fold8(vec):
    m = -(-vec.shape[0] // (8 * LANES)) * LANES
    return jnp.concatenate([vec, jnp.zeros((8 * m - vec.shape[0],), vec.dtype)]).reshape(8, m)


def _allgather_vectors(name, vec):
    return _allgather_rows(name, _fold8(vec)).reshape(8, -1)


def kernel(x, c, w_ada, b_ada, norm1_g, ffn1_w_gu, ffn1_w_down, norm2_g, w_mix_in, b_merge, conv_w, w_conv_out, w_attn_out, w_out, norm3_g, ffn2_w_gu, ffn2_w_down, final_g, loss_target, m_w_ada, m_b_ada, m_norm1_g, m_ffn1_w_gu, m_ffn1_w_down, m_norm2_g, m_w_mix_in, m_b_merge, m_conv_w, m_w_conv_out, m_w_attn_out, m_w_out, m_norm3_g, m_ffn2_w_gu, m_ffn2_w_down, m_final_g, v_w_ada, v_b_ada, v_norm1_g, v_ffn1_w_gu, v_ffn1_w_down, v_norm2_g, v_w_mix_in, v_b_merge, v_conv_w, v_w_conv_out, v_w_attn_out, v_w_out, v_norm3_g, v_ffn2_w_gu, v_ffn2_w_down, v_final_g):
    xi, yi, ci = lax.axis_index("x"), lax.axis_index("y"), lax.axis_index("c")
    chip = 2 * xi + yi
    dev = 4 * xi + 2 * yi + ci
    S = x.shape[1]
    h0 = x[0]
    target = loss_target[0]

    (wgu1,) = _allgather_weights([ffn1_w_gu[0].astype(BF16)])
    late_shards = [w[0].astype(BF16) for w in (w_conv_out, w_attn_out, w_out, ffn2_w_gu, ffn2_w_down)]
    c_idx = jnp.reshape(ci, (1,)).astype(jnp.int32)
    chip_idx = jnp.reshape(chip, (1,)).astype(jnp.int32)

    def reduce_pairs(tag, names, grads):
        g4 = [g.reshape(4, 2, g.shape[1] // 2, g.shape[2]) for g in grads]
        recv = _sibling_swap_halves("grad_sibling_swap_" + tag, g4)
        return [_pair_sum("pair_sum_" + nm, a, b, c_idx) for nm, a, b in zip(names, g4, recv)]

    small = jnp.concatenate([c[0], b_merge[0].reshape(-1), conv_w[0].reshape(-1)])
    gathered = _allgather_vectors("allgather_small", small)
    c_all = gathered[:, :D]
    per_chip = gathered[0::2]
    bm_full = jnp.concatenate([per_chip[k, D:D + 512].reshape(2, 256) for k in range(4)], axis=1)
    cw_full = jnp.concatenate([per_chip[k, D + 512:D + 896].reshape(3, 128) for k in range(4)], axis=1)
    b_l = lax.dynamic_slice_in_dim(b_ada, chip * ADA_SHARD, ADA_SHARD, axis=1)
    mod_l = _ada_fwd(c_all, w_ada[0], b_l)
    mod_g = _allgather_rows("allgather_mod", mod_l).reshape(8, 8, ADA_SHARD)
    mod_all = jnp.concatenate([mod_g[2 * k] for k in range(4)], axis=1)
    mod = lax.dynamic_slice_in_dim(mod_all, dev, 1, axis=0).reshape(3, 3, D)
    p1 = _pack(norm1_g[0], mod[0, 1], mod[0, 0], mod[0, 2])
    p2 = _pack(norm2_g[0], mod[1, 1], mod[1, 0], mod[1, 2])
    p3 = _pack(norm3_g[0], mod[2, 1], mod[2, 0], mod[2, 2])
    pf = _pack(final_g, final_g, final_g, final_g)

    u1 = _norm_mod_fwd("norm1_fwd", h0, p1)
    gu1, hm1, (wd1,) = _ffn_up("ffn1_up", u1, wgu1, [ffn1_w_down[0].astype(BF16)])
    f1, h1, u2, (wmix,) = _proj_residual("ffn1_down", hm1, wd1.reshape(D_FF, D), h0, p1, 0.5, p2,
                                         gather_shards=[w_mix_in[0].astype(BF16)])
    wd1 = wd1.reshape(D_FF, D)
    proj = _mm("mix_in", (4, S // TM), u2, pl.BlockSpec((TM, D), lambda s, i: (i, 0)),
               wmix, pl.BlockSpec((None, D, MIX_SHARD), lambda s, i: (s, 0, 0)), (1, 0),
               [_sds((S, MIX_W), BF16)], [pl.BlockSpec((TM, MIX_SHARD), lambda s, i: (i, s))], _store(BF16),
               semantics=("parallel", "parallel"))[0]
    sc = _conv_fwd(proj, cw_full)
    o, t_tot, (wco, wao, wout, wgu2, wd2) = _attn_fwd(proj, late_shards)
    wout = wout.reshape(D, D)
    wd2 = wd2.reshape(D_FF, D)
    ya, yb, merged = _merge_fwd(sc, o, wco, wao, proj, bm_full)
    y2, h2, u3, _ = _proj_residual("mix_out", merged, wout, h1, p2, 1.0, p3)
    gu3, hm3, _ = _ffn_up("ffn2_up", u3, wgu2)

    dh3, df3, sums_f, loss_blk = _proj_residual_loss("ffn2_down", hm3, wd2, h2, p3, 0.5, pf, target)
    du3, dwgu2, dwd2 = _ffn_bwd("ffn2", df3, u3, gu3, hm3, wgu2, wd2)
    dh2, dy2, sums3 = _norm_mod_bwd("norm3_bwd", du3, h2, p3, dh3, prev=(p2, y2, 1.0))

    dya, dyb, dga, dgb, sums_bm = _merge_bwd(dy2, wout, ya, yb, proj, bm_full)
    dwout = _mm("dw_out", (1, S // TK), merged, pl.BlockSpec((TK, D), lambda n, k: (k, 0)),
                dy2, pl.BlockSpec((TK, D), lambda n, k: (k, 0)), (0, 0),
                [_sds((D, D), BF16)], [pl.BlockSpec((D, D), lambda n, k: (0, 0))], _store(BF16),
                nk=S // TK, acc_shape=(D, D))[0]
    dsc = _back_through_cols("conv_out_bwd", dya, wco, 512)
    do = _back_through_cols("attn_out_bwd", dyb, wao, 512)
    dwco = _grad_w("dw_conv_out", sc, 512, dya, 256, lambda s, k: (k, s), 4, (4, 512, 256), (None, 512, 256),
                   lambda s, k: (s, 0, 0))
    dwao = _grad_w("dw_attn_out", o, 512, dyb, 256, lambda s, k: (k, s), 4, (4, 512, 256), (None, 512, 256),
                   lambda s, k: (s, 0, 0))
    dcb, dcc, dcx, dcw = _conv_bwd(dsc, proj, cw_full)
    names_e = ["ffn2_w_gu", "ffn2_w_down", "w_out", "w_conv_out", "w_attn_out"]
    part_e = reduce_pairs("early", names_e, [dwgu2, dwd2.reshape(4, 704, D), dwout.reshape(4, 256, D), dwco, dwao])
    dq, dk, dv, came_e = _attn_bwd(proj, t_tot, do, part_e)
    dproj = jnp.concatenate([dcb, dcc, dcx, dq, dk, dv, dga, dgb], axis=1)
    du2 = _mm("mix_in_bwd", (S // TM, 4), dproj, pl.BlockSpec((TM, MIX_SHARD), lambda i, s: (i, s)),
              wmix, pl.BlockSpec((None, D, MIX_SHARD), lambda i, s: (s, 0, 0)), (1, 1),
              [_sds((S, D), F32)], [pl.BlockSpec((TM, D), lambda i, s: (i, 0))], _store(F32),
              nk=4, acc_shape=(TM, D), semantics=("parallel", "arbitrary"))[0]
    dwmix = _grad_w("dw_mix_in", u2, D, dproj, MIX_SHARD, lambda s, k: (k, s), 4, (4, D, MIX_SHARD),
                    (None, D, MIX_SHARD), lambda s, k: (s, 0, 0))
    dh1, df1, sums2 = _norm_mod_bwd("norm2_bwd", du2, h1, p2, dh2, prev=(p1, f1, 0.5))

    part_mix = reduce_pairs("mix", ["w_mix_in"], [dwmix])
    dgu1, *came_mix = _ffn_down_bwd("ffn1_down_bwd", df1, wd1, gu1, a2a_parts=part_mix)
    dwd1 = _ffn_dw_down("ffn1_dw_down", hm1, df1)
    part_wd1 = reduce_pairs("wd1", ["ffn1_w_down"], [dwd1.reshape(4, 704, D)])
    dwgu1, *came_wd1 = _ffn_dw_gu("ffn1_dw_gu", u1, dgu1, a2a_parts=part_wd1)
    part_gu1 = reduce_pairs("gu1", ["ffn1_w_gu"], [dwgu1])
    du1, *came_gu1 = _ffn_up_bwd("ffn1_up_bwd", dgu1, wgu1, a2a_parts=part_gu1)
    grad_x, sums1 = _norm_mod_bwd("norm1_bwd", du1, h0, p1, dh1)

    dmod = jnp.stack([sums1[0], sums1[1], sums2[3], sums2[0], sums2[1], sums3[3], sums3[0], sums3[1], sums_f[1]])
    small_g = jnp.concatenate([dmod.reshape(-1), sums1[2], sums2[2], sums3[2], sums_f[0],
                               sums_bm[0], sums_bm[1], dcw.reshape(-1), loss_blk[0, 0:1]])
    all_g = _allgather_vectors("allgather_small_grads", small_g)
    tot = _sum_rows("sum_small_grads", all_g)[0]
    loss = tot[16 * D + 512]
    g_b_ada = tot[:9 * D][None, :]
    g_n1, g_n2, g_n3 = (tot[(9 + k) * D:(10 + k) * D][None, :] for k in range(3))
    g_fin = tot[12 * D:13 * D]
    g_bm = lax.dynamic_slice_in_dim(tot[13 * D:15 * D].reshape(2, D), chip * 256, 256, axis=1)[None]
    g_cw = lax.dynamic_slice_in_dim(tot[15 * D:16 * D + 512].reshape(3, 512), chip * 128, 128, axis=1)[None]
    dmod_l = lax.dynamic_slice_in_dim(all_g[:, :9 * D], chip * ADA_SHARD, ADA_SHARD, axis=1)
    g_w_ada = _ada_bwd(c_all.T, dmod_l)[None]

    names = names_e + ["w_mix_in", "ffn1_w_down", "ffn1_w_gu"]
    came = list(came_e) + came_mix + came_wd1 + came_gu1
    part = part_e + part_mix + part_wd1 + part_gu1
    half = [_sum4("chip_sum_" + nm, q, p, chip_idx) for nm, q, p in zip(names, came, part)]
    g_own = dict(zip(names, half))
    g_sib = dict(zip(names, _sibling_share(half)))

    weights = dict(w_ada=w_ada, b_ada=b_ada, norm1_g=norm1_g, ffn1_w_gu=ffn1_w_gu, ffn1_w_down=ffn1_w_down,
                   norm2_g=norm2_g, w_mix_in=w_mix_in, b_merge=b_merge, conv_w=conv_w, w_conv_out=w_conv_out,
                   w_attn_out=w_attn_out, w_out=w_out, norm3_g=norm3_g, ffn2_w_gu=ffn2_w_gu,
                   ffn2_w_down=ffn2_w_down, final_g=final_g)
    ms = dict(w_ada=m_w_ada, b_ada=m_b_ada, norm1_g=m_norm1_g, ffn1_w_gu=m_ffn1_w_gu, ffn1_w_down=m_ffn1_w_down,
              norm2_g=m_norm2_g, w_mix_in=m_w_mix_in, b_merge=m_b_merge, conv_w=m_conv_w, w_conv_out=m_w_conv_out,
              w_attn_out=m_w_attn_out, w_out=m_w_out, norm3_g=m_norm3_g, ffn2_w_gu=m_ffn2_w_gu,
              ffn2_w_down=m_ffn2_w_down, final_g=m_final_g)
    vs = dict(w_ada=v_w_ada, b_ada=v_b_ada, norm1_g=v_norm1_g, ffn1_w_gu=v_ffn1_w_gu, ffn1_w_down=v_ffn1_w_down,
              norm2_g=v_norm2_g, w_mix_in=v_w_mix_in, b_merge=v_b_merge, conv_w=v_conv_w, w_conv_out=v_w_conv_out,
              w_attn_out=v_w_attn_out, w_out=v_w_out, norm3_g=v_norm3_g, ffn2_w_gu=v_ffn2_w_gu,
              ffn2_w_down=v_ffn2_w_down, final_g=v_final_g)
    order = list(weights)
    grad = dict(w_ada=g_w_ada, b_ada=g_b_ada, norm1_g=g_n1, norm2_g=g_n2, norm3_g=g_n3, final_g=g_fin,
                b_merge=g_bm, conv_w=g_cw)
    delta, new_m, new_v = {}, {}, {}
    small_names = ["b_ada", "norm1_g", "norm2_g", "norm3_g", "final_g", "b_merge", "conv_w"]
    flat = lambda d: jnp.concatenate([d[nm].reshape(-1) for nm in small_names])[None, :]
    sd, sm, sv = _adamw("adamw_small", flat(weights), flat(grad), flat(ms), flat(vs))
    off = 0
    for nm in small_names:
        size = weights[nm].size
        for dst, src in ((delta, sd), (new_m, sm), (new_v, sv)):
            dst[nm] = src[0, off:off + size].reshape(weights[nm].shape)
        off += size
    for nm in order:
        if nm in small_names:
            continue
        shp = weights[nm].shape
        if nm in g_own:
            g2, d2, m2, v2 = _adamw_halves("adamw_" + nm, weights[nm][0], g_own[nm], g_sib[nm], ms[nm][0], vs[nm][0],
                                           c_idx)
            grad[nm] = g2.reshape(shp)
        else:
            d2, m2, v2 = _adamw("adamw_" + nm, weights[nm][0], grad[nm][0], ms[nm][0], vs[nm][0])
        delta[nm], new_m[nm], new_v[nm] = d2.reshape(shp), m2.reshape(shp), v2.reshape(shp)

    return (loss, grad_x[None], *[grad[nm] for nm in order], *[delta[nm] for nm in order],
            *[new_m[nm] for nm in order], *[new_v[nm] for nm in order])
```

```python
import functools

import jax
import jax.numpy as jnp
from jax import lax
from jax.experimental import pallas as pl
from jax.experimental.pallas import tpu as pltpu

F32 = jnp.float32
BF16 = jnp.bfloat16
MESH = pl.DeviceIdType.MESH

VMEM_LIMIT = 56 * 1024 * 1024
LANES = 128

D = 1024
D_FF = 2816
FF_SHARD = 1408
MIX_SHARD = 1280
MIX_W = 5120
HEAD_PAIRS = 4
HEAD_DIM = 64
CONV_W = 512
EPS = 1e-6
ATT_BLK = 256

ADAM_LR = 0.001
ADAM_B1 = 0.9
ADAM_B2 = 0.999
ADAM_EPS = 1e-08
ADAM_WD = 0.01
ADAM_STEP = 10


def _params(semantics=None):
    return pltpu.CompilerParams(dimension_semantics=semantics, vmem_limit_bytes=VMEM_LIMIT)


def _sigmoid(x):
    return 1.0 / (1.0 + jnp.exp(-x))


def _place():
    x, y, c = lax.axis_index("x"), lax.axis_index("y"), lax.axis_index("c")
    chips = [(1 - x, y), (x, 1 - y), (1 - x, 1 - y)]
    return x, y, c, chips


def _allgather_rows(name, blk):
    m_per, n = blk.shape

    def body(x_ref, out_ref, send_sems, recv_sems, local_sem):
        x, y, c, chips = _place()
        me, sibling = (x, y, c), (x, y, 1 - c)

        def rows(px, py, pc):
            return out_ref.at[pl.ds((4 * px + 2 * py + pc) * m_per, m_per), :]

        def copy(k, block, to, src=None):
            return pltpu.make_async_remote_copy(
                src_ref=rows(*block) if src is None else src, dst_ref=rows(*block),
                send_sem=send_sems.at[k], recv_sem=recv_sems.at[k], device_id=to, device_id_type=MESH)

        mine = pltpu.make_async_copy(x_ref, rows(*me), local_sem)
        mine.start()
        first = [copy(0, me, sibling, src=x_ref)]
        first += [copy(1 + j, me, (*chip, c), src=x_ref) for j, chip in enumerate(chips)]
        for cp in first:
            cp.start()
        passed = [copy(4 + j, (*chip, c), sibling) for j, chip in enumerate(chips)]
        for j, chip in enumerate(chips):
            copy(1 + j, (*chip, c), me).wait_recv()
            passed[j].start()
        copy(0, sibling, me).wait_recv()
        for j, chip in enumerate(chips):
            copy(4 + j, (*chip, 1 - c), me).wait_recv()
        for cp in first + passed:
            cp.wait_send()
        mine.wait()

    return pl.pallas_call(
        body, name=name,
        out_shape=jax.ShapeDtypeStruct((8 * m_per, n), blk.dtype),
        in_specs=[pl.BlockSpec(memory_space=pltpu.VMEM)],
        out_specs=pl.BlockSpec(memory_space=pltpu.VMEM),
        scratch_shapes=[pltpu.SemaphoreType.DMA((7,)), pltpu.SemaphoreType.DMA((7,)), pltpu.SemaphoreType.DMA],
        compiler_params=pltpu.CompilerParams(vmem_limit_bytes=VMEM_LIMIT),
    )(blk)


def _hbm_specs(n):
    return [pl.BlockSpec(memory_space=pltpu.HBM)] * n


def _allgather_weights(shards):
    n = len(shards)

    def body(*refs):
        start, relay, finish = _gather_protocol(refs[:n], refs[n:2 * n], *refs[2 * n:])
        start()
        relay()
        finish()

    gathered = pl.pallas_call(
        body, name="allgather_weights",
        out_shape=_gather_shapes(shards), in_specs=_hbm_specs(n), out_specs=_hbm_specs(n),
        scratch_shapes=_gather_sems(n),
    )(*shards)
    return _with_own_shard(gathered, shards)


def _gather_shapes(shards):
    return [jax.ShapeDtypeStruct((4, *s.shape), s.dtype) for s in shards]


def _gather_sems(n):
    return [pltpu.SemaphoreType.DMA((6 * n,)), pltpu.SemaphoreType.DMA((6 * n,))]


def _with_own_shard(gathered, shards):
    chip = 2 * lax.axis_index("x") + lax.axis_index("y")
    return [lax.dynamic_update_slice(g, s[None], (chip, 0, 0)) for g, s in zip(gathered, shards)]


def _gather_protocol(ins, outs, send_sems, recv_sems):
    n = len(ins)
    x, y, c, chips = _place()
    me, sibling = (x, y, c), (x, y, 1 - c)
    me_k = 2 * x + y

    def half(w, k, hc):
        h = ins[w].shape[0] // 2
        return outs[w].at[k, pl.ds(pl.multiple_of(hc * h, 8), h), :]

    def copy(w, j, k, hc, to, src=None):
        dst = half(w, k, hc)
        return pltpu.make_async_remote_copy(
            src_ref=dst if src is None else src, dst_ref=dst,
            send_sem=send_sems.at[6 * w + j], recv_sem=recv_sems.at[6 * w + j],
            device_id=to, device_id_type=MESH)

    def first(w, j):
        h = ins[w].shape[0] // 2
        src = ins[w].at[pl.ds(pl.multiple_of(c * h, 8), h), :]
        return copy(w, j, me_k, c, (*chips[j], c), src=src)

    def passed(w, j):
        px, py = chips[j]
        return copy(w, 3 + j, 2 * px + py, c, sibling)

    pairs = [(w, j) for w in range(n) for j in range(3)]

    def start():
        for w, j in pairs:
            first(w, j).start()

    def relay():
        for w, j in pairs:
            px, py = chips[j]
            copy(w, j, 2 * px + py, c, me).wait_recv()
            passed(w, j).start()

    def finish():
        for w, j in pairs:
            px, py = chips[j]
            copy(w, 3 + j, 2 * px + py, 1 - c, me).wait_recv()
        for w, j in pairs:
            first(w, j).wait_send()
            passed(w, j).wait_send()

    return start, relay, finish


def _sibling_swap_halves(name, grads):
    n = len(grads)

    def body(*refs):
        ins, outs = refs[:n], refs[n:2 * n]
        send_sems, recv_sems = refs[2 * n:]
        x, y, c, _ = _place()
        cps = []
        for w in range(n):
            cp = pltpu.make_async_remote_copy(
                src_ref=ins[w].at[:, 1 - c], dst_ref=outs[w],
                send_sem=send_sems.at[w], recv_sem=recv_sems.at[w],
                device_id=(x, y, 1 - c), device_id_type=MESH)
            cp.start()
            cps.append(cp)
        for cp in cps:
            cp.wait()

    return pl.pallas_call(
        body, name=name,
        out_shape=[jax.ShapeDtypeStruct((4, *g.shape[2:]), g.dtype) for g in grads],
        in_specs=_hbm_specs(n), out_specs=_hbm_specs(n),
        scratch_shapes=[pltpu.SemaphoreType.DMA((n,)), pltpu.SemaphoreType.DMA((n,))],
    )(*grads)


def _all_to_all_sems(n):
    return [pltpu.SemaphoreType.DMA((3 * n,)), pltpu.SemaphoreType.DMA((3 * n,))]


def _all_to_all_protocol(ins, outs, send_sems, recv_sems):
    n = len(ins)
    x, y, c, chips = _place()
    me_k = 2 * x + y
    pairs = [(w, j) for w in range(n) for j in range(3)]

    def sent(w, j):
        px, py = chips[j]
        return pltpu.make_async_remote_copy(
            src_ref=ins[w].at[2 * px + py], dst_ref=outs[w].at[me_k],
            send_sem=send_sems.at[3 * w + j], recv_sem=recv_sems.at[3 * w + j],
            device_id=(px, py, c), device_id_type=MESH)

    def start():
        for w, j in pairs:
            sent(w, j).start()

    def finish():
        for w, j in pairs:
            px, py = chips[j]
            slab = outs[w].at[2 * px + py]
            pltpu.make_async_remote_copy(
                src_ref=slab, dst_ref=slab, send_sem=send_sems.at[3 * w + j],
                recv_sem=recv_sems.at[3 * w + j], device_id=(px, py, c), device_id_type=MESH).wait_recv()
        for w, j in pairs:
            sent(w, j).wait_send()

    return start, finish


def _sibling_share(halves):
    n = len(halves)

    def body(*refs):
        ins, outs = refs[:n], refs[n:2 * n]
        send_sems, recv_sems = refs[2 * n:]
        x, y, c, _ = _place()
        cps = []
        for w in range(n):
            cp = pltpu.make_async_remote_copy(
                src_ref=ins[w], dst_ref=outs[w], send_sem=send_sems.at[w], recv_sem=recv_sems.at[w],
                device_id=(x, y, 1 - c), device_id_type=MESH)
            cp.start()
            cps.append(cp)
        for cp in cps:
            cp.wait()

    return pl.pallas_call(
        body, name="grad_sibling_share",
        out_shape=[jax.ShapeDtypeStruct(p.shape, p.dtype) for p in halves],
        in_specs=_hbm_specs(n), out_specs=_hbm_specs(n),
        scratch_shapes=[pltpu.SemaphoreType.DMA((n,)), pltpu.SemaphoreType.DMA((n,))],
    )(*halves)


def _mm(name, grid, a, a_spec, b, b_spec, contract, out_shapes, out_specs, epilogue,
        extras=(), extra_specs=(), nk=1, acc_shape=None, semantics=None, a2a_parts=(), gather_shards=(),
        relay_at=None):
    assert not (a2a_parts and gather_shards)
    moved = tuple(a2a_parts) + tuple(gather_shards)
    ne, no, nc = len(extras), len(out_shapes), len(moved)
    nd = len(grid)

    def body(*refs):
        a_ref, b_ref = refs[0], refs[1]
        ex, outs = refs[2:2 + ne], refs[2 + ne + nc:2 + ne + nc + no]
        if nc:
            ids = [pl.program_id(d) for d in range(nd)]
            comm_refs = (refs[2 + ne:2 + ne + nc], refs[2 + ne + nc + no:2 + ne + 2 * nc + no], *refs[-2:])
            at_start = functools.reduce(jnp.logical_and, [i == 0 for i in ids])
            if a2a_parts:
                start, finish = _all_to_all_protocol(*comm_refs)
                pl.when(at_start)(start)
            else:
                start, relay, finish = _gather_protocol(*comm_refs)
                pl.when(at_start)(start)
                at_relay = grid[0] // 2 if relay_at is None else relay_at
                pl.when(functools.reduce(jnp.logical_and, [ids[0] == at_relay] + [i == 0 for i in ids[1:]]))(relay)

        def prod():
            return lax.dot_general(a_ref[...], b_ref[...], (((contract[0],), (contract[1],)), ((), ())),
                                   preferred_element_type=F32)

        if nk == 1:
            epilogue(prod(), ex, outs)
        else:
            acc = refs[2 + ne + 2 * nc + no]
            k = pl.program_id(nd - 1)

            @pl.when(k == 0)
            def _():
                acc[...] = prod()

            @pl.when(k > 0)
            def _():
                acc[...] += prod()

            @pl.when(k == nk - 1)
            def _():
                epilogue(acc[...], ex, outs)

        if nc:
            pl.when(functools.reduce(jnp.logical_and, [i == g - 1 for i, g in zip(ids, grid)]))(finish)

    if semantics is None or nc:
        semantics = ("arbitrary",) * nd
    return pl.pallas_call(
        body, name=name, grid=grid,
        in_specs=[a_spec, b_spec, *extra_specs] + _hbm_specs(nc),
        out_specs=list(out_specs) + _hbm_specs(nc),
        out_shape=list(out_shapes) + [jax.ShapeDtypeStruct(p.shape, p.dtype) for p in a2a_parts]
        + _gather_shapes(gather_shards),
        scratch_shapes=([] if nk == 1 else [pltpu.VMEM(acc_shape, F32)])
        + (_all_to_all_sems(nc) if a2a_parts else _gather_sems(nc) if gather_shards else []),
        compiler_params=_params(semantics),
    )(a, b, *extras, *moved)


def _store(dtype):
    def epilogue(acc, ex, outs):
        outs[0][...] = acc.astype(dtype)
    return epilogue


def _sds(shape, dtype):
    return jax.ShapeDtypeStruct(shape, dtype)


TR = 512


def _row_spec(width, tr=TR):
    return pl.BlockSpec((tr, width), lambda i: (i, 0))


def _const_spec(shape):
    nd = len(shape)
    return pl.BlockSpec(shape, lambda i: (0,) * nd)


def _norm_mod_fwd(name, h, p):
    S = h.shape[0]

    def body(h_ref, p_ref, u_ref):
        hv = h_ref[...]
        r = lax.rsqrt(jnp.mean(hv * hv, axis=-1, keepdims=True) + EPS)
        nrm = (hv * r) * p_ref[0:1, :]
        u_ref[...] = (nrm * (1.0 + p_ref[1:2, :]) + p_ref[2:3, :]).astype(BF16)

    return pl.pallas_call(
        body, name=name, grid=(S // TR,),
        in_specs=[_row_spec(D), _const_spec((8, D))], out_specs=_row_spec(D),
        out_shape=_sds((S, D), BF16), compiler_params=_params(("parallel",)),
    )(h, p)


def _rmsnorm_parts(hv):
    r = lax.rsqrt(jnp.mean(hv * hv, axis=-1, keepdims=True) + EPS)
    return r, hv * r


def _norm_bwd_tail(h, p, dh_res, prev=None):
    S = h.shape[0]
    row = pl.BlockSpec((TM, D), lambda i, s: (i, 0))
    const = pl.BlockSpec((8, D), lambda i, s: (0, 0))
    extras, specs = [h, p, dh_res], [row, const, row]
    out_shapes, out_specs = [_sds((S, D), F32)], [row]
    if prev is not None:
        extras += [prev[0], prev[1]]
        specs += [const, row]
        out_shapes.append(_sds((S, D), BF16))
        out_specs.append(row)
    out_shapes.append(_sds((8, D), F32))
    out_specs.append(const)

    def epilogue(duv, ex, outs):
        h_ref, p_ref, r_ref = ex[:3]
        dh_ref, sums_ref = outs[0], outs[-1]

        @pl.when(pl.program_id(0) == 0)
        def _():
            sums_ref[...] = jnp.zeros_like(sums_ref)

        g = p_ref[0:1, :]
        r, xn = _rmsnorm_parts(h_ref[...])
        dn = duv * (1.0 + p_ref[1:2, :])
        dxn = dn * g
        dh = r_ref[...] + r * (dxn - xn * jnp.mean(dxn * xn, axis=-1, keepdims=True))
        dh_ref[...] = dh
        sums_ref[0:1, :] += jnp.sum(duv, axis=0, keepdims=True)
        sums_ref[1:2, :] += jnp.sum(duv * (xn * g), axis=0, keepdims=True)
        sums_ref[2:3, :] += jnp.sum(dn * xn, axis=0, keepdims=True)
        if prev is not None:
            pp_ref, f_ref = ex[3:5]
            outs[1][...] = (prev[2] * pp_ref[3:4, :] * dh).astype(BF16)
            sums_ref[3:4, :] += prev[2] * jnp.sum(dh * f_ref[...].astype(F32), axis=0, keepdims=True)

    return dict(extras=tuple(extras), extra_specs=tuple(specs), out_shapes=out_shapes, out_specs=out_specs,
                epilogue=epilogue)


TM = 512


def _ffn_up(name, u, wgu4, shards=()):
    S = u.shape[0]
    n = len(shards)
    ni = S // TM

    def body(u_ref, wg_ref, wu_ref, *rest):
        gu_ref, hm_ref = rest[n:n + 2]
        s, i = pl.program_id(0), pl.program_id(1)
        if n:
            start, relay, finish = _gather_protocol(rest[:n], rest[n + 2:2 * n + 2], *rest[2 * n + 2:])
            pl.when((s == 0) & (i == 0))(start)
            pl.when((s == 1) & (i == 0))(relay)
        uv = u_ref[...]
        g = jnp.dot(uv, wg_ref[...], preferred_element_type=F32)
        up = jnp.dot(uv, wu_ref[...], preferred_element_type=F32)
        gu_ref[0] = g.astype(BF16)
        gu_ref[1] = up.astype(BF16)
        hm_ref[...] = (g * _sigmoid(g) * up).astype(BF16)
        if n:
            pl.when((s == 1) & (i == ni - 1))(finish)

    gu, hm, *gathered = pl.pallas_call(
        body, name=name, grid=(2, ni),
        in_specs=[pl.BlockSpec((TM, D), lambda s, i: (i, 0)),
                  pl.BlockSpec((None, D, FF_SHARD), lambda s, i: (s, 0, 0)),
                  pl.BlockSpec((None, D, FF_SHARD), lambda s, i: (s + 2, 0, 0))] + _hbm_specs(n),
        out_specs=[pl.BlockSpec((2, TM, FF_SHARD), lambda s, i: (0, i, s)),
                   pl.BlockSpec((TM, FF_SHARD), lambda s, i: (i, s))] + _hbm_specs(n),
        out_shape=[_sds((2, S, D_FF), BF16), _sds((S, D_FF), BF16)] + _gather_shapes(shards),
        scratch_shapes=_gather_sems(n) if n else [],
        compiler_params=_params(("arbitrary", "arbitrary") if n else ("parallel", "parallel")),
    )(u, wgu4, wgu4, *shards)
    return gu, hm, _with_own_shard(gathered, shards)


def _proj_residual(name, a, w, h, p, weight, p_next, gather_shards=()):
    S, K = a.shape

    def epilogue(acc, ex, outs):
        h_ref, p_ref, pn_ref = ex
        outs[0][...] = acc.astype(BF16)
        hout = h_ref[...] + weight * p_ref[3:4, :] * acc
        outs[1][...] = hout
        _, xn = _rmsnorm_parts(hout)
        outs[2][...] = ((xn * pn_ref[0:1, :]) * (1.0 + pn_ref[1:2, :]) + pn_ref[2:3, :]).astype(BF16)

    row = _row_spec(D, TM)
    res = _mm(
        name, (S // TM,), a, pl.BlockSpec((TM, K), lambda i: (i, 0)), w, pl.BlockSpec((K, D), lambda i: (0, 0)),
        (1, 0), [_sds((S, D), BF16), _sds((S, D), F32), _sds((S, D), BF16)], [row, row, row], epilogue,
        extras=(h, p, p_next), extra_specs=(row, _const_spec((8, D)), _const_spec((8, D))),
        semantics=("parallel",), gather_shards=gather_shards, relay_at=S // TM - 1)
    return res[0], res[1], res[2], _with_own_shard(res[3:], gather_shards)


def _proj_residual_loss(name, a, w, h, p, weight, gf, target):
    S, K = a.shape

    def epilogue(acc, ex, outs):
        h_ref, p_ref, g_ref, t_ref = ex
        dh_ref, df_ref, sums_ref, loss_ref = outs

        @pl.when(pl.program_id(0) == 0)
        def _():
            sums_ref[...] = jnp.zeros_like(sums_ref)
            loss_ref[...] = jnp.zeros_like(loss_ref)

        gate = p_ref[3:4, :]
        g = g_ref[0:1, :]
        r, xn = _rmsnorm_parts(h_ref[...] + weight * gate * acc)
        err = xn * g - t_ref[...]
        loss_ref[...] += 0.5 * jnp.sum(err * err) * (1.0 / D)
        dout = err * (1.0 / D)
        dxn = dout * g
        dh = r * (dxn - xn * jnp.mean(dxn * xn, axis=-1, keepdims=True))
        dh_ref[...] = dh
        df_ref[...] = (weight * gate * dh).astype(BF16)
        sums_ref[0:1, :] += jnp.sum(dout * xn, axis=0, keepdims=True)
        sums_ref[1:2, :] += weight * jnp.sum(dh * acc, axis=0, keepdims=True)

    row = _row_spec(D, TM)
    return _mm(
        name, (S // TM,), a, pl.BlockSpec((TM, K), lambda i: (i, 0)), w, pl.BlockSpec((K, D), lambda i: (0, 0)),
        (1, 0), [_sds((S, D), F32), _sds((S, D), BF16), _sds((8, D), F32), _sds((8, LANES), F32)],
        [row, row, _const_spec((8, D)), _const_spec((8, LANES))], epilogue,
        extras=(h, p, gf, target), extra_specs=(row, _const_spec((8, D)), _const_spec((8, D)), row),
        semantics=("arbitrary",))


def _ffn_down_bwd(name, df, wd, gu, a2a_parts=()):
    S = df.shape[0]

    def epilogue(acc, ex, outs):
        g = ex[0][0].astype(F32)
        up = ex[0][1].astype(F32)
        sg = _sigmoid(g)
        outs[0][0] = (acc * up * (sg * (1.0 + g * (1.0 - sg)))).astype(BF16)
        outs[0][1] = (acc * g * sg).astype(BF16)

    gu_spec = pl.BlockSpec((2, TM, FF_SHARD), lambda n, i: (0, i, n))
    return _mm(
        name, (2, S // TM), df, pl.BlockSpec((TM, D), lambda n, i: (i, 0)),
        wd, pl.BlockSpec((FF_SHARD, D), lambda n, i: (n, 0)), (1, 1),
        [_sds((2, S, D_FF), BF16)], [gu_spec], epilogue, extras=(gu,), extra_specs=(gu_spec,),
        semantics=("parallel", "parallel"), a2a_parts=a2a_parts)


TK = 512


def _grad_w(name, a, a_w, b, b_w, b_map, n_out, out_shape, out_block, out_map, a2a_parts=()):
    S = a.shape[0]
    nk = S // TK
    res = _mm(
        name, (n_out, nk), a, pl.BlockSpec((TK, a_w), lambda s, k: (k, 0)), b, pl.BlockSpec(
            (None, TK, b_w) if b.ndim == 3 else (TK, b_w), b_map), (0, 0),
        [_sds(out_shape, BF16)], [pl.BlockSpec(out_block, out_map)], _store(BF16), nk=nk, acc_shape=(a_w, b_w),
        semantics=("parallel", "arbitrary"), a2a_parts=a2a_parts)
    return res if a2a_parts else res[0]


def _ffn_dw_down(name, hm, df):
    S = df.shape[0]
    return _mm(
        name, (2, S // TK), hm, pl.BlockSpec((TK, FF_SHARD), lambda m, k: (k, m)),
        df, pl.BlockSpec((TK, D), lambda m, k: (k, 0)), (0, 0),
        [_sds((D_FF, D), BF16)], [pl.BlockSpec((FF_SHARD, D), lambda m, k: (m, 0))], _store(BF16),
        nk=S // TK, acc_shape=(FF_SHARD, D), semantics=("parallel", "arbitrary"))[0]


def _ffn_up_bwd(name, dgu, wgu4, tail, a2a_parts=()):
    S = dgu.shape[1]
    return _mm(
        name, (S // TM, 4), dgu, pl.BlockSpec((None, TM, FF_SHARD), lambda i, s: (s // 2, i, s % 2)),
        wgu4, pl.BlockSpec((None, D, FF_SHARD), lambda i, s: (s, 0, 0)), (1, 1),
        tail["out_shapes"], tail["out_specs"], tail["epilogue"], extras=tail["extras"],
        extra_specs=tail["extra_specs"], nk=4, acc_shape=(TM, D), semantics=("arbitrary", "arbitrary"),
        a2a_parts=a2a_parts)


def _ffn_dw_gu(name, u_in, dgu, a2a_parts=()):
    return _grad_w(name, u_in, D, dgu, FF_SHARD, lambda s, k: (s // 2, k, s % 2), 4,
                   (4, D, FF_SHARD), (None, D, FF_SHARD), lambda s, k: (s, 0, 0), a2a_parts=a2a_parts)


def _ffn_bwd(tag, df, u_in, gu, hm, wgu4, wd, tail):
    dgu = _ffn_down_bwd(tag + "_down_bwd", df, wd, gu)[0]
    dwd = _ffn_dw_down(tag + "_dw_down", hm, df)
    res = _ffn_up_bwd(tag + "_up_bwd", dgu, wgu4, tail)
    dwgu = _ffn_dw_gu(tag + "_dw_gu", u_in, dgu)
    return res, dwgu, dwd


def _shift_down(v, k, row):
    return jnp.where(row >= k, pltpu.roll(v, k, axis=0), 0.0)


def _shift_up(v, k, row, S):
    return jnp.where(row < S - k, pltpu.roll(v, S - k, axis=0), 0.0)


def _conv_specs(S):
    cols = CONV_W // LANES
    return [pl.BlockSpec((S, LANES), functools.partial(lambda j, off: (0, off + j), off=o * cols))
            for o in range(3)]


def _conv_fwd(proj, conv_w):
    S = proj.shape[0]

    def body(cb_ref, cc_ref, cx_ref, w_ref, sc_ref):
        row = lax.broadcasted_iota(jnp.int32, (S, LANES), 0)
        v = cc_ref[...].astype(F32) * cx_ref[...].astype(F32)
        yv = w_ref[0:1, :] * _shift_down(v, 2, row) + w_ref[1:2, :] * _shift_down(v, 1, row) + w_ref[2:3, :] * v
        sc_ref[...] = (cb_ref[...].astype(F32) * yv).astype(BF16)

    return pl.pallas_call(
        body, name="conv_fwd", grid=(CONV_W // LANES,),
        in_specs=_conv_specs(S) + [pl.BlockSpec((3, LANES), lambda j: (0, j))],
        out_specs=pl.BlockSpec((S, LANES), lambda j: (0, j)), out_shape=_sds((S, CONV_W), BF16),
        compiler_params=_params(("parallel",)),
    )(proj, proj, proj, conv_w)


def _conv_bwd(dsc, proj, conv_w):
    S = proj.shape[0]

    def body(d_ref, cb_ref, cc_ref, cx_ref, w_ref, dcb_ref, dcc_ref, dcx_ref, dw_ref):
        row = lax.broadcasted_iota(jnp.int32, (S, LANES), 0)
        cc = cc_ref[...].astype(F32)
        cx = cx_ref[...].astype(F32)
        d = d_ref[...].astype(F32)
        v = cc * cx
        v1 = _shift_down(v, 1, row)
        v2 = _shift_down(v, 2, row)
        w0, w1, w2 = w_ref[0:1, :], w_ref[1:2, :], w_ref[2:3, :]
        dcb_ref[...] = (d * (w0 * v2 + w1 * v1 + w2 * v)).astype(BF16)
        dy = d * cb_ref[...].astype(F32)
        dw_ref[0:1, :] = jnp.sum(dy * v2, axis=0, keepdims=True)
        dw_ref[1:2, :] = jnp.sum(dy * v1, axis=0, keepdims=True)
        dw_ref[2:3, :] = jnp.sum(dy * v, axis=0, keepdims=True)
        dv = w2 * dy + w1 * _shift_up(dy, 1, row, S) + w0 * _shift_up(dy, 2, row, S)
        dcc_ref[...] = (dv * cx).astype(BF16)
        dcx_ref[...] = (dv * cc).astype(BF16)

    col = pl.BlockSpec((S, LANES), lambda j: (0, j))
    return pl.pallas_call(
        body, name="conv_bwd", grid=(CONV_W // LANES,),
        in_specs=[col] + _conv_specs(S) + [pl.BlockSpec((3, LANES), lambda j: (0, j))],
        out_specs=[col, col, col, pl.BlockSpec((3, LANES), lambda j: (0, j))],
        out_shape=[_sds((S, CONV_W), BF16)] * 3 + [_sds((3, CONV_W), F32)],
        compiler_params=_params(("parallel",)),
    )(dsc, proj, proj, proj, conv_w)


Q_COL, K_COL, V_COL = 1536 // LANES, 2048 // LANES, 2560 // LANES


def _split_dot(x, tri):
    hi = x.astype(BF16)
    lo = (x - hi.astype(F32)).astype(BF16)
    return jnp.dot(hi, tri, preferred_element_type=F32) + jnp.dot(lo, tri, preferred_element_type=F32)


def _tri_dot(tri, x):
    hi = x.astype(BF16)
    lo = (x - hi.astype(F32)).astype(BF16)
    return jnp.dot(tri, hi, preferred_element_type=F32) + jnp.dot(tri, lo, preferred_element_type=F32)


def _softplus(z):
    return jnp.maximum(z, 0.0) + jnp.log(1.0 + jnp.exp(-jnp.abs(z)))


def _nt(a, b):
    return lax.dot_general(a, b, (((1,), (1,)), ((), ())), preferred_element_type=F32)


def _tn(a, b):
    return lax.dot_general(a, b, (((0,), (0,)), ((), ())), preferred_element_type=F32)


def _interleave(gens, delays):
    results = [None] * len(gens)
    live = list(range(len(gens)))
    rnd = 0
    while live:
        for g in list(live):
            if rnd < delays[g]:
                continue
            try:
                next(gens[g])
            except StopIteration as stop:
                results[g] = stop.value
                live.remove(g)
        rnd += 1
    return results


def _attn_fwd(proj, shards):
    S = proj.shape[0]
    B = ATT_BLK
    nq = S // B
    n = len(shards)

    def body(q_ref, k_ref, v_ref, *rest):
        o_ref, t_ref = rest[n:n + 2]
        start, relay, finish = _gather_protocol(rest[:n], rest[n + 2:2 * n + 2], *rest[2 * n + 2:])
        p = pl.program_id(0)
        i = pl.program_id(1)
        pl.when((p == 0) & (i == 0))(start)
        pl.when((p == HEAD_PAIRS // 2) & (i == 0))(relay)
        lo_lane = lax.broadcasted_iota(jnp.int32, (B, LANES), 1) < HEAD_DIM
        row = lax.broadcasted_iota(jnp.int32, (B, B), 0)
        col = lax.broadcasted_iota(jnp.int32, (B, B), 1)
        after = (row > col).astype(BF16)
        causal = col < row
        q2 = q_ref[...] * 0.125
        zero = jnp.zeros((), BF16)
        q_heads = (jnp.where(lo_lane, q2, zero), jnp.where(lo_lane, zero, q2))

        def head_tile(q_h, st, kb, diag):
            k2 = k_ref[pl.ds(pl.multiple_of(kb * B, B), B), :]
            z = _nt(q_h, k2)
            yield
            spz = _softplus(z)
            sp = jnp.where(causal, spz, 0.0) if diag else spz
            hi = sp.astype(BF16)
            lo = (sp - hi.astype(F32)).astype(BF16)
            r = st["r"]
            st["r"] = r + jnp.sum(sp, axis=1, keepdims=True)
            yield
            rem = jnp.dot(hi, after, preferred_element_type=F32) + jnp.dot(lo, after, preferred_element_type=F32)
            yield
            a = jnp.exp(z - spz - (rem + r))
            if diag:
                a = jnp.where(causal, a, 0.0)
            ab = a.astype(BF16)
            yield
            v2 = v_ref[pl.ds(pl.multiple_of(kb * B, B), B), :]
            st["acc"] = st["acc"] + jnp.dot(ab, v2, preferred_element_type=F32)

        def tiles(kbs, carry, diags=(False, False)):
            sts = [dict(r=carry[0], acc=carry[1]), dict(r=carry[2], acc=carry[3])]
            gens = [head_tile(q_h, st, kb, dg) for kb, dg in zip(kbs, diags) for q_h, st in zip(q_heads, sts)]
            _interleave(gens, [t for t in range(len(kbs)) for _ in q_heads])
            return sts[0]["r"], sts[0]["acc"], sts[1]["r"], sts[1]["acc"]

        zr, za = jnp.zeros((B, 1), F32), jnp.zeros((B, LANES), F32)
        carry = lax.fori_loop(0, i % 2, lambda j, cr: tiles([i, i - 1], cr, (True, False)), (zr, za, zr, za))
        carry = lax.fori_loop(0, 1 - i % 2, lambda j, cr: tiles([i], cr, (True,)), carry)
        first = i - 1 - i % 2
        ra, acc_a, rb, acc_b = lax.fori_loop(
            0, i // 2, lambda j, cr: tiles([first - 2 * j, first - 2 * j - 1], cr), carry)
        o_ref[...] = jnp.where(lo_lane, acc_a, acc_b).astype(BF16)
        t_ref[...] = jnp.where(lo_lane, ra, rb).T
        pl.when((p == HEAD_PAIRS - 1) & (i == nq - 1))(finish)

    seq = lambda off: pl.BlockSpec((S, LANES), lambda p, i: (0, off + p))
    blk = pl.BlockSpec((B, LANES), lambda p, i: (i, p))
    o, t, *gathered = pl.pallas_call(
        body, name="attn_fwd", grid=(HEAD_PAIRS, nq),
        in_specs=[pl.BlockSpec((B, LANES), lambda p, i: (i, Q_COL + p)), seq(K_COL), seq(V_COL)] + _hbm_specs(n),
        out_specs=[blk, pl.BlockSpec((LANES, B), lambda p, i: (p, i))] + _hbm_specs(n),
        out_shape=[_sds((S, 512), BF16), _sds((512, S), F32)] + _gather_shapes(shards),
        scratch_shapes=_gather_sems(n),
        compiler_params=_params(("arbitrary", "arbitrary")),
    )(proj, proj, proj, *shards)
    return o, t, _with_own_shard(gathered, shards)


def _attn_bwd(proj, t, do, parts):
    S = proj.shape[0]
    kt = proj[:, K_COL * LANES:V_COL * LANES].T
    B = ATT_BLK
    nq = S // B
    n = len(parts)

    def body(q_ref, k_ref, v_ref, kt_ref, t_ref, do_ref, *rest):
        dq_ref, dk_ref, dv_ref = rest[n:n + 3]
        dk_acc, dv_acc = rest[2 * n + 3:2 * n + 5]
        start, finish = _all_to_all_protocol(rest[:n], rest[n + 3:2 * n + 3], *rest[2 * n + 5:])
        i = pl.program_id(1)
        pl.when((pl.program_id(0) == 0) & (i == 0))(start)

        @pl.when(i == 0)
        def _():
            dk_acc[...] = jnp.zeros_like(dk_acc)
            dv_acc[...] = jnp.zeros_like(dv_acc)

        lo_lane = lax.broadcasted_iota(jnp.int32, (B, LANES), 1) < HEAD_DIM
        key = lax.broadcasted_iota(jnp.int32, (B, B), 0)
        qry = lax.broadcasted_iota(jnp.int32, (B, B), 1)
        upto = (qry <= key).astype(BF16)
        before = (qry < key).astype(BF16)
        causal = key < qry
        zero = jnp.zeros((), BF16)
        q2 = q_ref[...] * 0.125
        do2 = do_ref[...]
        heads = ((jnp.where(lo_lane, q2, zero), jnp.where(lo_lane, do2, zero), t_ref[0:1, :]),
                 (jnp.where(lo_lane, zero, q2), jnp.where(lo_lane, zero, do2), t_ref[HEAD_DIM:HEAD_DIM + 1, :]))

        def head_tile(head, st, kb, diag):
            q_h, do_h, t_h = head
            rows = pl.ds(pl.multiple_of(kb * B, B), B)
            z = _nt(k_ref[rows, :], q_h)
            da = _nt(v_ref[rows, :], do_h)
            yield
            spz = _softplus(z)
            sp = jnp.where(causal, spz, 0.0) if diag else spz
            hi = sp.astype(BF16)
            lo = (sp - hi.astype(F32)).astype(BF16)
            pc = st["pc"]
            st["pc"] = pc + jnp.sum(sp, axis=0, keepdims=True)
            yield
            pref = jnp.dot(upto, hi, preferred_element_type=F32) + jnp.dot(upto, lo, preferred_element_type=F32)
            yield
            a = jnp.exp(z - spz - ((t_h - pc) - pref))
            if diag:
                a = jnp.where(causal, a, 0.0)
            e = a * da
            eb = e.astype(BF16)
            ab = a.astype(BF16)
            ec = st["ec"]
            st["ec"] = ec + jnp.sum(e, axis=0, keepdims=True)
            yield
            e_before = ec + jnp.dot(before, eb, preferred_element_type=F32)
            yield
            u = jnp.exp(-spz)
            dz = u * (e + e_before) - e_before
            if diag:
                dz = jnp.where(causal, dz, 0.0)
            dzb = dz.astype(BF16)
            yield
            st["dqt"] = st["dqt"] + jnp.dot(kt_ref[:, rows], dzb, preferred_element_type=F32)
            return (jnp.dot(dzb, q_h, preferred_element_type=F32), jnp.dot(ab, do_h, preferred_element_type=F32))

        def tiles(kbs, carry, diags=(False, False)):
            sts = [dict(pc=carry[3 * h], ec=carry[3 * h + 1], dqt=carry[3 * h + 2]) for h in range(2)]
            gens = [head_tile(hd, st, kb, dg) for kb, dg in zip(kbs, diags) for hd, st in zip(heads, sts)]
            res = _interleave(gens, [t for t in range(len(kbs)) for _ in heads])
            for t, kb in enumerate(kbs):
                rows = pl.ds(pl.multiple_of(kb * B, B), B)
                (dk_a, dv_a), (dk_b, dv_b) = res[2 * t], res[2 * t + 1]
                dk_acc[rows, :] += dk_a + dk_b
                dv_acc[rows, :] += dv_a + dv_b
            return tuple(st[nm] for st in sts for nm in ("pc", "ec", "dqt"))

        zc, zq = jnp.zeros((1, B), F32), jnp.zeros((LANES, B), F32)
        carry = lax.fori_loop(0, i // 2, lambda j, cr: tiles([2 * j, 2 * j + 1], cr), (zc, zc, zq, zc, zc, zq))
        carry = lax.fori_loop(0, i % 2, lambda j, cr: tiles([i - 1, i], cr, (False, True)), carry)
        _, _, dqt_a, _, _, dqt_b = lax.fori_loop(0, 1 - i % 2, lambda j, cr: tiles([i], cr, (True,)), carry)
        head0 = lax.broadcasted_iota(jnp.int32, (LANES, B), 0) < HEAD_DIM
        dq_ref[...] = (jnp.where(head0, dqt_a, dqt_b).T * 0.125).astype(BF16)

        @pl.when(i == nq - 1)
        def _():
            dk_ref[...] = dk_acc[...].astype(BF16)
            dv_ref[...] = dv_acc[...].astype(BF16)

        pl.when((pl.program_id(0) == HEAD_PAIRS - 1) & (i == nq - 1))(finish)

    seq = lambda off: pl.BlockSpec((S, LANES), lambda p, i: (0, off + p))
    blk = pl.BlockSpec((B, LANES), lambda p, i: (i, p))
    whole = pl.BlockSpec((S, LANES), lambda p, i: (0, p))
    dq, dk, dv, *came = pl.pallas_call(
        body, name="attn_bwd", grid=(HEAD_PAIRS, nq),
        in_specs=[pl.BlockSpec((B, LANES), lambda p, i: (i, Q_COL + p)), seq(K_COL), seq(V_COL),
                  pl.BlockSpec((LANES, S), lambda p, i: (p, 0)), pl.BlockSpec((LANES, B), lambda p, i: (p, i)), blk]
        + _hbm_specs(n),
        out_specs=[blk, whole, whole] + _hbm_specs(n),
        out_shape=[_sds((S, 512), BF16)] * 3 + [jax.ShapeDtypeStruct(p.shape, p.dtype) for p in parts],
        scratch_shapes=[pltpu.VMEM((S, LANES), F32), pltpu.VMEM((S, LANES), F32)] + _all_to_all_sems(n),
        compiler_params=_params(("arbitrary", "arbitrary")),
    )(proj, proj, proj, kt, t, do, *parts)
    return dq, dk, dv, came


GA_COL, GB_COL = 3072 // 256, 4096 // 256


def _merge_fwd(sc, o, wco4, wao4, proj, bm):
    S = sc.shape[0]

    def body(sc_ref, o_ref, wc_ref, wa_ref, ga_ref, gb_ref, bm_ref, ya_ref, yb_ref, mg_ref):
        ya = jnp.dot(sc_ref[...], wc_ref[...], preferred_element_type=F32)
        yb = jnp.dot(o_ref[...], wa_ref[...], preferred_element_type=F32)
        sa = _sigmoid(ga_ref[...].astype(F32) + bm_ref[0:1, :])
        sb = _sigmoid(gb_ref[...].astype(F32) + bm_ref[1:2, :])
        ya_ref[...] = ya.astype(BF16)
        yb_ref[...] = yb.astype(BF16)
        mg_ref[...] = (sa * ya + sb * yb).astype(BF16)

    wide = pl.BlockSpec((TM, 512), lambda n, i: (i, 0))
    wsp = pl.BlockSpec((None, 512, 256), lambda n, i: (n, 0, 0))
    out = pl.BlockSpec((TM, 256), lambda n, i: (i, n))
    return pl.pallas_call(
        body, name="merge_fwd", grid=(4, S // TM),
        in_specs=[wide, wide, wsp, wsp, pl.BlockSpec((TM, 256), lambda n, i: (i, GA_COL + n)),
                  pl.BlockSpec((TM, 256), lambda n, i: (i, GB_COL + n)), pl.BlockSpec((2, 256), lambda n, i: (0, n))],
        out_specs=[out, out, out], out_shape=[_sds((S, D), BF16)] * 3,
        compiler_params=_params(("parallel", "parallel")),
    )(sc, o, wco4, wao4, proj, proj, bm)


def _merge_bwd(dy2, wout, ya, yb, proj, bm):
    S = dy2.shape[0]

    def epilogue(acc, ex, outs):
        ya_ref, yb_ref, ga_ref, gb_ref, bm_ref = ex
        i = pl.program_id(1)
        sa = _sigmoid(ga_ref[...].astype(F32) + bm_ref[0:1, :])
        sb = _sigmoid(gb_ref[...].astype(F32) + bm_ref[1:2, :])
        dga = acc * ya_ref[...].astype(F32) * (sa * (1.0 - sa))
        dgb = acc * yb_ref[...].astype(F32) * (sb * (1.0 - sb))
        outs[0][...] = (acc * sa).astype(BF16)
        outs[1][...] = (acc * sb).astype(BF16)
        outs[2][...] = dga.astype(BF16)
        outs[3][...] = dgb.astype(BF16)

        @pl.when(i == 0)
        def _():
            outs[4][...] = jnp.zeros_like(outs[4])

        outs[4][0:1, :] += jnp.sum(dga, axis=0, keepdims=True)
        outs[4][1:2, :] += jnp.sum(dgb, axis=0, keepdims=True)

    out = pl.BlockSpec((TM, 256), lambda n, i: (i, n))
    return _mm(
        "merge_bwd", (4, S // TM), dy2, pl.BlockSpec((TM, D), lambda n, i: (i, 0)),
        wout, pl.BlockSpec((256, D), lambda n, i: (n, 0)), (1, 1),
        [_sds((S, D), BF16)] * 4 + [_sds((8, D), F32)], [out, out, out, out, pl.BlockSpec((8, 256), lambda n, i: (0, n))],
        epilogue, extras=(ya, yb, proj, proj, bm),
        extra_specs=(out, out, pl.BlockSpec((TM, 256), lambda n, i: (i, GA_COL + n)),
                     pl.BlockSpec((TM, 256), lambda n, i: (i, GB_COL + n)), pl.BlockSpec((2, 256), lambda n, i: (0, n))),
        semantics=("parallel", "arbitrary"))


def _back_through_cols(name, dy, w4, width):
    S = dy.shape[0]
    return _mm(
        name, (S // TM, 4), dy, pl.BlockSpec((TM, 256), lambda i, s: (i, s)),
        w4, pl.BlockSpec((None, width, 256), lambda i, s: (s, 0, 0)), (1, 1),
        [_sds((S, width), BF16)], [pl.BlockSpec((TM, width), lambda i, s: (i, 0))], _store(BF16),
        nk=4, acc_shape=(TM, width), semantics=("parallel", "arbitrary"))[0]


ADA_SHARD = 2304
ADA_TN = 768


def _ada_fwd(c_all, w_ada_l, b_l):
    def body(c_ref, w_ref, b_ref, o_ref):
        cv = c_ref[...]
        ca = cv * _sigmoid(cv)
        o_ref[...] = jnp.dot(ca.astype(BF16), w_ref[...].astype(BF16), preferred_element_type=F32) + b_ref[...]

    return pl.pallas_call(
        body, name="ada_fwd", grid=(ADA_SHARD // ADA_TN,),
        in_specs=[pl.BlockSpec((8, D), lambda j: (0, 0)), pl.BlockSpec((D, ADA_TN), lambda j: (0, j)),
                  pl.BlockSpec((1, ADA_TN), lambda j: (0, j))],
        out_specs=pl.BlockSpec((8, ADA_TN), lambda j: (0, j)), out_shape=_sds((8, ADA_SHARD), F32),
        compiler_params=_params(("parallel",)),
    )(c_all, w_ada_l, b_l)


def _ada_bwd(c_all_t, dmod_l):
    def body(c_ref, d_ref, o_ref):
        cv = c_ref[...]
        ca = cv * _sigmoid(cv)
        o_ref[...] = jnp.dot(ca.astype(BF16).astype(F32), d_ref[...].astype(BF16).astype(F32),
                             preferred_element_type=F32, precision=lax.Precision.HIGHEST)

    return pl.pallas_call(
        body, name="ada_bwd", grid=(ADA_SHARD // ADA_TN,),
        in_specs=[pl.BlockSpec((D, 8), lambda j: (0, 0)), pl.BlockSpec((8, ADA_TN), lambda j: (0, j))],
        out_specs=pl.BlockSpec((D, ADA_TN), lambda j: (0, j)), out_shape=_sds((D, ADA_SHARD), F32),
        compiler_params=_params(("parallel",)),
    )(c_all_t, dmod_l)


def _sum_rows(name, x):
    n = x.shape[1]

    def body(x_ref, o_ref):
        s = x_ref[0:1, :]
        for d in range(1, 8):
            s = s + x_ref[d:d + 1, :]
        o_ref[...] = s

    return pl.pallas_call(
        body, name=name, in_specs=[pl.BlockSpec(memory_space=pltpu.VMEM)],
        out_specs=pl.BlockSpec(memory_space=pltpu.VMEM), out_shape=_sds((1, n), F32),
        compiler_params=pltpu.CompilerParams(vmem_limit_bytes=VMEM_LIMIT),
    )(x)


def _pair_sum(name, g4, recv, c_idx):
    _, _, h, C = g4.shape
    tr = h if h <= 512 else h // (h // 256) if h % 256 == 0 else h // 2

    def body(c_ref, g_ref, r_ref, o_ref):
        o_ref[...] = (g_ref[...].astype(F32) + r_ref[...].astype(F32)).astype(BF16)

    grid_spec = pltpu.PrefetchScalarGridSpec(
        num_scalar_prefetch=1, grid=(4, h // tr),
        in_specs=[pl.BlockSpec((None, None, tr, C), lambda k, i, c: (k, c[0], i, 0)),
                  pl.BlockSpec((None, tr, C), lambda k, i, c: (k, i, 0))],
        out_specs=pl.BlockSpec((None, tr, C), lambda k, i, c: (k, i, 0)))
    return pl.pallas_call(
        body, name=name, grid_spec=grid_spec, out_shape=_sds((4, h, C), BF16),
        compiler_params=_params(("parallel", "parallel")),
    )(c_idx, g4, recv)


def _sum4(name, q, p, chip_idx):
    _, h, C = q.shape
    tr = h if h <= 512 else h // (h // 256) if h % 256 == 0 else h // 2

    def body(k_ref, q_ref, p_ref, o_ref):
        me = k_ref[0]
        terms = [jnp.where(me == k, p_ref[...], q_ref[k]).astype(F32) for k in range(4)]
        o_ref[...] = ((terms[0] + terms[1]) + terms[2]) + terms[3]

    grid_spec = pltpu.PrefetchScalarGridSpec(
        num_scalar_prefetch=1, grid=(h // tr,),
        in_specs=[pl.BlockSpec((4, tr, C), lambda i, k: (0, i, 0)),
                  pl.BlockSpec((None, tr, C), lambda i, k: (k[0], i, 0))],
        out_specs=pl.BlockSpec((tr, C), lambda i, k: (i, 0)))
    return pl.pallas_call(
        body, name=name, grid_spec=grid_spec, out_shape=_sds((h, C), F32), compiler_params=_params(("parallel",)),
    )(chip_idx, q, p)


def _adamw(name, w, g, m, v):
    R, C = w.shape
    tr = R
    while tr * C * 4 > (1 << 20) and tr % 16 == 0:
        tr //= 2
    c1 = 1.0 - ADAM_B1 ** ADAM_STEP
    c2 = 1.0 - ADAM_B2 ** ADAM_STEP

    def body(w_ref, g_ref, m_ref, v_ref, d_ref, nm_ref, nv_ref):
        gv = g_ref[...]
        nm = ADAM_B1 * m_ref[...] + (1.0 - ADAM_B1) * gv
        nv = ADAM_B2 * v_ref[...] + (1.0 - ADAM_B2) * (gv * gv)
        nm_ref[...] = nm
        nv_ref[...] = nv
        d_ref[...] = -ADAM_LR * ((nm * (1.0 / c1)) / (jnp.sqrt(nv * (1.0 / c2)) + ADAM_EPS) + ADAM_WD * w_ref[...])

    spec = pl.BlockSpec((tr, C), lambda i: (i, 0))
    return pl.pallas_call(
        body, name=name, grid=(R // tr,), in_specs=[spec] * 4, out_specs=[spec] * 3,
        out_shape=[_sds((R, C), F32)] * 3, compiler_params=_params(("parallel",)),
    )(w, g, m, v)


def _adamw_halves(name, w, own, sib, m, v, c_idx):
    R, C = w.shape
    h = R // 2
    tr = h
    while tr * C * 4 > (1 << 20) and tr % 16 == 0:
        tr //= 2
    nb = h // tr
    c1 = 1.0 - ADAM_B1 ** ADAM_STEP
    c2 = 1.0 - ADAM_B2 ** ADAM_STEP

    def body(c_ref, w_ref, own_ref, sib_ref, m_ref, v_ref, g_ref, d_ref, nm_ref, nv_ref):
        mine = (pl.program_id(0) // nb) == c_ref[0]
        gv = jnp.where(mine, own_ref[...], sib_ref[...])
        nm = ADAM_B1 * m_ref[...] + (1.0 - ADAM_B1) * gv
        nv = ADAM_B2 * v_ref[...] + (1.0 - ADAM_B2) * (gv * gv)
        g_ref[...] = gv
        nm_ref[...] = nm
        nv_ref[...] = nv
        d_ref[...] = -ADAM_LR * ((nm * (1.0 / c1)) / (jnp.sqrt(nv * (1.0 / c2)) + ADAM_EPS) + ADAM_WD * w_ref[...])

    spec = pl.BlockSpec((tr, C), lambda i, c: (i, 0))
    half = pl.BlockSpec((tr, C), lambda i, c: (i % nb, 0))
    grid_spec = pltpu.PrefetchScalarGridSpec(
        num_scalar_prefetch=1, grid=(R // tr,), in_specs=[spec, half, half, spec, spec], out_specs=[spec] * 4)
    return pl.pallas_call(
        body, name=name, grid_spec=grid_spec, out_shape=[_sds((R, C), F32)] * 4,
        compiler_params=_params(("parallel",)),
    )(c_idx, w, own, sib, m, v)


def _pack(g, scale, shift, gate):
    rows = jnp.stack([g, scale, shift, gate]).astype(F32)
    return jnp.concatenate([rows, jnp.zeros((4, D), F32)], axis=0)


def _fold8(vec):
    m = -(-vec.shape[0] // (8 * LANES)) * LANES
    return jnp.concatenate([vec, jnp.zeros((8 * m - vec.shape[0],), vec.dtype)]).reshape(8, m)


def _allgather_vectors(name, vec):
    return _allgather_rows(name, _fold8(vec)).reshape(8, -1)


def kernel(x, c, w_ada, b_ada, norm1_g, ffn1_w_gu, ffn1_w_down, norm2_g, w_mix_in, b_merge, conv_w, w_conv_out, w_attn_out, w_out, norm3_g, ffn2_w_gu, ffn2_w_down, final_g, loss_target, m_w_ada, m_b_ada, m_norm1_g, m_ffn1_w_gu, m_ffn1_w_down, m_norm2_g, m_w_mix_in, m_b_merge, m_conv_w, m_w_conv_out, m_w_attn_out, m_w_out, m_norm3_g, m_ffn2_w_gu, m_ffn2_w_down, m_final_g, v_w_ada, v_b_ada, v_norm1_g, v_ffn1_w_gu, v_ffn1_w_down, v_norm2_g, v_w_mix_in, v_b_merge, v_conv_w, v_w_conv_out, v_w_attn_out, v_w_out, v_norm3_g, v_ffn2_w_gu, v_ffn2_w_down, v_final_g):
    xi, yi, ci = lax.axis_index("x"), lax.axis_index("y"), lax.axis_index("c")
    chip = 2 * xi + yi
    dev = 4 * xi + 2 * yi + ci
    S = x.shape[1]
    h0 = x[0]
    target = loss_target[0]

    (wgu1,) = _allgather_weights([ffn1_w_gu[0].astype(BF16)])
    late_shards = [w[0].astype(BF16) for w in (w_conv_out, w_attn_out, w_out, ffn2_w_gu, ffn2_w_down)]
    c_idx = jnp.reshape(ci, (1,)).astype(jnp.int32)
    chip_idx = jnp.reshape(chip, (1,)).astype(jnp.int32)

    def reduce_pairs(tag, names, grads):
        g4 = [g.reshape(4, 2, g.shape[1] // 2, g.shape[2]) for g in grads]
        recv = _sibling_swap_halves("grad_sibling_swap_" + tag, g4)
        return [_pair_sum("pair_sum_" + nm, a, b, c_idx) for nm, a, b in zip(names, g4, recv)]

    small = jnp.concatenate([c[0], b_merge[0].reshape(-1), conv_w[0].reshape(-1)])
    gathered = _allgather_vectors("allgather_small", small)
    c_all = gathered[:, :D]
    per_chip = gathered[0::2]
    bm_full = jnp.concatenate([per_chip[k, D:D + 512].reshape(2, 256) for k in range(4)], axis=1)
    cw_full = jnp.concatenate([per_chip[k, D + 512:D + 896].reshape(3, 128) for k in range(4)], axis=1)
    b_l = lax.dynamic_slice_in_dim(b_ada, chip * ADA_SHARD, ADA_SHARD, axis=1)
    mod_l = _ada_fwd(c_all, w_ada[0], b_l)
    mod_g = _allgather_rows("allgather_mod", mod_l).reshape(8, 8, ADA_SHARD)
    mod_all = jnp.concatenate([mod_g[2 * k] for k in range(4)], axis=1)
    mod = lax.dynamic_slice_in_dim(mod_all, dev, 1, axis=0).reshape(3, 3, D)
    p1 = _pack(norm1_g[0], mod[0, 1], mod[0, 0], mod[0, 2])
    p2 = _pack(norm2_g[0], mod[1, 1], mod[1, 0], mod[1, 2])
    p3 = _pack(norm3_g[0], mod[2, 1], mod[2, 0], mod[2, 2])
    pf = _pack(final_g, final_g, final_g, final_g)

    u1 = _norm_mod_fwd("norm1_fwd", h0, p1)
    gu1, hm1, (wd1,) = _ffn_up("ffn1_up", u1, wgu1, [ffn1_w_down[0].astype(BF16)])
    f1, h1, u2, (wmix,) = _proj_residual("ffn1_down", hm1, wd1.reshape(D_FF, D), h0, p1, 0.5, p2,
                                         gather_shards=[w_mix_in[0].astype(BF16)])
    wd1 = wd1.reshape(D_FF, D)
    proj = _mm("mix_in", (4, S // TM), u2, pl.BlockSpec((TM, D), lambda s, i: (i, 0)),
               wmix, pl.BlockSpec((None, D, MIX_SHARD), lambda s, i: (s, 0, 0)), (1, 0),
               [_sds((S, MIX_W), BF16)], [pl.BlockSpec((TM, MIX_SHARD), lambda s, i: (i, s))], _store(BF16),
               semantics=("parallel", "parallel"))[0]
    sc = _conv_fwd(proj, cw_full)
    o, t_tot, (wco, wao, wout, wgu2, wd2) = _attn_fwd(proj, late_shards)
    wout = wout.reshape(D, D)
    wd2 = wd2.reshape(D_FF, D)
    ya, yb, merged = _merge_fwd(sc, o, wco, wao, proj, bm_full)
    y2, h2, u3, _ = _proj_residual("mix_out", merged, wout, h1, p2, 1.0, p3)
    gu3, hm3, _ = _ffn_up("ffn2_up", u3, wgu2)

    dh3, df3, sums_f, loss_blk = _proj_residual_loss("ffn2_down", hm3, wd2, h2, p3, 0.5, pf, target)
    (dh2, dy2, sums3), dwgu2, dwd2 = _ffn_bwd("ffn2", df3, u3, gu3, hm3, wgu2, wd2,
                                              _norm_bwd_tail(h2, p3, dh3, prev=(p2, y2, 1.0)))

    dya, dyb, dga, dgb, sums_bm = _merge_bwd(dy2, wout, ya, yb, proj, bm_full)
    dwout = _mm("dw_out", (1, S // TK), merged, pl.BlockSpec((TK, D), lambda n, k: (k, 0)),
                dy2, pl.BlockSpec((TK, D), lambda n, k: (k, 0)), (0, 0),
                [_sds((D, D), BF16)], [pl.BlockSpec((D, D), lambda n, k: (0, 0))], _store(BF16),
                nk=S // TK, acc_shape=(D, D))[0]
    dsc = _back_through_cols("conv_out_bwd", dya, wco, 512)
    do = _back_through_cols("attn_out_bwd", dyb, wao, 512)
    dwco = _grad_w("dw_conv_out", sc, 512, dya, 256, lambda s, k: (k, s), 4, (4, 512, 256), (None, 512, 256),
                   lambda s, k: (s, 0, 0))
    dwao = _grad_w("dw_attn_out", o, 512, dyb, 256, lambda s, k: (k, s), 4, (4, 512, 256), (None, 512, 256),
                   lambda s, k: (s, 0, 0))
    dcb, dcc, dcx, dcw = _conv_bwd(dsc, proj, cw_full)
    names_e = ["ffn2_w_gu", "ffn2_w_down", "w_out", "w_conv_out", "w_attn_out"]
    part_e = reduce_pairs("early", names_e, [dwgu2, dwd2.reshape(4, 704, D), dwout.reshape(4, 256, D), dwco, dwao])
    dq, dk, dv, came_e = _attn_bwd(proj, t_tot, do, part_e)
    dproj = jnp.concatenate([dcb, dcc, dcx, dq, dk, dv, dga, dgb], axis=1)
    tail2 = _norm_bwd_tail(h1, p2, dh2, prev=(p1, f1, 0.5))
    dh1, df1, sums2 = _mm("mix_in_bwd", (S // TM, 4), dproj, pl.BlockSpec((TM, MIX_SHARD), lambda i, s: (i, s)),
                          wmix, pl.BlockSpec((None, D, MIX_SHARD), lambda i, s: (s, 0, 0)), (1, 1),
                          tail2["out_shapes"], tail2["out_specs"], tail2["epilogue"], extras=tail2["extras"],
                          extra_specs=tail2["extra_specs"], nk=4, acc_shape=(TM, D),
                          semantics=("arbitrary", "arbitrary"))
    dwmix = _grad_w("dw_mix_in", u2, D, dproj, MIX_SHARD, lambda s, k: (k, s), 4, (4, D, MIX_SHARD),
                    (None, D, MIX_SHARD), lambda s, k: (s, 0, 0))

    part_mix = reduce_pairs("mix", ["w_mix_in"], [dwmix])
    dgu1, *came_mix = _ffn_down_bwd("ffn1_down_bwd", df1, wd1, gu1, a2a_parts=part_mix)
    dwd1 = _ffn_dw_down("ffn1_dw_down", hm1, df1)
    part_wd1 = reduce_pairs("wd1", ["ffn1_w_down"], [dwd1.reshape(4, 704, D)])
    dwgu1, *came_wd1 = _ffn_dw_gu("ffn1_dw_gu", u1, dgu1, a2a_parts=part_wd1)
    part_gu1 = reduce_pairs("gu1", ["ffn1_w_gu"], [dwgu1])
    grad_x, sums1, *came_gu1 = _ffn_up_bwd("ffn1_up_bwd", dgu1, wgu1, _norm_bwd_tail(h0, p1, dh1),
                                           a2a_parts=part_gu1)

    dmod = jnp.stack([sums1[0], sums1[1], sums2[3], sums2[0], sums2[1], sums3[3], sums3[0], sums3[1], sums_f[1]])
    small_g = jnp.concatenate([dmod.reshape(-1), sums1[2], sums2[2], sums3[2], sums_f[0],
                               sums_bm[0], sums_bm[1], dcw.reshape(-1), loss_blk[0, 0:1]])
    all_g = _allgather_vectors("allgather_small_grads", small_g)
    tot = _sum_rows("sum_small_grads", all_g)[0]
    loss = tot[16 * D + 512]
    g_b_ada = tot[:9 * D][None, :]
    g_n1, g_n2, g_n3 = (tot[(9 + k) * D:(10 + k) * D][None, :] for k in range(3))
    g_fin = tot[12 * D:13 * D]
    g_bm = lax.dynamic_slice_in_dim(tot[13 * D:15 * D].reshape(2, D), chip * 256, 256, axis=1)[None]
    g_cw = lax.dynamic_slice_in_dim(tot[15 * D:16 * D + 512].reshape(3, 512), chip * 128, 128, axis=1)[None]
    dmod_l = lax.dynamic_slice_in_dim(all_g[:, :9 * D], chip * ADA_SHARD, ADA_SHARD, axis=1)
    g_w_ada = _ada_bwd(c_all.T, dmod_l)[None]

    names = names_e + ["w_mix_in", "ffn1_w_down", "ffn1_w_gu"]
    came = list(came_e) + came_mix + came_wd1 + came_gu1
    part = part_e + part_mix + part_wd1 + part_gu1
    half = [_sum4("chip_sum_" + nm, q, p, chip_idx) for nm, q, p in zip(names, came, part)]
    g_own = dict(zip(names, half))
    g_sib = dict(zip(names, _sibling_share(half)))

    weights = dict(w_ada=w_ada, b_ada=b_ada, norm1_g=norm1_g, ffn1_w_gu=ffn1_w_gu, ffn1_w_down=ffn1_w_down,
                   norm2_g=norm2_g, w_mix_in=w_mix_in, b_merge=b_merge, conv_w=conv_w, w_conv_out=w_conv_out,
                   w_attn_out=w_attn_out, w_out=w_out, norm3_g=norm3_g, ffn2_w_gu=ffn2_w_gu,
                   ffn2_w_down=ffn2_w_down, final_g=final_g)
    ms = dict(w_ada=m_w_ada, b_ada=m_b_ada, norm1_g=m_norm1_g, ffn1_w_gu=m_ffn1_w_gu, ffn1_w_down=m_ffn1_w_down,
              norm2_g=m_norm2_g, w_mix_in=m_w_mix_in, b_merge=m_b_merge, conv_w=m_conv_w, w_conv_out=m_w_conv_out,
              w_attn_out=m_w_attn_out, w_out=m_w_out, norm3_g=m_norm3_g, ffn2_w_gu=m_ffn2_w_gu,
              ffn2_w_down=m_ffn2_w_down, final_g=m_final_g)
    vs = dict(w_ada=v_w_ada, b_ada=v_b_ada, norm1_g=v_norm1_g, ffn1_w_gu=v_ffn1_w_gu, ffn1_w_down=v_ffn1_w_down,
              norm2_g=v_norm2_g, w_mix_in=v_w_mix_in, b_merge=v_b_merge, conv_w=v_conv_w, w_conv_out=v_w_conv_out,
              w_attn_out=v_w_attn_out, w_out=v_w_out, norm3_g=v_norm3_g, ffn2_w_gu=v_ffn2_w_gu,
              ffn2_w_down=v_ffn2_w_down, final_g=v_final_g)
    order = list(weights)
    grad = dict(w_ada=g_w_ada, b_ada=g_b_ada, norm1_g=g_n1, norm2_g=g_n2, norm3_g=g_n3, final_g=g_fin,
                b_merge=g_bm, conv_w=g_cw)
    delta, new_m, new_v = {}, {}, {}
    small_names = ["b_ada", "norm1_g", "norm2_g", "norm3_g", "final_g", "b_merge", "conv_w"]
    flat = lambda d: jnp.concatenate([d[nm].reshape(-1) for nm in small_names])[None, :]
    sd, sm, sv = _adamw("adamw_small", flat(weights), flat(grad), flat(ms), flat(vs))
    off = 0
    for nm in small_names:
        size = weights[nm].size
        for dst, src in ((delta, sd), (new_m, sm), (new_v, sv)):
            dst[nm] = src[0, off:off + size].reshape(weights[nm].shape)
        off += size
    for nm in order:
        if nm in small_names:
            continue
        shp = weights[nm].shape
        if nm in g_own:
            g2, d2, m2, v2 = _adamw_halves("adamw_" + nm, weights[nm][0], g_own[nm], g_sib[nm], ms[nm][0], vs[nm][0],
                                           c_idx)
            grad[nm] = g2.reshape(shp)
        else:
            d2, m2, v2 = _adamw("adamw_" + nm, weights[nm][0], grad[nm][0], ms[nm][0], vs[nm][0])
        delta[nm], new_m[nm], new_v[nm] = d2.reshape(shp), m2.reshape(shp), v2.reshape(shp)

    return (loss, grad_x[None], *[grad[nm] for nm in order], *[delta[nm] for nm in order],
            *[new_m[nm] for nm in order], *[new_v[nm] for nm in order])
```

```python
import functools

import jax
import jax.numpy as jnp
from jax import lax
from jax.experimental import pallas as pl
from jax.experimental.pallas import tpu as pltpu

F32 = jnp.float32
BF16 = jnp.bfloat16
MESH = pl.DeviceIdType.MESH

VMEM_LIMIT = 56 * 1024 * 1024
LANES = 128

D = 1024
D_FF = 2816
FF_SHARD = 1408
MIX_SHARD = 1280
MIX_W = 5120
HEAD_PAIRS = 4
HEAD_DIM = 64
CONV_W = 512
EPS = 1e-6
ATT_BLK = 256

ADAM_LR = 0.001
ADAM_B1 = 0.9
ADAM_B2 = 0.999
ADAM_EPS = 1e-08
ADAM_WD = 0.01
ADAM_STEP = 10


def _params(semantics=None):
    return pltpu.CompilerParams(dimension_semantics=semantics, vmem_limit_bytes=VMEM_LIMIT)


def _sigmoid(x):
    return 1.0 / (1.0 + jnp.exp(-x))


def _place():
    x, y, c = lax.axis_index("x"), lax.axis_index("y"), lax.axis_index("c")
    chips = [(1 - x, y), (x, 1 - y), (1 - x, 1 - y)]
    return x, y, c, chips


def _allgather_rows(name, blk):
    m_per, n = blk.shape

    def body(x_ref, out_ref, send_sems, recv_sems, local_sem):
        x, y, c, chips = _place()
        me, sibling = (x, y, c), (x, y, 1 - c)

        def rows(px, py, pc):
            return out_ref.at[pl.ds((4 * px + 2 * py + pc) * m_per, m_per), :]

        def copy(k, block, to, src=None):
            return pltpu.make_async_remote_copy(
                src_ref=rows(*block) if src is None else src, dst_ref=rows(*block),
                send_sem=send_sems.at[k], recv_sem=recv_sems.at[k], device_id=to, device_id_type=MESH)

        mine = pltpu.make_async_copy(x_ref, rows(*me), local_sem)
        mine.start()
        first = [copy(0, me, sibling, src=x_ref)]
        first += [copy(1 + j, me, (*chip, c), src=x_ref) for j, chip in enumerate(chips)]
        for cp in first:
            cp.start()
        passed = [copy(4 + j, (*chip, c), sibling) for j, chip in enumerate(chips)]
        for j, chip in enumerate(chips):
            copy(1 + j, (*chip, c), me).wait_recv()
            passed[j].start()
        copy(0, sibling, me).wait_recv()
        for j, chip in enumerate(chips):
            copy(4 + j, (*chip, 1 - c), me).wait_recv()
        for cp in first + passed:
            cp.wait_send()
        mine.wait()

    return pl.pallas_call(
        body, name=name,
        out_shape=jax.ShapeDtypeStruct((8 * m_per, n), blk.dtype),
        in_specs=[pl.BlockSpec(memory_space=pltpu.VMEM)],
        out_specs=pl.BlockSpec(memory_space=pltpu.VMEM),
        scratch_shapes=[pltpu.SemaphoreType.DMA((7,)), pltpu.SemaphoreType.DMA((7,)), pltpu.SemaphoreType.DMA],
        compiler_params=pltpu.CompilerParams(vmem_limit_bytes=VMEM_LIMIT),
    )(blk)


def _hbm_specs(n):
    return [pl.BlockSpec(memory_space=pltpu.HBM)] * n


def _allgather_weights(shards):
    n = len(shards)

    def body(*refs):
        start, relay, finish = _gather_protocol(refs[:n], refs[n:2 * n], *refs[2 * n:])
        start()
        relay()
        finish()

    gathered = pl.pallas_call(
        body, name="allgather_weights",
        out_shape=_gather_shapes(shards), in_specs=_hbm_specs(n), out_specs=_hbm_specs(n),
        scratch_shapes=_gather_sems(n),
    )(*shards)
    return _with_own_shard(gathered, shards)


def _gather_shapes(shards):
    return [jax.ShapeDtypeStruct((4, *s.shape), s.dtype) for s in shards]


def _gather_sems(n):
    return [pltpu.SemaphoreType.DMA((6 * n,)), pltpu.SemaphoreType.DMA((6 * n,))]


def _with_own_shard(gathered, shards):
    chip = 2 * lax.axis_index("x") + lax.axis_index("y")
    return [lax.dynamic_update_slice(g, s[None], (chip, 0, 0)) for g, s in zip(gathered, shards)]


def _gather_protocol(ins, outs, send_sems, recv_sems):
    n = len(ins)
    x, y, c, chips = _place()
    me, sibling = (x, y, c), (x, y, 1 - c)
    me_k = 2 * x + y

    def half(w, k, hc):
        h = ins[w].shape[0] // 2
        return outs[w].at[k, pl.ds(pl.multiple_of(hc * h, 8), h), :]

    def copy(w, j, k, hc, to, src=None):
        dst = half(w, k, hc)
        return pltpu.make_async_remote_copy(
            src_ref=dst if src is None else src, dst_ref=dst,
            send_sem=send_sems.at[6 * w + j], recv_sem=recv_sems.at[6 * w + j],
            device_id=to, device_id_type=MESH)

    def first(w, j):
        h = ins[w].shape[0] // 2
        src = ins[w].at[pl.ds(pl.multiple_of(c * h, 8), h), :]
        return copy(w, j, me_k, c, (*chips[j], c), src=src)

    def passed(w, j):
        px, py = chips[j]
        return copy(w, 3 + j, 2 * px + py, c, sibling)

    pairs = [(w, j) for w in range(n) for j in range(3)]

    def start():
        for w, j in pairs:
            first(w, j).start()

    def relay():
        for w, j in pairs:
            px, py = chips[j]
            copy(w, j, 2 * px + py, c, me).wait_recv()
            passed(w, j).start()

    def finish():
        for w, j in pairs:
            px, py = chips[j]
            copy(w, 3 + j, 2 * px + py, 1 - c, me).wait_recv()
        for w, j in pairs:
            first(w, j).wait_send()
            passed(w, j).wait_send()

    return start, relay, finish


def _sibling_swap_halves(name, grads):
    n = len(grads)

    def body(*refs):
        ins, outs = refs[:n], refs[n:2 * n]
        send_sems, recv_sems = refs[2 * n:]
        x, y, c, _ = _place()
        cps = []
        for w in range(n):
            cp = pltpu.make_async_remote_copy(
                src_ref=ins[w].at[:, 1 - c], dst_ref=outs[w],
                send_sem=send_sems.at[w], recv_sem=recv_sems.at[w],
                device_id=(x, y, 1 - c), device_id_type=MESH)
            cp.start()
            cps.append(cp)
        for cp in cps:
            cp.wait()

    return pl.pallas_call(
        body, name=name,
        out_shape=[jax.ShapeDtypeStruct((4, *g.shape[2:]), g.dtype) for g in grads],
        in_specs=_hbm_specs(n), out_specs=_hbm_specs(n),
        scratch_shapes=[pltpu.SemaphoreType.DMA((n,)), pltpu.SemaphoreType.DMA((n,))],
    )(*grads)


def _all_to_all_sems(n):
    return [pltpu.SemaphoreType.DMA((3 * n,)), pltpu.SemaphoreType.DMA((3 * n,))]


def _all_to_all_protocol(ins, outs, send_sems, recv_sems):
    n = len(ins)
    x, y, c, chips = _place()
    me_k = 2 * x + y
    pairs = [(w, j) for w in range(n) for j in range(3)]

    def sent(w, j):
        px, py = chips[j]
        return pltpu.make_async_remote_copy(
            src_ref=ins[w].at[2 * px + py], dst_ref=outs[w].at[me_k],
            send_sem=send_sems.at[3 * w + j], recv_sem=recv_sems.at[3 * w + j],
            device_id=(px, py, c), device_id_type=MESH)

    def start():
        for w, j in pairs:
            sent(w, j).start()

    def finish():
        for w, j in pairs:
            px, py = chips[j]
            slab = outs[w].at[2 * px + py]
            pltpu.make_async_remote_copy(
                src_ref=slab, dst_ref=slab, send_sem=send_sems.at[3 * w + j],
                recv_sem=recv_sems.at[3 * w + j], device_id=(px, py, c), device_id_type=MESH).wait_recv()
        for w, j in pairs:
            sent(w, j).wait_send()

    return start, finish


def _sibling_share(halves):
    n = len(halves)

    def body(*refs):
        ins, outs = refs[:n], refs[n:2 * n]
        send_sems, recv_sems = refs[2 * n:]
        x, y, c, _ = _place()
        cps = []
        for w in range(n):
            cp = pltpu.make_async_remote_copy(
                src_ref=ins[w], dst_ref=outs[w], send_sem=send_sems.at[w], recv_sem=recv_sems.at[w],
                device_id=(x, y, 1 - c), device_id_type=MESH)
            cp.start()
            cps.append(cp)
        for cp in cps:
            cp.wait()

    return pl.pallas_call(
        body, name="grad_sibling_share",
        out_shape=[jax.ShapeDtypeStruct(p.shape, p.dtype) for p in halves],
        in_specs=_hbm_specs(n), out_specs=_hbm_specs(n),
        scratch_shapes=[pltpu.SemaphoreType.DMA((n,)), pltpu.SemaphoreType.DMA((n,))],
    )(*halves)


def _mm(name, grid, a, a_spec, b, b_spec, contract, out_shapes, out_specs, epilogue,
        extras=(), extra_specs=(), nk=1, acc_shape=None, semantics=None, a2a_parts=(), gather_shards=(),
        relay_at=None):
    assert not (a2a_parts and gather_shards)
    moved = tuple(a2a_parts) + tuple(gather_shards)
    ne, no, nc = len(extras), len(out_shapes), len(moved)
    nd = len(grid)

    def body(*refs):
        a_ref, b_ref = refs[0], refs[1]
        ex, outs = refs[2:2 + ne], refs[2 + ne + nc:2 + ne + nc + no]
        if nc:
            ids = [pl.program_id(d) for d in range(nd)]
            comm_refs = (refs[2 + ne:2 + ne + nc], refs[2 + ne + nc + no:2 + ne + 2 * nc + no], *refs[-2:])
            at_start = functools.reduce(jnp.logical_and, [i == 0 for i in ids])
            if a2a_parts:
                start, finish = _all_to_all_protocol(*comm_refs)
                pl.when(at_start)(start)
            else:
                start, relay, finish = _gather_protocol(*comm_refs)
                pl.when(at_start)(start)
                at_relay = grid[0] // 2 if relay_at is None else relay_at
                pl.when(functools.reduce(jnp.logical_and, [ids[0] == at_relay] + [i == 0 for i in ids[1:]]))(relay)

        def prod():
            return lax.dot_general(a_ref[...], b_ref[...], (((contract[0],), (contract[1],)), ((), ())),
                                   preferred_element_type=F32)

        if nk == 1:
            epilogue(prod(), ex, outs)
        else:
            acc = refs[2 + ne + 2 * nc + no]
            k = pl.program_id(nd - 1)

            @pl.when(k == 0)
            def _():
                acc[...] = prod()

            @pl.when(k > 0)
            def _():
                acc[...] += prod()

            @pl.when(k == nk - 1)
            def _():
                epilogue(acc[...], ex, outs)

        if nc:
            pl.when(functools.reduce(jnp.logical_and, [i == g - 1 for i, g in zip(ids, grid)]))(finish)

    if semantics is None or nc:
        semantics = ("arbitrary",) * nd
    return pl.pallas_call(
        body, name=name, grid=grid,
        in_specs=[a_spec, b_spec, *extra_specs] + _hbm_specs(nc),
        out_specs=list(out_specs) + _hbm_specs(nc),
        out_shape=list(out_shapes) + [jax.ShapeDtypeStruct(p.shape, p.dtype) for p in a2a_parts]
        + _gather_shapes(gather_shards),
        scratch_shapes=([] if nk == 1 else [pltpu.VMEM(acc_shape, F32)])
        + (_all_to_all_sems(nc) if a2a_parts else _gather_sems(nc) if gather_shards else []),
        compiler_params=_params(semantics),
    )(a, b, *extras, *moved)


def _store(dtype):
    def epilogue(acc, ex, outs):
        outs[0][...] = acc.astype(dtype)
    return epilogue


def _sds(shape, dtype):
    return jax.ShapeDtypeStruct(shape, dtype)


TR = 512


def _row_spec(width, tr=TR):
    return pl.BlockSpec((tr, width), lambda i: (i, 0))


def _const_spec(shape):
    nd = len(shape)
    return pl.BlockSpec(shape, lambda i: (0,) * nd)


def _norm_mod_fwd(name, h, p):
    S = h.shape[0]

    def body(h_ref, p_ref, u_ref):
        hv = h_ref[...]
        r = lax.rsqrt(jnp.mean(hv * hv, axis=-1, keepdims=True) + EPS)
        nrm = (hv * r) * p_ref[0:1, :]
        u_ref[...] = (nrm * (1.0 + p_ref[1:2, :]) + p_ref[2:3, :]).astype(BF16)

    return pl.pallas_call(
        body, name=name, grid=(S // TR,),
        in_specs=[_row_spec(D), _const_spec((8, D))], out_specs=_row_spec(D),
        out_shape=_sds((S, D), BF16), compiler_params=_params(("parallel",)),
    )(h, p)


def _rmsnorm_parts(hv):
    r = lax.rsqrt(jnp.mean(hv * hv, axis=-1, keepdims=True) + EPS)
    return r, hv * r


def _norm_bwd_tail(h, p, dh_res, prev=None):
    S = h.shape[0]
    row = pl.BlockSpec((TM, D), lambda i, s: (i, 0))
    const = pl.BlockSpec((8, D), lambda i, s: (0, 0))
    extras, specs = [h, p, dh_res], [row, const, row]
    out_shapes, out_specs = [_sds((S, D), F32)], [row]
    if prev is not None:
        extras += [prev[0], prev[1]]
        specs += [const, row]
        out_shapes.append(_sds((S, D), BF16))
        out_specs.append(row)
    out_shapes.append(_sds((8, D), F32))
    out_specs.append(const)

    def epilogue(duv, ex, outs):
        h_ref, p_ref, r_ref = ex[:3]
        dh_ref, sums_ref = outs[0], outs[-1]

        @pl.when(pl.program_id(0) == 0)
        def _():
            sums_ref[...] = jnp.zeros_like(sums_ref)

        g = p_ref[0:1, :]
        r, xn = _rmsnorm_parts(h_ref[...])
        dn = duv * (1.0 + p_ref[1:2, :])
        dxn = dn * g
        dh = r_ref[...] + r * (dxn - xn * jnp.mean(dxn * xn, axis=-1, keepdims=True))
        dh_ref[...] = dh
        sums_ref[0:1, :] += jnp.sum(duv, axis=0, keepdims=True)
        sums_ref[1:2, :] += jnp.sum(duv * (xn * g), axis=0, keepdims=True)
        sums_ref[2:3, :] += jnp.sum(dn * xn, axis=0, keepdims=True)
        if prev is not None:
            pp_ref, f_ref = ex[3:5]
            outs[1][...] = (prev[2] * pp_ref[3:4, :] * dh).astype(BF16)
            sums_ref[3:4, :] += prev[2] * jnp.sum(dh * f_ref[...].astype(F32), axis=0, keepdims=True)

    return dict(extras=tuple(extras), extra_specs=tuple(specs), out_shapes=out_shapes, out_specs=out_specs,
                epilogue=epilogue)


TM = 512


def _ffn_up(name, u, wgu4, shards=()):
    S = u.shape[0]
    n = len(shards)
    ni = S // TM

    def body(u_ref, wg_ref, wu_ref, *rest):
        gu_ref, hm_ref = rest[n:n + 2]
        s, i = pl.program_id(0), pl.program_id(1)
        if n:
            start, relay, finish = _gather_protocol(rest[:n], rest[n + 2:2 * n + 2], *rest[2 * n + 2:])
            pl.when((s == 0) & (i == 0))(start)
            pl.when((s == 1) & (i == 0))(relay)
        uv = u_ref[...]
        g = jnp.dot(uv, wg_ref[...], preferred_element_type=F32)
        up = jnp.dot(uv, wu_ref[...], preferred_element_type=F32)
        gu_ref[0] = g.astype(BF16)
        gu_ref[1] = up.astype(BF16)
        hm_ref[...] = (g * _sigmoid(g) * up).astype(BF16)
        if n:
            pl.when((s == 1) & (i == ni - 1))(finish)

    gu, hm, *gathered = pl.pallas_call(
        body, name=name, grid=(2, ni),
        in_specs=[pl.BlockSpec((TM, D), lambda s, i: (i, 0)),
                  pl.BlockSpec((None, D, FF_SHARD), lambda s, i: (s, 0, 0)),
                  pl.BlockSpec((None, D, FF_SHARD), lambda s, i: (s + 2, 0, 0))] + _hbm_specs(n),
        out_specs=[pl.BlockSpec((2, TM, FF_SHARD), lambda s, i: (0, i, s)),
                   pl.BlockSpec((TM, FF_SHARD), lambda s, i: (i, s))] + _hbm_specs(n),
        out_shape=[_sds((2, S, D_FF), BF16), _sds((S, D_FF), BF16)] + _gather_shapes(shards),
        scratch_shapes=_gather_sems(n) if n else [],
        compiler_params=_params(("arbitrary", "arbitrary") if n else ("parallel", "parallel")),
    )(u, wgu4, wgu4, *shards)
    return gu, hm, _with_own_shard(gathered, shards)


def _proj_residual(name, a, w, h, p, weight, p_next, gather_shards=()):
    S, K = a.shape

    def epilogue(acc, ex, outs):
        h_ref, p_ref, pn_ref = ex
        outs[0][...] = acc.astype(BF16)
        hout = h_ref[...] + weight * p_ref[3:4, :] * acc
        outs[1][...] = hout
        _, xn = _rmsnorm_parts(hout)
        outs[2][...] = ((xn * pn_ref[0:1, :]) * (1.0 + pn_ref[1:2, :]) + pn_ref[2:3, :]).astype(BF16)

    row = _row_spec(D, TM)
    res = _mm(
        name, (S // TM,), a, pl.BlockSpec((TM, K), lambda i: (i, 0)), w, pl.BlockSpec((K, D), lambda i: (0, 0)),
        (1, 0), [_sds((S, D), BF16), _sds((S, D), F32), _sds((S, D), BF16)], [row, row, row], epilogue,
        extras=(h, p, p_next), extra_specs=(row, _const_spec((8, D)), _const_spec((8, D))),
        semantics=("parallel",), gather_shards=gather_shards, relay_at=S // TM - 1)
    return res[0], res[1], res[2], _with_own_shard(res[3:], gather_shards)


def _proj_residual_loss(name, a, w, h, p, weight, gf, target):
    S, K = a.shape

    def epilogue(acc, ex, outs):
        h_ref, p_ref, g_ref, t_ref = ex
        dh_ref, df_ref, sums_ref, loss_ref = outs

        @pl.when(pl.program_id(0) == 0)
        def _():
            sums_ref[...] = jnp.zeros_like(sums_ref)
            loss_ref[...] = jnp.zeros_like(loss_ref)

        gate = p_ref[3:4, :]
        g = g_ref[0:1, :]
        r, xn = _rmsnorm_parts(h_ref[...] + weight * gate * acc)
        err = xn * g - t_ref[...]
        loss_ref[...] += 0.5 * jnp.sum(err * err) * (1.0 / D)
        dout = err * (1.0 / D)
        dxn = dout * g
        dh = r * (dxn - xn * jnp.mean(dxn * xn, axis=-1, keepdims=True))
        dh_ref[...] = dh
        df_ref[...] = (weight * gate * dh).astype(BF16)
        sums_ref[0:1, :] += jnp.sum(dout * xn, axis=0, keepdims=True)
        sums_ref[1:2, :] += weight * jnp.sum(dh * acc, axis=0, keepdims=True)

    row = _row_spec(D, TM)
    return _mm(
        name, (S // TM,), a, pl.BlockSpec((TM, K), lambda i: (i, 0)), w, pl.BlockSpec((K, D), lambda i: (0, 0)),
        (1, 0), [_sds((S, D), F32), _sds((S, D), BF16), _sds((8, D), F32), _sds((8, LANES), F32)],
        [row, row, _const_spec((8, D)), _const_spec((8, LANES))], epilogue,
        extras=(h, p, gf, target), extra_specs=(row, _const_spec((8, D)), _const_spec((8, D)), row),
        semantics=("arbitrary",))


def _ffn_down_bwd(name, df, wd, gu, a2a_parts=()):
    S = df.shape[0]

    def epilogue(acc, ex, outs):
        g = ex[0][0].astype(F32)
        up = ex[0][1].astype(F32)
        sg = _sigmoid(g)
        outs[0][0] = (acc * up * (sg * (1.0 + g * (1.0 - sg)))).astype(BF16)
        outs[0][1] = (acc * g * sg).astype(BF16)

    gu_spec = pl.BlockSpec((2, TM, FF_SHARD), lambda n, i: (0, i, n))
    return _mm(
        name, (2, S // TM), df, pl.BlockSpec((TM, D), lambda n, i: (i, 0)),
        wd, pl.BlockSpec((FF_SHARD, D), lambda n, i: (n, 0)), (1, 1),
        [_sds((2, S, D_FF), BF16)], [gu_spec], epilogue, extras=(gu,), extra_specs=(gu_spec,),
        semantics=("parallel", "parallel"), a2a_parts=a2a_parts)


TK = 1024


def _grad_w(name, a, a_w, b, b_w, b_map, n_out, out_shape, out_block, out_map, a2a_parts=()):
    S = a.shape[0]
    nk = S // TK
    res = _mm(
        name, (n_out, nk), a, pl.BlockSpec((TK, a_w), lambda s, k: (k, 0)), b, pl.BlockSpec(
            (None, TK, b_w) if b.ndim == 3 else (TK, b_w), b_map), (0, 0),
        [_sds(out_shape, BF16)], [pl.BlockSpec(out_block, out_map)], _store(BF16), nk=nk, acc_shape=(a_w, b_w),
        semantics=("parallel", "arbitrary"), a2a_parts=a2a_parts)
    return res if a2a_parts else res[0]


def _ffn_dw_down(name, hm, df):
    S = df.shape[0]
    return _mm(
        name, (2, S // TK), hm, pl.BlockSpec((TK, FF_SHARD), lambda m, k: (k, m)),
        df, pl.BlockSpec((TK, D), lambda m, k: (k, 0)), (0, 0),
        [_sds((D_FF, D), BF16)], [pl.BlockSpec((FF_SHARD, D), lambda m, k: (m, 0))], _store(BF16),
        nk=S // TK, acc_shape=(FF_SHARD, D), semantics=("parallel", "arbitrary"))[0]


def _ffn_up_bwd(name, dgu, wgu4, tail, a2a_parts=()):
    S = dgu.shape[1]
    return _mm(
        name, (S // TM, 4), dgu, pl.BlockSpec((None, TM, FF_SHARD), lambda i, s: (s // 2, i, s % 2)),
        wgu4, pl.BlockSpec((None, D, FF_SHARD), lambda i, s: (s, 0, 0)), (1, 1),
        tail["out_shapes"], tail["out_specs"], tail["epilogue"], extras=tail["extras"],
        extra_specs=tail["extra_specs"], nk=4, acc_shape=(TM, D), semantics=("arbitrary", "arbitrary"),
        a2a_parts=a2a_parts)


def _ffn_dw_gu(name, u_in, dgu, a2a_parts=()):
    return _grad_w(name, u_in, D, dgu, FF_SHARD, lambda s, k: (s // 2, k, s % 2), 4,
                   (4, D, FF_SHARD), (None, D, FF_SHARD), lambda s, k: (s, 0, 0), a2a_parts=a2a_parts)


def _ffn_bwd(tag, df, u_in, gu, hm, wgu4, wd, tail):
    dgu = _ffn_down_bwd(tag + "_down_bwd", df, wd, gu)[0]
    dwd = _ffn_dw_down(tag + "_dw_down", hm, df)
    res = _ffn_up_bwd(tag + "_up_bwd", dgu, wgu4, tail)
    dwgu = _ffn_dw_gu(tag + "_dw_gu", u_in, dgu)
    return res, dwgu, dwd


def _shift_down(v, k, row):
    return jnp.where(row >= k, pltpu.roll(v, k, axis=0), 0.0)


def _shift_up(v, k, row, S):
    return jnp.where(row < S - k, pltpu.roll(v, S - k, axis=0), 0.0)


def _conv_specs(S):
    cols = CONV_W // LANES
    return [pl.BlockSpec((S, LANES), functools.partial(lambda j, off: (0, off + j), off=o * cols))
            for o in range(3)]


def _conv_fwd(proj, conv_w):
    S = proj.shape[0]

    def body(cb_ref, cc_ref, cx_ref, w_ref, sc_ref):
        row = lax.broadcasted_iota(jnp.int32, (S, LANES), 0)
        v = cc_ref[...].astype(F32) * cx_ref[...].astype(F32)
        yv = w_ref[0:1, :] * _shift_down(v, 2, row) + w_ref[1:2, :] * _shift_down(v, 1, row) + w_ref[2:3, :] * v
        sc_ref[...] = (cb_ref[...].astype(F32) * yv).astype(BF16)

    return pl.pallas_call(
        body, name="conv_fwd", grid=(CONV_W // LANES,),
        in_specs=_conv_specs(S) + [pl.BlockSpec((3, LANES), lambda j: (0, j))],
        out_specs=pl.BlockSpec((S, LANES), lambda j: (0, j)), out_shape=_sds((S, CONV_W), BF16),
        compiler_params=_params(("parallel",)),
    )(proj, proj, proj, conv_w)


def _conv_bwd(dsc, proj, conv_w):
    S = proj.shape[0]

    def body(d_ref, cb_ref, cc_ref, cx_ref, w_ref, dcb_ref, dcc_ref, dcx_ref, dw_ref):
        row = lax.broadcasted_iota(jnp.int32, (S, LANES), 0)
        cc = cc_ref[...].astype(F32)
        cx = cx_ref[...].astype(F32)
        d = d_ref[...].astype(F32)
        v = cc * cx
        v1 = _shift_down(v, 1, row)
        v2 = _shift_down(v, 2, row)
        w0, w1, w2 = w_ref[0:1, :], w_ref[1:2, :], w_ref[2:3, :]
        dcb_ref[...] = (d * (w0 * v2 + w1 * v1 + w2 * v)).astype(BF16)
        dy = d * cb_ref[...].astype(F32)
        dw_ref[0:1, :] = jnp.sum(dy * v2, axis=0, keepdims=True)
        dw_ref[1:2, :] = jnp.sum(dy * v1, axis=0, keepdims=True)
        dw_ref[2:3, :] = jnp.sum(dy * v, axis=0, keepdims=True)
        dv = w2 * dy + w1 * _shift_up(dy, 1, row, S) + w0 * _shift_up(dy, 2, row, S)
        dcc_ref[...] = (dv * cx).astype(BF16)
        dcx_ref[...] = (dv * cc).astype(BF16)

    col = pl.BlockSpec((S, LANES), lambda j: (0, j))
    return pl.pallas_call(
        body, name="conv_bwd", grid=(CONV_W // LANES,),
        in_specs=[col] + _conv_specs(S) + [pl.BlockSpec((3, LANES), lambda j: (0, j))],
        out_specs=[col, col, col, pl.BlockSpec((3, LANES), lambda j: (0, j))],
        out_shape=[_sds((S, CONV_W), BF16)] * 3 + [_sds((3, CONV_W), F32)],
        compiler_params=_params(("parallel",)),
    )(dsc, proj, proj, proj, conv_w)


Q_COL, K_COL, V_COL = 1536 // LANES, 2048 // LANES, 2560 // LANES


def _split_dot(x, tri):
    hi = x.astype(BF16)
    lo = (x - hi.astype(F32)).astype(BF16)
    return jnp.dot(hi, tri, preferred_element_type=F32) + jnp.dot(lo, tri, preferred_element_type=F32)


def _tri_dot(tri, x):
    hi = x.astype(BF16)
    lo = (x - hi.astype(F32)).astype(BF16)
    return jnp.dot(tri, hi, preferred_element_type=F32) + jnp.dot(tri, lo, preferred_element_type=F32)


def _softplus(z):
    return jnp.maximum(z, 0.0) + jnp.log(1.0 + jnp.exp(-jnp.abs(z)))


def _nt(a, b):
    return lax.dot_general(a, b, (((1,), (1,)), ((), ())), preferred_element_type=F32)


def _tn(a, b):
    return lax.dot_general(a, b, (((0,), (0,)), ((), ())), preferred_element_type=F32)


def _interleave(gens, delays):
    results = [None] * len(gens)
    live = list(range(len(gens)))
    rnd = 0
    while live:
        for g in list(live):
            if rnd < delays[g]:
                continue
            try:
                next(gens[g])
            except StopIteration as stop:
                results[g] = stop.value
                live.remove(g)
        rnd += 1
    return results


def _attn_fwd(proj, shards):
    S = proj.shape[0]
    B = ATT_BLK
    nq = S // B
    n = len(shards)

    def body(q_ref, k_ref, v_ref, *rest):
        o_ref, t_ref = rest[n:n + 2]
        start, relay, finish = _gather_protocol(rest[:n], rest[n + 2:2 * n + 2], *rest[2 * n + 2:])
        p = pl.program_id(0)
        i = pl.program_id(1)
        pl.when((p == 0) & (i == 0))(start)
        pl.when((p == HEAD_PAIRS // 2) & (i == 0))(relay)
        lo_lane = lax.broadcasted_iota(jnp.int32, (B, LANES), 1) < HEAD_DIM
        row = lax.broadcasted_iota(jnp.int32, (B, B), 0)
        col = lax.broadcasted_iota(jnp.int32, (B, B), 1)
        after = (row > col).astype(BF16)
        causal = col < row
        q2 = q_ref[...] * 0.125
        zero = jnp.zeros((), BF16)
        q_heads = (jnp.where(lo_lane, q2, zero), jnp.where(lo_lane, zero, q2))

        def head_tile(q_h, st, kb, diag):
            k2 = k_ref[pl.ds(pl.multiple_of(kb * B, B), B), :]
            z = _nt(q_h, k2)
            yield
            spz = _softplus(z)
            sp = jnp.where(causal, spz, 0.0) if diag else spz
            hi = sp.astype(BF16)
            lo = (sp - hi.astype(F32)).astype(BF16)
            r = st["r"]
            st["r"] = r + jnp.sum(sp, axis=1, keepdims=True)
            yield
            rem = jnp.dot(hi, after, preferred_element_type=F32) + jnp.dot(lo, after, preferred_element_type=F32)
            yield
            a = jnp.exp(z - spz - (rem + r))
            if diag:
                a = jnp.where(causal, a, 0.0)
            ab = a.astype(BF16)
            yield
            v2 = v_ref[pl.ds(pl.multiple_of(kb * B, B), B), :]
            st["acc"] = st["acc"] + jnp.dot(ab, v2, preferred_element_type=F32)

        def tiles(kbs, carry, diags=(False, False)):
            sts = [dict(r=carry[0], acc=carry[1]), dict(r=carry[2], acc=carry[3])]
            gens = [head_tile(q_h, st, kb, dg) for kb, dg in zip(kbs, diags) for q_h, st in zip(q_heads, sts)]
            _interleave(gens, [t for t in range(len(kbs)) for _ in q_heads])
            return sts[0]["r"], sts[0]["acc"], sts[1]["r"], sts[1]["acc"]

        zr, za = jnp.zeros((B, 1), F32), jnp.zeros((B, LANES), F32)
        carry = lax.fori_loop(0, i % 2, lambda j, cr: tiles([i, i - 1], cr, (True, False)), (zr, za, zr, za))
        carry = lax.fori_loop(0, 1 - i % 2, lambda j, cr: tiles([i], cr, (True,)), carry)
        first = i - 1 - i % 2
        ra, acc_a, rb, acc_b = lax.fori_loop(
            0, i // 2, lambda j, cr: tiles([first - 2 * j, first - 2 * j - 1], cr), carry)
        o_ref[...] = jnp.where(lo_lane, acc_a, acc_b).astype(BF16)
        t_ref[...] = jnp.where(lo_lane, ra, rb).T
        pl.when((p == HEAD_PAIRS - 1) & (i == nq - 1))(finish)

    seq = lambda off: pl.BlockSpec((S, LANES), lambda p, i: (0, off + p))
    blk = pl.BlockSpec((B, LANES), lambda p, i: (i, p))
    o, t, *gathered = pl.pallas_call(
        body, name="attn_fwd", grid=(HEAD_PAIRS, nq),
        in_specs=[pl.BlockSpec((B, LANES), lambda p, i: (i, Q_COL + p)), seq(K_COL), seq(V_COL)] + _hbm_specs(n),
        out_specs=[blk, pl.BlockSpec((LANES, B), lambda p, i: (p, i))] + _hbm_specs(n),
        out_shape=[_sds((S, 512), BF16), _sds((512, S), F32)] + _gather_shapes(shards),
        scratch_shapes=_gather_sems(n),
        compiler_params=_params(("arbitrary", "arbitrary")),
    )(proj, proj, proj, *shards)
    return o, t, _with_own_shard(gathered, shards)


def _attn_bwd(proj, t, do, parts):
    S = proj.shape[0]
    kt = proj[:, K_COL * LANES:V_COL * LANES].T
    B = ATT_BLK
    nq = S // B
    n = len(parts)

    def body(q_ref, k_ref, v_ref, kt_ref, t_ref, do_ref, *rest):
        dq_ref, dk_ref, dv_ref = rest[n:n + 3]
        dk_acc, dv_acc = rest[2 * n + 3:2 * n + 5]
        start, finish = _all_to_all_protocol(rest[:n], rest[n + 3:2 * n + 3], *rest[2 * n + 5:])
        i = pl.program_id(1)
        pl.when((pl.program_id(0) == 0) & (i == 0))(start)

        @pl.when(i == 0)
        def _():
            dk_acc[...] = jnp.zeros_like(dk_acc)
            dv_acc[...] = jnp.zeros_like(dv_acc)

        lo_lane = lax.broadcasted_iota(jnp.int32, (B, LANES), 1) < HEAD_DIM
        key = lax.broadcasted_iota(jnp.int32, (B, B), 0)
        qry = lax.broadcasted_iota(jnp.int32, (B, B), 1)
        upto = (qry <= key).astype(BF16)
        before = (qry < key).astype(BF16)
        causal = key < qry
        zero = jnp.zeros((), BF16)
        q2 = q_ref[...] * 0.125
        do2 = do_ref[...]
        heads = ((jnp.where(lo_lane, q2, zero), jnp.where(lo_lane, do2, zero), t_ref[0:1, :]),
                 (jnp.where(lo_lane, zero, q2), jnp.where(lo_lane, zero, do2), t_ref[HEAD_DIM:HEAD_DIM + 1, :]))

        def head_tile(head, st, kb, diag):
            q_h, do_h, t_h = head
            rows = pl.ds(pl.multiple_of(kb * B, B), B)
            z = _nt(k_ref[rows, :], q_h)
            da = _nt(v_ref[rows, :], do_h)
            yield
            spz = _softplus(z)
            sp = jnp.where(causal, spz, 0.0) if diag else spz
            hi = sp.astype(BF16)
            lo = (sp - hi.astype(F32)).astype(BF16)
            pc = st["pc"]
            st["pc"] = pc + jnp.sum(sp, axis=0, keepdims=True)
            yield
            pref = jnp.dot(upto, hi, preferred_element_type=F32) + jnp.dot(upto, lo, preferred_element_type=F32)
            yield
            a = jnp.exp(z - spz - ((t_h - pc) - pref))
            if diag:
                a = jnp.where(causal, a, 0.0)
            e = a * da
            eb = e.astype(BF16)
            ab = a.astype(BF16)
            ec = st["ec"]
            st["ec"] = ec + jnp.sum(e, axis=0, keepdims=True)
            yield
            e_before = ec + jnp.dot(before, eb, preferred_element_type=F32)
            yield
            u = jnp.exp(-spz)
            dz = u * (e + e_before) - e_before
            if diag:
                dz = jnp.where(causal, dz, 0.0)
            dzb = dz.astype(BF16)
            yield
            st["dqt"] = st["dqt"] + jnp.dot(kt_ref[:, rows], dzb, preferred_element_type=F32)
            return (jnp.dot(dzb, q_h, preferred_element_type=F32), jnp.dot(ab, do_h, preferred_element_type=F32))

        def tiles(kbs, carry, diags=(False, False)):
            sts = [dict(pc=carry[3 * h], ec=carry[3 * h + 1], dqt=carry[3 * h + 2]) for h in range(2)]
            gens = [head_tile(hd, st, kb, dg) for kb, dg in zip(kbs, diags) for hd, st in zip(heads, sts)]
            res = _interleave(gens, [t for t in range(len(kbs)) for _ in heads])
            for t, kb in enumerate(kbs):
                rows = pl.ds(pl.multiple_of(kb * B, B), B)
                (dk_a, dv_a), (dk_b, dv_b) = res[2 * t], res[2 * t + 1]
                dk_acc[rows, :] += dk_a + dk_b
                dv_acc[rows, :] += dv_a + dv_b
            return tuple(st[nm] for st in sts for nm in ("pc", "ec", "dqt"))

        zc, zq = jnp.zeros((1, B), F32), jnp.zeros((LANES, B), F32)
        carry = lax.fori_loop(0, i // 2, lambda j, cr: tiles([2 * j, 2 * j + 1], cr), (zc, zc, zq, zc, zc, zq))
        carry = lax.fori_loop(0, i % 2, lambda j, cr: tiles([i - 1, i], cr, (False, True)), carry)
        _, _, dqt_a, _, _, dqt_b = lax.fori_loop(0, 1 - i % 2, lambda j, cr: tiles([i], cr, (True,)), carry)
        head0 = lax.broadcasted_iota(jnp.int32, (LANES, B), 0) < HEAD_DIM
        dq_ref[...] = (jnp.where(head0, dqt_a, dqt_b).T * 0.125).astype(BF16)

        @pl.when(i == nq - 1)
        def _():
            dk_ref[...] = dk_acc[...].astype(BF16)
            dv_ref[...] = dv_acc[...].astype(BF16)

        pl.when((pl.program_id(0) == HEAD_PAIRS - 1) & (i == nq - 1))(finish)

    seq = lambda off: pl.BlockSpec((S, LANES), lambda p, i: (0, off + p))
    blk = pl.BlockSpec((B, LANES), lambda p, i: (i, p))
    whole = pl.BlockSpec((S, LANES), lambda p, i: (0, p))
    dq, dk, dv, *came = pl.pallas_call(
        body, name="attn_bwd", grid=(HEAD_PAIRS, nq),
        in_specs=[pl.BlockSpec((B, LANES), lambda p, i: (i, Q_COL + p)), seq(K_COL), seq(V_COL),
                  pl.BlockSpec((LANES, S), lambda p, i: (p, 0)), pl.BlockSpec((LANES, B), lambda p, i: (p, i)), blk]
        + _hbm_specs(n),
        out_specs=[blk, whole, whole] + _hbm_specs(n),
        out_shape=[_sds((S, 512), BF16)] * 3 + [jax.ShapeDtypeStruct(p.shape, p.dtype) for p in parts],
        scratch_shapes=[pltpu.VMEM((S, LANES), F32), pltpu.VMEM((S, LANES), F32)] + _all_to_all_sems(n),
        compiler_params=_params(("arbitrary", "arbitrary")),
    )(proj, proj, proj, kt, t, do, *parts)
    return dq, dk, dv, came


GA_COL, GB_COL = 3072 // D, 4096 // D


def _merge_fwd(sc, o, wco4, wao4, proj, bm):
    S = sc.shape[0]

    def body(sc_ref, o_ref, wc_ref, wa_ref, ga_ref, gb_ref, bm_ref, ya_ref, yb_ref, mg_ref):
        scv, ov = sc_ref[...], o_ref[...]
        for s in range(4):
            cols = slice(s * 256, (s + 1) * 256)
            ya = jnp.dot(scv, wc_ref[s], preferred_element_type=F32)
            yb = jnp.dot(ov, wa_ref[s], preferred_element_type=F32)
            sa = _sigmoid(ga_ref[:, cols].astype(F32) + bm_ref[0:1, cols])
            sb = _sigmoid(gb_ref[:, cols].astype(F32) + bm_ref[1:2, cols])
            ya_ref[:, cols] = ya.astype(BF16)
            yb_ref[:, cols] = yb.astype(BF16)
            mg_ref[:, cols] = (sa * ya + sb * yb).astype(BF16)

    wide = pl.BlockSpec((TM, 512), lambda i: (i, 0))
    wsp = pl.BlockSpec((4, 512, 256), lambda i: (0, 0, 0))
    out = pl.BlockSpec((TM, D), lambda i: (i, 0))
    return pl.pallas_call(
        body, name="merge_fwd", grid=(S // TM,),
        in_specs=[wide, wide, wsp, wsp, pl.BlockSpec((TM, D), lambda i: (i, GA_COL)),
                  pl.BlockSpec((TM, D), lambda i: (i, GB_COL)), pl.BlockSpec((2, D), lambda i: (0, 0))],
        out_specs=[out, out, out], out_shape=[_sds((S, D), BF16)] * 3,
        compiler_params=_params(("parallel",)),
    )(sc, o, wco4, wao4, proj, proj, bm)


def _merge_bwd(dy2, wout, ya, yb, proj, bm):
    S = dy2.shape[0]

    def epilogue(acc, ex, outs):
        ya_ref, yb_ref, ga_ref, gb_ref, bm_ref = ex
        i = pl.program_id(0)
        sa = _sigmoid(ga_ref[...].astype(F32) + bm_ref[0:1, :])
        sb = _sigmoid(gb_ref[...].astype(F32) + bm_ref[1:2, :])
        dga = acc * ya_ref[...].astype(F32) * (sa * (1.0 - sa))
        dgb = acc * yb_ref[...].astype(F32) * (sb * (1.0 - sb))
        outs[0][...] = (acc * sa).astype(BF16)
        outs[1][...] = (acc * sb).astype(BF16)
        outs[2][...] = dga.astype(BF16)
        outs[3][...] = dgb.astype(BF16)

        @pl.when(i == 0)
        def _():
            outs[4][...] = jnp.zeros_like(outs[4])

        outs[4][0:1, :] += jnp.sum(dga, axis=0, keepdims=True)
        outs[4][1:2, :] += jnp.sum(dgb, axis=0, keepdims=True)

    tm = TM // 2
    out = pl.BlockSpec((tm, D), lambda i: (i, 0))
    return _mm(
        "merge_bwd", (S // tm,), dy2, out, wout, pl.BlockSpec((D, D), lambda i: (0, 0)), (1, 1),
        [_sds((S, D), BF16)] * 4 + [_sds((8, D), F32)], [out, out, out, out, _const_spec((8, D))],
        epilogue, extras=(ya, yb, proj, proj, bm),
        extra_specs=(out, out, pl.BlockSpec((tm, D), lambda i: (i, GA_COL)),
                     pl.BlockSpec((tm, D), lambda i: (i, GB_COL)), _const_spec((2, D))),
        semantics=("arbitrary",))


def _mixer_out_bwd(dya, dyb, wco4, wao4):
    S = dya.shape[0]

    def body(da_ref, db_ref, wc_ref, wa_ref, dsc_ref, do_ref):
        for d_ref, w_ref, o_ref in ((da_ref, wc_ref, dsc_ref), (db_ref, wa_ref, do_ref)):
            acc = _nt(d_ref[:, 0:256], w_ref[0])
            for s in range(1, 4):
                acc = acc + _nt(d_ref[:, s * 256:(s + 1) * 256], w_ref[s])
            o_ref[...] = acc.astype(BF16)

    wide = pl.BlockSpec((TM, D), lambda i: (i, 0))
    wsp = pl.BlockSpec((4, 512, 256), lambda i: (0, 0, 0))
    out = pl.BlockSpec((TM, 512), lambda i: (i, 0))
    return pl.pallas_call(
        body, name="mixer_out_bwd", grid=(S // TM,), in_specs=[wide, wide, wsp, wsp], out_specs=[out, out],
        out_shape=[_sds((S, 512), BF16)] * 2, compiler_params=_params(("parallel",)),
    )(dya, dyb, wco4, wao4)


def _mixer_out_dw(sc, o, dya, dyb):
    S = sc.shape[0]
    nk = S // TK

    def body(sc_ref, o_ref, da_ref, db_ref, dwc_ref, dwa_ref, acc_c, acc_a):
        k = pl.program_id(0)

        @pl.when(k == 0)
        def _():
            acc_c[...] = jnp.zeros_like(acc_c)
            acc_a[...] = jnp.zeros_like(acc_a)

        acc_c[...] += _tn(sc_ref[...], da_ref[...])
        acc_a[...] += _tn(o_ref[...], db_ref[...])

        @pl.when(k == nk - 1)
        def _():
            for s in range(4):
                dwc_ref[s] = acc_c[:, s * 256:(s + 1) * 256].astype(BF16)
                dwa_ref[s] = acc_a[:, s * 256:(s + 1) * 256].astype(BF16)

    narrow = pl.BlockSpec((TK, 512), lambda k: (k, 0))
    wide = pl.BlockSpec((TK, D), lambda k: (k, 0))
    out = pl.BlockSpec((4, 512, 256), lambda k: (0, 0, 0))
    return pl.pallas_call(
        body, name="mixer_out_dw", grid=(nk,), in_specs=[narrow, narrow, wide, wide], out_specs=[out, out],
        out_shape=[_sds((4, 512, 256), BF16)] * 2,
        scratch_shapes=[pltpu.VMEM((512, D), F32), pltpu.VMEM((512, D), F32)],
        compiler_params=_params(("arbitrary",)),
    )(sc, o, dya, dyb)


ADA_SHARD = 2304
ADA_TN = 768


def _ada_fwd(c_all, w_ada_l, b_l):
    def body(c_ref, w_ref, b_ref, o_ref):
        cv = c_ref[...]
        ca = cv * _sigmoid(cv)
        o_ref[...] = jnp.dot(ca.astype(BF16), w_ref[...].astype(BF16), preferred_element_type=F32) + b_ref[...]

    return pl.pallas_call(
        body, name="ada_fwd", grid=(ADA_SHARD // ADA_TN,),
        in_specs=[pl.BlockSpec((8, D), lambda j: (0, 0)), pl.BlockSpec((D, ADA_TN), lambda j: (0, j)),
                  pl.BlockSpec((1, ADA_TN), lambda j: (0, j))],
        out_specs=pl.BlockSpec((8, ADA_TN), lambda j: (0, j)), out_shape=_sds((8, ADA_SHARD), F32),
        compiler_params=_params(("parallel",)),
    )(c_all, w_ada_l, b_l)


def _ada_bwd(c_all_t, dmod_l):
    def body(c_ref, d_ref, o_ref):
        cv = c_ref[...]
        ca = cv * _sigmoid(cv)
        o_ref[...] = jnp.dot(ca.astype(BF16).astype(F32), d_ref[...].astype(BF16).astype(F32),
                             preferred_element_type=F32, precision=lax.Precision.HIGHEST)

    return pl.pallas_call(
        body, name="ada_bwd", grid=(ADA_SHARD // ADA_TN,),
        in_specs=[pl.BlockSpec((D, 8), lambda j: (0, 0)), pl.BlockSpec((8, ADA_TN), lambda j: (0, j))],
        out_specs=pl.BlockSpec((D, ADA_TN), lambda j: (0, j)), out_shape=_sds((D, ADA_SHARD), F32),
        compiler_params=_params(("parallel",)),
    )(c_all_t, dmod_l)


def _sum_rows(name, x):
    n = x.shape[1]

    def body(x_ref, o_ref):
        s = x_ref[0:1, :]
        for d in range(1, 8):
            s = s + x_ref[d:d + 1, :]
        o_ref[...] = s

    return pl.pallas_call(
        body, name=name, in_specs=[pl.BlockSpec(memory_space=pltpu.VMEM)],
        out_specs=pl.BlockSpec(memory_space=pltpu.VMEM), out_shape=_sds((1, n), F32),
        compiler_params=pltpu.CompilerParams(vmem_limit_bytes=VMEM_LIMIT),
    )(x)


def _pair_sum(name, g4, recv, c_idx):
    _, _, h, C = g4.shape
    tr = h if h <= 512 else h // (h // 256) if h % 256 == 0 else h // 2

    def body(c_ref, g_ref, r_ref, o_ref):
        o_ref[...] = (g_ref[...].astype(F32) + r_ref[...].astype(F32)).astype(BF16)

    grid_spec = pltpu.PrefetchScalarGridSpec(
        num_scalar_prefetch=1, grid=(4, h // tr),
        in_specs=[pl.BlockSpec((None, None, tr, C), lambda k, i, c: (k, c[0], i, 0)),
                  pl.BlockSpec((None, tr, C), lambda k, i, c: (k, i, 0))],
        out_specs=pl.BlockSpec((None, tr, C), lambda k, i, c: (k, i, 0)))
    return pl.pallas_call(
        body, name=name, grid_spec=grid_spec, out_shape=_sds((4, h, C), BF16),
        compiler_params=_params(("parallel", "parallel")),
    )(c_idx, g4, recv)


def _sum4(name, q, p, chip_idx):
    _, h, C = q.shape
    tr = h if h <= 512 else h // (h // 256) if h % 256 == 0 else h // 2

    def body(k_ref, q_ref, p_ref, o_ref):
        me = k_ref[0]
        terms = [jnp.where(me == k, p_ref[...], q_ref[k]).astype(F32) for k in range(4)]
        o_ref[...] = ((terms[0] + terms[1]) + terms[2]) + terms[3]

    grid_spec = pltpu.PrefetchScalarGridSpec(
        num_scalar_prefetch=1, grid=(h // tr,),
        in_specs=[pl.BlockSpec((4, tr, C), lambda i, k: (0, i, 0)),
                  pl.BlockSpec((None, tr, C), lambda i, k: (k[0], i, 0))],
        out_specs=pl.BlockSpec((tr, C), lambda i, k: (i, 0)))
    return pl.pallas_call(
        body, name=name, grid_spec=grid_spec, out_shape=_sds((h, C), F32), compiler_params=_params(("parallel",)),
    )(chip_idx, q, p)


def _adamw(name, w, g, m, v):
    R, C = w.shape
    tr = R
    while tr * C * 4 > (1 << 20) and tr % 16 == 0:
        tr //= 2
    c1 = 1.0 - ADAM_B1 ** ADAM_STEP
    c2 = 1.0 - ADAM_B2 ** ADAM_STEP

    def body(w_ref, g_ref, m_ref, v_ref, d_ref, nm_ref, nv_ref):
        gv = g_ref[...]
        nm = ADAM_B1 * m_ref[...] + (1.0 - ADAM_B1) * gv
        nv = ADAM_B2 * v_ref[...] + (1.0 - ADAM_B2) * (gv * gv)
        nm_ref[...] = nm
        nv_ref[...] = nv
        d_ref[...] = -ADAM_LR * ((nm * (1.0 / c1)) / (jnp.sqrt(nv * (1.0 / c2)) + ADAM_EPS) + ADAM_WD * w_ref[...])

    spec = pl.BlockSpec((tr, C), lambda i: (i, 0))
    return pl.pallas_call(
        body, name=name, grid=(R // tr,), in_specs=[spec] * 4, out_specs=[spec] * 3,
        out_shape=[_sds((R, C), F32)] * 3, compiler_params=_params(("parallel",)),
    )(w, g, m, v)


def _adamw_halves(name, w, own, sib, m, v, c_idx):
    R, C = w.shape
    h = R // 2
    tr = h
    while tr * C * 4 > (1 << 20) and tr % 16 == 0:
        tr //= 2
    nb = h // tr
    c1 = 1.0 - ADAM_B1 ** ADAM_STEP
    c2 = 1.0 - ADAM_B2 ** ADAM_STEP

    def body(c_ref, w_ref, own_ref, sib_ref, m_ref, v_ref, g_ref, d_ref, nm_ref, nv_ref):
        mine = (pl.program_id(0) // nb) == c_ref[0]
        gv = jnp.where(mine, own_ref[...], sib_ref[...])
        nm = ADAM_B1 * m_ref[...] + (1.0 - ADAM_B1) * gv
        nv = ADAM_B2 * v_ref[...] + (1.0 - ADAM_B2) * (gv * gv)
        g_ref[...] = gv
        nm_ref[...] = nm
        nv_ref[...] = nv
        d_ref[...] = -ADAM_LR * ((nm * (1.0 / c1)) / (jnp.sqrt(nv * (1.0 / c2)) + ADAM_EPS) + ADAM_WD * w_ref[...])

    spec = pl.BlockSpec((tr, C), lambda i, c: (i, 0))
    half = pl.BlockSpec((tr, C), lambda i, c: (i % nb, 0))
    grid_spec = pltpu.PrefetchScalarGridSpec(
        num_scalar_prefetch=1, grid=(R // tr,), in_specs=[spec, half, half, spec, spec], out_specs=[spec] * 4)
    return pl.pallas_call(
        body, name=name, grid_spec=grid_spec, out_shape=[_sds((R, C), F32)] * 4,
        compiler_params=_params(("parallel",)),
    )(c_idx, w, own, sib, m, v)


def _pack(g, scale, shift, gate):
    rows = jnp.stack([g, scale, shift, gate]).astype(F32)
    return jnp.concatenate([rows, jnp.zeros((4, D), F32)], axis=0)


def _fold8(vec):
    m = -(-vec.shape[0] // (8 * LANES)) * LANES
    return jnp.concatenate([vec, jnp.zeros((8 * m - vec.shape[0],), vec.dtype)]).reshape(8, m)


def _allgather_vectors(name, vec):
    return _allgather_rows(name, _fold8(vec)).reshape(8, -1)


def kernel(x, c, w_ada, b_ada, norm1_g, ffn1_w_gu, ffn1_w_down, norm2_g, w_mix_in, b_merge, conv_w, w_conv_out, w_attn_out, w_out, norm3_g, ffn2_w_gu, ffn2_w_down, final_g, loss_target, m_w_ada, m_b_ada, m_norm1_g, m_ffn1_w_gu, m_ffn1_w_down, m_norm2_g, m_w_mix_in, m_b_merge, m_conv_w, m_w_conv_out, m_w_attn_out, m_w_out, m_norm3_g, m_ffn2_w_gu, m_ffn2_w_down, m_final_g, v_w_ada, v_b_ada, v_norm1_g, v_ffn1_w_gu, v_ffn1_w_down, v_norm2_g, v_w_mix_in, v_b_merge, v_conv_w, v_w_conv_out, v_w_attn_out, v_w_out, v_norm3_g, v_ffn2_w_gu, v_ffn2_w_down, v_final_g):
    xi, yi, ci = lax.axis_index("x"), lax.axis_index("y"), lax.axis_index("c")
    chip = 2 * xi + yi
    dev = 4 * xi + 2 * yi + ci
    S = x.shape[1]
    h0 = x[0]
    target = loss_target[0]

    (wgu1,) = _allgather_weights([ffn1_w_gu[0].astype(BF16)])
    late_shards = [w[0].astype(BF16) for w in (w_conv_out, w_attn_out, w_out, ffn2_w_gu, ffn2_w_down)]
    c_idx = jnp.reshape(ci, (1,)).astype(jnp.int32)
    chip_idx = jnp.reshape(chip, (1,)).astype(jnp.int32)

    def reduce_pairs(tag, names, grads):
        g4 = [g.reshape(4, 2, g.shape[1] // 2, g.shape[2]) for g in grads]
        recv = _sibling_swap_halves("grad_sibling_swap_" + tag, g4)
        return [_pair_sum("pair_sum_" + nm, a, b, c_idx) for nm, a, b in zip(names, g4, recv)]

    small = jnp.concatenate([c[0], b_merge[0].reshape(-1), conv_w[0].reshape(-1)])
    gathered = _allgather_vectors("allgather_small", small)
    c_all = gathered[:, :D]
    per_chip = gathered[0::2]
    bm_full = jnp.concatenate([per_chip[k, D:D + 512].reshape(2, 256) for k in range(4)], axis=1)
    cw_full = jnp.concatenate([per_chip[k, D + 512:D + 896].reshape(3, 128) for k in range(4)], axis=1)
    b_l = lax.dynamic_slice_in_dim(b_ada, chip * ADA_SHARD, ADA_SHARD, axis=1)
    mod_l = _ada_fwd(c_all, w_ada[0], b_l)
    mod_g = _allgather_rows("allgather_mod", mod_l).reshape(8, 8, ADA_SHARD)
    mod_all = jnp.concatenate([mod_g[2 * k] for k in range(4)], axis=1)
    mod = lax.dynamic_slice_in_dim(mod_all, dev, 1, axis=0).reshape(3, 3, D)
    p1 = _pack(norm1_g[0], mod[0, 1], mod[0, 0], mod[0, 2])
    p2 = _pack(norm2_g[0], mod[1, 1], mod[1, 0], mod[1, 2])
    p3 = _pack(norm3_g[0], mod[2, 1], mod[2, 0], mod[2, 2])
    pf = _pack(final_g, final_g, final_g, final_g)

    u1 = _norm_mod_fwd("norm1_fwd", h0, p1)
    gu1, hm1, (wd1,) = _ffn_up("ffn1_up", u1, wgu1, [ffn1_w_down[0].astype(BF16)])
    f1, h1, u2, (wmix,) = _proj_residual("ffn1_down", hm1, wd1.reshape(D_FF, D), h0, p1, 0.5, p2,
                                         gather_shards=[w_mix_in[0].astype(BF16)])
    wd1 = wd1.reshape(D_FF, D)
    proj = _mm("mix_in", (4, S // TM), u2, pl.BlockSpec((TM, D), lambda s, i: (i, 0)),
               wmix, pl.BlockSpec((None, D, MIX_SHARD), lambda s, i: (s, 0, 0)), (1, 0),
               [_sds((S, MIX_W), BF16)], [pl.BlockSpec((TM, MIX_SHARD), lambda s, i: (i, s))], _store(BF16),
               semantics=("parallel", "parallel"))[0]
    sc = _conv_fwd(proj, cw_full)
    o, t_tot, (wco, wao, wout, wgu2, wd2) = _attn_fwd(proj, late_shards)
    wout = wout.reshape(D, D)
    wd2 = wd2.reshape(D_FF, D)
    ya, yb, merged = _merge_fwd(sc, o, wco, wao, proj, bm_full)
    y2, h2, u3, _ = _proj_residual("mix_out", merged, wout, h1, p2, 1.0, p3)
    gu3, hm3, _ = _ffn_up("ffn2_up", u3, wgu2)

    dh3, df3, sums_f, loss_blk = _proj_residual_loss("ffn2_down", hm3, wd2, h2, p3, 0.5, pf, target)
    (dh2, dy2, sums3), dwgu2, dwd2 = _ffn_bwd("ffn2", df3, u3, gu3, hm3, wgu2, wd2,
                                              _norm_bwd_tail(h2, p3, dh3, prev=(p2, y2, 1.0)))

    dya, dyb, dga, dgb, sums_bm = _merge_bwd(dy2, wout, ya, yb, proj, bm_full)
    dwout = _mm("dw_out", (1, S // TK), merged, pl.BlockSpec((TK, D), lambda n, k: (k, 0)),
                dy2, pl.BlockSpec((TK, D), lambda n, k: (k, 0)), (0, 0),
                [_sds((D, D), BF16)], [pl.BlockSpec((D, D), lambda n, k: (0, 0))], _store(BF16),
                nk=S // TK, acc_shape=(D, D))[0]
    dsc, do = _mixer_out_bwd(dya, dyb, wco, wao)
    dwco, dwao = _mixer_out_dw(sc, o, dya, dyb)
    dcb, dcc, dcx, dcw = _conv_bwd(dsc, proj, cw_full)
    names_e = ["ffn2_w_gu", "ffn2_w_down", "w_out", "w_conv_out", "w_attn_out"]
    part_e = reduce_pairs("early", names_e, [dwgu2, dwd2.reshape(4, 704, D), dwout.reshape(4, 256, D), dwco, dwao])
    dq, dk, dv, came_e = _attn_bwd(proj, t_tot, do, part_e)
    dproj = jnp.concatenate([dcb, dcc, dcx, dq, dk, dv, dga, dgb], axis=1)
    tail2 = _norm_bwd_tail(h1, p2, dh2, prev=(p1, f1, 0.5))
    dh1, df1, sums2 = _mm("mix_in_bwd", (S // TM, 4), dproj, pl.BlockSpec((TM, MIX_SHARD), lambda i, s: (i, s)),
                          wmix, pl.BlockSpec((None, D, MIX_SHARD), lambda i, s: (s, 0, 0)), (1, 1),
                          tail2["out_shapes"], tail2["out_specs"], tail2["epilogue"], extras=tail2["extras"],
                          extra_specs=tail2["extra_specs"], nk=4, acc_shape=(TM, D),
                          semantics=("arbitrary", "arbitrary"))
    dwmix = _grad_w("dw_mix_in", u2, D, dproj, MIX_SHARD, lambda s, k: (k, s), 4, (4, D, MIX_SHARD),
                    (None, D, MIX_SHARD), lambda s, k: (s, 0, 0))

    part_mix = reduce_pairs("mix", ["w_mix_in"], [dwmix])
    dgu1, *came_mix = _ffn_down_bwd("ffn1_down_bwd", df1, wd1, gu1, a2a_parts=part_mix)
    dwd1 = _ffn_dw_down("ffn1_dw_down", hm1, df1)
    part_wd1 = reduce_pairs("wd1", ["ffn1_w_down"], [dwd1.reshape(4, 704, D)])
    dwgu1, *came_wd1 = _ffn_dw_gu("ffn1_dw_gu", u1, dgu1, a2a_parts=part_wd1)
    part_gu1 = reduce_pairs("gu1", ["ffn1_w_gu"], [dwgu1])
    grad_x, sums1, *came_gu1 = _ffn_up_bwd("ffn1_up_bwd", dgu1, wgu1, _norm_bwd_tail(h0, p1, dh1),
                                           a2a_parts=part_gu1)

    dmod = jnp.stack([sums1[0], sums1[1], sums2[3], sums2[0], sums2[1], sums3[3], sums3[0], sums3[1], sums_f[1]])
    small_g = jnp.concatenate([dmod.reshape(-1), sums1[2], sums2[2], sums3[2], sums_f[0],
                               sums_bm[0], sums_bm[1], dcw.reshape(-1), loss_blk[0, 0:1]])
    all_g = _allgather_vectors("allgather_small_grads", small_g)
    tot = _sum_rows("sum_small_grads", all_g)[0]
    loss = tot[16 * D + 512]
    g_b_ada = tot[:9 * D][None, :]
    g_n1, g_n2, g_n3 = (tot[(9 + k) * D:(10 + k) * D][None, :] for k in range(3))
    g_fin = tot[12 * D:13 * D]
    g_bm = lax.dynamic_slice_in_dim(tot[13 * D:15 * D].reshape(2, D), chip * 256, 256, axis=1)[None]
    g_cw = lax.dynamic_slice_in_dim(tot[15 * D:16 * D + 512].reshape(3, 512), chip * 128, 128, axis=1)[None]
    dmod_l = lax.dynamic_slice_in_dim(all_g[:, :9 * D], chip * ADA_SHARD, ADA_SHARD, axis=1)
    g_w_ada = _ada_bwd(c_all.T, dmod_l)[None]

    names = names_e + ["w_mix_in", "ffn1_w_down", "ffn1_w_gu"]
    came = list(came_e) + came_mix + came_wd1 + came_gu1
    part = part_e + part_mix + part_wd1 + part_gu1
    half = [_sum4("chip_sum_" + nm, q, p, chip_idx) for nm, q, p in zip(names, came, part)]
    g_own = dict(zip(names, half))
    g_sib = dict(zip(names, _sibling_share(half)))

    weights = dict(w_ada=w_ada, b_ada=b_ada, norm1_g=norm1_g, ffn1_w_gu=ffn1_w_gu, ffn1_w_down=ffn1_w_down,
                   norm2_g=norm2_g, w_mix_in=w_mix_in, b_merge=b_merge, conv_w=conv_w, w_conv_out=w_conv_out,
                   w_attn_out=w_attn_out, w_out=w_out, norm3_g=norm3_g, ffn2_w_gu=ffn2_w_gu,
                   ffn2_w_down=ffn2_w_down, final_g=final_g)
    ms = dict(w_ada=m_w_ada, b_ada=m_b_ada, norm1_g=m_norm1_g, ffn1_w_gu=m_ffn1_w_gu, ffn1_w_down=m_ffn1_w_down,
              norm2_g=m_norm2_g, w_mix_in=m_w_mix_in, b_merge=m_b_merge, conv_w=m_conv_w, w_conv_out=m_w_conv_out,
              w_attn_out=m_w_attn_out, w_out=m_w_out, norm3_g=m_norm3_g, ffn2_w_gu=m_ffn2_w_gu,
              ffn2_w_down=m_ffn2_w_down, final_g=m_final_g)
    vs = dict(w_ada=v_w_ada, b_ada=v_b_ada, norm1_g=v_norm1_g, ffn1_w_gu=v_ffn1_w_gu, ffn1_w_down=v_ffn1_w_down,
              norm2_g=v_norm2_g, w_mix_in=v_w_mix_in, b_merge=v_b_merge, conv_w=v_conv_w, w_conv_out=v_w_conv_out,
              w_attn_out=v_w_attn_out, w_out=v_w_out, norm3_g=v_norm3_g, ffn2_w_gu=v_ffn2_w_gu,
              ffn2_w_down=v_ffn2_w_down, final_g=v_final_g)
    order = list(weights)
    grad = dict(w_ada=g_w_ada, b_ada=g_b_ada, norm1_g=g_n1, norm2_g=g_n2, norm3_g=g_n3, final_g=g_fin,
                b_merge=g_bm, conv_w=g_cw)
    delta, new_m, new_v = {}, {}, {}
    small_names = ["b_ada", "norm1_g", "norm2_g", "norm3_g", "final_g", "b_merge", "conv_w"]
    flat = lambda d: jnp.concatenate([d[nm].reshape(-1) for nm in small_names])[None, :]
    sd, sm, sv = _adamw("adamw_small", flat(weights), flat(grad), flat(ms), flat(vs))
    off = 0
    for nm in small_names:
        size = weights[nm].size
        for dst, src in ((delta, sd), (new_m, sm), (new_v, sv)):
            dst[nm] = src[0, off:off + size].reshape(weights[nm].shape)
        off += size
    for nm in order:
        if nm in small_names:
            continue
        shp = weights[nm].shape
        if nm in g_own:
            g2, d2, m2, v2 = _adamw_halves("adamw_" + nm, weights[nm][0], g_own[nm], g_sib[nm], ms[nm][0], vs[nm][0],
                                           c_idx)
            grad[nm] = g2.reshape(shp)
        else:
            d2, m2, v2 = _adamw("adamw_" + nm, weights[nm][0], grad[nm][0], ms[nm][0], vs[nm][0])
        delta[nm], new_m[nm], new_v[nm] = d2.reshape(shp), m2.reshape(shp), v2.reshape(shp)

    return (loss, grad_x[None], *[grad[nm] for nm in order], *[delta[nm] for nm in order],
            *[new_m[nm] for nm in order], *[new_v[nm] for nm in order])
```

```python
import functools

import jax
import jax.numpy as jnp
from jax import lax
from jax.experimental import pallas as pl
from jax.experimental.pallas import tpu as pltpu

F32 = jnp.float32
BF16 = jnp.bfloat16
MESH = pl.DeviceIdType.MESH

VMEM_LIMIT = 56 * 1024 * 1024
LANES = 128

D = 1024
D_FF = 2816
FF_SHARD = 1408
MIX_SHARD = 1280
MIX_W = 5120
HEAD_PAIRS = 4
HEAD_DIM = 64
CONV_W = 512
EPS = 1e-6
ATT_BLK = 256

ADAM_LR = 0.001
ADAM_B1 = 0.9
ADAM_B2 = 0.999
ADAM_EPS = 1e-08
ADAM_WD = 0.01
ADAM_STEP = 10


def _params(semantics=None):
    return pltpu.CompilerParams(dimension_semantics=semantics, vmem_limit_bytes=VMEM_LIMIT)


def _sigmoid(x):
    return 1.0 / (1.0 + jnp.exp(-x))


def _place():
    x, y, c = lax.axis_index("x"), lax.axis_index("y"), lax.axis_index("c")
    chips = [(1 - x, y), (x, 1 - y), (1 - x, 1 - y)]
    return x, y, c, chips


def _allgather_rows(name, blk):
    m_per, n = blk.shape

    def body(x_ref, out_ref, send_sems, recv_sems, local_sem):
        x, y, c, chips = _place()
        me, sibling = (x, y, c), (x, y, 1 - c)

        def rows(px, py, pc):
            return out_ref.at[pl.ds((4 * px + 2 * py + pc) * m_per, m_per), :]

        def copy(k, block, to, src=None):
            return pltpu.make_async_remote_copy(
                src_ref=rows(*block) if src is None else src, dst_ref=rows(*block),
                send_sem=send_sems.at[k], recv_sem=recv_sems.at[k], device_id=to, device_id_type=MESH)

        mine = pltpu.make_async_copy(x_ref, rows(*me), local_sem)
        mine.start()
        first = [copy(0, me, sibling, src=x_ref)]
        first += [copy(1 + j, me, (*chip, c), src=x_ref) for j, chip in enumerate(chips)]
        for cp in first:
            cp.start()
        passed = [copy(4 + j, (*chip, c), sibling) for j, chip in enumerate(chips)]
        for j, chip in enumerate(chips):
            copy(1 + j, (*chip, c), me).wait_recv()
            passed[j].start()
        copy(0, sibling, me).wait_recv()
        for j, chip in enumerate(chips):
            copy(4 + j, (*chip, 1 - c), me).wait_recv()
        for cp in first + passed:
            cp.wait_send()
        mine.wait()

    return pl.pallas_call(
        body, name=name,
        out_shape=jax.ShapeDtypeStruct((8 * m_per, n), blk.dtype),
        in_specs=[pl.BlockSpec(memory_space=pltpu.VMEM)],
        out_specs=pl.BlockSpec(memory_space=pltpu.VMEM),
        scratch_shapes=[pltpu.SemaphoreType.DMA((7,)), pltpu.SemaphoreType.DMA((7,)), pltpu.SemaphoreType.DMA],
        compiler_params=pltpu.CompilerParams(vmem_limit_bytes=VMEM_LIMIT),
    )(blk)


def _hbm_specs(n):
    return [pl.BlockSpec(memory_space=pltpu.HBM)] * n


def _allgather_weights(shards):
    n = len(shards)

    def body(*refs):
        start, relay, finish = _gather_protocol(refs[:n], refs[n:2 * n], *refs[2 * n:])
        start()
        relay()
        finish()

    gathered = pl.pallas_call(
        body, name="allgather_weights",
        out_shape=_gather_shapes(shards), in_specs=_hbm_specs(n), out_specs=_hbm_specs(n),
        scratch_shapes=_gather_sems(n),
    )(*shards)
    return _with_own_shard(gathered, shards)


def _gather_shapes(shards):
    return [jax.ShapeDtypeStruct((4, *s.shape), s.dtype) for s in shards]


def _gather_sems(n):
    return [pltpu.SemaphoreType.DMA((6 * n,)), pltpu.SemaphoreType.DMA((6 * n,))]


def _with_own_shard(gathered, shards):
    chip = 2 * lax.axis_index("x") + lax.axis_index("y")
    return [lax.dynamic_update_slice(g, s[None], (chip, 0, 0)) for g, s in zip(gathered, shards)]


def _gather_protocol(ins, outs, send_sems, recv_sems):
    n = len(ins)
    x, y, c, chips = _place()
    me, sibling = (x, y, c), (x, y, 1 - c)
    me_k = 2 * x + y

    def half(w, k, hc):
        h = ins[w].shape[0] // 2
        return outs[w].at[k, pl.ds(pl.multiple_of(hc * h, 8), h), :]

    def copy(w, j, k, hc, to, src=None):
        dst = half(w, k, hc)
        return pltpu.make_async_remote_copy(
            src_ref=dst if src is None else src, dst_ref=dst,
            send_sem=send_sems.at[6 * w + j], recv_sem=recv_sems.at[6 * w + j],
            device_id=to, device_id_type=MESH)

    def first(w, j):
        h = ins[w].shape[0] // 2
        src = ins[w].at[pl.ds(pl.multiple_of(c * h, 8), h), :]
        return copy(w, j, me_k, c, (*chips[j], c), src=src)

    def passed(w, j):
        px, py = chips[j]
        return copy(w, 3 + j, 2 * px + py, c, sibling)

    pairs = [(w, j) for w in range(n) for j in range(3)]

    def start():
        for w, j in pairs:
            first(w, j).start()

    def relay():
        for w, j in pairs:
            px, py = chips[j]
            copy(w, j, 2 * px + py, c, me).wait_recv()
            passed(w, j).start()

    def finish():
        for w, j in pairs:
            px, py = chips[j]
            copy(w, 3 + j, 2 * px + py, 1 - c, me).wait_recv()
        for w, j in pairs:
            first(w, j).wait_send()
            passed(w, j).wait_send()

    return start, relay, finish


def _sibling_swap_halves(name, grads):
    n = len(grads)

    def body(*refs):
        ins, outs = refs[:n], refs[n:2 * n]
        send_sems, recv_sems = refs[2 * n:]
        x, y, c, _ = _place()
        cps = []
        for w in range(n):
            cp = pltpu.make_async_remote_copy(
                src_ref=ins[w].at[:, 1 - c], dst_ref=outs[w],
                send_sem=send_sems.at[w], recv_sem=recv_sems.at[w],
                device_id=(x, y, 1 - c), device_id_type=MESH)
            cp.start()
            cps.append(cp)
        for cp in cps:
            cp.wait()

    return pl.pallas_call(
        body, name=name,
        out_shape=[jax.ShapeDtypeStruct((4, *g.shape[2:]), g.dtype) for g in grads],
        in_specs=_hbm_specs(n), out_specs=_hbm_specs(n),
        scratch_shapes=[pltpu.SemaphoreType.DMA((n,)), pltpu.SemaphoreType.DMA((n,))],
    )(*grads)


def _all_to_all_sems(n):
    return [pltpu.SemaphoreType.DMA((3 * n,)), pltpu.SemaphoreType.DMA((3 * n,))]


def _all_to_all_protocol(ins, outs, send_sems, recv_sems):
    n = len(ins)
    x, y, c, chips = _place()
    me_k = 2 * x + y
    pairs = [(w, j) for w in range(n) for j in range(3)]

    def sent(w, j):
        px, py = chips[j]
        return pltpu.make_async_remote_copy(
            src_ref=ins[w].at[2 * px + py], dst_ref=outs[w].at[me_k],
            send_sem=send_sems.at[3 * w + j], recv_sem=recv_sems.at[3 * w + j],
            device_id=(px, py, c), device_id_type=MESH)

    def start():
        for w, j in pairs:
            sent(w, j).start()

    def finish():
        for w, j in pairs:
            px, py = chips[j]
            slab = outs[w].at[2 * px + py]
            pltpu.make_async_remote_copy(
                src_ref=slab, dst_ref=slab, send_sem=send_sems.at[3 * w + j],
                recv_sem=recv_sems.at[3 * w + j], device_id=(px, py, c), device_id_type=MESH).wait_recv()
        for w, j in pairs:
            sent(w, j).wait_send()

    return start, finish


def _reduce8_sems(n):
    return [pltpu.SemaphoreType.DMA((7 * n,)), pltpu.SemaphoreType.DMA((7 * n,))]


def _reduce8_protocol(ins, outs, send_sems, recv_sems):
    n = len(ins)
    x, y, c, chips = _place()
    me_k = 2 * x + y
    far = [(w, j, hc) for w in range(n) for j in range(3) for hc in range(2)]

    def sent(w, j, hc):
        px, py = chips[j]
        return pltpu.make_async_remote_copy(
            src_ref=ins[w].at[2 * px + py, hc], dst_ref=outs[w].at[me_k, c],
            send_sem=send_sems.at[7 * w + 2 * j + hc], recv_sem=recv_sems.at[7 * w + 2 * j + c],
            device_id=(px, py, hc), device_id_type=MESH)

    def to_sibling(w):
        return pltpu.make_async_remote_copy(
            src_ref=ins[w].at[me_k, 1 - c], dst_ref=outs[w].at[me_k, c],
            send_sem=send_sems.at[7 * w + 6], recv_sem=recv_sems.at[7 * w + 6],
            device_id=(x, y, 1 - c), device_id_type=MESH)

    def arrival(w, slab, k):
        return pltpu.make_async_remote_copy(
            src_ref=slab, dst_ref=slab, send_sem=send_sems.at[7 * w + k], recv_sem=recv_sems.at[7 * w + k],
            device_id=(x, y, c), device_id_type=MESH)

    def start():
        for w in range(n):
            to_sibling(w).start()
        for w, j, hc in far:
            sent(w, j, hc).start()

    def finish():
        for w in range(n):
            arrival(w, outs[w].at[me_k, 1 - c], 6).wait_recv()
        for w, j, cc in far:
            px, py = chips[j]
            arrival(w, outs[w].at[2 * px + py, cc], 2 * j + cc).wait_recv()
        for w in range(n):
            to_sibling(w).wait_send()
        for w, j, hc in far:
            sent(w, j, hc).wait_send()

    return start, finish


def _sibling_share(halves):
    n = len(halves)

    def body(*refs):
        ins, outs = refs[:n], refs[n:2 * n]
        send_sems, recv_sems = refs[2 * n:]
        x, y, c, _ = _place()
        cps = []
        for w in range(n):
            cp = pltpu.make_async_remote_copy(
                src_ref=ins[w], dst_ref=outs[w], send_sem=send_sems.at[w], recv_sem=recv_sems.at[w],
                device_id=(x, y, 1 - c), device_id_type=MESH)
            cp.start()
            cps.append(cp)
        for cp in cps:
            cp.wait()

    return pl.pallas_call(
        body, name="grad_sibling_share",
        out_shape=[jax.ShapeDtypeStruct(p.shape, p.dtype) for p in halves],
        in_specs=_hbm_specs(n), out_specs=_hbm_specs(n),
        scratch_shapes=[pltpu.SemaphoreType.DMA((n,)), pltpu.SemaphoreType.DMA((n,))],
    )(*halves)


def _mm(name, grid, a, a_spec, b, b_spec, contract, out_shapes, out_specs, epilogue,
        extras=(), extra_specs=(), nk=1, acc_shape=None, semantics=None, a2a_parts=(), gather_shards=(),
        relay_at=None):
    assert not (a2a_parts and gather_shards)
    moved = tuple(a2a_parts) + tuple(gather_shards)
    ne, no, nc = len(extras), len(out_shapes), len(moved)
    nd = len(grid)

    def body(*refs):
        a_ref, b_ref = refs[0], refs[1]
        ex, outs = refs[2:2 + ne], refs[2 + ne + nc:2 + ne + nc + no]
        if nc:
            ids = [pl.program_id(d) for d in range(nd)]
            comm_refs = (refs[2 + ne:2 + ne + nc], refs[2 + ne + nc + no:2 + ne + 2 * nc + no], *refs[-2:])
            at_start = functools.reduce(jnp.logical_and, [i == 0 for i in ids])
            if a2a_parts:
                start, finish = _all_to_all_protocol(*comm_refs)
                pl.when(at_start)(start)
            else:
                start, relay, finish = _gather_protocol(*comm_refs)
                pl.when(at_start)(start)
                at_relay = grid[0] // 2 if relay_at is None else relay_at
                pl.when(functools.reduce(jnp.logical_and, [ids[0] == at_relay] + [i == 0 for i in ids[1:]]))(relay)

        def prod():
            return lax.dot_general(a_ref[...], b_ref[...], (((contract[0],), (contract[1],)), ((), ())),
                                   preferred_element_type=F32)

        if nk == 1:
            epilogue(prod(), ex, outs)
        else:
            acc = refs[2 + ne + 2 * nc + no]
            k = pl.program_id(nd - 1)

            @pl.when(k == 0)
            def _():
                acc[...] = prod()

            @pl.when(k > 0)
            def _():
                acc[...] += prod()

            @pl.when(k == nk - 1)
            def _():
                epilogue(acc[...], ex, outs)

        if nc:
            pl.when(functools.reduce(jnp.logical_and, [i == g - 1 for i, g in zip(ids, grid)]))(finish)

    if semantics is None or nc:
        semantics = ("arbitrary",) * nd
    return pl.pallas_call(
        body, name=name, grid=grid,
        in_specs=[a_spec, b_spec, *extra_specs] + _hbm_specs(nc),
        out_specs=list(out_specs) + _hbm_specs(nc),
        out_shape=list(out_shapes) + [jax.ShapeDtypeStruct(p.shape, p.dtype) for p in a2a_parts]
        + _gather_shapes(gather_shards),
        scratch_shapes=([] if nk == 1 else [pltpu.VMEM(acc_shape, F32)])
        + (_all_to_all_sems(nc) if a2a_parts else _gather_sems(nc) if gather_shards else []),
        compiler_params=_params(semantics),
    )(a, b, *extras, *moved)


def _store(dtype):
    def epilogue(acc, ex, outs):
        outs[0][...] = acc.astype(dtype)
    return epilogue


def _sds(shape, dtype):
    return jax.ShapeDtypeStruct(shape, dtype)


TR = 512


def _row_spec(width, tr=TR):
    return pl.BlockSpec((tr, width), lambda i: (i, 0))


def _const_spec(shape):
    nd = len(shape)
    return pl.BlockSpec(shape, lambda i: (0,) * nd)


def _norm_mod_fwd(name, h, p):
    S = h.shape[0]

    def body(h_ref, p_ref, u_ref):
        hv = h_ref[...]
        r = lax.rsqrt(jnp.mean(hv * hv, axis=-1, keepdims=True) + EPS)
        nrm = (hv * r) * p_ref[0:1, :]
        u_ref[...] = (nrm * (1.0 + p_ref[1:2, :]) + p_ref[2:3, :]).astype(BF16)

    return pl.pallas_call(
        body, name=name, grid=(S // TR,),
        in_specs=[_row_spec(D), _const_spec((8, D))], out_specs=_row_spec(D),
        out_shape=_sds((S, D), BF16), compiler_params=_params(("parallel",)),
    )(h, p)


def _rmsnorm_parts(hv):
    r = lax.rsqrt(jnp.mean(hv * hv, axis=-1, keepdims=True) + EPS)
    return r, hv * r


def _norm_bwd_tail(h, p, dh_res, prev=None):
    S = h.shape[0]
    row = pl.BlockSpec((TM, D), lambda i, s: (i, 0))
    const = pl.BlockSpec((8, D), lambda i, s: (0, 0))
    extras, specs = [h, p, dh_res], [row, const, row]
    out_shapes, out_specs = [_sds((S, D), F32)], [row]
    if prev is not None:
        extras += [prev[0], prev[1]]
        specs += [const, row]
        out_shapes.append(_sds((S, D), BF16))
        out_specs.append(row)
    out_shapes.append(_sds((8, D), F32))
    out_specs.append(const)

    def epilogue(duv, ex, outs):
        h_ref, p_ref, r_ref = ex[:3]
        dh_ref, sums_ref = outs[0], outs[-1]

        @pl.when(pl.program_id(0) == 0)
        def _():
            sums_ref[...] = jnp.zeros_like(sums_ref)

        g = p_ref[0:1, :]
        r, xn = _rmsnorm_parts(h_ref[...])
        dn = duv * (1.0 + p_ref[1:2, :])
        dxn = dn * g
        dh = r_ref[...] + r * (dxn - xn * jnp.mean(dxn * xn, axis=-1, keepdims=True))
        dh_ref[...] = dh
        sums_ref[0:1, :] += jnp.sum(duv, axis=0, keepdims=True)
        sums_ref[1:2, :] += jnp.sum(duv * (xn * g), axis=0, keepdims=True)
        sums_ref[2:3, :] += jnp.sum(dn * xn, axis=0, keepdims=True)
        if prev is not None:
            pp_ref, f_ref = ex[3:5]
            outs[1][...] = (prev[2] * pp_ref[3:4, :] * dh).astype(BF16)
            sums_ref[3:4, :] += prev[2] * jnp.sum(dh * f_ref[...].astype(F32), axis=0, keepdims=True)

    return dict(extras=tuple(extras), extra_specs=tuple(specs), out_shapes=out_shapes, out_specs=out_specs,
                epilogue=epilogue)


TM = 512


def _ffn_up(name, u, wgu4, shards=()):
    S = u.shape[0]
    n = len(shards)
    ni = S // TM

    def body(u_ref, wg_ref, wu_ref, *rest):
        gu_ref, hm_ref = rest[n:n + 2]
        s, i = pl.program_id(0), pl.program_id(1)
        if n:
            start, relay, finish = _gather_protocol(rest[:n], rest[n + 2:2 * n + 2], *rest[2 * n + 2:])
            pl.when((s == 0) & (i == 0))(start)
            pl.when((s == 1) & (i == 0))(relay)
        uv = u_ref[...]
        g = jnp.dot(uv, wg_ref[...], preferred_element_type=F32)
        up = jnp.dot(uv, wu_ref[...], preferred_element_type=F32)
        gu_ref[0] = g.astype(BF16)
        gu_ref[1] = up.astype(BF16)
        hm_ref[...] = (g * _sigmoid(g) * up).astype(BF16)
        if n:
            pl.when((s == 1) & (i == ni - 1))(finish)

    gu, hm, *gathered = pl.pallas_call(
        body, name=name, grid=(2, ni),
        in_specs=[pl.BlockSpec((TM, D), lambda s, i: (i, 0)),
                  pl.BlockSpec((None, D, FF_SHARD), lambda s, i: (s, 0, 0)),
                  pl.BlockSpec((None, D, FF_SHARD), lambda s, i: (s + 2, 0, 0))] + _hbm_specs(n),
        out_specs=[pl.BlockSpec((2, TM, FF_SHARD), lambda s, i: (0, i, s)),
                   pl.BlockSpec((TM, FF_SHARD), lambda s, i: (i, s))] + _hbm_specs(n),
        out_shape=[_sds((2, S, D_FF), BF16), _sds((S, D_FF), BF16)] + _gather_shapes(shards),
        scratch_shapes=_gather_sems(n) if n else [],
        compiler_params=_params(("arbitrary", "arbitrary") if n else ("parallel", "parallel")),
    )(u, wgu4, wgu4, *shards)
    return gu, hm, _with_own_shard(gathered, shards)


def _proj_residual(name, a, w, h, p, weight, p_next, gather_shards=()):
    S, K = a.shape

    def epilogue(acc, ex, outs):
        h_ref, p_ref, pn_ref = ex
        outs[0][...] = acc.astype(BF16)
        hout = h_ref[...] + weight * p_ref[3:4, :] * acc
        outs[1][...] = hout
        _, xn = _rmsnorm_parts(hout)
        outs[2][...] = ((xn * pn_ref[0:1, :]) * (1.0 + pn_ref[1:2, :]) + pn_ref[2:3, :]).astype(BF16)

    row = _row_spec(D, TM)
    res = _mm(
        name, (S // TM,), a, pl.BlockSpec((TM, K), lambda i: (i, 0)), w, pl.BlockSpec((K, D), lambda i: (0, 0)),
        (1, 0), [_sds((S, D), BF16), _sds((S, D), F32), _sds((S, D), BF16)], [row, row, row], epilogue,
        extras=(h, p, p_next), extra_specs=(row, _const_spec((8, D)), _const_spec((8, D))),
        semantics=("parallel",), gather_shards=gather_shards, relay_at=S // TM - 1)
    return res[0], res[1], res[2], _with_own_shard(res[3:], gather_shards)


def _proj_residual_loss(name, a, w, h, p, weight, gf, target):
    S, K = a.shape

    def epilogue(acc, ex, outs):
        h_ref, p_ref, g_ref, t_ref = ex
        dh_ref, df_ref, sums_ref, loss_ref = outs

        @pl.when(pl.program_id(0) == 0)
        def _():
            sums_ref[...] = jnp.zeros_like(sums_ref)
            loss_ref[...] = jnp.zeros_like(loss_ref)

        gate = p_ref[3:4, :]
        g = g_ref[0:1, :]
        r, xn = _rmsnorm_parts(h_ref[...] + weight * gate * acc)
        err = xn * g - t_ref[...]
        loss_ref[...] += 0.5 * jnp.sum(err * err) * (1.0 / D)
        dout = err * (1.0 / D)
        dxn = dout * g
        dh = r * (dxn - xn * jnp.mean(dxn * xn, axis=-1, keepdims=True))
        dh_ref[...] = dh
        df_ref[...] = (weight * gate * dh).astype(BF16)
        sums_ref[0:1, :] += jnp.sum(dout * xn, axis=0, keepdims=True)
        sums_ref[1:2, :] += weight * jnp.sum(dh * acc, axis=0, keepdims=True)

    row = _row_spec(D, TM)
    return _mm(
        name, (S // TM,), a, pl.BlockSpec((TM, K), lambda i: (i, 0)), w, pl.BlockSpec((K, D), lambda i: (0, 0)),
        (1, 0), [_sds((S, D), F32), _sds((S, D), BF16), _sds((8, D), F32), _sds((8, LANES), F32)],
        [row, row, _const_spec((8, D)), _const_spec((8, LANES))], epilogue,
        extras=(h, p, gf, target), extra_specs=(row, _const_spec((8, D)), _const_spec((8, D)), row),
        semantics=("arbitrary",))


def _ffn_down_bwd(name, df, wd, gu, a2a_parts=()):
    S = df.shape[0]

    def epilogue(acc, ex, outs):
        g = ex[0][0].astype(F32)
        up = ex[0][1].astype(F32)
        sg = _sigmoid(g)
        outs[0][0] = (acc * up * (sg * (1.0 + g * (1.0 - sg)))).astype(BF16)
        outs[0][1] = (acc * g * sg).astype(BF16)

    gu_spec = pl.BlockSpec((2, TM, FF_SHARD), lambda n, i: (0, i, n))
    return _mm(
        name, (2, S // TM), df, pl.BlockSpec((TM, D), lambda n, i: (i, 0)),
        wd, pl.BlockSpec((FF_SHARD, D), lambda n, i: (n, 0)), (1, 1),
        [_sds((2, S, D_FF), BF16)], [gu_spec], epilogue, extras=(gu,), extra_specs=(gu_spec,),
        semantics=("parallel", "parallel"), a2a_parts=a2a_parts)


TK = 1024


def _grad_w(name, a, a_w, b, b_w, b_map, n_out, out_shape, out_block, out_map, a2a_parts=()):
    S = a.shape[0]
    nk = S // TK
    res = _mm(
        name, (n_out, nk), a, pl.BlockSpec((TK, a_w), lambda s, k: (k, 0)), b, pl.BlockSpec(
            (None, TK, b_w) if b.ndim == 3 else (TK, b_w), b_map), (0, 0),
        [_sds(out_shape, BF16)], [pl.BlockSpec(out_block, out_map)], _store(BF16), nk=nk, acc_shape=(a_w, b_w),
        semantics=("parallel", "arbitrary"), a2a_parts=a2a_parts)
    return res if a2a_parts else res[0]


def _ffn_dw_down(name, hm, df):
    S = df.shape[0]
    return _mm(
        name, (2, S // TK), hm, pl.BlockSpec((TK, FF_SHARD), lambda m, k: (k, m)),
        df, pl.BlockSpec((TK, D), lambda m, k: (k, 0)), (0, 0),
        [_sds((D_FF, D), BF16)], [pl.BlockSpec((FF_SHARD, D), lambda m, k: (m, 0))], _store(BF16),
        nk=S // TK, acc_shape=(FF_SHARD, D), semantics=("parallel", "arbitrary"))[0]


def _ffn_up_bwd(name, dgu, wgu4, tail, a2a_parts=()):
    S = dgu.shape[1]
    return _mm(
        name, (S // TM, 4), dgu, pl.BlockSpec((None, TM, FF_SHARD), lambda i, s: (s // 2, i, s % 2)),
        wgu4, pl.BlockSpec((None, D, FF_SHARD), lambda i, s: (s, 0, 0)), (1, 1),
        tail["out_shapes"], tail["out_specs"], tail["epilogue"], extras=tail["extras"],
        extra_specs=tail["extra_specs"], nk=4, acc_shape=(TM, D), semantics=("arbitrary", "arbitrary"),
        a2a_parts=a2a_parts)


def _ffn_dw_gu(name, u_in, dgu, a2a_parts=()):
    return _grad_w(name, u_in, D, dgu, FF_SHARD, lambda s, k: (s // 2, k, s % 2), 4,
                   (4, D, FF_SHARD), (None, D, FF_SHARD), lambda s, k: (s, 0, 0), a2a_parts=a2a_parts)


def _ffn_bwd(tag, df, u_in, gu, hm, wgu4, wd, tail):
    dgu = _ffn_down_bwd(tag + "_down_bwd", df, wd, gu)[0]
    dwd = _ffn_dw_down(tag + "_dw_down", hm, df)
    res = _ffn_up_bwd(tag + "_up_bwd", dgu, wgu4, tail)
    dwgu = _ffn_dw_gu(tag + "_dw_gu", u_in, dgu)
    return res, dwgu, dwd


def _shift_down(v, k, row):
    return jnp.where(row >= k, pltpu.roll(v, k, axis=0), 0.0)


def _shift_up(v, k, row, S):
    return jnp.where(row < S - k, pltpu.roll(v, S - k, axis=0), 0.0)


def _conv_specs(S):
    cols = CONV_W // LANES
    return [pl.BlockSpec((S, LANES), functools.partial(lambda j, off: (0, off + j), off=o * cols))
            for o in range(3)]


def _conv_fwd(proj, conv_w):
    S = proj.shape[0]

    def body(cb_ref, cc_ref, cx_ref, w_ref, sc_ref):
        row = lax.broadcasted_iota(jnp.int32, (S, LANES), 0)
        v = cc_ref[...].astype(F32) * cx_ref[...].astype(F32)
        yv = w_ref[0:1, :] * _shift_down(v, 2, row) + w_ref[1:2, :] * _shift_down(v, 1, row) + w_ref[2:3, :] * v
        sc_ref[...] = (cb_ref[...].astype(F32) * yv).astype(BF16)

    return pl.pallas_call(
        body, name="conv_fwd", grid=(CONV_W // LANES,),
        in_specs=_conv_specs(S) + [pl.BlockSpec((3, LANES), lambda j: (0, j))],
        out_specs=pl.BlockSpec((S, LANES), lambda j: (0, j)), out_shape=_sds((S, CONV_W), BF16),
        compiler_params=_params(("parallel",)),
    )(proj, proj, proj, conv_w)


def _conv_bwd(dsc, proj, conv_w):
    S = proj.shape[0]

    def body(d_ref, cb_ref, cc_ref, cx_ref, w_ref, dcb_ref, dcc_ref, dcx_ref, dw_ref):
        row = lax.broadcasted_iota(jnp.int32, (S, LANES), 0)
        cc = cc_ref[...].astype(F32)
        cx = cx_ref[...].astype(F32)
        d = d_ref[...].astype(F32)
        v = cc * cx
        v1 = _shift_down(v, 1, row)
        v2 = _shift_down(v, 2, row)
        w0, w1, w2 = w_ref[0:1, :], w_ref[1:2, :], w_ref[2:3, :]
        dcb_ref[...] = (d * (w0 * v2 + w1 * v1 + w2 * v)).astype(BF16)
        dy = d * cb_ref[...].astype(F32)
        dw_ref[0:1, :] = jnp.sum(dy * v2, axis=0, keepdims=True)
        dw_ref[1:2, :] = jnp.sum(dy * v1, axis=0, keepdims=True)
        dw_ref[2:3, :] = jnp.sum(dy * v, axis=0, keepdims=True)
        dv = w2 * dy + w1 * _shift_up(dy, 1, row, S) + w0 * _shift_up(dy, 2, row, S)
        dcc_ref[...] = (dv * cx).astype(BF16)
        dcx_ref[...] = (dv * cc).astype(BF16)

    col = pl.BlockSpec((S, LANES), lambda j: (0, j))
    return pl.pallas_call(
        body, name="conv_bwd", grid=(CONV_W // LANES,),
        in_specs=[col] + _conv_specs(S) + [pl.BlockSpec((3, LANES), lambda j: (0, j))],
        out_specs=[col, col, col, pl.BlockSpec((3, LANES), lambda j: (0, j))],
        out_shape=[_sds((S, CONV_W), BF16)] * 3 + [_sds((3, CONV_W), F32)],
        compiler_params=_params(("parallel",)),
    )(dsc, proj, proj, proj, conv_w)


Q_COL, K_COL, V_COL = 1536 // LANES, 2048 // LANES, 2560 // LANES


def _split_dot(x, tri):
    hi = x.astype(BF16)
    lo = (x - hi.astype(F32)).astype(BF16)
    return jnp.dot(hi, tri, preferred_element_type=F32) + jnp.dot(lo, tri, preferred_element_type=F32)


def _tri_dot(tri, x):
    hi = x.astype(BF16)
    lo = (x - hi.astype(F32)).astype(BF16)
    return jnp.dot(tri, hi, preferred_element_type=F32) + jnp.dot(tri, lo, preferred_element_type=F32)


def _softplus(z):
    return jnp.maximum(z, 0.0) + jnp.log(1.0 + jnp.exp(-jnp.abs(z)))


def _nt(a, b):
    return lax.dot_general(a, b, (((1,), (1,)), ((), ())), preferred_element_type=F32)


def _tn(a, b):
    return lax.dot_general(a, b, (((0,), (0,)), ((), ())), preferred_element_type=F32)


def _interleave(gens, delays):
    results = [None] * len(gens)
    live = list(range(len(gens)))
    rnd = 0
    while live:
        for g in list(live):
            if rnd < delays[g]:
                continue
            try:
                next(gens[g])
            except StopIteration as stop:
                results[g] = stop.value
                live.remove(g)
        rnd += 1
    return results


def _attn_fwd(proj, shards):
    S = proj.shape[0]
    B = ATT_BLK
    nq = S // B
    n = len(shards)

    def body(q_ref, k_ref, v_ref, *rest):
        o_ref, t_ref = rest[n:n + 2]
        start, relay, finish = _gather_protocol(rest[:n], rest[n + 2:2 * n + 2], *rest[2 * n + 2:])
        p = pl.program_id(0)
        i = pl.program_id(1)
        pl.when((p == 0) & (i == 0))(start)
        pl.when((p == HEAD_PAIRS // 2) & (i == 0))(relay)
        lo_lane = lax.broadcasted_iota(jnp.int32, (B, LANES), 1) < HEAD_DIM
        row = lax.broadcasted_iota(jnp.int32, (B, B), 0)
        col = lax.broadcasted_iota(jnp.int32, (B, B), 1)
        after = (row > col).astype(BF16)
        causal = col < row
        q2 = q_ref[...] * 0.125
        zero = jnp.zeros((), BF16)
        q_heads = (jnp.where(lo_lane, q2, zero), jnp.where(lo_lane, zero, q2))

        def head_tile(q_h, st, kb, diag):
            k2 = k_ref[pl.ds(pl.multiple_of(kb * B, B), B), :]
            z = _nt(q_h, k2)
            yield
            spz = _softplus(z)
            sp = jnp.where(causal, spz, 0.0) if diag else spz
            hi = sp.astype(BF16)
            lo = (sp - hi.astype(F32)).astype(BF16)
            r = st["r"]
            st["r"] = r + jnp.sum(sp, axis=1, keepdims=True)
            yield
            rem = jnp.dot(hi, after, preferred_element_type=F32) + jnp.dot(lo, after, preferred_element_type=F32)
            yield
            a = jnp.exp(z - spz - (rem + r))
            if diag:
                a = jnp.where(causal, a, 0.0)
            ab = a.astype(BF16)
            yield
            v2 = v_ref[pl.ds(pl.multiple_of(kb * B, B), B), :]
            st["acc"] = st["acc"] + jnp.dot(ab, v2, preferred_element_type=F32)

        def tiles(kbs, carry, diags=(False, False)):
            sts = [dict(r=carry[0], acc=carry[1]), dict(r=carry[2], acc=carry[3])]
            gens = [head_tile(q_h, st, kb, dg) for kb, dg in zip(kbs, diags) for q_h, st in zip(q_heads, sts)]
            _interleave(gens, [t for t in range(len(kbs)) for _ in q_heads])
            return sts[0]["r"], sts[0]["acc"], sts[1]["r"], sts[1]["acc"]

        zr, za = jnp.zeros((B, 1), F32), jnp.zeros((B, LANES), F32)
        carry = lax.fori_loop(0, i % 2, lambda j, cr: tiles([i, i - 1], cr, (True, False)), (zr, za, zr, za))
        carry = lax.fori_loop(0, 1 - i % 2, lambda j, cr: tiles([i], cr, (True,)), carry)
        first = i - 1 - i % 2
        ra, acc_a, rb, acc_b = lax.fori_loop(
            0, i // 2, lambda j, cr: tiles([first - 2 * j, first - 2 * j - 1], cr), carry)
        o_ref[...] = jnp.where(lo_lane, acc_a, acc_b).astype(BF16)
        t_ref[...] = jnp.where(lo_lane, ra, rb).T
        pl.when((p == HEAD_PAIRS - 1) & (i == nq - 1))(finish)

    seq = lambda off: pl.BlockSpec((S, LANES), lambda p, i: (0, off + p))
    blk = pl.BlockSpec((B, LANES), lambda p, i: (i, p))
    o, t, *gathered = pl.pallas_call(
        body, name="attn_fwd", grid=(HEAD_PAIRS, nq),
        in_specs=[pl.BlockSpec((B, LANES), lambda p, i: (i, Q_COL + p)), seq(K_COL), seq(V_COL)] + _hbm_specs(n),
        out_specs=[blk, pl.BlockSpec((LANES, B), lambda p, i: (p, i))] + _hbm_specs(n),
        out_shape=[_sds((S, 512), BF16), _sds((512, S), F32)] + _gather_shapes(shards),
        scratch_shapes=_gather_sems(n),
        compiler_params=_params(("arbitrary", "arbitrary")),
    )(proj, proj, proj, *shards)
    return o, t, _with_own_shard(gathered, shards)


def _attn_bwd(proj, t, do, parts):
    S = proj.shape[0]
    kt = proj[:, K_COL * LANES:V_COL * LANES].T
    B = ATT_BLK
    nq = S // B
    n = len(parts)

    def body(q_ref, k_ref, v_ref, kt_ref, t_ref, do_ref, *rest):
        dq_ref, dk_ref, dv_ref = rest[n:n + 3]
        dk_acc, dv_acc = rest[2 * n + 3:2 * n + 5]
        start, finish = _reduce8_protocol(rest[:n], rest[n + 3:2 * n + 3], *rest[2 * n + 5:])
        i = pl.program_id(1)
        pl.when((pl.program_id(0) == 0) & (i == 0))(start)

        @pl.when(i == 0)
        def _():
            dk_acc[...] = jnp.zeros_like(dk_acc)
            dv_acc[...] = jnp.zeros_like(dv_acc)

        lo_lane = lax.broadcasted_iota(jnp.int32, (B, LANES), 1) < HEAD_DIM
        key = lax.broadcasted_iota(jnp.int32, (B, B), 0)
        qry = lax.broadcasted_iota(jnp.int32, (B, B), 1)
        upto = (qry <= key).astype(BF16)
        before = (qry < key).astype(BF16)
        causal = key < qry
        zero = jnp.zeros((), BF16)
        q2 = q_ref[...] * 0.125
        do2 = do_ref[...]
        heads = ((jnp.where(lo_lane, q2, zero), jnp.where(lo_lane, do2, zero), t_ref[0:1, :]),
                 (jnp.where(lo_lane, zero, q2), jnp.where(lo_lane, zero, do2), t_ref[HEAD_DIM:HEAD_DIM + 1, :]))

        def head_tile(head, st, kb, diag):
            q_h, do_h, t_h = head
            rows = pl.ds(pl.multiple_of(kb * B, B), B)
            z = _nt(k_ref[rows, :], q_h)
            da = _nt(v_ref[rows, :], do_h)
            yield
            spz = _softplus(z)
            sp = jnp.where(causal, spz, 0.0) if diag else spz
            hi = sp.astype(BF16)
            lo = (sp - hi.astype(F32)).astype(BF16)
            pc = st["pc"]
            st["pc"] = pc + jnp.sum(sp, axis=0, keepdims=True)
            yield
            pref = jnp.dot(upto, hi, preferred_element_type=F32) + jnp.dot(upto, lo, preferred_element_type=F32)
            yield
            a = jnp.exp(z - spz - ((t_h - pc) - pref))
            if diag:
                a = jnp.where(causal, a, 0.0)
            e = a * da
            eb = e.astype(BF16)
            ab = a.astype(BF16)
            ec = st["ec"]
            st["ec"] = ec + jnp.sum(e, axis=0, keepdims=True)
            yield
            e_before = ec + jnp.dot(before, eb, preferred_element_type=F32)
            yield
            u = jnp.exp(-spz)
            dz = u * (e + e_before) - e_before
            if diag:
                dz = jnp.where(causal, dz, 0.0)
            dzb = dz.astype(BF16)
            yield
            st["dqt"] = st["dqt"] + jnp.dot(kt_ref[:, rows], dzb, preferred_element_type=F32)
            return (jnp.dot(dzb, q_h, preferred_element_type=F32), jnp.dot(ab, do_h, preferred_element_type=F32))

        def tiles(kbs, carry, diags=(False, False)):
            sts = [dict(pc=carry[3 * h], ec=carry[3 * h + 1], dqt=carry[3 * h + 2]) for h in range(2)]
            gens = [head_tile(hd, st, kb, dg) for kb, dg in zip(kbs, diags) for hd, st in zip(heads, sts)]
            res = _interleave(gens, [t for t in range(len(kbs)) for _ in heads])
            for t, kb in enumerate(kbs):
                rows = pl.ds(pl.multiple_of(kb * B, B), B)
                (dk_a, dv_a), (dk_b, dv_b) = res[2 * t], res[2 * t + 1]
                dk_acc[rows, :] += dk_a + dk_b
                dv_acc[rows, :] += dv_a + dv_b
            return tuple(st[nm] for st in sts for nm in ("pc", "ec", "dqt"))

        zc, zq = jnp.zeros((1, B), F32), jnp.zeros((LANES, B), F32)
        carry = lax.fori_loop(0, i // 2, lambda j, cr: tiles([2 * j, 2 * j + 1], cr), (zc, zc, zq, zc, zc, zq))
        carry = lax.fori_loop(0, i % 2, lambda j, cr: tiles([i - 1, i], cr, (False, True)), carry)
        _, _, dqt_a, _, _, dqt_b = lax.fori_loop(0, 1 - i % 2, lambda j, cr: tiles([i], cr, (True,)), carry)
        head0 = lax.broadcasted_iota(jnp.int32, (LANES, B), 0) < HEAD_DIM
        dq_ref[...] = (jnp.where(head0, dqt_a, dqt_b).T * 0.125).astype(BF16)

        @pl.when(i == nq - 1)
        def _():
            dk_ref[...] = dk_acc[...].astype(BF16)
            dv_ref[...] = dv_acc[...].astype(BF16)

        pl.when((pl.program_id(0) == HEAD_PAIRS - 1) & (i == nq - 1))(finish)

    seq = lambda off: pl.BlockSpec((S, LANES), lambda p, i: (0, off + p))
    blk = pl.BlockSpec((B, LANES), lambda p, i: (i, p))
    whole = pl.BlockSpec((S, LANES), lambda p, i: (0, p))
    dq, dk, dv, *came = pl.pallas_call(
        body, name="attn_bwd", grid=(HEAD_PAIRS, nq),
        in_specs=[pl.BlockSpec((B, LANES), lambda p, i: (i, Q_COL + p)), seq(K_COL), seq(V_COL),
                  pl.BlockSpec((LANES, S), lambda p, i: (p, 0)), pl.BlockSpec((LANES, B), lambda p, i: (p, i)), blk]
        + _hbm_specs(n),
        out_specs=[blk, whole, whole] + _hbm_specs(n),
        out_shape=[_sds((S, 512), BF16)] * 3 + [jax.ShapeDtypeStruct(p.shape, p.dtype) for p in parts],
        scratch_shapes=[pltpu.VMEM((S, LANES), F32), pltpu.VMEM((S, LANES), F32)] + _reduce8_sems(n),
        compiler_params=_params(("arbitrary", "arbitrary")),
    )(proj, proj, proj, kt, t, do, *parts)
    return dq, dk, dv, came


GA_COL, GB_COL = 3072 // D, 4096 // D


def _merge_fwd(sc, o, wco4, wao4, proj, bm):
    S = sc.shape[0]

    def body(sc_ref, o_ref, wc_ref, wa_ref, ga_ref, gb_ref, bm_ref, ya_ref, yb_ref, mg_ref):
        scv, ov = sc_ref[...], o_ref[...]
        for s in range(4):
            cols = slice(s * 256, (s + 1) * 256)
            ya = jnp.dot(scv, wc_ref[s], preferred_element_type=F32)
            yb = jnp.dot(ov, wa_ref[s], preferred_element_type=F32)
            sa = _sigmoid(ga_ref[:, cols].astype(F32) + bm_ref[0:1, cols])
            sb = _sigmoid(gb_ref[:, cols].astype(F32) + bm_ref[1:2, cols])
            ya_ref[:, cols] = ya.astype(BF16)
            yb_ref[:, cols] = yb.astype(BF16)
            mg_ref[:, cols] = (sa * ya + sb * yb).astype(BF16)

    wide = pl.BlockSpec((TM, 512), lambda i: (i, 0))
    wsp = pl.BlockSpec((4, 512, 256), lambda i: (0, 0, 0))
    out = pl.BlockSpec((TM, D), lambda i: (i, 0))
    return pl.pallas_call(
        body, name="merge_fwd", grid=(S // TM,),
        in_specs=[wide, wide, wsp, wsp, pl.BlockSpec((TM, D), lambda i: (i, GA_COL)),
                  pl.BlockSpec((TM, D), lambda i: (i, GB_COL)), pl.BlockSpec((2, D), lambda i: (0, 0))],
        out_specs=[out, out, out], out_shape=[_sds((S, D), BF16)] * 3,
        compiler_params=_params(("parallel",)),
    )(sc, o, wco4, wao4, proj, proj, bm)


def _merge_bwd(dy2, wout, ya, yb, proj, bm):
    S = dy2.shape[0]

    def epilogue(acc, ex, outs):
        ya_ref, yb_ref, ga_ref, gb_ref, bm_ref = ex
        i = pl.program_id(0)
        sa = _sigmoid(ga_ref[...].astype(F32) + bm_ref[0:1, :])
        sb = _sigmoid(gb_ref[...].astype(F32) + bm_ref[1:2, :])
        dga = acc * ya_ref[...].astype(F32) * (sa * (1.0 - sa))
        dgb = acc * yb_ref[...].astype(F32) * (sb * (1.0 - sb))
        outs[0][...] = (acc * sa).astype(BF16)
        outs[1][...] = (acc * sb).astype(BF16)
        outs[2][...] = dga.astype(BF16)
        outs[3][...] = dgb.astype(BF16)

        @pl.when(i == 0)
        def _():
            outs[4][...] = jnp.zeros_like(outs[4])

        outs[4][0:1, :] += jnp.sum(dga, axis=0, keepdims=True)
        outs[4][1:2, :] += jnp.sum(dgb, axis=0, keepdims=True)

    tm = TM // 2
    out = pl.BlockSpec((tm, D), lambda i: (i, 0))
    return _mm(
        "merge_bwd", (S // tm,), dy2, out, wout, pl.BlockSpec((D, D), lambda i: (0, 0)), (1, 1),
        [_sds((S, D), BF16)] * 4 + [_sds((8, D), F32)], [out, out, out, out, _const_spec((8, D))],
        epilogue, extras=(ya, yb, proj, proj, bm),
        extra_specs=(out, out, pl.BlockSpec((tm, D), lambda i: (i, GA_COL)),
                     pl.BlockSpec((tm, D), lambda i: (i, GB_COL)), _const_spec((2, D))),
        semantics=("arbitrary",))


def _mixer_out_bwd(dya, dyb, wco4, wao4):
    S = dya.shape[0]

    def body(da_ref, db_ref, wc_ref, wa_ref, dsc_ref, do_ref):
        for d_ref, w_ref, o_ref in ((da_ref, wc_ref, dsc_ref), (db_ref, wa_ref, do_ref)):
            acc = _nt(d_ref[:, 0:256], w_ref[0])
            for s in range(1, 4):
                acc = acc + _nt(d_ref[:, s * 256:(s + 1) * 256], w_ref[s])
            o_ref[...] = acc.astype(BF16)

    wide = pl.BlockSpec((TM, D), lambda i: (i, 0))
    wsp = pl.BlockSpec((4, 512, 256), lambda i: (0, 0, 0))
    out = pl.BlockSpec((TM, 512), lambda i: (i, 0))
    return pl.pallas_call(
        body, name="mixer_out_bwd", grid=(S // TM,), in_specs=[wide, wide, wsp, wsp], out_specs=[out, out],
        out_shape=[_sds((S, 512), BF16)] * 2, compiler_params=_params(("parallel",)),
    )(dya, dyb, wco4, wao4)


def _mixer_out_dw(sc, o, dya, dyb):
    S = sc.shape[0]
    nk = S // TK

    def body(sc_ref, o_ref, da_ref, db_ref, dwc_ref, dwa_ref, acc_c, acc_a):
        k = pl.program_id(0)

        @pl.when(k == 0)
        def _():
            acc_c[...] = jnp.zeros_like(acc_c)
            acc_a[...] = jnp.zeros_like(acc_a)

        acc_c[...] += _tn(sc_ref[...], da_ref[...])
        acc_a[...] += _tn(o_ref[...], db_ref[...])

        @pl.when(k == nk - 1)
        def _():
            for s in range(4):
                dwc_ref[s] = acc_c[:, s * 256:(s + 1) * 256].astype(BF16)
                dwa_ref[s] = acc_a[:, s * 256:(s + 1) * 256].astype(BF16)

    narrow = pl.BlockSpec((TK, 512), lambda k: (k, 0))
    wide = pl.BlockSpec((TK, D), lambda k: (k, 0))
    out = pl.BlockSpec((4, 512, 256), lambda k: (0, 0, 0))
    return pl.pallas_call(
        body, name="mixer_out_dw", grid=(nk,), in_specs=[narrow, narrow, wide, wide], out_specs=[out, out],
        out_shape=[_sds((4, 512, 256), BF16)] * 2,
        scratch_shapes=[pltpu.VMEM((512, D), F32), pltpu.VMEM((512, D), F32)],
        compiler_params=_params(("arbitrary",)),
    )(sc, o, dya, dyb)


ADA_SHARD = 2304
ADA_TN = 768


def _ada_fwd(c_all, w_ada_l, b_l):
    def body(c_ref, w_ref, b_ref, o_ref):
        cv = c_ref[...]
        ca = cv * _sigmoid(cv)
        o_ref[...] = jnp.dot(ca.astype(BF16), w_ref[...].astype(BF16), preferred_element_type=F32) + b_ref[...]

    return pl.pallas_call(
        body, name="ada_fwd", grid=(ADA_SHARD // ADA_TN,),
        in_specs=[pl.BlockSpec((8, D), lambda j: (0, 0)), pl.BlockSpec((D, ADA_TN), lambda j: (0, j)),
                  pl.BlockSpec((1, ADA_TN), lambda j: (0, j))],
        out_specs=pl.BlockSpec((8, ADA_TN), lambda j: (0, j)), out_shape=_sds((8, ADA_SHARD), F32),
        compiler_params=_params(("parallel",)),
    )(c_all, w_ada_l, b_l)


def _ada_bwd(c_all_t, dmod_l):
    def body(c_ref, d_ref, o_ref):
        cv = c_ref[...]
        ca = cv * _sigmoid(cv)
        o_ref[...] = jnp.dot(ca.astype(BF16).astype(F32), d_ref[...].astype(BF16).astype(F32),
                             preferred_element_type=F32, precision=lax.Precision.HIGHEST)

    return pl.pallas_call(
        body, name="ada_bwd", grid=(ADA_SHARD // ADA_TN,),
        in_specs=[pl.BlockSpec((D, 8), lambda j: (0, 0)), pl.BlockSpec((8, ADA_TN), lambda j: (0, j))],
        out_specs=pl.BlockSpec((D, ADA_TN), lambda j: (0, j)), out_shape=_sds((D, ADA_SHARD), F32),
        compiler_params=_params(("parallel",)),
    )(c_all_t, dmod_l)


def _sum_rows(name, x):
    n = x.shape[1]

    def body(x_ref, o_ref):
        s = x_ref[0:1, :]
        for d in range(1, 8):
            s = s + x_ref[d:d + 1, :]
        o_ref[...] = s

    return pl.pallas_call(
        body, name=name, in_specs=[pl.BlockSpec(memory_space=pltpu.VMEM)],
        out_specs=pl.BlockSpec(memory_space=pltpu.VMEM), out_shape=_sds((1, n), F32),
        compiler_params=pltpu.CompilerParams(vmem_limit_bytes=VMEM_LIMIT),
    )(x)


def _pair_sum(name, g4, recv, c_idx):
    _, _, h, C = g4.shape
    tr = h if h <= 512 else h // (h // 256) if h % 256 == 0 else h // 2

    def body(c_ref, g_ref, r_ref, o_ref):
        o_ref[...] = (g_ref[...].astype(F32) + r_ref[...].astype(F32)).astype(BF16)

    grid_spec = pltpu.PrefetchScalarGridSpec(
        num_scalar_prefetch=1, grid=(4, h // tr),
        in_specs=[pl.BlockSpec((None, None, tr, C), lambda k, i, c: (k, c[0], i, 0)),
                  pl.BlockSpec((None, tr, C), lambda k, i, c: (k, i, 0))],
        out_specs=pl.BlockSpec((None, tr, C), lambda k, i, c: (k, i, 0)))
    return pl.pallas_call(
        body, name=name, grid_spec=grid_spec, out_shape=_sds((4, h, C), BF16),
        compiler_params=_params(("parallel", "parallel")),
    )(c_idx, g4, recv)


def _sum4(name, q, p, chip_idx):
    _, h, C = q.shape
    tr = h if h <= 512 else h // (h // 256) if h % 256 == 0 else h // 2

    def body(k_ref, q_ref, p_ref, o_ref):
        me = k_ref[0]
        terms = [jnp.where(me == k, p_ref[...], q_ref[k]).astype(F32) for k in range(4)]
        o_ref[...] = ((terms[0] + terms[1]) + terms[2]) + terms[3]

    grid_spec = pltpu.PrefetchScalarGridSpec(
        num_scalar_prefetch=1, grid=(h // tr,),
        in_specs=[pl.BlockSpec((4, tr, C), lambda i, k: (0, i, 0)),
                  pl.BlockSpec((None, tr, C), lambda i, k: (k[0], i, 0))],
        out_specs=pl.BlockSpec((tr, C), lambda i, k: (i, 0)))
    return pl.pallas_call(
        body, name=name, grid_spec=grid_spec, out_shape=_sds((h, C), F32), compiler_params=_params(("parallel",)),
    )(chip_idx, q, p)


def _sum8(name, q, p, idx):
    _, _, h, C = q.shape
    tr = h if h <= 256 else 256 if h % 256 == 0 else h // 2

    def body(k_ref, q_ref, p_ref, o_ref):
        total = None
        for j in range(4):
            for cc in range(2):
                mine = (k_ref[0] == j) & (k_ref[1] == cc)
                term = jnp.where(mine, p_ref[...], q_ref[j, cc]).astype(F32)
                total = term if total is None else total + term
        o_ref[...] = total

    grid_spec = pltpu.PrefetchScalarGridSpec(
        num_scalar_prefetch=1, grid=(h // tr,),
        in_specs=[pl.BlockSpec((4, 2, tr, C), lambda i, k: (0, 0, i, 0)),
                  pl.BlockSpec((None, None, tr, C), lambda i, k: (k[0], k[1], i, 0))],
        out_specs=pl.BlockSpec((tr, C), lambda i, k: (i, 0)))
    return pl.pallas_call(
        body, name=name, grid_spec=grid_spec, out_shape=_sds((h, C), F32), compiler_params=_params(("parallel",)),
    )(idx, q, p)


def _adamw(name, w, g, m, v):
    R, C = w.shape
    tr = R
    while tr * C * 4 > (1 << 20) and tr % 16 == 0:
        tr //= 2
    c1 = 1.0 - ADAM_B1 ** ADAM_STEP
    c2 = 1.0 - ADAM_B2 ** ADAM_STEP

    def body(w_ref, g_ref, m_ref, v_ref, d_ref, nm_ref, nv_ref):
        gv = g_ref[...]
        nm = ADAM_B1 * m_ref[...] + (1.0 - ADAM_B1) * gv
        nv = ADAM_B2 * v_ref[...] + (1.0 - ADAM_B2) * (gv * gv)
        nm_ref[...] = nm
        nv_ref[...] = nv
        d_ref[...] = -ADAM_LR * ((nm * (1.0 / c1)) / (jnp.sqrt(nv * (1.0 / c2)) + ADAM_EPS) + ADAM_WD * w_ref[...])

    spec = pl.BlockSpec((tr, C), lambda i: (i, 0))
    return pl.pallas_call(
        body, name=name, grid=(R // tr,), in_specs=[spec] * 4, out_specs=[spec] * 3,
        out_shape=[_sds((R, C), F32)] * 3, compiler_params=_params(("parallel",)),
    )(w, g, m, v)


def _adamw_halves(name, w, own, sib, m, v, c_idx):
    R, C = w.shape
    h = R // 2
    tr = h
    while tr * C * 4 > (1 << 20) and tr % 16 == 0:
        tr //= 2
    nb = h // tr
    c1 = 1.0 - ADAM_B1 ** ADAM_STEP
    c2 = 1.0 - ADAM_B2 ** ADAM_STEP

    def body(c_ref, w_ref, own_ref, sib_ref, m_ref, v_ref, g_ref, d_ref, nm_ref, nv_ref):
        mine = (pl.program_id(0) // nb) == c_ref[0]
        gv = jnp.where(mine, own_ref[...], sib_ref[...])
        nm = ADAM_B1 * m_ref[...] + (1.0 - ADAM_B1) * gv
        nv = ADAM_B2 * v_ref[...] + (1.0 - ADAM_B2) * (gv * gv)
        g_ref[...] = gv
        nm_ref[...] = nm
        nv_ref[...] = nv
        d_ref[...] = -ADAM_LR * ((nm * (1.0 / c1)) / (jnp.sqrt(nv * (1.0 / c2)) + ADAM_EPS) + ADAM_WD * w_ref[...])

    spec = pl.BlockSpec((tr, C), lambda i, c: (i, 0))
    half = pl.BlockSpec((tr, C), lambda i, c: (i % nb, 0))
    grid_spec = pltpu.PrefetchScalarGridSpec(
        num_scalar_prefetch=1, grid=(R // tr,), in_specs=[spec, half, half, spec, spec], out_specs=[spec] * 4)
    return pl.pallas_call(
        body, name=name, grid_spec=grid_spec, out_shape=[_sds((R, C), F32)] * 4,
        compiler_params=_params(("parallel",)),
    )(c_idx, w, own, sib, m, v)


def _pack(g, scale, shift, gate):
    rows = jnp.stack([g, scale, shift, gate]).astype(F32)
    return jnp.concatenate([rows, jnp.zeros((4, D), F32)], axis=0)


def _fold8(vec):
    m = -(-vec.shape[0] // (8 * LANES)) * LANES
    return jnp.concatenate([vec, jnp.zeros((8 * m - vec.shape[0],), vec.dtype)]).reshape(8, m)


def _allgather_vectors(name, vec):
    return _allgather_rows(name, _fold8(vec)).reshape(8, -1)


def kernel(x, c, w_ada, b_ada, norm1_g, ffn1_w_gu, ffn1_w_down, norm2_g, w_mix_in, b_merge, conv_w, w_conv_out, w_attn_out, w_out, norm3_g, ffn2_w_gu, ffn2_w_down, final_g, loss_target, m_w_ada, m_b_ada, m_norm1_g, m_ffn1_w_gu, m_ffn1_w_down, m_norm2_g, m_w_mix_in, m_b_merge, m_conv_w, m_w_conv_out, m_w_attn_out, m_w_out, m_norm3_g, m_ffn2_w_gu, m_ffn2_w_down, m_final_g, v_w_ada, v_b_ada, v_norm1_g, v_ffn1_w_gu, v_ffn1_w_down, v_norm2_g, v_w_mix_in, v_b_merge, v_conv_w, v_w_conv_out, v_w_attn_out, v_w_out, v_norm3_g, v_ffn2_w_gu, v_ffn2_w_down, v_final_g):
    xi, yi, ci = lax.axis_index("x"), lax.axis_index("y"), lax.axis_index("c")
    chip = 2 * xi + yi
    dev = 4 * xi + 2 * yi + ci
    S = x.shape[1]
    h0 = x[0]
    target = loss_target[0]

    (wgu1,) = _allgather_weights([ffn1_w_gu[0].astype(BF16)])
    late_shards = [w[0].astype(BF16) for w in (w_conv_out, w_attn_out, w_out, ffn2_w_gu, ffn2_w_down)]
    c_idx = jnp.reshape(ci, (1,)).astype(jnp.int32)
    chip_idx = jnp.reshape(chip, (1,)).astype(jnp.int32)

    def reduce_pairs(tag, names, grads):
        g4 = [g.reshape(4, 2, g.shape[1] // 2, g.shape[2]) for g in grads]
        recv = _sibling_swap_halves("grad_sibling_swap_" + tag, g4)
        return [_pair_sum("pair_sum_" + nm, a, b, c_idx) for nm, a, b in zip(names, g4, recv)]

    small = jnp.concatenate([c[0], b_merge[0].reshape(-1), conv_w[0].reshape(-1)])
    gathered = _allgather_vectors("allgather_small", small)
    c_all = gathered[:, :D]
    per_chip = gathered[0::2]
    bm_full = jnp.concatenate([per_chip[k, D:D + 512].reshape(2, 256) for k in range(4)], axis=1)
    cw_full = jnp.concatenate([per_chip[k, D + 512:D + 896].reshape(3, 128) for k in range(4)], axis=1)
    b_l = lax.dynamic_slice_in_dim(b_ada, chip * ADA_SHARD, ADA_SHARD, axis=1)
    mod_l = _ada_fwd(c_all, w_ada[0], b_l)
    mod_g = _allgather_rows("allgather_mod", mod_l).reshape(8, 8, ADA_SHARD)
    mod_all = jnp.concatenate([mod_g[2 * k] for k in range(4)], axis=1)
    mod = lax.dynamic_slice_in_dim(mod_all, dev, 1, axis=0).reshape(3, 3, D)
    p1 = _pack(norm1_g[0], mod[0, 1], mod[0, 0], mod[0, 2])
    p2 = _pack(norm2_g[0], mod[1, 1], mod[1, 0], mod[1, 2])
    p3 = _pack(norm3_g[0], mod[2, 1], mod[2, 0], mod[2, 2])
    pf = _pack(final_g, final_g, final_g, final_g)

    u1 = _norm_mod_fwd("norm1_fwd", h0, p1)
    gu1, hm1, (wd1,) = _ffn_up("ffn1_up", u1, wgu1, [ffn1_w_down[0].astype(BF16)])
    f1, h1, u2, (wmix,) = _proj_residual("ffn1_down", hm1, wd1.reshape(D_FF, D), h0, p1, 0.5, p2,
                                         gather_shards=[w_mix_in[0].astype(BF16)])
    wd1 = wd1.reshape(D_FF, D)
    proj = _mm("mix_in", (4, S // TM), u2, pl.BlockSpec((TM, D), lambda s, i: (i, 0)),
               wmix, pl.BlockSpec((None, D, MIX_SHARD), lambda s, i: (s, 0, 0)), (1, 0),
               [_sds((S, MIX_W), BF16)], [pl.BlockSpec((TM, MIX_SHARD), lambda s, i: (i, s))], _store(BF16),
               semantics=("parallel", "parallel"))[0]
    sc = _conv_fwd(proj, cw_full)
    o, t_tot, (wco, wao, wout, wgu2, wd2) = _attn_fwd(proj, late_shards)
    wout = wout.reshape(D, D)
    wd2 = wd2.reshape(D_FF, D)
    ya, yb, merged = _merge_fwd(sc, o, wco, wao, proj, bm_full)
    y2, h2, u3, _ = _proj_residual("mix_out", merged, wout, h1, p2, 1.0, p3)
    gu3, hm3, _ = _ffn_up("ffn2_up", u3, wgu2)

    dh3, df3, sums_f, loss_blk = _proj_residual_loss("ffn2_down", hm3, wd2, h2, p3, 0.5, pf, target)
    (dh2, dy2, sums3), dwgu2, dwd2 = _ffn_bwd("ffn2", df3, u3, gu3, hm3, wgu2, wd2,
                                              _norm_bwd_tail(h2, p3, dh3, prev=(p2, y2, 1.0)))

    dya, dyb, dga, dgb, sums_bm = _merge_bwd(dy2, wout, ya, yb, proj, bm_full)
    dwout = _mm("dw_out", (1, S // TK), merged, pl.BlockSpec((TK, D), lambda n, k: (k, 0)),
                dy2, pl.BlockSpec((TK, D), lambda n, k: (k, 0)), (0, 0),
                [_sds((D, D), BF16)], [pl.BlockSpec((D, D), lambda n, k: (0, 0))], _store(BF16),
                nk=S // TK, acc_shape=(D, D))[0]
    dsc, do = _mixer_out_bwd(dya, dyb, wco, wao)
    dwco, dwao = _mixer_out_dw(sc, o, dya, dyb)
    dcb, dcc, dcx, dcw = _conv_bwd(dsc, proj, cw_full)
    names_e = ["ffn2_w_gu", "ffn2_w_down", "w_out", "w_conv_out", "w_attn_out"]
    part_e = [g.reshape(4, 2, g.shape[1] // 2, g.shape[2])
              for g in (dwgu2, dwd2.reshape(4, 704, D), dwout.reshape(4, 256, D), dwco, dwao)]
    dq, dk, dv, came_e = _attn_bwd(proj, t_tot, do, part_e)
    dproj = jnp.concatenate([dcb, dcc, dcx, dq, dk, dv, dga, dgb], axis=1)
    tail2 = _norm_bwd_tail(h1, p2, dh2, prev=(p1, f1, 0.5))
    dh1, df1, sums2 = _mm("mix_in_bwd", (S // TM, 4), dproj, pl.BlockSpec((TM, MIX_SHARD), lambda i, s: (i, s)),
                          wmix, pl.BlockSpec((None, D, MIX_SHARD), lambda i, s: (s, 0, 0)), (1, 1),
                          tail2["out_shapes"], tail2["out_specs"], tail2["epilogue"], extras=tail2["extras"],
                          extra_specs=tail2["extra_specs"], nk=4, acc_shape=(TM, D),
                          semantics=("arbitrary", "arbitrary"))
    dwmix = _grad_w("dw_mix_in", u2, D, dproj, MIX_SHARD, lambda s, k: (k, s), 4, (4, D, MIX_SHARD),
                    (None, D, MIX_SHARD), lambda s, k: (s, 0, 0))

    part_mix = reduce_pairs("mix", ["w_mix_in"], [dwmix])
    dgu1, *came_mix = _ffn_down_bwd("ffn1_down_bwd", df1, wd1, gu1, a2a_parts=part_mix)
    dwd1 = _ffn_dw_down("ffn1_dw_down", hm1, df1)
    part_wd1 = reduce_pairs("wd1", ["ffn1_w_down"], [dwd1.reshape(4, 704, D)])
    dwgu1, *came_wd1 = _ffn_dw_gu("ffn1_dw_gu", u1, dgu1, a2a_parts=part_wd1)
    part_gu1 = reduce_pairs("gu1", ["ffn1_w_gu"], [dwgu1])
    grad_x, sums1, *came_gu1 = _ffn_up_bwd("ffn1_up_bwd", dgu1, wgu1, _norm_bwd_tail(h0, p1, dh1),
                                           a2a_parts=part_gu1)

    dmod = jnp.stack([sums1[0], sums1[1], sums2[3], sums2[0], sums2[1], sums3[3], sums3[0], sums3[1], sums_f[1]])
    small_g = jnp.concatenate([dmod.reshape(-1), sums1[2], sums2[2], sums3[2], sums_f[0],
                               sums_bm[0], sums_bm[1], dcw.reshape(-1), loss_blk[0, 0:1]])
    all_g = _allgather_vectors("allgather_small_grads", small_g)
    tot = _sum_rows("sum_small_grads", all_g)[0]
    loss = tot[16 * D + 512]
    g_b_ada = tot[:9 * D][None, :]
    g_n1, g_n2, g_n3 = (tot[(9 + k) * D:(10 + k) * D][None, :] for k in range(3))
    g_fin = tot[12 * D:13 * D]
    g_bm = lax.dynamic_slice_in_dim(tot[13 * D:15 * D].reshape(2, D), chip * 256, 256, axis=1)[None]
    g_cw = lax.dynamic_slice_in_dim(tot[15 * D:16 * D + 512].reshape(3, 512), chip * 128, 128, axis=1)[None]
    dmod_l = lax.dynamic_slice_in_dim(all_g[:, :9 * D], chip * ADA_SHARD, ADA_SHARD, axis=1)
    g_w_ada = _ada_bwd(c_all.T, dmod_l)[None]

    names_l = ["w_mix_in", "ffn1_w_down", "ffn1_w_gu"]
    names = names_e + names_l
    place_idx = jnp.stack([chip, ci]).astype(jnp.int32)
    half = [_sum8("device_sum_" + nm, q, p, place_idx) for nm, q, p in zip(names_e, came_e, part_e)]
    half += [_sum4("chip_sum_" + nm, q, p, chip_idx)
             for nm, q, p in zip(names_l, came_mix + came_wd1 + came_gu1, part_mix + part_wd1 + part_gu1)]
    g_own = dict(zip(names, half))
    g_sib = dict(zip(names, _sibling_share(half)))

    weights = dict(w_ada=w_ada, b_ada=b_ada, norm1_g=norm1_g, ffn1_w_gu=ffn1_w_gu, ffn1_w_down=ffn1_w_down,
                   norm2_g=norm2_g, w_mix_in=w_mix_in, b_merge=b_merge, conv_w=conv_w, w_conv_out=w_conv_out,
                   w_attn_out=w_attn_out, w_out=w_out, norm3_g=norm3_g, ffn2_w_gu=ffn2_w_gu,
                   ffn2_w_down=ffn2_w_down, final_g=final_g)
    ms = dict(w_ada=m_w_ada, b_ada=m_b_ada, norm1_g=m_norm1_g, ffn1_w_gu=m_ffn1_w_gu, ffn1_w_down=m_ffn1_w_down,
              norm2_g=m_norm2_g, w_mix_in=m_w_mix_in, b_merge=m_b_merge, conv_w=m_conv_w, w_conv_out=m_w_conv_out,
              w_attn_out=m_w_attn_out, w_out=m_w_out, norm3_g=m_norm3_g, ffn2_w_gu=m_ffn2_w_gu,
              ffn2_w_down=m_ffn2_w_down, final_g=m_final_g)
    vs = dict(w_ada=v_w_ada, b_ada=v_b_ada, norm1_g=v_norm1_g, ffn1_w_gu=v_ffn1_w_gu, ffn1_w_down=v_ffn1_w_down,
              norm2_g=v_norm2_g, w_mix_in=v_w_mix_in, b_merge=v_b_merge, conv_w=v_conv_w, w_conv_out=v_w_conv_out,
              w_attn_out=v_w_attn_out, w_out=v_w_out, norm3_g=v_norm3_g, ffn2_w_gu=v_ffn2_w_gu,
              ffn2_w_down=v_ffn2_w_down, final_g=v_final_g)
    order = list(weights)
    grad = dict(w_ada=g_w_ada, b_ada=g_b_ada, norm1_g=g_n1, norm2_g=g_n2, norm3_g=g_n3, final_g=g_fin,
                b_merge=g_bm, conv_w=g_cw)
    delta, new_m, new_v = {}, {}, {}
    small_names = ["b_ada", "norm1_g", "norm2_g", "norm3_g", "final_g", "b_merge", "conv_w"]
    flat = lambda d: jnp.concatenate([d[nm].reshape(-1) for nm in small_names])[None, :]
    sd, sm, sv = _adamw("adamw_small", flat(weights), flat(grad), flat(ms), flat(vs))
    off = 0
    for nm in small_names:
        size = weights[nm].size
        for dst, src in ((delta, sd), (new_m, sm), (new_v, sv)):
            dst[nm] = src[0, off:off + size].reshape(weights[nm].shape)
        off += size
    for nm in order:
        if nm in small_names:
            continue
        shp = weights[nm].shape
        if nm in g_own:
            g2, d2, m2, v2 = _adamw_halves("adamw_" + nm, weights[nm][0], g_own[nm], g_sib[nm], ms[nm][0], vs[nm][0],
                                           c_idx)
            grad[nm] = g2.reshape(shp)
        else:
            d2, m2, v2 = _adamw("adamw_" + nm, weights[nm][0], grad[nm][0], ms[nm][0], vs[nm][0])
        delta[nm], new_m[nm], new_v[nm] = d2.reshape(shp), m2.reshape(shp), v2.reshape(shp)

    return (loss, grad_x[None], *[grad[nm] for nm in order], *[delta[nm] for nm in order],
            *[new_m[nm] for nm in order], *[new_v[nm] for nm in order])
```

```python
import functools

import jax
import jax.numpy as jnp
from jax import lax
from jax.experimental import pallas as pl
from jax.experimental.pallas import tpu as pltpu

F32 = jnp.float32
BF16 = jnp.bfloat16
MESH = pl.DeviceIdType.MESH

VMEM_LIMIT = 56 * 1024 * 1024
LANES = 128

D = 1024
D_FF = 2816
FF_SHARD = 1408
MIX_SHARD = 1280
MIX_W = 5120
HEAD_PAIRS = 4
HEAD_DIM = 64
CONV_W = 512
EPS = 1e-6
ATT_BLK = 256

ADAM_LR = 0.001
ADAM_B1 = 0.9
ADAM_B2 = 0.999
ADAM_EPS = 1e-08
ADAM_WD = 0.01
ADAM_STEP = 10


def _params(semantics=None):
    return pltpu.CompilerParams(dimension_semantics=semantics, vmem_limit_bytes=VMEM_LIMIT)


def _sigmoid(x):
    return 1.0 / (1.0 + jnp.exp(-x))


def _place():
    x, y, c = lax.axis_index("x"), lax.axis_index("y"), lax.axis_index("c")
    chips = [(1 - x, y), (x, 1 - y), (1 - x, 1 - y)]
    return x, y, c, chips


def _allgather_rows(name, blk):
    m_per, n = blk.shape

    def body(x_ref, out_ref, send_sems, recv_sems, local_sem):
        x, y, c, chips = _place()
        me, sibling = (x, y, c), (x, y, 1 - c)

        def rows(px, py, pc):
            return out_ref.at[pl.ds((4 * px + 2 * py + pc) * m_per, m_per), :]

        def copy(k, block, to, src=None):
            return pltpu.make_async_remote_copy(
                src_ref=rows(*block) if src is None else src, dst_ref=rows(*block),
                send_sem=send_sems.at[k], recv_sem=recv_sems.at[k], device_id=to, device_id_type=MESH)

        mine = pltpu.make_async_copy(x_ref, rows(*me), local_sem)
        mine.start()
        first = [copy(0, me, sibling, src=x_ref)]
        first += [copy(1 + j, me, (*chip, c), src=x_ref) for j, chip in enumerate(chips)]
        for cp in first:
            cp.start()
        passed = [copy(4 + j, (*chip, c), sibling) for j, chip in enumerate(chips)]
        for j, chip in enumerate(chips):
            copy(1 + j, (*chip, c), me).wait_recv()
            passed[j].start()
        copy(0, sibling, me).wait_recv()
        for j, chip in enumerate(chips):
            copy(4 + j, (*chip, 1 - c), me).wait_recv()
        for cp in first + passed:
            cp.wait_send()
        mine.wait()

    return pl.pallas_call(
        body, name=name,
        out_shape=jax.ShapeDtypeStruct((8 * m_per, n), blk.dtype),
        in_specs=[pl.BlockSpec(memory_space=pltpu.VMEM)],
        out_specs=pl.BlockSpec(memory_space=pltpu.VMEM),
        scratch_shapes=[pltpu.SemaphoreType.DMA((7,)), pltpu.SemaphoreType.DMA((7,)), pltpu.SemaphoreType.DMA],
        compiler_params=pltpu.CompilerParams(vmem_limit_bytes=VMEM_LIMIT),
    )(blk)


def _hbm_specs(n):
    return [pl.BlockSpec(memory_space=pltpu.HBM)] * n


def _allgather_weights(shards):
    n = len(shards)

    def body(*refs):
        start, relay, finish = _gather_protocol(refs[:n], refs[n:2 * n], *refs[2 * n:])
        start()
        relay()
        finish()

    gathered = pl.pallas_call(
        body, name="allgather_weights",
        out_shape=_gather_shapes(shards), in_specs=_hbm_specs(n), out_specs=_hbm_specs(n),
        scratch_shapes=_gather_sems(n),
    )(*shards)
    return _with_own_shard(gathered, shards)


def _gather_shapes(shards):
    return [jax.ShapeDtypeStruct((4, *s.shape), s.dtype) for s in shards]


def _gather_sems(n):
    return [pltpu.SemaphoreType.DMA((6 * n,)), pltpu.SemaphoreType.DMA((6 * n,))]


def _with_own_shard(gathered, shards):
    chip = 2 * lax.axis_index("x") + lax.axis_index("y")
    return [lax.dynamic_update_slice(g, s[None], (chip, 0, 0)) for g, s in zip(gathered, shards)]


def _gather_protocol(ins, outs, send_sems, recv_sems):
    n = len(ins)
    x, y, c, chips = _place()
    me, sibling = (x, y, c), (x, y, 1 - c)
    me_k = 2 * x + y

    def half(w, k, hc):
        h = ins[w].shape[0] // 2
        return outs[w].at[k, pl.ds(pl.multiple_of(hc * h, 8), h), :]

    def copy(w, j, k, hc, to, src=None):
        dst = half(w, k, hc)
        return pltpu.make_async_remote_copy(
            src_ref=dst if src is None else src, dst_ref=dst,
            send_sem=send_sems.at[6 * w + j], recv_sem=recv_sems.at[6 * w + j],
            device_id=to, device_id_type=MESH)

    def first(w, j):
        h = ins[w].shape[0] // 2
        src = ins[w].at[pl.ds(pl.multiple_of(c * h, 8), h), :]
        return copy(w, j, me_k, c, (*chips[j], c), src=src)

    def passed(w, j):
        px, py = chips[j]
        return copy(w, 3 + j, 2 * px + py, c, sibling)

    pairs = [(w, j) for w in range(n) for j in range(3)]

    def start():
        for w, j in pairs:
            first(w, j).start()

    def relay():
        for w, j in pairs:
            px, py = chips[j]
            copy(w, j, 2 * px + py, c, me).wait_recv()
            passed(w, j).start()

    def finish():
        for w, j in pairs:
            px, py = chips[j]
            copy(w, 3 + j, 2 * px + py, 1 - c, me).wait_recv()
        for w, j in pairs:
            first(w, j).wait_send()
            passed(w, j).wait_send()

    return start, relay, finish


def _sibling_swap_halves(name, grads):
    n = len(grads)

    def body(*refs):
        ins, outs = refs[:n], refs[n:2 * n]
        send_sems, recv_sems = refs[2 * n:]
        x, y, c, _ = _place()
        cps = []
        for w in range(n):
            cp = pltpu.make_async_remote_copy(
                src_ref=ins[w].at[:, 1 - c], dst_ref=outs[w],
                send_sem=send_sems.at[w], recv_sem=recv_sems.at[w],
                device_id=(x, y, 1 - c), device_id_type=MESH)
            cp.start()
            cps.append(cp)
        for cp in cps:
            cp.wait()

    return pl.pallas_call(
        body, name=name,
        out_shape=[jax.ShapeDtypeStruct((4, *g.shape[2:]), g.dtype) for g in grads],
        in_specs=_hbm_specs(n), out_specs=_hbm_specs(n),
        scratch_shapes=[pltpu.SemaphoreType.DMA((n,)), pltpu.SemaphoreType.DMA((n,))],
    )(*grads)


def _all_to_all_sems(n):
    return [pltpu.SemaphoreType.DMA((3 * n,)), pltpu.SemaphoreType.DMA((3 * n,))]


def _all_to_all_protocol(ins, outs, send_sems, recv_sems):
    n = len(ins)
    x, y, c, chips = _place()
    me_k = 2 * x + y
    pairs = [(w, j) for w in range(n) for j in range(3)]

    def sent(w, j):
        px, py = chips[j]
        return pltpu.make_async_remote_copy(
            src_ref=ins[w].at[2 * px + py], dst_ref=outs[w].at[me_k],
            send_sem=send_sems.at[3 * w + j], recv_sem=recv_sems.at[3 * w + j],
            device_id=(px, py, c), device_id_type=MESH)

    def start():
        for w, j in pairs:
            sent(w, j).start()

    def finish():
        for w, j in pairs:
            px, py = chips[j]
            slab = outs[w].at[2 * px + py]
            pltpu.make_async_remote_copy(
                src_ref=slab, dst_ref=slab, send_sem=send_sems.at[3 * w + j],
                recv_sem=recv_sems.at[3 * w + j], device_id=(px, py, c), device_id_type=MESH).wait_recv()
        for w, j in pairs:
            sent(w, j).wait_send()

    return start, finish


def _reduce8_sems(n):
    return [pltpu.SemaphoreType.DMA((7 * n,)), pltpu.SemaphoreType.DMA((7 * n,))]


def _reduce8_protocol(ins, outs, send_sems, recv_sems):
    n = len(ins)
    x, y, c, chips = _place()
    me_k = 2 * x + y
    far = [(w, j, hc) for w in range(n) for j in range(3) for hc in range(2)]

    def sent(w, j, hc):
        px, py = chips[j]
        return pltpu.make_async_remote_copy(
            src_ref=ins[w].at[2 * px + py, hc], dst_ref=outs[w].at[me_k, c],
            send_sem=send_sems.at[7 * w + 2 * j + hc], recv_sem=recv_sems.at[7 * w + 2 * j + c],
            device_id=(px, py, hc), device_id_type=MESH)

    def to_sibling(w):
        return pltpu.make_async_remote_copy(
            src_ref=ins[w].at[me_k, 1 - c], dst_ref=outs[w].at[me_k, c],
            send_sem=send_sems.at[7 * w + 6], recv_sem=recv_sems.at[7 * w + 6],
            device_id=(x, y, 1 - c), device_id_type=MESH)

    def arrival(w, slab, k):
        return pltpu.make_async_remote_copy(
            src_ref=slab, dst_ref=slab, send_sem=send_sems.at[7 * w + k], recv_sem=recv_sems.at[7 * w + k],
            device_id=(x, y, c), device_id_type=MESH)

    def start():
        for w in range(n):
            to_sibling(w).start()
        for w, j, hc in far:
            sent(w, j, hc).start()

    def finish():
        for w in range(n):
            arrival(w, outs[w].at[me_k, 1 - c], 6).wait_recv()
        for w, j, cc in far:
            px, py = chips[j]
            arrival(w, outs[w].at[2 * px + py, cc], 2 * j + cc).wait_recv()
        for w in range(n):
            to_sibling(w).wait_send()
        for w, j, hc in far:
            sent(w, j, hc).wait_send()

    return start, finish


def _sibling_share(halves):
    n = len(halves)

    def body(*refs):
        ins, outs = refs[:n], refs[n:2 * n]
        send_sems, recv_sems = refs[2 * n:]
        x, y, c, _ = _place()
        cps = []
        for w in range(n):
            cp = pltpu.make_async_remote_copy(
                src_ref=ins[w], dst_ref=outs[w], send_sem=send_sems.at[w], recv_sem=recv_sems.at[w],
                device_id=(x, y, 1 - c), device_id_type=MESH)
            cp.start()
            cps.append(cp)
        for cp in cps:
            cp.wait()

    return pl.pallas_call(
        body, name="grad_sibling_share",
        out_shape=[jax.ShapeDtypeStruct(p.shape, p.dtype) for p in halves],
        in_specs=_hbm_specs(n), out_specs=_hbm_specs(n),
        scratch_shapes=[pltpu.SemaphoreType.DMA((n,)), pltpu.SemaphoreType.DMA((n,))],
    )(*halves)


def _mm(name, grid, a, a_spec, b, b_spec, contract, out_shapes, out_specs, epilogue,
        extras=(), extra_specs=(), nk=1, acc_shape=None, semantics=None, a2a_parts=(), gather_shards=(),
        relay_at=None):
    assert not (a2a_parts and gather_shards)
    moved = tuple(a2a_parts) + tuple(gather_shards)
    ne, no, nc = len(extras), len(out_shapes), len(moved)
    nd = len(grid)

    def body(*refs):
        a_ref, b_ref = refs[0], refs[1]
        ex, outs = refs[2:2 + ne], refs[2 + ne + nc:2 + ne + nc + no]
        if nc:
            ids = [pl.program_id(d) for d in range(nd)]
            comm_refs = (refs[2 + ne:2 + ne + nc], refs[2 + ne + nc + no:2 + ne + 2 * nc + no], *refs[-2:])
            at_start = functools.reduce(jnp.logical_and, [i == 0 for i in ids])
            if a2a_parts:
                start, finish = _all_to_all_protocol(*comm_refs)
                pl.when(at_start)(start)
            else:
                start, relay, finish = _gather_protocol(*comm_refs)
                pl.when(at_start)(start)
                at_relay = grid[0] // 2 if relay_at is None else relay_at
                pl.when(functools.reduce(jnp.logical_and, [ids[0] == at_relay] + [i == 0 for i in ids[1:]]))(relay)

        def prod():
            return lax.dot_general(a_ref[...], b_ref[...], (((contract[0],), (contract[1],)), ((), ())),
                                   preferred_element_type=F32)

        if nk == 1:
            epilogue(prod(), ex, outs)
        else:
            acc = refs[2 + ne + 2 * nc + no]
            k = pl.program_id(nd - 1)

            @pl.when(k == 0)
            def _():
                acc[...] = prod()

            @pl.when(k > 0)
            def _():
                acc[...] += prod()

            @pl.when(k == nk - 1)
            def _():
                epilogue(acc[...], ex, outs)

        if nc:
            pl.when(functools.reduce(jnp.logical_and, [i == g - 1 for i, g in zip(ids, grid)]))(finish)

    if semantics is None or nc:
        semantics = ("arbitrary",) * nd
    return pl.pallas_call(
        body, name=name, grid=grid,
        in_specs=[a_spec, b_spec, *extra_specs] + _hbm_specs(nc),
        out_specs=list(out_specs) + _hbm_specs(nc),
        out_shape=list(out_shapes) + [jax.ShapeDtypeStruct(p.shape, p.dtype) for p in a2a_parts]
        + _gather_shapes(gather_shards),
        scratch_shapes=([] if nk == 1 else [pltpu.VMEM(acc_shape, F32)])
        + (_all_to_all_sems(nc) if a2a_parts else _gather_sems(nc) if gather_shards else []),
        compiler_params=_params(semantics),
    )(a, b, *extras, *moved)


def _store(dtype):
    def epilogue(acc, ex, outs):
        outs[0][...] = acc.astype(dtype)
    return epilogue


def _sds(shape, dtype):
    return jax.ShapeDtypeStruct(shape, dtype)


TR = 512


def _row_spec(width, tr=TR):
    return pl.BlockSpec((tr, width), lambda i: (i, 0))


def _const_spec(shape):
    nd = len(shape)
    return pl.BlockSpec(shape, lambda i: (0,) * nd)


def _norm_mod_fwd(name, h, p):
    S = h.shape[0]

    def body(h_ref, p_ref, u_ref):
        hv = h_ref[...]
        r = lax.rsqrt(jnp.mean(hv * hv, axis=-1, keepdims=True) + EPS)
        nrm = (hv * r) * p_ref[0:1, :]
        u_ref[...] = (nrm * (1.0 + p_ref[1:2, :]) + p_ref[2:3, :]).astype(BF16)

    return pl.pallas_call(
        body, name=name, grid=(S // TR,),
        in_specs=[_row_spec(D), _const_spec((8, D))], out_specs=_row_spec(D),
        out_shape=_sds((S, D), BF16), compiler_params=_params(("parallel",)),
    )(h, p)


def _rmsnorm_parts(hv):
    r = lax.rsqrt(jnp.mean(hv * hv, axis=-1, keepdims=True) + EPS)
    return r, hv * r


def _norm_bwd_tail(h, p, dh_res, prev=None):
    S = h.shape[0]
    row = pl.BlockSpec((TM, D), lambda i, s: (i, 0))
    const = pl.BlockSpec((8, D), lambda i, s: (0, 0))
    extras, specs = [h, p, dh_res], [row, const, row]
    out_shapes, out_specs = [_sds((S, D), F32)], [row]
    if prev is not None:
        extras += [prev[0], prev[1]]
        specs += [const, row]
        out_shapes.append(_sds((S, D), BF16))
        out_specs.append(row)
    out_shapes.append(_sds((8, D), F32))
    out_specs.append(const)

    def epilogue(duv, ex, outs):
        h_ref, p_ref, r_ref = ex[:3]
        dh_ref, sums_ref = outs[0], outs[-1]

        @pl.when(pl.program_id(0) == 0)
        def _():
            sums_ref[...] = jnp.zeros_like(sums_ref)

        g = p_ref[0:1, :]
        r, xn = _rmsnorm_parts(h_ref[...])
        dn = duv * (1.0 + p_ref[1:2, :])
        dxn = dn * g
        dh = r_ref[...] + r * (dxn - xn * jnp.mean(dxn * xn, axis=-1, keepdims=True))
        dh_ref[...] = dh
        sums_ref[0:1, :] += jnp.sum(duv, axis=0, keepdims=True)
        sums_ref[1:2, :] += jnp.sum(duv * (xn * g), axis=0, keepdims=True)
        sums_ref[2:3, :] += jnp.sum(dn * xn, axis=0, keepdims=True)
        if prev is not None:
            pp_ref, f_ref = ex[3:5]
            outs[1][...] = (prev[2] * pp_ref[3:4, :] * dh).astype(BF16)
            sums_ref[3:4, :] += prev[2] * jnp.sum(dh * f_ref[...].astype(F32), axis=0, keepdims=True)

    return dict(extras=tuple(extras), extra_specs=tuple(specs), out_shapes=out_shapes, out_specs=out_specs,
                epilogue=epilogue)


TM = 512


def _ffn_up(name, u, wgu4, shards=()):
    S = u.shape[0]
    n = len(shards)
    ni = S // TM

    def body(u_ref, wg_ref, wu_ref, *rest):
        gu_ref, hm_ref = rest[n:n + 2]
        s, i = pl.program_id(0), pl.program_id(1)
        if n:
            start, relay, finish = _gather_protocol(rest[:n], rest[n + 2:2 * n + 2], *rest[2 * n + 2:])
            pl.when((s == 0) & (i == 0))(start)
            pl.when((s == 1) & (i == 0))(relay)
        uv = u_ref[...]
        g = jnp.dot(uv, wg_ref[...], preferred_element_type=F32)
        up = jnp.dot(uv, wu_ref[...], preferred_element_type=F32)
        gu_ref[0] = g.astype(BF16)
        gu_ref[1] = up.astype(BF16)
        hm_ref[...] = (g * _sigmoid(g) * up).astype(BF16)
        if n:
            pl.when((s == 1) & (i == ni - 1))(finish)

    gu, hm, *gathered = pl.pallas_call(
        body, name=name, grid=(2, ni),
        in_specs=[pl.BlockSpec((TM, D), lambda s, i: (i, 0)),
                  pl.BlockSpec((None, D, FF_SHARD), lambda s, i: (s, 0, 0)),
                  pl.BlockSpec((None, D, FF_SHARD), lambda s, i: (s + 2, 0, 0))] + _hbm_specs(n),
        out_specs=[pl.BlockSpec((2, TM, FF_SHARD), lambda s, i: (0, i, s)),
                   pl.BlockSpec((TM, FF_SHARD), lambda s, i: (i, s))] + _hbm_specs(n),
        out_shape=[_sds((2, S, D_FF), BF16), _sds((S, D_FF), BF16)] + _gather_shapes(shards),
        scratch_shapes=_gather_sems(n) if n else [],
        compiler_params=_params(("arbitrary", "arbitrary") if n else ("parallel", "parallel")),
    )(u, wgu4, wgu4, *shards)
    return gu, hm, _with_own_shard(gathered, shards)


def _proj_residual(name, a, w, h, p, weight, p_next, gather_shards=()):
    S, K = a.shape

    def epilogue(acc, ex, outs):
        h_ref, p_ref, pn_ref = ex
        outs[0][...] = acc.astype(BF16)
        hout = h_ref[...] + weight * p_ref[3:4, :] * acc
        outs[1][...] = hout
        _, xn = _rmsnorm_parts(hout)
        outs[2][...] = ((xn * pn_ref[0:1, :]) * (1.0 + pn_ref[1:2, :]) + pn_ref[2:3, :]).astype(BF16)

    row = _row_spec(D, TM)
    res = _mm(
        name, (S // TM,), a, pl.BlockSpec((TM, K), lambda i: (i, 0)), w, pl.BlockSpec((K, D), lambda i: (0, 0)),
        (1, 0), [_sds((S, D), BF16), _sds((S, D), F32), _sds((S, D), BF16)], [row, row, row], epilogue,
        extras=(h, p, p_next), extra_specs=(row, _const_spec((8, D)), _const_spec((8, D))),
        semantics=("parallel",), gather_shards=gather_shards, relay_at=S // TM - 1)
    return res[0], res[1], res[2], _with_own_shard(res[3:], gather_shards)


def _proj_residual_loss(name, a, w, h, p, weight, gf, target):
    S, K = a.shape

    def epilogue(acc, ex, outs):
        h_ref, p_ref, g_ref, t_ref = ex
        dh_ref, df_ref, sums_ref, loss_ref = outs

        @pl.when(pl.program_id(0) == 0)
        def _():
            sums_ref[...] = jnp.zeros_like(sums_ref)
            loss_ref[...] = jnp.zeros_like(loss_ref)

        gate = p_ref[3:4, :]
        g = g_ref[0:1, :]
        r, xn = _rmsnorm_parts(h_ref[...] + weight * gate * acc)
        err = xn * g - t_ref[...]
        loss_ref[...] += 0.5 * jnp.sum(err * err) * (1.0 / D)
        dout = err * (1.0 / D)
        dxn = dout * g
        dh = r * (dxn - xn * jnp.mean(dxn * xn, axis=-1, keepdims=True))
        dh_ref[...] = dh
        df_ref[...] = (weight * gate * dh).astype(BF16)
        sums_ref[0:1, :] += jnp.sum(dout * xn, axis=0, keepdims=True)
        sums_ref[1:2, :] += weight * jnp.sum(dh * acc, axis=0, keepdims=True)

    row = _row_spec(D, TM)
    return _mm(
        name, (S // TM,), a, pl.BlockSpec((TM, K), lambda i: (i, 0)), w, pl.BlockSpec((K, D), lambda i: (0, 0)),
        (1, 0), [_sds((S, D), F32), _sds((S, D), BF16), _sds((8, D), F32), _sds((8, LANES), F32)],
        [row, row, _const_spec((8, D)), _const_spec((8, LANES))], epilogue,
        extras=(h, p, gf, target), extra_specs=(row, _const_spec((8, D)), _const_spec((8, D)), row),
        semantics=("arbitrary",))


def _ffn_down_bwd(name, df, wd, gu, a2a_parts=()):
    S = df.shape[0]

    def epilogue(acc, ex, outs):
        g = ex[0][0].astype(F32)
        up = ex[0][1].astype(F32)
        sg = _sigmoid(g)
        outs[0][0] = (acc * up * (sg * (1.0 + g * (1.0 - sg)))).astype(BF16)
        outs[0][1] = (acc * g * sg).astype(BF16)

    gu_spec = pl.BlockSpec((2, TM, FF_SHARD), lambda n, i: (0, i, n))
    return _mm(
        name, (2, S // TM), df, pl.BlockSpec((TM, D), lambda n, i: (i, 0)),
        wd, pl.BlockSpec((FF_SHARD, D), lambda n, i: (n, 0)), (1, 1),
        [_sds((2, S, D_FF), BF16)], [gu_spec], epilogue, extras=(gu,), extra_specs=(gu_spec,),
        semantics=("parallel", "parallel"), a2a_parts=a2a_parts)


TK = 2048


def _tk(S):
    return min(TK, S)


def _grad_w(name, a, a_w, b, b_w, b_map, n_out, out_shape, out_block, out_map, a2a_parts=()):
    S = a.shape[0]
    tk = _tk(S)
    nk = S // tk
    res = _mm(
        name, (n_out, nk), a, pl.BlockSpec((tk, a_w), lambda s, k: (k, 0)), b, pl.BlockSpec(
            (None, tk, b_w) if b.ndim == 3 else (tk, b_w), b_map), (0, 0),
        [_sds(out_shape, BF16)], [pl.BlockSpec(out_block, out_map)], _store(BF16), nk=nk, acc_shape=(a_w, b_w),
        semantics=("parallel", "arbitrary"), a2a_parts=a2a_parts)
    return res if a2a_parts else res[0]


def _ffn_dw_down(name, hm, df):
    S = df.shape[0]
    tk = _tk(S)
    return _mm(
        name, (2, S // tk), hm, pl.BlockSpec((tk, FF_SHARD), lambda m, k: (k, m)),
        df, pl.BlockSpec((tk, D), lambda m, k: (k, 0)), (0, 0),
        [_sds((D_FF, D), BF16)], [pl.BlockSpec((FF_SHARD, D), lambda m, k: (m, 0))], _store(BF16),
        nk=S // tk, acc_shape=(FF_SHARD, D), semantics=("parallel", "arbitrary"))[0]


def _ffn_up_bwd(name, dgu, wgu4, tail, a2a_parts=()):
    S = dgu.shape[1]
    return _mm(
        name, (S // TM, 4), dgu, pl.BlockSpec((None, TM, FF_SHARD), lambda i, s: (s // 2, i, s % 2)),
        wgu4, pl.BlockSpec((None, D, FF_SHARD), lambda i, s: (s, 0, 0)), (1, 1),
        tail["out_shapes"], tail["out_specs"], tail["epilogue"], extras=tail["extras"],
        extra_specs=tail["extra_specs"], nk=4, acc_shape=(TM, D), semantics=("arbitrary", "arbitrary"),
        a2a_parts=a2a_parts)


def _ffn_dw_gu(name, u_in, dgu, a2a_parts=()):
    return _grad_w(name, u_in, D, dgu, FF_SHARD, lambda s, k: (s // 2, k, s % 2), 4,
                   (4, D, FF_SHARD), (None, D, FF_SHARD), lambda s, k: (s, 0, 0), a2a_parts=a2a_parts)


def _ffn_bwd(tag, df, u_in, gu, hm, wgu4, wd, tail):
    dgu = _ffn_down_bwd(tag + "_down_bwd", df, wd, gu)[0]
    dwd = _ffn_dw_down(tag + "_dw_down", hm, df)
    res = _ffn_up_bwd(tag + "_up_bwd", dgu, wgu4, tail)
    dwgu = _ffn_dw_gu(tag + "_dw_gu", u_in, dgu)
    return res, dwgu, dwd


def _shift_down(v, k, row):
    return jnp.where(row >= k, pltpu.roll(v, k, axis=0), 0.0)


def _shift_up(v, k, row, S):
    return jnp.where(row < S - k, pltpu.roll(v, S - k, axis=0), 0.0)


def _conv_specs(S):
    cols = CONV_W // LANES
    return [pl.BlockSpec((S, LANES), functools.partial(lambda j, off: (0, off + j), off=o * cols))
            for o in range(3)]


def _conv_fwd(proj, conv_w):
    S = proj.shape[0]

    def body(cb_ref, cc_ref, cx_ref, w_ref, sc_ref):
        row = lax.broadcasted_iota(jnp.int32, (S, LANES), 0)
        v = cc_ref[...].astype(F32) * cx_ref[...].astype(F32)
        yv = w_ref[0:1, :] * _shift_down(v, 2, row) + w_ref[1:2, :] * _shift_down(v, 1, row) + w_ref[2:3, :] * v
        sc_ref[...] = (cb_ref[...].astype(F32) * yv).astype(BF16)

    return pl.pallas_call(
        body, name="conv_fwd", grid=(CONV_W // LANES,),
        in_specs=_conv_specs(S) + [pl.BlockSpec((3, LANES), lambda j: (0, j))],
        out_specs=pl.BlockSpec((S, LANES), lambda j: (0, j)), out_shape=_sds((S, CONV_W), BF16),
        compiler_params=_params(("parallel",)),
    )(proj, proj, proj, conv_w)


def _conv_bwd(dsc, proj, conv_w):
    S = proj.shape[0]

    def body(d_ref, cb_ref, cc_ref, cx_ref, w_ref, dcb_ref, dcc_ref, dcx_ref, dw_ref):
        row = lax.broadcasted_iota(jnp.int32, (S, LANES), 0)
        cc = cc_ref[...].astype(F32)
        cx = cx_ref[...].astype(F32)
        d = d_ref[...].astype(F32)
        v = cc * cx
        v1 = _shift_down(v, 1, row)
        v2 = _shift_down(v, 2, row)
        w0, w1, w2 = w_ref[0:1, :], w_ref[1:2, :], w_ref[2:3, :]
        dcb_ref[...] = (d * (w0 * v2 + w1 * v1 + w2 * v)).astype(BF16)
        dy = d * cb_ref[...].astype(F32)
        dw_ref[0:1, :] = jnp.sum(dy * v2, axis=0, keepdims=True)
        dw_ref[1:2, :] = jnp.sum(dy * v1, axis=0, keepdims=True)
        dw_ref[2:3, :] = jnp.sum(dy * v, axis=0, keepdims=True)
        dv = w2 * dy + w1 * _shift_up(dy, 1, row, S) + w0 * _shift_up(dy, 2, row, S)
        dcc_ref[...] = (dv * cx).astype(BF16)
        dcx_ref[...] = (dv * cc).astype(BF16)

    col = pl.BlockSpec((S, LANES), lambda j: (0, j))
    return pl.pallas_call(
        body, name="conv_bwd", grid=(CONV_W // LANES,),
        in_specs=[col] + _conv_specs(S) + [pl.BlockSpec((3, LANES), lambda j: (0, j))],
        out_specs=[col, col, col, pl.BlockSpec((3, LANES), lambda j: (0, j))],
        out_shape=[_sds((S, CONV_W), BF16)] * 3 + [_sds((3, CONV_W), F32)],
        compiler_params=_params(("parallel",)),
    )(dsc, proj, proj, proj, conv_w)


Q_COL, K_COL, V_COL = 1536 // LANES, 2048 // LANES, 2560 // LANES


def _split_dot(x, tri):
    hi = x.astype(BF16)
    lo = (x - hi.astype(F32)).astype(BF16)
    return jnp.dot(hi, tri, preferred_element_type=F32) + jnp.dot(lo, tri, preferred_element_type=F32)


def _tri_dot(tri, x):
    hi = x.astype(BF16)
    lo = (x - hi.astype(F32)).astype(BF16)
    return jnp.dot(tri, hi, preferred_element_type=F32) + jnp.dot(tri, lo, preferred_element_type=F32)


def _softplus(z):
    return jnp.maximum(z, 0.0) + jnp.log(1.0 + jnp.exp(-jnp.abs(z)))


def _nt(a, b):
    return lax.dot_general(a, b, (((1,), (1,)), ((), ())), preferred_element_type=F32)


def _tn(a, b):
    return lax.dot_general(a, b, (((0,), (0,)), ((), ())), preferred_element_type=F32)


def _interleave(gens, delays):
    results = [None] * len(gens)
    live = list(range(len(gens)))
    rnd = 0
    while live:
        for g in list(live):
            if rnd < delays[g]:
                continue
            try:
                next(gens[g])
            except StopIteration as stop:
                results[g] = stop.value
                live.remove(g)
        rnd += 1
    return results


def _attn_fwd(proj, shards):
    S = proj.shape[0]
    B = ATT_BLK
    nq = S // B
    n = len(shards)

    def body(q_ref, k_ref, v_ref, *rest):
        o_ref, t_ref = rest[n:n + 2]
        start, relay, finish = _gather_protocol(rest[:n], rest[n + 2:2 * n + 2], *rest[2 * n + 2:])
        p = pl.program_id(0)
        i = pl.program_id(1)
        pl.when((p == 0) & (i == 0))(start)
        pl.when((p == HEAD_PAIRS // 2) & (i == 0))(relay)
        lo_lane = lax.broadcasted_iota(jnp.int32, (B, LANES), 1) < HEAD_DIM
        row = lax.broadcasted_iota(jnp.int32, (B, B), 0)
        col = lax.broadcasted_iota(jnp.int32, (B, B), 1)
        after = (row > col).astype(BF16)
        causal = col < row
        q2 = q_ref[...] * 0.125
        zero = jnp.zeros((), BF16)
        q_heads = (jnp.where(lo_lane, q2, zero), jnp.where(lo_lane, zero, q2))

        def head_tile(q_h, st, kb, diag):
            k2 = k_ref[pl.ds(pl.multiple_of(kb * B, B), B), :]
            z = _nt(q_h, k2)
            yield
            spz = _softplus(z)
            sp = jnp.where(causal, spz, 0.0) if diag else spz
            hi = sp.astype(BF16)
            lo = (sp - hi.astype(F32)).astype(BF16)
            r = st["r"]
            st["r"] = r + jnp.sum(sp, axis=1, keepdims=True)
            yield
            rem = jnp.dot(hi, after, preferred_element_type=F32) + jnp.dot(lo, after, preferred_element_type=F32)
            yield
            a = jnp.exp(z - spz - (rem + r))
            if diag:
                a = jnp.where(causal, a, 0.0)
            ab = a.astype(BF16)
            yield
            v2 = v_ref[pl.ds(pl.multiple_of(kb * B, B), B), :]
            st["acc"] = st["acc"] + jnp.dot(ab, v2, preferred_element_type=F32)

        def tiles(kbs, carry, diags=(False, False)):
            sts = [dict(r=carry[0], acc=carry[1]), dict(r=carry[2], acc=carry[3])]
            gens = [head_tile(q_h, st, kb, dg) for kb, dg in zip(kbs, diags) for q_h, st in zip(q_heads, sts)]
            _interleave(gens, [t for t in range(len(kbs)) for _ in q_heads])
            return sts[0]["r"], sts[0]["acc"], sts[1]["r"], sts[1]["acc"]

        zr, za = jnp.zeros((B, 1), F32), jnp.zeros((B, LANES), F32)
        carry = lax.fori_loop(0, i % 2, lambda j, cr: tiles([i, i - 1], cr, (True, False)), (zr, za, zr, za))
        carry = lax.fori_loop(0, 1 - i % 2, lambda j, cr: tiles([i], cr, (True,)), carry)
        first = i - 1 - i % 2
        ra, acc_a, rb, acc_b = lax.fori_loop(
            0, i // 2, lambda j, cr: tiles([first - 2 * j, first - 2 * j - 1], cr), carry)
        o_ref[...] = jnp.where(lo_lane, acc_a, acc_b).astype(BF16)
        t_ref[...] = jnp.where(lo_lane, ra, rb).T
        pl.when((p == HEAD_PAIRS - 1) & (i == nq - 1))(finish)

    seq = lambda off: pl.BlockSpec((S, LANES), lambda p, i: (0, off + p))
    blk = pl.BlockSpec((B, LANES), lambda p, i: (i, p))
    o, t, *gathered = pl.pallas_call(
        body, name="attn_fwd", grid=(HEAD_PAIRS, nq),
        in_specs=[pl.BlockSpec((B, LANES), lambda p, i: (i, Q_COL + p)), seq(K_COL), seq(V_COL)] + _hbm_specs(n),
        out_specs=[blk, pl.BlockSpec((LANES, B), lambda p, i: (p, i))] + _hbm_specs(n),
        out_shape=[_sds((S, 512), BF16), _sds((512, S), F32)] + _gather_shapes(shards),
        scratch_shapes=_gather_sems(n),
        compiler_params=_params(("arbitrary", "arbitrary")),
    )(proj, proj, proj, *shards)
    return o, t, _with_own_shard(gathered, shards)


def _attn_bwd(proj, t, do, parts):
    S = proj.shape[0]
    kt = proj[:, K_COL * LANES:V_COL * LANES].T
    B = ATT_BLK
    nq = S // B
    n = len(parts)

    def body(q_ref, k_ref, v_ref, kt_ref, t_ref, do_ref, *rest):
        dq_ref, dk_ref, dv_ref = rest[n:n + 3]
        dk_acc, dv_acc = rest[2 * n + 3:2 * n + 5]
        start, finish = _reduce8_protocol(rest[:n], rest[n + 3:2 * n + 3], *rest[2 * n + 5:])
        i = pl.program_id(1)
        pl.when((pl.program_id(0) == 0) & (i == 0))(start)

        @pl.when(i == 0)
        def _():
            dk_acc[...] = jnp.zeros_like(dk_acc)
            dv_acc[...] = jnp.zeros_like(dv_acc)

        lo_lane = lax.broadcasted_iota(jnp.int32, (B, LANES), 1) < HEAD_DIM
        key = lax.broadcasted_iota(jnp.int32, (B, B), 0)
        qry = lax.broadcasted_iota(jnp.int32, (B, B), 1)
        upto = (qry <= key).astype(BF16)
        before = (qry < key).astype(BF16)
        causal = key < qry
        zero = jnp.zeros((), BF16)
        q2 = q_ref[...] * 0.125
        do2 = do_ref[...]
        heads = ((jnp.where(lo_lane, q2, zero), jnp.where(lo_lane, do2, zero), t_ref[0:1, :]),
                 (jnp.where(lo_lane, zero, q2), jnp.where(lo_lane, zero, do2), t_ref[HEAD_DIM:HEAD_DIM + 1, :]))

        def head_tile(head, st, kb, diag):
            q_h, do_h, t_h = head
            rows = pl.ds(pl.multiple_of(kb * B, B), B)
            z = _nt(k_ref[rows, :], q_h)
            da = _nt(v_ref[rows, :], do_h)
            yield
            spz = _softplus(z)
            sp = jnp.where(causal, spz, 0.0) if diag else spz
            hi = sp.astype(BF16)
            lo = (sp - hi.astype(F32)).astype(BF16)
            pc = st["pc"]
            st["pc"] = pc + jnp.sum(sp, axis=0, keepdims=True)
            yield
            pref = jnp.dot(upto, hi, preferred_element_type=F32) + jnp.dot(upto, lo, preferred_element_type=F32)
            yield
            a = jnp.exp(z - spz - ((t_h - pc) - pref))
            if diag:
                a = jnp.where(causal, a, 0.0)
            e = a * da
            eb = e.astype(BF16)
            ab = a.astype(BF16)
            ec = st["ec"]
            st["ec"] = ec + jnp.sum(e, axis=0, keepdims=True)
            yield
            e_before = ec + jnp.dot(before, eb, preferred_element_type=F32)
            yield
            u = jnp.exp(-spz)
            dz = u * (e + e_before) - e_before
            if diag:
                dz = jnp.where(causal, dz, 0.0)
            dzb = dz.astype(BF16)
            yield
            st["dqt"] = st["dqt"] + jnp.dot(kt_ref[:, rows], dzb, preferred_element_type=F32)
            return (jnp.dot(dzb, q_h, preferred_element_type=F32), jnp.dot(ab, do_h, preferred_element_type=F32))

        def tiles(kbs, carry, diags=(False, False)):
            sts = [dict(pc=carry[3 * h], ec=carry[3 * h + 1], dqt=carry[3 * h + 2]) for h in range(2)]
            gens = [head_tile(hd, st, kb, dg) for kb, dg in zip(kbs, diags) for hd, st in zip(heads, sts)]
            res = _interleave(gens, [t for t in range(len(kbs)) for _ in heads])
            for t, kb in enumerate(kbs):
                rows = pl.ds(pl.multiple_of(kb * B, B), B)
                (dk_a, dv_a), (dk_b, dv_b) = res[2 * t], res[2 * t + 1]
                dk_acc[rows, :] += dk_a + dk_b
                dv_acc[rows, :] += dv_a + dv_b
            return tuple(st[nm] for st in sts for nm in ("pc", "ec", "dqt"))

        zc, zq = jnp.zeros((1, B), F32), jnp.zeros((LANES, B), F32)
        carry = lax.fori_loop(0, i // 2, lambda j, cr: tiles([2 * j, 2 * j + 1], cr), (zc, zc, zq, zc, zc, zq))
        carry = lax.fori_loop(0, i % 2, lambda j, cr: tiles([i - 1, i], cr, (False, True)), carry)
        _, _, dqt_a, _, _, dqt_b = lax.fori_loop(0, 1 - i % 2, lambda j, cr: tiles([i], cr, (True,)), carry)
        head0 = lax.broadcasted_iota(jnp.int32, (LANES, B), 0) < HEAD_DIM
        dq_ref[...] = (jnp.where(head0, dqt_a, dqt_b).T * 0.125).astype(BF16)

        @pl.when(i == nq - 1)
        def _():
            dk_ref[...] = dk_acc[...].astype(BF16)
            dv_ref[...] = dv_acc[...].astype(BF16)

        pl.when((pl.program_id(0) == HEAD_PAIRS - 1) & (i == nq - 1))(finish)

    seq = lambda off: pl.BlockSpec((S, LANES), lambda p, i: (0, off + p))
    blk = pl.BlockSpec((B, LANES), lambda p, i: (i, p))
    whole = pl.BlockSpec((S, LANES), lambda p, i: (0, p))
    dq, dk, dv, *came = pl.pallas_call(
        body, name="attn_bwd", grid=(HEAD_PAIRS, nq),
        in_specs=[pl.BlockSpec((B, LANES), lambda p, i: (i, Q_COL + p)), seq(K_COL), seq(V_COL),
                  pl.BlockSpec((LANES, S), lambda p, i: (p, 0)), pl.BlockSpec((LANES, B), lambda p, i: (p, i)), blk]
        + _hbm_specs(n),
        out_specs=[blk, whole, whole] + _hbm_specs(n),
        out_shape=[_sds((S, 512), BF16)] * 3 + [jax.ShapeDtypeStruct(p.shape, p.dtype) for p in parts],
        scratch_shapes=[pltpu.VMEM((S, LANES), F32), pltpu.VMEM((S, LANES), F32)] + _reduce8_sems(n),
        compiler_params=_params(("arbitrary", "arbitrary")),
    )(proj, proj, proj, kt, t, do, *parts)
    return dq, dk, dv, came


GA_COL, GB_COL = 3072 // D, 4096 // D


def _merge_fwd(sc, o, wco4, wao4, proj, bm):
    S = sc.shape[0]

    def body(sc_ref, o_ref, wc_ref, wa_ref, ga_ref, gb_ref, bm_ref, ya_ref, yb_ref, mg_ref):
        scv, ov = sc_ref[...], o_ref[...]
        for s in range(4):
            cols = slice(s * 256, (s + 1) * 256)
            ya = jnp.dot(scv, wc_ref[s], preferred_element_type=F32)
            yb = jnp.dot(ov, wa_ref[s], preferred_element_type=F32)
            sa = _sigmoid(ga_ref[:, cols].astype(F32) + bm_ref[0:1, cols])
            sb = _sigmoid(gb_ref[:, cols].astype(F32) + bm_ref[1:2, cols])
            ya_ref[:, cols] = ya.astype(BF16)
            yb_ref[:, cols] = yb.astype(BF16)
            mg_ref[:, cols] = (sa * ya + sb * yb).astype(BF16)

    wide = pl.BlockSpec((TM, 512), lambda i: (i, 0))
    wsp = pl.BlockSpec((4, 512, 256), lambda i: (0, 0, 0))
    out = pl.BlockSpec((TM, D), lambda i: (i, 0))
    return pl.pallas_call(
        body, name="merge_fwd", grid=(S // TM,),
        in_specs=[wide, wide, wsp, wsp, pl.BlockSpec((TM, D), lambda i: (i, GA_COL)),
                  pl.BlockSpec((TM, D), lambda i: (i, GB_COL)), pl.BlockSpec((2, D), lambda i: (0, 0))],
        out_specs=[out, out, out], out_shape=[_sds((S, D), BF16)] * 3,
        compiler_params=_params(("parallel",)),
    )(sc, o, wco4, wao4, proj, proj, bm)


def _merge_bwd(dy2, wout, ya, yb, proj, bm):
    S = dy2.shape[0]

    def epilogue(acc, ex, outs):
        ya_ref, yb_ref, ga_ref, gb_ref, bm_ref = ex
        i = pl.program_id(0)
        sa = _sigmoid(ga_ref[...].astype(F32) + bm_ref[0:1, :])
        sb = _sigmoid(gb_ref[...].astype(F32) + bm_ref[1:2, :])
        dga = acc * ya_ref[...].astype(F32) * (sa * (1.0 - sa))
        dgb = acc * yb_ref[...].astype(F32) * (sb * (1.0 - sb))
        outs[0][...] = (acc * sa).astype(BF16)
        outs[1][...] = (acc * sb).astype(BF16)
        outs[2][...] = dga.astype(BF16)
        outs[3][...] = dgb.astype(BF16)

        @pl.when(i == 0)
        def _():
            outs[4][...] = jnp.zeros_like(outs[4])

        outs[4][0:1, :] += jnp.sum(dga, axis=0, keepdims=True)
        outs[4][1:2, :] += jnp.sum(dgb, axis=0, keepdims=True)

    tm = TM // 2
    out = pl.BlockSpec((tm, D), lambda i: (i, 0))
    return _mm(
        "merge_bwd", (S // tm,), dy2, out, wout, pl.BlockSpec((D, D), lambda i: (0, 0)), (1, 1),
        [_sds((S, D), BF16)] * 4 + [_sds((8, D), F32)], [out, out, out, out, _const_spec((8, D))],
        epilogue, extras=(ya, yb, proj, proj, bm),
        extra_specs=(out, out, pl.BlockSpec((tm, D), lambda i: (i, GA_COL)),
                     pl.BlockSpec((tm, D), lambda i: (i, GB_COL)), _const_spec((2, D))),
        semantics=("arbitrary",))


def _mixer_out_bwd(dya, dyb, wco4, wao4):
    S = dya.shape[0]

    def body(da_ref, db_ref, wc_ref, wa_ref, dsc_ref, do_ref):
        for d_ref, w_ref, o_ref in ((da_ref, wc_ref, dsc_ref), (db_ref, wa_ref, do_ref)):
            acc = _nt(d_ref[:, 0:256], w_ref[0])
            for s in range(1, 4):
                acc = acc + _nt(d_ref[:, s * 256:(s + 1) * 256], w_ref[s])
            o_ref[...] = acc.astype(BF16)

    wide = pl.BlockSpec((TM, D), lambda i: (i, 0))
    wsp = pl.BlockSpec((4, 512, 256), lambda i: (0, 0, 0))
    out = pl.BlockSpec((TM, 512), lambda i: (i, 0))
    return pl.pallas_call(
        body, name="mixer_out_bwd", grid=(S // TM,), in_specs=[wide, wide, wsp, wsp], out_specs=[out, out],
        out_shape=[_sds((S, 512), BF16)] * 2, compiler_params=_params(("parallel",)),
    )(dya, dyb, wco4, wao4)


def _mixer_out_dw(sc, o, dya, dyb):
    S = sc.shape[0]
    tk = _tk(S)
    nk = S // tk

    def body(sc_ref, o_ref, da_ref, db_ref, dwc_ref, dwa_ref, acc_c, acc_a):
        k = pl.program_id(0)

        @pl.when(k == 0)
        def _():
            acc_c[...] = jnp.zeros_like(acc_c)
            acc_a[...] = jnp.zeros_like(acc_a)

        acc_c[...] += _tn(sc_ref[...], da_ref[...])
        acc_a[...] += _tn(o_ref[...], db_ref[...])

        @pl.when(k == nk - 1)
        def _():
            for s in range(4):
                dwc_ref[s] = acc_c[:, s * 256:(s + 1) * 256].astype(BF16)
                dwa_ref[s] = acc_a[:, s * 256:(s + 1) * 256].astype(BF16)

    narrow = pl.BlockSpec((tk, 512), lambda k: (k, 0))
    wide = pl.BlockSpec((tk, D), lambda k: (k, 0))
    out = pl.BlockSpec((4, 512, 256), lambda k: (0, 0, 0))
    return pl.pallas_call(
        body, name="mixer_out_dw", grid=(nk,), in_specs=[narrow, narrow, wide, wide], out_specs=[out, out],
        out_shape=[_sds((4, 512, 256), BF16)] * 2,
        scratch_shapes=[pltpu.VMEM((512, D), F32), pltpu.VMEM((512, D), F32)],
        compiler_params=_params(("arbitrary",)),
    )(sc, o, dya, dyb)


ADA_SHARD = 2304
ADA_TN = 768


def _ada_fwd(c_all, w_ada_l, b_l):
    def body(c_ref, w_ref, b_ref, o_ref):
        cv = c_ref[...]
        ca = cv * _sigmoid(cv)
        o_ref[...] = jnp.dot(ca.astype(BF16), w_ref[...].astype(BF16), preferred_element_type=F32) + b_ref[...]

    return pl.pallas_call(
        body, name="ada_fwd", grid=(ADA_SHARD // ADA_TN,),
        in_specs=[pl.BlockSpec((8, D), lambda j: (0, 0)), pl.BlockSpec((D, ADA_TN), lambda j: (0, j)),
                  pl.BlockSpec((1, ADA_TN), lambda j: (0, j))],
        out_specs=pl.BlockSpec((8, ADA_TN), lambda j: (0, j)), out_shape=_sds((8, ADA_SHARD), F32),
        compiler_params=_params(("parallel",)),
    )(c_all, w_ada_l, b_l)


def _ada_bwd(c_all_t, dmod_l):
    def body(c_ref, d_ref, o_ref):
        cv = c_ref[...]
        ca = cv * _sigmoid(cv)
        o_ref[...] = jnp.dot(ca.astype(BF16).astype(F32), d_ref[...].astype(BF16).astype(F32),
                             preferred_element_type=F32, precision=lax.Precision.HIGHEST)

    return pl.pallas_call(
        body, name="ada_bwd", grid=(ADA_SHARD // ADA_TN,),
        in_specs=[pl.BlockSpec((D, 8), lambda j: (0, 0)), pl.BlockSpec((8, ADA_TN), lambda j: (0, j))],
        out_specs=pl.BlockSpec((D, ADA_TN), lambda j: (0, j)), out_shape=_sds((D, ADA_SHARD), F32),
        compiler_params=_params(("parallel",)),
    )(c_all_t, dmod_l)


def _sum_rows(name, x):
    n = x.shape[1]

    def body(x_ref, o_ref):
        s = x_ref[0:1, :]
        for d in range(1, 8):
            s = s + x_ref[d:d + 1, :]
        o_ref[...] = s

    return pl.pallas_call(
        body, name=name, in_specs=[pl.BlockSpec(memory_space=pltpu.VMEM)],
        out_specs=pl.BlockSpec(memory_space=pltpu.VMEM), out_shape=_sds((1, n), F32),
        compiler_params=pltpu.CompilerParams(vmem_limit_bytes=VMEM_LIMIT),
    )(x)


def _pair_sum(name, g4, recv, c_idx):
    _, _, h, C = g4.shape
    tr = h if h <= 512 else h // (h // 256) if h % 256 == 0 else h // 2

    def body(c_ref, g_ref, r_ref, o_ref):
        o_ref[...] = (g_ref[...].astype(F32) + r_ref[...].astype(F32)).astype(BF16)

    grid_spec = pltpu.PrefetchScalarGridSpec(
        num_scalar_prefetch=1, grid=(4, h // tr),
        in_specs=[pl.BlockSpec((None, None, tr, C), lambda k, i, c: (k, c[0], i, 0)),
                  pl.BlockSpec((None, tr, C), lambda k, i, c: (k, i, 0))],
        out_specs=pl.BlockSpec((None, tr, C), lambda k, i, c: (k, i, 0)))
    return pl.pallas_call(
        body, name=name, grid_spec=grid_spec, out_shape=_sds((4, h, C), BF16),
        compiler_params=_params(("parallel", "parallel")),
    )(c_idx, g4, recv)


def _sum4(name, q, p, chip_idx):
    _, h, C = q.shape
    tr = h if h <= 512 else h // (h // 256) if h % 256 == 0 else h // 2

    def body(k_ref, q_ref, p_ref, o_ref):
        me = k_ref[0]
        terms = [jnp.where(me == k, p_ref[...], q_ref[k]).astype(F32) for k in range(4)]
        o_ref[...] = ((terms[0] + terms[1]) + terms[2]) + terms[3]

    grid_spec = pltpu.PrefetchScalarGridSpec(
        num_scalar_prefetch=1, grid=(h // tr,),
        in_specs=[pl.BlockSpec((4, tr, C), lambda i, k: (0, i, 0)),
                  pl.BlockSpec((None, tr, C), lambda i, k: (k[0], i, 0))],
        out_specs=pl.BlockSpec((tr, C), lambda i, k: (i, 0)))
    return pl.pallas_call(
        body, name=name, grid_spec=grid_spec, out_shape=_sds((h, C), F32), compiler_params=_params(("parallel",)),
    )(chip_idx, q, p)


def _sum8(name, q, p, idx):
    _, _, h, C = q.shape
    tr = h if h <= 256 else 256 if h % 256 == 0 else h // 2

    def body(k_ref, q_ref, p_ref, o_ref):
        total = None
        for j in range(4):
            for cc in range(2):
                mine = (k_ref[0] == j) & (k_ref[1] == cc)
                term = jnp.where(mine, p_ref[...], q_ref[j, cc]).astype(F32)
                total = term if total is None else total + term
        o_ref[...] = total

    grid_spec = pltpu.PrefetchScalarGridSpec(
        num_scalar_prefetch=1, grid=(h // tr,),
        in_specs=[pl.BlockSpec((4, 2, tr, C), lambda i, k: (0, 0, i, 0)),
                  pl.BlockSpec((None, None, tr, C), lambda i, k: (k[0], k[1], i, 0))],
        out_specs=pl.BlockSpec((tr, C), lambda i, k: (i, 0)))
    return pl.pallas_call(
        body, name=name, grid_spec=grid_spec, out_shape=_sds((h, C), F32), compiler_params=_params(("parallel",)),
    )(idx, q, p)


def _adamw(name, w, g, m, v):
    R, C = w.shape
    tr = R
    while tr * C * 4 > (1 << 20) and tr % 16 == 0:
        tr //= 2
    c1 = 1.0 - ADAM_B1 ** ADAM_STEP
    c2 = 1.0 - ADAM_B2 ** ADAM_STEP

    def body(w_ref, g_ref, m_ref, v_ref, d_ref, nm_ref, nv_ref):
        gv = g_ref[...]
        nm = ADAM_B1 * m_ref[...] + (1.0 - ADAM_B1) * gv
        nv = ADAM_B2 * v_ref[...] + (1.0 - ADAM_B2) * (gv * gv)
        nm_ref[...] = nm
        nv_ref[...] = nv
        d_ref[...] = -ADAM_LR * ((nm * (1.0 / c1)) / (jnp.sqrt(nv * (1.0 / c2)) + ADAM_EPS) + ADAM_WD * w_ref[...])

    spec = pl.BlockSpec((tr, C), lambda i: (i, 0))
    return pl.pallas_call(
        body, name=name, grid=(R // tr,), in_specs=[spec] * 4, out_specs=[spec] * 3,
        out_shape=[_sds((R, C), F32)] * 3, compiler_params=_params(("parallel",)),
    )(w, g, m, v)


def _adamw_halves(name, w, own, sib, m, v, c_idx):
    R, C = w.shape
    h = R // 2
    tr = h
    while tr * C * 4 > (1 << 20) and tr % 16 == 0:
        tr //= 2
    nb = h // tr
    c1 = 1.0 - ADAM_B1 ** ADAM_STEP
    c2 = 1.0 - ADAM_B2 ** ADAM_STEP

    def body(c_ref, w_ref, own_ref, sib_ref, m_ref, v_ref, g_ref, d_ref, nm_ref, nv_ref):
        mine = (pl.program_id(0) // nb) == c_ref[0]
        gv = jnp.where(mine, own_ref[...], sib_ref[...])
        nm = ADAM_B1 * m_ref[...] + (1.0 - ADAM_B1) * gv
        nv = ADAM_B2 * v_ref[...] + (1.0 - ADAM_B2) * (gv * gv)
        g_ref[...] = gv
        nm_ref[...] = nm
        nv_ref[...] = nv
        d_ref[...] = -ADAM_LR * ((nm * (1.0 / c1)) / (jnp.sqrt(nv * (1.0 / c2)) + ADAM_EPS) + ADAM_WD * w_ref[...])

    spec = pl.BlockSpec((tr, C), lambda i, c: (i, 0))
    half = pl.BlockSpec((tr, C), lambda i, c: (i % nb, 0))
    grid_spec = pltpu.PrefetchScalarGridSpec(
        num_scalar_prefetch=1, grid=(R // tr,), in_specs=[spec, half, half, spec, spec], out_specs=[spec] * 4)
    return pl.pallas_call(
        body, name=name, grid_spec=grid_spec, out_shape=[_sds((R, C), F32)] * 4,
        compiler_params=_params(("parallel",)),
    )(c_idx, w, own, sib, m, v)


def _pack(g, scale, shift, gate):
    rows = jnp.stack([g, scale, shift, gate]).astype(F32)
    return jnp.concatenate([rows, jnp.zeros((4, D), F32)], axis=0)


def _fold8(vec):
    m = -(-vec.shape[0] // (8 * LANES)) * LANES
    return jnp.concatenate([vec, jnp.zeros((8 * m - vec.shape[0],), vec.dtype)]).reshape(8, m)


def _allgather_vectors(name, vec):
    return _allgather_rows(name, _fold8(vec)).reshape(8, -1)


def kernel(x, c, w_ada, b_ada, norm1_g, ffn1_w_gu, ffn1_w_down, norm2_g, w_mix_in, b_merge, conv_w, w_conv_out, w_attn_out, w_out, norm3_g, ffn2_w_gu, ffn2_w_down, final_g, loss_target, m_w_ada, m_b_ada, m_norm1_g, m_ffn1_w_gu, m_ffn1_w_down, m_norm2_g, m_w_mix_in, m_b_merge, m_conv_w, m_w_conv_out, m_w_attn_out, m_w_out, m_norm3_g, m_ffn2_w_gu, m_ffn2_w_down, m_final_g, v_w_ada, v_b_ada, v_norm1_g, v_ffn1_w_gu, v_ffn1_w_down, v_norm2_g, v_w_mix_in, v_b_merge, v_conv_w, v_w_conv_out, v_w_attn_out, v_w_out, v_norm3_g, v_ffn2_w_gu, v_ffn2_w_down, v_final_g):
    xi, yi, ci = lax.axis_index("x"), lax.axis_index("y"), lax.axis_index("c")
    chip = 2 * xi + yi
    dev = 4 * xi + 2 * yi + ci
    S = x.shape[1]
    h0 = x[0]
    target = loss_target[0]

    (wgu1,) = _allgather_weights([ffn1_w_gu[0].astype(BF16)])
    late_shards = [w[0].astype(BF16) for w in (w_conv_out, w_attn_out, w_out, ffn2_w_gu, ffn2_w_down)]
    c_idx = jnp.reshape(ci, (1,)).astype(jnp.int32)
    chip_idx = jnp.reshape(chip, (1,)).astype(jnp.int32)

    def reduce_pairs(tag, names, grads):
        g4 = [g.reshape(4, 2, g.shape[1] // 2, g.shape[2]) for g in grads]
        recv = _sibling_swap_halves("grad_sibling_swap_" + tag, g4)
        return [_pair_sum("pair_sum_" + nm, a, b, c_idx) for nm, a, b in zip(names, g4, recv)]

    small = jnp.concatenate([c[0], b_merge[0].reshape(-1), conv_w[0].reshape(-1)])
    gathered = _allgather_vectors("allgather_small", small)
    c_all = gathered[:, :D]
    per_chip = gathered[0::2]
    bm_full = jnp.concatenate([per_chip[k, D:D + 512].reshape(2, 256) for k in range(4)], axis=1)
    cw_full = jnp.concatenate([per_chip[k, D + 512:D + 896].reshape(3, 128) for k in range(4)], axis=1)
    b_l = lax.dynamic_slice_in_dim(b_ada, chip * ADA_SHARD, ADA_SHARD, axis=1)
    mod_l = _ada_fwd(c_all, w_ada[0], b_l)
    mod_g = _allgather_rows("allgather_mod", mod_l).reshape(8, 8, ADA_SHARD)
    mod_all = jnp.concatenate([mod_g[2 * k] for k in range(4)], axis=1)
    mod = lax.dynamic_slice_in_dim(mod_all, dev, 1, axis=0).reshape(3, 3, D)
    p1 = _pack(norm1_g[0], mod[0, 1], mod[0, 0], mod[0, 2])
    p2 = _pack(norm2_g[0], mod[1, 1], mod[1, 0], mod[1, 2])
    p3 = _pack(norm3_g[0], mod[2, 1], mod[2, 0], mod[2, 2])
    pf = _pack(final_g, final_g, final_g, final_g)

    u1 = _norm_mod_fwd("norm1_fwd", h0, p1)
    gu1, hm1, (wd1,) = _ffn_up("ffn1_up", u1, wgu1, [ffn1_w_down[0].astype(BF16)])
    f1, h1, u2, (wmix,) = _proj_residual("ffn1_down", hm1, wd1.reshape(D_FF, D), h0, p1, 0.5, p2,
                                         gather_shards=[w_mix_in[0].astype(BF16)])
    wd1 = wd1.reshape(D_FF, D)
    proj = _mm("mix_in", (4, S // TM), u2, pl.BlockSpec((TM, D), lambda s, i: (i, 0)),
               wmix, pl.BlockSpec((None, D, MIX_SHARD), lambda s, i: (s, 0, 0)), (1, 0),
               [_sds((S, MIX_W), BF16)], [pl.BlockSpec((TM, MIX_SHARD), lambda s, i: (i, s))], _store(BF16),
               semantics=("parallel", "parallel"))[0]
    sc = _conv_fwd(proj, cw_full)
    o, t_tot, (wco, wao, wout, wgu2, wd2) = _attn_fwd(proj, late_shards)
    wout = wout.reshape(D, D)
    wd2 = wd2.reshape(D_FF, D)
    ya, yb, merged = _merge_fwd(sc, o, wco, wao, proj, bm_full)
    y2, h2, u3, _ = _proj_residual("mix_out", merged, wout, h1, p2, 1.0, p3)
    gu3, hm3, _ = _ffn_up("ffn2_up", u3, wgu2)

    dh3, df3, sums_f, loss_blk = _proj_residual_loss("ffn2_down", hm3, wd2, h2, p3, 0.5, pf, target)
    (dh2, dy2, sums3), dwgu2, dwd2 = _ffn_bwd("ffn2", df3, u3, gu3, hm3, wgu2, wd2,
                                              _norm_bwd_tail(h2, p3, dh3, prev=(p2, y2, 1.0)))

    dya, dyb, dga, dgb, sums_bm = _merge_bwd(dy2, wout, ya, yb, proj, bm_full)
    tk = _tk(S)
    dwout = _mm("dw_out", (1, S // tk), merged, pl.BlockSpec((tk, D), lambda n, k: (k, 0)),
                dy2, pl.BlockSpec((tk, D), lambda n, k: (k, 0)), (0, 0),
                [_sds((D, D), BF16)], [pl.BlockSpec((D, D), lambda n, k: (0, 0))], _store(BF16),
                nk=S // tk, acc_shape=(D, D))[0]
    dsc, do = _mixer_out_bwd(dya, dyb, wco, wao)
    dwco, dwao = _mixer_out_dw(sc, o, dya, dyb)
    dcb, dcc, dcx, dcw = _conv_bwd(dsc, proj, cw_full)
    names_e = ["ffn2_w_gu", "ffn2_w_down", "w_out", "w_conv_out", "w_attn_out"]
    part_e = [g.reshape(4, 2, g.shape[1] // 2, g.shape[2])
              for g in (dwgu2, dwd2.reshape(4, 704, D), dwout.reshape(4, 256, D), dwco, dwao)]
    dq, dk, dv, came_e = _attn_bwd(proj, t_tot, do, part_e)
    dproj = jnp.concatenate([dcb, dcc, dcx, dq, dk, dv, dga, dgb], axis=1)
    tail2 = _norm_bwd_tail(h1, p2, dh2, prev=(p1, f1, 0.5))
    dh1, df1, sums2 = _mm("mix_in_bwd", (S // TM, 4), dproj, pl.BlockSpec((TM, MIX_SHARD), lambda i, s: (i, s)),
                          wmix, pl.BlockSpec((None, D, MIX_SHARD), lambda i, s: (s, 0, 0)), (1, 1),
                          tail2["out_shapes"], tail2["out_specs"], tail2["epilogue"], extras=tail2["extras"],
                          extra_specs=tail2["extra_specs"], nk=4, acc_shape=(TM, D),
                          semantics=("arbitrary", "arbitrary"))
    dwmix = _grad_w("dw_mix_in", u2, D, dproj, MIX_SHARD, lambda s, k: (k, s), 4, (4, D, MIX_SHARD),
                    (None, D, MIX_SHARD), lambda s, k: (s, 0, 0))

    part_mix = reduce_pairs("mix", ["w_mix_in"], [dwmix])
    dgu1, *came_mix = _ffn_down_bwd("ffn1_down_bwd", df1, wd1, gu1, a2a_parts=part_mix)
    dwd1 = _ffn_dw_down("ffn1_dw_down", hm1, df1)
    part_wd1 = reduce_pairs("wd1", ["ffn1_w_down"], [dwd1.reshape(4, 704, D)])
    dwgu1, *came_wd1 = _ffn_dw_gu("ffn1_dw_gu", u1, dgu1, a2a_parts=part_wd1)
    part_gu1 = reduce_pairs("gu1", ["ffn1_w_gu"], [dwgu1])
    grad_x, sums1, *came_gu1 = _ffn_up_bwd("ffn1_up_bwd", dgu1, wgu1, _norm_bwd_tail(h0, p1, dh1),
                                           a2a_parts=part_gu1)

    dmod = jnp.stack([sums1[0], sums1[1], sums2[3], sums2[0], sums2[1], sums3[3], sums3[0], sums3[1], sums_f[1]])
    small_g = jnp.concatenate([dmod.reshape(-1), sums1[2], sums2[2], sums3[2], sums_f[0],
                               sums_bm[0], sums_bm[1], dcw.reshape(-1), loss_blk[0, 0:1]])
    all_g = _allgather_vectors("allgather_small_grads", small_g)
    tot = _sum_rows("sum_small_grads", all_g)[0]
    loss = tot[16 * D + 512]
    g_b_ada = tot[:9 * D][None, :]
    g_n1, g_n2, g_n3 = (tot[(9 + k) * D:(10 + k) * D][None, :] for k in range(3))
    g_fin = tot[12 * D:13 * D]
    g_bm = lax.dynamic_slice_in_dim(tot[13 * D:15 * D].reshape(2, D), chip * 256, 256, axis=1)[None]
    g_cw = lax.dynamic_slice_in_dim(tot[15 * D:16 * D + 512].reshape(3, 512), chip * 128, 128, axis=1)[None]
    dmod_l = lax.dynamic_slice_in_dim(all_g[:, :9 * D], chip * ADA_SHARD, ADA_SHARD, axis=1)
    g_w_ada = _ada_bwd(c_all.T, dmod_l)[None]

    names_l = ["w_mix_in", "ffn1_w_down", "ffn1_w_gu"]
    names = names_e + names_l
    place_idx = jnp.stack([chip, ci]).astype(jnp.int32)
    half = [_sum8("device_sum_" + nm, q, p, place_idx) for nm, q, p in zip(names_e, came_e, part_e)]
    half += [_sum4("chip_sum_" + nm, q, p, chip_idx)
             for nm, q, p in zip(names_l, came_mix + came_wd1 + came_gu1, part_mix + part_wd1 + part_gu1)]
    g_own = dict(zip(names, half))
    g_sib = dict(zip(names, _sibling_share(half)))

    weights = dict(w_ada=w_ada, b_ada=b_ada, norm1_g=norm1_g, ffn1_w_gu=ffn1_w_gu, ffn1_w_down=ffn1_w_down,
                   norm2_g=norm2_g, w_mix_in=w_mix_in, b_merge=b_merge, conv_w=conv_w, w_conv_out=w_conv_out,
                   w_attn_out=w_attn_out, w_out=w_out, norm3_g=norm3_g, ffn2_w_gu=ffn2_w_gu,
                   ffn2_w_down=ffn2_w_down, final_g=final_g)
    ms = dict(w_ada=m_w_ada, b_ada=m_b_ada, norm1_g=m_norm1_g, ffn1_w_gu=m_ffn1_w_gu, ffn1_w_down=m_ffn1_w_down,
              norm2_g=m_norm2_g, w_mix_in=m_w_mix_in, b_merge=m_b_merge, conv_w=m_conv_w, w_conv_out=m_w_conv_out,
              w_attn_out=m_w_attn_out, w_out=m_w_out, norm3_g=m_norm3_g, ffn2_w_gu=m_ffn2_w_gu,
              ffn2_w_down=m_ffn2_w_down, final_g=m_final_g)
    vs = dict(w_ada=v_w_ada, b_ada=v_b_ada, norm1_g=v_norm1_g, ffn1_w_gu=v_ffn1_w_gu, ffn1_w_down=v_ffn1_w_down,
              norm2_g=v_norm2_g, w_mix_in=v_w_mix_in, b_merge=v_b_merge, conv_w=v_conv_w, w_conv_out=v_w_conv_out,
              w_attn_out=v_w_attn_out, w_out=v_w_out, norm3_g=v_norm3_g, ffn2_w_gu=v_ffn2_w_gu,
              ffn2_w_down=v_ffn2_w_down, final_g=v_final_g)
    order = list(weights)
    grad = dict(w_ada=g_w_ada, b_ada=g_b_ada, norm1_g=g_n1, norm2_g=g_n2, norm3_g=g_n3, final_g=g_fin,
                b_merge=g_bm, conv_w=g_cw)
    delta, new_m, new_v = {}, {}, {}
    small_names = ["b_ada", "norm1_g", "norm2_g", "norm3_g", "final_g", "b_merge", "conv_w"]
    flat = lambda d: jnp.concatenate([d[nm].reshape(-1) for nm in small_names])[None, :]
    sd, sm, sv = _adamw("adamw_small", flat(weights), flat(grad), flat(ms), flat(vs))
    off = 0
    for nm in small_names:
        size = weights[nm].size
        for dst, src in ((delta, sd), (new_m, sm), (new_v, sv)):
            dst[nm] = src[0, off:off + size].reshape(weights[nm].shape)
        off += size
    for nm in order:
        if nm in small_names:
            continue
        shp = weights[nm].shape
        if nm in g_own:
            g2, d2, m2, v2 = _adamw_halves("adamw_" + nm, weights[nm][0], g_own[nm], g_sib[nm], ms[nm][0], vs[nm][0],
                                           c_idx)
            grad[nm] = g2.reshape(shp)
        else:
            d2, m2, v2 = _adamw("adamw_" + nm, weights[nm][0], grad[nm][0], ms[nm][0], vs[nm][0])
        delta[nm], new_m[nm], new_v[nm] = d2.reshape(shp), m2.reshape(shp), v2.reshape(shp)

    return (loss, grad_x[None], *[grad[nm] for nm in order], *[delta[nm] for nm in order],
            *[new_m[nm] for nm in order], *[new_v[nm] for nm in order])
```

```python
import functools

import jax
import jax.numpy as jnp
from jax import lax
from jax.experimental import pallas as pl
from jax.experimental.pallas import tpu as pltpu

F32 = jnp.float32
BF16 = jnp.bfloat16
MESH = pl.DeviceIdType.MESH

VMEM_LIMIT = 56 * 1024 * 1024
LANES = 128

D = 1024
D_FF = 2816
FF_SHARD = 1408
MIX_SHARD = 1280
MIX_W = 5120
HEAD_PAIRS = 4
HEAD_DIM = 64
CONV_W = 512
EPS = 1e-6
ATT_BLK = 256

ADAM_LR = 0.001
ADAM_B1 = 0.9
ADAM_B2 = 0.999
ADAM_EPS = 1e-08
ADAM_WD = 0.01
ADAM_STEP = 10


def _params(semantics=None):
    return pltpu.CompilerParams(dimension_semantics=semantics, vmem_limit_bytes=VMEM_LIMIT)


def _sigmoid(x):
    return 1.0 / (1.0 + jnp.exp(-x))


def _place():
    x, y, c = lax.axis_index("x"), lax.axis_index("y"), lax.axis_index("c")
    chips = [(1 - x, y), (x, 1 - y), (1 - x, 1 - y)]
    return x, y, c, chips


def _allgather_rows(name, blk):
    m_per, n = blk.shape

    def body(x_ref, out_ref, send_sems, recv_sems, local_sem):
        x, y, c, chips = _place()
        me, sibling = (x, y, c), (x, y, 1 - c)

        def rows(px, py, pc):
            return out_ref.at[pl.ds((4 * px + 2 * py + pc) * m_per, m_per), :]

        def copy(k, block, to, src=None):
            return pltpu.make_async_remote_copy(
                src_ref=rows(*block) if src is None else src, dst_ref=rows(*block),
                send_sem=send_sems.at[k], recv_sem=recv_sems.at[k], device_id=to, device_id_type=MESH)

        mine = pltpu.make_async_copy(x_ref, rows(*me), local_sem)
        mine.start()
        first = [copy(0, me, sibling, src=x_ref)]
        first += [copy(1 + j, me, (*chip, c), src=x_ref) for j, chip in enumerate(chips)]
        for cp in first:
            cp.start()
        passed = [copy(4 + j, (*chip, c), sibling) for j, chip in enumerate(chips)]
        for j, chip in enumerate(chips):
            copy(1 + j, (*chip, c), me).wait_recv()
            passed[j].start()
        copy(0, sibling, me).wait_recv()
        for j, chip in enumerate(chips):
            copy(4 + j, (*chip, 1 - c), me).wait_recv()
        for cp in first + passed:
            cp.wait_send()
        mine.wait()

    return pl.pallas_call(
        body, name=name,
        out_shape=jax.ShapeDtypeStruct((8 * m_per, n), blk.dtype),
        in_specs=[pl.BlockSpec(memory_space=pltpu.VMEM)],
        out_specs=pl.BlockSpec(memory_space=pltpu.VMEM),
        scratch_shapes=[pltpu.SemaphoreType.DMA((7,)), pltpu.SemaphoreType.DMA((7,)), pltpu.SemaphoreType.DMA],
        compiler_params=pltpu.CompilerParams(vmem_limit_bytes=VMEM_LIMIT),
    )(blk)


def _hbm_specs(n):
    return [pl.BlockSpec(memory_space=pltpu.HBM)] * n


def _allgather_weights(shards):
    n = len(shards)

    def body(*refs):
        start, relay, finish = _gather_protocol(refs[:n], refs[n:2 * n], *refs[2 * n:])
        start()
        relay()
        finish()

    gathered = pl.pallas_call(
        body, name="allgather_weights",
        out_shape=_gather_shapes(shards), in_specs=_hbm_specs(n), out_specs=_hbm_specs(n),
        scratch_shapes=_gather_sems(n),
    )(*shards)
    return _with_own_shard(gathered, shards)


def _gather_shapes(shards):
    return [jax.ShapeDtypeStruct((4, *s.shape), s.dtype) for s in shards]


def _gather_sems(n):
    return [pltpu.SemaphoreType.DMA((6 * n,)), pltpu.SemaphoreType.DMA((6 * n,))]


def _with_own_shard(gathered, shards):
    chip = 2 * lax.axis_index("x") + lax.axis_index("y")
    return [lax.dynamic_update_slice(g, s[None], (chip, 0, 0)) for g, s in zip(gathered, shards)]


def _gather_protocol(ins, outs, send_sems, recv_sems):
    n = len(ins)
    x, y, c, chips = _place()
    me, sibling = (x, y, c), (x, y, 1 - c)
    me_k = 2 * x + y

    def half(w, k, hc):
        h = ins[w].shape[0] // 2
        return outs[w].at[k, pl.ds(pl.multiple_of(hc * h, 8), h), :]

    def copy(w, j, k, hc, to, src=None):
        dst = half(w, k, hc)
        return pltpu.make_async_remote_copy(
            src_ref=dst if src is None else src, dst_ref=dst,
            send_sem=send_sems.at[6 * w + j], recv_sem=recv_sems.at[6 * w + j],
            device_id=to, device_id_type=MESH)

    def first(w, j):
        h = ins[w].shape[0] // 2
        src = ins[w].at[pl.ds(pl.multiple_of(c * h, 8), h), :]
        return copy(w, j, me_k, c, (*chips[j], c), src=src)

    def passed(w, j):
        px, py = chips[j]
        return copy(w, 3 + j, 2 * px + py, c, sibling)

    pairs = [(w, j) for w in range(n) for j in range(3)]

    def start():
        for w, j in pairs:
            first(w, j).start()

    def relay():
        for w, j in pairs:
            px, py = chips[j]
            copy(w, j, 2 * px + py, c, me).wait_recv()
            passed(w, j).start()

    def finish():
        for w, j in pairs:
            px, py = chips[j]
            copy(w, 3 + j, 2 * px + py, 1 - c, me).wait_recv()
        for w, j in pairs:
            first(w, j).wait_send()
            passed(w, j).wait_send()

    return start, relay, finish


def _sibling_swap_halves(name, grads):
    n = len(grads)

    def body(*refs):
        ins, outs = refs[:n], refs[n:2 * n]
        send_sems, recv_sems = refs[2 * n:]
        x, y, c, _ = _place()
        cps = []
        for w in range(n):
            cp = pltpu.make_async_remote_copy(
                src_ref=ins[w].at[:, 1 - c], dst_ref=outs[w],
                send_sem=send_sems.at[w], recv_sem=recv_sems.at[w],
                device_id=(x, y, 1 - c), device_id_type=MESH)
            cp.start()
            cps.append(cp)
        for cp in cps:
            cp.wait()

    return pl.pallas_call(
        body, name=name,
        out_shape=[jax.ShapeDtypeStruct((4, *g.shape[2:]), g.dtype) for g in grads],
        in_specs=_hbm_specs(n), out_specs=_hbm_specs(n),
        scratch_shapes=[pltpu.SemaphoreType.DMA((n,)), pltpu.SemaphoreType.DMA((n,))],
    )(*grads)


def _all_to_all_sems(n):
    return [pltpu.SemaphoreType.DMA((3 * n,)), pltpu.SemaphoreType.DMA((3 * n,))]


def _all_to_all_protocol(ins, outs, send_sems, recv_sems):
    n = len(ins)
    x, y, c, chips = _place()
    me_k = 2 * x + y
    pairs = [(w, j) for w in range(n) for j in range(3)]

    def sent(w, j):
        px, py = chips[j]
        return pltpu.make_async_remote_copy(
            src_ref=ins[w].at[2 * px + py], dst_ref=outs[w].at[me_k],
            send_sem=send_sems.at[3 * w + j], recv_sem=recv_sems.at[3 * w + j],
            device_id=(px, py, c), device_id_type=MESH)

    def start():
        for w, j in pairs:
            sent(w, j).start()

    def finish():
        for w, j in pairs:
            px, py = chips[j]
            slab = outs[w].at[2 * px + py]
            pltpu.make_async_remote_copy(
                src_ref=slab, dst_ref=slab, send_sem=send_sems.at[3 * w + j],
                recv_sem=recv_sems.at[3 * w + j], device_id=(px, py, c), device_id_type=MESH).wait_recv()
        for w, j in pairs:
            sent(w, j).wait_send()

    return start, finish


def _reduce8_sems(n):
    return [pltpu.SemaphoreType.DMA((7 * n,)), pltpu.SemaphoreType.DMA((7 * n,))]


def _reduce8_protocol(ins, outs, send_sems, recv_sems):
    n = len(ins)
    x, y, c, chips = _place()
    me_k = 2 * x + y
    far = [(w, j, hc) for w in range(n) for j in range(3) for hc in range(2)]

    def sent(w, j, hc):
        px, py = chips[j]
        return pltpu.make_async_remote_copy(
            src_ref=ins[w].at[2 * px + py, hc], dst_ref=outs[w].at[me_k, c],
            send_sem=send_sems.at[7 * w + 2 * j + hc], recv_sem=recv_sems.at[7 * w + 2 * j + c],
            device_id=(px, py, hc), device_id_type=MESH)

    def to_sibling(w):
        return pltpu.make_async_remote_copy(
            src_ref=ins[w].at[me_k, 1 - c], dst_ref=outs[w].at[me_k, c],
            send_sem=send_sems.at[7 * w + 6], recv_sem=recv_sems.at[7 * w + 6],
            device_id=(x, y, 1 - c), device_id_type=MESH)

    def arrival(w, slab, k):
        return pltpu.make_async_remote_copy(
            src_ref=slab, dst_ref=slab, send_sem=send_sems.at[7 * w + k], recv_sem=recv_sems.at[7 * w + k],
            device_id=(x, y, c), device_id_type=MESH)

    def start():
        for w in range(n):
            to_sibling(w).start()
        for w, j, hc in far:
            sent(w, j, hc).start()

    def finish():
        for w in range(n):
            arrival(w, outs[w].at[me_k, 1 - c], 6).wait_recv()
        for w, j, cc in far:
            px, py = chips[j]
            arrival(w, outs[w].at[2 * px + py, cc], 2 * j + cc).wait_recv()
        for w in range(n):
            to_sibling(w).wait_send()
        for w, j, hc in far:
            sent(w, j, hc).wait_send()

    return start, finish


def _sibling_share(halves):
    n = len(halves)

    def body(*refs):
        ins, outs = refs[:n], refs[n:2 * n]
        send_sems, recv_sems = refs[2 * n:]
        x, y, c, _ = _place()
        cps = []
        for w in range(n):
            cp = pltpu.make_async_remote_copy(
                src_ref=ins[w], dst_ref=outs[w], send_sem=send_sems.at[w], recv_sem=recv_sems.at[w],
                device_id=(x, y, 1 - c), device_id_type=MESH)
            cp.start()
            cps.append(cp)
        for cp in cps:
            cp.wait()

    return pl.pallas_call(
        body, name="grad_sibling_share",
        out_shape=[jax.ShapeDtypeStruct(p.shape, p.dtype) for p in halves],
        in_specs=_hbm_specs(n), out_specs=_hbm_specs(n),
        scratch_shapes=[pltpu.SemaphoreType.DMA((n,)), pltpu.SemaphoreType.DMA((n,))],
    )(*halves)


def _mm(name, grid, a, a_spec, b, b_spec, contract, out_shapes, out_specs, epilogue,
        extras=(), extra_specs=(), nk=1, acc_shape=None, semantics=None, a2a_parts=(), gather_shards=(),
        relay_at=None, prod_fn=None):
    assert not (a2a_parts and gather_shards)
    moved = tuple(a2a_parts) + tuple(gather_shards)
    ne, no, nc = len(extras), len(out_shapes), len(moved)
    nd = len(grid)

    def body(*refs):
        a_ref, b_ref = refs[0], refs[1]
        ex, outs = refs[2:2 + ne], refs[2 + ne + nc:2 + ne + nc + no]
        if nc:
            ids = [pl.program_id(d) for d in range(nd)]
            comm_refs = (refs[2 + ne:2 + ne + nc], refs[2 + ne + nc + no:2 + ne + 2 * nc + no], *refs[-2:])
            at_start = functools.reduce(jnp.logical_and, [i == 0 for i in ids])
            if a2a_parts:
                start, finish = _all_to_all_protocol(*comm_refs)
                pl.when(at_start)(start)
            else:
                start, relay, finish = _gather_protocol(*comm_refs)
                pl.when(at_start)(start)
                at_relay = grid[0] // 2 if relay_at is None else relay_at
                pl.when(functools.reduce(jnp.logical_and, [ids[0] == at_relay] + [i == 0 for i in ids[1:]]))(relay)

        def prod():
            if prod_fn is not None:
                return prod_fn(a_ref, b_ref)
            return lax.dot_general(a_ref[...], b_ref[...], (((contract[0],), (contract[1],)), ((), ())),
                                   preferred_element_type=F32)

        if nk == 1:
            epilogue(prod(), ex, outs)
        else:
            acc = refs[2 + ne + 2 * nc + no]
            k = pl.program_id(nd - 1)

            @pl.when(k == 0)
            def _():
                acc[...] = prod()

            @pl.when(k > 0)
            def _():
                acc[...] += prod()

            @pl.when(k == nk - 1)
            def _():
                epilogue(acc[...], ex, outs)

        if nc:
            pl.when(functools.reduce(jnp.logical_and, [i == g - 1 for i, g in zip(ids, grid)]))(finish)

    if semantics is None or nc:
        semantics = ("arbitrary",) * nd
    return pl.pallas_call(
        body, name=name, grid=grid,
        in_specs=[a_spec, b_spec, *extra_specs] + _hbm_specs(nc),
        out_specs=list(out_specs) + _hbm_specs(nc),
        out_shape=list(out_shapes) + [jax.ShapeDtypeStruct(p.shape, p.dtype) for p in a2a_parts]
        + _gather_shapes(gather_shards),
        scratch_shapes=([] if nk == 1 else [pltpu.VMEM(acc_shape, F32)])
        + (_all_to_all_sems(nc) if a2a_parts else _gather_sems(nc) if gather_shards else []),
        compiler_params=_params(semantics),
    )(a, b, *extras, *moved)


def _store(dtype):
    def epilogue(acc, ex, outs):
        outs[0][...] = acc.astype(dtype)
    return epilogue


def _sds(shape, dtype):
    return jax.ShapeDtypeStruct(shape, dtype)


TR = 512


def _row_spec(width, tr=TR):
    return pl.BlockSpec((tr, width), lambda i: (i, 0))


def _const_spec(shape):
    nd = len(shape)
    return pl.BlockSpec(shape, lambda i: (0,) * nd)


def _norm_mod_fwd(name, h, p):
    S = h.shape[0]

    def body(h_ref, p_ref, u_ref):
        hv = h_ref[...]
        r = lax.rsqrt(jnp.mean(hv * hv, axis=-1, keepdims=True) + EPS)
        nrm = (hv * r) * p_ref[0:1, :]
        u_ref[...] = (nrm * (1.0 + p_ref[1:2, :]) + p_ref[2:3, :]).astype(BF16)

    return pl.pallas_call(
        body, name=name, grid=(S // TR,),
        in_specs=[_row_spec(D), _const_spec((8, D))], out_specs=_row_spec(D),
        out_shape=_sds((S, D), BF16), compiler_params=_params(("parallel",)),
    )(h, p)


def _rmsnorm_parts(hv):
    r = lax.rsqrt(jnp.mean(hv * hv, axis=-1, keepdims=True) + EPS)
    return r, hv * r


TB = 256


def _norm_bwd_tail(h, p, dh_res, prev=None):
    S = h.shape[0]
    row = pl.BlockSpec((TB, D), lambda i: (i, 0))
    const = pl.BlockSpec((8, D), lambda i: (0, 0))
    extras, specs = [h, p, dh_res], [row, const, row]
    out_shapes, out_specs = [_sds((S, D), F32)], [row]
    if prev is not None:
        extras += [prev[0], prev[1]]
        specs += [const, row]
        out_shapes.append(_sds((S, D), BF16))
        out_specs.append(row)
    out_shapes.append(_sds((8, D), F32))
    out_specs.append(const)

    def epilogue(duv, ex, outs):
        h_ref, p_ref, r_ref = ex[:3]
        dh_ref, sums_ref = outs[0], outs[-1]

        @pl.when(pl.program_id(0) == 0)
        def _():
            sums_ref[...] = jnp.zeros_like(sums_ref)

        g = p_ref[0:1, :]
        r, xn = _rmsnorm_parts(h_ref[...])
        dn = duv * (1.0 + p_ref[1:2, :])
        dxn = dn * g
        dh = r_ref[...] + r * (dxn - xn * jnp.mean(dxn * xn, axis=-1, keepdims=True))
        dh_ref[...] = dh
        sums_ref[0:1, :] += jnp.sum(duv, axis=0, keepdims=True)
        sums_ref[1:2, :] += jnp.sum(duv * (xn * g), axis=0, keepdims=True)
        sums_ref[2:3, :] += jnp.sum(dn * xn, axis=0, keepdims=True)
        if prev is not None:
            pp_ref, f_ref = ex[3:5]
            outs[1][...] = (prev[2] * pp_ref[3:4, :] * dh).astype(BF16)
            sums_ref[3:4, :] += prev[2] * jnp.sum(dh * f_ref[...].astype(F32), axis=0, keepdims=True)

    return dict(extras=tuple(extras), extra_specs=tuple(specs), out_shapes=out_shapes, out_specs=out_specs,
                epilogue=epilogue)


TM = 512


def _ffn_up(name, u, wgu4, shards=()):
    S = u.shape[0]
    n = len(shards)
    ni = S // TM

    def body(u_ref, wg_ref, wu_ref, *rest):
        gu_ref, hm_ref = rest[n:n + 2]
        s, i = pl.program_id(0), pl.program_id(1)
        if n:
            start, relay, finish = _gather_protocol(rest[:n], rest[n + 2:2 * n + 2], *rest[2 * n + 2:])
            pl.when((s == 0) & (i == 0))(start)
            pl.when((s == 1) & (i == 0))(relay)
        uv = u_ref[...]
        g = jnp.dot(uv, wg_ref[...], preferred_element_type=F32)
        up = jnp.dot(uv, wu_ref[...], preferred_element_type=F32)
        gu_ref[0] = g.astype(BF16)
        gu_ref[1] = up.astype(BF16)
        hm_ref[...] = (g * _sigmoid(g) * up).astype(BF16)
        if n:
            pl.when((s == 1) & (i == ni - 1))(finish)

    gu, hm, *gathered = pl.pallas_call(
        body, name=name, grid=(2, ni),
        in_specs=[pl.BlockSpec((TM, D), lambda s, i: (i, 0)),
                  pl.BlockSpec((None, D, FF_SHARD), lambda s, i: (s, 0, 0)),
                  pl.BlockSpec((None, D, FF_SHARD), lambda s, i: (s + 2, 0, 0))] + _hbm_specs(n),
        out_specs=[pl.BlockSpec((2, TM, FF_SHARD), lambda s, i: (0, i, s)),
                   pl.BlockSpec((TM, FF_SHARD), lambda s, i: (i, s))] + _hbm_specs(n),
        out_shape=[_sds((2, S, D_FF), BF16), _sds((S, D_FF), BF16)] + _gather_shapes(shards),
        scratch_shapes=_gather_sems(n) if n else [],
        compiler_params=_params(("arbitrary", "arbitrary") if n else ("parallel", "parallel")),
    )(u, wgu4, wgu4, *shards)
    return gu, hm, _with_own_shard(gathered, shards)


def _proj_residual(name, a, w, h, p, weight, p_next, gather_shards=()):
    S, K = a.shape

    def epilogue(acc, ex, outs):
        h_ref, p_ref, pn_ref = ex
        outs[0][...] = acc.astype(BF16)
        hout = h_ref[...] + weight * p_ref[3:4, :] * acc
        outs[1][...] = hout
        _, xn = _rmsnorm_parts(hout)
        outs[2][...] = ((xn * pn_ref[0:1, :]) * (1.0 + pn_ref[1:2, :]) + pn_ref[2:3, :]).astype(BF16)

    row = _row_spec(D, TM)
    res = _mm(
        name, (S // TM,), a, pl.BlockSpec((TM, K), lambda i: (i, 0)), w, pl.BlockSpec((K, D), lambda i: (0, 0)),
        (1, 0), [_sds((S, D), BF16), _sds((S, D), F32), _sds((S, D), BF16)], [row, row, row], epilogue,
        extras=(h, p, p_next), extra_specs=(row, _const_spec((8, D)), _const_spec((8, D))),
        semantics=("parallel",), gather_shards=gather_shards, relay_at=S // TM - 1)
    return res[0], res[1], res[2], _with_own_shard(res[3:], gather_shards)


def _proj_residual_loss(name, a, w, h, p, weight, gf, target):
    S, K = a.shape

    def epilogue(acc, ex, outs):
        h_ref, p_ref, g_ref, t_ref = ex
        dh_ref, df_ref, sums_ref, loss_ref = outs

        @pl.when(pl.program_id(0) == 0)
        def _():
            sums_ref[...] = jnp.zeros_like(sums_ref)
            loss_ref[...] = jnp.zeros_like(loss_ref)

        gate = p_ref[3:4, :]
        g = g_ref[0:1, :]
        r, xn = _rmsnorm_parts(h_ref[...] + weight * gate * acc)
        err = xn * g - t_ref[...]
        loss_ref[...] += 0.5 * jnp.sum(err * err) * (1.0 / D)
        dout = err * (1.0 / D)
        dxn = dout * g
        dh = r * (dxn - xn * jnp.mean(dxn * xn, axis=-1, keepdims=True))
        dh_ref[...] = dh
        df_ref[...] = (weight * gate * dh).astype(BF16)
        sums_ref[0:1, :] += jnp.sum(dout * xn, axis=0, keepdims=True)
        sums_ref[1:2, :] += weight * jnp.sum(dh * acc, axis=0, keepdims=True)

    row = _row_spec(D, TM)
    return _mm(
        name, (S // TM,), a, pl.BlockSpec((TM, K), lambda i: (i, 0)), w, pl.BlockSpec((K, D), lambda i: (0, 0)),
        (1, 0), [_sds((S, D), F32), _sds((S, D), BF16), _sds((8, D), F32), _sds((8, LANES), F32)],
        [row, row, _const_spec((8, D)), _const_spec((8, LANES))], epilogue,
        extras=(h, p, gf, target), extra_specs=(row, _const_spec((8, D)), _const_spec((8, D)), row),
        semantics=("arbitrary",))


def _ffn_down_bwd(name, df, wd, gu, a2a_parts=()):
    S = df.shape[0]

    def epilogue(acc, ex, outs):
        g = ex[0][0].astype(F32)
        up = ex[0][1].astype(F32)
        sg = _sigmoid(g)
        outs[0][0] = (acc * up * (sg * (1.0 + g * (1.0 - sg)))).astype(BF16)
        outs[0][1] = (acc * g * sg).astype(BF16)

    gu_spec = pl.BlockSpec((2, TM, FF_SHARD), lambda n, i: (0, i, n))
    return _mm(
        name, (2, S // TM), df, pl.BlockSpec((TM, D), lambda n, i: (i, 0)),
        wd, pl.BlockSpec((FF_SHARD, D), lambda n, i: (n, 0)), (1, 1),
        [_sds((2, S, D_FF), BF16)], [gu_spec], epilogue, extras=(gu,), extra_specs=(gu_spec,),
        semantics=("parallel", "parallel"), a2a_parts=a2a_parts)


TK = 2048


def _tk(S):
    return min(TK, S)


def _grad_w(name, a, a_w, b, b_w, b_map, n_out, out_shape, out_block, out_map, a2a_parts=()):
    S = a.shape[0]
    tk = _tk(S)
    nk = S // tk
    res = _mm(
        name, (n_out, nk), a, pl.BlockSpec((tk, a_w), lambda s, k: (k, 0)), b, pl.BlockSpec(
            (None, tk, b_w) if b.ndim == 3 else (tk, b_w), b_map), (0, 0),
        [_sds(out_shape, BF16)], [pl.BlockSpec(out_block, out_map)], _store(BF16), nk=nk, acc_shape=(a_w, b_w),
        semantics=("parallel", "arbitrary"), a2a_parts=a2a_parts)
    return res if a2a_parts else res[0]


def _ffn_dw_down(name, hm, df):
    S = df.shape[0]
    tk = _tk(S)
    return _mm(
        name, (2, S // tk), hm, pl.BlockSpec((tk, FF_SHARD), lambda m, k: (k, m)),
        df, pl.BlockSpec((tk, D), lambda m, k: (k, 0)), (0, 0),
        [_sds((D_FF, D), BF16)], [pl.BlockSpec((FF_SHARD, D), lambda m, k: (m, 0))], _store(BF16),
        nk=S // tk, acc_shape=(FF_SHARD, D), semantics=("parallel", "arbitrary"))[0]


def _ffn_up_bwd(name, dgu, wgu4, tail, a2a_parts=()):
    S = dgu.shape[1]

    def prod_fn(a_ref, b_ref):
        acc = None
        for s in range(4):
            cols = slice((s % 2) * FF_SHARD, (s % 2 + 1) * FF_SHARD)
            term = _nt(a_ref[s // 2, :, cols], b_ref[s])
            acc = term if acc is None else acc + term
        return acc

    return _mm(
        name, (S // TB,), dgu, pl.BlockSpec((2, TB, D_FF), lambda i: (0, i, 0)),
        wgu4, pl.BlockSpec((4, D, FF_SHARD), lambda i: (0, 0, 0)), (1, 1),
        tail["out_shapes"], tail["out_specs"], tail["epilogue"], extras=tail["extras"],
        extra_specs=tail["extra_specs"], semantics=("arbitrary",), a2a_parts=a2a_parts, prod_fn=prod_fn)


def _ffn_dw_gu(name, u_in, dgu, a2a_parts=()):
    return _grad_w(name, u_in, D, dgu, FF_SHARD, lambda s, k: (s // 2, k, s % 2), 4,
                   (4, D, FF_SHARD), (None, D, FF_SHARD), lambda s, k: (s, 0, 0), a2a_parts=a2a_parts)


def _ffn_bwd(tag, df, u_in, gu, hm, wgu4, wd, tail):
    dgu = _ffn_down_bwd(tag + "_down_bwd", df, wd, gu)[0]
    dwd = _ffn_dw_down(tag + "_dw_down", hm, df)
    res = _ffn_up_bwd(tag + "_up_bwd", dgu, wgu4, tail)
    dwgu = _ffn_dw_gu(tag + "_dw_gu", u_in, dgu)
    return res, dwgu, dwd


def _shift_down(v, k, row):
    return jnp.where(row >= k, pltpu.roll(v, k, axis=0), 0.0)


def _shift_up(v, k, row, S):
    return jnp.where(row < S - k, pltpu.roll(v, S - k, axis=0), 0.0)


def _conv_specs(S):
    cols = CONV_W // LANES
    return [pl.BlockSpec((S, LANES), functools.partial(lambda j, off: (0, off + j), off=o * cols))
            for o in range(3)]


def _conv_fwd(proj, conv_w):
    S = proj.shape[0]

    def body(cb_ref, cc_ref, cx_ref, w_ref, sc_ref):
        row = lax.broadcasted_iota(jnp.int32, (S, LANES), 0)
        v = cc_ref[...].astype(F32) * cx_ref[...].astype(F32)
        yv = w_ref[0:1, :] * _shift_down(v, 2, row) + w_ref[1:2, :] * _shift_down(v, 1, row) + w_ref[2:3, :] * v
        sc_ref[...] = (cb_ref[...].astype(F32) * yv).astype(BF16)

    return pl.pallas_call(
        body, name="conv_fwd", grid=(CONV_W // LANES,),
        in_specs=_conv_specs(S) + [pl.BlockSpec((3, LANES), lambda j: (0, j))],
        out_specs=pl.BlockSpec((S, LANES), lambda j: (0, j)), out_shape=_sds((S, CONV_W), BF16),
        compiler_params=_params(("parallel",)),
    )(proj, proj, proj, conv_w)


def _conv_bwd(dsc, proj, conv_w):
    S = proj.shape[0]

    def body(d_ref, cb_ref, cc_ref, cx_ref, w_ref, dcb_ref, dcc_ref, dcx_ref, dw_ref):
        row = lax.broadcasted_iota(jnp.int32, (S, LANES), 0)
        cc = cc_ref[...].astype(F32)
        cx = cx_ref[...].astype(F32)
        d = d_ref[...].astype(F32)
        v = cc * cx
        v1 = _shift_down(v, 1, row)
        v2 = _shift_down(v, 2, row)
        w0, w1, w2 = w_ref[0:1, :], w_ref[1:2, :], w_ref[2:3, :]
        dcb_ref[...] = (d * (w0 * v2 + w1 * v1 + w2 * v)).astype(BF16)
        dy = d * cb_ref[...].astype(F32)
        dw_ref[0:1, :] = jnp.sum(dy * v2, axis=0, keepdims=True)
        dw_ref[1:2, :] = jnp.sum(dy * v1, axis=0, keepdims=True)
        dw_ref[2:3, :] = jnp.sum(dy * v, axis=0, keepdims=True)
        dv = w2 * dy + w1 * _shift_up(dy, 1, row, S) + w0 * _shift_up(dy, 2, row, S)
        dcc_ref[...] = (dv * cx).astype(BF16)
        dcx_ref[...] = (dv * cc).astype(BF16)

    col = pl.BlockSpec((S, LANES), lambda j: (0, j))
    return pl.pallas_call(
        body, name="conv_bwd", grid=(CONV_W // LANES,),
        in_specs=[col] + _conv_specs(S) + [pl.BlockSpec((3, LANES), lambda j: (0, j))],
        out_specs=[col, col, col, pl.BlockSpec((3, LANES), lambda j: (0, j))],
        out_shape=[_sds((S, CONV_W), BF16)] * 3 + [_sds((3, CONV_W), F32)],
        compiler_params=_params(("parallel",)),
    )(dsc, proj, proj, proj, conv_w)


Q_COL, K_COL, V_COL = 1536 // LANES, 2048 // LANES, 2560 // LANES


def _split_dot(x, tri):
    hi = x.astype(BF16)
    lo = (x - hi.astype(F32)).astype(BF16)
    return jnp.dot(hi, tri, preferred_element_type=F32) + jnp.dot(lo, tri, preferred_element_type=F32)


def _tri_dot(tri, x):
    hi = x.astype(BF16)
    lo = (x - hi.astype(F32)).astype(BF16)
    return jnp.dot(tri, hi, preferred_element_type=F32) + jnp.dot(tri, lo, preferred_element_type=F32)


def _softplus(z):
    return jnp.maximum(z, 0.0) + jnp.log(1.0 + jnp.exp(-jnp.abs(z)))


def _nt(a, b):
    return lax.dot_general(a, b, (((1,), (1,)), ((), ())), preferred_element_type=F32)


def _tn(a, b):
    return lax.dot_general(a, b, (((0,), (0,)), ((), ())), preferred_element_type=F32)


def _interleave(gens, delays):
    results = [None] * len(gens)
    live = list(range(len(gens)))
    rnd = 0
    while live:
        for g in list(live):
            if rnd < delays[g]:
                continue
            try:
                next(gens[g])
            except StopIteration as stop:
                results[g] = stop.value
                live.remove(g)
        rnd += 1
    return results


def _attn_fwd(proj, shards):
    S = proj.shape[0]
    B = ATT_BLK
    nq = S // B
    n = len(shards)

    def body(q_ref, k_ref, v_ref, *rest):
        o_ref, t_ref = rest[n:n + 2]
        start, relay, finish = _gather_protocol(rest[:n], rest[n + 2:2 * n + 2], *rest[2 * n + 2:])
        p = pl.program_id(0)
        i = pl.program_id(1)
        pl.when((p == 0) & (i == 0))(start)
        pl.when((p == HEAD_PAIRS // 2) & (i == 0))(relay)
        lo_lane = lax.broadcasted_iota(jnp.int32, (B, LANES), 1) < HEAD_DIM
        row = lax.broadcasted_iota(jnp.int32, (B, B), 0)
        col = lax.broadcasted_iota(jnp.int32, (B, B), 1)
        after = (row > col).astype(BF16)
        causal = col < row
        q2 = q_ref[...] * 0.125
        zero = jnp.zeros((), BF16)
        q_heads = (jnp.where(lo_lane, q2, zero), jnp.where(lo_lane, zero, q2))

        def head_tile(q_h, st, kb, diag):
            k2 = k_ref[pl.ds(pl.multiple_of(kb * B, B), B), :]
            z = _nt(q_h, k2)
            yield
            spz = _softplus(z)
            sp = jnp.where(causal, spz, 0.0) if diag else spz
            hi = sp.astype(BF16)
            lo = (sp - hi.astype(F32)).astype(BF16)
            r = st["r"]
            st["r"] = r + jnp.sum(sp, axis=1, keepdims=True)
            yield
            rem = jnp.dot(hi, after, preferred_element_type=F32) + jnp.dot(lo, after, preferred_element_type=F32)
            yield
            a = jnp.exp(z - spz - (rem + r))
            if diag:
                a = jnp.where(causal, a, 0.0)
            ab = a.astype(BF16)
            yield
            v2 = v_ref[pl.ds(pl.multiple_of(kb * B, B), B), :]
            st["acc"] = st["acc"] + jnp.dot(ab, v2, preferred_element_type=F32)

        def tiles(kbs, carry, diags=(False, False)):
            sts = [dict(r=carry[0], acc=carry[1]), dict(r=carry[2], acc=carry[3])]
            gens = [head_tile(q_h, st, kb, dg) for kb, dg in zip(kbs, diags) for q_h, st in zip(q_heads, sts)]
            _interleave(gens, [t for t in range(len(kbs)) for _ in q_heads])
            return sts[0]["r"], sts[0]["acc"], sts[1]["r"], sts[1]["acc"]

        zr, za = jnp.zeros((B, 1), F32), jnp.zeros((B, LANES), F32)
        carry = lax.fori_loop(0, i % 2, lambda j, cr: tiles([i, i - 1], cr, (True, False)), (zr, za, zr, za))
        carry = lax.fori_loop(0, 1 - i % 2, lambda j, cr: tiles([i], cr, (True,)), carry)
        first = i - 1 - i % 2
        ra, acc_a, rb, acc_b = lax.fori_loop(
            0, i // 2, lambda j, cr: tiles([first - 2 * j, first - 2 * j - 1], cr), carry)
        o_ref[...] = jnp.where(lo_lane, acc_a, acc_b).astype(BF16)
        t_ref[...] = jnp.where(lo_lane, ra, rb).T
        pl.when((p == HEAD_PAIRS - 1) & (i == nq - 1))(finish)

    seq = lambda off: pl.BlockSpec((S, LANES), lambda p, i: (0, off + p))
    blk = pl.BlockSpec((B, LANES), lambda p, i: (i, p))
    o, t, *gathered = pl.pallas_call(
        body, name="attn_fwd", grid=(HEAD_PAIRS, nq),
        in_specs=[pl.BlockSpec((B, LANES), lambda p, i: (i, Q_COL + p)), seq(K_COL), seq(V_COL)] + _hbm_specs(n),
        out_specs=[blk, pl.BlockSpec((LANES, B), lambda p, i: (p, i))] + _hbm_specs(n),
        out_shape=[_sds((S, 512), BF16), _sds((512, S), F32)] + _gather_shapes(shards),
        scratch_shapes=_gather_sems(n),
        compiler_params=_params(("arbitrary", "arbitrary")),
    )(proj, proj, proj, *shards)
    return o, t, _with_own_shard(gathered, shards)


def _attn_bwd(proj, t, do, parts):
    S = proj.shape[0]
    kt = proj[:, K_COL * LANES:V_COL * LANES].T
    B = ATT_BLK
    nq = S // B
    n = len(parts)

    def body(q_ref, k_ref, v_ref, kt_ref, t_ref, do_ref, *rest):
        dq_ref, dk_ref, dv_ref = rest[n:n + 3]
        dk_acc, dv_acc = rest[2 * n + 3:2 * n + 5]
        start, finish = _reduce8_protocol(rest[:n], rest[n + 3:2 * n + 3], *rest[2 * n + 5:])
        i = pl.program_id(1)
        pl.when((pl.program_id(0) == 0) & (i == 0))(start)

        @pl.when(i == 0)
        def _():
            dk_acc[...] = jnp.zeros_like(dk_acc)
            dv_acc[...] = jnp.zeros_like(dv_acc)

        lo_lane = lax.broadcasted_iota(jnp.int32, (B, LANES), 1) < HEAD_DIM
        key = lax.broadcasted_iota(jnp.int32, (B, B), 0)
        qry = lax.broadcasted_iota(jnp.int32, (B, B), 1)
        upto = (qry <= key).astype(BF16)
        before = (qry < key).astype(BF16)
        causal = key < qry
        zero = jnp.zeros((), BF16)
        q2 = q_ref[...] * 0.125
        do2 = do_ref[...]
        heads = ((jnp.where(lo_lane, q2, zero), jnp.where(lo_lane, do2, zero), t_ref[0:1, :]),
                 (jnp.where(lo_lane, zero, q2), jnp.where(lo_lane, zero, do2), t_ref[HEAD_DIM:HEAD_DIM + 1, :]))

        def head_tile(head, st, kb, diag):
            q_h, do_h, t_h = head
            rows = pl.ds(pl.multiple_of(kb * B, B), B)
            z = _nt(k_ref[rows, :], q_h)
            da = _nt(v_ref[rows, :], do_h)
            yield
            spz = _softplus(z)
            sp = jnp.where(causal, spz, 0.0) if diag else spz
            hi = sp.astype(BF16)
            lo = (sp - hi.astype(F32)).astype(BF16)
            pc = st["pc"]
            st["pc"] = pc + jnp.sum(sp, axis=0, keepdims=True)
            yield
            pref = jnp.dot(upto, hi, preferred_element_type=F32) + jnp.dot(upto, lo, preferred_element_type=F32)
            yield
            a = jnp.exp(z - spz - ((t_h - pc) - pref))
            if diag:
                a = jnp.where(causal, a, 0.0)
            e = a * da
            eb = e.astype(BF16)
            ab = a.astype(BF16)
            ec = st["ec"]
            st["ec"] = ec + jnp.sum(e, axis=0, keepdims=True)
            yield
            e_before = ec + jnp.dot(before, eb, preferred_element_type=F32)
            yield
            u = jnp.exp(-spz)
            dz = u * (e + e_before) - e_before
            if diag:
                dz = jnp.where(causal, dz, 0.0)
            dzb = dz.astype(BF16)
            yield
            st["dqt"] = st["dqt"] + jnp.dot(kt_ref[:, rows], dzb, preferred_element_type=F32)
            return (jnp.dot(dzb, q_h, preferred_element_type=F32), jnp.dot(ab, do_h, preferred_element_type=F32))

        def tiles(kbs, carry, diags=(False, False)):
            sts = [dict(pc=carry[3 * h], ec=carry[3 * h + 1], dqt=carry[3 * h + 2]) for h in range(2)]
            gens = [head_tile(hd, st, kb, dg) for kb, dg in zip(kbs, diags) for hd, st in zip(heads, sts)]
            res = _interleave(gens, [t for t in range(len(kbs)) for _ in heads])
            for t, kb in enumerate(kbs):
                rows = pl.ds(pl.multiple_of(kb * B, B), B)
                (dk_a, dv_a), (dk_b, dv_b) = res[2 * t], res[2 * t + 1]
                dk_acc[rows, :] += dk_a + dk_b
                dv_acc[rows, :] += dv_a + dv_b
            return tuple(st[nm] for st in sts for nm in ("pc", "ec", "dqt"))

        zc, zq = jnp.zeros((1, B), F32), jnp.zeros((LANES, B), F32)
        carry = lax.fori_loop(0, i // 2, lambda j, cr: tiles([2 * j, 2 * j + 1], cr), (zc, zc, zq, zc, zc, zq))
        carry = lax.fori_loop(0, i % 2, lambda j, cr: tiles([i - 1, i], cr, (False, True)), carry)
        _, _, dqt_a, _, _, dqt_b = lax.fori_loop(0, 1 - i % 2, lambda j, cr: tiles([i], cr, (True,)), carry)
        head0 = lax.broadcasted_iota(jnp.int32, (LANES, B), 0) < HEAD_DIM
        dq_ref[...] = (jnp.where(head0, dqt_a, dqt_b).T * 0.125).astype(BF16)

        @pl.when(i == nq - 1)
        def _():
            dk_ref[...] = dk_acc[...].astype(BF16)
            dv_ref[...] = dv_acc[...].astype(BF16)

        pl.when((pl.program_id(0) == HEAD_PAIRS - 1) & (i == nq - 1))(finish)

    seq = lambda off: pl.BlockSpec((S, LANES), lambda p, i: (0, off + p))
    blk = pl.BlockSpec((B, LANES), lambda p, i: (i, p))
    whole = pl.BlockSpec((S, LANES), lambda p, i: (0, p))
    dq, dk, dv, *came = pl.pallas_call(
        body, name="attn_bwd", grid=(HEAD_PAIRS, nq),
        in_specs=[pl.BlockSpec((B, LANES), lambda p, i: (i, Q_COL + p)), seq(K_COL), seq(V_COL),
                  pl.BlockSpec((LANES, S), lambda p, i: (p, 0)), pl.BlockSpec((LANES, B), lambda p, i: (p, i)), blk]
        + _hbm_specs(n),
        out_specs=[blk, whole, whole] + _hbm_specs(n),
        out_shape=[_sds((S, 512), BF16)] * 3 + [jax.ShapeDtypeStruct(p.shape, p.dtype) for p in parts],
        scratch_shapes=[pltpu.VMEM((S, LANES), F32), pltpu.VMEM((S, LANES), F32)] + _reduce8_sems(n),
        compiler_params=_params(("arbitrary", "arbitrary")),
    )(proj, proj, proj, kt, t, do, *parts)
    return dq, dk, dv, came


GA_COL, GB_COL = 3072 // D, 4096 // D


def _merge_fwd(sc, o, wco4, wao4, proj, bm):
    S = sc.shape[0]

    def body(sc_ref, o_ref, wc_ref, wa_ref, ga_ref, gb_ref, bm_ref, ya_ref, yb_ref, mg_ref):
        scv, ov = sc_ref[...], o_ref[...]
        for s in range(4):
            cols = slice(s * 256, (s + 1) * 256)
            ya = jnp.dot(scv, wc_ref[s], preferred_element_type=F32)
            yb = jnp.dot(ov, wa_ref[s], preferred_element_type=F32)
            sa = _sigmoid(ga_ref[:, cols].astype(F32) + bm_ref[0:1, cols])
            sb = _sigmoid(gb_ref[:, cols].astype(F32) + bm_ref[1:2, cols])
            ya_ref[:, cols] = ya.astype(BF16)
            yb_ref[:, cols] = yb.astype(BF16)
            mg_ref[:, cols] = (sa * ya + sb * yb).astype(BF16)

    wide = pl.BlockSpec((TM, 512), lambda i: (i, 0))
    wsp = pl.BlockSpec((4, 512, 256), lambda i: (0, 0, 0))
    out = pl.BlockSpec((TM, D), lambda i: (i, 0))
    return pl.pallas_call(
        body, name="merge_fwd", grid=(S // TM,),
        in_specs=[wide, wide, wsp, wsp, pl.BlockSpec((TM, D), lambda i: (i, GA_COL)),
                  pl.BlockSpec((TM, D), lambda i: (i, GB_COL)), pl.BlockSpec((2, D), lambda i: (0, 0))],
        out_specs=[out, out, out], out_shape=[_sds((S, D), BF16)] * 3,
        compiler_params=_params(("parallel",)),
    )(sc, o, wco4, wao4, proj, proj, bm)


def _merge_bwd(dy2, wout, ya, yb, proj, bm):
    S = dy2.shape[0]

    def epilogue(acc, ex, outs):
        ya_ref, yb_ref, ga_ref, gb_ref, bm_ref = ex
        i = pl.program_id(0)
        sa = _sigmoid(ga_ref[...].astype(F32) + bm_ref[0:1, :])
        sb = _sigmoid(gb_ref[...].astype(F32) + bm_ref[1:2, :])
        dga = acc * ya_ref[...].astype(F32) * (sa * (1.0 - sa))
        dgb = acc * yb_ref[...].astype(F32) * (sb * (1.0 - sb))
        outs[0][...] = (acc * sa).astype(BF16)
        outs[1][...] = (acc * sb).astype(BF16)
        outs[2][...] = dga.astype(BF16)
        outs[3][...] = dgb.astype(BF16)

        @pl.when(i == 0)
        def _():
            outs[4][...] = jnp.zeros_like(outs[4])

        outs[4][0:1, :] += jnp.sum(dga, axis=0, keepdims=True)
        outs[4][1:2, :] += jnp.sum(dgb, axis=0, keepdims=True)

    tm = TM // 2
    out = pl.BlockSpec((tm, D), lambda i: (i, 0))
    return _mm(
        "merge_bwd", (S // tm,), dy2, out, wout, pl.BlockSpec((D, D), lambda i: (0, 0)), (1, 1),
        [_sds((S, D), BF16)] * 4 + [_sds((8, D), F32)], [out, out, out, out, _const_spec((8, D))],
        epilogue, extras=(ya, yb, proj, proj, bm),
        extra_specs=(out, out, pl.BlockSpec((tm, D), lambda i: (i, GA_COL)),
                     pl.BlockSpec((tm, D), lambda i: (i, GB_COL)), _const_spec((2, D))),
        semantics=("arbitrary",))


def _mixer_out_bwd(dya, dyb, wco4, wao4):
    S = dya.shape[0]

    def body(da_ref, db_ref, wc_ref, wa_ref, dsc_ref, do_ref):
        for d_ref, w_ref, o_ref in ((da_ref, wc_ref, dsc_ref), (db_ref, wa_ref, do_ref)):
            acc = _nt(d_ref[:, 0:256], w_ref[0])
            for s in range(1, 4):
                acc = acc + _nt(d_ref[:, s * 256:(s + 1) * 256], w_ref[s])
            o_ref[...] = acc.astype(BF16)

    wide = pl.BlockSpec((TM, D), lambda i: (i, 0))
    wsp = pl.BlockSpec((4, 512, 256), lambda i: (0, 0, 0))
    out = pl.BlockSpec((TM, 512), lambda i: (i, 0))
    return pl.pallas_call(
        body, name="mixer_out_bwd", grid=(S // TM,), in_specs=[wide, wide, wsp, wsp], out_specs=[out, out],
        out_shape=[_sds((S, 512), BF16)] * 2, compiler_params=_params(("parallel",)),
    )(dya, dyb, wco4, wao4)


def _mixer_out_dw(sc, o, dya, dyb):
    S = sc.shape[0]
    tk = _tk(S)
    nk = S // tk

    def body(sc_ref, o_ref, da_ref, db_ref, dwc_ref, dwa_ref, acc_c, acc_a):
        k = pl.program_id(0)

        @pl.when(k == 0)
        def _():
            acc_c[...] = jnp.zeros_like(acc_c)
            acc_a[...] = jnp.zeros_like(acc_a)

        acc_c[...] += _tn(sc_ref[...], da_ref[...])
        acc_a[...] += _tn(o_ref[...], db_ref[...])

        @pl.when(k == nk - 1)
        def _():
            for s in range(4):
                dwc_ref[s] = acc_c[:, s * 256:(s + 1) * 256].astype(BF16)
                dwa_ref[s] = acc_a[:, s * 256:(s + 1) * 256].astype(BF16)

    narrow = pl.BlockSpec((tk, 512), lambda k: (k, 0))
    wide = pl.BlockSpec((tk, D), lambda k: (k, 0))
    out = pl.BlockSpec((4, 512, 256), lambda k: (0, 0, 0))
    return pl.pallas_call(
        body, name="mixer_out_dw", grid=(nk,), in_specs=[narrow, narrow, wide, wide], out_specs=[out, out],
        out_shape=[_sds((4, 512, 256), BF16)] * 2,
        scratch_shapes=[pltpu.VMEM((512, D), F32), pltpu.VMEM((512, D), F32)],
        compiler_params=_params(("arbitrary",)),
    )(sc, o, dya, dyb)


ADA_SHARD = 2304
ADA_TN = 768


def _ada_fwd(c_all, w_ada_l, b_l):
    def body(c_ref, w_ref, b_ref, o_ref):
        cv = c_ref[...]
        ca = cv * _sigmoid(cv)
        o_ref[...] = jnp.dot(ca.astype(BF16), w_ref[...].astype(BF16), preferred_element_type=F32) + b_ref[...]

    return pl.pallas_call(
        body, name="ada_fwd", grid=(ADA_SHARD // ADA_TN,),
        in_specs=[pl.BlockSpec((8, D), lambda j: (0, 0)), pl.BlockSpec((D, ADA_TN), lambda j: (0, j)),
                  pl.BlockSpec((1, ADA_TN), lambda j: (0, j))],
        out_specs=pl.BlockSpec((8, ADA_TN), lambda j: (0, j)), out_shape=_sds((8, ADA_SHARD), F32),
        compiler_params=_params(("parallel",)),
    )(c_all, w_ada_l, b_l)


def _ada_bwd(c_all_t, dmod_l):
    def body(c_ref, d_ref, o_ref):
        cv = c_ref[...]
        ca = cv * _sigmoid(cv)
        o_ref[...] = jnp.dot(ca.astype(BF16).astype(F32), d_ref[...].astype(BF16).astype(F32),
                             preferred_element_type=F32, precision=lax.Precision.HIGHEST)

    return pl.pallas_call(
        body, name="ada_bwd", grid=(ADA_SHARD // ADA_TN,),
        in_specs=[pl.BlockSpec((D, 8), lambda j: (0, 0)), pl.BlockSpec((8, ADA_TN), lambda j: (0, j))],
        out_specs=pl.BlockSpec((D, ADA_TN), lambda j: (0, j)), out_shape=_sds((D, ADA_SHARD), F32),
        compiler_params=_params(("parallel",)),
    )(c_all_t, dmod_l)


def _sum_rows(name, x):
    n = x.shape[1]

    def body(x_ref, o_ref):
        s = x_ref[0:1, :]
        for d in range(1, 8):
            s = s + x_ref[d:d + 1, :]
        o_ref[...] = s

    return pl.pallas_call(
        body, name=name, in_specs=[pl.BlockSpec(memory_space=pltpu.VMEM)],
        out_specs=pl.BlockSpec(memory_space=pltpu.VMEM), out_shape=_sds((1, n), F32),
        compiler_params=pltpu.CompilerParams(vmem_limit_bytes=VMEM_LIMIT),
    )(x)


def _pair_sum(name, g4, recv, c_idx):
    _, _, h, C = g4.shape
    tr = h if h <= 512 else h // (h // 256) if h % 256 == 0 else h // 2

    def body(c_ref, g_ref, r_ref, o_ref):
        o_ref[...] = (g_ref[...].astype(F32) + r_ref[...].astype(F32)).astype(BF16)

    grid_spec = pltpu.PrefetchScalarGridSpec(
        num_scalar_prefetch=1, grid=(4, h // tr),
        in_specs=[pl.BlockSpec((None, None, tr, C), lambda k, i, c: (k, c[0], i, 0)),
                  pl.BlockSpec((None, tr, C), lambda k, i, c: (k, i, 0))],
        out_specs=pl.BlockSpec((None, tr, C), lambda k, i, c: (k, i, 0)))
    return pl.pallas_call(
        body, name=name, grid_spec=grid_spec, out_shape=_sds((4, h, C), BF16),
        compiler_params=_params(("parallel", "parallel")),
    )(c_idx, g4, recv)


def _sum4(name, q, p, chip_idx):
    _, h, C = q.shape
    tr = h if h <= 512 else h // (h // 256) if h % 256 == 0 else h // 2

    def body(k_ref, q_ref, p_ref, o_ref):
        me = k_ref[0]
        terms = [jnp.where(me == k, p_ref[...], q_ref[k]).astype(F32) for k in range(4)]
        o_ref[...] = ((terms[0] + terms[1]) + terms[2]) + terms[3]

    grid_spec = pltpu.PrefetchScalarGridSpec(
        num_scalar_prefetch=1, grid=(h // tr,),
        in_specs=[pl.BlockSpec((4, tr, C), lambda i, k: (0, i, 0)),
                  pl.BlockSpec((None, tr, C), lambda i, k: (k[0], i, 0))],
        out_specs=pl.BlockSpec((tr, C), lambda i, k: (i, 0)))
    return pl.pallas_call(
        body, name=name, grid_spec=grid_spec, out_shape=_sds((h, C), F32), compiler_params=_params(("parallel",)),
    )(chip_idx, q, p)


def _sum8(name, q, p, idx):
    _, _, h, C = q.shape
    tr = h if h <= 256 else 256 if h % 256 == 0 else h // 2

    def body(k_ref, q_ref, p_ref, o_ref):
        total = None
        for j in range(4):
            for cc in range(2):
                mine = (k_ref[0] == j) & (k_ref[1] == cc)
                term = jnp.where(mine, p_ref[...], q_ref[j, cc]).astype(F32)
                total = term if total is None else total + term
        o_ref[...] = total

    grid_spec = pltpu.PrefetchScalarGridSpec(
        num_scalar_prefetch=1, grid=(h // tr,),
        in_specs=[pl.BlockSpec((4, 2, tr, C), lambda i, k: (0, 0, i, 0)),
                  pl.BlockSpec((None, None, tr, C), lambda i, k: (k[0], k[1], i, 0))],
        out_specs=pl.BlockSpec((tr, C), lambda i, k: (i, 0)))
    return pl.pallas_call(
        body, name=name, grid_spec=grid_spec, out_shape=_sds((h, C), F32), compiler_params=_params(("parallel",)),
    )(idx, q, p)


def _adamw(name, w, g, m, v):
    R, C = w.shape
    tr = R
    while tr * C * 4 > (1 << 20) and tr % 16 == 0:
        tr //= 2
    c1 = 1.0 - ADAM_B1 ** ADAM_STEP
    c2 = 1.0 - ADAM_B2 ** ADAM_STEP

    def body(w_ref, g_ref, m_ref, v_ref, d_ref, nm_ref, nv_ref):
        gv = g_ref[...]
        nm = ADAM_B1 * m_ref[...] + (1.0 - ADAM_B1) * gv
        nv = ADAM_B2 * v_ref[...] + (1.0 - ADAM_B2) * (gv * gv)
        nm_ref[...] = nm
        nv_ref[...] = nv
        d_ref[...] = -ADAM_LR * ((nm * (1.0 / c1)) / (jnp.sqrt(nv * (1.0 / c2)) + ADAM_EPS) + ADAM_WD * w_ref[...])

    spec = pl.BlockSpec((tr, C), lambda i: (i, 0))
    return pl.pallas_call(
        body, name=name, grid=(R // tr,), in_specs=[spec] * 4, out_specs=[spec] * 3,
        out_shape=[_sds((R, C), F32)] * 3, compiler_params=_params(("parallel",)),
    )(w, g, m, v)


def _adamw_halves(name, w, own, sib, m, v, c_idx):
    R, C = w.shape
    h = R // 2
    tr = h
    while tr * C * 4 > (1 << 20) and tr % 16 == 0:
        tr //= 2
    nb = h // tr
    c1 = 1.0 - ADAM_B1 ** ADAM_STEP
    c2 = 1.0 - ADAM_B2 ** ADAM_STEP

    def body(c_ref, w_ref, own_ref, sib_ref, m_ref, v_ref, g_ref, d_ref, nm_ref, nv_ref):
        mine = (pl.program_id(0) // nb) == c_ref[0]
        gv = jnp.where(mine, own_ref[...], sib_ref[...])
        nm = ADAM_B1 * m_ref[...] + (1.0 - ADAM_B1) * gv
        nv = ADAM_B2 * v_ref[...] + (1.0 - ADAM_B2) * (gv * gv)
        g_ref[...] = gv
        nm_ref[...] = nm
        nv_ref[...] = nv
        d_ref[...] = -ADAM_LR * ((nm * (1.0 / c1)) / (jnp.sqrt(nv * (1.0 / c2)) + ADAM_EPS) + ADAM_WD * w_ref[...])

    spec = pl.BlockSpec((tr, C), lambda i, c: (i, 0))
    half = pl.BlockSpec((tr, C), lambda i, c: (i % nb, 0))
    grid_spec = pltpu.PrefetchScalarGridSpec(
        num_scalar_prefetch=1, grid=(R // tr,), in_specs=[spec, half, half, spec, spec], out_specs=[spec] * 4)
    return pl.pallas_call(
        body, name=name, grid_spec=grid_spec, out_shape=[_sds((R, C), F32)] * 4,
        compiler_params=_params(("parallel",)),
    )(c_idx, w, own, sib, m, v)


def _pack(g, scale, shift, gate):
    rows = jnp.stack([g, scale, shift, gate]).astype(F32)
    return jnp.concatenate([rows, jnp.zeros((4, D), F32)], axis=0)


def _fold8(vec):
    m = -(-vec.shape[0] // (8 * LANES)) * LANES
    return jnp.concatenate([vec, jnp.zeros((8 * m - vec.shape[0],), vec.dtype)]).reshape(8, m)


def _allgather_vectors(name, vec):
    return _allgather_rows(name, _fold8(vec)).reshape(8, -1)


def kernel(x, c, w_ada, b_ada, norm1_g, ffn1_w_gu, ffn1_w_down, norm2_g, w_mix_in, b_merge, conv_w, w_conv_out, w_attn_out, w_out, norm3_g, ffn2_w_gu, ffn2_w_down, final_g, loss_target, m_w_ada, m_b_ada, m_norm1_g, m_ffn1_w_gu, m_ffn1_w_down, m_norm2_g, m_w_mix_in, m_b_merge, m_conv_w, m_w_conv_out, m_w_attn_out, m_w_out, m_norm3_g, m_ffn2_w_gu, m_ffn2_w_down, m_final_g, v_w_ada, v_b_ada, v_norm1_g, v_ffn1_w_gu, v_ffn1_w_down, v_norm2_g, v_w_mix_in, v_b_merge, v_conv_w, v_w_conv_out, v_w_attn_out, v_w_out, v_norm3_g, v_ffn2_w_gu, v_ffn2_w_down, v_final_g):
    xi, yi, ci = lax.axis_index("x"), lax.axis_index("y"), lax.axis_index("c")
    chip = 2 * xi + yi
    dev = 4 * xi + 2 * yi + ci
    S = x.shape[1]
    h0 = x[0]
    target = loss_target[0]

    (wgu1,) = _allgather_weights([ffn1_w_gu[0].astype(BF16)])
    late_shards = [w[0].astype(BF16) for w in (w_conv_out, w_attn_out, w_out, ffn2_w_gu, ffn2_w_down)]
    c_idx = jnp.reshape(ci, (1,)).astype(jnp.int32)
    chip_idx = jnp.reshape(chip, (1,)).astype(jnp.int32)

    def reduce_pairs(tag, names, grads):
        g4 = [g.reshape(4, 2, g.shape[1] // 2, g.shape[2]) for g in grads]
        recv = _sibling_swap_halves("grad_sibling_swap_" + tag, g4)
        return [_pair_sum("pair_sum_" + nm, a, b, c_idx) for nm, a, b in zip(names, g4, recv)]

    small = jnp.concatenate([c[0], b_merge[0].reshape(-1), conv_w[0].reshape(-1)])
    gathered = _allgather_vectors("allgather_small", small)
    c_all = gathered[:, :D]
    per_chip = gathered[0::2]
    bm_full = jnp.concatenate([per_chip[k, D:D + 512].reshape(2, 256) for k in range(4)], axis=1)
    cw_full = jnp.concatenate([per_chip[k, D + 512:D + 896].reshape(3, 128) for k in range(4)], axis=1)
    b_l = lax.dynamic_slice_in_dim(b_ada, chip * ADA_SHARD, ADA_SHARD, axis=1)
    mod_l = _ada_fwd(c_all, w_ada[0], b_l)
    mod_g = _allgather_rows("allgather_mod", mod_l).reshape(8, 8, ADA_SHARD)
    mod_all = jnp.concatenate([mod_g[2 * k] for k in range(4)], axis=1)
    mod = lax.dynamic_slice_in_dim(mod_all, dev, 1, axis=0).reshape(3, 3, D)
    p1 = _pack(norm1_g[0], mod[0, 1], mod[0, 0], mod[0, 2])
    p2 = _pack(norm2_g[0], mod[1, 1], mod[1, 0], mod[1, 2])
    p3 = _pack(norm3_g[0], mod[2, 1], mod[2, 0], mod[2, 2])
    pf = _pack(final_g, final_g, final_g, final_g)

    u1 = _norm_mod_fwd("norm1_fwd", h0, p1)
    gu1, hm1, (wd1,) = _ffn_up("ffn1_up", u1, wgu1, [ffn1_w_down[0].astype(BF16)])
    f1, h1, u2, (wmix,) = _proj_residual("ffn1_down", hm1, wd1.reshape(D_FF, D), h0, p1, 0.5, p2,
                                         gather_shards=[w_mix_in[0].astype(BF16)])
    wd1 = wd1.reshape(D_FF, D)
    proj = _mm("mix_in", (4, S // TM), u2, pl.BlockSpec((TM, D), lambda s, i: (i, 0)),
               wmix, pl.BlockSpec((None, D, MIX_SHARD), lambda s, i: (s, 0, 0)), (1, 0),
               [_sds((S, MIX_W), BF16)], [pl.BlockSpec((TM, MIX_SHARD), lambda s, i: (i, s))], _store(BF16),
               semantics=("parallel", "parallel"))[0]
    sc = _conv_fwd(proj, cw_full)
    o, t_tot, (wco, wao, wout, wgu2, wd2) = _attn_fwd(proj, late_shards)
    wout = wout.reshape(D, D)
    wd2 = wd2.reshape(D_FF, D)
    ya, yb, merged = _merge_fwd(sc, o, wco, wao, proj, bm_full)
    y2, h2, u3, _ = _proj_residual("mix_out", merged, wout, h1, p2, 1.0, p3)
    gu3, hm3, _ = _ffn_up("ffn2_up", u3, wgu2)

    dh3, df3, sums_f, loss_blk = _proj_residual_loss("ffn2_down", hm3, wd2, h2, p3, 0.5, pf, target)
    (dh2, dy2, sums3), dwgu2, dwd2 = _ffn_bwd("ffn2", df3, u3, gu3, hm3, wgu2, wd2,
                                              _norm_bwd_tail(h2, p3, dh3, prev=(p2, y2, 1.0)))

    dya, dyb, dga, dgb, sums_bm = _merge_bwd(dy2, wout, ya, yb, proj, bm_full)
    tk = _tk(S)
    dwout = _mm("dw_out", (1, S // tk), merged, pl.BlockSpec((tk, D), lambda n, k: (k, 0)),
                dy2, pl.BlockSpec((tk, D), lambda n, k: (k, 0)), (0, 0),
                [_sds((D, D), BF16)], [pl.BlockSpec((D, D), lambda n, k: (0, 0))], _store(BF16),
                nk=S // tk, acc_shape=(D, D))[0]
    dsc, do = _mixer_out_bwd(dya, dyb, wco, wao)
    dwco, dwao = _mixer_out_dw(sc, o, dya, dyb)
    dcb, dcc, dcx, dcw = _conv_bwd(dsc, proj, cw_full)
    names_e = ["ffn2_w_gu", "ffn2_w_down", "w_out", "w_conv_out", "w_attn_out"]
    part_e = [g.reshape(4, 2, g.shape[1] // 2, g.shape[2])
              for g in (dwgu2, dwd2.reshape(4, 704, D), dwout.reshape(4, 256, D), dwco, dwao)]
    dq, dk, dv, came_e = _attn_bwd(proj, t_tot, do, part_e)
    dproj = jnp.concatenate([dcb, dcc, dcx, dq, dk, dv, dga, dgb], axis=1)
    tail2 = _norm_bwd_tail(h1, p2, dh2, prev=(p1, f1, 0.5))

    def mix_prod(a_ref, b_ref):
        acc = _nt(a_ref[:, 0:MIX_SHARD], b_ref[0])
        for s in range(1, 4):
            acc = acc + _nt(a_ref[:, s * MIX_SHARD:(s + 1) * MIX_SHARD], b_ref[s])
        return acc

    dh1, df1, sums2 = _mm("mix_in_bwd", (S // TB,), dproj, pl.BlockSpec((TB, MIX_W), lambda i: (i, 0)),
                          wmix, pl.BlockSpec((4, D, MIX_SHARD), lambda i: (0, 0, 0)), (1, 1),
                          tail2["out_shapes"], tail2["out_specs"], tail2["epilogue"], extras=tail2["extras"],
                          extra_specs=tail2["extra_specs"], semantics=("arbitrary",), prod_fn=mix_prod)
    dwmix = _grad_w("dw_mix_in", u2, D, dproj, MIX_SHARD, lambda s, k: (k, s), 4, (4, D, MIX_SHARD),
                    (None, D, MIX_SHARD), lambda s, k: (s, 0, 0))

    part_mix = reduce_pairs("mix", ["w_mix_in"], [dwmix])
    dgu1, *came_mix = _ffn_down_bwd("ffn1_down_bwd", df1, wd1, gu1, a2a_parts=part_mix)
    dwd1 = _ffn_dw_down("ffn1_dw_down", hm1, df1)
    part_wd1 = reduce_pairs("wd1", ["ffn1_w_down"], [dwd1.reshape(4, 704, D)])
    dwgu1, *came_wd1 = _ffn_dw_gu("ffn1_dw_gu", u1, dgu1, a2a_parts=part_wd1)
    part_gu1 = reduce_pairs("gu1", ["ffn1_w_gu"], [dwgu1])
    grad_x, sums1, *came_gu1 = _ffn_up_bwd("ffn1_up_bwd", dgu1, wgu1, _norm_bwd_tail(h0, p1, dh1),
                                           a2a_parts=part_gu1)

    dmod = jnp.stack([sums1[0], sums1[1], sums2[3], sums2[0], sums2[1], sums3[3], sums3[0], sums3[1], sums_f[1]])
    small_g = jnp.concatenate([dmod.reshape(-1), sums1[2], sums2[2], sums3[2], sums_f[0],
                               sums_bm[0], sums_bm[1], dcw.reshape(-1), loss_blk[0, 0:1]])
    all_g = _allgather_vectors("allgather_small_grads", small_g)
    tot = _sum_rows("sum_small_grads", all_g)[0]
    loss = tot[16 * D + 512]
    g_b_ada = tot[:9 * D][None, :]
    g_n1, g_n2, g_n3 = (tot[(9 + k) * D:(10 + k) * D][None, :] for k in range(3))
    g_fin = tot[12 * D:13 * D]
    g_bm = lax.dynamic_slice_in_dim(tot[13 * D:15 * D].reshape(2, D), chip * 256, 256, axis=1)[None]
    g_cw = lax.dynamic_slice_in_dim(tot[15 * D:16 * D + 512].reshape(3, 512), chip * 128, 128, axis=1)[None]
    dmod_l = lax.dynamic_slice_in_dim(all_g[:, :9 * D], chip * ADA_SHARD, ADA_SHARD, axis=1)
    g_w_ada = _ada_bwd(c_all.T, dmod_l)[None]

    names_l = ["w_mix_in", "ffn1_w_down", "ffn1_w_gu"]
    names = names_e + names_l
    place_idx = jnp.stack([chip, ci]).astype(jnp.int32)
    half = [_sum8("device_sum_" + nm, q, p, place_idx) for nm, q, p in zip(names_e, came_e, part_e)]
    half += [_sum4("chip_sum_" + nm, q, p, chip_idx)
             for nm, q, p in zip(names_l, came_mix + came_wd1 + came_gu1, part_mix + part_wd1 + part_gu1)]
    g_own = dict(zip(names, half))
    g_sib = dict(zip(names, _sibling_share(half)))

    weights = dict(w_ada=w_ada, b_ada=b_ada, norm1_g=norm1_g, ffn1_w_gu=ffn1_w_gu, ffn1_w_down=ffn1_w_down,
                   norm2_g=norm2_g, w_mix_in=w_mix_in, b_merge=b_merge, conv_w=conv_w, w_conv_out=w_conv_out,
                   w_attn_out=w_attn_out, w_out=w_out, norm3_g=norm3_g, ffn2_w_gu=ffn2_w_gu,
                   ffn2_w_down=ffn2_w_down, final_g=final_g)
    ms = dict(w_ada=m_w_ada, b_ada=m_b_ada, norm1_g=m_norm1_g, ffn1_w_gu=m_ffn1_w_gu, ffn1_w_down=m_ffn1_w_down,
              norm2_g=m_norm2_g, w_mix_in=m_w_mix_in, b_merge=m_b_merge, conv_w=m_conv_w, w_conv_out=m_w_conv_out,
              w_attn_out=m_w_attn_out, w_out=m_w_out, norm3_g=m_norm3_g, ffn2_w_gu=m_ffn2_w_gu,
              ffn2_w_down=m_ffn2_w_down, final_g=m_final_g)
    vs = dict(w_ada=v_w_ada, b_ada=v_b_ada, norm1_g=v_norm1_g, ffn1_w_gu=v_ffn1_w_gu, ffn1_w_down=v_ffn1_w_down,
              norm2_g=v_norm2_g, w_mix_in=v_w_mix_in, b_merge=v_b_merge, conv_w=v_conv_w, w_conv_out=v_w_conv_out,
              w_attn_out=v_w_attn_out, w_out=v_w_out, norm3_g=v_norm3_g, ffn2_w_gu=v_ffn2_w_gu,
              ffn2_w_down=v_ffn2_w_down, final_g=v_final_g)
    order = list(weights)
    grad = dict(w_ada=g_w_ada, b_ada=g_b_ada, norm1_g=g_n1, norm2_g=g_n2, norm3_g=g_n3, final_g=g_fin,
                b_merge=g_bm, conv_w=g_cw)
    delta, new_m, new_v = {}, {}, {}
    small_names = ["b_ada", "norm1_g", "norm2_g", "norm3_g", "final_g", "b_merge", "conv_w"]
    flat = lambda d: jnp.concatenate([d[nm].reshape(-1) for nm in small_names])[None, :]
    sd, sm, sv = _adamw("adamw_small", flat(weights), flat(grad), flat(ms), flat(vs))
    off = 0
    for nm in small_names:
        size = weights[nm].size
        for dst, src in ((delta, sd), (new_m, sm), (new_v, sv)):
            dst[nm] = src[0, off:off + size].reshape(weights[nm].shape)
        off += size
    for nm in order:
        if nm in small_names:
            continue
        shp = weights[nm].shape
        if nm in g_own:
            g2, d2, m2, v2 = _adamw_halves("adamw_" + nm, weights[nm][0], g_own[nm], g_sib[nm], ms[nm][0], vs[nm][0],
                                           c_idx)
            grad[nm] = g2.reshape(shp)
        else:
            d2, m2, v2 = _adamw("adamw_" + nm, weights[nm][0], grad[nm][0], ms[nm][0], vs[nm][0])
        delta[nm], new_m[nm], new_v[nm] = d2.reshape(shp), m2.reshape(shp), v2.reshape(shp)

    return (loss, grad_x[None], *[grad[nm] for nm in order], *[delta[nm] for nm in order],
            *[new_m[nm] for nm in order], *[new_v[nm] for nm in order])
```

```python
import functools

import jax
import jax.numpy as jnp
from jax import lax
from jax.experimental import pallas as pl
from jax.experimental.pallas import tpu as pltpu

F32 = jnp.float32
BF16 = jnp.bfloat16
MESH = pl.DeviceIdType.MESH

VMEM_LIMIT = 56 * 1024 * 1024
LANES = 128

D = 1024
D_FF = 2816
FF_SHARD = 1408
MIX_SHARD = 1280
MIX_W = 5120
HEAD_PAIRS = 4
HEAD_DIM = 64
CONV_W = 512
EPS = 1e-6
ATT_BLK = 256

ADAM_LR = 0.001
ADAM_B1 = 0.9
ADAM_B2 = 0.999
ADAM_EPS = 1e-08
ADAM_WD = 0.01
ADAM_STEP = 10


def _params(semantics=None):
    return pltpu.CompilerParams(dimension_semantics=semantics, vmem_limit_bytes=VMEM_LIMIT)


def _sigmoid(x):
    return 1.0 / (1.0 + jnp.exp(-x))


def _place():
    x, y, c = lax.axis_index("x"), lax.axis_index("y"), lax.axis_index("c")
    chips = [(1 - x, y), (x, 1 - y), (1 - x, 1 - y)]
    return x, y, c, chips


def _allgather_rows(name, blk):
    m_per, n = blk.shape

    def body(x_ref, out_ref, send_sems, recv_sems, local_sem):
        x, y, c, chips = _place()
        me, sibling = (x, y, c), (x, y, 1 - c)

        def rows(px, py, pc):
            return out_ref.at[pl.ds((4 * px + 2 * py + pc) * m_per, m_per), :]

        def copy(k, block, to, src=None):
            return pltpu.make_async_remote_copy(
                src_ref=rows(*block) if src is None else src, dst_ref=rows(*block),
                send_sem=send_sems.at[k], recv_sem=recv_sems.at[k], device_id=to, device_id_type=MESH)

        mine = pltpu.make_async_copy(x_ref, rows(*me), local_sem)
        mine.start()
        first = [copy(0, me, sibling, src=x_ref)]
        first += [copy(1 + j, me, (*chip, c), src=x_ref) for j, chip in enumerate(chips)]
        for cp in first:
            cp.start()
        passed = [copy(4 + j, (*chip, c), sibling) for j, chip in enumerate(chips)]
        for j, chip in enumerate(chips):
            copy(1 + j, (*chip, c), me).wait_recv()
            passed[j].start()
        copy(0, sibling, me).wait_recv()
        for j, chip in enumerate(chips):
            copy(4 + j, (*chip, 1 - c), me).wait_recv()
        for cp in first + passed:
            cp.wait_send()
        mine.wait()

    return pl.pallas_call(
        body, name=name,
        out_shape=jax.ShapeDtypeStruct((8 * m_per, n), blk.dtype),
        in_specs=[pl.BlockSpec(memory_space=pltpu.VMEM)],
        out_specs=pl.BlockSpec(memory_space=pltpu.VMEM),
        scratch_shapes=[pltpu.SemaphoreType.DMA((7,)), pltpu.SemaphoreType.DMA((7,)), pltpu.SemaphoreType.DMA],
        compiler_params=pltpu.CompilerParams(vmem_limit_bytes=VMEM_LIMIT),
    )(blk)


def _hbm_specs(n):
    return [pl.BlockSpec(memory_space=pltpu.HBM)] * n


def _allgather_weights(shards):
    n = len(shards)

    def body(*refs):
        start, relay, finish = _gather_protocol(refs[:n], refs[n:2 * n], *refs[2 * n:])
        start()
        relay()
        finish()

    gathered = pl.pallas_call(
        body, name="allgather_weights",
        out_shape=_gather_shapes(shards), in_specs=_hbm_specs(n), out_specs=_hbm_specs(n),
        scratch_shapes=_gather_sems(n),
    )(*shards)
    return _with_own_shard(gathered, shards)


def _gather_shapes(shards):
    return [jax.ShapeDtypeStruct((4, *s.shape), s.dtype) for s in shards]


def _gather_sems(n):
    return [pltpu.SemaphoreType.DMA((6 * n,)), pltpu.SemaphoreType.DMA((6 * n,))]


def _with_own_shard(gathered, shards):
    chip = 2 * lax.axis_index("x") + lax.axis_index("y")
    return [lax.dynamic_update_slice(g, s[None], (chip, 0, 0)) for g, s in zip(gathered, shards)]


def _gather_protocol(ins, outs, send_sems, recv_sems):
    n = len(ins)
    x, y, c, chips = _place()
    me, sibling = (x, y, c), (x, y, 1 - c)
    me_k = 2 * x + y

    def half(w, k, hc):
        h = ins[w].shape[0] // 2
        return outs[w].at[k, pl.ds(pl.multiple_of(hc * h, 8), h), :]

    def copy(w, j, k, hc, to, src=None):
        dst = half(w, k, hc)
        return pltpu.make_async_remote_copy(
            src_ref=dst if src is None else src, dst_ref=dst,
            send_sem=send_sems.at[6 * w + j], recv_sem=recv_sems.at[6 * w + j],
            device_id=to, device_id_type=MESH)

    def first(w, j):
        h = ins[w].shape[0] // 2
        src = ins[w].at[pl.ds(pl.multiple_of(c * h, 8), h), :]
        return copy(w, j, me_k, c, (*chips[j], c), src=src)

    def passed(w, j):
        px, py = chips[j]
        return copy(w, 3 + j, 2 * px + py, c, sibling)

    pairs = [(w, j) for w in range(n) for j in range(3)]

    def start():
        for w, j in pairs:
            first(w, j).start()

    def relay():
        for w, j in pairs:
            px, py = chips[j]
            copy(w, j, 2 * px + py, c, me).wait_recv()
            passed(w, j).start()

    def finish():
        for w, j in pairs:
            px, py = chips[j]
            copy(w, 3 + j, 2 * px + py, 1 - c, me).wait_recv()
        for w, j in pairs:
            first(w, j).wait_send()
            passed(w, j).wait_send()

    return start, relay, finish


def _sibling_swap_halves(name, grads):
    n = len(grads)

    def body(*refs):
        ins, outs = refs[:n], refs[n:2 * n]
        send_sems, recv_sems = refs[2 * n:]
        x, y, c, _ = _place()
        cps = []
        for w in range(n):
            cp = pltpu.make_async_remote_copy(
                src_ref=ins[w].at[:, 1 - c], dst_ref=outs[w],
                send_sem=send_sems.at[w], recv_sem=recv_sems.at[w],
                device_id=(x, y, 1 - c), device_id_type=MESH)
            cp.start()
            cps.append(cp)
        for cp in cps:
            cp.wait()

    return pl.pallas_call(
        body, name=name,
        out_shape=[jax.ShapeDtypeStruct((4, *g.shape[2:]), g.dtype) for g in grads],
        in_specs=_hbm_specs(n), out_specs=_hbm_specs(n),
        scratch_shapes=[pltpu.SemaphoreType.DMA((n,)), pltpu.SemaphoreType.DMA((n,))],
    )(*grads)


def _all_to_all_sems(n):
    return [pltpu.SemaphoreType.DMA((3 * n,)), pltpu.SemaphoreType.DMA((3 * n,))]


def _all_to_all_protocol(ins, outs, send_sems, recv_sems):
    n = len(ins)
    x, y, c, chips = _place()
    me_k = 2 * x + y
    pairs = [(w, j) for w in range(n) for j in range(3)]

    def sent(w, j):
        px, py = chips[j]
        return pltpu.make_async_remote_copy(
            src_ref=ins[w].at[2 * px + py], dst_ref=outs[w].at[me_k],
            send_sem=send_sems.at[3 * w + j], recv_sem=recv_sems.at[3 * w + j],
            device_id=(px, py, c), device_id_type=MESH)

    def start():
        for w, j in pairs:
            sent(w, j).start()

    def finish():
        for w, j in pairs:
            px, py = chips[j]
            slab = outs[w].at[2 * px + py]
            pltpu.make_async_remote_copy(
                src_ref=slab, dst_ref=slab, send_sem=send_sems.at[3 * w + j],
                recv_sem=recv_sems.at[3 * w + j], device_id=(px, py, c), device_id_type=MESH).wait_recv()
        for w, j in pairs:
            sent(w, j).wait_send()

    return start, finish


def _reduce8_sems(n):
    return [pltpu.SemaphoreType.DMA((7 * n,)), pltpu.SemaphoreType.DMA((7 * n,))]


def _reduce8_protocol(ins, outs, send_sems, recv_sems):
    n = len(ins)
    x, y, c, chips = _place()
    me_k = 2 * x + y
    far = [(w, j, hc) for w in range(n) for j in range(3) for hc in range(2)]

    def sent(w, j, hc):
        px, py = chips[j]
        return pltpu.make_async_remote_copy(
            src_ref=ins[w].at[2 * px + py, hc], dst_ref=outs[w].at[me_k, c],
            send_sem=send_sems.at[7 * w + 2 * j + hc], recv_sem=recv_sems.at[7 * w + 2 * j + c],
            device_id=(px, py, hc), device_id_type=MESH)

    def to_sibling(w):
        return pltpu.make_async_remote_copy(
            src_ref=ins[w].at[me_k, 1 - c], dst_ref=outs[w].at[me_k, c],
            send_sem=send_sems.at[7 * w + 6], recv_sem=recv_sems.at[7 * w + 6],
            device_id=(x, y, 1 - c), device_id_type=MESH)

    def arrival(w, slab, k):
        return pltpu.make_async_remote_copy(
            src_ref=slab, dst_ref=slab, send_sem=send_sems.at[7 * w + k], recv_sem=recv_sems.at[7 * w + k],
            device_id=(x, y, c), device_id_type=MESH)

    def start():
        for w in range(n):
            to_sibling(w).start()
        for w, j, hc in far:
            sent(w, j, hc).start()

    def finish():
        for w in range(n):
            arrival(w, outs[w].at[me_k, 1 - c], 6).wait_recv()
        for w, j, cc in far:
            px, py = chips[j]
            arrival(w, outs[w].at[2 * px + py, cc], 2 * j + cc).wait_recv()
        for w in range(n):
            to_sibling(w).wait_send()
        for w, j, hc in far:
            sent(w, j, hc).wait_send()

    return start, finish


def _sibling_share(halves):
    n = len(halves)

    def body(*refs):
        ins, outs = refs[:n], refs[n:2 * n]
        send_sems, recv_sems = refs[2 * n:]
        x, y, c, _ = _place()
        cps = []
        for w in range(n):
            cp = pltpu.make_async_remote_copy(
                src_ref=ins[w], dst_ref=outs[w], send_sem=send_sems.at[w], recv_sem=recv_sems.at[w],
                device_id=(x, y, 1 - c), device_id_type=MESH)
            cp.start()
            cps.append(cp)
        for cp in cps:
            cp.wait()

    return pl.pallas_call(
        body, name="grad_sibling_share",
        out_shape=[jax.ShapeDtypeStruct(p.shape, p.dtype) for p in halves],
        in_specs=_hbm_specs(n), out_specs=_hbm_specs(n),
        scratch_shapes=[pltpu.SemaphoreType.DMA((n,)), pltpu.SemaphoreType.DMA((n,))],
    )(*halves)


def _mm(name, grid, a, a_spec, b, b_spec, contract, out_shapes, out_specs, epilogue,
        extras=(), extra_specs=(), nk=1, acc_shape=None, semantics=None, a2a_parts=(), gather_shards=(),
        relay_at=None, prod_fn=None):
    assert not (a2a_parts and gather_shards)
    moved = tuple(a2a_parts) + tuple(gather_shards)
    ne, no, nc = len(extras), len(out_shapes), len(moved)
    nd = len(grid)

    def body(*refs):
        a_ref, b_ref = refs[0], refs[1]
        ex, outs = refs[2:2 + ne], refs[2 + ne + nc:2 + ne + nc + no]
        if nc:
            ids = [pl.program_id(d) for d in range(nd)]
            comm_refs = (refs[2 + ne:2 + ne + nc], refs[2 + ne + nc + no:2 + ne + 2 * nc + no], *refs[-2:])
            at_start = functools.reduce(jnp.logical_and, [i == 0 for i in ids])
            if a2a_parts:
                start, finish = _all_to_all_protocol(*comm_refs)
                pl.when(at_start)(start)
            else:
                start, relay, finish = _gather_protocol(*comm_refs)
                pl.when(at_start)(start)
                at_relay = grid[0] // 2 if relay_at is None else relay_at
                pl.when(functools.reduce(jnp.logical_and, [ids[0] == at_relay] + [i == 0 for i in ids[1:]]))(relay)

        def prod():
            if prod_fn is not None:
                return prod_fn(a_ref, b_ref)
            return lax.dot_general(a_ref[...], b_ref[...], (((contract[0],), (contract[1],)), ((), ())),
                                   preferred_element_type=F32)

        if nk == 1:
            epilogue(prod(), ex, outs)
        else:
            acc = refs[2 + ne + 2 * nc + no]
            k = pl.program_id(nd - 1)

            @pl.when(k == 0)
            def _():
                acc[...] = prod()

            @pl.when(k > 0)
            def _():
                acc[...] += prod()

            @pl.when(k == nk - 1)
            def _():
                epilogue(acc[...], ex, outs)

        if nc:
            pl.when(functools.reduce(jnp.logical_and, [i == g - 1 for i, g in zip(ids, grid)]))(finish)

    if semantics is None or nc:
        semantics = ("arbitrary",) * nd
    return pl.pallas_call(
        body, name=name, grid=grid,
        in_specs=[a_spec, b_spec, *extra_specs] + _hbm_specs(nc),
        out_specs=list(out_specs) + _hbm_specs(nc),
        out_shape=list(out_shapes) + [jax.ShapeDtypeStruct(p.shape, p.dtype) for p in a2a_parts]
        + _gather_shapes(gather_shards),
        scratch_shapes=([] if nk == 1 else [pltpu.VMEM(acc_shape, F32)])
        + (_all_to_all_sems(nc) if a2a_parts else _gather_sems(nc) if gather_shards else []),
        compiler_params=_params(semantics),
    )(a, b, *extras, *moved)


def _store(dtype):
    def epilogue(acc, ex, outs):
        outs[0][...] = acc.astype(dtype)
    return epilogue


def _sds(shape, dtype):
    return jax.ShapeDtypeStruct(shape, dtype)


TR = 512


def _row_spec(width, tr=TR):
    return pl.BlockSpec((tr, width), lambda i: (i, 0))


def _const_spec(shape):
    nd = len(shape)
    return pl.BlockSpec(shape, lambda i: (0,) * nd)


def _norm_mod_fwd(name, h, p):
    S = h.shape[0]

    def body(h_ref, p_ref, u_ref):
        hv = h_ref[...]
        r = lax.rsqrt(jnp.mean(hv * hv, axis=-1, keepdims=True) + EPS)
        nrm = (hv * r) * p_ref[0:1, :]
        u_ref[...] = (nrm * (1.0 + p_ref[1:2, :]) + p_ref[2:3, :]).astype(BF16)

    return pl.pallas_call(
        body, name=name, grid=(S // TR,),
        in_specs=[_row_spec(D), _const_spec((8, D))], out_specs=_row_spec(D),
        out_shape=_sds((S, D), BF16), compiler_params=_params(("parallel",)),
    )(h, p)


def _rmsnorm_parts(hv):
    r = lax.rsqrt(jnp.mean(hv * hv, axis=-1, keepdims=True) + EPS)
    return r, hv * r


TB = 256


def _norm_bwd_tail(h, p, dh_res, prev=None):
    S = h.shape[0]
    row = pl.BlockSpec((TB, D), lambda i: (i, 0))
    const = pl.BlockSpec((8, D), lambda i: (0, 0))
    extras, specs = [h, p, dh_res], [row, const, row]
    out_shapes, out_specs = [_sds((S, D), F32)], [row]
    if prev is not None:
        extras += [prev[0], prev[1]]
        specs += [const, row]
        out_shapes.append(_sds((S, D), BF16))
        out_specs.append(row)
    out_shapes.append(_sds((8, D), F32))
    out_specs.append(const)

    def epilogue(duv, ex, outs):
        h_ref, p_ref, r_ref = ex[:3]
        dh_ref, sums_ref = outs[0], outs[-1]

        @pl.when(pl.program_id(0) == 0)
        def _():
            sums_ref[...] = jnp.zeros_like(sums_ref)

        g = p_ref[0:1, :]
        r, xn = _rmsnorm_parts(h_ref[...])
        dn = duv * (1.0 + p_ref[1:2, :])
        dxn = dn * g
        dh = r_ref[...] + r * (dxn - xn * jnp.mean(dxn * xn, axis=-1, keepdims=True))
        dh_ref[...] = dh
        sums_ref[0:1, :] += jnp.sum(duv, axis=0, keepdims=True)
        sums_ref[1:2, :] += jnp.sum(duv * (xn * g), axis=0, keepdims=True)
        sums_ref[2:3, :] += jnp.sum(dn * xn, axis=0, keepdims=True)
        if prev is not None:
            pp_ref, f_ref = ex[3:5]
            outs[1][...] = (prev[2] * pp_ref[3:4, :] * dh).astype(BF16)
            sums_ref[3:4, :] += prev[2] * jnp.sum(dh * f_ref[...].astype(F32), axis=0, keepdims=True)

    return dict(extras=tuple(extras), extra_specs=tuple(specs), out_shapes=out_shapes, out_specs=out_specs,
                epilogue=epilogue)


TM = 512


def _ffn_up(name, u, wgu4, shards=()):
    S = u.shape[0]
    n = len(shards)
    ni = S // TM

    def body(u_ref, wg_ref, wu_ref, *rest):
        gu_ref, hm_ref = rest[n:n + 2]
        s, i = pl.program_id(0), pl.program_id(1)
        if n:
            start, relay, finish = _gather_protocol(rest[:n], rest[n + 2:2 * n + 2], *rest[2 * n + 2:])
            pl.when((s == 0) & (i == 0))(start)
            pl.when((s == 1) & (i == 0))(relay)
        uv = u_ref[...]
        g = jnp.dot(uv, wg_ref[...], preferred_element_type=F32)
        up = jnp.dot(uv, wu_ref[...], preferred_element_type=F32)
        gu_ref[0] = g.astype(BF16)
        gu_ref[1] = up.astype(BF16)
        hm_ref[...] = (g * _sigmoid(g) * up).astype(BF16)
        if n:
            pl.when((s == 1) & (i == ni - 1))(finish)

    gu, hm, *gathered = pl.pallas_call(
        body, name=name, grid=(2, ni),
        in_specs=[pl.BlockSpec((TM, D), lambda s, i: (i, 0)),
                  pl.BlockSpec((None, D, FF_SHARD), lambda s, i: (s, 0, 0)),
                  pl.BlockSpec((None, D, FF_SHARD), lambda s, i: (s + 2, 0, 0))] + _hbm_specs(n),
        out_specs=[pl.BlockSpec((2, TM, FF_SHARD), lambda s, i: (0, i, s)),
                   pl.BlockSpec((TM, FF_SHARD), lambda s, i: (i, s))] + _hbm_specs(n),
        out_shape=[_sds((2, S, D_FF), BF16), _sds((S, D_FF), BF16)] + _gather_shapes(shards),
        scratch_shapes=_gather_sems(n) if n else [],
        compiler_params=_params(("arbitrary", "arbitrary") if n else ("parallel", "parallel")),
    )(u, wgu4, wgu4, *shards)
    return gu, hm, _with_own_shard(gathered, shards)


def _proj_residual(name, a, w, h, p, weight, p_next, gather_shards=()):
    S, K = a.shape

    def epilogue(acc, ex, outs):
        h_ref, p_ref, pn_ref = ex
        outs[0][...] = acc.astype(BF16)
        hout = h_ref[...] + weight * p_ref[3:4, :] * acc
        outs[1][...] = hout
        _, xn = _rmsnorm_parts(hout)
        outs[2][...] = ((xn * pn_ref[0:1, :]) * (1.0 + pn_ref[1:2, :]) + pn_ref[2:3, :]).astype(BF16)

    row = _row_spec(D, TM)
    res = _mm(
        name, (S // TM,), a, pl.BlockSpec((TM, K), lambda i: (i, 0)), w, pl.BlockSpec((K, D), lambda i: (0, 0)),
        (1, 0), [_sds((S, D), BF16), _sds((S, D), F32), _sds((S, D), BF16)], [row, row, row], epilogue,
        extras=(h, p, p_next), extra_specs=(row, _const_spec((8, D)), _const_spec((8, D))),
        semantics=("parallel",), gather_shards=gather_shards, relay_at=S // TM - 1)
    return res[0], res[1], res[2], _with_own_shard(res[3:], gather_shards)


def _proj_residual_loss(name, a, w, h, p, weight, gf, target):
    S, K = a.shape

    def epilogue(acc, ex, outs):
        h_ref, p_ref, g_ref, t_ref = ex
        dh_ref, df_ref, sums_ref, loss_ref = outs

        @pl.when(pl.program_id(0) == 0)
        def _():
            sums_ref[...] = jnp.zeros_like(sums_ref)
            loss_ref[...] = jnp.zeros_like(loss_ref)

        gate = p_ref[3:4, :]
        g = g_ref[0:1, :]
        r, xn = _rmsnorm_parts(h_ref[...] + weight * gate * acc)
        err = xn * g - t_ref[...]
        loss_ref[...] += 0.5 * jnp.sum(err * err) * (1.0 / D)
        dout = err * (1.0 / D)
        dxn = dout * g
        dh = r * (dxn - xn * jnp.mean(dxn * xn, axis=-1, keepdims=True))
        dh_ref[...] = dh
        df_ref[...] = (weight * gate * dh).astype(BF16)
        sums_ref[0:1, :] += jnp.sum(dout * xn, axis=0, keepdims=True)
        sums_ref[1:2, :] += weight * jnp.sum(dh * acc, axis=0, keepdims=True)

    row = _row_spec(D, TM)
    return _mm(
        name, (S // TM,), a, pl.BlockSpec((TM, K), lambda i: (i, 0)), w, pl.BlockSpec((K, D), lambda i: (0, 0)),
        (1, 0), [_sds((S, D), F32), _sds((S, D), BF16), _sds((8, D), F32), _sds((8, LANES), F32)],
        [row, row, _const_spec((8, D)), _const_spec((8, LANES))], epilogue,
        extras=(h, p, gf, target), extra_specs=(row, _const_spec((8, D)), _const_spec((8, D)), row),
        semantics=("arbitrary",))


def _ffn_down_bwd(name, df, wd, gu, a2a_parts=()):
    S = df.shape[0]

    def epilogue(acc, ex, outs):
        g = ex[0][0].astype(F32)
        up = ex[0][1].astype(F32)
        sg = _sigmoid(g)
        outs[0][0] = (acc * up * (sg * (1.0 + g * (1.0 - sg)))).astype(BF16)
        outs[0][1] = (acc * g * sg).astype(BF16)

    gu_spec = pl.BlockSpec((2, TM, FF_SHARD), lambda n, i: (0, i, n))
    return _mm(
        name, (2, S // TM), df, pl.BlockSpec((TM, D), lambda n, i: (i, 0)),
        wd, pl.BlockSpec((FF_SHARD, D), lambda n, i: (n, 0)), (1, 1),
        [_sds((2, S, D_FF), BF16)], [gu_spec], epilogue, extras=(gu,), extra_specs=(gu_spec,),
        semantics=("parallel", "parallel"), a2a_parts=a2a_parts)


TK = 2048


def _tk(S):
    return min(TK, S)


def _grad_w(name, a, a_w, b, b_w, b_map, n_out, out_shape, out_block, out_map, a2a_parts=()):
    S = a.shape[0]
    tk = _tk(S)
    nk = S // tk
    res = _mm(
        name, (n_out, nk), a, pl.BlockSpec((tk, a_w), lambda s, k: (k, 0)), b, pl.BlockSpec(
            (None, tk, b_w) if b.ndim == 3 else (tk, b_w), b_map), (0, 0),
        [_sds(out_shape, BF16)], [pl.BlockSpec(out_block, out_map)], _store(BF16), nk=nk, acc_shape=(a_w, b_w),
        semantics=("parallel", "arbitrary"), a2a_parts=a2a_parts)
    return res if a2a_parts else res[0]


def _ffn_dw_down(name, hm, df):
    S = df.shape[0]
    tk = _tk(S)
    return _mm(
        name, (2, S // tk), hm, pl.BlockSpec((tk, FF_SHARD), lambda m, k: (k, m)),
        df, pl.BlockSpec((tk, D), lambda m, k: (k, 0)), (0, 0),
        [_sds((D_FF, D), BF16)], [pl.BlockSpec((FF_SHARD, D), lambda m, k: (m, 0))], _store(BF16),
        nk=S // tk, acc_shape=(FF_SHARD, D), semantics=("parallel", "arbitrary"))[0]


def _ffn_up_bwd(name, dgu, wgu4, tail, a2a_parts=()):
    S = dgu.shape[1]

    def prod_fn(a_ref, b_ref):
        acc = None
        for s in range(4):
            cols = slice((s % 2) * FF_SHARD, (s % 2 + 1) * FF_SHARD)
            term = _nt(a_ref[s // 2, :, cols], b_ref[s])
            acc = term if acc is None else acc + term
        return acc

    return _mm(
        name, (S // TB,), dgu, pl.BlockSpec((2, TB, D_FF), lambda i: (0, i, 0)),
        wgu4, pl.BlockSpec((4, D, FF_SHARD), lambda i: (0, 0, 0)), (1, 1),
        tail["out_shapes"], tail["out_specs"], tail["epilogue"], extras=tail["extras"],
        extra_specs=tail["extra_specs"], semantics=("arbitrary",), a2a_parts=a2a_parts, prod_fn=prod_fn)


def _ffn_dw_gu(name, u_in, dgu, a2a_parts=()):
    return _grad_w(name, u_in, D, dgu, FF_SHARD, lambda s, k: (s // 2, k, s % 2), 4,
                   (4, D, FF_SHARD), (None, D, FF_SHARD), lambda s, k: (s, 0, 0), a2a_parts=a2a_parts)


def _ffn_bwd(tag, df, u_in, gu, hm, wgu4, wd, tail):
    dgu = _ffn_down_bwd(tag + "_down_bwd", df, wd, gu)[0]
    dwd = _ffn_dw_down(tag + "_dw_down", hm, df)
    res = _ffn_up_bwd(tag + "_up_bwd", dgu, wgu4, tail)
    dwgu = _ffn_dw_gu(tag + "_dw_gu", u_in, dgu)
    return res, dwgu, dwd


def _shift_down(v, k, row):
    return jnp.where(row >= k, pltpu.roll(v, k, axis=0), 0.0)


def _shift_up(v, k, row, S):
    return jnp.where(row < S - k, pltpu.roll(v, S - k, axis=0), 0.0)


def _conv_specs(S):
    cols = CONV_W // LANES
    return [pl.BlockSpec((S, LANES), functools.partial(lambda j, off: (0, off + j), off=o * cols))
            for o in range(3)]


def _conv_fwd(proj, conv_w):
    S = proj.shape[0]

    def body(cb_ref, cc_ref, cx_ref, w_ref, sc_ref):
        row = lax.broadcasted_iota(jnp.int32, (S, LANES), 0)
        v = cc_ref[...].astype(F32) * cx_ref[...].astype(F32)
        yv = w_ref[0:1, :] * _shift_down(v, 2, row) + w_ref[1:2, :] * _shift_down(v, 1, row) + w_ref[2:3, :] * v
        sc_ref[...] = (cb_ref[...].astype(F32) * yv).astype(BF16)

    return pl.pallas_call(
        body, name="conv_fwd", grid=(CONV_W // LANES,),
        in_specs=_conv_specs(S) + [pl.BlockSpec((3, LANES), lambda j: (0, j))],
        out_specs=pl.BlockSpec((S, LANES), lambda j: (0, j)), out_shape=_sds((S, CONV_W), BF16),
        compiler_params=_params(("parallel",)),
    )(proj, proj, proj, conv_w)


def _conv_bwd(dsc, proj, conv_w):
    S = proj.shape[0]

    def body(d_ref, cb_ref, cc_ref, cx_ref, w_ref, dcb_ref, dcc_ref, dcx_ref, dw_ref):
        row = lax.broadcasted_iota(jnp.int32, (S, LANES), 0)
        cc = cc_ref[...].astype(F32)
        cx = cx_ref[...].astype(F32)
        d = d_ref[...].astype(F32)
        v = cc * cx
        v1 = _shift_down(v, 1, row)
        v2 = _shift_down(v, 2, row)
        w0, w1, w2 = w_ref[0:1, :], w_ref[1:2, :], w_ref[2:3, :]
        dcb_ref[...] = (d * (w0 * v2 + w1 * v1 + w2 * v)).astype(BF16)
        dy = d * cb_ref[...].astype(F32)
        dw_ref[0:1, :] = jnp.sum(dy * v2, axis=0, keepdims=True)
        dw_ref[1:2, :] = jnp.sum(dy * v1, axis=0, keepdims=True)
        dw_ref[2:3, :] = jnp.sum(dy * v, axis=0, keepdims=True)
        dv = w2 * dy + w1 * _shift_up(dy, 1, row, S) + w0 * _shift_up(dy, 2, row, S)
        dcc_ref[...] = (dv * cx).astype(BF16)
        dcx_ref[...] = (dv * cc).astype(BF16)

    col = pl.BlockSpec((S, LANES), lambda j: (0, j))
    return pl.pallas_call(
        body, name="conv_bwd", grid=(CONV_W // LANES,),
        in_specs=[col] + _conv_specs(S) + [pl.BlockSpec((3, LANES), lambda j: (0, j))],
        out_specs=[col, col, col, pl.BlockSpec((3, LANES), lambda j: (0, j))],
        out_shape=[_sds((S, CONV_W), BF16)] * 3 + [_sds((3, CONV_W), F32)],
        compiler_params=_params(("parallel",)),
    )(dsc, proj, proj, proj, conv_w)


Q_COL, K_COL, V_COL = 1536 // LANES, 2048 // LANES, 2560 // LANES


def _split_dot(x, tri):
    hi = x.astype(BF16)
    lo = (x - hi.astype(F32)).astype(BF16)
    return jnp.dot(hi, tri, preferred_element_type=F32) + jnp.dot(lo, tri, preferred_element_type=F32)


def _tri_dot(tri, x):
    hi = x.astype(BF16)
    lo = (x - hi.astype(F32)).astype(BF16)
    return jnp.dot(tri, hi, preferred_element_type=F32) + jnp.dot(tri, lo, preferred_element_type=F32)


def _softplus(z):
    return jnp.maximum(z, 0.0) + jnp.log(1.0 + jnp.exp(-jnp.abs(z)))


def _nt(a, b):
    return lax.dot_general(a, b, (((1,), (1,)), ((), ())), preferred_element_type=F32)


def _tn(a, b):
    return lax.dot_general(a, b, (((0,), (0,)), ((), ())), preferred_element_type=F32)


def _interleave(gens, delays):
    results = [None] * len(gens)
    live = list(range(len(gens)))
    rnd = 0
    while live:
        for g in list(live):
            if rnd < delays[g]:
                continue
            try:
                next(gens[g])
            except StopIteration as stop:
                results[g] = stop.value
                live.remove(g)
        rnd += 1
    return results


def _attn_fwd(proj, shards):
    S = proj.shape[0]
    B = ATT_BLK
    nq = S // B
    n = len(shards)

    def body(q_ref, k_ref, v_ref, *rest):
        o_ref, t_ref = rest[n:n + 2]
        start, relay, finish = _gather_protocol(rest[:n], rest[n + 2:2 * n + 2], *rest[2 * n + 2:])
        p = pl.program_id(0)
        i = pl.program_id(1)
        pl.when((p == 0) & (i == 0))(start)
        pl.when((p == HEAD_PAIRS // 2) & (i == 0))(relay)
        lo_lane = lax.broadcasted_iota(jnp.int32, (B, LANES), 1) < HEAD_DIM
        row = lax.broadcasted_iota(jnp.int32, (B, B), 0)
        col = lax.broadcasted_iota(jnp.int32, (B, B), 1)
        after = (row > col).astype(BF16)
        causal = col < row
        q2 = q_ref[...] * 0.125
        zero = jnp.zeros((), BF16)
        q_heads = (jnp.where(lo_lane, q2, zero), jnp.where(lo_lane, zero, q2))

        def head_tile(q_h, st, kb, diag):
            k2 = k_ref[pl.ds(pl.multiple_of(kb * B, B), B), :]
            z = _nt(q_h, k2)
            yield
            spz = _softplus(z)
            sp = jnp.where(causal, spz, 0.0) if diag else spz
            hi = sp.astype(BF16)
            lo = (sp - hi.astype(F32)).astype(BF16)
            r = st["r"]
            st["r"] = r + jnp.sum(sp, axis=1, keepdims=True)
            yield
            rem = jnp.dot(hi, after, preferred_element_type=F32) + jnp.dot(lo, after, preferred_element_type=F32)
            yield
            a = jnp.exp(z - spz - (rem + r))
            if diag:
                a = jnp.where(causal, a, 0.0)
            ab = a.astype(BF16)
            yield
            v2 = v_ref[pl.ds(pl.multiple_of(kb * B, B), B), :]
            st["acc"] = st["acc"] + jnp.dot(ab, v2, preferred_element_type=F32)

        def tiles(kbs, carry, diags=(False, False)):
            sts = [dict(r=carry[0], acc=carry[1]), dict(r=carry[2], acc=carry[3])]
            gens = [head_tile(q_h, st, kb, dg) for kb, dg in zip(kbs, diags) for q_h, st in zip(q_heads, sts)]
            _interleave(gens, [t for t in range(len(kbs)) for _ in q_heads])
            return sts[0]["r"], sts[0]["acc"], sts[1]["r"], sts[1]["acc"]

        zr, za = jnp.zeros((B, 1), F32), jnp.zeros((B, LANES), F32)
        carry = lax.fori_loop(0, i % 2, lambda j, cr: tiles([i, i - 1], cr, (True, False)), (zr, za, zr, za))
        carry = lax.fori_loop(0, 1 - i % 2, lambda j, cr: tiles([i], cr, (True,)), carry)
        first = i - 1 - i % 2
        ra, acc_a, rb, acc_b = lax.fori_loop(
            0, i // 2, lambda j, cr: tiles([first - 2 * j, first - 2 * j - 1], cr), carry)
        o_ref[...] = jnp.where(lo_lane, acc_a, acc_b).astype(BF16)
        t_ref[...] = jnp.where(lo_lane, ra, rb).T
        pl.when((p == HEAD_PAIRS - 1) & (i == nq - 1))(finish)

    seq = lambda off: pl.BlockSpec((S, LANES), lambda p, i: (0, off + p))
    blk = pl.BlockSpec((B, LANES), lambda p, i: (i, p))
    o, t, *gathered = pl.pallas_call(
        body, name="attn_fwd", grid=(HEAD_PAIRS, nq),
        in_specs=[pl.BlockSpec((B, LANES), lambda p, i: (i, Q_COL + p)), seq(K_COL), seq(V_COL)] + _hbm_specs(n),
        out_specs=[blk, pl.BlockSpec((LANES, B), lambda p, i: (p, i))] + _hbm_specs(n),
        out_shape=[_sds((S, 512), BF16), _sds((512, S), F32)] + _gather_shapes(shards),
        scratch_shapes=_gather_sems(n),
        compiler_params=_params(("arbitrary", "arbitrary")),
    )(proj, proj, proj, *shards)
    return o, t, _with_own_shard(gathered, shards)


def _attn_bwd(proj, t, do, parts):
    S = proj.shape[0]
    kt = proj[:, K_COL * LANES:V_COL * LANES].T
    B = ATT_BLK
    nq = S // B
    n = len(parts)

    def body(q_ref, k_ref, v_ref, kt_ref, t_ref, do_ref, *rest):
        dq_ref, dk_ref, dv_ref = rest[n:n + 3]
        dk_acc, dv_acc = rest[2 * n + 3:2 * n + 5]
        start, finish = _reduce8_protocol(rest[:n], rest[n + 3:2 * n + 3], *rest[2 * n + 5:])
        i = pl.program_id(1)
        pl.when((pl.program_id(0) == 0) & (i == 0))(start)

        @pl.when(i == 0)
        def _():
            dk_acc[...] = jnp.zeros_like(dk_acc)
            dv_acc[...] = jnp.zeros_like(dv_acc)

        lo_lane = lax.broadcasted_iota(jnp.int32, (B, LANES), 1) < HEAD_DIM
        key = lax.broadcasted_iota(jnp.int32, (B, B), 0)
        qry = lax.broadcasted_iota(jnp.int32, (B, B), 1)
        upto = (qry <= key).astype(BF16)
        before = (qry < key).astype(BF16)
        causal = key < qry
        zero = jnp.zeros((), BF16)
        q2 = q_ref[...] * 0.125
        do2 = do_ref[...]
        heads = ((jnp.where(lo_lane, q2, zero), jnp.where(lo_lane, do2, zero), t_ref[0:1, :]),
                 (jnp.where(lo_lane, zero, q2), jnp.where(lo_lane, zero, do2), t_ref[HEAD_DIM:HEAD_DIM + 1, :]))

        def head_tile(head, st, kb, diag):
            q_h, do_h, t_h = head
            rows = pl.ds(pl.multiple_of(kb * B, B), B)
            z = _nt(k_ref[rows, :], q_h)
            da = _nt(v_ref[rows, :], do_h)
            yield
            spz = _softplus(z)
            sp = jnp.where(causal, spz, 0.0) if diag else spz
            hi = sp.astype(BF16)
            lo = (sp - hi.astype(F32)).astype(BF16)
            pc = st["pc"]
            st["pc"] = pc + jnp.sum(sp, axis=0, keepdims=True)
            yield
            pref = jnp.dot(upto, hi, preferred_element_type=F32) + jnp.dot(upto, lo, preferred_element_type=F32)
            yield
            a = jnp.exp(z - spz - ((t_h - pc) - pref))
            if diag:
                a = jnp.where(causal, a, 0.0)
            e = a * da
            eb = e.astype(BF16)
            ab = a.astype(BF16)
            ec = st["ec"]
            st["ec"] = ec + jnp.sum(e, axis=0, keepdims=True)
            yield
            e_before = ec + jnp.dot(before, eb, preferred_element_type=F32)
            yield
            u = jnp.exp(-spz)
            dz = u * (e + e_before) - e_before
            if diag:
                dz = jnp.where(causal, dz, 0.0)
            dzb = dz.astype(BF16)
            yield
            st["dqt"] = st["dqt"] + jnp.dot(kt_ref[:, rows], dzb, preferred_element_type=F32)
            return (jnp.dot(dzb, q_h, preferred_element_type=F32), jnp.dot(ab, do_h, preferred_element_type=F32))

        def tiles(kbs, carry, diags=(False, False)):
            sts = [dict(pc=carry[3 * h], ec=carry[3 * h + 1], dqt=carry[3 * h + 2]) for h in range(2)]
            gens = [head_tile(hd, st, kb, dg) for kb, dg in zip(kbs, diags) for hd, st in zip(heads, sts)]
            res = _interleave(gens, [t for t in range(len(kbs)) for _ in heads])
            for t, kb in enumerate(kbs):
                rows = pl.ds(pl.multiple_of(kb * B, B), B)
                (dk_a, dv_a), (dk_b, dv_b) = res[2 * t], res[2 * t + 1]
                dk_acc[rows, :] += dk_a + dk_b
                dv_acc[rows, :] += dv_a + dv_b
            return tuple(st[nm] for st in sts for nm in ("pc", "ec", "dqt"))

        zc, zq = jnp.zeros((1, B), F32), jnp.zeros((LANES, B), F32)
        carry = lax.fori_loop(0, i // 2, lambda j, cr: tiles([2 * j, 2 * j + 1], cr), (zc, zc, zq, zc, zc, zq))
        carry = lax.fori_loop(0, i % 2, lambda j, cr: tiles([i - 1, i], cr, (False, True)), carry)
        _, _, dqt_a, _, _, dqt_b = lax.fori_loop(0, 1 - i % 2, lambda j, cr: tiles([i], cr, (True,)), carry)
        head0 = lax.broadcasted_iota(jnp.int32, (LANES, B), 0) < HEAD_DIM
        dq_ref[...] = (jnp.where(head0, dqt_a, dqt_b).T * 0.125).astype(BF16)

        @pl.when(i == nq - 1)
        def _():
            dk_ref[...] = dk_acc[...].astype(BF16)
            dv_ref[...] = dv_acc[...].astype(BF16)

        pl.when((pl.program_id(0) == HEAD_PAIRS - 1) & (i == nq - 1))(finish)

    seq = lambda off: pl.BlockSpec((S, LANES), lambda p, i: (0, off + p))
    blk = pl.BlockSpec((B, LANES), lambda p, i: (i, p))
    whole = pl.BlockSpec((S, LANES), lambda p, i: (0, p))
    dq, dk, dv, *came = pl.pallas_call(
        body, name="attn_bwd", grid=(HEAD_PAIRS, nq),
        in_specs=[pl.BlockSpec((B, LANES), lambda p, i: (i, Q_COL + p)), seq(K_COL), seq(V_COL),
                  pl.BlockSpec((LANES, S), lambda p, i: (p, 0)), pl.BlockSpec((LANES, B), lambda p, i: (p, i)), blk]
        + _hbm_specs(n),
        out_specs=[blk, whole, whole] + _hbm_specs(n),
        out_shape=[_sds((S, 512), BF16)] * 3 + [jax.ShapeDtypeStruct(p.shape, p.dtype) for p in parts],
        scratch_shapes=[pltpu.VMEM((S, LANES), F32), pltpu.VMEM((S, LANES), F32)] + _reduce8_sems(n),
        compiler_params=_params(("arbitrary", "arbitrary")),
    )(proj, proj, proj, kt, t, do, *parts)
    return dq, dk, dv, came


GA_COL, GB_COL = 3072 // D, 4096 // D


def _merge_fwd(sc, o, wco4, wao4, proj, bm):
    S = sc.shape[0]

    def body(sc_ref, o_ref, wc_ref, wa_ref, ga_ref, gb_ref, bm_ref, ya_ref, yb_ref, mg_ref):
        scv, ov = sc_ref[...], o_ref[...]
        for s in range(4):
            cols = slice(s * 256, (s + 1) * 256)
            ya = jnp.dot(scv, wc_ref[s], preferred_element_type=F32)
            yb = jnp.dot(ov, wa_ref[s], preferred_element_type=F32)
            sa = _sigmoid(ga_ref[:, cols].astype(F32) + bm_ref[0:1, cols])
            sb = _sigmoid(gb_ref[:, cols].astype(F32) + bm_ref[1:2, cols])
            ya_ref[:, cols] = ya.astype(BF16)
            yb_ref[:, cols] = yb.astype(BF16)
            mg_ref[:, cols] = (sa * ya + sb * yb).astype(BF16)

    wide = pl.BlockSpec((TM, 512), lambda i: (i, 0))
    wsp = pl.BlockSpec((4, 512, 256), lambda i: (0, 0, 0))
    out = pl.BlockSpec((TM, D), lambda i: (i, 0))
    return pl.pallas_call(
        body, name="merge_fwd", grid=(S // TM,),
        in_specs=[wide, wide, wsp, wsp, pl.BlockSpec((TM, D), lambda i: (i, GA_COL)),
                  pl.BlockSpec((TM, D), lambda i: (i, GB_COL)), pl.BlockSpec((2, D), lambda i: (0, 0))],
        out_specs=[out, out, out], out_shape=[_sds((S, D), BF16)] * 3,
        compiler_params=_params(("parallel",)),
    )(sc, o, wco4, wao4, proj, proj, bm)


def _merge_bwd(dy2, wout, ya, yb, proj, bm):
    S = dy2.shape[0]

    def epilogue(acc, ex, outs):
        ya_ref, yb_ref, ga_ref, gb_ref, bm_ref = ex
        i = pl.program_id(0)
        sa = _sigmoid(ga_ref[...].astype(F32) + bm_ref[0:1, :])
        sb = _sigmoid(gb_ref[...].astype(F32) + bm_ref[1:2, :])
        dga = acc * ya_ref[...].astype(F32) * (sa * (1.0 - sa))
        dgb = acc * yb_ref[...].astype(F32) * (sb * (1.0 - sb))
        outs[0][...] = (acc * sa).astype(BF16)
        outs[1][...] = (acc * sb).astype(BF16)
        outs[2][...] = dga.astype(BF16)
        outs[3][...] = dgb.astype(BF16)

        @pl.when(i == 0)
        def _():
            outs[4][...] = jnp.zeros_like(outs[4])

        outs[4][0:1, :] += jnp.sum(dga, axis=0, keepdims=True)
        outs[4][1:2, :] += jnp.sum(dgb, axis=0, keepdims=True)

    tm = TM // 2
    out = pl.BlockSpec((tm, D), lambda i: (i, 0))
    return _mm(
        "merge_bwd", (S // tm,), dy2, out, wout, pl.BlockSpec((D, D), lambda i: (0, 0)), (1, 1),
        [_sds((S, D), BF16)] * 4 + [_sds((8, D), F32)], [out, out, out, out, _const_spec((8, D))],
        epilogue, extras=(ya, yb, proj, proj, bm),
        extra_specs=(out, out, pl.BlockSpec((tm, D), lambda i: (i, GA_COL)),
                     pl.BlockSpec((tm, D), lambda i: (i, GB_COL)), _const_spec((2, D))),
        semantics=("arbitrary",))


def _mixer_out_bwd(dya, dyb, wco4, wao4):
    S = dya.shape[0]

    def body(da_ref, db_ref, wc_ref, wa_ref, dsc_ref, do_ref):
        for d_ref, w_ref, o_ref in ((da_ref, wc_ref, dsc_ref), (db_ref, wa_ref, do_ref)):
            acc = _nt(d_ref[:, 0:256], w_ref[0])
            for s in range(1, 4):
                acc = acc + _nt(d_ref[:, s * 256:(s + 1) * 256], w_ref[s])
            o_ref[...] = acc.astype(BF16)

    wide = pl.BlockSpec((TM, D), lambda i: (i, 0))
    wsp = pl.BlockSpec((4, 512, 256), lambda i: (0, 0, 0))
    out = pl.BlockSpec((TM, 512), lambda i: (i, 0))
    return pl.pallas_call(
        body, name="mixer_out_bwd", grid=(S // TM,), in_specs=[wide, wide, wsp, wsp], out_specs=[out, out],
        out_shape=[_sds((S, 512), BF16)] * 2, compiler_params=_params(("parallel",)),
    )(dya, dyb, wco4, wao4)


def _mixer_out_dw(sc, o, dya, dyb):
    S = sc.shape[0]
    tk = _tk(S)
    nk = S // tk

    def body(sc_ref, o_ref, da_ref, db_ref, dwc_ref, dwa_ref, acc_c, acc_a):
        k = pl.program_id(0)

        @pl.when(k == 0)
        def _():
            acc_c[...] = jnp.zeros_like(acc_c)
            acc_a[...] = jnp.zeros_like(acc_a)

        acc_c[...] += _tn(sc_ref[...], da_ref[...])
        acc_a[...] += _tn(o_ref[...], db_ref[...])

        @pl.when(k == nk - 1)
        def _():
            for s in range(4):
                dwc_ref[s] = acc_c[:, s * 256:(s + 1) * 256].astype(BF16)
                dwa_ref[s] = acc_a[:, s * 256:(s + 1) * 256].astype(BF16)

    narrow = pl.BlockSpec((tk, 512), lambda k: (k, 0))
    wide = pl.BlockSpec((tk, D), lambda k: (k, 0))
    out = pl.BlockSpec((4, 512, 256), lambda k: (0, 0, 0))
    return pl.pallas_call(
        body, name="mixer_out_dw", grid=(nk,), in_specs=[narrow, narrow, wide, wide], out_specs=[out, out],
        out_shape=[_sds((4, 512, 256), BF16)] * 2,
        scratch_shapes=[pltpu.VMEM((512, D), F32), pltpu.VMEM((512, D), F32)],
        compiler_params=_params(("arbitrary",)),
    )(sc, o, dya, dyb)


ADA_SHARD = 2304
ADA_TN = 768


def _ada_fwd(c_all, w_ada_l, b_l):
    def body(c_ref, w_ref, b_ref, o_ref):
        cv = c_ref[...]
        ca = cv * _sigmoid(cv)
        o_ref[...] = jnp.dot(ca.astype(BF16), w_ref[...].astype(BF16), preferred_element_type=F32) + b_ref[...]

    return pl.pallas_call(
        body, name="ada_fwd", grid=(ADA_SHARD // ADA_TN,),
        in_specs=[pl.BlockSpec((8, D), lambda j: (0, 0)), pl.BlockSpec((D, ADA_TN), lambda j: (0, j)),
                  pl.BlockSpec((1, ADA_TN), lambda j: (0, j))],
        out_specs=pl.BlockSpec((8, ADA_TN), lambda j: (0, j)), out_shape=_sds((8, ADA_SHARD), F32),
        compiler_params=_params(("parallel",)),
    )(c_all, w_ada_l, b_l)


def _ada_bwd(c_all_t, dmod_l):
    def body(c_ref, d_ref, o_ref):
        cv = c_ref[...]
        ca = cv * _sigmoid(cv)
        o_ref[...] = jnp.dot(ca.astype(BF16).astype(F32), d_ref[...].astype(BF16).astype(F32),
                             preferred_element_type=F32, precision=lax.Precision.HIGHEST)

    return pl.pallas_call(
        body, name="ada_bwd", grid=(ADA_SHARD // ADA_TN,),
        in_specs=[pl.BlockSpec((D, 8), lambda j: (0, 0)), pl.BlockSpec((8, ADA_TN), lambda j: (0, j))],
        out_specs=pl.BlockSpec((D, ADA_TN), lambda j: (0, j)), out_shape=_sds((D, ADA_SHARD), F32),
        compiler_params=_params(("parallel",)),
    )(c_all_t, dmod_l)


def _sum_rows(name, x):
    n = x.shape[1]

    def body(x_ref, o_ref):
        s = x_ref[0:1, :]
        for d in range(1, 8):
            s = s + x_ref[d:d + 1, :]
        o_ref[...] = s

    return pl.pallas_call(
        body, name=name, in_specs=[pl.BlockSpec(memory_space=pltpu.VMEM)],
        out_specs=pl.BlockSpec(memory_space=pltpu.VMEM), out_shape=_sds((1, n), F32),
        compiler_params=pltpu.CompilerParams(vmem_limit_bytes=VMEM_LIMIT),
    )(x)


def _pair_sum(name, g4, recv, c_idx):
    _, _, h, C = g4.shape
    tr = h if h <= 512 else h // (h // 256) if h % 256 == 0 else h // 2

    def body(c_ref, g_ref, r_ref, o_ref):
        o_ref[...] = (g_ref[...].astype(F32) + r_ref[...].astype(F32)).astype(BF16)

    grid_spec = pltpu.PrefetchScalarGridSpec(
        num_scalar_prefetch=1, grid=(4, h // tr),
        in_specs=[pl.BlockSpec((None, None, tr, C), lambda k, i, c: (k, c[0], i, 0)),
                  pl.BlockSpec((None, tr, C), lambda k, i, c: (k, i, 0))],
        out_specs=pl.BlockSpec((None, tr, C), lambda k, i, c: (k, i, 0)))
    return pl.pallas_call(
        body, name=name, grid_spec=grid_spec, out_shape=_sds((4, h, C), BF16),
        compiler_params=_params(("parallel", "parallel")),
    )(c_idx, g4, recv)


def _sum4(name, q, p, chip_idx):
    _, h, C = q.shape
    tr = h if h <= 512 else h // (h // 256) if h % 256 == 0 else h // 2

    def body(k_ref, q_ref, p_ref, o_ref):
        me = k_ref[0]
        terms = [jnp.where(me == k, p_ref[...], q_ref[k]).astype(F32) for k in range(4)]
        o_ref[...] = ((terms[0] + terms[1]) + terms[2]) + terms[3]

    grid_spec = pltpu.PrefetchScalarGridSpec(
        num_scalar_prefetch=1, grid=(h // tr,),
        in_specs=[pl.BlockSpec((4, tr, C), lambda i, k: (0, i, 0)),
                  pl.BlockSpec((None, tr, C), lambda i, k: (k[0], i, 0))],
        out_specs=pl.BlockSpec((tr, C), lambda i, k: (i, 0)))
    return pl.pallas_call(
        body, name=name, grid_spec=grid_spec, out_shape=_sds((h, C), F32), compiler_params=_params(("parallel",)),
    )(chip_idx, q, p)


def _sum8(name, q, p, idx):
    _, _, h, C = q.shape
    tr = h if h <= 256 else 256 if h % 256 == 0 else h // 2

    def body(k_ref, q_ref, p_ref, o_ref):
        total = None
        for j in range(4):
            for cc in range(2):
                mine = (k_ref[0] == j) & (k_ref[1] == cc)
                term = jnp.where(mine, p_ref[...], q_ref[j, cc]).astype(F32)
                total = term if total is None else total + term
        o_ref[...] = total

    grid_spec = pltpu.PrefetchScalarGridSpec(
        num_scalar_prefetch=1, grid=(h // tr,),
        in_specs=[pl.BlockSpec((4, 2, tr, C), lambda i, k: (0, 0, i, 0)),
                  pl.BlockSpec((None, None, tr, C), lambda i, k: (k[0], k[1], i, 0))],
        out_specs=pl.BlockSpec((tr, C), lambda i, k: (i, 0)))
    return pl.pallas_call(
        body, name=name, grid_spec=grid_spec, out_shape=_sds((h, C), F32), compiler_params=_params(("parallel",)),
    )(idx, q, p)


def _adamw(name, w, g, m, v):
    R, C = w.shape
    tr = R
    while tr * C * 4 > (1 << 20) and tr % 16 == 0:
        tr //= 2
    c1 = 1.0 - ADAM_B1 ** ADAM_STEP
    c2 = 1.0 - ADAM_B2 ** ADAM_STEP

    def body(w_ref, g_ref, m_ref, v_ref, d_ref, nm_ref, nv_ref):
        gv = g_ref[...]
        nm = ADAM_B1 * m_ref[...] + (1.0 - ADAM_B1) * gv
        nv = ADAM_B2 * v_ref[...] + (1.0 - ADAM_B2) * (gv * gv)
        nm_ref[...] = nm
        nv_ref[...] = nv
        d_ref[...] = -ADAM_LR * ((nm * (1.0 / c1)) / (jnp.sqrt(nv * (1.0 / c2)) + ADAM_EPS) + ADAM_WD * w_ref[...])

    spec = pl.BlockSpec((tr, C), lambda i: (i, 0))
    return pl.pallas_call(
        body, name=name, grid=(R // tr,), in_specs=[spec] * 4, out_specs=[spec] * 3,
        out_shape=[_sds((R, C), F32)] * 3, compiler_params=_params(("parallel",)),
    )(w, g, m, v)


def _adamw_halves(name, w, own, sib, m, v, c_idx):
    R, C = w.shape
    h = R // 2
    tr = h
    while tr * C * 4 > (1 << 20) and tr % 16 == 0:
        tr //= 2
    nb = h // tr
    c1 = 1.0 - ADAM_B1 ** ADAM_STEP
    c2 = 1.0 - ADAM_B2 ** ADAM_STEP

    def body(c_ref, w_ref, own_ref, sib_ref, m_ref, v_ref, g_ref, d_ref, nm_ref, nv_ref):
        mine = (pl.program_id(0) // nb) == c_ref[0]
        gv = jnp.where(mine, own_ref[...], sib_ref[...])
        nm = ADAM_B1 * m_ref[...] + (1.0 - ADAM_B1) * gv
        nv = ADAM_B2 * v_ref[...] + (1.0 - ADAM_B2) * (gv * gv)
        g_ref[...] = gv
        nm_ref[...] = nm
        nv_ref[...] = nv
        d_ref[...] = -ADAM_LR * ((nm * (1.0 / c1)) / (jnp.sqrt(nv * (1.0 / c2)) + ADAM_EPS) + ADAM_WD * w_ref[...])

    spec = pl.BlockSpec((tr, C), lambda i, c: (i, 0))
    half = pl.BlockSpec((tr, C), lambda i, c: (i % nb, 0))
    grid_spec = pltpu.PrefetchScalarGridSpec(
        num_scalar_prefetch=1, grid=(R // tr,), in_specs=[spec, half, half, spec, spec], out_specs=[spec] * 4)
    return pl.pallas_call(
        body, name=name, grid_spec=grid_spec, out_shape=[_sds((R, C), F32)] * 4,
        compiler_params=_params(("parallel",)),
    )(c_idx, w, own, sib, m, v)


def _pack(g, scale, shift, gate):
    rows = jnp.stack([g, scale, shift, gate]).astype(F32)
    return jnp.concatenate([rows, jnp.zeros((4, D), F32)], axis=0)


def _fold8(vec):
    m = -(-vec.shape[0] // (8 * LANES)) * LANES
    return jnp.concatenate([vec, jnp.zeros((8 * m - vec.shape[0],), vec.dtype)]).reshape(8, m)


def _allgather_vectors(name, vec):
    return _allgather_rows(name, _fold8(vec)).reshape(8, -1)


def kernel(x, c, w_ada, b_ada, norm1_g, ffn1_w_gu, ffn1_w_down, norm2_g, w_mix_in, b_merge, conv_w, w_conv_out, w_attn_out, w_out, norm3_g, ffn2_w_gu, ffn2_w_down, final_g, loss_target, m_w_ada, m_b_ada, m_norm1_g, m_ffn1_w_gu, m_ffn1_w_down, m_norm2_g, m_w_mix_in, m_b_merge, m_conv_w, m_w_conv_out, m_w_attn_out, m_w_out, m_norm3_g, m_ffn2_w_gu, m_ffn2_w_down, m_final_g, v_w_ada, v_b_ada, v_norm1_g, v_ffn1_w_gu, v_ffn1_w_down, v_norm2_g, v_w_mix_in, v_b_merge, v_conv_w, v_w_conv_out, v_w_attn_out, v_w_out, v_norm3_g, v_ffn2_w_gu, v_ffn2_w_down, v_final_g):
    xi, yi, ci = lax.axis_index("x"), lax.axis_index("y"), lax.axis_index("c")
    chip = 2 * xi + yi
    dev = 4 * xi + 2 * yi + ci
    S = x.shape[1]
    h0 = x[0]
    target = loss_target[0]

    (wgu1,) = _allgather_weights([ffn1_w_gu[0].astype(BF16)])
    late_shards = [w[0].astype(BF16) for w in (w_conv_out, w_attn_out, w_out, ffn2_w_gu, ffn2_w_down)]
    c_idx = jnp.reshape(ci, (1,)).astype(jnp.int32)
    chip_idx = jnp.reshape(chip, (1,)).astype(jnp.int32)

    def reduce_pairs(tag, names, grads):
        g4 = [g.reshape(4, 2, g.shape[1] // 2, g.shape[2]) for g in grads]
        recv = _sibling_swap_halves("grad_sibling_swap_" + tag, g4)
        return [_pair_sum("pair_sum_" + nm, a, b, c_idx) for nm, a, b in zip(names, g4, recv)]

    small = jnp.concatenate([c[0], b_merge[0].reshape(-1), conv_w[0].reshape(-1)])
    gathered = _allgather_vectors("allgather_small", small)
    c_all = gathered[:, :D]
    per_chip = gathered[0::2]
    bm_full = jnp.concatenate([per_chip[k, D:D + 512].reshape(2, 256) for k in range(4)], axis=1)
    cw_full = jnp.concatenate([per_chip[k, D + 512:D + 896].reshape(3, 128) for k in range(4)], axis=1)
    b_l = lax.dynamic_slice_in_dim(b_ada, chip * ADA_SHARD, ADA_SHARD, axis=1)
    mod_l = _ada_fwd(c_all, w_ada[0], b_l)
    mod_g = _allgather_rows("allgather_mod", mod_l).reshape(8, 8, ADA_SHARD)
    mod_all = jnp.concatenate([mod_g[2 * k] for k in range(4)], axis=1)
    mod = lax.dynamic_slice_in_dim(mod_all, dev, 1, axis=0).reshape(3, 3, D)
    p1 = _pack(norm1_g[0], mod[0, 1], mod[0, 0], mod[0, 2])
    p2 = _pack(norm2_g[0], mod[1, 1], mod[1, 0], mod[1, 2])
    p3 = _pack(norm3_g[0], mod[2, 1], mod[2, 0], mod[2, 2])
    pf = _pack(final_g, final_g, final_g, final_g)

    u1 = _norm_mod_fwd("norm1_fwd", h0, p1)
    gu1, hm1, (wd1,) = _ffn_up("ffn1_up", u1, wgu1, [ffn1_w_down[0].astype(BF16)])
    f1, h1, u2, (wmix,) = _proj_residual("ffn1_down", hm1, wd1.reshape(D_FF, D), h0, p1, 0.5, p2,
                                         gather_shards=[w_mix_in[0].astype(BF16)])
    wd1 = wd1.reshape(D_FF, D)

    def mix_in_prod(a_ref, b_ref):
        av = a_ref[...]
        return [jnp.dot(av, b_ref[s], preferred_element_type=F32) for s in range(4)]

    def mix_in_store(accs, ex, outs):
        for s, acc in enumerate(accs):
            outs[0][:, s * MIX_SHARD:(s + 1) * MIX_SHARD] = acc.astype(BF16)

    proj = _mm("mix_in", (S // TM,), u2, pl.BlockSpec((TM, D), lambda i: (i, 0)),
               wmix, pl.BlockSpec((4, D, MIX_SHARD), lambda i: (0, 0, 0)), (1, 0),
               [_sds((S, MIX_W), BF16)], [pl.BlockSpec((TM, MIX_W), lambda i: (i, 0))], mix_in_store,
               semantics=("parallel",), prod_fn=mix_in_prod)[0]
    sc = _conv_fwd(proj, cw_full)
    o, t_tot, (wco, wao, wout, wgu2, wd2) = _attn_fwd(proj, late_shards)
    wout = wout.reshape(D, D)
    wd2 = wd2.reshape(D_FF, D)
    ya, yb, merged = _merge_fwd(sc, o, wco, wao, proj, bm_full)
    y2, h2, u3, _ = _proj_residual("mix_out", merged, wout, h1, p2, 1.0, p3)
    gu3, hm3, _ = _ffn_up("ffn2_up", u3, wgu2)

    dh3, df3, sums_f, loss_blk = _proj_residual_loss("ffn2_down", hm3, wd2, h2, p3, 0.5, pf, target)
    (dh2, dy2, sums3), dwgu2, dwd2 = _ffn_bwd("ffn2", df3, u3, gu3, hm3, wgu2, wd2,
                                              _norm_bwd_tail(h2, p3, dh3, prev=(p2, y2, 1.0)))

    dya, dyb, dga, dgb, sums_bm = _merge_bwd(dy2, wout, ya, yb, proj, bm_full)
    tk = _tk(S)
    dwout = _mm("dw_out", (1, S // tk), merged, pl.BlockSpec((tk, D), lambda n, k: (k, 0)),
                dy2, pl.BlockSpec((tk, D), lambda n, k: (k, 0)), (0, 0),
                [_sds((D, D), BF16)], [pl.BlockSpec((D, D), lambda n, k: (0, 0))], _store(BF16),
                nk=S // tk, acc_shape=(D, D))[0]
    dsc, do = _mixer_out_bwd(dya, dyb, wco, wao)
    dwco, dwao = _mixer_out_dw(sc, o, dya, dyb)
    dcb, dcc, dcx, dcw = _conv_bwd(dsc, proj, cw_full)
    names_e = ["ffn2_w_gu", "ffn2_w_down", "w_out", "w_conv_out", "w_attn_out"]
    part_e = [g.reshape(4, 2, g.shape[1] // 2, g.shape[2])
              for g in (dwgu2, dwd2.reshape(4, 704, D), dwout.reshape(4, 256, D), dwco, dwao)]
    dq, dk, dv, came_e = _attn_bwd(proj, t_tot, do, part_e)
    dproj = jnp.concatenate([dcb, dcc, dcx, dq, dk, dv, dga, dgb], axis=1)
    tail2 = _norm_bwd_tail(h1, p2, dh2, prev=(p1, f1, 0.5))

    def mix_prod(a_ref, b_ref):
        acc = _nt(a_ref[:, 0:MIX_SHARD], b_ref[0])
        for s in range(1, 4):
            acc = acc + _nt(a_ref[:, s * MIX_SHARD:(s + 1) * MIX_SHARD], b_ref[s])
        return acc

    dh1, df1, sums2 = _mm("mix_in_bwd", (S // TB,), dproj, pl.BlockSpec((TB, MIX_W), lambda i: (i, 0)),
                          wmix, pl.BlockSpec((4, D, MIX_SHARD), lambda i: (0, 0, 0)), (1, 1),
                          tail2["out_shapes"], tail2["out_specs"], tail2["epilogue"], extras=tail2["extras"],
                          extra_specs=tail2["extra_specs"], semantics=("arbitrary",), prod_fn=mix_prod)
    dwmix = _grad_w("dw_mix_in", u2, D, dproj, MIX_SHARD, lambda s, k: (k, s), 4, (4, D, MIX_SHARD),
                    (None, D, MIX_SHARD), lambda s, k: (s, 0, 0))

    part_mix = reduce_pairs("mix", ["w_mix_in"], [dwmix])
    dgu1, *came_mix = _ffn_down_bwd("ffn1_down_bwd", df1, wd1, gu1, a2a_parts=part_mix)
    dwd1 = _ffn_dw_down("ffn1_dw_down", hm1, df1)
    part_wd1 = reduce_pairs("wd1", ["ffn1_w_down"], [dwd1.reshape(4, 704, D)])
    dwgu1, *came_wd1 = _ffn_dw_gu("ffn1_dw_gu", u1, dgu1, a2a_parts=part_wd1)
    part_gu1 = reduce_pairs("gu1", ["ffn1_w_gu"], [dwgu1])
    grad_x, sums1, *came_gu1 = _ffn_up_bwd("ffn1_up_bwd", dgu1, wgu1, _norm_bwd_tail(h0, p1, dh1),
                                           a2a_parts=part_gu1)

    dmod = jnp.stack([sums1[0], sums1[1], sums2[3], sums2[0], sums2[1], sums3[3], sums3[0], sums3[1], sums_f[1]])
    small_g = jnp.concatenate([dmod.reshape(-1), sums1[2], sums2[2], sums3[2], sums_f[0],
                               sums_bm[0], sums_bm[1], dcw.reshape(-1), loss_blk[0, 0:1]])
    all_g = _allgather_vectors("allgather_small_grads", small_g)
    tot = _sum_rows("sum_small_grads", all_g)[0]
    loss = tot[16 * D + 512]
    g_b_ada = tot[:9 * D][None, :]
    g_n1, g_n2, g_n3 = (tot[(9 + k) * D:(10 + k) * D][None, :] for k in range(3))
    g_fin = tot[12 * D:13 * D]
    g_bm = lax.dynamic_slice_in_dim(tot[13 * D:15 * D].reshape(2, D), chip * 256, 256, axis=1)[None]
    g_cw = lax.dynamic_slice_in_dim(tot[15 * D:16 * D + 512].reshape(3, 512), chip * 128, 128, axis=1)[None]
    dmod_l = lax.dynamic_slice_in_dim(all_g[:, :9 * D], chip * ADA_SHARD, ADA_SHARD, axis=1)
    g_w_ada = _ada_bwd(c_all.T, dmod_l)[None]

    names_l = ["w_mix_in", "ffn1_w_down", "ffn1_w_gu"]
    names = names_e + names_l
    place_idx = jnp.stack([chip, ci]).astype(jnp.int32)
    half = [_sum8("device_sum_" + nm, q, p, place_idx) for nm, q, p in zip(names_e, came_e, part_e)]
    half += [_sum4("chip_sum_" + nm, q, p, chip_idx)
             for nm, q, p in zip(names_l, came_mix + came_wd1 + came_gu1, part_mix + part_wd1 + part_gu1)]
    g_own = dict(zip(names, half))
    g_sib = dict(zip(names, _sibling_share(half)))

    weights = dict(w_ada=w_ada, b_ada=b_ada, norm1_g=norm1_g, ffn1_w_gu=ffn1_w_gu, ffn1_w_down=ffn1_w_down,
                   norm2_g=norm2_g, w_mix_in=w_mix_in, b_merge=b_merge, conv_w=conv_w, w_conv_out=w_conv_out,
                   w_attn_out=w_attn_out, w_out=w_out, norm3_g=norm3_g, ffn2_w_gu=ffn2_w_gu,
                   ffn2_w_down=ffn2_w_down, final_g=final_g)
    ms = dict(w_ada=m_w_ada, b_ada=m_b_ada, norm1_g=m_norm1_g, ffn1_w_gu=m_ffn1_w_gu, ffn1_w_down=m_ffn1_w_down,
              norm2_g=m_norm2_g, w_mix_in=m_w_mix_in, b_merge=m_b_merge, conv_w=m_conv_w, w_conv_out=m_w_conv_out,
              w_attn_out=m_w_attn_out, w_out=m_w_out, norm3_g=m_norm3_g, ffn2_w_gu=m_ffn2_w_gu,
              ffn2_w_down=m_ffn2_w_down, final_g=m_final_g)
    vs = dict(w_ada=v_w_ada, b_ada=v_b_ada, norm1_g=v_norm1_g, ffn1_w_gu=v_ffn1_w_gu, ffn1_w_down=v_ffn1_w_down,
              norm2_g=v_norm2_g, w_mix_in=v_w_mix_in, b_merge=v_b_merge, conv_w=v_conv_w, w_conv_out=v_w_conv_out,
              w_attn_out=v_w_attn_out, w_out=v_w_out, norm3_g=v_norm3_g, ffn2_w_gu=v_ffn2_w_gu,
              ffn2_w_down=v_ffn2_w_down, final_g=v_final_g)
    order = list(weights)
    grad = dict(w_ada=g_w_ada, b_ada=g_b_ada, norm1_g=g_n1, norm2_g=g_n2, norm3_g=g_n3, final_g=g_fin,
                b_merge=g_bm, conv_w=g_cw)
    delta, new_m, new_v = {}, {}, {}
    small_names = ["b_ada", "norm1_g", "norm2_g", "norm3_g", "final_g", "b_merge", "conv_w"]
    flat = lambda d: jnp.concatenate([d[nm].reshape(-1) for nm in small_names])[None, :]
    sd, sm, sv = _adamw("adamw_small", flat(weights), flat(grad), flat(ms), flat(vs))
    off = 0
    for nm in small_names:
        size = weights[nm].size
        for dst, src in ((delta, sd), (new_m, sm), (new_v, sv)):
            dst[nm] = src[0, off:off + size].reshape(weights[nm].shape)
        off += size
    for nm in order:
        if nm in small_names:
            continue
        shp = weights[nm].shape
        if nm in g_own:
            g2, d2, m2, v2 = _adamw_halves("adamw_" + nm, weights[nm][0], g_own[nm], g_sib[nm], ms[nm][0], vs[nm][0],
                                           c_idx)
            grad[nm] = g2.reshape(shp)
        else:
            d2, m2, v2 = _adamw("adamw_" + nm, weights[nm][0], grad[nm][0], ms[nm][0], vs[nm][0])
        delta[nm], new_m[nm], new_v[nm] = d2.reshape(shp), m2.reshape(shp), v2.reshape(shp)

    return (loss, grad_x[None], *[grad[nm] for nm in order], *[delta[nm] for nm in order],
            *[new_m[nm] for nm in order], *[new_v[nm] for nm in order])
```

```python
import functools

import jax
import jax.numpy as jnp
from jax import lax
from jax.experimental import pallas as pl
from jax.experimental.pallas import tpu as pltpu

F32 = jnp.float32
BF16 = jnp.bfloat16
MESH = pl.DeviceIdType.MESH

VMEM_LIMIT = 56 * 1024 * 1024
LANES = 128

D = 1024
D_FF = 2816
FF_SHARD = 1408
MIX_SHARD = 1280
MIX_W = 5120
HEAD_PAIRS = 4
HEAD_DIM = 64
CONV_W = 512
EPS = 1e-6
ATT_BLK = 256

ADAM_LR = 0.001
ADAM_B1 = 0.9
ADAM_B2 = 0.999
ADAM_EPS = 1e-08
ADAM_WD = 0.01
ADAM_STEP = 10


def _params(semantics=None):
    return pltpu.CompilerParams(dimension_semantics=semantics, vmem_limit_bytes=VMEM_LIMIT)


def _sigmoid(x):
    return 1.0 / (1.0 + jnp.exp(-x))


def _place():
    x, y, c = lax.axis_index("x"), lax.axis_index("y"), lax.axis_index("c")
    chips = [(1 - x, y), (x, 1 - y), (1 - x, 1 - y)]
    return x, y, c, chips


def _allgather_rows(name, blk):
    m_per, n = blk.shape

    def body(x_ref, out_ref, send_sems, recv_sems, local_sem):
        x, y, c, chips = _place()
        me, sibling = (x, y, c), (x, y, 1 - c)

        def rows(px, py, pc):
            return out_ref.at[pl.ds((4 * px + 2 * py + pc) * m_per, m_per), :]

        def copy(k, block, to, src=None):
            return pltpu.make_async_remote_copy(
                src_ref=rows(*block) if src is None else src, dst_ref=rows(*block),
                send_sem=send_sems.at[k], recv_sem=recv_sems.at[k], device_id=to, device_id_type=MESH)

        mine = pltpu.make_async_copy(x_ref, rows(*me), local_sem)
        mine.start()
        first = [copy(0, me, sibling, src=x_ref)]
        first += [copy(1 + j, me, (*chip, c), src=x_ref) for j, chip in enumerate(chips)]
        for cp in first:
            cp.start()
        passed = [copy(4 + j, (*chip, c), sibling) for j, chip in enumerate(chips)]
        for j, chip in enumerate(chips):
            copy(1 + j, (*chip, c), me).wait_recv()
            passed[j].start()
        copy(0, sibling, me).wait_recv()
        for j, chip in enumerate(chips):
            copy(4 + j, (*chip, 1 - c), me).wait_recv()
        for cp in first + passed:
            cp.wait_send()
        mine.wait()

    return pl.pallas_call(
        body, name=name,
        out_shape=jax.ShapeDtypeStruct((8 * m_per, n), blk.dtype),
        in_specs=[pl.BlockSpec(memory_space=pltpu.VMEM)],
        out_specs=pl.BlockSpec(memory_space=pltpu.VMEM),
        scratch_shapes=[pltpu.SemaphoreType.DMA((7,)), pltpu.SemaphoreType.DMA((7,)), pltpu.SemaphoreType.DMA],
        compiler_params=pltpu.CompilerParams(vmem_limit_bytes=VMEM_LIMIT),
    )(blk)


def _hbm_specs(n):
    return [pl.BlockSpec(memory_space=pltpu.HBM)] * n


def _allgather_weights(shards):
    n = len(shards)

    def body(*refs):
        start, relay, finish = _gather_protocol(refs[:n], refs[n:2 * n], *refs[2 * n:])
        start()
        relay()
        finish()

    gathered = pl.pallas_call(
        body, name="allgather_weights",
        out_shape=_gather_shapes(shards), in_specs=_hbm_specs(n), out_specs=_hbm_specs(n),
        scratch_shapes=_gather_sems(n),
    )(*shards)
    return _with_own_shard(gathered, shards)


def _gather_shapes(shards):
    return [jax.ShapeDtypeStruct((4, *s.shape), s.dtype) for s in shards]


def _gather_sems(n):
    return [pltpu.SemaphoreType.DMA((6 * n,)), pltpu.SemaphoreType.DMA((6 * n,))]


def _with_own_shard(gathered, shards):
    chip = 2 * lax.axis_index("x") + lax.axis_index("y")
    return [lax.dynamic_update_slice(g, s[None], (chip, 0, 0)) for g, s in zip(gathered, shards)]


def _gather_protocol(ins, outs, send_sems, recv_sems):
    n = len(ins)
    x, y, c, chips = _place()
    me, sibling = (x, y, c), (x, y, 1 - c)
    me_k = 2 * x + y

    def half(w, k, hc):
        h = ins[w].shape[0] // 2
        return outs[w].at[k, pl.ds(pl.multiple_of(hc * h, 8), h), :]

    def copy(w, j, k, hc, to, src=None):
        dst = half(w, k, hc)
        return pltpu.make_async_remote_copy(
            src_ref=dst if src is None else src, dst_ref=dst,
            send_sem=send_sems.at[6 * w + j], recv_sem=recv_sems.at[6 * w + j],
            device_id=to, device_id_type=MESH)

    def first(w, j):
        h = ins[w].shape[0] // 2
        src = ins[w].at[pl.ds(pl.multiple_of(c * h, 8), h), :]
        return copy(w, j, me_k, c, (*chips[j], c), src=src)

    def passed(w, j):
        px, py = chips[j]
        return copy(w, 3 + j, 2 * px + py, c, sibling)

    pairs = [(w, j) for w in range(n) for j in range(3)]

    def start():
        for w, j in pairs:
            first(w, j).start()

    def relay():
        for w, j in pairs:
            px, py = chips[j]
            copy(w, j, 2 * px + py, c, me).wait_recv()
            passed(w, j).start()

    def finish():
        for w, j in pairs:
            px, py = chips[j]
            copy(w, 3 + j, 2 * px + py, 1 - c, me).wait_recv()
        for w, j in pairs:
            first(w, j).wait_send()
            passed(w, j).wait_send()

    return start, relay, finish


def _sibling_swap_halves(name, grads):
    n = len(grads)

    def body(*refs):
        ins, outs = refs[:n], refs[n:2 * n]
        send_sems, recv_sems = refs[2 * n:]
        x, y, c, _ = _place()
        cps = []
        for w in range(n):
            cp = pltpu.make_async_remote_copy(
                src_ref=ins[w].at[:, 1 - c], dst_ref=outs[w],
                send_sem=send_sems.at[w], recv_sem=recv_sems.at[w],
                device_id=(x, y, 1 - c), device_id_type=MESH)
            cp.start()
            cps.append(cp)
        for cp in cps:
            cp.wait()

    return pl.pallas_call(
        body, name=name,
        out_shape=[jax.ShapeDtypeStruct((4, *g.shape[2:]), g.dtype) for g in grads],
        in_specs=_hbm_specs(n), out_specs=_hbm_specs(n),
        scratch_shapes=[pltpu.SemaphoreType.DMA((n,)), pltpu.SemaphoreType.DMA((n,))],
    )(*grads)


def _all_to_all_sems(n):
    return [pltpu.SemaphoreType.DMA((3 * n,)), pltpu.SemaphoreType.DMA((3 * n,))]


def _all_to_all_protocol(ins, outs, send_sems, recv_sems):
    n = len(ins)
    x, y, c, chips = _place()
    me_k = 2 * x + y
    pairs = [(w, j) for w in range(n) for j in range(3)]

    def sent(w, j):
        px, py = chips[j]
        return pltpu.make_async_remote_copy(
            src_ref=ins[w].at[2 * px + py], dst_ref=outs[w].at[me_k],
            send_sem=send_sems.at[3 * w + j], recv_sem=recv_sems.at[3 * w + j],
            device_id=(px, py, c), device_id_type=MESH)

    def start():
        for w, j in pairs:
            sent(w, j).start()

    def finish():
        for w, j in pairs:
            px, py = chips[j]
            slab = outs[w].at[2 * px + py]
            pltpu.make_async_remote_copy(
                src_ref=slab, dst_ref=slab, send_sem=send_sems.at[3 * w + j],
                recv_sem=recv_sems.at[3 * w + j], device_id=(px, py, c), device_id_type=MESH).wait_recv()
        for w, j in pairs:
            sent(w, j).wait_send()

    return start, finish


def _reduce8_sems(n):
    return [pltpu.SemaphoreType.DMA((7 * n,)), pltpu.SemaphoreType.DMA((7 * n,))]


def _reduce8_protocol(ins, outs, send_sems, recv_sems):
    n = len(ins)
    x, y, c, chips = _place()
    me_k = 2 * x + y
    far = [(w, j, hc) for w in range(n) for j in range(3) for hc in range(2)]

    def sent(w, j, hc):
        px, py = chips[j]
        return pltpu.make_async_remote_copy(
            src_ref=ins[w].at[2 * px + py, hc], dst_ref=outs[w].at[me_k, c],
            send_sem=send_sems.at[7 * w + 2 * j + hc], recv_sem=recv_sems.at[7 * w + 2 * j + c],
            device_id=(px, py, hc), device_id_type=MESH)

    def to_sibling(w):
        return pltpu.make_async_remote_copy(
            src_ref=ins[w].at[me_k, 1 - c], dst_ref=outs[w].at[me_k, c],
            send_sem=send_sems.at[7 * w + 6], recv_sem=recv_sems.at[7 * w + 6],
            device_id=(x, y, 1 - c), device_id_type=MESH)

    def arrival(w, slab, k):
        return pltpu.make_async_remote_copy(
            src_ref=slab, dst_ref=slab, send_sem=send_sems.at[7 * w + k], recv_sem=recv_sems.at[7 * w + k],
            device_id=(x, y, c), device_id_type=MESH)

    def start():
        for w in range(n):
            to_sibling(w).start()
        for w, j, hc in far:
            sent(w, j, hc).start()

    def finish():
        for w in range(n):
            arrival(w, outs[w].at[me_k, 1 - c], 6).wait_recv()
        for w, j, cc in far:
            px, py = chips[j]
            arrival(w, outs[w].at[2 * px + py, cc], 2 * j + cc).wait_recv()
        for w in range(n):
            to_sibling(w).wait_send()
        for w, j, hc in far:
            sent(w, j, hc).wait_send()

    return start, finish


def _sibling_share(halves):
    n = len(halves)

    def body(*refs):
        ins, outs = refs[:n], refs[n:2 * n]
        send_sems, recv_sems = refs[2 * n:]
        x, y, c, _ = _place()
        cps = []
        for w in range(n):
            cp = pltpu.make_async_remote_copy(
                src_ref=ins[w], dst_ref=outs[w], send_sem=send_sems.at[w], recv_sem=recv_sems.at[w],
                device_id=(x, y, 1 - c), device_id_type=MESH)
            cp.start()
            cps.append(cp)
        for cp in cps:
            cp.wait()

    return pl.pallas_call(
        body, name="grad_sibling_share",
        out_shape=[jax.ShapeDtypeStruct(p.shape, p.dtype) for p in halves],
        in_specs=_hbm_specs(n), out_specs=_hbm_specs(n),
        scratch_shapes=[pltpu.SemaphoreType.DMA((n,)), pltpu.SemaphoreType.DMA((n,))],
    )(*halves)


def _mm(name, grid, a, a_spec, b, b_spec, contract, out_shapes, out_specs, epilogue,
        extras=(), extra_specs=(), nk=1, acc_shape=None, semantics=None, a2a_parts=(), gather_shards=(),
        relay_at=None, prod_fn=None):
    assert not (a2a_parts and gather_shards)
    moved = tuple(a2a_parts) + tuple(gather_shards)
    ne, no, nc = len(extras), len(out_shapes), len(moved)
    nd = len(grid)

    def body(*refs):
        a_ref, b_ref = refs[0], refs[1]
        ex, outs = refs[2:2 + ne], refs[2 + ne + nc:2 + ne + nc + no]
        if nc:
            ids = [pl.program_id(d) for d in range(nd)]
            comm_refs = (refs[2 + ne:2 + ne + nc], refs[2 + ne + nc + no:2 + ne + 2 * nc + no], *refs[-2:])
            at_start = functools.reduce(jnp.logical_and, [i == 0 for i in ids])
            if a2a_parts:
                start, finish = _all_to_all_protocol(*comm_refs)
                pl.when(at_start)(start)
            else:
                start, relay, finish = _gather_protocol(*comm_refs)
                pl.when(at_start)(start)
                at_relay = grid[0] // 2 if relay_at is None else relay_at
                pl.when(functools.reduce(jnp.logical_and, [ids[0] == at_relay] + [i == 0 for i in ids[1:]]))(relay)

        def prod():
            if prod_fn is not None:
                return prod_fn(a_ref, b_ref)
            return lax.dot_general(a_ref[...], b_ref[...], (((contract[0],), (contract[1],)), ((), ())),
                                   preferred_element_type=F32)

        if nk == 1:
            epilogue(prod(), ex, outs)
        else:
            acc = refs[2 + ne + 2 * nc + no]
            k = pl.program_id(nd - 1)

            @pl.when(k == 0)
            def _():
                acc[...] = prod()

            @pl.when(k > 0)
            def _():
                acc[...] += prod()

            @pl.when(k == nk - 1)
            def _():
                epilogue(acc[...], ex, outs)

        if nc:
            pl.when(functools.reduce(jnp.logical_and, [i == g - 1 for i, g in zip(ids, grid)]))(finish)

    if semantics is None or nc:
        semantics = ("arbitrary",) * nd
    return pl.pallas_call(
        body, name=name, grid=grid,
        in_specs=[a_spec, b_spec, *extra_specs] + _hbm_specs(nc),
        out_specs=list(out_specs) + _hbm_specs(nc),
        out_shape=list(out_shapes) + [jax.ShapeDtypeStruct(p.shape, p.dtype) for p in a2a_parts]
        + _gather_shapes(gather_shards),
        scratch_shapes=([] if nk == 1 else [pltpu.VMEM(acc_shape, F32)])
        + (_all_to_all_sems(nc) if a2a_parts else _gather_sems(nc) if gather_shards else []),
        compiler_params=_params(semantics),
    )(a, b, *extras, *moved)


def _store(dtype):
    def epilogue(acc, ex, outs):
        outs[0][...] = acc.astype(dtype)
    return epilogue


def _sds(shape, dtype):
    return jax.ShapeDtypeStruct(shape, dtype)


TR = 512


def _row_spec(width, tr=TR):
    return pl.BlockSpec((tr, width), lambda i: (i, 0))


def _const_spec(shape):
    nd = len(shape)
    return pl.BlockSpec(shape, lambda i: (0,) * nd)


def _norm_mod_fwd(name, h, p):
    S = h.shape[0]

    def body(h_ref, p_ref, u_ref):
        hv = h_ref[...]
        r = lax.rsqrt(jnp.mean(hv * hv, axis=-1, keepdims=True) + EPS)
        nrm = (hv * r) * p_ref[0:1, :]
        u_ref[...] = (nrm * (1.0 + p_ref[1:2, :]) + p_ref[2:3, :]).astype(BF16)

    return pl.pallas_call(
        body, name=name, grid=(S // TR,),
        in_specs=[_row_spec(D), _const_spec((8, D))], out_specs=_row_spec(D),
        out_shape=_sds((S, D), BF16), compiler_params=_params(("parallel",)),
    )(h, p)


def _rmsnorm_parts(hv):
    r = lax.rsqrt(jnp.mean(hv * hv, axis=-1, keepdims=True) + EPS)
    return r, hv * r


TB = 256


def _norm_bwd_tail(h, p, dh_res, prev=None):
    S = h.shape[0]
    row = pl.BlockSpec((TB, D), lambda i: (i, 0))
    const = pl.BlockSpec((8, D), lambda i: (0, 0))
    extras, specs = [h, p, dh_res], [row, const, row]
    out_shapes, out_specs = [_sds((S, D), F32)], [row]
    if prev is not None:
        extras += [prev[0], prev[1]]
        specs += [const, row]
        out_shapes.append(_sds((S, D), BF16))
        out_specs.append(row)
    out_shapes.append(_sds((8, D), F32))
    out_specs.append(const)

    def epilogue(duv, ex, outs):
        h_ref, p_ref, r_ref = ex[:3]
        dh_ref, sums_ref = outs[0], outs[-1]

        @pl.when(pl.program_id(0) == 0)
        def _():
            sums_ref[...] = jnp.zeros_like(sums_ref)

        g = p_ref[0:1, :]
        r, xn = _rmsnorm_parts(h_ref[...])
        dn = duv * (1.0 + p_ref[1:2, :])
        dxn = dn * g
        dh = r_ref[...] + r * (dxn - xn * jnp.mean(dxn * xn, axis=-1, keepdims=True))
        dh_ref[...] = dh
        sums_ref[0:1, :] += jnp.sum(duv, axis=0, keepdims=True)
        sums_ref[1:2, :] += jnp.sum(duv * (xn * g), axis=0, keepdims=True)
        sums_ref[2:3, :] += jnp.sum(dn * xn, axis=0, keepdims=True)
        if prev is not None:
            pp_ref, f_ref = ex[3:5]
            outs[1][...] = (prev[2] * pp_ref[3:4, :] * dh).astype(BF16)
            sums_ref[3:4, :] += prev[2] * jnp.sum(dh * f_ref[...].astype(F32), axis=0, keepdims=True)

    return dict(extras=tuple(extras), extra_specs=tuple(specs), out_shapes=out_shapes, out_specs=out_specs,
                epilogue=epilogue)


TM = 512


def _ffn_up(name, u, wgu4, shards=()):
    S = u.shape[0]
    n = len(shards)
    ni = S // TM

    def body(u_ref, wg_ref, wu_ref, *rest):
        gu_ref, hm_ref = rest[n:n + 2]
        s, i = pl.program_id(0), pl.program_id(1)
        if n:
            start, relay, finish = _gather_protocol(rest[:n], rest[n + 2:2 * n + 2], *rest[2 * n + 2:])
            pl.when((s == 0) & (i == 0))(start)
            pl.when((s == 1) & (i == 0))(relay)
        uv = u_ref[...]
        g = jnp.dot(uv, wg_ref[...], preferred_element_type=F32)
        up = jnp.dot(uv, wu_ref[...], preferred_element_type=F32)
        gu_ref[0] = g.astype(BF16)
        gu_ref[1] = up.astype(BF16)
        hm_ref[...] = (g * _sigmoid(g) * up).astype(BF16)
        if n:
            pl.when((s == 1) & (i == ni - 1))(finish)

    gu, hm, *gathered = pl.pallas_call(
        body, name=name, grid=(2, ni),
        in_specs=[pl.BlockSpec((TM, D), lambda s, i: (i, 0)),
                  pl.BlockSpec((None, D, FF_SHARD), lambda s, i: (s, 0, 0)),
                  pl.BlockSpec((None, D, FF_SHARD), lambda s, i: (s + 2, 0, 0))] + _hbm_specs(n),
        out_specs=[pl.BlockSpec((2, TM, FF_SHARD), lambda s, i: (0, i, s)),
                   pl.BlockSpec((TM, FF_SHARD), lambda s, i: (i, s))] + _hbm_specs(n),
        out_shape=[_sds((2, S, D_FF), BF16), _sds((S, D_FF), BF16)] + _gather_shapes(shards),
        scratch_shapes=_gather_sems(n) if n else [],
        compiler_params=_params(("arbitrary", "arbitrary") if n else ("parallel", "parallel")),
    )(u, wgu4, wgu4, *shards)
    return gu, hm, _with_own_shard(gathered, shards)


def _proj_residual(name, a, w, h, p, weight, p_next, gather_shards=()):
    S, K = a.shape

    def epilogue(acc, ex, outs):
        h_ref, p_ref, pn_ref = ex
        outs[0][...] = acc.astype(BF16)
        hout = h_ref[...] + weight * p_ref[3:4, :] * acc
        outs[1][...] = hout
        _, xn = _rmsnorm_parts(hout)
        outs[2][...] = ((xn * pn_ref[0:1, :]) * (1.0 + pn_ref[1:2, :]) + pn_ref[2:3, :]).astype(BF16)

    row = _row_spec(D, TM)
    res = _mm(
        name, (S // TM,), a, pl.BlockSpec((TM, K), lambda i: (i, 0)), w, pl.BlockSpec((K, D), lambda i: (0, 0)),
        (1, 0), [_sds((S, D), BF16), _sds((S, D), F32), _sds((S, D), BF16)], [row, row, row], epilogue,
        extras=(h, p, p_next), extra_specs=(row, _const_spec((8, D)), _const_spec((8, D))),
        semantics=("parallel",), gather_shards=gather_shards, relay_at=S // TM - 1)
    return res[0], res[1], res[2], _with_own_shard(res[3:], gather_shards)


def _proj_residual_loss(name, a, w, h, p, weight, gf, target):
    S, K = a.shape

    def epilogue(acc, ex, outs):
        h_ref, p_ref, g_ref, t_ref = ex
        dh_ref, df_ref, sums_ref, loss_ref = outs

        @pl.when(pl.program_id(0) == 0)
        def _():
            sums_ref[...] = jnp.zeros_like(sums_ref)
            loss_ref[...] = jnp.zeros_like(loss_ref)

        gate = p_ref[3:4, :]
        g = g_ref[0:1, :]
        r, xn = _rmsnorm_parts(h_ref[...] + weight * gate * acc)
        err = xn * g - t_ref[...]
        loss_ref[...] += 0.5 * jnp.sum(err * err) * (1.0 / D)
        dout = err * (1.0 / D)
        dxn = dout * g
        dh = r * (dxn - xn * jnp.mean(dxn * xn, axis=-1, keepdims=True))
        dh_ref[...] = dh
        df_ref[...] = (weight * gate * dh).astype(BF16)
        sums_ref[0:1, :] += jnp.sum(dout * xn, axis=0, keepdims=True)
        sums_ref[1:2, :] += weight * jnp.sum(dh * acc, axis=0, keepdims=True)

    row = _row_spec(D, TM)
    return _mm(
        name, (S // TM,), a, pl.BlockSpec((TM, K), lambda i: (i, 0)), w, pl.BlockSpec((K, D), lambda i: (0, 0)),
        (1, 0), [_sds((S, D), F32), _sds((S, D), BF16), _sds((8, D), F32), _sds((8, LANES), F32)],
        [row, row, _const_spec((8, D)), _const_spec((8, LANES))], epilogue,
        extras=(h, p, gf, target), extra_specs=(row, _const_spec((8, D)), _const_spec((8, D)), row),
        semantics=("arbitrary",))


def _ffn_down_bwd(name, df, wd, gu, a2a_parts=()):
    S = df.shape[0]

    def epilogue(acc, ex, outs):
        g = ex[0][0].astype(F32)
        up = ex[0][1].astype(F32)
        sg = _sigmoid(g)
        outs[0][0] = (acc * up * (sg * (1.0 + g * (1.0 - sg)))).astype(BF16)
        outs[0][1] = (acc * g * sg).astype(BF16)

    gu_spec = pl.BlockSpec((2, TM, FF_SHARD), lambda n, i: (0, i, n))
    return _mm(
        name, (2, S // TM), df, pl.BlockSpec((TM, D), lambda n, i: (i, 0)),
        wd, pl.BlockSpec((FF_SHARD, D), lambda n, i: (n, 0)), (1, 1),
        [_sds((2, S, D_FF), BF16)], [gu_spec], epilogue, extras=(gu,), extra_specs=(gu_spec,),
        semantics=("parallel", "parallel"), a2a_parts=a2a_parts)


TK = 2048


def _tk(S):
    return min(TK, S)


def _grad_w(name, a, a_w, b, b_w, b_map, n_out, out_shape, out_block, out_map, a2a_parts=()):
    S = a.shape[0]
    tk = _tk(S)
    nk = S // tk
    res = _mm(
        name, (n_out, nk), a, pl.BlockSpec((tk, a_w), lambda s, k: (k, 0)), b, pl.BlockSpec(
            (None, tk, b_w) if b.ndim == 3 else (tk, b_w), b_map), (0, 0),
        [_sds(out_shape, BF16)], [pl.BlockSpec(out_block, out_map)], _store(BF16), nk=nk, acc_shape=(a_w, b_w),
        semantics=("parallel", "arbitrary"), a2a_parts=a2a_parts)
    return res if a2a_parts else res[0]


def _ffn_dw_down(name, hm, df):
    S = df.shape[0]
    tk = _tk(S)
    return _mm(
        name, (2, S // tk), hm, pl.BlockSpec((tk, FF_SHARD), lambda m, k: (k, m)),
        df, pl.BlockSpec((tk, D), lambda m, k: (k, 0)), (0, 0),
        [_sds((D_FF, D), BF16)], [pl.BlockSpec((FF_SHARD, D), lambda m, k: (m, 0))], _store(BF16),
        nk=S // tk, acc_shape=(FF_SHARD, D), semantics=("parallel", "arbitrary"))[0]


def _ffn_up_bwd(name, dgu, wgu4, tail, a2a_parts=()):
    S = dgu.shape[1]

    def prod_fn(a_ref, b_ref):
        acc = None
        for s in range(4):
            cols = slice((s % 2) * FF_SHARD, (s % 2 + 1) * FF_SHARD)
            term = _nt(a_ref[s // 2, :, cols], b_ref[s])
            acc = term if acc is None else acc + term
        return acc

    return _mm(
        name, (S // TB,), dgu, pl.BlockSpec((2, TB, D_FF), lambda i: (0, i, 0)),
        wgu4, pl.BlockSpec((4, D, FF_SHARD), lambda i: (0, 0, 0)), (1, 1),
        tail["out_shapes"], tail["out_specs"], tail["epilogue"], extras=tail["extras"],
        extra_specs=tail["extra_specs"], semantics=("arbitrary",), a2a_parts=a2a_parts, prod_fn=prod_fn)


def _ffn_dw_gu(name, u_in, dgu, a2a_parts=()):
    return _grad_w(name, u_in, D, dgu, FF_SHARD, lambda s, k: (s // 2, k, s % 2), 4,
                   (4, D, FF_SHARD), (None, D, FF_SHARD), lambda s, k: (s, 0, 0), a2a_parts=a2a_parts)


def _ffn_bwd(tag, df, u_in, gu, hm, wgu4, wd, tail):
    dgu = _ffn_down_bwd(tag + "_down_bwd", df, wd, gu)[0]
    dwd = _ffn_dw_down(tag + "_dw_down", hm, df)
    res = _ffn_up_bwd(tag + "_up_bwd", dgu, wgu4, tail)
    dwgu = _ffn_dw_gu(tag + "_dw_gu", u_in, dgu)
    return res, dwgu, dwd


def _shift_down(v, k, row):
    return jnp.where(row >= k, pltpu.roll(v, k, axis=0), 0.0)


def _shift_up(v, k, row, S):
    return jnp.where(row < S - k, pltpu.roll(v, S - k, axis=0), 0.0)


def _conv_specs(S):
    cols = CONV_W // LANES
    return [pl.BlockSpec((S, LANES), functools.partial(lambda j, off: (0, off + j), off=o * cols))
            for o in range(3)]


def _conv_fwd(proj, conv_w):
    S = proj.shape[0]

    def body(cb_ref, cc_ref, cx_ref, w_ref, sc_ref):
        row = lax.broadcasted_iota(jnp.int32, (S, LANES), 0)
        v = cc_ref[...].astype(F32) * cx_ref[...].astype(F32)
        yv = w_ref[0:1, :] * _shift_down(v, 2, row) + w_ref[1:2, :] * _shift_down(v, 1, row) + w_ref[2:3, :] * v
        sc_ref[...] = (cb_ref[...].astype(F32) * yv).astype(BF16)

    return pl.pallas_call(
        body, name="conv_fwd", grid=(CONV_W // LANES,),
        in_specs=_conv_specs(S) + [pl.BlockSpec((3, LANES), lambda j: (0, j))],
        out_specs=pl.BlockSpec((S, LANES), lambda j: (0, j)), out_shape=_sds((S, CONV_W), BF16),
        compiler_params=_params(("parallel",)),
    )(proj, proj, proj, conv_w)


def _conv_bwd(dsc, proj, conv_w):
    S = proj.shape[0]

    def body(d_ref, cb_ref, cc_ref, cx_ref, w_ref, dcb_ref, dcc_ref, dcx_ref, dw_ref):
        row = lax.broadcasted_iota(jnp.int32, (S, LANES), 0)
        cc = cc_ref[...].astype(F32)
        cx = cx_ref[...].astype(F32)
        d = d_ref[...].astype(F32)
        v = cc * cx
        v1 = _shift_down(v, 1, row)
        v2 = _shift_down(v, 2, row)
        w0, w1, w2 = w_ref[0:1, :], w_ref[1:2, :], w_ref[2:3, :]
        dcb_ref[...] = (d * (w0 * v2 + w1 * v1 + w2 * v)).astype(BF16)
        dy = d * cb_ref[...].astype(F32)
        dw_ref[0:1, :] = jnp.sum(dy * v2, axis=0, keepdims=True)
        dw_ref[1:2, :] = jnp.sum(dy * v1, axis=0, keepdims=True)
        dw_ref[2:3, :] = jnp.sum(dy * v, axis=0, keepdims=True)
        dv = w2 * dy + w1 * _shift_up(dy, 1, row, S) + w0 * _shift_up(dy, 2, row, S)
        dcc_ref[...] = (dv * cx).astype(BF16)
        dcx_ref[...] = (dv * cc).astype(BF16)

    col = pl.BlockSpec((S, LANES), lambda j: (0, j))
    return pl.pallas_call(
        body, name="conv_bwd", grid=(CONV_W // LANES,),
        in_specs=[col] + _conv_specs(S) + [pl.BlockSpec((3, LANES), lambda j: (0, j))],
        out_specs=[col, col, col, pl.BlockSpec((3, LANES), lambda j: (0, j))],
        out_shape=[_sds((S, CONV_W), BF16)] * 3 + [_sds((3, CONV_W), F32)],
        compiler_params=_params(("parallel",)),
    )(dsc, proj, proj, proj, conv_w)


Q_COL, K_COL, V_COL = 1536 // LANES, 2048 // LANES, 2560 // LANES


def _softplus(z):
    return jnp.maximum(z, 0.0) + jnp.log(1.0 + jnp.exp(-jnp.abs(z)))


def _nt(a, b):
    return lax.dot_general(a, b, (((1,), (1,)), ((), ())), preferred_element_type=F32)


def _tn(a, b):
    return lax.dot_general(a, b, (((0,), (0,)), ((), ())), preferred_element_type=F32)


def _interleave(gens, delays):
    results = [None] * len(gens)
    live = list(range(len(gens)))
    rnd = 0
    while live:
        for g in list(live):
            if rnd < delays[g]:
                continue
            try:
                next(gens[g])
            except StopIteration as stop:
                results[g] = stop.value
                live.remove(g)
        rnd += 1
    return results


def _attn_fwd(proj, shards):
    S = proj.shape[0]
    B = ATT_BLK
    nq = S // B
    n = len(shards)

    def body(q_ref, k_ref, v_ref, *rest):
        o_ref, t_ref = rest[n:n + 2]
        start, relay, finish = _gather_protocol(rest[:n], rest[n + 2:2 * n + 2], *rest[2 * n + 2:])
        p = pl.program_id(0)
        i = pl.program_id(1)
        pl.when((p == 0) & (i == 0))(start)
        pl.when((p == HEAD_PAIRS // 2) & (i == 0))(relay)
        lo_lane = lax.broadcasted_iota(jnp.int32, (B, LANES), 1) < HEAD_DIM
        row = lax.broadcasted_iota(jnp.int32, (B, B), 0)
        col = lax.broadcasted_iota(jnp.int32, (B, B), 1)
        after = (row > col).astype(BF16)
        causal = col < row
        q2 = q_ref[...] * 0.125
        zero = jnp.zeros((), BF16)
        q_heads = (jnp.where(lo_lane, q2, zero), jnp.where(lo_lane, zero, q2))

        def head_tile(q_h, st, kb, diag):
            k2 = k_ref[pl.ds(pl.multiple_of(kb * B, B), B), :]
            z = _nt(q_h, k2)
            yield
            spz = _softplus(z)
            sp = jnp.where(causal, spz, 0.0) if diag else spz
            hi = sp.astype(BF16)
            lo = (sp - hi.astype(F32)).astype(BF16)
            r = st["r"]
            st["r"] = r + jnp.sum(sp, axis=1, keepdims=True)
            yield
            rem = jnp.dot(hi, after, preferred_element_type=F32) + jnp.dot(lo, after, preferred_element_type=F32)
            yield
            a = jnp.exp(z - spz - (rem + r))
            if diag:
                a = jnp.where(causal, a, 0.0)
            ab = a.astype(BF16)
            yield
            v2 = v_ref[pl.ds(pl.multiple_of(kb * B, B), B), :]
            st["acc"] = st["acc"] + jnp.dot(ab, v2, preferred_element_type=F32)

        def tiles(kbs, carry, diags=(False, False)):
            sts = [dict(r=carry[0], acc=carry[1]), dict(r=carry[2], acc=carry[3])]
            gens = [head_tile(q_h, st, kb, dg) for kb, dg in zip(kbs, diags) for q_h, st in zip(q_heads, sts)]
            _interleave(gens, [t for t in range(len(kbs)) for _ in q_heads])
            return sts[0]["r"], sts[0]["acc"], sts[1]["r"], sts[1]["acc"]

        zr, za = jnp.zeros((B, 1), F32), jnp.zeros((B, LANES), F32)
        carry = lax.fori_loop(0, i % 2, lambda j, cr: tiles([i, i - 1], cr, (True, False)), (zr, za, zr, za))
        carry = lax.fori_loop(0, 1 - i % 2, lambda j, cr: tiles([i], cr, (True,)), carry)
        first = i - 1 - i % 2
        ra, acc_a, rb, acc_b = lax.fori_loop(
            0, i // 2, lambda j, cr: tiles([first - 2 * j, first - 2 * j - 1], cr), carry)
        o_ref[...] = jnp.where(lo_lane, acc_a, acc_b).astype(BF16)
        t_ref[...] = jnp.where(lo_lane, ra, rb).T
        pl.when((p == HEAD_PAIRS - 1) & (i == nq - 1))(finish)

    seq = lambda off: pl.BlockSpec((S, LANES), lambda p, i: (0, off + p))
    blk = pl.BlockSpec((B, LANES), lambda p, i: (i, p))
    o, t, *gathered = pl.pallas_call(
        body, name="attn_fwd", grid=(HEAD_PAIRS, nq),
        in_specs=[pl.BlockSpec((B, LANES), lambda p, i: (i, Q_COL + p)), seq(K_COL), seq(V_COL)] + _hbm_specs(n),
        out_specs=[blk, pl.BlockSpec((LANES, B), lambda p, i: (p, i))] + _hbm_specs(n),
        out_shape=[_sds((S, 512), BF16), _sds((512, S), F32)] + _gather_shapes(shards),
        scratch_shapes=_gather_sems(n),
        compiler_params=_params(("arbitrary", "arbitrary")),
    )(proj, proj, proj, *shards)
    return o, t, _with_own_shard(gathered, shards)


def _attn_bwd(proj, t, do, parts):
    S = proj.shape[0]
    kt = proj[:, K_COL * LANES:V_COL * LANES].T
    B = ATT_BLK
    nq = S // B
    n = len(parts)

    def body(q_ref, k_ref, v_ref, kt_ref, t_ref, do_ref, *rest):
        dq_ref, dk_ref, dv_ref = rest[n:n + 3]
        dk_acc, dv_acc = rest[2 * n + 3:2 * n + 5]
        start, finish = _reduce8_protocol(rest[:n], rest[n + 3:2 * n + 3], *rest[2 * n + 5:])
        i = pl.program_id(1)
        pl.when((pl.program_id(0) == 0) & (i == 0))(start)

        @pl.when(i == 0)
        def _():
            dk_acc[...] = jnp.zeros_like(dk_acc)
            dv_acc[...] = jnp.zeros_like(dv_acc)

        lo_lane = lax.broadcasted_iota(jnp.int32, (B, LANES), 1) < HEAD_DIM
        key = lax.broadcasted_iota(jnp.int32, (B, B), 0)
        qry = lax.broadcasted_iota(jnp.int32, (B, B), 1)
        upto = (qry <= key).astype(BF16)
        before = (qry < key).astype(BF16)
        causal = key < qry
        zero = jnp.zeros((), BF16)
        q2 = q_ref[...] * 0.125
        do2 = do_ref[...]
        heads = ((jnp.where(lo_lane, q2, zero), jnp.where(lo_lane, do2, zero), t_ref[0:1, :]),
                 (jnp.where(lo_lane, zero, q2), jnp.where(lo_lane, zero, do2), t_ref[HEAD_DIM:HEAD_DIM + 1, :]))

        def head_tile(head, st, kb, diag):
            q_h, do_h, t_h = head
            rows = pl.ds(pl.multiple_of(kb * B, B), B)
            z = _nt(k_ref[rows, :], q_h)
            da = _nt(v_ref[rows, :], do_h)
            yield
            spz = _softplus(z)
            sp = jnp.where(causal, spz, 0.0) if diag else spz
            hi = sp.astype(BF16)
            lo = (sp - hi.astype(F32)).astype(BF16)
            pc = st["pc"]
            st["pc"] = pc + jnp.sum(sp, axis=0, keepdims=True)
            yield
            pref = jnp.dot(upto, hi, preferred_element_type=F32) + jnp.dot(upto, lo, preferred_element_type=F32)
            yield
            a = jnp.exp(z - spz - ((t_h - pc) - pref))
            if diag:
                a = jnp.where(causal, a, 0.0)
            e = a * da
            eb = e.astype(BF16)
            ab = a.astype(BF16)
            ec = st["ec"]
            st["ec"] = ec + jnp.sum(e, axis=0, keepdims=True)
            yield
            e_before = ec + jnp.dot(before, eb, preferred_element_type=F32)
            yield
            u = jnp.exp(-spz)
            dz = u * (e + e_before) - e_before
            if diag:
                dz = jnp.where(causal, dz, 0.0)
            dzb = dz.astype(BF16)
            yield
            st["dqt"] = st["dqt"] + jnp.dot(kt_ref[:, rows], dzb, preferred_element_type=F32)
            return (jnp.dot(dzb, q_h, preferred_element_type=F32), jnp.dot(ab, do_h, preferred_element_type=F32))

        def tiles(kbs, carry, diags=(False, False)):
            sts = [dict(pc=carry[3 * h], ec=carry[3 * h + 1], dqt=carry[3 * h + 2]) for h in range(2)]
            gens = [head_tile(hd, st, kb, dg) for kb, dg in zip(kbs, diags) for hd, st in zip(heads, sts)]
            res = _interleave(gens, [t for t in range(len(kbs)) for _ in heads])
            for t, kb in enumerate(kbs):
                rows = pl.ds(pl.multiple_of(kb * B, B), B)
                (dk_a, dv_a), (dk_b, dv_b) = res[2 * t], res[2 * t + 1]
                dk_acc[rows, :] += dk_a + dk_b
                dv_acc[rows, :] += dv_a + dv_b
            return tuple(st[nm] for st in sts for nm in ("pc", "ec", "dqt"))

        zc, zq = jnp.zeros((1, B), F32), jnp.zeros((LANES, B), F32)
        carry = lax.fori_loop(0, i // 2, lambda j, cr: tiles([2 * j, 2 * j + 1], cr), (zc, zc, zq, zc, zc, zq))
        carry = lax.fori_loop(0, i % 2, lambda j, cr: tiles([i - 1, i], cr, (False, True)), carry)
        _, _, dqt_a, _, _, dqt_b = lax.fori_loop(0, 1 - i % 2, lambda j, cr: tiles([i], cr, (True,)), carry)
        head0 = lax.broadcasted_iota(jnp.int32, (LANES, B), 0) < HEAD_DIM
        dq_ref[...] = (jnp.where(head0, dqt_a, dqt_b).T * 0.125).astype(BF16)

        @pl.when(i == nq - 1)
        def _():
            dk_ref[...] = dk_acc[...].astype(BF16)
            dv_ref[...] = dv_acc[...].astype(BF16)

        pl.when((pl.program_id(0) == HEAD_PAIRS - 1) & (i == nq - 1))(finish)

    seq = lambda off: pl.BlockSpec((S, LANES), lambda p, i: (0, off + p))
    blk = pl.BlockSpec((B, LANES), lambda p, i: (i, p))
    whole = pl.BlockSpec((S, LANES), lambda p, i: (0, p))
    dq, dk, dv, *came = pl.pallas_call(
        body, name="attn_bwd", grid=(HEAD_PAIRS, nq),
        in_specs=[pl.BlockSpec((B, LANES), lambda p, i: (i, Q_COL + p)), seq(K_COL), seq(V_COL),
                  pl.BlockSpec((LANES, S), lambda p, i: (p, 0)), pl.BlockSpec((LANES, B), lambda p, i: (p, i)), blk]
        + _hbm_specs(n),
        out_specs=[blk, whole, whole] + _hbm_specs(n),
        out_shape=[_sds((S, 512), BF16)] * 3 + [jax.ShapeDtypeStruct(p.shape, p.dtype) for p in parts],
        scratch_shapes=[pltpu.VMEM((S, LANES), F32), pltpu.VMEM((S, LANES), F32)] + _reduce8_sems(n),
        compiler_params=_params(("arbitrary", "arbitrary")),
    )(proj, proj, proj, kt, t, do, *parts)
    return dq, dk, dv, came


GA_COL, GB_COL = 3072 // D, 4096 // D


def _merge_fwd(sc, o, wco4, wao4, proj, bm):
    S = sc.shape[0]

    def body(sc_ref, o_ref, wc_ref, wa_ref, ga_ref, gb_ref, bm_ref, ya_ref, yb_ref, mg_ref):
        scv, ov = sc_ref[...], o_ref[...]
        for s in range(4):
            cols = slice(s * 256, (s + 1) * 256)
            ya = jnp.dot(scv, wc_ref[s], preferred_element_type=F32)
            yb = jnp.dot(ov, wa_ref[s], preferred_element_type=F32)
            sa = _sigmoid(ga_ref[:, cols].astype(F32) + bm_ref[0:1, cols])
            sb = _sigmoid(gb_ref[:, cols].astype(F32) + bm_ref[1:2, cols])
            ya_ref[:, cols] = ya.astype(BF16)
            yb_ref[:, cols] = yb.astype(BF16)
            mg_ref[:, cols] = (sa * ya + sb * yb).astype(BF16)

    wide = pl.BlockSpec((TM, 512), lambda i: (i, 0))
    wsp = pl.BlockSpec((4, 512, 256), lambda i: (0, 0, 0))
    out = pl.BlockSpec((TM, D), lambda i: (i, 0))
    return pl.pallas_call(
        body, name="merge_fwd", grid=(S // TM,),
        in_specs=[wide, wide, wsp, wsp, pl.BlockSpec((TM, D), lambda i: (i, GA_COL)),
                  pl.BlockSpec((TM, D), lambda i: (i, GB_COL)), pl.BlockSpec((2, D), lambda i: (0, 0))],
        out_specs=[out, out, out], out_shape=[_sds((S, D), BF16)] * 3,
        compiler_params=_params(("parallel",)),
    )(sc, o, wco4, wao4, proj, proj, bm)


def _merge_bwd(dy2, wout, ya, yb, proj, bm):
    S = dy2.shape[0]

    def epilogue(acc, ex, outs):
        ya_ref, yb_ref, ga_ref, gb_ref, bm_ref = ex

        @pl.when(pl.program_id(0) == 0)
        def _():
            outs[4][...] = jnp.zeros_like(outs[4])

        for q in range(4):
            cols = slice(q * 256, (q + 1) * 256)
            dm = acc[:, cols]
            sa = _sigmoid(ga_ref[:, cols].astype(F32) + bm_ref[0:1, cols])
            sb = _sigmoid(gb_ref[:, cols].astype(F32) + bm_ref[1:2, cols])
            dga = dm * ya_ref[:, cols].astype(F32) * (sa * (1.0 - sa))
            dgb = dm * yb_ref[:, cols].astype(F32) * (sb * (1.0 - sb))
            outs[0][:, cols] = (dm * sa).astype(BF16)
            outs[1][:, cols] = (dm * sb).astype(BF16)
            outs[2][:, cols] = dga.astype(BF16)
            outs[3][:, cols] = dgb.astype(BF16)
            outs[4][0:1, cols] += jnp.sum(dga, axis=0, keepdims=True)
            outs[4][1:2, cols] += jnp.sum(dgb, axis=0, keepdims=True)

    tm = TM
    out = pl.BlockSpec((tm, D), lambda i: (i, 0))
    return _mm(
        "merge_bwd", (S // tm,), dy2, out, wout, pl.BlockSpec((D, D), lambda i: (0, 0)), (1, 1),
        [_sds((S, D), BF16)] * 4 + [_sds((8, D), F32)], [out, out, out, out, _const_spec((8, D))],
        epilogue, extras=(ya, yb, proj, proj, bm),
        extra_specs=(out, out, pl.BlockSpec((tm, D), lambda i: (i, GA_COL)),
                     pl.BlockSpec((tm, D), lambda i: (i, GB_COL)), _const_spec((2, D))),
        semantics=("arbitrary",))


def _mixer_out_bwd(dya, dyb, wco4, wao4):
    S = dya.shape[0]

    def body(da_ref, db_ref, wc_ref, wa_ref, dsc_ref, do_ref):
        for d_ref, w_ref, o_ref in ((da_ref, wc_ref, dsc_ref), (db_ref, wa_ref, do_ref)):
            acc = _nt(d_ref[:, 0:256], w_ref[0])
            for s in range(1, 4):
                acc = acc + _nt(d_ref[:, s * 256:(s + 1) * 256], w_ref[s])
            o_ref[...] = acc.astype(BF16)

    wide = pl.BlockSpec((TM, D), lambda i: (i, 0))
    wsp = pl.BlockSpec((4, 512, 256), lambda i: (0, 0, 0))
    out = pl.BlockSpec((TM, 512), lambda i: (i, 0))
    return pl.pallas_call(
        body, name="mixer_out_bwd", grid=(S // TM,), in_specs=[wide, wide, wsp, wsp], out_specs=[out, out],
        out_shape=[_sds((S, 512), BF16)] * 2, compiler_params=_params(("parallel",)),
    )(dya, dyb, wco4, wao4)


def _mixer_out_dw(sc, o, dya, dyb):
    S = sc.shape[0]
    tk = _tk(S)
    nk = S // tk

    def body(sc_ref, o_ref, da_ref, db_ref, dwc_ref, dwa_ref, acc_c, acc_a):
        k = pl.program_id(0)

        @pl.when(k == 0)
        def _():
            acc_c[...] = jnp.zeros_like(acc_c)
            acc_a[...] = jnp.zeros_like(acc_a)

        acc_c[...] += _tn(sc_ref[...], da_ref[...])
        acc_a[...] += _tn(o_ref[...], db_ref[...])

        @pl.when(k == nk - 1)
        def _():
            for s in range(4):
                dwc_ref[s] = acc_c[:, s * 256:(s + 1) * 256].astype(BF16)
                dwa_ref[s] = acc_a[:, s * 256:(s + 1) * 256].astype(BF16)

    narrow = pl.BlockSpec((tk, 512), lambda k: (k, 0))
    wide = pl.BlockSpec((tk, D), lambda k: (k, 0))
    out = pl.BlockSpec((4, 512, 256), lambda k: (0, 0, 0))
    return pl.pallas_call(
        body, name="mixer_out_dw", grid=(nk,), in_specs=[narrow, narrow, wide, wide], out_specs=[out, out],
        out_shape=[_sds((4, 512, 256), BF16)] * 2,
        scratch_shapes=[pltpu.VMEM((512, D), F32), pltpu.VMEM((512, D), F32)],
        compiler_params=_params(("arbitrary",)),
    )(sc, o, dya, dyb)


ADA_SHARD = 2304
ADA_TN = 768


def _ada_fwd(c_all, w_ada_l, b_l):
    def body(c_ref, w_ref, b_ref, o_ref):
        cv = c_ref[...]
        ca = cv * _sigmoid(cv)
        o_ref[...] = jnp.dot(ca.astype(BF16), w_ref[...].astype(BF16), preferred_element_type=F32) + b_ref[...]

    return pl.pallas_call(
        body, name="ada_fwd", grid=(ADA_SHARD // ADA_TN,),
        in_specs=[pl.BlockSpec((8, D), lambda j: (0, 0)), pl.BlockSpec((D, ADA_TN), lambda j: (0, j)),
                  pl.BlockSpec((1, ADA_TN), lambda j: (0, j))],
        out_specs=pl.BlockSpec((8, ADA_TN), lambda j: (0, j)), out_shape=_sds((8, ADA_SHARD), F32),
        compiler_params=_params(("parallel",)),
    )(c_all, w_ada_l, b_l)


def _ada_bwd(c_all_t, dmod_l):
    def body(c_ref, d_ref, o_ref):
        cv = c_ref[...]
        ca = cv * _sigmoid(cv)
        o_ref[...] = jnp.dot(ca.astype(BF16).astype(F32), d_ref[...].astype(BF16).astype(F32),
                             preferred_element_type=F32, precision=lax.Precision.HIGHEST)

    return pl.pallas_call(
        body, name="ada_bwd", grid=(ADA_SHARD // ADA_TN,),
        in_specs=[pl.BlockSpec((D, 8), lambda j: (0, 0)), pl.BlockSpec((8, ADA_TN), lambda j: (0, j))],
        out_specs=pl.BlockSpec((D, ADA_TN), lambda j: (0, j)), out_shape=_sds((D, ADA_SHARD), F32),
        compiler_params=_params(("parallel",)),
    )(c_all_t, dmod_l)


def _sum_rows(name, x):
    n = x.shape[1]

    def body(x_ref, o_ref):
        s = x_ref[0:1, :]
        for d in range(1, 8):
            s = s + x_ref[d:d + 1, :]
        o_ref[...] = s

    return pl.pallas_call(
        body, name=name, in_specs=[pl.BlockSpec(memory_space=pltpu.VMEM)],
        out_specs=pl.BlockSpec(memory_space=pltpu.VMEM), out_shape=_sds((1, n), F32),
        compiler_params=pltpu.CompilerParams(vmem_limit_bytes=VMEM_LIMIT),
    )(x)


def _pair_sum(name, g4, recv, c_idx):
    _, _, h, C = g4.shape
    tr = h if h <= 512 else h // (h // 256) if h % 256 == 0 else h // 2

    def body(c_ref, g_ref, r_ref, o_ref):
        o_ref[...] = (g_ref[...].astype(F32) + r_ref[...].astype(F32)).astype(BF16)

    grid_spec = pltpu.PrefetchScalarGridSpec(
        num_scalar_prefetch=1, grid=(4, h // tr),
        in_specs=[pl.BlockSpec((None, None, tr, C), lambda k, i, c: (k, c[0], i, 0)),
                  pl.BlockSpec((None, tr, C), lambda k, i, c: (k, i, 0))],
        out_specs=pl.BlockSpec((None, tr, C), lambda k, i, c: (k, i, 0)))
    return pl.pallas_call(
        body, name=name, grid_spec=grid_spec, out_shape=_sds((4, h, C), BF16),
        compiler_params=_params(("parallel", "parallel")),
    )(c_idx, g4, recv)


def _sum4(name, q, p, chip_idx):
    _, h, C = q.shape
    tr = h if h <= 512 else h // (h // 256) if h % 256 == 0 else h // 2

    def body(k_ref, q_ref, p_ref, o_ref):
        me = k_ref[0]
        terms = [jnp.where(me == k, p_ref[...], q_ref[k]).astype(F32) for k in range(4)]
        o_ref[...] = ((terms[0] + terms[1]) + terms[2]) + terms[3]

    grid_spec = pltpu.PrefetchScalarGridSpec(
        num_scalar_prefetch=1, grid=(h // tr,),
        in_specs=[pl.BlockSpec((4, tr, C), lambda i, k: (0, i, 0)),
                  pl.BlockSpec((None, tr, C), lambda i, k: (k[0], i, 0))],
        out_specs=pl.BlockSpec((tr, C), lambda i, k: (i, 0)))
    return pl.pallas_call(
        body, name=name, grid_spec=grid_spec, out_shape=_sds((h, C), F32), compiler_params=_params(("parallel",)),
    )(chip_idx, q, p)


def _sum8(name, q, p, idx):
    _, _, h, C = q.shape
    tr = h if h <= 256 else 256 if h % 256 == 0 else h // 2

    def body(k_ref, q_ref, p_ref, o_ref):
        total = None
        for j in range(4):
            for cc in range(2):
                mine = (k_ref[0] == j) & (k_ref[1] == cc)
                term = jnp.where(mine, p_ref[...], q_ref[j, cc]).astype(F32)
                total = term if total is None else total + term
        o_ref[...] = total

    grid_spec = pltpu.PrefetchScalarGridSpec(
        num_scalar_prefetch=1, grid=(h // tr,),
        in_specs=[pl.BlockSpec((4, 2, tr, C), lambda i, k: (0, 0, i, 0)),
                  pl.BlockSpec((None, None, tr, C), lambda i, k: (k[0], k[1], i, 0))],
        out_specs=pl.BlockSpec((tr, C), lambda i, k: (i, 0)))
    return pl.pallas_call(
        body, name=name, grid_spec=grid_spec, out_shape=_sds((h, C), F32), compiler_params=_params(("parallel",)),
    )(idx, q, p)


def _adamw(name, w, g, m, v):
    R, C = w.shape
    tr = R
    while tr * C * 4 > (1 << 20) and tr % 16 == 0:
        tr //= 2
    c1 = 1.0 - ADAM_B1 ** ADAM_STEP
    c2 = 1.0 - ADAM_B2 ** ADAM_STEP

    def body(w_ref, g_ref, m_ref, v_ref, d_ref, nm_ref, nv_ref):
        gv = g_ref[...]
        nm = ADAM_B1 * m_ref[...] + (1.0 - ADAM_B1) * gv
        nv = ADAM_B2 * v_ref[...] + (1.0 - ADAM_B2) * (gv * gv)
        nm_ref[...] = nm
        nv_ref[...] = nv
        d_ref[...] = -ADAM_LR * ((nm * (1.0 / c1)) / (jnp.sqrt(nv * (1.0 / c2)) + ADAM_EPS) + ADAM_WD * w_ref[...])

    spec = pl.BlockSpec((tr, C), lambda i: (i, 0))
    return pl.pallas_call(
        body, name=name, grid=(R // tr,), in_specs=[spec] * 4, out_specs=[spec] * 3,
        out_shape=[_sds((R, C), F32)] * 3, compiler_params=_params(("parallel",)),
    )(w, g, m, v)


def _adamw_halves(name, w, own, sib, m, v, c_idx):
    R, C = w.shape
    h = R // 2
    tr = h
    while tr * C * 4 > (1 << 20) and tr % 16 == 0:
        tr //= 2
    nb = h // tr
    c1 = 1.0 - ADAM_B1 ** ADAM_STEP
    c2 = 1.0 - ADAM_B2 ** ADAM_STEP

    def body(c_ref, w_ref, own_ref, sib_ref, m_ref, v_ref, g_ref, d_ref, nm_ref, nv_ref):
        mine = (pl.program_id(0) // nb) == c_ref[0]
        gv = jnp.where(mine, own_ref[...], sib_ref[...])
        nm = ADAM_B1 * m_ref[...] + (1.0 - ADAM_B1) * gv
        nv = ADAM_B2 * v_ref[...] + (1.0 - ADAM_B2) * (gv * gv)
        g_ref[...] = gv
        nm_ref[...] = nm
        nv_ref[...] = nv
        d_ref[...] = -ADAM_LR * ((nm * (1.0 / c1)) / (jnp.sqrt(nv * (1.0 / c2)) + ADAM_EPS) + ADAM_WD * w_ref[...])

    spec = pl.BlockSpec((tr, C), lambda i, c: (i, 0))
    half = pl.BlockSpec((tr, C), lambda i, c: (i % nb, 0))
    grid_spec = pltpu.PrefetchScalarGridSpec(
        num_scalar_prefetch=1, grid=(R // tr,), in_specs=[spec, half, half, spec, spec], out_specs=[spec] * 4)
    return pl.pallas_call(
        body, name=name, grid_spec=grid_spec, out_shape=[_sds((R, C), F32)] * 4,
        compiler_params=_params(("parallel",)),
    )(c_idx, w, own, sib, m, v)


def _pack(g, scale, shift, gate):
    rows = jnp.stack([g, scale, shift, gate]).astype(F32)
    return jnp.concatenate([rows, jnp.zeros((4, D), F32)], axis=0)


def _fold8(vec):
    m = -(-vec.shape[0] // (8 * LANES)) * LANES
    return jnp.concatenate([vec, jnp.zeros((8 * m - vec.shape[0],), vec.dtype)]).reshape(8, m)


def _allgather_vectors(name, vec):
    return _allgather_rows(name, _fold8(vec)).reshape(8, -1)


def kernel(x, c, w_ada, b_ada, norm1_g, ffn1_w_gu, ffn1_w_down, norm2_g, w_mix_in, b_merge, conv_w, w_conv_out, w_attn_out, w_out, norm3_g, ffn2_w_gu, ffn2_w_down, final_g, loss_target, m_w_ada, m_b_ada, m_norm1_g, m_ffn1_w_gu, m_ffn1_w_down, m_norm2_g, m_w_mix_in, m_b_merge, m_conv_w, m_w_conv_out, m_w_attn_out, m_w_out, m_norm3_g, m_ffn2_w_gu, m_ffn2_w_down, m_final_g, v_w_ada, v_b_ada, v_norm1_g, v_ffn1_w_gu, v_ffn1_w_down, v_norm2_g, v_w_mix_in, v_b_merge, v_conv_w, v_w_conv_out, v_w_attn_out, v_w_out, v_norm3_g, v_ffn2_w_gu, v_ffn2_w_down, v_final_g):
    xi, yi, ci = lax.axis_index("x"), lax.axis_index("y"), lax.axis_index("c")
    chip = 2 * xi + yi
    dev = 4 * xi + 2 * yi + ci
    S = x.shape[1]
    h0 = x[0]
    target = loss_target[0]

    (wgu1,) = _allgather_weights([ffn1_w_gu[0].astype(BF16)])
    late_shards = [w[0].astype(BF16) for w in (w_conv_out, w_attn_out, w_out, ffn2_w_gu, ffn2_w_down)]
    c_idx = jnp.reshape(ci, (1,)).astype(jnp.int32)
    chip_idx = jnp.reshape(chip, (1,)).astype(jnp.int32)

    def reduce_pairs(tag, names, grads):
        g4 = [g.reshape(4, 2, g.shape[1] // 2, g.shape[2]) for g in grads]
        recv = _sibling_swap_halves("grad_sibling_swap_" + tag, g4)
        return [_pair_sum("pair_sum_" + nm, a, b, c_idx) for nm, a, b in zip(names, g4, recv)]

    small = jnp.concatenate([c[0], b_merge[0].reshape(-1), conv_w[0].reshape(-1)])
    gathered = _allgather_vectors("allgather_small", small)
    c_all = gathered[:, :D]
    per_chip = gathered[0::2]
    bm_full = jnp.concatenate([per_chip[k, D:D + 512].reshape(2, 256) for k in range(4)], axis=1)
    cw_full = jnp.concatenate([per_chip[k, D + 512:D + 896].reshape(3, 128) for k in range(4)], axis=1)
    b_l = lax.dynamic_slice_in_dim(b_ada, chip * ADA_SHARD, ADA_SHARD, axis=1)
    mod_l = _ada_fwd(c_all, w_ada[0], b_l)
    mod_g = _allgather_rows("allgather_mod", mod_l).reshape(8, 8, ADA_SHARD)
    mod_all = jnp.concatenate([mod_g[2 * k] for k in range(4)], axis=1)
    mod = lax.dynamic_slice_in_dim(mod_all, dev, 1, axis=0).reshape(3, 3, D)
    p1 = _pack(norm1_g[0], mod[0, 1], mod[0, 0], mod[0, 2])
    p2 = _pack(norm2_g[0], mod[1, 1], mod[1, 0], mod[1, 2])
    p3 = _pack(norm3_g[0], mod[2, 1], mod[2, 0], mod[2, 2])
    pf = _pack(final_g, final_g, final_g, final_g)

    u1 = _norm_mod_fwd("norm1_fwd", h0, p1)
    gu1, hm1, (wd1,) = _ffn_up("ffn1_up", u1, wgu1, [ffn1_w_down[0].astype(BF16)])
    f1, h1, u2, (wmix,) = _proj_residual("ffn1_down", hm1, wd1.reshape(D_FF, D), h0, p1, 0.5, p2,
                                         gather_shards=[w_mix_in[0].astype(BF16)])
    wd1 = wd1.reshape(D_FF, D)

    def mix_in_prod(a_ref, b_ref):
        av = a_ref[...]
        return [jnp.dot(av, b_ref[s], preferred_element_type=F32) for s in range(4)]

    def mix_in_store(accs, ex, outs):
        for s, acc in enumerate(accs):
            outs[0][:, s * MIX_SHARD:(s + 1) * MIX_SHARD] = acc.astype(BF16)

    proj = _mm("mix_in", (S // TM,), u2, pl.BlockSpec((TM, D), lambda i: (i, 0)),
               wmix, pl.BlockSpec((4, D, MIX_SHARD), lambda i: (0, 0, 0)), (1, 0),
               [_sds((S, MIX_W), BF16)], [pl.BlockSpec((TM, MIX_W), lambda i: (i, 0))], mix_in_store,
               semantics=("parallel",), prod_fn=mix_in_prod)[0]
    sc = _conv_fwd(proj, cw_full)
    o, t_tot, (wco, wao, wout, wgu2, wd2) = _attn_fwd(proj, late_shards)
    wout = wout.reshape(D, D)
    wd2 = wd2.reshape(D_FF, D)
    ya, yb, merged = _merge_fwd(sc, o, wco, wao, proj, bm_full)
    y2, h2, u3, _ = _proj_residual("mix_out", merged, wout, h1, p2, 1.0, p3)
    gu3, hm3, _ = _ffn_up("ffn2_up", u3, wgu2)

    dh3, df3, sums_f, loss_blk = _proj_residual_loss("ffn2_down", hm3, wd2, h2, p3, 0.5, pf, target)
    (dh2, dy2, sums3), dwgu2, dwd2 = _ffn_bwd("ffn2", df3, u3, gu3, hm3, wgu2, wd2,
                                              _norm_bwd_tail(h2, p3, dh3, prev=(p2, y2, 1.0)))

    dya, dyb, dga, dgb, sums_bm = _merge_bwd(dy2, wout, ya, yb, proj, bm_full)
    tk = _tk(S)
    dwout = _mm("dw_out", (1, S // tk), merged, pl.BlockSpec((tk, D), lambda n, k: (k, 0)),
                dy2, pl.BlockSpec((tk, D), lambda n, k: (k, 0)), (0, 0),
                [_sds((D, D), BF16)], [pl.BlockSpec((D, D), lambda n, k: (0, 0))], _store(BF16),
                nk=S // tk, acc_shape=(D, D))[0]
    dsc, do = _mixer_out_bwd(dya, dyb, wco, wao)
    dwco, dwao = _mixer_out_dw(sc, o, dya, dyb)
    dcb, dcc, dcx, dcw = _conv_bwd(dsc, proj, cw_full)
    names_e = ["ffn2_w_gu", "ffn2_w_down", "w_out", "w_conv_out", "w_attn_out"]
    part_e = [g.reshape(4, 2, g.shape[1] // 2, g.shape[2])
              for g in (dwgu2, dwd2.reshape(4, 704, D), dwout.reshape(4, 256, D), dwco, dwao)]
    dq, dk, dv, came_e = _attn_bwd(proj, t_tot, do, part_e)
    dproj = jnp.concatenate([dcb, dcc, dcx, dq, dk, dv, dga, dgb], axis=1)
    tail2 = _norm_bwd_tail(h1, p2, dh2, prev=(p1, f1, 0.5))

    def mix_prod(a_ref, b_ref):
        acc = _nt(a_ref[:, 0:MIX_SHARD], b_ref[0])
        for s in range(1, 4):
            acc = acc + _nt(a_ref[:, s * MIX_SHARD:(s + 1) * MIX_SHARD], b_ref[s])
        return acc

    dh1, df1, sums2 = _mm("mix_in_bwd", (S // TB,), dproj, pl.BlockSpec((TB, MIX_W), lambda i: (i, 0)),
                          wmix, pl.BlockSpec((4, D, MIX_SHARD), lambda i: (0, 0, 0)), (1, 1),
                          tail2["out_shapes"], tail2["out_specs"], tail2["epilogue"], extras=tail2["extras"],
                          extra_specs=tail2["extra_specs"], semantics=("arbitrary",), prod_fn=mix_prod)
    dwmix = _grad_w("dw_mix_in", u2, D, dproj, MIX_SHARD, lambda s, k: (k, s), 4, (4, D, MIX_SHARD),
                    (None, D, MIX_SHARD), lambda s, k: (s, 0, 0))

    part_mix = reduce_pairs("mix", ["w_mix_in"], [dwmix])
    dgu1, *came_mix = _ffn_down_bwd("ffn1_down_bwd", df1, wd1, gu1, a2a_parts=part_mix)
    dwd1 = _ffn_dw_down("ffn1_dw_down", hm1, df1)
    part_wd1 = reduce_pairs("wd1", ["ffn1_w_down"], [dwd1.reshape(4, 704, D)])
    dwgu1, *came_wd1 = _ffn_dw_gu("ffn1_dw_gu", u1, dgu1, a2a_parts=part_wd1)
    part_gu1 = reduce_pairs("gu1", ["ffn1_w_gu"], [dwgu1])
    grad_x, sums1, *came_gu1 = _ffn_up_bwd("ffn1_up_bwd", dgu1, wgu1, _norm_bwd_tail(h0, p1, dh1),
                                           a2a_parts=part_gu1)

    dmod = jnp.stack([sums1[0], sums1[1], sums2[3], sums2[0], sums2[1], sums3[3], sums3[0], sums3[1], sums_f[1]])
    small_g = jnp.concatenate([dmod.reshape(-1), sums1[2], sums2[2], sums3[2], sums_f[0],
                               sums_bm[0], sums_bm[1], dcw.reshape(-1), loss_blk[0, 0:1]])
    all_g = _allgather_vectors("allgather_small_grads", small_g)
    tot = _sum_rows("sum_small_grads", all_g)[0]
    loss = tot[16 * D + 512]
    g_b_ada = tot[:9 * D][None, :]
    g_n1, g_n2, g_n3 = (tot[(9 + k) * D:(10 + k) * D][None, :] for k in range(3))
    g_fin = tot[12 * D:13 * D]
    g_bm = lax.dynamic_slice_in_dim(tot[13 * D:15 * D].reshape(2, D), chip * 256, 256, axis=1)[None]
    g_cw = lax.dynamic_slice_in_dim(tot[15 * D:16 * D + 512].reshape(3, 512), chip * 128, 128, axis=1)[None]
    dmod_l = lax.dynamic_slice_in_dim(all_g[:, :9 * D], chip * ADA_SHARD, ADA_SHARD, axis=1)
    g_w_ada = _ada_bwd(c_all.T, dmod_l)[None]

    names_l = ["w_mix_in", "ffn1_w_down", "ffn1_w_gu"]
    names = names_e + names_l
    place_idx = jnp.stack([chip, ci]).astype(jnp.int32)
    half = [_sum8("device_sum_" + nm, q, p, place_idx) for nm, q, p in zip(names_e, came_e, part_e)]
    half += [_sum4("chip_sum_" + nm, q, p, chip_idx)
             for nm, q, p in zip(names_l, came_mix + came_wd1 + came_gu1, part_mix + part_wd1 + part_gu1)]
    g_own = dict(zip(names, half))
    g_sib = dict(zip(names, _sibling_share(half)))

    weights = dict(w_ada=w_ada, b_ada=b_ada, norm1_g=norm1_g, ffn1_w_gu=ffn1_w_gu, ffn1_w_down=ffn1_w_down,
                   norm2_g=norm2_g, w_mix_in=w_mix_in, b_merge=b_merge, conv_w=conv_w, w_conv_out=w_conv_out,
                   w_attn_out=w_attn_out, w_out=w_out, norm3_g=norm3_g, ffn2_w_gu=ffn2_w_gu,
                   ffn2_w_down=ffn2_w_down, final_g=final_g)
    ms = dict(w_ada=m_w_ada, b_ada=m_b_ada, norm1_g=m_norm1_g, ffn1_w_gu=m_ffn1_w_gu, ffn1_w_down=m_ffn1_w_down,
              norm2_g=m_norm2_g, w_mix_in=m_w_mix_in, b_merge=m_b_merge, conv_w=m_conv_w, w_conv_out=m_w_conv_out,
              w_attn_out=m_w_attn_out, w_out=m_w_out, norm3_g=m_norm3_g, ffn2_w_gu=m_ffn2_w_gu,
              ffn2_w_down=m_ffn2_w_down, final_g=m_final_g)
    vs = dict(w_ada=v_w_ada, b_ada=v_b_ada, norm1_g=v_norm1_g, ffn1_w_gu=v_ffn1_w_gu, ffn1_w_down=v_ffn1_w_down,
              norm2_g=v_norm2_g, w_mix_in=v_w_mix_in, b_merge=v_b_merge, conv_w=v_conv_w, w_conv_out=v_w_conv_out,
              w_attn_out=v_w_attn_out, w_out=v_w_out, norm3_g=v_norm3_g, ffn2_w_gu=v_ffn2_w_gu,
              ffn2_w_down=v_ffn2_w_down, final_g=v_final_g)
    order = list(weights)
    grad = dict(w_ada=g_w_ada, b_ada=g_b_ada, norm1_g=g_n1, norm2_g=g_n2, norm3_g=g_n3, final_g=g_fin,
                b_merge=g_bm, conv_w=g_cw)
    delta, new_m, new_v = {}, {}, {}
    small_names = ["b_ada", "norm1_g", "norm2_g", "norm3_g", "final_g", "b_merge", "conv_w"]
    flat = lambda d: jnp.concatenate([d[nm].reshape(-1) for nm in small_names])[None, :]
    sd, sm, sv = _adamw("adamw_small", flat(weights), flat(grad), flat(ms), flat(vs))
    off = 0
    for nm in small_names:
        size = weights[nm].size
        for dst, src in ((delta, sd), (new_m, sm), (new_v, sv)):
            dst[nm] = src[0, off:off + size].reshape(weights[nm].shape)
        off += size
    for nm in order:
        if nm in small_names:
            continue
        shp = weights[nm].shape
        if nm in g_own:
            g2, d2, m2, v2 = _adamw_halves("adamw_" + nm, weights[nm][0], g_own[nm], g_sib[nm], ms[nm][0], vs[nm][0],
                                           c_idx)
            grad[nm] = g2.reshape(shp)
        else:
            d2, m2, v2 = _adamw("adamw_" + nm, weights[nm][0], grad[nm][0], ms[nm][0], vs[nm][0])
        delta[nm], new_m[nm], new_v[nm] = d2.reshape(shp), m2.reshape(shp), v2.reshape(shp)

    return (loss, grad_x[None], *[grad[nm] for nm in order], *[delta[nm] for nm in order],
            *[new_m[nm] for nm in order], *[new_v[nm] for nm in order])
```
